```python
import math
import jax, jax.numpy as jnp
from jax import lax
import numpy as np

D_MODEL = 1024
BATCH = 32
SEQ = 2048
DEPTH = 1

CHUNK = 64
LEFT_CHUNKS = 8
BAND_CHUNKS = LEFT_CHUNKS + 1
BAND = BAND_CHUNKS * CHUNK
MAX_REL = 128
N_REL = (CHUNK - 1) + MAX_REL + 1

CONV_WIDTH = D_MODEL // 2
CONV_KERNEL = 31
ATT_HEADS = 8
ATT_HEAD_DIM = 64
ATT_WIDTH = ATT_HEADS * ATT_HEAD_DIM
MEM_LEN = 256
MEM_HEADS = 4
MEM_HEAD_DIM = 128
MEM_WIDTH = MEM_HEADS * MEM_HEAD_DIM
N_BRANCHES = 3
D_FF = 2816
EPS = 1e-6
MASK_VALUE = -1e30

COL_CONV = 2 * CONV_WIDTH
COL_ATT = 3 * ATT_WIDTH
COL_MEM = MEM_WIDTH
COL_GATE = N_BRANCHES * D_MODEL
IN_COLS = COL_CONV + COL_ATT + COL_MEM + COL_GATE

kernel_name = "chunk_causal_hybrid_conformer_block"


def rms_norm(x, g):
    xf = x.astype(jnp.float32)
    y = xf * lax.rsqrt(jnp.mean(xf * xf, axis=-1, keepdims=True) + EPS)
    return (y * g.astype(jnp.float32)).astype(x.dtype)


def layer_norm(x, g, b):
    xf = x.astype(jnp.float32)
    mu = jnp.mean(xf, axis=-1, keepdims=True)
    xc = xf - mu
    y = xc * lax.rsqrt(jnp.mean(xc * xc, axis=-1, keepdims=True) + EPS)
    return (y * g.astype(jnp.float32) + b.astype(jnp.float32)).astype(x.dtype)


def swiglu_ffn(h, w_up, w_down):
    a, b = jnp.split(h @ w_up, 2, axis=-1)
    return (jax.nn.silu(a) * b) @ w_down


def conv_module(u, dw_w, dw_b, ln_g, ln_b, w_pw):
    a, g = jnp.split(u, 2, axis=-1)
    v = a * jax.nn.sigmoid(g)
    v = jnp.pad(v, ((0, 0), (CONV_KERNEL - 1, 0), (0, 0)))
    v = lax.conv_general_dilated(
        v, dw_w[:, None, :].astype(v.dtype), window_strides=(1,), padding="VALID",
        dimension_numbers=("NWC", "WIO", "NWC"), feature_group_count=CONV_WIDTH)
    v = v + dw_b
    v = jax.nn.silu(layer_norm(v, ln_g, ln_b))
    return v @ w_pw


def chunked_attention(q, k, v, rel_bias, w_o):
    B, S, _ = q.shape
    nc = S // CHUNK
    q = q.reshape(B, nc, CHUNK, ATT_HEADS, ATT_HEAD_DIM)
    pad = ((0, 0), (LEFT_CHUNKS * CHUNK, 0), (0, 0))
    kc = jnp.pad(k, pad).reshape(B, nc + LEFT_CHUNKS, CHUNK, ATT_HEADS, ATT_HEAD_DIM)
    vc = jnp.pad(v, pad).reshape(B, nc + LEFT_CHUNKS, CHUNK, ATT_HEADS, ATT_HEAD_DIM)
    band_idx = jnp.arange(nc)[:, None] + jnp.arange(BAND_CHUNKS)[None, :]
    kb = kc[:, band_idx].reshape(B, nc, BAND, ATT_HEADS, ATT_HEAD_DIM)
    vb = vc[:, band_idx].reshape(B, nc, BAND, ATT_HEADS, ATT_HEAD_DIM)
    scores = jnp.einsum("bcqhd,bckhd->bhcqk", q, kb,
                        preferred_element_type=jnp.float32) * (ATT_HEAD_DIM ** -0.5)
    dist = (LEFT_CHUNKS * CHUNK + jnp.arange(CHUNK)[:, None]) - jnp.arange(BAND)[None, :]
    rel_idx = jnp.clip(dist, -(CHUNK - 1), MAX_REL) + (CHUNK - 1)
    bias = rel_bias.astype(jnp.float32)[:, rel_idx]
    key_pos = jnp.arange(nc)[:, None] * CHUNK + jnp.arange(BAND)[None, :] - LEFT_CHUNKS * CHUNK
    valid = (key_pos >= 0)[None, None, :, None, :]
    scores = jnp.where(valid, scores + bias[None, :, None], MASK_VALUE)
    p = jax.nn.softmax(scores, axis=-1).astype(v.dtype)
    o = jnp.einsum("bhcqk,bckhd->bcqhd", p, vb).reshape(B, S, ATT_WIDTH)
    return o @ w_o


def memory_attention(mq, mem_h, w_kv, w_o):
    B, S, _ = mq.shape
    mk, mv = jnp.split(mem_h @ w_kv, 2, axis=-1)
    mq = mq.reshape(B, S, MEM_HEADS, MEM_HEAD_DIM)
    mk = mk.reshape(B, -1, MEM_HEADS, MEM_HEAD_DIM)
    mv = mv.reshape(B, -1, MEM_HEADS, MEM_HEAD_DIM)
    scores = jnp.einsum("bshd,bmhd->bhsm", mq, mk,
                        preferred_element_type=jnp.float32) * (MEM_HEAD_DIM ** -0.5)
    p = jax.nn.softmax(scores, axis=-1).astype(mv.dtype)
    o = jnp.einsum("bhsm,bmhd->bshd", p, mv).reshape(B, S, MEM_WIDTH)
    return o @ w_o


def _fwd_setup_inputs(seed: int = 0) -> dict:
    key = jax.random.key(seed)
    ks = jax.random.split(key, 24)
    L = DEPTH

    def nrm(k, shape, scale):
        return jax.random.normal(k, shape, jnp.float32) * scale

    def gain(k, shape):
        return 1.0 + nrm(k, shape, 0.05)

    return {
        "x": nrm(ks[0], (BATCH, SEQ, D_MODEL), 1.0),
        "mem": nrm(ks[1], (BATCH, MEM_LEN, D_MODEL), 1.0),
        "ffn1_norm": gain(ks[2], (L, D_MODEL)),
        "ffn1_w_up": nrm(ks[3], (L, D_MODEL, 2 * D_FF), D_MODEL ** -0.5),
        "ffn1_w_down": nrm(ks[4], (L, D_FF, D_MODEL), D_FF ** -0.5),
        "mix_norm": gain(ks[5], (L, D_MODEL)),
        "mem_norm": gain(ks[6], (L, D_MODEL)),
        "w_in": nrm(ks[7], (L, D_MODEL, IN_COLS), D_MODEL ** -0.5),
        "b_gate": nrm(ks[8], (L, N_BRANCHES * D_MODEL), 0.1),
        "conv_dw_w": nrm(ks[9], (L, CONV_KERNEL, CONV_WIDTH), CONV_KERNEL ** -0.5),
        "conv_dw_b": nrm(ks[10], (L, CONV_WIDTH), 0.02),
        "conv_ln_g": gain(ks[11], (L, CONV_WIDTH)),
        "conv_ln_b": nrm(ks[12], (L, CONV_WIDTH), 0.02),
        "conv_w_pw": nrm(ks[13], (L, CONV_WIDTH, D_MODEL), CONV_WIDTH ** -0.5),
        "att_rel_bias": nrm(ks[14], (L, ATT_HEADS, N_REL), 0.5),
        "att_w_o": nrm(ks[15], (L, ATT_WIDTH, D_MODEL), ATT_WIDTH ** -0.5),
        "mem_w_kv": nrm(ks[16], (L, D_MODEL, 2 * MEM_WIDTH), D_MODEL ** -0.5),
        "mem_w_o": nrm(ks[17], (L, MEM_WIDTH, D_MODEL), MEM_WIDTH ** -0.5),
        "w_out": nrm(ks[18], (L, D_MODEL, D_MODEL), D_MODEL ** -0.5),
        "ffn2_norm": gain(ks[19], (L, D_MODEL)),
        "ffn2_w_up": nrm(ks[20], (L, D_MODEL, 2 * D_FF), D_MODEL ** -0.5),
        "ffn2_w_down": nrm(ks[21], (L, D_FF, D_MODEL), D_FF ** -0.5),
        "final_norm": gain(ks[22], (D_MODEL,)),
    }


def _fwd_reference(x, mem, ffn1_norm, ffn1_w_up, ffn1_w_down, mix_norm, mem_norm, w_in,
              b_gate, conv_dw_w, conv_dw_b, conv_ln_g, conv_ln_b, conv_w_pw,
              att_rel_bias, att_w_o, mem_w_kv, mem_w_o, w_out, ffn2_norm,
              ffn2_w_up, ffn2_w_down, final_norm):
    B, S, _ = x.shape
    for l in range(DEPTH):
        x = x + 0.5 * swiglu_ffn(rms_norm(x, ffn1_norm[l]), ffn1_w_up[l], ffn1_w_down[l])

        h = rms_norm(x, mix_norm[l])
        proj = h @ w_in[l]
        u_conv, qkv, mq, g_logits = jnp.split(
            proj, [COL_CONV, COL_CONV + COL_ATT, COL_CONV + COL_ATT + COL_MEM], axis=-1)
        q, k, v = jnp.split(qkv, 3, axis=-1)

        y_conv = conv_module(u_conv, conv_dw_w[l], conv_dw_b[l], conv_ln_g[l],
                             conv_ln_b[l], conv_w_pw[l])
        y_att = chunked_attention(q, k, v, att_rel_bias[l], att_w_o[l])
        y_mem = memory_attention(mq, rms_norm(mem, mem_norm[l]), mem_w_kv[l], mem_w_o[l])

        gates = jax.nn.sigmoid(g_logits + b_gate[l]).reshape(B, S, N_BRANCHES, D_MODEL)
        y = gates[:, :, 0] * y_conv + gates[:, :, 1] * y_att + gates[:, :, 2] * y_mem
        x = x + y @ w_out[l]

        x = x + 0.5 * swiglu_ffn(rms_norm(x, ffn2_norm[l]), ffn2_w_up[l], ffn2_w_down[l])
    return rms_norm(x, final_norm)


import jax as _jax
import jax.numpy as _jnp

TWIN_FORMAT = 'train_step'
FWD_PARAMS = ['x', 'mem', 'ffn1_norm', 'ffn1_w_up', 'ffn1_w_down', 'mix_norm', 'mem_norm', 'w_in', 'b_gate', 'conv_dw_w', 'conv_dw_b', 'conv_ln_g', 'conv_ln_b', 'conv_w_pw', 'att_rel_bias', 'att_w_o', 'mem_w_kv', 'mem_w_o', 'w_out', 'ffn2_norm', 'ffn2_w_up', 'ffn2_w_down', 'final_norm']
TWIN_WEIGHTS = ['ffn1_norm', 'ffn1_w_up', 'ffn1_w_down', 'mix_norm', 'mem_norm', 'w_in', 'b_gate', 'conv_dw_w', 'conv_dw_b', 'conv_ln_g', 'conv_ln_b', 'conv_w_pw', 'att_rel_bias', 'att_w_o', 'mem_w_kv', 'mem_w_o', 'w_out', 'ffn2_norm', 'ffn2_w_up', 'ffn2_w_down', 'final_norm']
TWIN_DIFF_INPUT = 'x'
TWIN_INPUTS = ['x', 'mem', 'ffn1_norm', 'ffn1_w_up', 'ffn1_w_down', 'mix_norm', 'mem_norm', 'w_in', 'b_gate', 'conv_dw_w', 'conv_dw_b', 'conv_ln_g', 'conv_ln_b', 'conv_w_pw', 'att_rel_bias', 'att_w_o', 'mem_w_kv', 'mem_w_o', 'w_out', 'ffn2_norm', 'ffn2_w_up', 'ffn2_w_down', 'final_norm', 'loss_target', 'm_ffn1_norm', 'm_ffn1_w_up', 'm_ffn1_w_down', 'm_mix_norm', 'm_mem_norm', 'm_w_in', 'm_b_gate', 'm_conv_dw_w', 'm_conv_dw_b', 'm_conv_ln_g', 'm_conv_ln_b', 'm_conv_w_pw', 'm_att_rel_bias', 'm_att_w_o', 'm_mem_w_kv', 'm_mem_w_o', 'm_w_out', 'm_ffn2_norm', 'm_ffn2_w_up', 'm_ffn2_w_down', 'm_final_norm', 'v_ffn1_norm', 'v_ffn1_w_up', 'v_ffn1_w_down', 'v_mix_norm', 'v_mem_norm', 'v_w_in', 'v_b_gate', 'v_conv_dw_w', 'v_conv_dw_b', 'v_conv_ln_g', 'v_conv_ln_b', 'v_conv_w_pw', 'v_att_rel_bias', 'v_att_w_o', 'v_mem_w_kv', 'v_mem_w_o', 'v_w_out', 'v_ffn2_norm', 'v_ffn2_w_up', 'v_ffn2_w_down', 'v_final_norm']
TWIN_OUTPUTS = ['loss', 'grad_x', 'grad_ffn1_norm', 'grad_ffn1_w_up', 'grad_ffn1_w_down', 'grad_mix_norm', 'grad_mem_norm', 'grad_w_in', 'grad_b_gate', 'grad_conv_dw_w', 'grad_conv_dw_b', 'grad_conv_ln_g', 'grad_conv_ln_b', 'grad_conv_w_pw', 'grad_att_rel_bias', 'grad_att_w_o', 'grad_mem_w_kv', 'grad_mem_w_o', 'grad_w_out', 'grad_ffn2_norm', 'grad_ffn2_w_up', 'grad_ffn2_w_down', 'grad_final_norm', 'delta_ffn1_norm', 'delta_ffn1_w_up', 'delta_ffn1_w_down', 'delta_mix_norm', 'delta_mem_norm', 'delta_w_in', 'delta_b_gate', 'delta_conv_dw_w', 'delta_conv_dw_b', 'delta_conv_ln_g', 'delta_conv_ln_b', 'delta_conv_w_pw', 'delta_att_rel_bias', 'delta_att_w_o', 'delta_mem_w_kv', 'delta_mem_w_o', 'delta_w_out', 'delta_ffn2_norm', 'delta_ffn2_w_up', 'delta_ffn2_w_down', 'delta_final_norm', 'new_m_ffn1_norm', 'new_m_ffn1_w_up', 'new_m_ffn1_w_down', 'new_m_mix_norm', 'new_m_mem_norm', 'new_m_w_in', 'new_m_b_gate', 'new_m_conv_dw_w', 'new_m_conv_dw_b', 'new_m_conv_ln_g', 'new_m_conv_ln_b', 'new_m_conv_w_pw', 'new_m_att_rel_bias', 'new_m_att_w_o', 'new_m_mem_w_kv', 'new_m_mem_w_o', 'new_m_w_out', 'new_m_ffn2_norm', 'new_m_ffn2_w_up', 'new_m_ffn2_w_down', 'new_m_final_norm', 'new_v_ffn1_norm', 'new_v_ffn1_w_up', 'new_v_ffn1_w_down', 'new_v_mix_norm', 'new_v_mem_norm', 'new_v_w_in', 'new_v_b_gate', 'new_v_conv_dw_w', 'new_v_conv_dw_b', 'new_v_conv_ln_g', 'new_v_conv_ln_b', 'new_v_conv_w_pw', 'new_v_att_rel_bias', 'new_v_att_w_o', 'new_v_mem_w_kv', 'new_v_mem_w_o', 'new_v_w_out', 'new_v_ffn2_norm', 'new_v_ffn2_w_up', 'new_v_ffn2_w_down', 'new_v_final_norm']
TWIN_LEAF_KINDS = {'loss': 'loss', 'grad_x': 'grad_x', 'grad_ffn1_norm': 'grad_w', 'grad_ffn1_w_up': 'grad_w', 'grad_ffn1_w_down': 'grad_w', 'grad_mix_norm': 'grad_w', 'grad_mem_norm': 'grad_w', 'grad_w_in': 'grad_w', 'grad_b_gate': 'grad_w', 'grad_conv_dw_w': 'grad_w', 'grad_conv_dw_b': 'grad_w', 'grad_conv_ln_g': 'grad_w', 'grad_conv_ln_b': 'grad_w', 'grad_conv_w_pw': 'grad_w', 'grad_att_rel_bias': 'grad_w', 'grad_att_w_o': 'grad_w', 'grad_mem_w_kv': 'grad_w', 'grad_mem_w_o': 'grad_w', 'grad_w_out': 'grad_w', 'grad_ffn2_norm': 'grad_w', 'grad_ffn2_w_up': 'grad_w', 'grad_ffn2_w_down': 'grad_w', 'grad_final_norm': 'grad_w', 'delta_ffn1_norm': 'delta_w', 'delta_ffn1_w_up': 'delta_w', 'delta_ffn1_w_down': 'delta_w', 'delta_mix_norm': 'delta_w', 'delta_mem_norm': 'delta_w', 'delta_w_in': 'delta_w', 'delta_b_gate': 'delta_w', 'delta_conv_dw_w': 'delta_w', 'delta_conv_dw_b': 'delta_w', 'delta_conv_ln_g': 'delta_w', 'delta_conv_ln_b': 'delta_w', 'delta_conv_w_pw': 'delta_w', 'delta_att_rel_bias': 'delta_w', 'delta_att_w_o': 'delta_w', 'delta_mem_w_kv': 'delta_w', 'delta_mem_w_o': 'delta_w', 'delta_w_out': 'delta_w', 'delta_ffn2_norm': 'delta_w', 'delta_ffn2_w_up': 'delta_w', 'delta_ffn2_w_down': 'delta_w', 'delta_final_norm': 'delta_w', 'new_m_ffn1_norm': 'new_m', 'new_m_ffn1_w_up': 'new_m', 'new_m_ffn1_w_down': 'new_m', 'new_m_mix_norm': 'new_m', 'new_m_mem_norm': 'new_m', 'new_m_w_in': 'new_m', 'new_m_b_gate': 'new_m', 'new_m_conv_dw_w': 'new_m', 'new_m_conv_dw_b': 'new_m', 'new_m_conv_ln_g': 'new_m', 'new_m_conv_ln_b': 'new_m', 'new_m_conv_w_pw': 'new_m', 'new_m_att_rel_bias': 'new_m', 'new_m_att_w_o': 'new_m', 'new_m_mem_w_kv': 'new_m', 'new_m_mem_w_o': 'new_m', 'new_m_w_out': 'new_m', 'new_m_ffn2_norm': 'new_m', 'new_m_ffn2_w_up': 'new_m', 'new_m_ffn2_w_down': 'new_m', 'new_m_final_norm': 'new_m', 'new_v_ffn1_norm': 'new_v', 'new_v_ffn1_w_up': 'new_v', 'new_v_ffn1_w_down': 'new_v', 'new_v_mix_norm': 'new_v', 'new_v_mem_norm': 'new_v', 'new_v_w_in': 'new_v', 'new_v_b_gate': 'new_v', 'new_v_conv_dw_w': 'new_v', 'new_v_conv_dw_b': 'new_v', 'new_v_conv_ln_g': 'new_v', 'new_v_conv_ln_b': 'new_v', 'new_v_conv_w_pw': 'new_v', 'new_v_att_rel_bias': 'new_v', 'new_v_att_w_o': 'new_v', 'new_v_mem_w_kv': 'new_v', 'new_v_mem_w_o': 'new_v', 'new_v_w_out': 'new_v', 'new_v_ffn2_norm': 'new_v', 'new_v_ffn2_w_up': 'new_v', 'new_v_ffn2_w_down': 'new_v', 'new_v_final_norm': 'new_v'}


def _forward(args):
    return _fwd_reference(*[args[k] for k in FWD_PARAMS])


def _output_shape():
    out = _jax.eval_shape(lambda: _forward(_fwd_setup_inputs(0)))
    return out.shape, out.dtype

N_MICROBATCH = 1
ADAM_LR = 0.001
ADAM_B1 = 0.9
ADAM_B2 = 0.999
ADAM_EPS = 1e-08
ADAM_WD = 0.01
ADAM_STEP = 10
PER_EXAMPLE_BATCH_AXIS = {'x': 0, 'mem': 0, 'loss_target': 0}
SHARED_INPUTS = []
_WEIGHT_DTYPES = {'ffn1_norm': _jnp.float32, 'ffn1_w_up': _jnp.float32, 'ffn1_w_down': _jnp.float32, 'mix_norm': _jnp.float32, 'mem_norm': _jnp.float32, 'w_in': _jnp.float32, 'b_gate': _jnp.float32, 'conv_dw_w': _jnp.float32, 'conv_dw_b': _jnp.float32, 'conv_ln_g': _jnp.float32, 'conv_ln_b': _jnp.float32, 'conv_w_pw': _jnp.float32, 'att_rel_bias': _jnp.float32, 'att_w_o': _jnp.float32, 'mem_w_kv': _jnp.float32, 'mem_w_o': _jnp.float32, 'w_out': _jnp.float32, 'ffn2_norm': _jnp.float32, 'ffn2_w_up': _jnp.float32, 'ffn2_w_down': _jnp.float32, 'final_norm': _jnp.float32}
MOMENT_SCALE = {'ffn1_norm': 1.163672e-01, 'ffn1_w_up': 4.682121e-02, 'ffn1_w_down': 7.648765e-02, 'mix_norm': 9.841342e-02, 'mem_norm': 2.129294e-02, 'w_in': 4.059057e-02, 'b_gate': 1.966947e-02, 'conv_dw_w': 1.134195e-01, 'conv_dw_b': 2.514326e-01, 'conv_ln_g': 1.548500e-01, 'conv_ln_b': 1.576989e-01, 'conv_w_pw': 8.297643e-02, 'att_rel_bias': 1.568715e-02, 'att_w_o': 2.096649e-02, 'mem_w_kv': 1.953768e-02, 'mem_w_o': 1.419506e-02, 'w_out': 8.311627e-02, 'ffn2_norm': 9.540977e-02, 'ffn2_w_up': 4.064230e-02, 'ffn2_w_down': 6.665165e-02, 'final_norm': 6.403358e+01}


def _to_microbatches(a, axis):
    t = _jnp.moveaxis(a, axis, 0)
    t = t.reshape((N_MICROBATCH, t.shape[0] // N_MICROBATCH) + t.shape[1:])
    return _jnp.moveaxis(t, 1, axis + 1)


def setup_inputs(seed: int = 0) -> dict:
    inp = _fwd_setup_inputs(seed)
    key = _jax.random.fold_in(_jax.random.key(seed), 7919)
    shape, _ = _output_shape()
    out = dict(inp)
    out["loss_target"] = _jax.random.normal(_jax.random.fold_in(key, 0), shape, _jnp.float32)
    for i, name in enumerate(TWIN_WEIGHTS):
        w = inp[name].astype(_jnp.float32)
        if MOMENT_SCALE is None:
            s = _jnp.sqrt(_jnp.mean(_jnp.square(w)) + 1e-30)
        else:
            s = MOMENT_SCALE[name]
        km, kv = _jax.random.split(_jax.random.fold_in(key, i + 1))
        out[name] = w
        out["m_" + name] = s * _jax.random.normal(km, w.shape, _jnp.float32)
        out["v_" + name] = (s * s) * _jax.random.uniform(kv, w.shape, _jnp.float32, 0.5, 1.5)
    if N_MICROBATCH > 1:
        for name, axis in PER_EXAMPLE_BATCH_AXIS.items():
            out[name] = _to_microbatches(out[name], axis)
    return {'x': out['x'], 'mem': out['mem'], 'ffn1_norm': out['ffn1_norm'], 'ffn1_w_up': out['ffn1_w_up'], 'ffn1_w_down': out['ffn1_w_down'], 'mix_norm': out['mix_norm'], 'mem_norm': out['mem_norm'], 'w_in': out['w_in'], 'b_gate': out['b_gate'], 'conv_dw_w': out['conv_dw_w'], 'conv_dw_b': out['conv_dw_b'], 'conv_ln_g': out['conv_ln_g'], 'conv_ln_b': out['conv_ln_b'], 'conv_w_pw': out['conv_w_pw'], 'att_rel_bias': out['att_rel_bias'], 'att_w_o': out['att_w_o'], 'mem_w_kv': out['mem_w_kv'], 'mem_w_o': out['mem_w_o'], 'w_out': out['w_out'], 'ffn2_norm': out['ffn2_norm'], 'ffn2_w_up': out['ffn2_w_up'], 'ffn2_w_down': out['ffn2_w_down'], 'final_norm': out['final_norm'], 'loss_target': out['loss_target'], 'm_ffn1_norm': out['m_ffn1_norm'], 'm_ffn1_w_up': out['m_ffn1_w_up'], 'm_ffn1_w_down': out['m_ffn1_w_down'], 'm_mix_norm': out['m_mix_norm'], 'm_mem_norm': out['m_mem_norm'], 'm_w_in': out['m_w_in'], 'm_b_gate': out['m_b_gate'], 'm_conv_dw_w': out['m_conv_dw_w'], 'm_conv_dw_b': out['m_conv_dw_b'], 'm_conv_ln_g': out['m_conv_ln_g'], 'm_conv_ln_b': out['m_conv_ln_b'], 'm_conv_w_pw': out['m_conv_w_pw'], 'm_att_rel_bias': out['m_att_rel_bias'], 'm_att_w_o': out['m_att_w_o'], 'm_mem_w_kv': out['m_mem_w_kv'], 'm_mem_w_o': out['m_mem_w_o'], 'm_w_out': out['m_w_out'], 'm_ffn2_norm': out['m_ffn2_norm'], 'm_ffn2_w_up': out['m_ffn2_w_up'], 'm_ffn2_w_down': out['m_ffn2_w_down'], 'm_final_norm': out['m_final_norm'], 'v_ffn1_norm': out['v_ffn1_norm'], 'v_ffn1_w_up': out['v_ffn1_w_up'], 'v_ffn1_w_down': out['v_ffn1_w_down'], 'v_mix_norm': out['v_mix_norm'], 'v_mem_norm': out['v_mem_norm'], 'v_w_in': out['v_w_in'], 'v_b_gate': out['v_b_gate'], 'v_conv_dw_w': out['v_conv_dw_w'], 'v_conv_dw_b': out['v_conv_dw_b'], 'v_conv_ln_g': out['v_conv_ln_g'], 'v_conv_ln_b': out['v_conv_ln_b'], 'v_conv_w_pw': out['v_conv_w_pw'], 'v_att_rel_bias': out['v_att_rel_bias'], 'v_att_w_o': out['v_att_w_o'], 'v_mem_w_kv': out['v_mem_w_kv'], 'v_mem_w_o': out['v_mem_w_o'], 'v_w_out': out['v_w_out'], 'v_ffn2_norm': out['v_ffn2_norm'], 'v_ffn2_w_up': out['v_ffn2_w_up'], 'v_ffn2_w_down': out['v_ffn2_w_down'], 'v_final_norm': out['v_final_norm']}


def _loss(weights, diff, rest, loss_target):
    with _jax.named_scope("forward"):
        args = {**rest, TWIN_DIFF_INPUT: diff, **{k: w.astype(_WEIGHT_DTYPES[k]) for k, w in weights.items()}}
        y = _forward(args)
    with _jax.named_scope("loss_head"):
        err = _jnp.square(y.astype(_jnp.float32) - loss_target)
        return 0.5 * _jnp.sum(_jnp.mean(err, axis=-1)) if err.ndim else 0.5 * err


def _adamw(w, g, m, v):
    m = ADAM_B1 * m + (1.0 - ADAM_B1) * g
    v = ADAM_B2 * v + (1.0 - ADAM_B2) * _jnp.square(g)
    m_hat = m / (1.0 - ADAM_B1 ** ADAM_STEP)
    v_hat = v / (1.0 - ADAM_B2 ** ADAM_STEP)
    delta = -ADAM_LR * (m_hat / (_jnp.sqrt(v_hat) + ADAM_EPS) + ADAM_WD * w)
    return delta, m, v


def reference(x, mem, ffn1_norm, ffn1_w_up, ffn1_w_down, mix_norm, mem_norm, w_in, b_gate, conv_dw_w, conv_dw_b, conv_ln_g, conv_ln_b, conv_w_pw, att_rel_bias, att_w_o, mem_w_kv, mem_w_o, w_out, ffn2_norm, ffn2_w_up, ffn2_w_down, final_norm, loss_target, m_ffn1_norm, m_ffn1_w_up, m_ffn1_w_down, m_mix_norm, m_mem_norm, m_w_in, m_b_gate, m_conv_dw_w, m_conv_dw_b, m_conv_ln_g, m_conv_ln_b, m_conv_w_pw, m_att_rel_bias, m_att_w_o, m_mem_w_kv, m_mem_w_o, m_w_out, m_ffn2_norm, m_ffn2_w_up, m_ffn2_w_down, m_final_norm, v_ffn1_norm, v_ffn1_w_up, v_ffn1_w_down, v_mix_norm, v_mem_norm, v_w_in, v_b_gate, v_conv_dw_w, v_conv_dw_b, v_conv_ln_g, v_conv_ln_b, v_conv_w_pw, v_att_rel_bias, v_att_w_o, v_mem_w_kv, v_mem_w_o, v_w_out, v_ffn2_norm, v_ffn2_w_up, v_ffn2_w_down, v_final_norm):
    given = dict(x=x, mem=mem, ffn1_norm=ffn1_norm, ffn1_w_up=ffn1_w_up, ffn1_w_down=ffn1_w_down, mix_norm=mix_norm, mem_norm=mem_norm, w_in=w_in, b_gate=b_gate, conv_dw_w=conv_dw_w, conv_dw_b=conv_dw_b, conv_ln_g=conv_ln_g, conv_ln_b=conv_ln_b, conv_w_pw=conv_w_pw, att_rel_bias=att_rel_bias, att_w_o=att_w_o, mem_w_kv=mem_w_kv, mem_w_o=mem_w_o, w_out=w_out, ffn2_norm=ffn2_norm, ffn2_w_up=ffn2_w_up, ffn2_w_down=ffn2_w_down, final_norm=final_norm, loss_target=loss_target, m_ffn1_norm=m_ffn1_norm, m_ffn1_w_up=m_ffn1_w_up, m_ffn1_w_down=m_ffn1_w_down, m_mix_norm=m_mix_norm, m_mem_norm=m_mem_norm, m_w_in=m_w_in, m_b_gate=m_b_gate, m_conv_dw_w=m_conv_dw_w, m_conv_dw_b=m_conv_dw_b, m_conv_ln_g=m_conv_ln_g, m_conv_ln_b=m_conv_ln_b, m_conv_w_pw=m_conv_w_pw, m_att_rel_bias=m_att_rel_bias, m_att_w_o=m_att_w_o, m_mem_w_kv=m_mem_w_kv, m_mem_w_o=m_mem_w_o, m_w_out=m_w_out, m_ffn2_norm=m_ffn2_norm, m_ffn2_w_up=m_ffn2_w_up, m_ffn2_w_down=m_ffn2_w_down, m_final_norm=m_final_norm, v_ffn1_norm=v_ffn1_norm, v_ffn1_w_up=v_ffn1_w_up, v_ffn1_w_down=v_ffn1_w_down, v_mix_norm=v_mix_norm, v_mem_norm=v_mem_norm, v_w_in=v_w_in, v_b_gate=v_b_gate, v_conv_dw_w=v_conv_dw_w, v_conv_dw_b=v_conv_dw_b, v_conv_ln_g=v_conv_ln_g, v_conv_ln_b=v_conv_ln_b, v_conv_w_pw=v_conv_w_pw, v_att_rel_bias=v_att_rel_bias, v_att_w_o=v_att_w_o, v_mem_w_kv=v_mem_w_kv, v_mem_w_o=v_mem_w_o, v_w_out=v_w_out, v_ffn2_norm=v_ffn2_norm, v_ffn2_w_up=v_ffn2_w_up, v_ffn2_w_down=v_ffn2_w_down, v_final_norm=v_final_norm)
    weights = {n: given[n] for n in TWIN_WEIGHTS}
    shared = {n: given[n] for n in SHARED_INPUTS}
    per_example = {n: given[n] for n in ['x', 'mem']}
    grad_fn = _jax.value_and_grad(_loss, argnums=(0, 1))

    def one_microbatch(ex, loss_target):
        ex = dict(ex)
        diff = ex.pop(TWIN_DIFF_INPUT)
        return grad_fn(weights, diff, {**shared, **ex}, loss_target)

    if N_MICROBATCH == 1:
        loss, (grad_w, grad_x) = one_microbatch(per_example, given["loss_target"])
    else:
        def body(carry, xs):
            loss_sum, grad_sum = carry
            l_k, (gw_k, gx_k) = one_microbatch(xs[0], xs[1])
            with _jax.named_scope("update"):
                return (loss_sum + l_k, _jax.tree.map(_jnp.add, grad_sum, gw_k)), gx_k

        init = (_jnp.zeros((), _jnp.float32), _jax.tree.map(_jnp.zeros_like, weights))
        (loss, grad_w), grad_x = _jax.lax.scan(body, init, (per_example, given["loss_target"]))
    with _jax.named_scope("update"):
        delta_w, new_m, new_v = {}, {}, {}
        for n in TWIN_WEIGHTS:
            delta_w[n], new_m[n], new_v[n] = _adamw(weights[n], grad_w[n], given["m_" + n], given["v_" + n])
    return (loss, grad_x, *[grad_w[n] for n in TWIN_WEIGHTS], *[delta_w[n] for n in TWIN_WEIGHTS],
            *[new_m[n] for n in TWIN_WEIGHTS], *[new_v[n] for n in TWIN_WEIGHTS])
```

```python
import functools

import jax
import jax.numpy as jnp
from jax import lax
from jax.experimental import pallas as pl
from jax.experimental.pallas import tpu as pltpu

F32 = jnp.float32
BF16 = jnp.bfloat16

EPS = 1e-6
MASK_VALUE = -1e30
D = 1024
NDEV = 8
FF_SHARD = 704
FF_SHARD_PAD = 768
FF_HALF_ROWS = 352
FF_PAD = 4 * FF_SHARD_PAD
IN_COLS = 6144
CONV_W = 512
CONV_K = 31
CONV_HALO = 32
CONV_CHUNK = 128
ATT_W = 512
ATT_HEADS = 8
ATT_HD = 64
CHUNK = 64
LEFT_CHUNKS = 8
MAX_REL = 128
N_REL = 192
QB = 256
KWIN = QB + LEFT_CHUNKS * CHUNK
KPAD = LEFT_CHUNKS * CHUNK
DS_LANES = 1024
MEM_W = 512
MEM_HEADS = 4
MEM_HD = 128
ADAM_LR = 0.001
ADAM_B1 = 0.9
ADAM_B2 = 0.999
ADAM_EPS = 1e-08
ADAM_WD = 0.01
ADAM_STEP = 10
VMEM_LIMIT = 60 * 1024 * 1024

MESH = pl.DeviceIdType.MESH
ANY = pl.BlockSpec(memory_space=pl.ANY)

WEIGHTS = ['ffn1_norm', 'ffn1_w_up', 'ffn1_w_down', 'mix_norm', 'mem_norm', 'w_in', 'b_gate', 'conv_dw_w', 'conv_dw_b',
           'conv_ln_g', 'conv_ln_b', 'conv_w_pw', 'att_rel_bias', 'att_w_o', 'mem_w_kv', 'mem_w_o', 'w_out', 'ffn2_norm',
           'ffn2_w_up', 'ffn2_w_down', 'final_norm']
BIG = {
    'ffn1_w_up': ('col', FF_SHARD_PAD), 'ffn1_w_down': ('wd', FF_HALF_ROWS), 'w_in': ('col', 768),
    'conv_w_pw': ('col', 128), 'att_w_o': ('col', 128), 'mem_w_kv': ('row', 128), 'mem_w_o': ('col', 128),
    'w_out': ('row', 128), 'ffn2_w_up': ('col', FF_SHARD_PAD), 'ffn2_w_down': ('wd', FF_HALF_ROWS),
}
BIG_ORDER = ['ffn1_w_up', 'ffn1_w_down', 'w_in', 'conv_w_pw', 'att_w_o', 'mem_w_kv', 'mem_w_o', 'w_out', 'ffn2_w_up', 'ffn2_w_down']
SMALL = [('loss', 128), ('ffn1_norm', 1024), ('mix_norm', 1024), ('mem_norm', 1024), ('b_gate', 3072),
         ('conv_dw_w', CONV_K * CONV_W), ('conv_dw_b', 512), ('conv_ln_g', 512), ('conv_ln_b', 512),
         ('att_rel_bias', ATT_HEADS * N_REL), ('ffn2_norm', 1024), ('final_norm', 1024)]
SMALL_ROWS = 216


def _dot(a, b):
    return jnp.dot(a, b, preferred_element_type=F32)


def _dot_nt(a, b):
    return lax.dot_general(a, b, (((1,), (1,)), ((), ())), preferred_element_type=F32)


def _dot_tn(a, b):
    return lax.dot_general(a, b, (((0,), (0,)), ((), ())), preferred_element_type=F32)


def _sigmoid(v):
    return jax.nn.sigmoid(v)


def _const(shape):
    return pl.BlockSpec(shape, lambda *_: (0,) * len(shape), pipeline_mode=pl.Buffered(1))


def _params(*sem):
    return pltpu.CompilerParams(dimension_semantics=sem if sem else None, vmem_limit_bytes=VMEM_LIMIT)


def _my_coords():
    return lax.axis_index("x"), lax.axis_index("y"), lax.axis_index("c")


def _dev_index(px, py, pc):
    return 4 * px + 2 * py + pc


def _window(ref, kind, n, p):
    if kind == 'row':
        return ref.at[pl.ds(pl.multiple_of(p * n, n), n), :]
    if kind == 'col':
        return ref.at[:, pl.ds(pl.multiple_of(p * n, 128), n)]
    start = (p // 2) * FF_SHARD_PAD + (p % 2) * FF_HALF_ROWS
    return ref.at[pl.ds(pl.multiple_of(start, 32), n), :]


def _full_shape(kind, n, shard_shape):
    if kind == 'row':
        return (NDEV * n, shard_shape[1])
    if kind == 'col':
        return (shard_shape[0], NDEV * n)
    return (FF_PAD, shard_shape[1])


def _cast_shards(shards):
    n = len(shards)

    def body(*refs):
        for i in range(n):
            src, dst = refs[i], refs[n + i]
            if src.shape[1] == FF_SHARD:
                dst[:, pl.ds(0, FF_SHARD)] = src[...].astype(BF16)
                dst[:, pl.ds(FF_SHARD, FF_SHARD_PAD - FF_SHARD)] = jnp.zeros((src.shape[0], FF_SHARD_PAD - FF_SHARD), BF16)
            else:
                dst[...] = src[...].astype(BF16)

    out_shape = [jax.ShapeDtypeStruct((s.shape[0], FF_SHARD_PAD if s.shape[1] == FF_SHARD else s.shape[1]), BF16) for s in shards]
    return pl.pallas_call(body, out_shape=out_shape, name="cast_shards", compiler_params=_params())(*shards)


def _all_gather(shards, kinds, zeros_pad):
    n = len(shards)
    wd_ids = [i for i, (k, _) in enumerate(kinds) if k == 'wd']

    def body(*refs):
        src = refs[:n]
        zero_ref = refs[n]
        out = refs[n + 1:2 * n + 1]
        send_sems, recv_sems, local_sems, zero_sems = refs[2 * n + 1:]
        x, y, c = _my_coords()
        me, sibling = (x, y, c), (x, y, 1 - c)
        chips = [(1 - x, y), (x, 1 - y), (1 - x, 1 - y)]

        def win(i, dev):
            return _window(out[i], kinds[i][0], kinds[i][1], _dev_index(*dev))

        def copy(i, k, block, to, from_shard=False):
            return pltpu.make_async_remote_copy(
                src_ref=src[i] if from_shard else win(i, block), dst_ref=win(i, block),
                send_sem=send_sems.at[i, k], recv_sem=recv_sems.at[i, k], device_id=to, device_id_type=MESH)

        local = [pltpu.make_async_copy(src[i], win(i, me), local_sems.at[i]) for i in range(n)]
        for j, i in enumerate(wd_ids):
            for q in range(4):
                pad_rows = out[i].at[pl.ds(q * FF_SHARD_PAD + 2 * FF_HALF_ROWS, FF_SHARD_PAD - 2 * FF_HALF_ROWS), :]
                local.append(pltpu.make_async_copy(zero_ref, pad_rows, zero_sems.at[j, q]))
        for cp in local:
            cp.start()
        first = []
        for i in range(n):
            first.append(copy(i, 0, me, sibling, from_shard=True))
            first += [copy(i, 1 + j, me, (*chip, c), from_shard=True) for j, chip in enumerate(chips)]
        for cp in first:
            cp.start()
        passed = []
        for j, chip in enumerate(chips):
            for i in range(n):
                copy(i, 1 + j, (*chip, c), me).wait_recv()
                fwd = copy(i, 4 + j, (*chip, c), sibling)
                fwd.start()
                passed.append(fwd)
        for i in range(n):
            copy(i, 0, sibling, me).wait_recv()
            for j, chip in enumerate(chips):
                copy(i, 4 + j, (*chip, 1 - c), me).wait_recv()
        for cp in first + passed:
            cp.wait_send()
        for cp in local:
            cp.wait()

    out_shape = [jax.ShapeDtypeStruct(_full_shape(k, m, s.shape), s.dtype) for s, (k, m) in zip(shards, kinds)]
    return pl.pallas_call(
        body, out_shape=out_shape, in_specs=[ANY] * (n + 1), out_specs=[ANY] * n, name="all_gather_weights",
        scratch_shapes=[pltpu.SemaphoreType.DMA((n, 7)), pltpu.SemaphoreType.DMA((n, 7)), pltpu.SemaphoreType.DMA((n,)),
                        pltpu.SemaphoreType.DMA((max(len(wd_ids), 1), 4))],
        compiler_params=pltpu.CompilerParams(has_side_effects=True),
    )(*shards, zeros_pad)


def _rms_stats(xf):
    r = lax.rsqrt(jnp.mean(xf * xf, axis=-1, keepdims=True) + EPS)
    return xf * r, r


def _rms_bwd(dh, g, xhat, r):
    dxhat = dh * g
    return r * (dxhat - xhat * jnp.mean(dxhat * xhat, axis=-1, keepdims=True))


def _ffn_fwd(x, g, wu, wd, tm, name):
    T = x.shape[0]

    def body(x_ref, g_ref, wu_ref, wd_ref, xo_ref, ab_ref):
        xf = x_ref[...]
        xhat, _ = _rms_stats(xf)
        h = (xhat * g_ref[...]).astype(BF16)
        acc = jnp.zeros((tm, D), F32)
        for j in range(4):
            ca = slice(j * FF_SHARD_PAD, (j + 1) * FF_SHARD_PAD)
            cb = slice(FF_PAD + j * FF_SHARD_PAD, FF_PAD + (j + 1) * FF_SHARD_PAD)
            a = _dot(h, wu_ref[:, ca])
            b = _dot(h, wu_ref[:, cb])
            ab_ref[:, ca] = a.astype(BF16)
            ab_ref[:, cb] = b.astype(BF16)
            act = (a * _sigmoid(a) * b).astype(BF16)
            acc = acc + _dot(act, wd_ref[ca, :])
        xo_ref[...] = xf + 0.5 * acc

    return pl.pallas_call(
        body, grid=(T // tm,), name=name,
        in_specs=[pl.BlockSpec((tm, D), lambda t: (t, 0)), _const((1, D)), _const((D, 2 * FF_PAD)), _const((FF_PAD, D))],
        out_specs=[pl.BlockSpec((tm, D), lambda t: (t, 0)), pl.BlockSpec((tm, 2 * FF_PAD), lambda t: (t, 0))],
        out_shape=[jax.ShapeDtypeStruct((T, D), F32), jax.ShapeDtypeStruct((T, 2 * FF_PAD), BF16)],
        compiler_params=_params("arbitrary"),
    )(x, g, wu, wd)


def _ffn_bwd(x, dy, ab, g, wu, wd, tm, name):
    T = x.shape[0]

    def body(x_ref, dy_ref, ab_ref, g_ref, wu_ref, wd_ref, dx_ref, dab_ref, act_ref, h_ref, dg_ref):
        xf = x_ref[...]
        xhat, r = _rms_stats(xf)
        gain = g_ref[...]
        h_ref[...] = (xhat * gain).astype(BF16)
        dy = dy_ref[...]
        dyh = (0.5 * dy).astype(BF16)
        dh = jnp.zeros((tm, D), F32)
        for j in range(4):
            ca = slice(j * FF_SHARD_PAD, (j + 1) * FF_SHARD_PAD)
            cb = slice(FF_PAD + j * FF_SHARD_PAD, FF_PAD + (j + 1) * FF_SHARD_PAD)
            a = ab_ref[:, ca].astype(F32)
            b = ab_ref[:, cb].astype(F32)
            dact = _dot_nt(dyh, wd_ref[ca, :])
            sg = _sigmoid(a)
            sl = a * sg
            act_ref[:, ca] = (sl * b).astype(BF16)
            da = (dact * b * (sg * (1.0 + a * (1.0 - sg)))).astype(BF16)
            db = (dact * sl).astype(BF16)
            dab_ref[:, ca] = da
            dab_ref[:, cb] = db
            dh = dh + _dot_nt(da, wu_ref[:, ca]) + _dot_nt(db, wu_ref[:, cb])
        dx_ref[...] = dy + _rms_bwd(dh, gain, xhat, r)

        @pl.when(pl.program_id(0) == 0)
        def _():
            dg_ref[...] = jnp.zeros_like(dg_ref)
        dg_ref[...] += jnp.sum(dh * xhat, axis=0, keepdims=True)

    return pl.pallas_call(
        body, grid=(T // tm,), name=name,
        in_specs=[pl.BlockSpec((tm, D), lambda t: (t, 0)), pl.BlockSpec((tm, D), lambda t: (t, 0)),
                  pl.BlockSpec((tm, 2 * FF_PAD), lambda t: (t, 0)), _const((1, D)), _const((D, 2 * FF_PAD)), _const((FF_PAD, D))],
        out_specs=[pl.BlockSpec((tm, D), lambda t: (t, 0)), pl.BlockSpec((tm, 2 * FF_PAD), lambda t: (t, 0)),
                   pl.BlockSpec((tm, FF_PAD), lambda t: (t, 0)), pl.BlockSpec((tm, D), lambda t: (t, 0)),
                   pl.BlockSpec((1, D), lambda t: (0, 0))],
        out_shape=[jax.ShapeDtypeStruct((T, D), F32), jax.ShapeDtypeStruct((T, 2 * FF_PAD), BF16),
                   jax.ShapeDtypeStruct((T, FF_PAD), BF16), jax.ShapeDtypeStruct((T, D), BF16), jax.ShapeDtypeStruct((1, D), F32)],
        compiler_params=_params("arbitrary"),
    )(x, dy, ab, g, wu, wd)


def _tn_matmul(xm, ym, tn, name, scale=None, out_cols=None, col_off=0, prev=None, tt=512):
    T, K = xm.shape
    N = ym.shape[1]
    out_cols = N if out_cols is None else out_cols
    tt = min(tt, T)
    nt = T // tt
    off = col_off // tn

    def body(*refs):
        x_ref, y_ref = refs[0], refs[1]
        o_ref, acc = refs[-2], refs[-1]

        @pl.when(pl.program_id(1) == 0)
        def _():
            acc[...] = jnp.zeros_like(acc)
        acc[...] += _dot_tn(x_ref[...].astype(BF16), y_ref[...].astype(BF16))

        @pl.when(pl.program_id(1) == nt - 1)
        def _():
            res = acc[...]
            o_ref[...] = (res if scale is None else res * scale).astype(BF16)

    in_specs = [pl.BlockSpec((tt, K), lambda n, t: (t, 0)), pl.BlockSpec((tt, tn), lambda n, t: (t, n))]
    args = [xm, ym]
    aliases = {}
    if prev is not None:
        in_specs.append(ANY)
        args.append(prev)
        aliases = {2: 0}
    return pl.pallas_call(
        body, grid=(N // tn, nt), name=name, in_specs=in_specs,
        out_specs=pl.BlockSpec((K, tn), lambda n, t: (0, n + off)),
        out_shape=jax.ShapeDtypeStruct((K, out_cols), BF16), scratch_shapes=[pltpu.VMEM((K, tn), F32)],
        input_output_aliases=aliases, compiler_params=_params("parallel", "arbitrary"),
    )(*args)


def _mix_fwd(x, g, w_in, tm):
    T = x.shape[0]

    def body(x_ref, g_ref, w_ref, uc_ref, qkv_ref, mq_ref, gl_ref, h_ref):
        xhat, _ = _rms_stats(x_ref[...])
        h = (xhat * g_ref[...]).astype(BF16)
        h_ref[...] = h
        uc_ref[...] = _dot(h, w_ref[:, 0:1024])
        qkv_ref[...] = _dot(h, w_ref[:, 1024:2560]).astype(BF16)
        mq_ref[...] = _dot(h, w_ref[:, 2560:3072]).astype(BF16)
        for j in range(3):
            gl_ref[:, j * D:(j + 1) * D] = _dot(h, w_ref[:, 3072 + j * D:3072 + (j + 1) * D])

    row = lambda w: pl.BlockSpec((tm, w), lambda t: (t, 0))
    return pl.pallas_call(
        body, grid=(T // tm,), name="mix_fwd",
        in_specs=[row(D), _const((1, D)), _const((D, IN_COLS))],
        out_specs=[row(1024), row(1536), row(512), row(3072), row(D)],
        out_shape=[jax.ShapeDtypeStruct((T, 1024), F32), jax.ShapeDtypeStruct((T, 1536), BF16), jax.ShapeDtypeStruct((T, 512), BF16),
                   jax.ShapeDtypeStruct((T, 3072), F32), jax.ShapeDtypeStruct((T, D), BF16)],
        compiler_params=_params("parallel"),
    )(x, g, w_in)


def _mix_bwd(x, dres, duc, dqkv, dmq, dgl, g, w_in, tm):
    T = x.shape[0]

    def body(x_ref, dres_ref, duc_ref, dqkv_ref, dmq_ref, dgl_ref, g_ref, w_ref, dx_ref, dg_ref):
        xhat, r = _rms_stats(x_ref[...])
        dh = _dot_nt(duc_ref[...], w_ref[:, 0:1024])
        dh = dh + _dot_nt(dqkv_ref[...], w_ref[:, 1024:2560])
        dh = dh + _dot_nt(dmq_ref[...], w_ref[:, 2560:3072])
        dh = dh + _dot_nt(dgl_ref[...], w_ref[:, 3072:6144])
        dx_ref[...] = dres_ref[...] + _rms_bwd(dh, g_ref[...], xhat, r)

        @pl.when(pl.program_id(0) == 0)
        def _():
            dg_ref[...] = jnp.zeros_like(dg_ref)
        dg_ref[...] += jnp.sum(dh * xhat, axis=0, keepdims=True)

    row = lambda w: pl.BlockSpec((tm, w), lambda t: (t, 0))
    return pl.pallas_call(
        body, grid=(T // tm,), name="mix_bwd",
        in_specs=[row(D), row(D), row(1024), row(1536), row(512), row(3072), _const((1, D)), _const((D, IN_COLS))],
        out_specs=[row(D), pl.BlockSpec((1, D), lambda t: (0, 0))],
        out_shape=[jax.ShapeDtypeStruct((T, D), F32), jax.ShapeDtypeStruct((1, D), F32)],
        compiler_params=_params("arbitrary"),
    )(x, dres, duc, dqkv, dmq, dgl, g, w_in)


def _conv_taps(win, w_ref, first):
    acc = jnp.zeros((CONV_CHUNK, CONV_W), F32)
    for j in range(CONV_K):
        s = first(j)
        acc = acc + win[s:s + CONV_CHUNK] * w_ref[j:j + 1, :]
    return acc


def _glu_into(uc_ref, vpad, S):
    vpad[pl.ds(0, CONV_HALO), :] = jnp.zeros((CONV_HALO, CONV_W), F32)

    def glu(i, carry):
        r0 = pl.multiple_of(i * CONV_CHUNK, CONV_CHUNK)
        a = uc_ref[0, pl.ds(r0, CONV_CHUNK), 0:CONV_W]
        gt = uc_ref[0, pl.ds(r0, CONV_CHUNK), CONV_W:2 * CONV_W]
        vpad[pl.ds(pl.multiple_of(r0 + CONV_HALO, CONV_HALO), CONV_CHUNK), :] = a * _sigmoid(gt)
        return carry
    lax.fori_loop(0, S // CONV_CHUNK, glu, 0)


def _conv_ln(vpad, w_ref, vec_ref, r0):
    win = vpad[pl.ds(r0, CONV_CHUNK + CONV_HALO), :]
    z = _conv_taps(win, w_ref, lambda j: j + CONV_HALO - (CONV_K - 1)) + vec_ref[0:1, :]
    xc = z - jnp.mean(z, axis=-1, keepdims=True)
    rstd = lax.rsqrt(jnp.mean(xc * xc, axis=-1, keepdims=True) + EPS)
    xn = xc * rstd
    return xn, rstd, xn * vec_ref[1:2, :] + vec_ref[2:3, :]


def _conv_fwd(uc, dw_w, vec):
    NB, S, _ = uc.shape

    def body(uc_ref, w_ref, vec_ref, o_ref, vpad):
        _glu_into(uc_ref, vpad, S)

        def conv(i, carry):
            r0 = pl.multiple_of(i * CONV_CHUNK, CONV_CHUNK)
            _, _, yln = _conv_ln(vpad, w_ref, vec_ref, r0)
            o_ref[0, pl.ds(r0, CONV_CHUNK), :] = (yln * _sigmoid(yln)).astype(BF16)
            return carry
        lax.fori_loop(0, S // CONV_CHUNK, conv, 0)

    return pl.pallas_call(
        body, grid=(NB,), name="conv_fwd",
        in_specs=[pl.BlockSpec((1, S, 2 * CONV_W), lambda b: (b, 0, 0)), _const((CONV_K, CONV_W)), _const((8, CONV_W))],
        out_specs=pl.BlockSpec((1, S, CONV_W), lambda b: (b, 0, 0)),
        out_shape=jax.ShapeDtypeStruct((NB, S, CONV_W), BF16),
        scratch_shapes=[pltpu.VMEM((S + CONV_HALO, CONV_W), F32)],
        compiler_params=_params("parallel"),
    )(uc, dw_w, vec)


def _conv_bwd(uc, dcact, dw_w, vec):
    NB, S, _ = uc.shape
    n_chunks = S // CONV_CHUNK

    def body(uc_ref, dc_ref, w_ref, vec_ref, duc_ref, dw_ref, dvec_ref, vpad, dzpad):
        @pl.when(pl.program_id(0) == 0)
        def _():
            dw_ref[...] = jnp.zeros_like(dw_ref)
            dvec_ref[...] = jnp.zeros_like(dvec_ref)
        _glu_into(uc_ref, vpad, S)
        dzpad[pl.ds(S, CONV_HALO), :] = jnp.zeros((CONV_HALO, CONV_W), F32)

        def norm_bwd(i, carry):
            r0 = pl.multiple_of(i * CONV_CHUNK, CONV_CHUNK)
            xn, rstd, yln = _conv_ln(vpad, w_ref, vec_ref, r0)
            sg = _sigmoid(yln)
            dyln = dc_ref[0, pl.ds(r0, CONV_CHUNK), :] * (sg * (1.0 + yln * (1.0 - sg)))
            dxn = dyln * vec_ref[1:2, :]
            dz = rstd * (dxn - jnp.mean(dxn, axis=-1, keepdims=True) - xn * jnp.mean(dxn * xn, axis=-1, keepdims=True))
            dzpad[pl.ds(r0, CONV_CHUNK), :] = dz
            dvec_ref[0:1, :] += jnp.sum(dz, axis=0, keepdims=True)
            dvec_ref[1:2, :] += jnp.sum(dyln * xn, axis=0, keepdims=True)
            dvec_ref[2:3, :] += jnp.sum(dyln, axis=0, keepdims=True)
            return carry
        lax.fori_loop(0, n_chunks, norm_bwd, 0)

        def taps_bwd(i, carry):
            r0 = pl.multiple_of(i * CONV_CHUNK, CONV_CHUNK)
            dzwin = dzpad[pl.ds(r0, CONV_CHUNK + CONV_HALO), :]
            dv = _conv_taps(dzwin, w_ref, lambda j: CONV_K - 1 - j)
            vwin = vpad[pl.ds(r0, CONV_CHUNK + CONV_HALO), :]
            dz = dzwin[0:CONV_CHUNK]
            for j in range(CONV_K):
                s = j + CONV_HALO - (CONV_K - 1)
                dw_ref[j:j + 1, :] += jnp.sum(dz * vwin[s:s + CONV_CHUNK], axis=0, keepdims=True)
            a = uc_ref[0, pl.ds(r0, CONV_CHUNK), 0:CONV_W]
            sg = _sigmoid(uc_ref[0, pl.ds(r0, CONV_CHUNK), CONV_W:2 * CONV_W])
            duc_ref[0, pl.ds(r0, CONV_CHUNK), 0:CONV_W] = (dv * sg).astype(BF16)
            duc_ref[0, pl.ds(r0, CONV_CHUNK), CONV_W:2 * CONV_W] = (dv * a * sg * (1.0 - sg)).astype(BF16)
            return carry
        lax.fori_loop(0, n_chunks, taps_bwd, 0)

    return pl.pallas_call(
        body, grid=(NB,), name="conv_bwd",
        in_specs=[pl.BlockSpec((1, S, 2 * CONV_W), lambda b: (b, 0, 0)), pl.BlockSpec((1, S, CONV_W), lambda b: (b, 0, 0)),
                  _const((CONV_K, CONV_W)), _const((8, CONV_W))],
        out_specs=[pl.BlockSpec((1, S, 2 * CONV_W), lambda b: (b, 0, 0)), pl.BlockSpec((32, CONV_W), lambda b: (0, 0)),
                   pl.BlockSpec((8, CONV_W), lambda b: (0, 0))],
        out_shape=[jax.ShapeDtypeStruct((NB, S, 2 * CONV_W), BF16), jax.ShapeDtypeStruct((32, CONV_W), F32),
                   jax.ShapeDtypeStruct((8, CONV_W), F32)],
        scratch_shapes=[pltpu.VMEM((S + CONV_HALO, CONV_W), F32), pltpu.VMEM((S + CONV_HALO, CONV_W), F32)],
        compiler_params=_params("arbitrary"),
    )(uc, dcact, dw_w, vec)


def _bias_table(rel_bias):
    qi = jnp.arange(QB)[:, None]
    kj = jnp.arange(KWIN)[None, :]
    ridx = jnp.clip(KPAD + qi - kj, -(CHUNK - 1), MAX_REL) + (CHUNK - 1)
    band = (kj // CHUNK >= qi // CHUNK) & (kj // CHUNK <= qi // CHUNK + LEFT_CHUNKS)
    return jnp.where(band[None], rel_bias[:, ridx], MASK_VALUE)


def _load_keys(i, k_ref, v_ref, kpad, vpad, S):
    @pl.when(i == 0)
    def _():
        kpad[pl.ds(0, KPAD), :] = jnp.zeros((KPAD, ATT_W), BF16)
        vpad[pl.ds(0, KPAD), :] = jnp.zeros((KPAD, ATT_W), BF16)
        kpad[pl.ds(KPAD, S), :] = k_ref[0]
        vpad[pl.ds(KPAD, S), :] = v_ref[0]


def _att_probs(q2, k2, tab_ref, head, in_head, in_seq):
    qm = jnp.where(in_head, q2, jnp.zeros_like(q2))
    s = _dot_nt(qm, k2) * (ATT_HD ** -0.5) + tab_ref[head]
    s = jnp.where(in_seq, s, MASK_VALUE)
    e = jnp.exp(s - jnp.max(s, axis=-1, keepdims=True))
    return e * (1.0 / jnp.sum(e, axis=-1, keepdims=True))


def _att_fwd(qkv, tab):
    NB, S, _ = qkv.shape

    def body(q_ref, k_ref, v_ref, tab_ref, o_ref, kpad, vpad):
        i = pl.program_id(1)
        _load_keys(i, k_ref, v_ref, kpad, vpad, S)
        koff = pl.multiple_of(i * QB, QB)
        lane = lax.broadcasted_iota(jnp.int32, (1, 128), 1)
        in_seq = (lax.broadcasted_iota(jnp.int32, (1, KWIN), 1) + i * QB) >= KPAD
        for pair in range(ATT_HEADS // 2):
            cols = slice(pair * 128, (pair + 1) * 128)
            q2 = q_ref[0, :, cols]
            k2 = kpad[pl.ds(koff, KWIN), cols]
            v2 = vpad[pl.ds(koff, KWIN), cols]
            o2 = jnp.zeros((QB, 128), F32)
            for hh in range(2):
                in_head = (lane // ATT_HD) == hh
                p = _att_probs(q2, k2, tab_ref, 2 * pair + hh, in_head, in_seq)
                o2 = jnp.where(in_head, _dot(p.astype(BF16), v2), o2)
            o_ref[0, :, cols] = o2.astype(BF16)

    seq = lambda col: pl.BlockSpec((1, S, ATT_W), lambda b, i: (b, 0, col), pipeline_mode=pl.Buffered(1))
    return pl.pallas_call(
        body, grid=(NB, S // QB), name="att_fwd",
        in_specs=[pl.BlockSpec((1, QB, ATT_W), lambda b, i: (b, i, 0)), seq(1), seq(2), _const((ATT_HEADS, QB, KWIN))],
        out_specs=pl.BlockSpec((1, QB, ATT_W), lambda b, i: (b, i, 0)),
        out_shape=jax.ShapeDtypeStruct((NB, S, ATT_W), BF16),
        scratch_shapes=[pltpu.VMEM((S + KPAD, ATT_W), BF16), pltpu.VMEM((S + KPAD, ATT_W), BF16)],
        compiler_params=_params("arbitrary", "arbitrary"),
    )(qkv, qkv, qkv, tab)


def _att_bwd(qkv, do, tab):
    NB, S, _ = qkv.shape
    nq = S // QB

    def body(q_ref, k_ref, v_ref, do_ref, tab_ref, dqkv_ref, ds_hbm, kpad, vpad, dkpad, dvpad, ds_acc, ds_sem):
        b, i = pl.program_id(0), pl.program_id(1)
        _load_keys(i, k_ref, v_ref, kpad, vpad, S)

        @pl.when(i == 0)
        def _():
            dkpad[...] = jnp.zeros_like(dkpad)
            dvpad[...] = jnp.zeros_like(dvpad)

        @pl.when((i == 0) & (b == 0))
        def _():
            ds_acc[...] = jnp.zeros_like(ds_acc)

        koff = pl.multiple_of(i * QB, QB)
        lane = lax.broadcasted_iota(jnp.int32, (1, 128), 1)
        in_seq = (lax.broadcasted_iota(jnp.int32, (1, KWIN), 1) + i * QB) >= KPAD
        for pair in range(ATT_HEADS // 2):
            cols = slice(pair * 128, (pair + 1) * 128)
            q2 = q_ref[0, :, cols]
            do2 = do_ref[0, :, cols]
            k2 = kpad[pl.ds(koff, KWIN), cols]
            v2 = vpad[pl.ds(koff, KWIN), cols]
            dq2 = jnp.zeros((QB, 128), F32)
            dk2 = jnp.zeros((KWIN, 128), F32)
            dv2 = jnp.zeros((KWIN, 128), F32)
            for hh in range(2):
                head = 2 * pair + hh
                in_head = (lane // ATT_HD) == hh
                p = _att_probs(q2, k2, tab_ref, head, in_head, in_seq)
                dom = jnp.where(in_head, do2, jnp.zeros_like(do2))
                dp = _dot_nt(dom, v2)
                ds = p * (dp - jnp.sum(p * dp, axis=-1, keepdims=True))
                ds_acc[head] += ds
                dss = (ds * (ATT_HD ** -0.5)).astype(BF16)
                dq2 = jnp.where(in_head, _dot(dss, k2), dq2)
                dk2 = jnp.where(in_head, _dot_tn(dss, q2), dk2)
                dv2 = jnp.where(in_head, _dot_tn(p.astype(BF16), do2), dv2)
            dqkv_ref[0, pl.ds(koff, QB), cols] = dq2.astype(BF16)
            dkpad[pl.ds(koff, KWIN), cols] += dk2
            dvpad[pl.ds(koff, KWIN), cols] += dv2

        @pl.when(i == nq - 1)
        def _():
            dqkv_ref[0, :, ATT_W:2 * ATT_W] = dkpad[pl.ds(KPAD, S), :].astype(BF16)
            dqkv_ref[0, :, 2 * ATT_W:3 * ATT_W] = dvpad[pl.ds(KPAD, S), :].astype(BF16)

        @pl.when((i == nq - 1) & (b == NB - 1))
        def _():
            out = pltpu.make_async_copy(ds_acc, ds_hbm, ds_sem)
            out.start()
            out.wait()

    seq = lambda col: pl.BlockSpec((1, S, ATT_W), lambda b, i: (b, 0, col), pipeline_mode=pl.Buffered(1))
    return pl.pallas_call(
        body, grid=(NB, nq), name="att_bwd",
        in_specs=[pl.BlockSpec((1, QB, ATT_W), lambda b, i: (b, i, 0)), seq(1), seq(2),
                  pl.BlockSpec((1, QB, ATT_W), lambda b, i: (b, i, 0)), _const((ATT_HEADS, QB, KWIN))],
        out_specs=[pl.BlockSpec((1, S, 3 * ATT_W), lambda b, i: (b, 0, 0)), ANY],
        out_shape=[jax.ShapeDtypeStruct((NB, S, 3 * ATT_W), BF16), jax.ShapeDtypeStruct((ATT_HEADS, QB, KWIN), F32)],
        scratch_shapes=[pltpu.VMEM((S + KPAD, ATT_W), BF16), pltpu.VMEM((S + KPAD, ATT_W), BF16),
                        pltpu.VMEM((S + KPAD, ATT_W), F32), pltpu.VMEM((S + KPAD, ATT_W), F32),
                        pltpu.VMEM((ATT_HEADS, QB, KWIN), F32), pltpu.SemaphoreType.DMA],
        compiler_params=_params("arbitrary", "arbitrary"),
    )(qkv, qkv, qkv, do, tab)


def _rel_bias_grad(ds):
    def body(ds_ref, o_ref):
        sub = lax.broadcasted_iota(jnp.int32, (8, 1), 0)
        col = lax.broadcasted_iota(jnp.int32, (DS_LANES, 1), 0)
        offset = jnp.where(col < KWIN, col, col - DS_LANES)
        ridx = jnp.clip(KPAD - offset, -(CHUNK - 1), MAX_REL) + (CHUNK - 1)
        onehot = (ridx == lax.broadcasted_iota(jnp.int32, (1, N_REL), 1)).astype(F32)
        for head in range(ATT_HEADS):
            def rows8(q8, acc):
                tile = ds_ref[head, pl.ds(pl.multiple_of(q8 * 8, 8), 8), :]
                tile = jnp.concatenate([tile, jnp.zeros((8, DS_LANES - KWIN), F32)], axis=1)
                return acc + pltpu.roll(tile, lax.rem(DS_LANES - q8 * 8, DS_LANES), 1)
            acc = lax.fori_loop(0, QB // 8, rows8, jnp.zeros((8, DS_LANES), F32))
            diag = jnp.zeros((8, DS_LANES), F32)
            for s in range(8):
                shifted = acc if s == 0 else pltpu.roll(acc, DS_LANES - s, 1)
                diag = jnp.where(sub == s, shifted, diag)
            z = jnp.sum(diag, axis=0, keepdims=True)
            o_ref[head:head + 1, :] = jnp.dot(z, onehot, preferred_element_type=F32, precision=lax.Precision.HIGHEST)

    return pl.pallas_call(body, out_shape=jax.ShapeDtypeStruct((ATT_HEADS, N_REL), F32), name="rel_bias_grad",
                          compiler_params=_params())(ds)


def _memkv_fwd(mem, g, w_kv, tm):
    R = mem.shape[0]
    tm = min(tm, R)

    def body(m_ref, g_ref, w_ref, h_ref, kv_ref):
        xhat, _ = _rms_stats(m_ref[...])
        h = (xhat * g_ref[...]).astype(BF16)
        h_ref[...] = h
        kv_ref[...] = _dot(h, w_ref[...]).astype(BF16)

    row = pl.BlockSpec((tm, D), lambda t: (t, 0))
    return pl.pallas_call(
        body, grid=(R // tm,), name="memkv_fwd", in_specs=[row, _const((1, D)), _const((D, 2 * MEM_W))], out_specs=[row, row],
        out_shape=[jax.ShapeDtypeStruct((R, D), BF16), jax.ShapeDtypeStruct((R, 2 * MEM_W), BF16)],
        compiler_params=_params("parallel"),
    )(mem, g, w_kv)


def _memkv_bwd(mem, dkv, w_kv, tm):
    R = mem.shape[0]
    tm = min(tm, R)

    def body(m_ref, dkv_ref, w_ref, dg_ref):
        xhat, _ = _rms_stats(m_ref[...])
        dh = _dot_nt(dkv_ref[...].astype(BF16), w_ref[...])

        @pl.when(pl.program_id(0) == 0)
        def _():
            dg_ref[...] = jnp.zeros_like(dg_ref)
        dg_ref[...] += jnp.sum(dh * xhat, axis=0, keepdims=True)

    row = pl.BlockSpec((tm, D), lambda t: (t, 0))
    return pl.pallas_call(
        body, grid=(R // tm,), name="memkv_bwd", in_specs=[row, row, _const((D, 2 * MEM_W))],
        out_specs=pl.BlockSpec((1, D), lambda t: (0, 0)), out_shape=jax.ShapeDtypeStruct((1, D), F32),
        compiler_params=_params("arbitrary"),
    )(mem, dkv, w_kv)


def _mem_probs(qh, kh):
    s = _dot_nt(qh, kh) * (MEM_HD ** -0.5)
    e = jnp.exp(s - jnp.max(s, axis=-1, keepdims=True))
    return e * (1.0 / jnp.sum(e, axis=-1, keepdims=True))


def _mematt_fwd(mq, kv, tq):
    NB, S, _ = mq.shape
    M = kv.shape[1]

    def body(q_ref, kv_ref, o_ref):
        for h in range(MEM_HEADS):
            cols = slice(h * MEM_HD, (h + 1) * MEM_HD)
            p = _mem_probs(q_ref[0, :, cols], kv_ref[0, :, cols])
            o_ref[0, :, cols] = _dot(p.astype(BF16), kv_ref[0, :, MEM_W + h * MEM_HD:MEM_W + (h + 1) * MEM_HD]).astype(BF16)

    return pl.pallas_call(
        body, grid=(NB, S // tq), name="mematt_fwd",
        in_specs=[pl.BlockSpec((1, tq, MEM_W), lambda b, i: (b, i, 0)), pl.BlockSpec((1, M, 2 * MEM_W), lambda b, i: (b, 0, 0))],
        out_specs=pl.BlockSpec((1, tq, MEM_W), lambda b, i: (b, i, 0)),
        out_shape=jax.ShapeDtypeStruct((NB, S, MEM_W), BF16), compiler_params=_params("parallel", "parallel"),
    )(mq, kv)


def _mematt_bwd(mq, kv, do, tq):
    NB, S, _ = mq.shape
    M = kv.shape[1]

    def body(q_ref, kv_ref, do_ref, dq_ref, dkv_ref):
        @pl.when(pl.program_id(1) == 0)
        def _():
            dkv_ref[...] = jnp.zeros_like(dkv_ref)
        for h in range(MEM_HEADS):
            cols = slice(h * MEM_HD, (h + 1) * MEM_HD)
            vcols = slice(MEM_W + h * MEM_HD, MEM_W + (h + 1) * MEM_HD)
            qh, kh, vh, doh = q_ref[0, :, cols], kv_ref[0, :, cols], kv_ref[0, :, vcols], do_ref[0, :, cols]
            p = _mem_probs(qh, kh)
            dp = _dot_nt(doh, vh)
            ds = p * (dp - jnp.sum(p * dp, axis=-1, keepdims=True))
            dss = (ds * (MEM_HD ** -0.5)).astype(BF16)
            dq_ref[0, :, cols] = _dot(dss, kh).astype(BF16)
            dkv_ref[0, :, cols] += _dot_tn(dss, qh)
            dkv_ref[0, :, vcols] += _dot_tn(p.astype(BF16), doh)

    qspec = pl.BlockSpec((1, tq, MEM_W), lambda b, i: (b, i, 0))
    kvspec = pl.BlockSpec((1, M, 2 * MEM_W), lambda b, i: (b, 0, 0))
    return pl.pallas_call(
        body, grid=(NB, S // tq), name="mematt_bwd", in_specs=[qspec, kvspec, qspec], out_specs=[qspec, kvspec],
        out_shape=[jax.ShapeDtypeStruct((NB, S, MEM_W), BF16), jax.ShapeDtypeStruct((NB, M, 2 * MEM_W), F32)],
        compiler_params=_params("arbitrary", "arbitrary"),
    )(mq, kv, do)


def _branches(c_ref, a_ref, m_ref, gl_ref, bg_ref, wpw_ref, wo_ref, wmo_ref):
    ys = [_dot(c_ref[...], wpw_ref[...]), _dot(a_ref[...], wo_ref[...]), _dot(m_ref[...], wmo_ref[...])]
    gates = [_sigmoid(gl_ref[:, j * D:(j + 1) * D] + bg_ref[:, j * D:(j + 1) * D]) for j in range(3)]
    return ys, gates


def _combine_fwd(x, cact, oatt, omem, gl, bg, wpw, wo, wmo, wout, tm):
    T = x.shape[0]

    def body(x_ref, c_ref, a_ref, m_ref, gl_ref, bg_ref, wpw_ref, wo_ref, wmo_ref, wout_ref, xo_ref, y_ref):
        ys, gates = _branches(c_ref, a_ref, m_ref, gl_ref, bg_ref, wpw_ref, wo_ref, wmo_ref)
        y = (gates[0] * ys[0] + gates[1] * ys[1] + gates[2] * ys[2]).astype(BF16)
        y_ref[...] = y
        xo_ref[...] = x_ref[...] + _dot(y, wout_ref[...])

    row = lambda w: pl.BlockSpec((tm, w), lambda t: (t, 0))
    wbr = _const((512, D))
    return pl.pallas_call(
        body, grid=(T // tm,), name="combine_fwd",
        in_specs=[row(D), row(512), row(512), row(512), row(3 * D), _const((1, 3 * D)), wbr, wbr, wbr, _const((D, D))],
        out_specs=[row(D), row(D)],
        out_shape=[jax.ShapeDtypeStruct((T, D), F32), jax.ShapeDtypeStruct((T, D), BF16)],
        compiler_params=_params("parallel"),
    )(x, cact, oatt, omem, gl, bg, wpw, wo, wmo, wout)


def _combine_bwd(dx, cact, oatt, omem, gl, bg, wpw, wo, wmo, wout, tm):
    T = dx.shape[0]

    def body(dx_ref, c_ref, a_ref, m_ref, gl_ref, bg_ref, wpw_ref, wo_ref, wmo_ref, wout_ref,
             dgl_ref, dc_ref, da_ref, dm_ref, dyc_ref, dya_ref, dym_ref, dbg_ref):
        ys, gates = _branches(c_ref, a_ref, m_ref, gl_ref, bg_ref, wpw_ref, wo_ref, wmo_ref)
        dy = _dot_nt(dx_ref[...].astype(BF16), wout_ref[...])

        @pl.when(pl.program_id(0) == 0)
        def _():
            dbg_ref[...] = jnp.zeros_like(dbg_ref)
        dyb = []
        for j in range(3):
            dlogit = dy * ys[j] * gates[j] * (1.0 - gates[j])
            dgl_ref[:, j * D:(j + 1) * D] = dlogit.astype(BF16)
            dbg_ref[:, j * D:(j + 1) * D] += jnp.sum(dlogit, axis=0, keepdims=True)
            dyb.append((dy * gates[j]).astype(BF16))
        dyc_ref[...], dya_ref[...], dym_ref[...] = dyb
        dc_ref[...] = _dot_nt(dyb[0], wpw_ref[...])
        da_ref[...] = _dot_nt(dyb[1], wo_ref[...]).astype(BF16)
        dm_ref[...] = _dot_nt(dyb[2], wmo_ref[...]).astype(BF16)

    row = lambda w: pl.BlockSpec((tm, w), lambda t: (t, 0))
    wbr = _const((512, D))
    sds = jax.ShapeDtypeStruct
    return pl.pallas_call(
        body, grid=(T // tm,), name="combine_bwd",
        in_specs=[row(D), row(512), row(512), row(512), row(3 * D), _const((1, 3 * D)), wbr, wbr, wbr, _const((D, D))],
        out_specs=[row(3 * D), row(512), row(512), row(512), row(D), row(D), row(D), pl.BlockSpec((1, 3 * D), lambda t: (0, 0))],
        out_shape=[sds((T, 3 * D), BF16), sds((T, 512), F32), sds((T, 512), BF16), sds((T, 512), BF16),
                   sds((T, D), BF16), sds((T, D), BF16), sds((T, D), BF16), sds((1, 3 * D), F32)],
        compiler_params=_params("arbitrary"),
    )(dx, cact, oatt, omem, gl, bg, wpw, wo, wmo, wout)


def _final(x, g, target, tm):
    T = x.shape[0]

    def body(x_ref, g_ref, t_ref, loss_ref, dx_ref, dg_ref):
        xhat, r = _rms_stats(x_ref[...])
        gain = g_ref[...]
        diff = xhat * gain - t_ref[...]
        dout = diff * (1.0 / D)

        @pl.when(pl.program_id(0) == 0)
        def _():
            loss_ref[...] = jnp.zeros_like(loss_ref)
            dg_ref[...] = jnp.zeros_like(dg_ref)
        sq = jnp.sum(jnp.sum(diff * diff, axis=0, keepdims=True), axis=1, keepdims=True)
        loss_ref[...] += jnp.broadcast_to(sq * (0.5 / D), (1, 128))
        dg_ref[...] += jnp.sum(dout * xhat, axis=0, keepdims=True)
        dx_ref[...] = _rms_bwd(dout, gain, xhat, r)

    row = pl.BlockSpec((tm, D), lambda t: (t, 0))
    return pl.pallas_call(
        body, grid=(T // tm,), name="final_loss", in_specs=[row, _const((1, D)), row],
        out_specs=[pl.BlockSpec((1, 128), lambda t: (0, 0)), row, pl.BlockSpec((1, D), lambda t: (0, 0))],
        out_shape=[jax.ShapeDtypeStruct((1, 128), F32), jax.ShapeDtypeStruct((T, D), F32), jax.ShapeDtypeStruct((1, D), F32)],
        compiler_params=_params("arbitrary"),
    )(x, g, target)


def _peer(x, y, c, rel):
    rx, ry, rc = (rel >> 2) & 1, (rel >> 1) & 1, rel & 1
    return ((1 - x) if rx else x, (1 - y) if ry else y, (1 - c) if rc else c)


def _all_sum_small(part):
    def body(p_ref, o_ref, slots, send_sems, recv_sems):
        x, y, c = _my_coords()
        me = _dev_index(x, y, c)
        slots[me] = p_ref[...]
        copies = []
        for rel in range(1, NDEV):
            cp = pltpu.make_async_remote_copy(src_ref=p_ref, dst_ref=slots.at[me], send_sem=send_sems.at[rel - 1],
                                              recv_sem=recv_sems.at[rel - 1], device_id=_peer(x, y, c, rel), device_id_type=MESH)
            cp.start()
            copies.append(cp)
        for rel in range(1, NDEV):
            src_dev = _dev_index(*_peer(x, y, c, rel))
            pltpu.make_async_remote_copy(src_ref=p_ref, dst_ref=slots.at[src_dev], send_sem=send_sems.at[rel - 1],
                                         recv_sem=recv_sems.at[rel - 1], device_id=_peer(x, y, c, rel), device_id_type=MESH).wait_recv()
        for cp in copies:
            cp.wait_send()
        total = slots[0]
        for d in range(1, NDEV):
            total = total + slots[d]
        o_ref[...] = total

    return pl.pallas_call(
        body, out_shape=jax.ShapeDtypeStruct(part.shape, F32), name="all_sum_small",
        in_specs=[pl.BlockSpec(memory_space=pltpu.VMEM)], out_specs=pl.BlockSpec(memory_space=pltpu.VMEM),
        scratch_shapes=[pltpu.VMEM((NDEV,) + part.shape, F32), pltpu.SemaphoreType.DMA((NDEV - 1,)), pltpu.SemaphoreType.DMA((NDEV - 1,))],
        compiler_params=pltpu.CompilerParams(has_side_effects=True),
    )(part)


def _scatter_grads(grads, kinds):
    n = len(grads)

    def body(*refs):
        g = refs[:n]
        out = refs[n:2 * n]
        send_sems, recv_sems, local_sems = refs[2 * n:]
        x, y, c = _my_coords()
        me = _dev_index(x, y, c)
        copies = []
        for i in range(n):
            kind, m = kinds[i]
            own = pltpu.make_async_copy(_window(g[i], kind, m, me), out[i].at[me], local_sems.at[i])
            own.start()
            copies.append(own)
        sends = []
        for rel in range(1, NDEV):
            peer = _peer(x, y, c, rel)
            for i in range(n):
                kind, m = kinds[i]
                cp = pltpu.make_async_remote_copy(src_ref=_window(g[i], kind, m, _dev_index(*peer)), dst_ref=out[i].at[me],
                                                  send_sem=send_sems.at[i, rel - 1], recv_sem=recv_sems.at[i, rel - 1],
                                                  device_id=peer, device_id_type=MESH)
                cp.start()
                sends.append(cp)
        for rel in range(1, NDEV):
            peer = _peer(x, y, c, rel)
            src_dev = _dev_index(*peer)
            for i in range(n):
                kind, m = kinds[i]
                pltpu.make_async_remote_copy(src_ref=_window(g[i], kind, m, me), dst_ref=out[i].at[src_dev],
                                             send_sem=send_sems.at[i, rel - 1], recv_sem=recv_sems.at[i, rel - 1],
                                             device_id=peer, device_id_type=MESH).wait_recv()
        for cp in sends:
            cp.wait_send()
        for cp in copies:
            cp.wait()

    def block_shape(gr, kind, m):
        return (m, gr.shape[1]) if kind in ('row', 'wd') else (gr.shape[0], m)

    out_shape = [jax.ShapeDtypeStruct((NDEV,) + block_shape(gr, k, m), gr.dtype) for gr, (k, m) in zip(grads, kinds)]
    return pl.pallas_call(
        body, out_shape=out_shape, in_specs=[ANY] * n, out_specs=[ANY] * n, name="scatter_grads",
        scratch_shapes=[pltpu.SemaphoreType.DMA((n, NDEV - 1)), pltpu.SemaphoreType.DMA((n, NDEV - 1)), pltpu.SemaphoreType.DMA((n,))],
        compiler_params=pltpu.CompilerParams(has_side_effects=True),
    )(*grads)


def _adamw_math(w, g, m, v):
    m = ADAM_B1 * m + (1.0 - ADAM_B1) * g
    v = ADAM_B2 * v + (1.0 - ADAM_B2) * (g * g)
    m_hat = m / (1.0 - ADAM_B1 ** ADAM_STEP)
    v_hat = v / (1.0 - ADAM_B2 ** ADAM_STEP)
    delta = -ADAM_LR * (m_hat / (jnp.sqrt(v_hat) + ADAM_EPS) + ADAM_WD * w)
    return delta, m, v


def _sum_adamw(parts, w, m, v, name):
    R, C = w.shape
    Cp = parts.shape[2]
    tr = max(t for t in range(8, 257, 8) if R % t == 0)

    def body(p_ref, w_ref, m_ref, v_ref, g_ref, d_ref, mo_ref, vo_ref):
        g = p_ref[0, :, pl.ds(0, C)].astype(F32)
        for d in range(1, NDEV):
            g = g + p_ref[d, :, pl.ds(0, C)].astype(F32)
        g_ref[...] = g
        d_ref[...], mo_ref[...], vo_ref[...] = _adamw_math(w_ref[...], g, m_ref[...], v_ref[...])

    row = pl.BlockSpec((tr, C), lambda t: (t, 0))
    return pl.pallas_call(
        body, grid=(R // tr,), name=name, in_specs=[pl.BlockSpec((NDEV, tr, Cp), lambda t: (0, t, 0)), row, row, row],
        out_specs=[row] * 4, out_shape=[jax.ShapeDtypeStruct((R, C), F32)] * 4, compiler_params=_params("parallel"),
    )(parts, w, m, v)


def _adamw_small(w, g, m, v, name):
    def body(w_ref, g_ref, m_ref, v_ref, d_ref, mo_ref, vo_ref):
        d_ref[...], mo_ref[...], vo_ref[...] = _adamw_math(w_ref[...], g_ref[...], m_ref[...], v_ref[...])
    return pl.pallas_call(body, out_shape=[jax.ShapeDtypeStruct(w.shape, F32)] * 3, name=name, compiler_params=_params())(w, g, m, v)


def _pack_small(parts):
    rows = [jnp.reshape(parts[name], (size // 128, 128)) for name, size in SMALL]
    used = sum(size // 128 for _, size in SMALL)
    rows.append(jnp.zeros((SMALL_ROWS - used, 128), F32))
    return jnp.concatenate(rows, axis=0)


def _unpack_small(packed):
    out, r = {}, 0
    for name, size in SMALL:
        out[name] = packed[r:r + size // 128]
        r += size // 128
    return out


def kernel(x, mem, ffn1_norm, ffn1_w_up, ffn1_w_down, mix_norm, mem_norm, w_in, b_gate, conv_dw_w, conv_dw_b, conv_ln_g, conv_ln_b, conv_w_pw, att_rel_bias, att_w_o, mem_w_kv, mem_w_o, w_out, ffn2_norm, ffn2_w_up, ffn2_w_down, final_norm, loss_target, m_ffn1_norm, m_ffn1_w_up, m_ffn1_w_down, m_mix_norm, m_mem_norm, m_w_in, m_b_gate, m_conv_dw_w, m_conv_dw_b, m_conv_ln_g, m_conv_ln_b, m_conv_w_pw, m_att_rel_bias, m_att_w_o, m_mem_w_kv, m_mem_w_o, m_w_out, m_ffn2_norm, m_ffn2_w_up, m_ffn2_w_down, m_final_norm, v_ffn1_norm, v_ffn1_w_up, v_ffn1_w_down, v_mix_norm, v_mem_norm, v_w_in, v_b_gate, v_conv_dw_w, v_conv_dw_b, v_conv_ln_g, v_conv_ln_b, v_conv_w_pw, v_att_rel_bias, v_att_w_o, v_mem_w_kv, v_mem_w_o, v_w_out, v_ffn2_norm, v_ffn2_w_up, v_ffn2_w_down, v_final_norm):
    given = dict(locals())
    w = {n: given[n] for n in WEIGHTS}
    mom = {n: given["m_" + n] for n in WEIGHTS}
    var = {n: given["v_" + n] for n in WEIGHTS}

    NB, S, _ = x.shape
    T = NB * S
    ML = mem.shape[1]
    x0 = x.reshape(T, D)
    target = loss_target.reshape(T, D)
    mem2 = mem.reshape(NB * ML, D)

    shards = _cast_shards([w[n][0] for n in BIG_ORDER])
    dw_t = jnp.transpose(conv_dw_w[0])
    gathered = _all_gather(list(shards) + [dw_t], [BIG[n] for n in BIG_ORDER] + [('row', dw_t.shape[0])],
                           jnp.zeros((FF_SHARD_PAD - 2 * FF_HALF_ROWS, D), BF16))
    W = dict(zip(BIG_ORDER, gathered[:-1]))
    dw_full = jnp.transpose(gathered[-1])
    conv_vec = jnp.concatenate([conv_dw_b, conv_ln_g, conv_ln_b, jnp.zeros((5, CONV_W), F32)], axis=0)
    tab = _bias_table(att_rel_bias[0])
    fin_g = final_norm.reshape(1, D)

    x1, ab1 = _ffn_fwd(x0, ffn1_norm, W['ffn1_w_up'], W['ffn1_w_down'], 256, "ffn1_fwd")
    uc, qkv, mq, gl, hmix = _mix_fwd(x1, mix_norm, W['w_in'], 512)
    uc3 = uc.reshape(NB, S, 2 * CONV_W)
    qkv3 = qkv.reshape(NB, S, 3 * ATT_W)
    mq3 = mq.reshape(NB, S, MEM_W)
    cact = _conv_fwd(uc3, dw_full, conv_vec).reshape(T, CONV_W)
    oatt = _att_fwd(qkv3, tab).reshape(T, ATT_W)
    memh, kv = _memkv_fwd(mem2, mem_norm, W['mem_w_kv'], 512)
    kv3 = kv.reshape(NB, ML, 2 * MEM_W)
    omem = _mematt_fwd(mq3, kv3, 512).reshape(T, MEM_W)
    x2, ymix = _combine_fwd(x1, cact, oatt, omem, gl, b_gate, W['conv_w_pw'], W['att_w_o'], W['mem_w_o'], W['w_out'], 256)
    x3, ab2 = _ffn_fwd(x2, ffn2_norm, W['ffn2_w_up'], W['ffn2_w_down'], 256, "ffn2_fwd")
    loss_part, dx3, dg_final = _final(x3, fin_g, target, 512)

    G = {}
    dx2, dab2, act2, h2, dg_ffn2 = _ffn_bwd(x2, dx3, ab2, ffn2_norm, W['ffn2_w_up'], W['ffn2_w_down'], 256, "ffn2_bwd")
    G['ffn2_w_up'] = _tn_matmul(h2, dab2, 768, "grad_ffn2_w_up")
    G['ffn2_w_down'] = _tn_matmul(act2, dx3, 512, "grad_ffn2_w_down", scale=0.5)
    dgl, dcact, doatt, domem, dyc, dya, dym, dbg = _combine_bwd(
        dx2, cact, oatt, omem, gl, b_gate, W['conv_w_pw'], W['att_w_o'], W['mem_w_o'], W['w_out'], 256)
    G['w_out'] = _tn_matmul(ymix, dx2, 512, "grad_w_out")
    G['conv_w_pw'] = _tn_matmul(cact, dyc, 512, "grad_conv_w_pw")
    G['att_w_o'] = _tn_matmul(oatt, dya, 512, "grad_att_w_o")
    G['mem_w_o'] = _tn_matmul(omem, dym, 512, "grad_mem_w_o")
    dmq3, dkv3 = _mematt_bwd(mq3, kv3, domem.reshape(NB, S, MEM_W), 512)
    dkv = dkv3.reshape(NB * ML, 2 * MEM_W)
    dg_mem = _memkv_bwd(mem2, dkv, W['mem_w_kv'], 512)
    G['mem_w_kv'] = _tn_matmul(memh, dkv, 512, "grad_mem_w_kv")
    dqkv3, dscore = _att_bwd(qkv3, doatt.reshape(NB, S, ATT_W), tab)
    d_rel = _rel_bias_grad(dscore)
    duc3, d_dw, d_cvec = _conv_bwd(uc3, dcact.reshape(NB, S, CONV_W), dw_full, conv_vec)
    duc, dqkv, dmq = duc3.reshape(T, 2 * CONV_W), dqkv3.reshape(T, 3 * ATT_W), dmq3.reshape(T, MEM_W)
    g_in = _tn_matmul(hmix, duc, 512, "grad_w_in_conv", out_cols=IN_COLS, col_off=0)
    g_in = _tn_matmul(hmix, dqkv, 512, "grad_w_in_qkv", out_cols=IN_COLS, col_off=1024, prev=g_in)
    g_in = _tn_matmul(hmix, dmq, 512, "grad_w_in_mq", out_cols=IN_COLS, col_off=2560, prev=g_in)
    G['w_in'] = _tn_matmul(hmix, dgl, 512, "grad_w_in_gate", out_cols=IN_COLS, col_off=3072, prev=g_in)
    dx1, dg_mix = _mix_bwd(x1, dx2, duc, dqkv, dmq, dgl, mix_norm, W['w_in'], 512)
    dx0, dab1, act1, h1, dg_ffn1 = _ffn_bwd(x0, dx1, ab1, ffn1_norm, W['ffn1_w_up'], W['ffn1_w_down'], 256, "ffn1_bwd")
    G['ffn1_w_up'] = _tn_matmul(h1, dab1, 768, "grad_ffn1_w_up")
    G['ffn1_w_down'] = _tn_matmul(act1, dx1, 512, "grad_ffn1_w_down", scale=0.5)

    small = _unpack_small(_all_sum_small(_pack_small({
        'loss': loss_part, 'ffn1_norm': dg_ffn1, 'mix_norm': dg_mix, 'mem_norm': dg_mem, 'b_gate': dbg,
        'conv_dw_w': d_dw[:CONV_K], 'conv_dw_b': d_cvec[0], 'conv_ln_g': d_cvec[1], 'conv_ln_b': d_cvec[2],
        'att_rel_bias': d_rel, 'ffn2_norm': dg_ffn2, 'final_norm': dg_final})))
    loss = small['loss'][0, 0]
    me = _dev_index(*_my_coords())
    grad, delta, new_m, new_v = {}, {}, {}, {}
    for n in WEIGHTS:
        if n in BIG:
            continue
        shape = w[n].shape
        if n == 'conv_dw_w':
            g_small = lax.dynamic_slice(small[n].reshape(CONV_K, CONV_W), (0, me * shape[2]), (CONV_K, shape[2]))
        else:
            g_small = small[n]
        rows = g_small.shape
        d_, m_, v_ = _adamw_small(w[n].reshape(rows), g_small, mom[n].reshape(rows), var[n].reshape(rows), "adamw_" + n)
        grad[n], delta[n], new_m[n], new_v[n] = (t.reshape(shape) for t in (g_small, d_, m_, v_))

    parts = _scatter_grads([G[n] for n in BIG_ORDER], [BIG[n] for n in BIG_ORDER])
    for n, p in zip(BIG_ORDER, parts):
        shape = w[n].shape
        g_, d_, m_, v_ = _sum_adamw(p, w[n][0], mom[n][0], var[n][0], "adamw_" + n)
        grad[n], delta[n], new_m[n], new_v[n] = (t.reshape(shape) for t in (g_, d_, m_, v_))

    return (loss, dx0.reshape(NB, S, D), *[grad[n] for n in WEIGHTS], *[delta[n] for n in WEIGHTS],
            *[new_m[n] for n in WEIGHTS], *[new_v[n] for n in WEIGHTS])
```

```python
import functools

import jax
import jax.numpy as jnp
from jax import lax
from jax.experimental import pallas as pl
from jax.experimental.pallas import tpu as pltpu

F32 = jnp.float32
BF16 = jnp.bfloat16

EPS = 1e-6
MASK_VALUE = -1e30
D = 1024
NDEV = 8
FF_SHARD = 704
FF_SHARD_PAD = 768
FF_HALF_ROWS = 352
FF_PAD = 4 * FF_SHARD_PAD
IN_COLS = 6144
CONV_W = 512
CONV_K = 31
CONV_HALO = 32
CONV_CHUNK = 128
ATT_W = 512
ATT_HEADS = 8
ATT_HD = 64
CHUNK = 64
LEFT_CHUNKS = 8
MAX_REL = 128
N_REL = 192
QB = 256
KWIN = QB + LEFT_CHUNKS * CHUNK
KPAD = LEFT_CHUNKS * CHUNK
DS_LANES = 1024
MEM_W = 512
MEM_HEADS = 4
MEM_HD = 128
ADAM_LR = 0.001
ADAM_B1 = 0.9
ADAM_B2 = 0.999
ADAM_EPS = 1e-08
ADAM_WD = 0.01
ADAM_STEP = 10
VMEM_LIMIT = 60 * 1024 * 1024
TILE_FFN = 256
TILE_COMBINE = 256
TILE_TOKENS = 512

MESH = pl.DeviceIdType.MESH
ANY = pl.BlockSpec(memory_space=pl.ANY)

WEIGHTS = ['ffn1_norm', 'ffn1_w_up', 'ffn1_w_down', 'mix_norm', 'mem_norm', 'w_in', 'b_gate', 'conv_dw_w', 'conv_dw_b',
           'conv_ln_g', 'conv_ln_b', 'conv_w_pw', 'att_rel_bias', 'att_w_o', 'mem_w_kv', 'mem_w_o', 'w_out', 'ffn2_norm',
           'ffn2_w_up', 'ffn2_w_down', 'final_norm']
BIG = {
    'ffn1_w_up': ('col', FF_SHARD_PAD), 'ffn1_w_down': ('wd', FF_HALF_ROWS), 'w_in': ('col', 768),
    'conv_w_pw': ('col', 128), 'att_w_o': ('col', 128), 'mem_w_kv': ('row', 128), 'mem_w_o': ('col', 128),
    'w_out': ('row', 128), 'ffn2_w_up': ('col', FF_SHARD_PAD), 'ffn2_w_down': ('wd', FF_HALF_ROWS),
}
BIG_ORDER = ['ffn1_w_up', 'ffn1_w_down', 'w_in', 'conv_w_pw', 'att_w_o', 'mem_w_kv', 'mem_w_o', 'w_out', 'ffn2_w_up', 'ffn2_w_down']
SMALL = [('loss', 128), ('ffn1_norm', 1024), ('mix_norm', 1024), ('mem_norm', 1024), ('b_gate', 3072),
         ('conv_dw_w', CONV_K * CONV_W), ('conv_dw_b', 512), ('conv_ln_g', 512), ('conv_ln_b', 512),
         ('att_rel_bias', ATT_HEADS * N_REL), ('ffn2_norm', 1024), ('final_norm', 1024)]
SMALL_ROWS = 216


def _dot(a, b):
    return jnp.dot(a, b, preferred_element_type=F32)


def _dot_nt(a, b):
    return lax.dot_general(a, b, (((1,), (1,)), ((), ())), preferred_element_type=F32)


def _dot_tn(a, b):
    return lax.dot_general(a, b, (((0,), (0,)), ((), ())), preferred_element_type=F32)


def _sigmoid(v):
    return jax.nn.sigmoid(v)


def _const(shape):
    return pl.BlockSpec(shape, lambda *_: (0,) * len(shape), pipeline_mode=pl.Buffered(1))


def _params(*sem):
    return pltpu.CompilerParams(dimension_semantics=sem if sem else None, vmem_limit_bytes=VMEM_LIMIT)


def _my_coords():
    return lax.axis_index("x"), lax.axis_index("y"), lax.axis_index("c")


def _dev_index(px, py, pc):
    return 4 * px + 2 * py + pc


def _window(ref, kind, n, p):
    if kind == 'row':
        return ref.at[pl.ds(pl.multiple_of(p * n, n), n), :]
    if kind == 'col':
        return ref.at[:, pl.ds(pl.multiple_of(p * n, 128), n)]
    start = (p // 2) * FF_SHARD_PAD + (p % 2) * FF_HALF_ROWS
    return ref.at[pl.ds(pl.multiple_of(start, 32), n), :]


def _full_shape(kind, n, shard_shape):
    if kind == 'row':
        return (NDEV * n, shard_shape[1])
    if kind == 'col':
        return (shard_shape[0], NDEV * n)
    return (FF_PAD, shard_shape[1])


def _cast_shards(shards):
    n = len(shards)

    def body(*refs):
        for i in range(n):
            src, dst = refs[i], refs[n + i]
            if src.shape[1] == FF_SHARD:
                dst[:, pl.ds(0, FF_SHARD)] = src[...].astype(BF16)
                dst[:, pl.ds(FF_SHARD, FF_SHARD_PAD - FF_SHARD)] = jnp.zeros((src.shape[0], FF_SHARD_PAD - FF_SHARD), BF16)
            else:
                dst[...] = src[...].astype(BF16)

    out_shape = [jax.ShapeDtypeStruct((s.shape[0], FF_SHARD_PAD if s.shape[1] == FF_SHARD else s.shape[1]), BF16) for s in shards]
    return pl.pallas_call(body, out_shape=out_shape, name="cast_shards", compiler_params=_params())(*shards)


class _Ride:
    def __init__(self, inputs, out_shape, scratch, start, finish, mid=None):
        self.inputs, self.out_shape, self.scratch = list(inputs), list(out_shape), list(scratch)
        self.start, self.finish, self.mid = start, finish, mid


def _pallas(body, name, grid, in_specs, out_specs, out_shape, args, scratch_shapes=(), sem=None, aliases=None, ride=None):
    if ride is None:
        outs = pl.pallas_call(body, grid=grid, name=name, in_specs=in_specs, out_specs=out_specs, out_shape=out_shape,
                              scratch_shapes=list(scratch_shapes), input_output_aliases=aliases or {},
                              compiler_params=_params(*sem))(*args)
        return list(outs), []
    n_in, n_out, n_scr = len(args), len(out_shape), len(scratch_shapes)
    r_in, r_out = len(ride.inputs), len(ride.out_shape)

    def wrapped(*refs):
        k_in, rin = refs[:n_in], refs[n_in:n_in + r_in]
        o0 = n_in + r_in
        k_out, rout = refs[o0:o0 + n_out], refs[o0 + n_out:o0 + n_out + r_out]
        s0 = o0 + n_out + r_out
        k_scr, rscr = refs[s0:s0 + n_scr], refs[s0 + n_scr:]
        ids = [pl.program_id(k) for k in range(len(grid))]
        first = functools.reduce(jnp.logical_and, [i == 0 for i in ids])
        last = functools.reduce(jnp.logical_and, [i == g - 1 for i, g in zip(ids, grid)])
        pl.when(first)(lambda: ride.start(rin, rout, rscr))
        if ride.mid is not None:
            at_mid = functools.reduce(jnp.logical_and, [ids[0] == grid[0] // 2] + [i == 0 for i in ids[1:]])
            pl.when(at_mid)(lambda: ride.mid(rin, rout, rscr))
        body(*k_in, *k_out, *k_scr)
        pl.when(last)(lambda: ride.finish(rin, rout, rscr))

    outs = pl.pallas_call(
        wrapped, grid=grid, name=name, in_specs=list(in_specs) + [ANY] * r_in, out_specs=list(out_specs) + [ANY] * r_out,
        out_shape=list(out_shape) + ride.out_shape, scratch_shapes=list(scratch_shapes) + ride.scratch,
        input_output_aliases=aliases or {}, compiler_params=_params(*(["arbitrary"] * len(grid))),
    )(*args, *ride.inputs)
    return list(outs[:n_out]), list(outs[n_out:])


def _exchange_alone(ride, name):
    r_in, r_out = len(ride.inputs), len(ride.out_shape)

    def body(*refs):
        rin, rout, rscr = refs[:r_in], refs[r_in:r_in + r_out], refs[r_in + r_out:]
        ride.start(rin, rout, rscr)
        if ride.mid is not None:
            ride.mid(rin, rout, rscr)
        ride.finish(rin, rout, rscr)

    return pl.pallas_call(body, out_shape=ride.out_shape, in_specs=[ANY] * r_in, out_specs=[ANY] * r_out, name=name,
                          scratch_shapes=ride.scratch, compiler_params=pltpu.CompilerParams(has_side_effects=True))(*ride.inputs)


def _gather_ride(shards, kinds, zeros_pad=None):
    n = len(shards)
    wd_ids = [i for i, (k, _) in enumerate(kinds) if k == 'wd']

    def plan(rin, out, sems):
        send_sems, recv_sems, local_sems = sems[:3]
        x, y, c = _my_coords()
        me, sibling = (x, y, c), (x, y, 1 - c)
        chips = [(1 - x, y), (x, 1 - y), (1 - x, 1 - y)]

        def win(i, dev):
            return _window(out[i], kinds[i][0], kinds[i][1], _dev_index(*dev))

        def copy(i, k, block, to, from_shard=False):
            return pltpu.make_async_remote_copy(
                src_ref=rin[i] if from_shard else win(i, block), dst_ref=win(i, block),
                send_sem=send_sems.at[i, k], recv_sem=recv_sems.at[i, k], device_id=to, device_id_type=MESH)

        def local():
            cps = [pltpu.make_async_copy(rin[i], win(i, me), local_sems.at[i]) for i in range(n)]
            for j, i in enumerate(wd_ids):
                for q in range(4):
                    pad_rows = out[i].at[pl.ds(q * FF_SHARD_PAD + 2 * FF_HALF_ROWS, FF_SHARD_PAD - 2 * FF_HALF_ROWS), :]
                    cps.append(pltpu.make_async_copy(rin[n], pad_rows, sems[3].at[j, q]))
            return cps

        def first():
            cps = []
            for i in range(n):
                cps.append(copy(i, 0, me, sibling, from_shard=True))
                cps += [copy(i, 1 + j, me, (*chip, c), from_shard=True) for j, chip in enumerate(chips)]
            return cps

        def arrived():
            return [copy(i, 1 + j, (*chip, c), me) for j, chip in enumerate(chips) for i in range(n)]

        def passed():
            return [copy(i, 4 + j, (*chip, c), sibling) for j, chip in enumerate(chips) for i in range(n)]

        def from_sibling():
            cps = [copy(i, 0, sibling, me) for i in range(n)]
            return cps + [copy(i, 4 + j, (*chip, 1 - c), me) for i in range(n) for j, chip in enumerate(chips)]

        return local, first, arrived, passed, from_sibling

    def start(rin, out, sems):
        local, first, _, _, _ = plan(rin, out, sems)
        for cp in local() + first():
            cp.start()

    def mid(rin, out, sems):
        _, _, arrived, passed, _ = plan(rin, out, sems)
        for got, fwd in zip(arrived(), passed()):
            got.wait_recv()
            fwd.start()

    def finish(rin, out, sems):
        local, first, _, passed, from_sibling = plan(rin, out, sems)
        for cp in from_sibling():
            cp.wait_recv()
        for cp in first() + passed():
            cp.wait_send()
        for cp in local():
            cp.wait()

    out_shape = [jax.ShapeDtypeStruct(_full_shape(k, m, s.shape), s.dtype) for s, (k, m) in zip(shards, kinds)]
    scratch = [pltpu.SemaphoreType.DMA((n, 7)), pltpu.SemaphoreType.DMA((n, 7)), pltpu.SemaphoreType.DMA((n,))]
    if wd_ids:
        scratch.append(pltpu.SemaphoreType.DMA((len(wd_ids), 4)))
    return _Ride(list(shards) + ([zeros_pad] if wd_ids else []), out_shape, scratch, start, finish, mid)


def _scatter_ride(grads, kinds):
    n = len(grads)

    def plan(g, out, sems):
        send_sems, recv_sems, local_sems = sems
        x, y, c = _my_coords()
        me = _dev_index(x, y, c)

        def local():
            return [pltpu.make_async_copy(_window(g[i], kinds[i][0], kinds[i][1], me), out[i].at[me], local_sems.at[i])
                    for i in range(n)]

        def remote(arrival):
            cps = []
            for rel in range(1, NDEV):
                peer = _peer(x, y, c, rel)
                dev = _dev_index(*peer)
                for i in range(n):
                    kind, m = kinds[i]
                    cps.append(pltpu.make_async_remote_copy(
                        src_ref=_window(g[i], kind, m, me if arrival else dev), dst_ref=out[i].at[dev if arrival else me],
                        send_sem=send_sems.at[i, rel - 1], recv_sem=recv_sems.at[i, rel - 1], device_id=peer, device_id_type=MESH))
            return cps

        return local, remote

    def start(g, out, sems):
        local, remote = plan(g, out, sems)
        for cp in local() + remote(False):
            cp.start()

    def finish(g, out, sems):
        local, remote = plan(g, out, sems)
        for cp in remote(True):
            cp.wait_recv()
        for cp in remote(False):
            cp.wait_send()
        for cp in local():
            cp.wait()

    def block_shape(gr, kind, m):
        return (m, gr.shape[1]) if kind in ('row', 'wd') else (gr.shape[0], m)

    out_shape = [jax.ShapeDtypeStruct((NDEV,) + block_shape(gr, k, m), gr.dtype) for gr, (k, m) in zip(grads, kinds)]
    scratch = [pltpu.SemaphoreType.DMA((n, NDEV - 1)), pltpu.SemaphoreType.DMA((n, NDEV - 1)), pltpu.SemaphoreType.DMA((n,))]
    return _Ride(grads, out_shape, scratch, start, finish)


def _rms_stats(xf):
    r = lax.rsqrt(jnp.mean(xf * xf, axis=-1, keepdims=True) + EPS)
    return xf * r, r


def _rms_bwd(dh, g, xhat, r):
    dxhat = dh * g
    return r * (dxhat - xhat * jnp.mean(dxhat * xhat, axis=-1, keepdims=True))


def _ffn_fwd(x, g, wu, wd, tm, name, ride=None):
    T = x.shape[0]

    def body(x_ref, g_ref, wu_ref, wd_ref, xo_ref, ab_ref):
        xf = x_ref[...]
        xhat, _ = _rms_stats(xf)
        h = (xhat * g_ref[...]).astype(BF16)
        acc = jnp.zeros((tm, D), F32)
        for j in range(4):
            ca = slice(j * FF_SHARD_PAD, (j + 1) * FF_SHARD_PAD)
            cb = slice(FF_PAD + j * FF_SHARD_PAD, FF_PAD + (j + 1) * FF_SHARD_PAD)
            a = _dot(h, wu_ref[:, ca])
            b = _dot(h, wu_ref[:, cb])
            ab_ref[:, ca] = a.astype(BF16)
            ab_ref[:, cb] = b.astype(BF16)
            act = (a * _sigmoid(a) * b).astype(BF16)
            acc = acc + _dot(act, wd_ref[ca, :])
        xo_ref[...] = xf + 0.5 * acc

    return _pallas(
        body, name, (T // tm,),
        [pl.BlockSpec((tm, D), lambda t: (t, 0)), _const((1, D)), _const((D, 2 * FF_PAD)), _const((FF_PAD, D))],
        [pl.BlockSpec((tm, D), lambda t: (t, 0)), pl.BlockSpec((tm, 2 * FF_PAD), lambda t: (t, 0))],
        [jax.ShapeDtypeStruct((T, D), F32), jax.ShapeDtypeStruct((T, 2 * FF_PAD), BF16)],
        (x, g, wu, wd), sem=("arbitrary",), ride=ride)


def _ffn_bwd(x, dy, ab, g, wu, wd, tm, name):
    T = x.shape[0]

    def body(x_ref, dy_ref, ab_ref, g_ref, wu_ref, wd_ref, dx_ref, dab_ref, act_ref, h_ref, dg_ref):
        xf = x_ref[...]
        xhat, r = _rms_stats(xf)
        gain = g_ref[...]
        h_ref[...] = (xhat * gain).astype(BF16)
        dy = dy_ref[...]
        dyh = (0.5 * dy).astype(BF16)
        dh = jnp.zeros((tm, D), F32)
        for j in range(4):
            ca = slice(j * FF_SHARD_PAD, (j + 1) * FF_SHARD_PAD)
            cb = slice(FF_PAD + j * FF_SHARD_PAD, FF_PAD + (j + 1) * FF_SHARD_PAD)
            a = ab_ref[:, ca].astype(F32)
            b = ab_ref[:, cb].astype(F32)
            dact = _dot_nt(dyh, wd_ref[ca, :])
            sg = _sigmoid(a)
            sl = a * sg
            act_ref[:, ca] = (sl * b).astype(BF16)
            da = (dact * b * (sg * (1.0 + a * (1.0 - sg)))).astype(BF16)
            db = (dact * sl).astype(BF16)
            dab_ref[:, ca] = da
            dab_ref[:, cb] = db
            dh = dh + _dot_nt(da, wu_ref[:, ca]) + _dot_nt(db, wu_ref[:, cb])
        dx_ref[...] = dy + _rms_bwd(dh, gain, xhat, r)

        @pl.when(pl.program_id(0) == 0)
        def _():
            dg_ref[...] = jnp.zeros_like(dg_ref)
        dg_ref[...] += jnp.sum(dh * xhat, axis=0, keepdims=True)

    return pl.pallas_call(
        body, grid=(T // tm,), name=name,
        in_specs=[pl.BlockSpec((tm, D), lambda t: (t, 0)), pl.BlockSpec((tm, D), lambda t: (t, 0)),
                  pl.BlockSpec((tm, 2 * FF_PAD), lambda t: (t, 0)), _const((1, D)), _const((D, 2 * FF_PAD)), _const((FF_PAD, D))],
        out_specs=[pl.BlockSpec((tm, D), lambda t: (t, 0)), pl.BlockSpec((tm, 2 * FF_PAD), lambda t: (t, 0)),
                   pl.BlockSpec((tm, FF_PAD), lambda t: (t, 0)), pl.BlockSpec((tm, D), lambda t: (t, 0)),
                   pl.BlockSpec((1, D), lambda t: (0, 0))],
        out_shape=[jax.ShapeDtypeStruct((T, D), F32), jax.ShapeDtypeStruct((T, 2 * FF_PAD), BF16),
                   jax.ShapeDtypeStruct((T, FF_PAD), BF16), jax.ShapeDtypeStruct((T, D), BF16), jax.ShapeDtypeStruct((1, D), F32)],
        compiler_params=_params("arbitrary"),
    )(x, dy, ab, g, wu, wd)


def _tn_matmul(xm, ym, tn, name, scale=None, out_cols=None, col_off=0, prev=None, tt=512, x_part=(0, 1), ride=None):
    T = xm.shape[0]
    xi, xn = x_part
    K = xm.shape[1] // xn
    N = ym.shape[1]
    out_cols = N if out_cols is None else out_cols
    tt = min(tt, T)
    nt = T // tt
    off = col_off // tn

    def body(*refs):
        x_ref, y_ref = refs[0], refs[1]
        o_ref, acc = refs[-2], refs[-1]

        @pl.when(pl.program_id(1) == 0)
        def _():
            acc[...] = jnp.zeros_like(acc)
        acc[...] += _dot_tn(x_ref[...].astype(BF16), y_ref[...].astype(BF16))

        @pl.when(pl.program_id(1) == nt - 1)
        def _():
            res = acc[...]
            o_ref[...] = (res if scale is None else res * scale).astype(BF16)

    in_specs = [pl.BlockSpec((tt, K), lambda n, t: (t, xi)), pl.BlockSpec((tt, tn), lambda n, t: (t, n))]
    args = [xm, ym]
    aliases = {}
    if prev is not None:
        in_specs.append(ANY)
        args.append(prev)
        aliases = {2: 0}
    outs, rode = _pallas(
        body, name, (N // tn, nt), in_specs, [pl.BlockSpec((K, tn), lambda n, t: (0, n + off))],
        [jax.ShapeDtypeStruct((K, out_cols), BF16)], args, scratch_shapes=[pltpu.VMEM((K, tn), F32)],
        sem=("parallel", "arbitrary"), aliases=aliases, ride=ride)
    return outs[0], rode


def _mix_fwd(x, g, w_in, tm, ride=None):
    T = x.shape[0]

    def body(x_ref, g_ref, w_ref, uc_ref, qkv_ref, mq_ref, gl_ref, h_ref):
        xhat, _ = _rms_stats(x_ref[...])
        h = (xhat * g_ref[...]).astype(BF16)
        h_ref[...] = h
        uc_ref[...] = _dot(h, w_ref[:, 0:1024])
        qkv_ref[...] = _dot(h, w_ref[:, 1024:2560]).astype(BF16)
        mq_ref[...] = _dot(h, w_ref[:, 2560:3072]).astype(BF16)
        for j in range(3):
            gl_ref[:, j * D:(j + 1) * D] = _dot(h, w_ref[:, 3072 + j * D:3072 + (j + 1) * D])

    row = lambda w: pl.BlockSpec((tm, w), lambda t: (t, 0))
    return _pallas(
        body, "mix_fwd", (T // tm,), [row(D), _const((1, D)), _const((D, IN_COLS))],
        [row(1024), row(1536), row(512), row(3072), row(D)],
        [jax.ShapeDtypeStruct((T, 1024), F32), jax.ShapeDtypeStruct((T, 1536), BF16), jax.ShapeDtypeStruct((T, 512), BF16),
         jax.ShapeDtypeStruct((T, 3072), F32), jax.ShapeDtypeStruct((T, D), BF16)],
        (x, g, w_in), sem=("parallel",), ride=ride)


def _mix_bwd(x, dres, duc, dqkv, dmq, dgl, g, w_in, tm, ride=None):
    T = x.shape[0]

    def body(x_ref, dres_ref, duc_ref, dqkv_ref, dmq_ref, dgl_ref, g_ref, w_ref, dx_ref, dg_ref):
        xhat, r = _rms_stats(x_ref[...])
        dh = _dot_nt(duc_ref[...], w_ref[:, 0:1024])
        dh = dh + _dot_nt(dqkv_ref[...], w_ref[:, 1024:2560])
        dh = dh + _dot_nt(dmq_ref[...], w_ref[:, 2560:3072])
        dh = dh + _dot_nt(dgl_ref[...], w_ref[:, 3072:6144])
        dx_ref[...] = dres_ref[...] + _rms_bwd(dh, g_ref[...], xhat, r)

        @pl.when(pl.program_id(0) == 0)
        def _():
            dg_ref[...] = jnp.zeros_like(dg_ref)
        dg_ref[...] += jnp.sum(dh * xhat, axis=0, keepdims=True)

    row = lambda w: pl.BlockSpec((tm, w), lambda t: (t, 0))
    return _pallas(
        body, "mix_bwd", (T // tm,),
        [row(D), row(D), row(1024), row(1536), row(512), row(3072), _const((1, D)), _const((D, IN_COLS))],
        [row(D), pl.BlockSpec((1, D), lambda t: (0, 0))],
        [jax.ShapeDtypeStruct((T, D), F32), jax.ShapeDtypeStruct((1, D), F32)],
        (x, dres, duc, dqkv, dmq, dgl, g, w_in), sem=("arbitrary",), ride=ride)


def _conv_taps(win, w_ref, first):
    acc = jnp.zeros((CONV_CHUNK, CONV_W), F32)
    for j in range(CONV_K):
        s = first(j)
        acc = acc + win[s:s + CONV_CHUNK] * w_ref[j:j + 1, :]
    return acc


def _glu_into(uc_ref, vpad, S):
    vpad[pl.ds(0, CONV_HALO), :] = jnp.zeros((CONV_HALO, CONV_W), F32)

    def glu(i, carry):
        r0 = pl.multiple_of(i * CONV_CHUNK, CONV_CHUNK)
        a = uc_ref[0, pl.ds(r0, CONV_CHUNK), 0:CONV_W]
        gt = uc_ref[0, pl.ds(r0, CONV_CHUNK), CONV_W:2 * CONV_W]
        vpad[pl.ds(pl.multiple_of(r0 + CONV_HALO, CONV_HALO), CONV_CHUNK), :] = a * _sigmoid(gt)
        return carry
    lax.fori_loop(0, S // CONV_CHUNK, glu, 0)


def _conv_ln(vpad, w_ref, vec_ref, r0):
    win = vpad[pl.ds(r0, CONV_CHUNK + CONV_HALO), :]
    z = _conv_taps(win, w_ref, lambda j: j + CONV_HALO - (CONV_K - 1)) + vec_ref[0:1, :]
    xc = z - jnp.mean(z, axis=-1, keepdims=True)
    rstd = lax.rsqrt(jnp.mean(xc * xc, axis=-1, keepdims=True) + EPS)
    xn = xc * rstd
    return xn, rstd, xn * vec_ref[1:2, :] + vec_ref[2:3, :]


def _conv_fwd(uc, dw_w, vec):
    NB, S, _ = uc.shape

    def body(uc_ref, w_ref, vec_ref, o_ref, vpad):
        _glu_into(uc_ref, vpad, S)

        def conv(i, carry):
            r0 = pl.multiple_of(i * CONV_CHUNK, CONV_CHUNK)
            _, _, yln = _conv_ln(vpad, w_ref, vec_ref, r0)
            o_ref[0, pl.ds(r0, CONV_CHUNK), :] = (yln * _sigmoid(yln)).astype(BF16)
            return carry
        lax.fori_loop(0, S // CONV_CHUNK, conv, 0)

    return pl.pallas_call(
        body, grid=(NB,), name="conv_fwd",
        in_specs=[pl.BlockSpec((1, S, 2 * CONV_W), lambda b: (b, 0, 0)), _const((CONV_K, CONV_W)), _const((8, CONV_W))],
        out_specs=pl.BlockSpec((1, S, CONV_W), lambda b: (b, 0, 0)),
        out_shape=jax.ShapeDtypeStruct((NB, S, CONV_W), BF16),
        scratch_shapes=[pltpu.VMEM((S + CONV_HALO, CONV_W), F32)],
        compiler_params=_params("parallel"),
    )(uc, dw_w, vec)


def _conv_bwd(uc, dcact, dw_w, vec, ride=None):
    NB, S, _ = uc.shape
    n_chunks = S // CONV_CHUNK

    def body(uc_ref, dc_ref, w_ref, vec_ref, duc_ref, dw_ref, dvec_ref, vpad, dzpad):
        @pl.when(pl.program_id(0) == 0)
        def _():
            dw_ref[...] = jnp.zeros_like(dw_ref)
            dvec_ref[...] = jnp.zeros_like(dvec_ref)
        _glu_into(uc_ref, vpad, S)
        dzpad[pl.ds(S, CONV_HALO), :] = jnp.zeros((CONV_HALO, CONV_W), F32)

        def norm_bwd(i, carry):
            r0 = pl.multiple_of(i * CONV_CHUNK, CONV_CHUNK)
            xn, rstd, yln = _conv_ln(vpad, w_ref, vec_ref, r0)
            sg = _sigmoid(yln)
            dyln = dc_ref[0, pl.ds(r0, CONV_CHUNK), :] * (sg * (1.0 + yln * (1.0 - sg)))
            dxn = dyln * vec_ref[1:2, :]
            dz = rstd * (dxn - jnp.mean(dxn, axis=-1, keepdims=True) - xn * jnp.mean(dxn * xn, axis=-1, keepdims=True))
            dzpad[pl.ds(r0, CONV_CHUNK), :] = dz
            dvec_ref[0:1, :] += jnp.sum(dz, axis=0, keepdims=True)
            dvec_ref[1:2, :] += jnp.sum(dyln * xn, axis=0, keepdims=True)
            dvec_ref[2:3, :] += jnp.sum(dyln, axis=0, keepdims=True)
            return carry
        lax.fori_loop(0, n_chunks, norm_bwd, 0)

        def taps_bwd(i, carry):
            r0 = pl.multiple_of(i * CONV_CHUNK, CONV_CHUNK)
            dzwin = dzpad[pl.ds(r0, CONV_CHUNK + CONV_HALO), :]
            dv = _conv_taps(dzwin, w_ref, lambda j: CONV_K - 1 - j)
            vwin = vpad[pl.ds(r0, CONV_CHUNK + CONV_HALO), :]
            dz = dzwin[0:CONV_CHUNK]
            for j in range(CONV_K):
                s = j + CONV_HALO - (CONV_K - 1)
                dw_ref[j:j + 1, :] += jnp.sum(dz * vwin[s:s + CONV_CHUNK], axis=0, keepdims=True)
            a = uc_ref[0, pl.ds(r0, CONV_CHUNK), 0:CONV_W]
            sg = _sigmoid(uc_ref[0, pl.ds(r0, CONV_CHUNK), CONV_W:2 * CONV_W])
            duc_ref[0, pl.ds(r0, CONV_CHUNK), 0:CONV_W] = (dv * sg).astype(BF16)
            duc_ref[0, pl.ds(r0, CONV_CHUNK), CONV_W:2 * CONV_W] = (dv * a * sg * (1.0 - sg)).astype(BF16)
            return carry
        lax.fori_loop(0, n_chunks, taps_bwd, 0)

    return _pallas(
        body, "conv_bwd", (NB,),
        [pl.BlockSpec((1, S, 2 * CONV_W), lambda b: (b, 0, 0)), pl.BlockSpec((1, S, CONV_W), lambda b: (b, 0, 0)),
         _const((CONV_K, CONV_W)), _const((8, CONV_W))],
        [pl.BlockSpec((1, S, 2 * CONV_W), lambda b: (b, 0, 0)), pl.BlockSpec((32, CONV_W), lambda b: (0, 0)),
         pl.BlockSpec((8, CONV_W), lambda b: (0, 0))],
        [jax.ShapeDtypeStruct((NB, S, 2 * CONV_W), BF16), jax.ShapeDtypeStruct((32, CONV_W), F32),
         jax.ShapeDtypeStruct((8, CONV_W), F32)],
        (uc, dcact, dw_w, vec),
        scratch_shapes=[pltpu.VMEM((S + CONV_HALO, CONV_W), F32), pltpu.VMEM((S + CONV_HALO, CONV_W), F32)],
        sem=("arbitrary",), ride=ride)


def _rel_index_of_column(cols):
    offset = jnp.where(cols < KWIN, cols, cols - DS_LANES)
    return jnp.clip(KPAD - offset, -(CHUNK - 1), MAX_REL) + (CHUNK - 1)


def _bias_table(rel_bias):
    def body(rb_ref, o_ref, by_offset):
        ridx = _rel_index_of_column(lax.broadcasted_iota(jnp.int32, (1, DS_LANES), 1))
        onehot = (ridx == lax.broadcasted_iota(jnp.int32, (N_REL, 1), 0)).astype(F32)
        by_offset[...] = jnp.dot(rb_ref[...], onehot, preferred_element_type=F32, precision=lax.Precision.HIGHEST)
        sub = lax.broadcasted_iota(jnp.int32, (8, 1), 0)
        kchunk = lax.broadcasted_iota(jnp.int32, (1, KWIN), 1) // CHUNK
        for head in range(ATT_HEADS):
            base = jnp.broadcast_to(by_offset[head:head + 1, :], (8, DS_LANES))
            rows = base
            for s in range(1, 8):
                rows = jnp.where(sub == s, pltpu.roll(base, s, 1), rows)

            def rows8(q8, carry):
                qchunk = (q8 * 8 + sub) // CHUNK
                tile = pltpu.roll(rows, q8 * 8, 1)[:, 0:KWIN]
                band = (kchunk >= qchunk) & (kchunk <= qchunk + LEFT_CHUNKS)
                o_ref[head, pl.ds(pl.multiple_of(q8 * 8, 8), 8), :] = jnp.where(band, tile, MASK_VALUE)
                return carry
            lax.fori_loop(0, QB // 8, rows8, 0)

    return pl.pallas_call(body, out_shape=jax.ShapeDtypeStruct((ATT_HEADS, QB, KWIN), F32), name="bias_table",
                          scratch_shapes=[pltpu.VMEM((ATT_HEADS, DS_LANES), F32)], compiler_params=_params())(rel_bias)


def _load_keys(i, k_ref, v_ref, kpad, vpad, S):
    @pl.when(i == 0)
    def _():
        kpad[pl.ds(0, KPAD), :] = jnp.zeros((KPAD, ATT_W), BF16)
        vpad[pl.ds(0, KPAD), :] = jnp.zeros((KPAD, ATT_W), BF16)
        kpad[pl.ds(KPAD, S), :] = k_ref[0]
        vpad[pl.ds(KPAD, S), :] = v_ref[0]


def _att_probs(q2, k2, tab_ref, head, in_head, in_seq):
    qm = jnp.where(in_head, q2, jnp.zeros_like(q2))
    s = _dot_nt(qm, k2) * (ATT_HD ** -0.5) + tab_ref[head]
    s = jnp.where(in_seq, s, MASK_VALUE)
    e = jnp.exp(s - jnp.max(s, axis=-1, keepdims=True))
    return e * (1.0 / jnp.sum(e, axis=-1, keepdims=True))


def _att_fwd(qkv, tab):
    NB, S, _ = qkv.shape

    def body(q_ref, k_ref, v_ref, tab_ref, o_ref, kpad, vpad):
        i = pl.program_id(1)
        _load_keys(i, k_ref, v_ref, kpad, vpad, S)
        koff = pl.multiple_of(i * QB, QB)
        lane = lax.broadcasted_iota(jnp.int32, (1, 128), 1)
        in_seq = (lax.broadcasted_iota(jnp.int32, (1, KWIN), 1) + i * QB) >= KPAD
        for pair in range(ATT_HEADS // 2):
            cols = slice(pair * 128, (pair + 1) * 128)
            q2 = q_ref[0, :, cols]
            k2 = kpad[pl.ds(koff, KWIN), cols]
            v2 = vpad[pl.ds(koff, KWIN), cols]
            o2 = jnp.zeros((QB, 128), F32)
            for hh in range(2):
                in_head = (lane // ATT_HD) == hh
                p = _att_probs(q2, k2, tab_ref, 2 * pair + hh, in_head, in_seq)
                o2 = jnp.where(in_head, _dot(p.astype(BF16), v2), o2)
            o_ref[0, :, cols] = o2.astype(BF16)

    seq = lambda col: pl.BlockSpec((1, S, ATT_W), lambda b, i: (b, 0, col), pipeline_mode=pl.Buffered(1))
    return pl.pallas_call(
        body, grid=(NB, S // QB), name="att_fwd",
        in_specs=[pl.BlockSpec((1, QB, ATT_W), lambda b, i: (b, i, 0)), seq(1), seq(2), _const((ATT_HEADS, QB, KWIN))],
        out_specs=pl.BlockSpec((1, QB, ATT_W), lambda b, i: (b, i, 0)),
        out_shape=jax.ShapeDtypeStruct((NB, S, ATT_W), BF16),
        scratch_shapes=[pltpu.VMEM((S + KPAD, ATT_W), BF16), pltpu.VMEM((S + KPAD, ATT_W), BF16)],
        compiler_params=_params("arbitrary", "arbitrary"),
    )(qkv, qkv, qkv, tab)


def _att_bwd(qkv, do, tab, ride=None):
    NB, S, _ = qkv.shape
    nq = S // QB

    def body(q_ref, k_ref, v_ref, do_ref, tab_ref, dqkv_ref, ds_hbm, kpad, vpad, dkpad, dvpad, ds_acc, ds_sem):
        b, i = pl.program_id(0), pl.program_id(1)
        _load_keys(i, k_ref, v_ref, kpad, vpad, S)

        @pl.when(i == 0)
        def _():
            dkpad[...] = jnp.zeros_like(dkpad)
            dvpad[...] = jnp.zeros_like(dvpad)

        @pl.when((i == 0) & (b == 0))
        def _():
            ds_acc[...] = jnp.zeros_like(ds_acc)

        koff = pl.multiple_of(i * QB, QB)
        lane = lax.broadcasted_iota(jnp.int32, (1, 128), 1)
        in_seq = (lax.broadcasted_iota(jnp.int32, (1, KWIN), 1) + i * QB) >= KPAD
        for pair in range(ATT_HEADS // 2):
            cols = slice(pair * 128, (pair + 1) * 128)
            q2 = q_ref[0, :, cols]
            do2 = do_ref[0, :, cols]
            k2 = kpad[pl.ds(koff, KWIN), cols]
            v2 = vpad[pl.ds(koff, KWIN), cols]
            dq2 = jnp.zeros((QB, 128), F32)
            dk2 = jnp.zeros((KWIN, 128), F32)
            dv2 = jnp.zeros((KWIN, 128), F32)
            for hh in range(2):
                head = 2 * pair + hh
                in_head = (lane // ATT_HD) == hh
                p = _att_probs(q2, k2, tab_ref, head, in_head, in_seq)
                dom = jnp.where(in_head, do2, jnp.zeros_like(do2))
                dp = _dot_nt(dom, v2)
                ds = p * (dp - jnp.sum(p * dp, axis=-1, keepdims=True))
                ds_acc[head] += ds
                dss = (ds * (ATT_HD ** -0.5)).astype(BF16)
                dq2 = jnp.where(in_head, _dot(dss, k2), dq2)
                dk2 = jnp.where(in_head, _dot_tn(dss, q2), dk2)
                dv2 = jnp.where(in_head, _dot_tn(p.astype(BF16), do2), dv2)
            dqkv_ref[0, pl.ds(koff, QB), cols] = dq2.astype(BF16)
            dkpad[pl.ds(koff, KWIN), cols] += dk2
            dvpad[pl.ds(koff, KWIN), cols] += dv2

        @pl.when(i == nq - 1)
        def _():
            dqkv_ref[0, :, ATT_W:2 * ATT_W] = dkpad[pl.ds(KPAD, S), :].astype(BF16)
            dqkv_ref[0, :, 2 * ATT_W:3 * ATT_W] = dvpad[pl.ds(KPAD, S), :].astype(BF16)

        @pl.when((i == nq - 1) & (b == NB - 1))
        def _():
            out = pltpu.make_async_copy(ds_acc, ds_hbm, ds_sem)
            out.start()
            out.wait()

    seq = lambda col: pl.BlockSpec((1, S, ATT_W), lambda b, i: (b, 0, col), pipeline_mode=pl.Buffered(1))
    return _pallas(
        body, "att_bwd", (NB, nq),
        [pl.BlockSpec((1, QB, ATT_W), lambda b, i: (b, i, 0)), seq(1), seq(2),
         pl.BlockSpec((1, QB, ATT_W), lambda b, i: (b, i, 0)), _const((ATT_HEADS, QB, KWIN))],
        [pl.BlockSpec((1, S, 3 * ATT_W), lambda b, i: (b, 0, 0)), ANY],
        [jax.ShapeDtypeStruct((NB, S, 3 * ATT_W), BF16), jax.ShapeDtypeStruct((ATT_HEADS, QB, KWIN), F32)],
        (qkv, qkv, qkv, do, tab),
        scratch_shapes=[pltpu.VMEM((S + KPAD, ATT_W), BF16), pltpu.VMEM((S + KPAD, ATT_W), BF16),
                        pltpu.VMEM((S + KPAD, ATT_W), F32), pltpu.VMEM((S + KPAD, ATT_W), F32),
                        pltpu.VMEM((ATT_HEADS, QB, KWIN), F32), pltpu.SemaphoreType.DMA],
        sem=("arbitrary", "arbitrary"), ride=ride)


def _rel_bias_grad(ds):
    def body(ds_ref, o_ref):
        sub = lax.broadcasted_iota(jnp.int32, (8, 1), 0)
        ridx = _rel_index_of_column(lax.broadcasted_iota(jnp.int32, (DS_LANES, 1), 0))
        onehot = (ridx == lax.broadcasted_iota(jnp.int32, (1, N_REL), 1)).astype(F32)
        for head in range(ATT_HEADS):
            def rows8(q8, acc):
                tile = ds_ref[head, pl.ds(pl.multiple_of(q8 * 8, 8), 8), :]
                tile = jnp.concatenate([tile, jnp.zeros((8, DS_LANES - KWIN), F32)], axis=1)
                return acc + pltpu.roll(tile, lax.rem(DS_LANES - q8 * 8, DS_LANES), 1)
            acc = lax.fori_loop(0, QB // 8, rows8, jnp.zeros((8, DS_LANES), F32))
            diag = jnp.zeros((8, DS_LANES), F32)
            for s in range(8):
                shifted = acc if s == 0 else pltpu.roll(acc, DS_LANES - s, 1)
                diag = jnp.where(sub == s, shifted, diag)
            z = jnp.sum(diag, axis=0, keepdims=True)
            o_ref[head:head + 1, :] = jnp.dot(z, onehot, preferred_element_type=F32, precision=lax.Precision.HIGHEST)

    return pl.pallas_call(body, out_shape=jax.ShapeDtypeStruct((ATT_HEADS, N_REL), F32), name="rel_bias_grad",
                          compiler_params=_params())(ds)


def _memkv_fwd(mem, g, w_kv, tm):
    R = mem.shape[0]
    tm = min(tm, R)

    def body(m_ref, g_ref, w_ref, h_ref, kv_ref):
        xhat, _ = _rms_stats(m_ref[...])
        h = (xhat * g_ref[...]).astype(BF16)
        h_ref[...] = h
        kv_ref[...] = _dot(h, w_ref[...]).astype(BF16)

    row = pl.BlockSpec((tm, D), lambda t: (t, 0))
    return pl.pallas_call(
        body, grid=(R // tm,), name="memkv_fwd", in_specs=[row, _const((1, D)), _const((D, 2 * MEM_W))], out_specs=[row, row],
        out_shape=[jax.ShapeDtypeStruct((R, D), BF16), jax.ShapeDtypeStruct((R, 2 * MEM_W), BF16)],
        compiler_params=_params("parallel"),
    )(mem, g, w_kv)


def _memkv_bwd(mem, dkv, w_kv, tm):
    R = mem.shape[0]
    tm = min(tm, R)

    def body(m_ref, dkv_ref, w_ref, dg_ref):
        xhat, _ = _rms_stats(m_ref[...])
        dh = _dot_nt(dkv_ref[...].astype(BF16), w_ref[...])

        @pl.when(pl.program_id(0) == 0)
        def _():
            dg_ref[...] = jnp.zeros_like(dg_ref)
        dg_ref[...] += jnp.sum(dh * xhat, axis=0, keepdims=True)

    row = pl.BlockSpec((tm, D), lambda t: (t, 0))
    return pl.pallas_call(
        body, grid=(R // tm,), name="memkv_bwd", in_specs=[row, row, _const((D, 2 * MEM_W))],
        out_specs=pl.BlockSpec((1, D), lambda t: (0, 0)), out_shape=jax.ShapeDtypeStruct((1, D), F32),
        compiler_params=_params("arbitrary"),
    )(mem, dkv, w_kv)


def _mem_probs(qh, kh):
    s = _dot_nt(qh, kh) * (MEM_HD ** -0.5)
    e = jnp.exp(s - jnp.max(s, axis=-1, keepdims=True))
    return e * (1.0 / jnp.sum(e, axis=-1, keepdims=True))


def _mematt_fwd(mq, kv, tq):
    NB, S, _ = mq.shape
    M = kv.shape[1]

    def body(q_ref, kv_ref, o_ref):
        for h in range(MEM_HEADS):
            cols = slice(h * MEM_HD, (h + 1) * MEM_HD)
            p = _mem_probs(q_ref[0, :, cols], kv_ref[0, :, cols])
            o_ref[0, :, cols] = _dot(p.astype(BF16), kv_ref[0, :, MEM_W + h * MEM_HD:MEM_W + (h + 1) * MEM_HD]).astype(BF16)

    return pl.pallas_call(
        body, grid=(NB, S // tq), name="mematt_fwd",
        in_specs=[pl.BlockSpec((1, tq, MEM_W), lambda b, i: (b, i, 0)), pl.BlockSpec((1, M, 2 * MEM_W), lambda b, i: (b, 0, 0))],
        out_specs=pl.BlockSpec((1, tq, MEM_W), lambda b, i: (b, i, 0)),
        out_shape=jax.ShapeDtypeStruct((NB, S, MEM_W), BF16), compiler_params=_params("parallel", "parallel"),
    )(mq, kv)


def _mematt_bwd(mq, kv, do, tq):
    NB, S, _ = mq.shape
    M = kv.shape[1]

    def body(q_ref, kv_ref, do_ref, dq_ref, dkv_ref):
        @pl.when(pl.program_id(1) == 0)
        def _():
            dkv_ref[...] = jnp.zeros_like(dkv_ref)
        for h in range(MEM_HEADS):
            cols = slice(h * MEM_HD, (h + 1) * MEM_HD)
            vcols = slice(MEM_W + h * MEM_HD, MEM_W + (h + 1) * MEM_HD)
            qh, kh, vh, doh = q_ref[0, :, cols], kv_ref[0, :, cols], kv_ref[0, :, vcols], do_ref[0, :, cols]
            p = _mem_probs(qh, kh)
            dp = _dot_nt(doh, vh)
            ds = p * (dp - jnp.sum(p * dp, axis=-1, keepdims=True))
            dss = (ds * (MEM_HD ** -0.5)).astype(BF16)
            dq_ref[0, :, cols] = _dot(dss, kh).astype(BF16)
            dkv_ref[0, :, cols] += _dot_tn(dss, qh)
            dkv_ref[0, :, vcols] += _dot_tn(p.astype(BF16), doh)

    qspec = pl.BlockSpec((1, tq, MEM_W), lambda b, i: (b, i, 0))
    kvspec = pl.BlockSpec((1, M, 2 * MEM_W), lambda b, i: (b, 0, 0))
    return pl.pallas_call(
        body, grid=(NB, S // tq), name="mematt_bwd", in_specs=[qspec, kvspec, qspec], out_specs=[qspec, kvspec],
        out_shape=[jax.ShapeDtypeStruct((NB, S, MEM_W), BF16), jax.ShapeDtypeStruct((NB, M, 2 * MEM_W), F32)],
        compiler_params=_params("arbitrary", "arbitrary"),
    )(mq, kv, do)


def _branches(c_ref, a_ref, m_ref, gl_ref, bg_ref, wpw_ref, wo_ref, wmo_ref):
    ys = [_dot(c_ref[...], wpw_ref[...]), _dot(a_ref[...], wo_ref[...]), _dot(m_ref[...], wmo_ref[...])]
    gates = [_sigmoid(gl_ref[:, j * D:(j + 1) * D] + bg_ref[:, j * D:(j + 1) * D]) for j in range(3)]
    return ys, gates


def _combine_fwd(x, cact, oatt, omem, gl, bg, wpw, wo, wmo, wout, tm):
    T = x.shape[0]

    def body(x_ref, c_ref, a_ref, m_ref, gl_ref, bg_ref, wpw_ref, wo_ref, wmo_ref, wout_ref, xo_ref, y_ref):
        ys, gates = _branches(c_ref, a_ref, m_ref, gl_ref, bg_ref, wpw_ref, wo_ref, wmo_ref)
        y = (gates[0] * ys[0] + gates[1] * ys[1] + gates[2] * ys[2]).astype(BF16)
        y_ref[...] = y
        xo_ref[...] = x_ref[...] + _dot(y, wout_ref[...])

    row = lambda w: pl.BlockSpec((tm, w), lambda t: (t, 0))
    wbr = _const((512, D))
    return pl.pallas_call(
        body, grid=(T // tm,), name="combine_fwd",
        in_specs=[row(D), row(512), row(512), row(512), row(3 * D), _const((1, 3 * D)), wbr, wbr, wbr, _const((D, D))],
        out_specs=[row(D), row(D)],
        out_shape=[jax.ShapeDtypeStruct((T, D), F32), jax.ShapeDtypeStruct((T, D), BF16)],
        compiler_params=_params("parallel"),
    )(x, cact, oatt, omem, gl, bg, wpw, wo, wmo, wout)


def _combine_bwd(dx, cact, oatt, omem, gl, bg, wpw, wo, wmo, wout, tm, ride=None):
    T = dx.shape[0]

    def body(dx_ref, c_ref, a_ref, m_ref, gl_ref, bg_ref, wpw_ref, wo_ref, wmo_ref, wout_ref,
             dgl_ref, dc_ref, da_ref, dm_ref, dyc_ref, dya_ref, dym_ref, dbg_ref):
        ys, gates = _branches(c_ref, a_ref, m_ref, gl_ref, bg_ref, wpw_ref, wo_ref, wmo_ref)
        dy = _dot_nt(dx_ref[...].astype(BF16), wout_ref[...])

        @pl.when(pl.program_id(0) == 0)
        def _():
            dbg_ref[...] = jnp.zeros_like(dbg_ref)
        dyb = []
        for j in range(3):
            dlogit = dy * ys[j] * gates[j] * (1.0 - gates[j])
            dgl_ref[:, j * D:(j + 1) * D] = dlogit.astype(BF16)
            dbg_ref[:, j * D:(j + 1) * D] += jnp.sum(dlogit, axis=0, keepdims=True)
            dyb.append((dy * gates[j]).astype(BF16))
        dyc_ref[...], dya_ref[...], dym_ref[...] = dyb
        dc_ref[...] = _dot_nt(dyb[0], wpw_ref[...])
        da_ref[...] = _dot_nt(dyb[1], wo_ref[...]).astype(BF16)
        dm_ref[...] = _dot_nt(dyb[2], wmo_ref[...]).astype(BF16)

    row = lambda w: pl.BlockSpec((tm, w), lambda t: (t, 0))
    wbr = _const((512, D))
    sds = jax.ShapeDtypeStruct
    return _pallas(
        body, "combine_bwd", (T // tm,),
        [row(D), row(512), row(512), row(512), row(3 * D), _const((1, 3 * D)), wbr, wbr, wbr, _const((D, D))],
        [row(3 * D), row(512), row(512), row(512), row(D), row(D), row(D), pl.BlockSpec((1, 3 * D), lambda t: (0, 0))],
        [sds((T, 3 * D), BF16), sds((T, 512), F32), sds((T, 512), BF16), sds((T, 512), BF16),
         sds((T, D), BF16), sds((T, D), BF16), sds((T, D), BF16), sds((1, 3 * D), F32)],
        (dx, cact, oatt, omem, gl, bg, wpw, wo, wmo, wout), sem=("arbitrary",), ride=ride)


def _final(x, g, target, tm):
    T = x.shape[0]

    def body(x_ref, g_ref, t_ref, loss_ref, dx_ref, dg_ref):
        xhat, r = _rms_stats(x_ref[...])
        gain = g_ref[...]
        diff = xhat * gain - t_ref[...]
        dout = diff * (1.0 / D)

        @pl.when(pl.program_id(0) == 0)
        def _():
            loss_ref[...] = jnp.zeros_like(loss_ref)
            dg_ref[...] = jnp.zeros_like(dg_ref)
        sq = jnp.sum(jnp.sum(diff * diff, axis=0, keepdims=True), axis=1, keepdims=True)
        loss_ref[...] += jnp.broadcast_to(sq * (0.5 / D), (1, 128))
        dg_ref[...] += jnp.sum(dout * xhat, axis=0, keepdims=True)
        dx_ref[...] = _rms_bwd(dout, gain, xhat, r)

    row = pl.BlockSpec((tm, D), lambda t: (t, 0))
    return pl.pallas_call(
        body, grid=(T // tm,), name="final_loss", in_specs=[row, _const((1, D)), row],
        out_specs=[pl.BlockSpec((1, 128), lambda t: (0, 0)), row, pl.BlockSpec((1, D), lambda t: (0, 0))],
        out_shape=[jax.ShapeDtypeStruct((1, 128), F32), jax.ShapeDtypeStruct((T, D), F32), jax.ShapeDtypeStruct((1, D), F32)],
        compiler_params=_params("arbitrary"),
    )(x, g, target)


def _peer(x, y, c, rel):
    rx, ry, rc = (rel >> 2) & 1, (rel >> 1) & 1, rel & 1
    return ((1 - x) if rx else x, (1 - y) if ry else y, (1 - c) if rc else c)


def _all_sum_small(part):
    def body(p_ref, o_ref, slots, send_sems, recv_sems):
        x, y, c = _my_coords()
        me = _dev_index(x, y, c)
        slots[me] = p_ref[...]
        copies = []
        for rel in range(1, NDEV):
            cp = pltpu.make_async_remote_copy(src_ref=p_ref, dst_ref=slots.at[me], send_sem=send_sems.at[rel - 1],
                                              recv_sem=recv_sems.at[rel - 1], device_id=_peer(x, y, c, rel), device_id_type=MESH)
            cp.start()
            copies.append(cp)
        for rel in range(1, NDEV):
            src_dev = _dev_index(*_peer(x, y, c, rel))
            pltpu.make_async_remote_copy(src_ref=p_ref, dst_ref=slots.at[src_dev], send_sem=send_sems.at[rel - 1],
                                         recv_sem=recv_sems.at[rel - 1], device_id=_peer(x, y, c, rel), device_id_type=MESH).wait_recv()
        for cp in copies:
            cp.wait_send()
        total = slots[0]
        for d in range(1, NDEV):
            total = total + slots[d]
        o_ref[...] = total

    return pl.pallas_call(
        body, out_shape=jax.ShapeDtypeStruct(part.shape, F32), name="all_sum_small",
        in_specs=[pl.BlockSpec(memory_space=pltpu.VMEM)], out_specs=pl.BlockSpec(memory_space=pltpu.VMEM),
        scratch_shapes=[pltpu.VMEM((NDEV,) + part.shape, F32), pltpu.SemaphoreType.DMA((NDEV - 1,)), pltpu.SemaphoreType.DMA((NDEV - 1,))],
        compiler_params=pltpu.CompilerParams(has_side_effects=True),
    )(part)


def _adamw_math(w, g, m, v):
    m = ADAM_B1 * m + (1.0 - ADAM_B1) * g
    v = ADAM_B2 * v + (1.0 - ADAM_B2) * (g * g)
    m_hat = m / (1.0 - ADAM_B1 ** ADAM_STEP)
    v_hat = v / (1.0 - ADAM_B2 ** ADAM_STEP)
    delta = -ADAM_LR * (m_hat / (jnp.sqrt(v_hat) + ADAM_EPS) + ADAM_WD * w)
    return delta, m, v


def _sum_adamw(parts, w, m, v, name):
    R, C = w.shape
    n_parts = len(parts)
    rows = R // n_parts
    Cp = parts[0].shape[2]
    tr = max(t for t in range(8, 257, 8) if rows % t == 0)
    per = rows // tr

    def body(*refs):
        p_refs = refs[:n_parts]
        w_ref, m_ref, v_ref, g_ref, d_ref, mo_ref, vo_ref = refs[n_parts:]
        t = pl.program_id(0)
        for k, p_ref in enumerate(p_refs):
            @pl.when((t >= k * per) & (t < (k + 1) * per))
            def _():
                g = p_ref[0, :, pl.ds(0, C)].astype(F32)
                for d in range(1, NDEV):
                    g = g + p_ref[d, :, pl.ds(0, C)].astype(F32)
                g_ref[...] = g
                d_ref[...], mo_ref[...], vo_ref[...] = _adamw_math(w_ref[...], g, m_ref[...], v_ref[...])

    def part_spec(k):
        return pl.BlockSpec((NDEV, tr, Cp), lambda t: (0, jnp.clip(t - k * per, 0, per - 1), 0))

    row = pl.BlockSpec((tr, C), lambda t: (t, 0))
    return pl.pallas_call(
        body, grid=(R // tr,), name=name, in_specs=[part_spec(k) for k in range(n_parts)] + [row, row, row],
        out_specs=[row] * 4, out_shape=[jax.ShapeDtypeStruct((R, C), F32)] * 4, compiler_params=_params("parallel"),
    )(*parts, w, m, v)


def _adamw_small(w, g, m, v, name):
    def body(w_ref, g_ref, m_ref, v_ref, d_ref, mo_ref, vo_ref):
        d_ref[...], mo_ref[...], vo_ref[...] = _adamw_math(w_ref[...], g_ref[...], m_ref[...], v_ref[...])
    return pl.pallas_call(body, out_shape=[jax.ShapeDtypeStruct(w.shape, F32)] * 3, name=name, compiler_params=_params())(w, g, m, v)


def _pack_small(parts):
    rows = [jnp.reshape(parts[name], (size // 128, 128)) for name, size in SMALL]
    used = sum(size // 128 for _, size in SMALL)
    rows.append(jnp.zeros((SMALL_ROWS - used, 128), F32))
    return jnp.concatenate(rows, axis=0)


def _unpack_small(packed):
    out, r = {}, 0
    for name, size in SMALL:
        out[name] = packed[r:r + size // 128]
        r += size // 128
    return out


def kernel(x, mem, ffn1_norm, ffn1_w_up, ffn1_w_down, mix_norm, mem_norm, w_in, b_gate, conv_dw_w, conv_dw_b, conv_ln_g, conv_ln_b, conv_w_pw, att_rel_bias, att_w_o, mem_w_kv, mem_w_o, w_out, ffn2_norm, ffn2_w_up, ffn2_w_down, final_norm, loss_target, m_ffn1_norm, m_ffn1_w_up, m_ffn1_w_down, m_mix_norm, m_mem_norm, m_w_in, m_b_gate, m_conv_dw_w, m_conv_dw_b, m_conv_ln_g, m_conv_ln_b, m_conv_w_pw, m_att_rel_bias, m_att_w_o, m_mem_w_kv, m_mem_w_o, m_w_out, m_ffn2_norm, m_ffn2_w_up, m_ffn2_w_down, m_final_norm, v_ffn1_norm, v_ffn1_w_up, v_ffn1_w_down, v_mix_norm, v_mem_norm, v_w_in, v_b_gate, v_conv_dw_w, v_conv_dw_b, v_conv_ln_g, v_conv_ln_b, v_conv_w_pw, v_att_rel_bias, v_att_w_o, v_mem_w_kv, v_mem_w_o, v_w_out, v_ffn2_norm, v_ffn2_w_up, v_ffn2_w_down, v_final_norm):
    given = dict(locals())
    w = {n: given[n] for n in WEIGHTS}
    mom = {n: given["m_" + n] for n in WEIGHTS}
    var = {n: given["v_" + n] for n in WEIGHTS}

    NB, S, _ = x.shape
    T = NB * S
    ML = mem.shape[1]
    x0 = x.reshape(T, D)
    target = loss_target.reshape(T, D)
    mem2 = mem.reshape(NB * ML, D)

    sh = dict(zip(BIG_ORDER, _cast_shards([w[n][0] for n in BIG_ORDER])))
    zeros_pad = jnp.zeros((FF_SHARD_PAD - 2 * FF_HALF_ROWS, D), BF16)
    dw_t = jnp.transpose(conv_dw_w[0])

    def gather(names, extra=(), extra_kinds=()):
        return _gather_ride([sh[n] for n in names] + list(extra), [BIG[n] for n in names] + list(extra_kinds), zeros_pad)

    W = {}
    names0 = ['ffn1_w_up', 'ffn1_w_down']
    got = _exchange_alone(gather(names0, [dw_t], [('row', dw_t.shape[0])]), "gather_ffn1")
    W.update(zip(names0, got[:2]))
    dw_full = jnp.transpose(got[2])
    conv_vec = jnp.concatenate([conv_dw_b, conv_ln_g, conv_ln_b, jnp.zeros((5, CONV_W), F32)], axis=0)
    tab = _bias_table(att_rel_bias[0])
    fin_g = final_norm.reshape(1, D)

    names1 = ['w_in', 'conv_w_pw', 'att_w_o', 'mem_w_kv', 'mem_w_o', 'w_out']
    (x1, ab1), got = _ffn_fwd(x0, ffn1_norm, W['ffn1_w_up'], W['ffn1_w_down'], TILE_FFN, "ffn1_fwd", ride=gather(names1))
    W.update(zip(names1, got))
    names2 = ['ffn2_w_up', 'ffn2_w_down']
    (uc, qkv, mq, gl, hmix), got = _mix_fwd(x1, mix_norm, W['w_in'], TILE_TOKENS, ride=gather(names2))
    W.update(zip(names2, got))
    uc3 = uc.reshape(NB, S, 2 * CONV_W)
    qkv3 = qkv.reshape(NB, S, 3 * ATT_W)
    mq3 = mq.reshape(NB, S, MEM_W)
    cact = _conv_fwd(uc3, dw_full, conv_vec).reshape(T, CONV_W)
    oatt = _att_fwd(qkv3, tab).reshape(T, ATT_W)
    memh, kv = _memkv_fwd(mem2, mem_norm, W['mem_w_kv'], TILE_TOKENS)
    kv3 = kv.reshape(NB, ML, 2 * MEM_W)
    omem = _mematt_fwd(mq3, kv3, TILE_TOKENS).reshape(T, MEM_W)
    branch_w = (W['conv_w_pw'], W['att_w_o'], W['mem_w_o'], W['w_out'])
    x2, ymix = _combine_fwd(x1, cact, oatt, omem, gl, b_gate, *branch_w, TILE_COMBINE)
    (x3, ab2), _ = _ffn_fwd(x2, ffn2_norm, W['ffn2_w_up'], W['ffn2_w_down'], TILE_FFN, "ffn2_fwd")
    loss_part, dx3, dg_final = _final(x3, fin_g, target, TILE_TOKENS)

    def scatter(grads, names):
        return _scatter_ride(grads, [BIG[n] for n in names])

    G, P = {}, {}
    dx2, dab2, act2, h2, dg_ffn2 = _ffn_bwd(x2, dx3, ab2, ffn2_norm, W['ffn2_w_up'], W['ffn2_w_down'], TILE_FFN, "ffn2_bwd")
    G['ffn2_w_up'], _ = _tn_matmul(h2, dab2, 768, "grad_ffn2_w_up")
    G['ffn2_w_down'], _ = _tn_matmul(act2, dx3, 512, "grad_ffn2_w_down", scale=0.5)
    (dgl, dcact, doatt, domem, dyc, dya, dym, dbg), got = _combine_bwd(
        dx2, cact, oatt, omem, gl, b_gate, *branch_w, TILE_COMBINE, ride=scatter([G['ffn2_w_up']], ['ffn2_w_up']))
    P['ffn2_w_up'] = got
    G['w_out'], _ = _tn_matmul(ymix, dx2, 512, "grad_w_out")
    G['conv_w_pw'], _ = _tn_matmul(cact, dyc, 512, "grad_conv_w_pw")
    G['att_w_o'], _ = _tn_matmul(oatt, dya, 512, "grad_att_w_o")
    G['mem_w_o'], _ = _tn_matmul(omem, dym, 512, "grad_mem_w_o")
    dmq3, dkv3 = _mematt_bwd(mq3, kv3, domem.reshape(NB, S, MEM_W), TILE_TOKENS)
    dkv = dkv3.reshape(NB * ML, 2 * MEM_W)
    dg_mem = _memkv_bwd(mem2, dkv, W['mem_w_kv'], TILE_TOKENS)
    G['mem_w_kv'], _ = _tn_matmul(memh, dkv, 512, "grad_mem_w_kv")
    names = ['ffn2_w_down', 'w_out', 'conv_w_pw', 'att_w_o', 'mem_w_o']
    (dqkv3, dscore), got = _att_bwd(qkv3, doatt.reshape(NB, S, ATT_W), tab, ride=scatter([G[n] for n in names], names))
    P.update((n, [p]) for n, p in zip(names, got))
    d_rel = _rel_bias_grad(dscore)
    (duc3, d_dw, d_cvec), got = _conv_bwd(uc3, dcact.reshape(NB, S, CONV_W), dw_full, conv_vec,
                                          ride=scatter([G['mem_w_kv']], ['mem_w_kv']))
    P['mem_w_kv'] = got
    duc, dqkv, dmq = duc3.reshape(T, 2 * CONV_W), dqkv3.reshape(T, 3 * ATT_W), dmq3.reshape(T, MEM_W)
    g_in, _ = _tn_matmul(hmix, duc, 512, "grad_w_in_conv", out_cols=IN_COLS, col_off=0)
    g_in, _ = _tn_matmul(hmix, dqkv, 512, "grad_w_in_qkv", out_cols=IN_COLS, col_off=1024, prev=g_in)
    g_in, _ = _tn_matmul(hmix, dmq, 512, "grad_w_in_mq", out_cols=IN_COLS, col_off=2560, prev=g_in)
    G['w_in'], _ = _tn_matmul(hmix, dgl, 512, "grad_w_in_gate", out_cols=IN_COLS, col_off=3072, prev=g_in)
    (dx1, dg_mix), got = _mix_bwd(x1, dx2, duc, dqkv, dmq, dgl, mix_norm, W['w_in'], TILE_TOKENS,
                                  ride=scatter([G['w_in']], ['w_in']))
    P['w_in'] = got
    dx0, dab1, act1, h1, dg_ffn1 = _ffn_bwd(x0, dx1, ab1, ffn1_norm, W['ffn1_w_up'], W['ffn1_w_down'], TILE_FFN, "ffn1_bwd")
    g_wd1, _ = _tn_matmul(act1, dx1, 512, "grad_ffn1_w_down", scale=0.5)
    g_wu1a, got = _tn_matmul(h1, dab1, 768, "grad_ffn1_w_up_a", x_part=(0, 2), ride=scatter([g_wd1], ['ffn1_w_down']))
    P['ffn1_w_down'] = got
    g_wu1b, got_a = _tn_matmul(h1, dab1, 768, "grad_ffn1_w_up_b", x_part=(1, 2), ride=scatter([g_wu1a], ['ffn1_w_up']))
    got_b = _exchange_alone(scatter([g_wu1b], ['ffn1_w_up']), "scatter_last")
    P['ffn1_w_up'] = [got_a[0], got_b[0]]

    small = _unpack_small(_all_sum_small(_pack_small({
        'loss': loss_part, 'ffn1_norm': dg_ffn1, 'mix_norm': dg_mix, 'mem_norm': dg_mem, 'b_gate': dbg,
        'conv_dw_w': d_dw[:CONV_K], 'conv_dw_b': d_cvec[0], 'conv_ln_g': d_cvec[1], 'conv_ln_b': d_cvec[2],
        'att_rel_bias': d_rel, 'ffn2_norm': dg_ffn2, 'final_norm': dg_final})))
    loss = small['loss'][0, 0]
    me = _dev_index(*_my_coords())
    grad, delta, new_m, new_v = {}, {}, {}, {}
    for n in WEIGHTS:
        shape = w[n].shape
        if n in BIG:
            g_, d_, m_, v_ = _sum_adamw(P[n], w[n][0], mom[n][0], var[n][0], "adamw_" + n)
        else:
            if n == 'conv_dw_w':
                g_ = lax.dynamic_slice(small[n].reshape(CONV_K, CONV_W), (0, me * shape[2]), (CONV_K, shape[2]))
            else:
                g_ = small[n]
            rows = g_.shape
            d_, m_, v_ = _adamw_small(w[n].reshape(rows), g_, mom[n].reshape(rows), var[n].reshape(rows), "adamw_" + n)
        grad[n], delta[n], new_m[n], new_v[n] = (t.reshape(shape) for t in (g_, d_, m_, v_))

    return (loss, dx0.reshape(NB, S, D), *[grad[n] for n in WEIGHTS], *[delta[n] for n in WEIGHTS],
            *[new_m[n] for n in WEIGHTS], *[new_v[n] for n in WEIGHTS])
```

```python
import functools

import jax
import jax.numpy as jnp
from jax import lax
from jax.experimental import pallas as pl
from jax.experimental.pallas import tpu as pltpu

F32 = jnp.float32
BF16 = jnp.bfloat16

EPS = 1e-6
MASK_VALUE = -1e30
D = 1024
NDEV = 8
FF_SHARD = 704
FF_SHARD_PAD = 768
FF_HALF_ROWS = 352
FF_PAD = 4 * FF_SHARD_PAD
IN_COLS = 6144
CONV_W = 512
CONV_K = 31
CONV_HALO = 32
CONV_CHUNK = 32
CONV_WIN = CONV_CHUNK + 40
GLU_CHUNK = 128
ATT_W = 512
ATT_HEADS = 8
ATT_HD = 64
CHUNK = 64
LEFT_CHUNKS = 8
MAX_REL = 128
N_REL = 192
QB = 256
KWIN = QB + LEFT_CHUNKS * CHUNK
KPAD = LEFT_CHUNKS * CHUNK
DS_LANES = 1024
MEM_W = 512
MEM_HEADS = 4
MEM_HD = 128
ADAM_LR = 0.001
ADAM_B1 = 0.9
ADAM_B2 = 0.999
ADAM_EPS = 1e-08
ADAM_WD = 0.01
ADAM_STEP = 10
VMEM_LIMIT = 60 * 1024 * 1024
TILE_FFN = 256
TILE_COMBINE = 256
TILE_TOKENS = 512
TILE_GRAD_TOKENS = 2048
TILE_GRAD_TOKENS_WIDE = 1024

MESH = pl.DeviceIdType.MESH
ANY = pl.BlockSpec(memory_space=pl.ANY)

WEIGHTS = ['ffn1_norm', 'ffn1_w_up', 'ffn1_w_down', 'mix_norm', 'mem_norm', 'w_in', 'b_gate', 'conv_dw_w', 'conv_dw_b',
           'conv_ln_g', 'conv_ln_b', 'conv_w_pw', 'att_rel_bias', 'att_w_o', 'mem_w_kv', 'mem_w_o', 'w_out', 'ffn2_norm',
           'ffn2_w_up', 'ffn2_w_down', 'final_norm']
BIG = {
    'ffn1_w_up': ('col', FF_SHARD_PAD), 'ffn1_w_down': ('wd', FF_HALF_ROWS), 'w_in': ('col', 768),
    'conv_w_pw': ('col', 128), 'att_w_o': ('col', 128), 'mem_w_kv': ('row', 128), 'mem_w_o': ('col', 128),
    'w_out': ('row', 128), 'ffn2_w_up': ('col', FF_SHARD_PAD), 'ffn2_w_down': ('wd', FF_HALF_ROWS),
}
BIG_ORDER = ['ffn1_w_up', 'ffn1_w_down', 'w_in', 'conv_w_pw', 'att_w_o', 'mem_w_kv', 'mem_w_o', 'w_out', 'ffn2_w_up', 'ffn2_w_down']
SMALL = [('loss', 128), ('ffn1_norm', 1024), ('mix_norm', 1024), ('mem_norm', 1024), ('b_gate', 3072),
         ('conv_dw_w', CONV_K * CONV_W), ('conv_dw_b', 512), ('conv_ln_g', 512), ('conv_ln_b', 512),
         ('att_rel_bias', ATT_HEADS * N_REL), ('ffn2_norm', 1024), ('final_norm', 1024)]
SMALL_ROWS = 216


def _dot(a, b):
    return jnp.dot(a, b, preferred_element_type=F32)


def _dot_nt(a, b):
    return lax.dot_general(a, b, (((1,), (1,)), ((), ())), preferred_element_type=F32)


def _dot_tn(a, b):
    return lax.dot_general(a, b, (((0,), (0,)), ((), ())), preferred_element_type=F32)


def _sigmoid(v):
    return jax.nn.sigmoid(v)


def _const(shape):
    return pl.BlockSpec(shape, lambda *_: (0,) * len(shape), pipeline_mode=pl.Buffered(1))


def _params(*sem):
    return pltpu.CompilerParams(dimension_semantics=sem if sem else None, vmem_limit_bytes=VMEM_LIMIT)


def _my_coords():
    return lax.axis_index("x"), lax.axis_index("y"), lax.axis_index("c")


def _dev_index(px, py, pc):
    return 4 * px + 2 * py + pc


def _window(ref, kind, n, p):
    if kind == 'row':
        return ref.at[pl.ds(pl.multiple_of(p * n, n), n), :]
    if kind == 'col':
        return ref.at[:, pl.ds(pl.multiple_of(p * n, 128), n)]
    start = (p // 2) * FF_SHARD_PAD + (p % 2) * FF_HALF_ROWS
    return ref.at[pl.ds(pl.multiple_of(start, 32), n), :]


def _full_shape(kind, n, shard_shape):
    if kind == 'row':
        return (NDEV * n, shard_shape[1])
    if kind == 'col':
        return (shard_shape[0], NDEV * n)
    return (FF_PAD, shard_shape[1])


def _cast_shards(shards):
    n = len(shards)

    def body(*refs):
        for i in range(n):
            src, dst = refs[i], refs[n + i]
            if src.shape[1] == FF_SHARD:
                dst[:, pl.ds(0, FF_SHARD)] = src[...].astype(BF16)
                dst[:, pl.ds(FF_SHARD, FF_SHARD_PAD - FF_SHARD)] = jnp.zeros((src.shape[0], FF_SHARD_PAD - FF_SHARD), BF16)
            else:
                dst[...] = src[...].astype(BF16)

    out_shape = [jax.ShapeDtypeStruct((s.shape[0], FF_SHARD_PAD if s.shape[1] == FF_SHARD else s.shape[1]), BF16) for s in shards]
    return pl.pallas_call(body, out_shape=out_shape, name="cast_shards", compiler_params=_params())(*shards)


class _Ride:
    def __init__(self, inputs, out_shape, scratch, start, finish, mid=None):
        self.inputs, self.out_shape, self.scratch = list(inputs), list(out_shape), list(scratch)
        self.start, self.finish, self.mid = start, finish, mid


def _pallas(body, name, grid, in_specs, out_specs, out_shape, args, scratch_shapes=(), sem=None, aliases=None, ride=None):
    if ride is None:
        outs = pl.pallas_call(body, grid=grid, name=name, in_specs=in_specs, out_specs=out_specs, out_shape=out_shape,
                              scratch_shapes=list(scratch_shapes), input_output_aliases=aliases or {},
                              compiler_params=_params(*sem))(*args)
        return list(outs), []
    n_in, n_out, n_scr = len(args), len(out_shape), len(scratch_shapes)
    r_in, r_out = len(ride.inputs), len(ride.out_shape)

    def wrapped(*refs):
        k_in, rin = refs[:n_in], refs[n_in:n_in + r_in]
        o0 = n_in + r_in
        k_out, rout = refs[o0:o0 + n_out], refs[o0 + n_out:o0 + n_out + r_out]
        s0 = o0 + n_out + r_out
        k_scr, rscr = refs[s0:s0 + n_scr], refs[s0 + n_scr:]
        ids = [pl.program_id(k) for k in range(len(grid))]
        first = functools.reduce(jnp.logical_and, [i == 0 for i in ids])
        last = functools.reduce(jnp.logical_and, [i == g - 1 for i, g in zip(ids, grid)])
        pl.when(first)(lambda: ride.start(rin, rout, rscr))
        if ride.mid is not None:
            at_mid = functools.reduce(jnp.logical_and, [ids[0] == grid[0] // 2] + [i == 0 for i in ids[1:]])
            pl.when(at_mid)(lambda: ride.mid(rin, rout, rscr))
        body(*k_in, *k_out, *k_scr)
        pl.when(last)(lambda: ride.finish(rin, rout, rscr))

    outs = pl.pallas_call(
        wrapped, grid=grid, name=name, in_specs=list(in_specs) + [ANY] * r_in, out_specs=list(out_specs) + [ANY] * r_out,
        out_shape=list(out_shape) + ride.out_shape, scratch_shapes=list(scratch_shapes) + ride.scratch,
        input_output_aliases=aliases or {}, compiler_params=_params(*(["arbitrary"] * len(grid))),
    )(*args, *ride.inputs)
    return list(outs[:n_out]), list(outs[n_out:])


def _exchange_alone(ride, name):
    r_in, r_out = len(ride.inputs), len(ride.out_shape)

    def body(*refs):
        rin, rout, rscr = refs[:r_in], refs[r_in:r_in + r_out], refs[r_in + r_out:]
        ride.start(rin, rout, rscr)
        if ride.mid is not None:
            ride.mid(rin, rout, rscr)
        ride.finish(rin, rout, rscr)

    return pl.pallas_call(body, out_shape=ride.out_shape, in_specs=[ANY] * r_in, out_specs=[ANY] * r_out, name=name,
                          scratch_shapes=ride.scratch, compiler_params=pltpu.CompilerParams(has_side_effects=True))(*ride.inputs)


def _gather_ride(shards, kinds, zeros_pad=None):
    n = len(shards)
    wd_ids = [i for i, (k, _) in enumerate(kinds) if k == 'wd']

    def plan(rin, out, sems):
        send_sems, recv_sems, local_sems = sems[:3]
        x, y, c = _my_coords()
        me, sibling = (x, y, c), (x, y, 1 - c)
        chips = [(1 - x, y), (x, 1 - y), (1 - x, 1 - y)]

        def win(i, dev):
            return _window(out[i], kinds[i][0], kinds[i][1], _dev_index(*dev))

        def copy(i, k, block, to, from_shard=False):
            return pltpu.make_async_remote_copy(
                src_ref=rin[i] if from_shard else win(i, block), dst_ref=win(i, block),
                send_sem=send_sems.at[i, k], recv_sem=recv_sems.at[i, k], device_id=to, device_id_type=MESH)

        def local():
            cps = [pltpu.make_async_copy(rin[i], win(i, me), local_sems.at[i]) for i in range(n)]
            for j, i in enumerate(wd_ids):
                for q in range(4):
                    pad_rows = out[i].at[pl.ds(q * FF_SHARD_PAD + 2 * FF_HALF_ROWS, FF_SHARD_PAD - 2 * FF_HALF_ROWS), :]
                    cps.append(pltpu.make_async_copy(rin[n], pad_rows, sems[3].at[j, q]))
            return cps

        def first():
            cps = []
            for i in range(n):
                cps.append(copy(i, 0, me, sibling, from_shard=True))
                cps += [copy(i, 1 + j, me, (*chip, c), from_shard=True) for j, chip in enumerate(chips)]
            return cps

        def arrived():
            return [copy(i, 1 + j, (*chip, c), me) for j, chip in enumerate(chips) for i in range(n)]

        def passed():
            return [copy(i, 4 + j, (*chip, c), sibling) for j, chip in enumerate(chips) for i in range(n)]

        def from_sibling():
            cps = [copy(i, 0, sibling, me) for i in range(n)]
            return cps + [copy(i, 4 + j, (*chip, 1 - c), me) for i in range(n) for j, chip in enumerate(chips)]

        return local, first, arrived, passed, from_sibling

    def start(rin, out, sems):
        local, first, _, _, _ = plan(rin, out, sems)
        for cp in local() + first():
            cp.start()

    def mid(rin, out, sems):
        _, _, arrived, passed, _ = plan(rin, out, sems)
        for got, fwd in zip(arrived(), passed()):
            got.wait_recv()
            fwd.start()

    def finish(rin, out, sems):
        local, first, _, passed, from_sibling = plan(rin, out, sems)
        for cp in from_sibling():
            cp.wait_recv()
        for cp in first() + passed():
            cp.wait_send()
        for cp in local():
            cp.wait()

    out_shape = [jax.ShapeDtypeStruct(_full_shape(k, m, s.shape), s.dtype) for s, (k, m) in zip(shards, kinds)]
    scratch = [pltpu.SemaphoreType.DMA((n, 7)), pltpu.SemaphoreType.DMA((n, 7)), pltpu.SemaphoreType.DMA((n,))]
    if wd_ids:
        scratch.append(pltpu.SemaphoreType.DMA((len(wd_ids), 4)))
    return _Ride(list(shards) + ([zeros_pad] if wd_ids else []), out_shape, scratch, start, finish, mid)


def _scatter_ride(grads, kinds):
    n = len(grads)

    def plan(g, out, sems):
        send_sems, recv_sems, local_sems = sems
        x, y, c = _my_coords()
        me = _dev_index(x, y, c)

        def local():
            return [pltpu.make_async_copy(_window(g[i], kinds[i][0], kinds[i][1], me), out[i].at[me], local_sems.at[i])
                    for i in range(n)]

        def remote(arrival):
            cps = []
            for rel in range(1, NDEV):
                peer = _peer(x, y, c, rel)
                dev = _dev_index(*peer)
                for i in range(n):
                    kind, m = kinds[i]
                    cps.append(pltpu.make_async_remote_copy(
                        src_ref=_window(g[i], kind, m, me if arrival else dev), dst_ref=out[i].at[dev if arrival else me],
                        send_sem=send_sems.at[i, rel - 1], recv_sem=recv_sems.at[i, rel - 1], device_id=peer, device_id_type=MESH))
            return cps

        return local, remote

    def start(g, out, sems):
        local, remote = plan(g, out, sems)
        for cp in local() + remote(False):
            cp.start()

    def finish(g, out, sems):
        local, remote = plan(g, out, sems)
        for cp in remote(True):
            cp.wait_recv()
        for cp in remote(False):
            cp.wait_send()
        for cp in local():
            cp.wait()

    def block_shape(gr, kind, m):
        return (m, gr.shape[1]) if kind in ('row', 'wd') else (gr.shape[0], m)

    out_shape = [jax.ShapeDtypeStruct((NDEV,) + block_shape(gr, k, m), gr.dtype) for gr, (k, m) in zip(grads, kinds)]
    scratch = [pltpu.SemaphoreType.DMA((n, NDEV - 1)), pltpu.SemaphoreType.DMA((n, NDEV - 1)), pltpu.SemaphoreType.DMA((n,))]
    return _Ride(grads, out_shape, scratch, start, finish)


def _rms_stats(xf):
    r = lax.rsqrt(jnp.mean(xf * xf, axis=-1, keepdims=True) + EPS)
    return xf * r, r


def _rms_bwd(dh, g, xhat, r):
    dxhat = dh * g
    return r * (dxhat - xhat * jnp.mean(dxhat * xhat, axis=-1, keepdims=True))


def _ffn_fwd(x, g, wu, wd, tm, name, ride=None):
    T = x.shape[0]

    def body(x_ref, g_ref, wu_ref, wd_ref, xo_ref, ab_ref):
        xf = x_ref[...]
        xhat, _ = _rms_stats(xf)
        h = (xhat * g_ref[...]).astype(BF16)
        acc = jnp.zeros((tm, D), F32)
        for j in range(4):
            ca = slice(j * FF_SHARD_PAD, (j + 1) * FF_SHARD_PAD)
            cb = slice(FF_PAD + j * FF_SHARD_PAD, FF_PAD + (j + 1) * FF_SHARD_PAD)
            a = _dot(h, wu_ref[:, ca])
            b = _dot(h, wu_ref[:, cb])
            ab_ref[:, ca] = a.astype(BF16)
            ab_ref[:, cb] = b.astype(BF16)
            act = (a * _sigmoid(a) * b).astype(BF16)
            acc = acc + _dot(act, wd_ref[ca, :])
        xo_ref[...] = xf + 0.5 * acc

    return _pallas(
        body, name, (T // tm,),
        [pl.BlockSpec((tm, D), lambda t: (t, 0)), _const((1, D)), _const((D, 2 * FF_PAD)), _const((FF_PAD, D))],
        [pl.BlockSpec((tm, D), lambda t: (t, 0)), pl.BlockSpec((tm, 2 * FF_PAD), lambda t: (t, 0))],
        [jax.ShapeDtypeStruct((T, D), F32), jax.ShapeDtypeStruct((T, 2 * FF_PAD), BF16)],
        (x, g, wu, wd), sem=("arbitrary",), ride=ride)


def _ffn_bwd(x, dy, ab, g, wu, wd, tm, name):
    T = x.shape[0]

    def body(x_ref, dy_ref, ab_ref, g_ref, wu_ref, wd_ref, dx_ref, dab_ref, act_ref, h_ref, dg_ref):
        xf = x_ref[...]
        xhat, r = _rms_stats(xf)
        gain = g_ref[...]
        h_ref[...] = (xhat * gain).astype(BF16)
        dy = dy_ref[...]
        dyh = (0.5 * dy).astype(BF16)
        dh = jnp.zeros((tm, D), F32)
        for j in range(4):
            ca = slice(j * FF_SHARD_PAD, (j + 1) * FF_SHARD_PAD)
            cb = slice(FF_PAD + j * FF_SHARD_PAD, FF_PAD + (j + 1) * FF_SHARD_PAD)
            a = ab_ref[:, ca].astype(F32)
            b = ab_ref[:, cb].astype(F32)
            dact = _dot_nt(dyh, wd_ref[ca, :])
            sg = _sigmoid(a)
            sl = a * sg
            act_ref[:, ca] = (sl * b).astype(BF16)
            da = (dact * b * (sg * (1.0 + a * (1.0 - sg)))).astype(BF16)
            db = (dact * sl).astype(BF16)
            dab_ref[:, ca] = da
            dab_ref[:, cb] = db
            dh = dh + _dot_nt(da, wu_ref[:, ca]) + _dot_nt(db, wu_ref[:, cb])
        dx_ref[...] = dy + _rms_bwd(dh, gain, xhat, r)

        @pl.when(pl.program_id(0) == 0)
        def _():
            dg_ref[...] = jnp.zeros_like(dg_ref)
        dg_ref[...] += jnp.sum(dh * xhat, axis=0, keepdims=True)

    return pl.pallas_call(
        body, grid=(T // tm,), name=name,
        in_specs=[pl.BlockSpec((tm, D), lambda t: (t, 0)), pl.BlockSpec((tm, D), lambda t: (t, 0)),
                  pl.BlockSpec((tm, 2 * FF_PAD), lambda t: (t, 0)), _const((1, D)), _const((D, 2 * FF_PAD)), _const((FF_PAD, D))],
        out_specs=[pl.BlockSpec((tm, D), lambda t: (t, 0)), pl.BlockSpec((tm, 2 * FF_PAD), lambda t: (t, 0)),
                   pl.BlockSpec((tm, FF_PAD), lambda t: (t, 0)), pl.BlockSpec((tm, D), lambda t: (t, 0)),
                   pl.BlockSpec((1, D), lambda t: (0, 0))],
        out_shape=[jax.ShapeDtypeStruct((T, D), F32), jax.ShapeDtypeStruct((T, 2 * FF_PAD), BF16),
                   jax.ShapeDtypeStruct((T, FF_PAD), BF16), jax.ShapeDtypeStruct((T, D), BF16), jax.ShapeDtypeStruct((1, D), F32)],
        compiler_params=_params("arbitrary"),
    )(x, dy, ab, g, wu, wd)


def _tn_matmul(xm, ym, tn, name, scale=None, out_cols=None, col_off=0, prev=None, tt=TILE_GRAD_TOKENS, x_part=(0, 1), ride=None):
    T = xm.shape[0]
    xi, xn = x_part
    K = xm.shape[1] // xn
    N = ym.shape[1]
    out_cols = N if out_cols is None else out_cols
    tt = min(tt, T)
    nt = T // tt
    off = col_off // tn

    def body(*refs):
        x_ref, y_ref = refs[0], refs[1]
        o_ref, acc = refs[-2], refs[-1]

        @pl.when(pl.program_id(1) == 0)
        def _():
            acc[...] = jnp.zeros_like(acc)
        acc[...] += _dot_tn(x_ref[...].astype(BF16), y_ref[...].astype(BF16))

        @pl.when(pl.program_id(1) == nt - 1)
        def _():
            res = acc[...]
            o_ref[...] = (res if scale is None else res * scale).astype(BF16)

    in_specs = [pl.BlockSpec((tt, K), lambda n, t: (t, xi)), pl.BlockSpec((tt, tn), lambda n, t: (t, n))]
    args = [xm, ym]
    aliases = {}
    if prev is not None:
        in_specs.append(ANY)
        args.append(prev)
        aliases = {2: 0}
    outs, rode = _pallas(
        body, name, (N // tn, nt), in_specs, [pl.BlockSpec((K, tn), lambda n, t: (0, n + off))],
        [jax.ShapeDtypeStruct((K, out_cols), BF16)], args, scratch_shapes=[pltpu.VMEM((K, tn), F32)],
        sem=("parallel", "arbitrary"), aliases=aliases, ride=ride)
    return outs[0], rode


def _mix_fwd(x, g, w_in, tm, ride=None):
    T = x.shape[0]

    def body(x_ref, g_ref, w_ref, uc_ref, qkv_ref, mq_ref, gl_ref, h_ref):
        xhat, _ = _rms_stats(x_ref[...])
        h = (xhat * g_ref[...]).astype(BF16)
        h_ref[...] = h
        uc_ref[...] = _dot(h, w_ref[:, 0:1024])
        qkv_ref[...] = _dot(h, w_ref[:, 1024:2560]).astype(BF16)
        mq_ref[...] = _dot(h, w_ref[:, 2560:3072]).astype(BF16)
        for j in range(3):
            gl_ref[:, j * D:(j + 1) * D] = _dot(h, w_ref[:, 3072 + j * D:3072 + (j + 1) * D])

    row = lambda w: pl.BlockSpec((tm, w), lambda t: (t, 0))
    return _pallas(
        body, "mix_fwd", (T // tm,), [row(D), _const((1, D)), _const((D, IN_COLS))],
        [row(1024), row(1536), row(512), row(3072), row(D)],
        [jax.ShapeDtypeStruct((T, 1024), F32), jax.ShapeDtypeStruct((T, 1536), BF16), jax.ShapeDtypeStruct((T, 512), BF16),
         jax.ShapeDtypeStruct((T, 3072), F32), jax.ShapeDtypeStruct((T, D), BF16)],
        (x, g, w_in), sem=("parallel",), ride=ride)


def _mix_bwd(x, dres, duc, dqkv, dmq, dgl, g, w_in, tm, ride=None):
    T = x.shape[0]

    def body(x_ref, dres_ref, duc_ref, dqkv_ref, dmq_ref, dgl_ref, g_ref, w_ref, dx_ref, dg_ref):
        xhat, r = _rms_stats(x_ref[...])
        dh = _dot_nt(duc_ref[...], w_ref[:, 0:1024])
        dh = dh + _dot_nt(dqkv_ref[...], w_ref[:, 1024:2560])
        dh = dh + _dot_nt(dmq_ref[...], w_ref[:, 2560:3072])
        dh = dh + _dot_nt(dgl_ref[...], w_ref[:, 3072:6144])
        dx_ref[...] = dres_ref[...] + _rms_bwd(dh, g_ref[...], xhat, r)

        @pl.when(pl.program_id(0) == 0)
        def _():
            dg_ref[...] = jnp.zeros_like(dg_ref)
        dg_ref[...] += jnp.sum(dh * xhat, axis=0, keepdims=True)

    row = lambda w: pl.BlockSpec((tm, w), lambda t: (t, 0))
    return _pallas(
        body, "mix_bwd", (T // tm,),
        [row(D), row(D), row(1024), row(1536), row(512), row(3072), _const((1, D)), _const((D, IN_COLS))],
        [row(D), pl.BlockSpec((1, D), lambda t: (0, 0))],
        [jax.ShapeDtypeStruct((T, D), F32), jax.ShapeDtypeStruct((1, D), F32)],
        (x, dres, duc, dqkv, dmq, dgl, g, w_in), sem=("arbitrary",), ride=ride)


def _shifted(win, base, copies):
    for k in range(8):
        copies[k] = win[base + k:base + k + CONV_CHUNK + 24]
    return copies


def _tap_slices(copies, tap):
    out = []
    for k in range(8):
        for a in range(4):
            j = tap(a, k)
            if 0 <= j < CONV_K:
                out.append((j, copies[k, pl.ds(8 * a, CONV_CHUNK), :]))
    return out


def _conv_taps(copies, w_ref, tap):
    acc = jnp.zeros((CONV_CHUNK, CONV_W), F32)
    for j, rows in _tap_slices(copies, tap):
        acc = acc + rows * w_ref[j:j + 1, :]
    return acc


def _fold8(v):
    acc = v[0:8]
    for r in range(8, CONV_CHUNK, 8):
        acc = acc + v[r:r + 8]
    return acc


def _glu_into(uc_ref, vpad, S):
    vpad[pl.ds(0, CONV_HALO), :] = jnp.zeros((CONV_HALO, CONV_W), F32)
    vpad[pl.ds(S + CONV_HALO, CONV_HALO), :] = jnp.zeros((CONV_HALO, CONV_W), F32)

    def glu(i, carry):
        r0 = pl.multiple_of(i * GLU_CHUNK, GLU_CHUNK)
        a = uc_ref[0, pl.ds(r0, GLU_CHUNK), 0:CONV_W]
        gt = uc_ref[0, pl.ds(r0, GLU_CHUNK), CONV_W:2 * CONV_W]
        vpad[pl.ds(pl.multiple_of(r0 + CONV_HALO, CONV_HALO), GLU_CHUNK), :] = a * _sigmoid(gt)
        return carry
    lax.fori_loop(0, S // GLU_CHUNK, glu, 0)


def _conv_ln(vpad, w_ref, vec_ref, r0, copies):
    win = vpad[pl.ds(r0, CONV_WIN), :]
    z = _conv_taps(_shifted(win, CONV_HALO - (CONV_K - 1), copies), w_ref, lambda a, k: 8 * a + k) + vec_ref[0:1, :]
    xc = z - jnp.mean(z, axis=-1, keepdims=True)
    rstd = lax.rsqrt(jnp.mean(xc * xc, axis=-1, keepdims=True) + EPS)
    xn = xc * rstd
    return xn, rstd, xn * vec_ref[1:2, :] + vec_ref[2:3, :]


def _conv_fwd(uc, dw_w, vec):
    NB, S, _ = uc.shape

    def body(uc_ref, w_ref, vec_ref, o_ref, vpad, copies):
        _glu_into(uc_ref, vpad, S)

        def conv(i, carry):
            r0 = pl.multiple_of(i * CONV_CHUNK, CONV_CHUNK)
            _, _, yln = _conv_ln(vpad, w_ref, vec_ref, r0, copies)
            o_ref[0, pl.ds(r0, CONV_CHUNK), :] = (yln * _sigmoid(yln)).astype(BF16)
            return carry
        lax.fori_loop(0, S // CONV_CHUNK, conv, 0, unroll=2)

    return pl.pallas_call(
        body, grid=(NB,), name="conv_fwd",
        in_specs=[pl.BlockSpec((1, S, 2 * CONV_W), lambda b: (b, 0, 0)), _const((CONV_K, CONV_W)), _const((8, CONV_W))],
        out_specs=pl.BlockSpec((1, S, CONV_W), lambda b: (b, 0, 0)),
        out_shape=jax.ShapeDtypeStruct((NB, S, CONV_W), BF16),
        scratch_shapes=[pltpu.VMEM((S + 2 * CONV_HALO, CONV_W), F32), pltpu.VMEM((8, CONV_CHUNK + 24, CONV_W), F32)],
        compiler_params=_params("parallel"),
    )(uc, dw_w, vec)


def _conv_bwd(uc, dcact, dw_w, vec, ride=None):
    NB, S, _ = uc.shape
    n_chunks = S // CONV_CHUNK

    def body(uc_ref, dc_ref, w_ref, vec_ref, duc_ref, dw_ref, dvec_ref, vpad, dzpad, dw8, dvec8, copies):
        @pl.when(pl.program_id(0) == 0)
        def _():
            dw8[...] = jnp.zeros_like(dw8)
            dvec8[...] = jnp.zeros_like(dvec8)
        _glu_into(uc_ref, vpad, S)
        dzpad[pl.ds(S, 2 * CONV_HALO), :] = jnp.zeros((2 * CONV_HALO, CONV_W), F32)

        def norm_bwd(i, carry):
            r0 = pl.multiple_of(i * CONV_CHUNK, CONV_CHUNK)
            xn, rstd, yln = _conv_ln(vpad, w_ref, vec_ref, r0, copies)
            sg = _sigmoid(yln)
            dyln = dc_ref[0, pl.ds(r0, CONV_CHUNK), :] * (sg * (1.0 + yln * (1.0 - sg)))
            dxn = dyln * vec_ref[1:2, :]
            dz = rstd * (dxn - jnp.mean(dxn, axis=-1, keepdims=True) - xn * jnp.mean(dxn * xn, axis=-1, keepdims=True))
            dzpad[pl.ds(r0, CONV_CHUNK), :] = dz
            dvec8[0] += _fold8(dz)
            dvec8[1] += _fold8(dyln * xn)
            dvec8[2] += _fold8(dyln)
            return carry
        lax.fori_loop(0, n_chunks, norm_bwd, 0, unroll=2)

        def taps_bwd(i, carry):
            r0 = pl.multiple_of(i * CONV_CHUNK, CONV_CHUNK)
            dzwin = dzpad[pl.ds(r0, CONV_WIN), :]
            dv = _conv_taps(_shifted(dzwin, 0, copies), w_ref, lambda a, k: CONV_K - 1 - 8 * a - k)
            dz = dzwin[0:CONV_CHUNK]
            vwin = vpad[pl.ds(r0, CONV_WIN), :]
            for j, rows in _tap_slices(_shifted(vwin, CONV_HALO - (CONV_K - 1), copies), lambda a, k: 8 * a + k):
                dw8[j] += _fold8(dz * rows)
            a = uc_ref[0, pl.ds(r0, CONV_CHUNK), 0:CONV_W]
            sg = _sigmoid(uc_ref[0, pl.ds(r0, CONV_CHUNK), CONV_W:2 * CONV_W])
            duc_ref[0, pl.ds(r0, CONV_CHUNK), 0:CONV_W] = (dv * sg).astype(BF16)
            duc_ref[0, pl.ds(r0, CONV_CHUNK), CONV_W:2 * CONV_W] = (dv * a * sg * (1.0 - sg)).astype(BF16)
            return carry
        lax.fori_loop(0, n_chunks, taps_bwd, 0, unroll=2)

        @pl.when(pl.program_id(0) == NB - 1)
        def _():
            dw_ref[...] = jnp.zeros_like(dw_ref)
            dvec_ref[...] = jnp.zeros_like(dvec_ref)
            for j in range(CONV_K):
                dw_ref[j:j + 1, :] = jnp.sum(dw8[j], axis=0, keepdims=True)
            for j in range(3):
                dvec_ref[j:j + 1, :] = jnp.sum(dvec8[j], axis=0, keepdims=True)

    return _pallas(
        body, "conv_bwd", (NB,),
        [pl.BlockSpec((1, S, 2 * CONV_W), lambda b: (b, 0, 0)), pl.BlockSpec((1, S, CONV_W), lambda b: (b, 0, 0)),
         _const((CONV_K, CONV_W)), _const((8, CONV_W))],
        [pl.BlockSpec((1, S, 2 * CONV_W), lambda b: (b, 0, 0)), pl.BlockSpec((32, CONV_W), lambda b: (0, 0)),
         pl.BlockSpec((8, CONV_W), lambda b: (0, 0))],
        [jax.ShapeDtypeStruct((NB, S, 2 * CONV_W), BF16), jax.ShapeDtypeStruct((32, CONV_W), F32),
         jax.ShapeDtypeStruct((8, CONV_W), F32)],
        (uc, dcact, dw_w, vec),
        scratch_shapes=[pltpu.VMEM((S + 2 * CONV_HALO, CONV_W), F32), pltpu.VMEM((S + 2 * CONV_HALO, CONV_W), F32),
                        pltpu.VMEM((CONV_K, 8, CONV_W), F32), pltpu.VMEM((3, 8, CONV_W), F32),
                        pltpu.VMEM((8, CONV_CHUNK + 24, CONV_W), F32)],
        sem=("arbitrary",), ride=ride)


def _rel_index_of_column(cols):
    offset = jnp.where(cols < KWIN, cols, cols - DS_LANES)
    return jnp.clip(KPAD - offset, -(CHUNK - 1), MAX_REL) + (CHUNK - 1)


def _bias_table(rel_bias):
    def body(rb_ref, o_ref, by_offset):
        ridx = _rel_index_of_column(lax.broadcasted_iota(jnp.int32, (1, DS_LANES), 1))
        onehot = (ridx == lax.broadcasted_iota(jnp.int32, (N_REL, 1), 0)).astype(F32)
        by_offset[...] = jnp.dot(rb_ref[...], onehot, preferred_element_type=F32, precision=lax.Precision.HIGHEST)
        sub = lax.broadcasted_iota(jnp.int32, (8, 1), 0)
        kchunk = lax.broadcasted_iota(jnp.int32, (1, KWIN), 1) // CHUNK
        for head in range(ATT_HEADS):
            base = jnp.broadcast_to(by_offset[head:head + 1, :], (8, DS_LANES))
            rows = base
            for s in range(1, 8):
                rows = jnp.where(sub == s, pltpu.roll(base, s, 1), rows)

            def rows8(q8, carry):
                qchunk = (q8 * 8 + sub) // CHUNK
                tile = pltpu.roll(rows, q8 * 8, 1)[:, 0:KWIN]
                band = (kchunk >= qchunk) & (kchunk <= qchunk + LEFT_CHUNKS)
                o_ref[head, pl.ds(pl.multiple_of(q8 * 8, 8), 8), :] = jnp.where(band, tile, MASK_VALUE)
                return carry
            lax.fori_loop(0, QB // 8, rows8, 0)

    return pl.pallas_call(body, out_shape=jax.ShapeDtypeStruct((ATT_HEADS, QB, KWIN), F32), name="bias_table",
                          scratch_shapes=[pltpu.VMEM((ATT_HEADS, DS_LANES), F32)], compiler_params=_params())(rel_bias)


def _load_keys(i, k_ref, v_ref, kpad, vpad, S):
    @pl.when(i == 0)
    def _():
        kpad[pl.ds(0, KPAD), :] = jnp.zeros((KPAD, ATT_W), BF16)
        vpad[pl.ds(0, KPAD), :] = jnp.zeros((KPAD, ATT_W), BF16)
        kpad[pl.ds(KPAD, S), :] = k_ref[0]
        vpad[pl.ds(KPAD, S), :] = v_ref[0]


def _att_probs(q2, k2, tab_ref, head, in_head, in_seq):
    qm = jnp.where(in_head, q2, jnp.zeros_like(q2))
    s = _dot_nt(qm, k2) * (ATT_HD ** -0.5) + tab_ref[head]
    s = jnp.where(in_seq, s, MASK_VALUE)
    e = jnp.exp(s - jnp.max(s, axis=-1, keepdims=True))
    return e * (1.0 / jnp.sum(e, axis=-1, keepdims=True))


def _att_fwd(qkv, tab):
    NB, S, _ = qkv.shape

    def body(q_ref, k_ref, v_ref, tab_ref, o_ref, kpad, vpad):
        i = pl.program_id(1)
        _load_keys(i, k_ref, v_ref, kpad, vpad, S)
        koff = pl.multiple_of(i * QB, QB)
        lane = lax.broadcasted_iota(jnp.int32, (1, 128), 1)
        in_seq = (lax.broadcasted_iota(jnp.int32, (1, KWIN), 1) + i * QB) >= KPAD
        for pair in range(ATT_HEADS // 2):
            cols = slice(pair * 128, (pair + 1) * 128)
            q2 = q_ref[0, :, cols]
            k2 = kpad[pl.ds(koff, KWIN), cols]
            v2 = vpad[pl.ds(koff, KWIN), cols]
            o2 = jnp.zeros((QB, 128), F32)
            for hh in range(2):
                in_head = (lane // ATT_HD) == hh
                p = _att_probs(q2, k2, tab_ref, 2 * pair + hh, in_head, in_seq)
                o2 = jnp.where(in_head, _dot(p.astype(BF16), v2), o2)
            o_ref[0, :, cols] = o2.astype(BF16)

    seq = lambda col: pl.BlockSpec((1, S, ATT_W), lambda b, i: (b, 0, col), pipeline_mode=pl.Buffered(1))
    return pl.pallas_call(
        body, grid=(NB, S // QB), name="att_fwd",
        in_specs=[pl.BlockSpec((1, QB, ATT_W), lambda b, i: (b, i, 0)), seq(1), seq(2), _const((ATT_HEADS, QB, KWIN))],
        out_specs=pl.BlockSpec((1, QB, ATT_W), lambda b, i: (b, i, 0)),
        out_shape=jax.ShapeDtypeStruct((NB, S, ATT_W), BF16),
        scratch_shapes=[pltpu.VMEM((S + KPAD, ATT_W), BF16), pltpu.VMEM((S + KPAD, ATT_W), BF16)],
        compiler_params=_params("arbitrary", "arbitrary"),
    )(qkv, qkv, qkv, tab)


def _att_bwd(qkv, do, tab, ride=None):
    NB, S, _ = qkv.shape
    nq = S // QB

    def body(q_ref, k_ref, v_ref, do_ref, tab_ref, dqkv_ref, ds_hbm, kpad, vpad, dkpad, dvpad, ds_acc, ds_sem):
        b, i = pl.program_id(0), pl.program_id(1)
        _load_keys(i, k_ref, v_ref, kpad, vpad, S)

        @pl.when(i == 0)
        def _():
            dkpad[...] = jnp.zeros_like(dkpad)
            dvpad[...] = jnp.zeros_like(dvpad)

        @pl.when((i == 0) & (b == 0))
        def _():
            ds_acc[...] = jnp.zeros_like(ds_acc)

        koff = pl.multiple_of(i * QB, QB)
        lane = lax.broadcasted_iota(jnp.int32, (1, 128), 1)
        in_seq = (lax.broadcasted_iota(jnp.int32, (1, KWIN), 1) + i * QB) >= KPAD
        for pair in range(ATT_HEADS // 2):
            cols = slice(pair * 128, (pair + 1) * 128)
            q2 = q_ref[0, :, cols]
            do2 = do_ref[0, :, cols]
            k2 = kpad[pl.ds(koff, KWIN), cols]
            v2 = vpad[pl.ds(koff, KWIN), cols]
            dq2 = jnp.zeros((QB, 128), F32)
            dk2 = jnp.zeros((KWIN, 128), F32)
            dv2 = jnp.zeros((KWIN, 128), F32)
            for hh in range(2):
                head = 2 * pair + hh
                in_head = (lane // ATT_HD) == hh
                p = _att_probs(q2, k2, tab_ref, head, in_head, in_seq)
                dom = jnp.where(in_head, do2, jnp.zeros_like(do2))
                dp = _dot_nt(dom, v2)
                ds = p * (dp - jnp.sum(p * dp, axis=-1, keepdims=True))
                ds_acc[head] += ds
                dss = (ds * (ATT_HD ** -0.5)).astype(BF16)
                dq2 = jnp.where(in_head, _dot(dss, k2), dq2)
                dk2 = jnp.where(in_head, _dot_tn(dss, q2), dk2)
                dv2 = jnp.where(in_head, _dot_tn(p.astype(BF16), do2), dv2)
            dqkv_ref[0, pl.ds(koff, QB), cols] = dq2.astype(BF16)
            dkpad[pl.ds(koff, KWIN), cols] += dk2
            dvpad[pl.ds(koff, KWIN), cols] += dv2

        @pl.when(i == nq - 1)
        def _():
            dqkv_ref[0, :, ATT_W:2 * ATT_W] = dkpad[pl.ds(KPAD, S), :].astype(BF16)
            dqkv_ref[0, :, 2 * ATT_W:3 * ATT_W] = dvpad[pl.ds(KPAD, S), :].astype(BF16)

        @pl.when((i == nq - 1) & (b == NB - 1))
        def _():
            out = pltpu.make_async_copy(ds_acc, ds_hbm, ds_sem)
            out.start()
            out.wait()

    seq = lambda col: pl.BlockSpec((1, S, ATT_W), lambda b, i: (b, 0, col), pipeline_mode=pl.Buffered(1))
    return _pallas(
        body, "att_bwd", (NB, nq),
        [pl.BlockSpec((1, QB, ATT_W), lambda b, i: (b, i, 0)), seq(1), seq(2),
         pl.BlockSpec((1, QB, ATT_W), lambda b, i: (b, i, 0)), _const((ATT_HEADS, QB, KWIN))],
        [pl.BlockSpec((1, S, 3 * ATT_W), lambda b, i: (b, 0, 0)), ANY],
        [jax.ShapeDtypeStruct((NB, S, 3 * ATT_W), BF16), jax.ShapeDtypeStruct((ATT_HEADS, QB, KWIN), F32)],
        (qkv, qkv, qkv, do, tab),
        scratch_shapes=[pltpu.VMEM((S + KPAD, ATT_W), BF16), pltpu.VMEM((S + KPAD, ATT_W), BF16),
                        pltpu.VMEM((S + KPAD, ATT_W), F32), pltpu.VMEM((S + KPAD, ATT_W), F32),
                        pltpu.VMEM((ATT_HEADS, QB, KWIN), F32), pltpu.SemaphoreType.DMA],
        sem=("arbitrary", "arbitrary"), ride=ride)


def _rel_bias_grad(ds):
    def body(ds_ref, o_ref):
        sub = lax.broadcasted_iota(jnp.int32, (8, 1), 0)
        ridx = _rel_index_of_column(lax.broadcasted_iota(jnp.int32, (DS_LANES, 1), 0))
        onehot = (ridx == lax.broadcasted_iota(jnp.int32, (1, N_REL), 1)).astype(F32)
        for head in range(ATT_HEADS):
            def rows8(q8, acc):
                tile = ds_ref[head, pl.ds(pl.multiple_of(q8 * 8, 8), 8), :]
                tile = jnp.concatenate([tile, jnp.zeros((8, DS_LANES - KWIN), F32)], axis=1)
                return acc + pltpu.roll(tile, lax.rem(DS_LANES - q8 * 8, DS_LANES), 1)
            acc = lax.fori_loop(0, QB // 8, rows8, jnp.zeros((8, DS_LANES), F32))
            diag = jnp.zeros((8, DS_LANES), F32)
            for s in range(8):
                shifted = acc if s == 0 else pltpu.roll(acc, DS_LANES - s, 1)
                diag = jnp.where(sub == s, shifted, diag)
            z = jnp.sum(diag, axis=0, keepdims=True)
            o_ref[head:head + 1, :] = jnp.dot(z, onehot, preferred_element_type=F32, precision=lax.Precision.HIGHEST)

    return pl.pallas_call(body, out_shape=jax.ShapeDtypeStruct((ATT_HEADS, N_REL), F32), name="rel_bias_grad",
                          compiler_params=_params())(ds)


def _memkv_fwd(mem, g, w_kv, tm):
    R = mem.shape[0]
    tm = min(tm, R)

    def body(m_ref, g_ref, w_ref, h_ref, kv_ref):
        xhat, _ = _rms_stats(m_ref[...])
        h = (xhat * g_ref[...]).astype(BF16)
        h_ref[...] = h
        kv_ref[...] = _dot(h, w_ref[...]).astype(BF16)

    row = pl.BlockSpec((tm, D), lambda t: (t, 0))
    return pl.pallas_call(
        body, grid=(R // tm,), name="memkv_fwd", in_specs=[row, _const((1, D)), _const((D, 2 * MEM_W))], out_specs=[row, row],
        out_shape=[jax.ShapeDtypeStruct((R, D), BF16), jax.ShapeDtypeStruct((R, 2 * MEM_W), BF16)],
        compiler_params=_params("parallel"),
    )(mem, g, w_kv)


def _memkv_bwd(mem, dkv, w_kv, tm):
    R = mem.shape[0]
    tm = min(tm, R)

    def body(m_ref, dkv_ref, w_ref, dg_ref):
        xhat, _ = _rms_stats(m_ref[...])
        dh = _dot_nt(dkv_ref[...].astype(BF16), w_ref[...])

        @pl.when(pl.program_id(0) == 0)
        def _():
            dg_ref[...] = jnp.zeros_like(dg_ref)
        dg_ref[...] += jnp.sum(dh * xhat, axis=0, keepdims=True)

    row = pl.BlockSpec((tm, D), lambda t: (t, 0))
    return pl.pallas_call(
        body, grid=(R // tm,), name="memkv_bwd", in_specs=[row, row, _const((D, 2 * MEM_W))],
        out_specs=pl.BlockSpec((1, D), lambda t: (0, 0)), out_shape=jax.ShapeDtypeStruct((1, D), F32),
        compiler_params=_params("arbitrary"),
    )(mem, dkv, w_kv)


def _mem_probs(qh, kh):
    s = _dot_nt(qh, kh) * (MEM_HD ** -0.5)
    e = jnp.exp(s - jnp.max(s, axis=-1, keepdims=True))
    return e * (1.0 / jnp.sum(e, axis=-1, keepdims=True))


def _mematt_fwd(mq, kv, tq):
    NB, S, _ = mq.shape
    M = kv.shape[1]

    def body(q_ref, kv_ref, o_ref):
        for h in range(MEM_HEADS):
            cols = slice(h * MEM_HD, (h + 1) * MEM_HD)
            p = _mem_probs(q_ref[0, :, cols], kv_ref[0, :, cols])
            o_ref[0, :, cols] = _dot(p.astype(BF16), kv_ref[0, :, MEM_W + h * MEM_HD:MEM_W + (h + 1) * MEM_HD]).astype(BF16)

    return pl.pallas_call(
        body, grid=(NB, S // tq), name="mematt_fwd",
        in_specs=[pl.BlockSpec((1, tq, MEM_W), lambda b, i: (b, i, 0)), pl.BlockSpec((1, M, 2 * MEM_W), lambda b, i: (b, 0, 0))],
        out_specs=pl.BlockSpec((1, tq, MEM_W), lambda b, i: (b, i, 0)),
        out_shape=jax.ShapeDtypeStruct((NB, S, MEM_W), BF16), compiler_params=_params("parallel", "parallel"),
    )(mq, kv)


def _mematt_bwd(mq, kv, do, tq):
    NB, S, _ = mq.shape
    M = kv.shape[1]

    def body(q_ref, kv_ref, do_ref, dq_ref, dkv_ref):
        @pl.when(pl.program_id(1) == 0)
        def _():
            dkv_ref[...] = jnp.zeros_like(dkv_ref)
        for h in range(MEM_HEADS):
            cols = slice(h * MEM_HD, (h + 1) * MEM_HD)
            vcols = slice(MEM_W + h * MEM_HD, MEM_W + (h + 1) * MEM_HD)
            qh, kh, vh, doh = q_ref[0, :, cols], kv_ref[0, :, cols], kv_ref[0, :, vcols], do_ref[0, :, cols]
            p = _mem_probs(qh, kh)
            dp = _dot_nt(doh, vh)
            ds = p * (dp - jnp.sum(p * dp, axis=-1, keepdims=True))
            dss = (ds * (MEM_HD ** -0.5)).astype(BF16)
            dq_ref[0, :, cols] = _dot(dss, kh).astype(BF16)
            dkv_ref[0, :, cols] += _dot_tn(dss, qh)
            dkv_ref[0, :, vcols] += _dot_tn(p.astype(BF16), doh)

    qspec = pl.BlockSpec((1, tq, MEM_W), lambda b, i: (b, i, 0))
    kvspec = pl.BlockSpec((1, M, 2 * MEM_W), lambda b, i: (b, 0, 0))
    return pl.pallas_call(
        body, grid=(NB, S // tq), name="mematt_bwd", in_specs=[qspec, kvspec, qspec], out_specs=[qspec, kvspec],
        out_shape=[jax.ShapeDtypeStruct((NB, S, MEM_W), BF16), jax.ShapeDtypeStruct((NB, M, 2 * MEM_W), F32)],
        compiler_params=_params("arbitrary", "arbitrary"),
    )(mq, kv, do)


def _branches(c_ref, a_ref, m_ref, gl_ref, bg_ref, wpw_ref, wo_ref, wmo_ref):
    ys = [_dot(c_ref[...], wpw_ref[...]), _dot(a_ref[...], wo_ref[...]), _dot(m_ref[...], wmo_ref[...])]
    gates = [_sigmoid(gl_ref[:, j * D:(j + 1) * D] + bg_ref[:, j * D:(j + 1) * D]) for j in range(3)]
    return ys, gates


def _combine_fwd(x, cact, oatt, omem, gl, bg, wpw, wo, wmo, wout, tm):
    T = x.shape[0]

    def body(x_ref, c_ref, a_ref, m_ref, gl_ref, bg_ref, wpw_ref, wo_ref, wmo_ref, wout_ref, xo_ref, y_ref):
        ys, gates = _branches(c_ref, a_ref, m_ref, gl_ref, bg_ref, wpw_ref, wo_ref, wmo_ref)
        y = (gates[0] * ys[0] + gates[1] * ys[1] + gates[2] * ys[2]).astype(BF16)
        y_ref[...] = y
        xo_ref[...] = x_ref[...] + _dot(y, wout_ref[...])

    row = lambda w: pl.BlockSpec((tm, w), lambda t: (t, 0))
    wbr = _const((512, D))
    return pl.pallas_call(
        body, grid=(T // tm,), name="combine_fwd",
        in_specs=[row(D), row(512), row(512), row(512), row(3 * D), _const((1, 3 * D)), wbr, wbr, wbr, _const((D, D))],
        out_specs=[row(D), row(D)],
        out_shape=[jax.ShapeDtypeStruct((T, D), F32), jax.ShapeDtypeStruct((T, D), BF16)],
        compiler_params=_params("parallel"),
    )(x, cact, oatt, omem, gl, bg, wpw, wo, wmo, wout)


def _combine_bwd(dx, cact, oatt, omem, gl, bg, wpw, wo, wmo, wout, tm, ride=None):
    T = dx.shape[0]

    def body(dx_ref, c_ref, a_ref, m_ref, gl_ref, bg_ref, wpw_ref, wo_ref, wmo_ref, wout_ref,
             dgl_ref, dc_ref, da_ref, dm_ref, dyc_ref, dya_ref, dym_ref, dbg_ref):
        ys, gates = _branches(c_ref, a_ref, m_ref, gl_ref, bg_ref, wpw_ref, wo_ref, wmo_ref)
        dy = _dot_nt(dx_ref[...].astype(BF16), wout_ref[...])

        @pl.when(pl.program_id(0) == 0)
        def _():
            dbg_ref[...] = jnp.zeros_like(dbg_ref)
        dyb = []
        for j in range(3):
            dlogit = dy * ys[j] * gates[j] * (1.0 - gates[j])
            dgl_ref[:, j * D:(j + 1) * D] = dlogit.astype(BF16)
            dbg_ref[:, j * D:(j + 1) * D] += jnp.sum(dlogit, axis=0, keepdims=True)
            dyb.append((dy * gates[j]).astype(BF16))
        dyc_ref[...], dya_ref[...], dym_ref[...] = dyb
        dc_ref[...] = _dot_nt(dyb[0], wpw_ref[...])
        da_ref[...] = _dot_nt(dyb[1], wo_ref[...]).astype(BF16)
        dm_ref[...] = _dot_nt(dyb[2], wmo_ref[...]).astype(BF16)

    row = lambda w: pl.BlockSpec((tm, w), lambda t: (t, 0))
    wbr = _const((512, D))
    sds = jax.ShapeDtypeStruct
    return _pallas(
        body, "combine_bwd", (T // tm,),
        [row(D), row(512), row(512), row(512), row(3 * D), _const((1, 3 * D)), wbr, wbr, wbr, _const((D, D))],
        [row(3 * D), row(512), row(512), row(512), row(D), row(D), row(D), pl.BlockSpec((1, 3 * D), lambda t: (0, 0))],
        [sds((T, 3 * D), BF16), sds((T, 512), F32), sds((T, 512), BF16), sds((T, 512), BF16),
         sds((T, D), BF16), sds((T, D), BF16), sds((T, D), BF16), sds((1, 3 * D), F32)],
        (dx, cact, oatt, omem, gl, bg, wpw, wo, wmo, wout), sem=("arbitrary",), ride=ride)


def _final(x, g, target, tm):
    T = x.shape[0]

    def body(x_ref, g_ref, t_ref, loss_ref, dx_ref, dg_ref):
        xhat, r = _rms_stats(x_ref[...])
        gain = g_ref[...]
        diff = xhat * gain - t_ref[...]
        dout = diff * (1.0 / D)

        @pl.when(pl.program_id(0) == 0)
        def _():
            loss_ref[...] = jnp.zeros_like(loss_ref)
            dg_ref[...] = jnp.zeros_like(dg_ref)
        sq = jnp.sum(jnp.sum(diff * diff, axis=0, keepdims=True), axis=1, keepdims=True)
        loss_ref[...] += jnp.broadcast_to(sq * (0.5 / D), (1, 128))
        dg_ref[...] += jnp.sum(dout * xhat, axis=0, keepdims=True)
        dx_ref[...] = _rms_bwd(dout, gain, xhat, r)

    row = pl.BlockSpec((tm, D), lambda t: (t, 0))
    return pl.pallas_call(
        body, grid=(T // tm,), name="final_loss", in_specs=[row, _const((1, D)), row],
        out_specs=[pl.BlockSpec((1, 128), lambda t: (0, 0)), row, pl.BlockSpec((1, D), lambda t: (0, 0))],
        out_shape=[jax.ShapeDtypeStruct((1, 128), F32), jax.ShapeDtypeStruct((T, D), F32), jax.ShapeDtypeStruct((1, D), F32)],
        compiler_params=_params("arbitrary"),
    )(x, g, target)


def _peer(x, y, c, rel):
    rx, ry, rc = (rel >> 2) & 1, (rel >> 1) & 1, rel & 1
    return ((1 - x) if rx else x, (1 - y) if ry else y, (1 - c) if rc else c)


def _all_sum_small(part):
    def body(p_ref, o_ref, slots, send_sems, recv_sems):
        x, y, c = _my_coords()
        me = _dev_index(x, y, c)
        slots[me] = p_ref[...]
        copies = []
        for rel in range(1, NDEV):
            cp = pltpu.make_async_remote_copy(src_ref=p_ref, dst_ref=slots.at[me], send_sem=send_sems.at[rel - 1],
                                              recv_sem=recv_sems.at[rel - 1], device_id=_peer(x, y, c, rel), device_id_type=MESH)
            cp.start()
            copies.append(cp)
        for rel in range(1, NDEV):
            src_dev = _dev_index(*_peer(x, y, c, rel))
            pltpu.make_async_remote_copy(src_ref=p_ref, dst_ref=slots.at[src_dev], send_sem=send_sems.at[rel - 1],
                                         recv_sem=recv_sems.at[rel - 1], device_id=_peer(x, y, c, rel), device_id_type=MESH).wait_recv()
        for cp in copies:
            cp.wait_send()
        total = slots[0]
        for d in range(1, NDEV):
            total = total + slots[d]
        o_ref[...] = total

    return pl.pallas_call(
        body, out_shape=jax.ShapeDtypeStruct(part.shape, F32), name="all_sum_small",
        in_specs=[pl.BlockSpec(memory_space=pltpu.VMEM)], out_specs=pl.BlockSpec(memory_space=pltpu.VMEM),
        scratch_shapes=[pltpu.VMEM((NDEV,) + part.shape, F32), pltpu.SemaphoreType.DMA((NDEV - 1,)), pltpu.SemaphoreType.DMA((NDEV - 1,))],
        compiler_params=pltpu.CompilerParams(has_side_effects=True),
    )(part)


def _adamw_math(w, g, m, v):
    m = ADAM_B1 * m + (1.0 - ADAM_B1) * g
    v = ADAM_B2 * v + (1.0 - ADAM_B2) * (g * g)
    m_hat = m / (1.0 - ADAM_B1 ** ADAM_STEP)
    v_hat = v / (1.0 - ADAM_B2 ** ADAM_STEP)
    delta = -ADAM_LR * (m_hat / (jnp.sqrt(v_hat) + ADAM_EPS) + ADAM_WD * w)
    return delta, m, v


def _sum_adamw(parts, w, m, v, name):
    R, C = w.shape
    n_parts = len(parts)
    rows = R // n_parts
    Cp = parts[0].shape[2]
    tr = max(t for t in range(8, 257, 8) if rows % t == 0)
    per = rows // tr

    def body(*refs):
        p_refs = refs[:n_parts]
        w_ref, m_ref, v_ref, g_ref, d_ref, mo_ref, vo_ref = refs[n_parts:]
        t = pl.program_id(0)
        for k, p_ref in enumerate(p_refs):
            @pl.when((t >= k * per) & (t < (k + 1) * per))
            def _():
                g = p_ref[0, :, pl.ds(0, C)].astype(F32)
                for d in range(1, NDEV):
                    g = g + p_ref[d, :, pl.ds(0, C)].astype(F32)
                g_ref[...] = g
                d_ref[...], mo_ref[...], vo_ref[...] = _adamw_math(w_ref[...], g, m_ref[...], v_ref[...])

    def part_spec(k):
        return pl.BlockSpec((NDEV, tr, Cp), lambda t: (0, jnp.clip(t - k * per, 0, per - 1), 0))

    row = pl.BlockSpec((tr, C), lambda t: (t, 0))
    return pl.pallas_call(
        body, grid=(R // tr,), name=name, in_specs=[part_spec(k) for k in range(n_parts)] + [row, row, row],
        out_specs=[row] * 4, out_shape=[jax.ShapeDtypeStruct((R, C), F32)] * 4, compiler_params=_params("parallel"),
    )(*parts, w, m, v)


def _adamw_small(w, g, m, v, name):
    def body(w_ref, g_ref, m_ref, v_ref, d_ref, mo_ref, vo_ref):
        d_ref[...], mo_ref[...], vo_ref[...] = _adamw_math(w_ref[...], g_ref[...], m_ref[...], v_ref[...])
    return pl.pallas_call(body, out_shape=[jax.ShapeDtypeStruct(w.shape, F32)] * 3, name=name, compiler_params=_params())(w, g, m, v)


def _pack_small(parts):
    rows = [jnp.reshape(parts[name], (size // 128, 128)) for name, size in SMALL]
    used = sum(size // 128 for _, size in SMALL)
    rows.append(jnp.zeros((SMALL_ROWS - used, 128), F32))
    return jnp.concatenate(rows, axis=0)


def _unpack_small(packed):
    out, r = {}, 0
    for name, size in SMALL:
        out[name] = packed[r:r + size // 128]
        r += size // 128
    return out


def kernel(x, mem, ffn1_norm, ffn1_w_up, ffn1_w_down, mix_norm, mem_norm, w_in, b_gate, conv_dw_w, conv_dw_b, conv_ln_g, conv_ln_b, conv_w_pw, att_rel_bias, att_w_o, mem_w_kv, mem_w_o, w_out, ffn2_norm, ffn2_w_up, ffn2_w_down, final_norm, loss_target, m_ffn1_norm, m_ffn1_w_up, m_ffn1_w_down, m_mix_norm, m_mem_norm, m_w_in, m_b_gate, m_conv_dw_w, m_conv_dw_b, m_conv_ln_g, m_conv_ln_b, m_conv_w_pw, m_att_rel_bias, m_att_w_o, m_mem_w_kv, m_mem_w_o, m_w_out, m_ffn2_norm, m_ffn2_w_up, m_ffn2_w_down, m_final_norm, v_ffn1_norm, v_ffn1_w_up, v_ffn1_w_down, v_mix_norm, v_mem_norm, v_w_in, v_b_gate, v_conv_dw_w, v_conv_dw_b, v_conv_ln_g, v_conv_ln_b, v_conv_w_pw, v_att_rel_bias, v_att_w_o, v_mem_w_kv, v_mem_w_o, v_w_out, v_ffn2_norm, v_ffn2_w_up, v_ffn2_w_down, v_final_norm):
    given = dict(locals())
    w = {n: given[n] for n in WEIGHTS}
    mom = {n: given["m_" + n] for n in WEIGHTS}
    var = {n: given["v_" + n] for n in WEIGHTS}

    NB, S, _ = x.shape
    T = NB * S
    ML = mem.shape[1]
    x0 = x.reshape(T, D)
    target = loss_target.reshape(T, D)
    mem2 = mem.reshape(NB * ML, D)

    sh = dict(zip(BIG_ORDER, _cast_shards([w[n][0] for n in BIG_ORDER])))
    zeros_pad = jnp.zeros((FF_SHARD_PAD - 2 * FF_HALF_ROWS, D), BF16)
    dw_t = jnp.transpose(conv_dw_w[0])

    def gather(names, extra=(), extra_kinds=()):
        return _gather_ride([sh[n] for n in names] + list(extra), [BIG[n] for n in names] + list(extra_kinds), zeros_pad)

    W = {}
    names0 = ['ffn1_w_up', 'ffn1_w_down']
    got = _exchange_alone(gather(names0, [dw_t], [('row', dw_t.shape[0])]), "gather_ffn1")
    W.update(zip(names0, got[:2]))
    dw_full = jnp.transpose(got[2])
    conv_vec = jnp.concatenate([conv_dw_b, conv_ln_g, conv_ln_b, jnp.zeros((5, CONV_W), F32)], axis=0)
    tab = _bias_table(att_rel_bias[0])
    fin_g = final_norm.reshape(1, D)

    names1 = ['w_in', 'conv_w_pw', 'att_w_o', 'mem_w_kv', 'mem_w_o', 'w_out']
    (x1, ab1), got = _ffn_fwd(x0, ffn1_norm, W['ffn1_w_up'], W['ffn1_w_down'], TILE_FFN, "ffn1_fwd", ride=gather(names1))
    W.update(zip(names1, got))
    names2 = ['ffn2_w_up', 'ffn2_w_down']
    (uc, qkv, mq, gl, hmix), got = _mix_fwd(x1, mix_norm, W['w_in'], TILE_TOKENS, ride=gather(names2))
    W.update(zip(names2, got))
    uc3 = uc.reshape(NB, S, 2 * CONV_W)
    qkv3 = qkv.reshape(NB, S, 3 * ATT_W)
    mq3 = mq.reshape(NB, S, MEM_W)
    cact = _conv_fwd(uc3, dw_full, conv_vec).reshape(T, CONV_W)
    oatt = _att_fwd(qkv3, tab).reshape(T, ATT_W)
    memh, kv = _memkv_fwd(mem2, mem_norm, W['mem_w_kv'], TILE_TOKENS)
    kv3 = kv.reshape(NB, ML, 2 * MEM_W)
    omem = _mematt_fwd(mq3, kv3, TILE_TOKENS).reshape(T, MEM_W)
    branch_w = (W['conv_w_pw'], W['att_w_o'], W['mem_w_o'], W['w_out'])
    x2, ymix = _combine_fwd(x1, cact, oatt, omem, gl, b_gate, *branch_w, TILE_COMBINE)
    (x3, ab2), _ = _ffn_fwd(x2, ffn2_norm, W['ffn2_w_up'], W['ffn2_w_down'], TILE_FFN, "ffn2_fwd")
    loss_part, dx3, dg_final = _final(x3, fin_g, target, TILE_TOKENS)

    def scatter(grads, names):
        return _scatter_ride(grads, [BIG[n] for n in names])

    G, P = {}, {}
    dx2, dab2, act2, h2, dg_ffn2 = _ffn_bwd(x2, dx3, ab2, ffn2_norm, W['ffn2_w_up'], W['ffn2_w_down'], TILE_FFN, "ffn2_bwd")
    G['ffn2_w_up'], _ = _tn_matmul(h2, dab2, 768, "grad_ffn2_w_up")
    G['ffn2_w_down'], _ = _tn_matmul(act2, dx3, 512, "grad_ffn2_w_down", scale=0.5, tt=TILE_GRAD_TOKENS_WIDE)
    (dgl, dcact, doatt, domem, dyc, dya, dym, dbg), got = _combine_bwd(
        dx2, cact, oatt, omem, gl, b_gate, *branch_w, TILE_COMBINE, ride=scatter([G['ffn2_w_up']], ['ffn2_w_up']))
    P['ffn2_w_up'] = got
    G['w_out'], _ = _tn_matmul(ymix, dx2, 512, "grad_w_out")
    G['conv_w_pw'], _ = _tn_matmul(cact, dyc, 512, "grad_conv_w_pw")
    G['att_w_o'], _ = _tn_matmul(oatt, dya, 512, "grad_att_w_o")
    G['mem_w_o'], _ = _tn_matmul(omem, dym, 512, "grad_mem_w_o")
    dmq3, dkv3 = _mematt_bwd(mq3, kv3, domem.reshape(NB, S, MEM_W), TILE_TOKENS)
    dkv = dkv3.reshape(NB * ML, 2 * MEM_W)
    dg_mem = _memkv_bwd(mem2, dkv, W['mem_w_kv'], TILE_TOKENS)
    G['mem_w_kv'], _ = _tn_matmul(memh, dkv, 512, "grad_mem_w_kv")
    names = ['ffn2_w_down', 'w_out', 'conv_w_pw', 'att_w_o', 'mem_w_o']
    (dqkv3, dscore), got = _att_bwd(qkv3, doatt.reshape(NB, S, ATT_W), tab, ride=scatter([G[n] for n in names], names))
    P.update((n, [p]) for n, p in zip(names, got))
    d_rel = _rel_bias_grad(dscore)
    (duc3, d_dw, d_cvec), got = _conv_bwd(uc3, dcact.reshape(NB, S, CONV_W), dw_full, conv_vec,
                                          ride=scatter([G['mem_w_kv']], ['mem_w_kv']))
    P['mem_w_kv'] = got
    duc, dqkv, dmq = duc3.reshape(T, 2 * CONV_W), dqkv3.reshape(T, 3 * ATT_W), dmq3.reshape(T, MEM_W)
    g_in, _ = _tn_matmul(hmix, duc, 512, "grad_w_in_conv", out_cols=IN_COLS, col_off=0)
    g_in, _ = _tn_matmul(hmix, dqkv, 512, "grad_w_in_qkv", out_cols=IN_COLS, col_off=1024, prev=g_in)
    g_in, _ = _tn_matmul(hmix, dmq, 512, "grad_w_in_mq", out_cols=IN_COLS, col_off=2560, prev=g_in)
    G['w_in'], _ = _tn_matmul(hmix, dgl, 512, "grad_w_in_gate", out_cols=IN_COLS, col_off=3072, prev=g_in)
    (dx1, dg_mix), got = _mix_bwd(x1, dx2, duc, dqkv, dmq, dgl, mix_norm, W['w_in'], TILE_TOKENS,
                                  ride=scatter([G['w_in']], ['w_in']))
    P['w_in'] = got
    dx0, dab1, act1, h1, dg_ffn1 = _ffn_bwd(x0, dx1, ab1, ffn1_norm, W['ffn1_w_up'], W['ffn1_w_down'], TILE_FFN, "ffn1_bwd")
    g_wd1, _ = _tn_matmul(act1, dx1, 512, "grad_ffn1_w_down", scale=0.5, tt=TILE_GRAD_TOKENS_WIDE)
    g_wu1a, got = _tn_matmul(h1, dab1, 768, "grad_ffn1_w_up_a", x_part=(0, 2), ride=scatter([g_wd1], ['ffn1_w_down']))
    P['ffn1_w_down'] = got
    g_wu1b, got_a = _tn_matmul(h1, dab1, 768, "grad_ffn1_w_up_b", x_part=(1, 2), ride=scatter([g_wu1a], ['ffn1_w_up']))
    got_b = _exchange_alone(scatter([g_wu1b], ['ffn1_w_up']), "scatter_last")
    P['ffn1_w_up'] = [got_a[0], got_b[0]]

    small = _unpack_small(_all_sum_small(_pack_small({
        'loss': loss_part, 'ffn1_norm': dg_ffn1, 'mix_norm': dg_mix, 'mem_norm': dg_mem, 'b_gate': dbg,
        'conv_dw_w': d_dw[:CONV_K], 'conv_dw_b': d_cvec[0], 'conv_ln_g': d_cvec[1], 'conv_ln_b': d_cvec[2],
        'att_rel_bias': d_rel, 'ffn2_norm': dg_ffn2, 'final_norm': dg_final})))
    loss = small['loss'][0, 0]
    me = _dev_index(*_my_coords())
    grad, delta, new_m, new_v = {}, {}, {}, {}
    for n in WEIGHTS:
        shape = w[n].shape
        if n in BIG:
            g_, d_, m_, v_ = _sum_adamw(P[n], w[n][0], mom[n][0], var[n][0], "adamw_" + n)
        else:
            if n == 'conv_dw_w':
                g_ = lax.dynamic_slice(small[n].reshape(CONV_K, CONV_W), (0, me * shape[2]), (CONV_K, shape[2]))
            else:
                g_ = small[n]
            rows = g_.shape
            d_, m_, v_ = _adamw_small(w[n].reshape(rows), g_, mom[n].reshape(rows), var[n].reshape(rows), "adamw_" + n)
        grad[n], delta[n], new_m[n], new_v[n] = (t.reshape(shape) for t in (g_, d_, m_, v_))

    return (loss, dx0.reshape(NB, S, D), *[grad[n] for n in WEIGHTS], *[delta[n] for n in WEIGHTS],
            *[new_m[n] for n in WEIGHTS], *[new_v[n] for n in WEIGHTS])
```

```python
import functools

import jax
import jax.numpy as jnp
from jax import lax
from jax.experimental import pallas as pl
from jax.experimental.pallas import tpu as pltpu

F32 = jnp.float32
BF16 = jnp.bfloat16

EPS = 1e-6
MASK_VALUE = -1e30
D = 1024
NDEV = 8
FF = 2816
FF_SHARD = 704
FF_HALF_ROWS = 352
FF_BLOCK_EDGES = (1280,)
IN_COLS = 6144
CONV_W = 512
CONV_K = 31
CONV_HALO = 32
CONV_CHUNK = 32
CONV_WIN = CONV_CHUNK + 40
GLU_CHUNK = 128
ATT_W = 512
ATT_HEADS = 8
ATT_HD = 64
CHUNK = 64
LEFT_CHUNKS = 8
MAX_REL = 128
N_REL = 192
QB = 256
KWIN = QB + LEFT_CHUNKS * CHUNK
KPAD = LEFT_CHUNKS * CHUNK
DS_LANES = 1024
MEM_W = 512
MEM_HEADS = 4
MEM_HD = 128
ADAM_LR = 0.001
ADAM_B1 = 0.9
ADAM_B2 = 0.999
ADAM_EPS = 1e-08
ADAM_WD = 0.01
ADAM_STEP = 10
VMEM_LIMIT = 60 * 1024 * 1024
TILE_FFN = 256
TILE_COMBINE = 256
TILE_TOKENS = 512
TILE_GRAD_TOKENS = 2048
TILE_GRAD_TOKENS_WIDE = 1024
TILE_GRAD_TOKENS_WIDEST = 512

MESH = pl.DeviceIdType.MESH
ANY = pl.BlockSpec(memory_space=pl.ANY)

WEIGHTS = ['ffn1_norm', 'ffn1_w_up', 'ffn1_w_down', 'mix_norm', 'mem_norm', 'w_in', 'b_gate', 'conv_dw_w', 'conv_dw_b',
           'conv_ln_g', 'conv_ln_b', 'conv_w_pw', 'att_rel_bias', 'att_w_o', 'mem_w_kv', 'mem_w_o', 'w_out', 'ffn2_norm',
           'ffn2_w_up', 'ffn2_w_down', 'final_norm']
BIG = {
    'ffn1_w_up': ('row', FF_SHARD), 'ffn1_w_down': ('row', FF_HALF_ROWS), 'w_in': ('col', 768),
    'conv_w_pw': ('col', 128), 'att_w_o': ('col', 128), 'mem_w_kv': ('row', 128), 'mem_w_o': ('col', 128),
    'w_out': ('row', 128), 'ffn2_w_up': ('row', FF_SHARD), 'ffn2_w_down': ('row', FF_HALF_ROWS),
}
BIG_ORDER = ['ffn1_w_up', 'ffn1_w_down', 'w_in', 'conv_w_pw', 'att_w_o', 'mem_w_kv', 'mem_w_o', 'w_out', 'ffn2_w_up', 'ffn2_w_down']
TRANSPOSED = ('ffn1_w_up', 'ffn2_w_up')


def _dot(a, b):
    return jnp.dot(a, b, preferred_element_type=F32)


def _dot_nt(a, b):
    return lax.dot_general(a, b, (((1,), (1,)), ((), ())), preferred_element_type=F32)


def _dot_tn(a, b):
    return lax.dot_general(a, b, (((0,), (0,)), ((), ())), preferred_element_type=F32)


def _sigmoid(v):
    return jax.nn.sigmoid(v)


def _const(shape):
    return pl.BlockSpec(shape, lambda *_: (0,) * len(shape), pipeline_mode=pl.Buffered(1))


def _params(*sem):
    return pltpu.CompilerParams(dimension_semantics=sem if sem else None, vmem_limit_bytes=VMEM_LIMIT)


def _my_coords():
    return lax.axis_index("x"), lax.axis_index("y"), lax.axis_index("c")


def _dev_index(px, py, pc):
    return 4 * px + 2 * py + pc


def _window(ref, kind, n, p):
    if kind == 'row':
        return ref.at[pl.ds(pl.multiple_of(p * n, n), n), :]
    return ref.at[:, pl.ds(pl.multiple_of(p * n, 128), n)]


def _full_shape(kind, n, shard_shape):
    if kind == 'row':
        return (NDEV * n, shard_shape[1])
    return (shard_shape[0], NDEV * n)


def _cast_shards(shards):
    n = len(shards)

    def body(*refs):
        for i in range(n):
            refs[n + i][...] = refs[i][...].astype(BF16)

    out_shape = [jax.ShapeDtypeStruct(s.shape, BF16) for s in shards]
    return pl.pallas_call(body, out_shape=out_shape, name="cast_shards", compiler_params=_params())(*shards)


class _Ride:
    def __init__(self, inputs, out_shape, scratch, start, finish, mid=None):
        self.inputs, self.out_shape, self.scratch = list(inputs), list(out_shape), list(scratch)
        self.start, self.finish, self.mid = start, finish, mid


def _pallas(body, name, grid, in_specs, out_specs, out_shape, args, scratch_shapes=(), sem=None, aliases=None, ride=None):
    if ride is None:
        outs = pl.pallas_call(body, grid=grid, name=name, in_specs=in_specs, out_specs=out_specs, out_shape=out_shape,
                              scratch_shapes=list(scratch_shapes), input_output_aliases=aliases or {},
                              compiler_params=_params(*sem))(*args)
        return list(outs), []
    n_in, n_out, n_scr = len(args), len(out_shape), len(scratch_shapes)
    r_in, r_out = len(ride.inputs), len(ride.out_shape)

    def wrapped(*refs):
        k_in, rin = refs[:n_in], refs[n_in:n_in + r_in]
        o0 = n_in + r_in
        k_out, rout = refs[o0:o0 + n_out], refs[o0 + n_out:o0 + n_out + r_out]
        s0 = o0 + n_out + r_out
        k_scr, rscr = refs[s0:s0 + n_scr], refs[s0 + n_scr:]
        ids = [pl.program_id(k) for k in range(len(grid))]
        first = functools.reduce(jnp.logical_and, [i == 0 for i in ids])
        last = functools.reduce(jnp.logical_and, [i == g - 1 for i, g in zip(ids, grid)])
        pl.when(first)(lambda: ride.start(rin, rout, rscr))
        if ride.mid is not None:
            at_mid = functools.reduce(jnp.logical_and, [ids[0] == (3 * grid[0]) // 4] + [i == 0 for i in ids[1:]])
            pl.when(at_mid)(lambda: ride.mid(rin, rout, rscr))
        body(*k_in, *k_out, *k_scr)
        pl.when(last)(lambda: ride.finish(rin, rout, rscr))

    outs = pl.pallas_call(
        wrapped, grid=grid, name=name, in_specs=list(in_specs) + [ANY] * r_in, out_specs=list(out_specs) + [ANY] * r_out,
        out_shape=list(out_shape) + ride.out_shape, scratch_shapes=list(scratch_shapes) + ride.scratch,
        input_output_aliases=aliases or {}, compiler_params=_params(*(["arbitrary"] * len(grid))),
    )(*args, *ride.inputs)
    return list(outs[:n_out]), list(outs[n_out:])


def _exchange_alone(ride, name):
    r_in, r_out = len(ride.inputs), len(ride.out_shape)

    def body(*refs):
        rin, rout, rscr = refs[:r_in], refs[r_in:r_in + r_out], refs[r_in + r_out:]
        ride.start(rin, rout, rscr)
        if ride.mid is not None:
            ride.mid(rin, rout, rscr)
        ride.finish(rin, rout, rscr)

    return pl.pallas_call(body, out_shape=ride.out_shape, in_specs=[ANY] * r_in, out_specs=[ANY] * r_out, name=name,
                          scratch_shapes=ride.scratch, compiler_params=pltpu.CompilerParams(has_side_effects=True))(*ride.inputs)


def _gather_ride(shards, kinds):
    n = len(shards)

    def plan(rin, out, sems):
        send_sems, recv_sems, local_sems = sems[:3]
        x, y, c = _my_coords()
        me, sibling = (x, y, c), (x, y, 1 - c)
        chips = [(1 - x, y), (x, 1 - y), (1 - x, 1 - y)]

        def win(i, dev):
            return _window(out[i], kinds[i][0], kinds[i][1], _dev_index(*dev))

        def copy(i, k, block, to, from_shard=False):
            return pltpu.make_async_remote_copy(
                src_ref=rin[i] if from_shard else win(i, block), dst_ref=win(i, block),
                send_sem=send_sems.at[i, k], recv_sem=recv_sems.at[i, k], device_id=to, device_id_type=MESH)

        def local():
            return [pltpu.make_async_copy(rin[i], win(i, me), local_sems.at[i]) for i in range(n)]

        def first():
            cps = []
            for i in range(n):
                cps.append(copy(i, 0, me, sibling, from_shard=True))
                cps += [copy(i, 1 + j, me, (*chip, c), from_shard=True) for j, chip in enumerate(chips)]
            return cps

        def arrived():
            return [copy(i, 1 + j, (*chip, c), me) for j, chip in enumerate(chips) for i in range(n)]

        def passed():
            return [copy(i, 4 + j, (*chip, c), sibling) for j, chip in enumerate(chips) for i in range(n)]

        def from_sibling():
            cps = [copy(i, 0, sibling, me) for i in range(n)]
            return cps + [copy(i, 4 + j, (*chip, 1 - c), me) for i in range(n) for j, chip in enumerate(chips)]

        return local, first, arrived, passed, from_sibling

    def start(rin, out, sems):
        local, first, _, _, _ = plan(rin, out, sems)
        for cp in local() + first():
            cp.start()

    def mid(rin, out, sems):
        _, _, arrived, passed, _ = plan(rin, out, sems)
        for got, fwd in zip(arrived(), passed()):
            got.wait_recv()
            fwd.start()

    def finish(rin, out, sems):
        local, first, _, passed, from_sibling = plan(rin, out, sems)
        for cp in from_sibling():
            cp.wait_recv()
        for cp in first() + passed():
            cp.wait_send()
        for cp in local():
            cp.wait()

    out_shape = [jax.ShapeDtypeStruct(_full_shape(k, m, s.shape), s.dtype) for s, (k, m) in zip(shards, kinds)]
    scratch = [pltpu.SemaphoreType.DMA((n, 7)), pltpu.SemaphoreType.DMA((n, 7)), pltpu.SemaphoreType.DMA((n,))]
    return _Ride(shards, out_shape, scratch, start, finish, mid)


def _scatter_ride(grads, kinds):
    n = len(grads)

    def plan(g, out, sems):
        send_sems, recv_sems, local_sems = sems
        x, y, c = _my_coords()
        me = _dev_index(x, y, c)

        def local():
            return [pltpu.make_async_copy(_window(g[i], kinds[i][0], kinds[i][1], me), out[i].at[me], local_sems.at[i])
                    for i in range(n)]

        def remote(arrival):
            cps = []
            for rel in range(1, NDEV):
                peer = _peer(x, y, c, rel)
                dev = _dev_index(*peer)
                for i in range(n):
                    kind, m = kinds[i]
                    cps.append(pltpu.make_async_remote_copy(
                        src_ref=_window(g[i], kind, m, me if arrival else dev), dst_ref=out[i].at[dev if arrival else me],
                        send_sem=send_sems.at[i, rel - 1], recv_sem=recv_sems.at[i, rel - 1], device_id=peer, device_id_type=MESH))
            return cps

        return local, remote

    def start(g, out, sems):
        local, remote = plan(g, out, sems)
        for cp in local() + remote(False):
            cp.start()

    def finish(g, out, sems):
        local, remote = plan(g, out, sems)
        for cp in remote(True):
            cp.wait_recv()
        for cp in remote(False):
            cp.wait_send()
        for cp in local():
            cp.wait()

    def block_shape(gr, kind, m):
        return (m, gr.shape[1]) if kind == 'row' else (gr.shape[0], m)

    out_shape = [jax.ShapeDtypeStruct((NDEV,) + block_shape(gr, k, m), gr.dtype) for gr, (k, m) in zip(grads, kinds)]
    scratch = [pltpu.SemaphoreType.DMA((n, NDEV - 1)), pltpu.SemaphoreType.DMA((n, NDEV - 1)), pltpu.SemaphoreType.DMA((n,))]
    return _Ride(grads, out_shape, scratch, start, finish)


def _rms_stats(xf):
    r = lax.rsqrt(jnp.mean(xf * xf, axis=-1, keepdims=True) + EPS)
    return xf * r, r


def _rms_bwd(dh, g, xhat, r):
    dxhat = dh * g
    return r * (dxhat - xhat * jnp.mean(dxhat * xhat, axis=-1, keepdims=True))


def _ffn_blocks():
    edges = (0,) + FF_BLOCK_EDGES + (FF,)
    return [(slice(lo, hi), slice(FF + lo, FF + hi)) for lo, hi in zip(edges[:-1], edges[1:])]


def _ffn_fwd(x, g, wut, wd, tm, name, ride=None):
    T = x.shape[0]

    def body(x_ref, g_ref, wut_ref, wd_ref, xo_ref, ab_ref):
        xf = x_ref[...]
        xhat, _ = _rms_stats(xf)
        h = (xhat * g_ref[...]).astype(BF16)
        acc = jnp.zeros((tm, D), F32)
        for ra, rb in _ffn_blocks():
            a = _dot_nt(h, wut_ref[ra, :])
            b = _dot_nt(h, wut_ref[rb, :])
            ab_ref[:, ra] = a.astype(BF16)
            ab_ref[:, rb] = b.astype(BF16)
            act = (a * _sigmoid(a) * b).astype(BF16)
            acc = acc + _dot(act, wd_ref[ra, :])
        xo_ref[...] = xf + 0.5 * acc

    return _pallas(
        body, name, (T // tm,),
        [pl.BlockSpec((tm, D), lambda t: (t, 0)), _const((1, D)), _const((2 * FF, D)), _const((FF, D))],
        [pl.BlockSpec((tm, D), lambda t: (t, 0)), pl.BlockSpec((tm, 2 * FF), lambda t: (t, 0))],
        [jax.ShapeDtypeStruct((T, D), F32), jax.ShapeDtypeStruct((T, 2 * FF), BF16)],
        (x, g, wut, wd), sem=("arbitrary",), ride=ride)


def _ffn_bwd(x, dy, ab, g, wut, wd, tm, name):
    T = x.shape[0]

    def body(x_ref, dy_ref, ab_ref, g_ref, wut_ref, wd_ref, dx_ref, dab_ref, act_ref, h_ref, dg_ref):
        xf = x_ref[...]
        xhat, r = _rms_stats(xf)
        gain = g_ref[...]
        h_ref[...] = (xhat * gain).astype(BF16)
        dy = dy_ref[...]
        dyh = (0.5 * dy).astype(BF16)
        dh = jnp.zeros((tm, D), F32)
        for ra, rb in _ffn_blocks():
            a = ab_ref[:, ra].astype(F32)
            b = ab_ref[:, rb].astype(F32)
            dact = _dot_nt(dyh, wd_ref[ra, :])
            sg = _sigmoid(a)
            sl = a * sg
            act_ref[:, ra] = (sl * b).astype(BF16)
            da = (dact * b * (sg * (1.0 + a * (1.0 - sg)))).astype(BF16)
            db = (dact * sl).astype(BF16)
            dab_ref[:, ra] = da
            dab_ref[:, rb] = db
            dh = dh + _dot(da, wut_ref[ra, :]) + _dot(db, wut_ref[rb, :])
        dx_ref[...] = dy + _rms_bwd(dh, gain, xhat, r)

        @pl.when(pl.program_id(0) == 0)
        def _():
            dg_ref[...] = jnp.zeros_like(dg_ref)
        dg_ref[...] += jnp.sum(dh * xhat, axis=0, keepdims=True)

    return pl.pallas_call(
        body, grid=(T // tm,), name=name,
        in_specs=[pl.BlockSpec((tm, D), lambda t: (t, 0)), pl.BlockSpec((tm, D), lambda t: (t, 0)),
                  pl.BlockSpec((tm, 2 * FF), lambda t: (t, 0)), _const((1, D)), _const((2 * FF, D)), _const((FF, D))],
        out_specs=[pl.BlockSpec((tm, D), lambda t: (t, 0)), pl.BlockSpec((tm, 2 * FF), lambda t: (t, 0)),
                   pl.BlockSpec((tm, FF), lambda t: (t, 0)), pl.BlockSpec((tm, D), lambda t: (t, 0)),
                   pl.BlockSpec((1, D), lambda t: (0, 0))],
        out_shape=[jax.ShapeDtypeStruct((T, D), F32), jax.ShapeDtypeStruct((T, 2 * FF), BF16),
                   jax.ShapeDtypeStruct((T, FF), BF16), jax.ShapeDtypeStruct((T, D), BF16), jax.ShapeDtypeStruct((1, D), F32)],
        compiler_params=_params("arbitrary"),
    )(x, dy, ab, g, wut, wd)


def _tn_matmul(xm, ym, tn, name, scale=None, out_cols=None, col_off=0, prev=None, tt=TILE_GRAD_TOKENS, x_part=(0, 1),
               out_rows=None, y_part=(0, 1), ride=None):
    T = xm.shape[0]
    xi, xn = x_part
    yi, yn = y_part
    K = xm.shape[1] // xn
    N = ym.shape[1] // yn
    out_cols = N if out_cols is None else out_cols
    row_blk = xi if out_rows is not None else 0
    out_rows = K if out_rows is None else out_rows
    tt = min(tt, T)
    nt = T // tt
    off = col_off // tn

    def body(*refs):
        x_ref, y_ref = refs[0], refs[1]
        o_ref, acc = refs[-2], refs[-1]

        @pl.when(pl.program_id(1) == 0)
        def _():
            acc[...] = jnp.zeros_like(acc)
        acc[...] += _dot_tn(x_ref[...].astype(BF16), y_ref[...].astype(BF16))

        @pl.when(pl.program_id(1) == nt - 1)
        def _():
            res = acc[...]
            o_ref[...] = (res if scale is None else res * scale).astype(BF16)

    ycol = yi * (N // tn)
    in_specs = [pl.BlockSpec((tt, K), lambda n, t: (t, xi)), pl.BlockSpec((tt, tn), lambda n, t: (t, n + ycol))]
    args = [xm, ym]
    aliases = {}
    if prev is not None:
        in_specs.append(ANY)
        args.append(prev)
        aliases = {2: 0}
    outs, rode = _pallas(
        body, name, (N // tn, nt), in_specs, [pl.BlockSpec((K, tn), lambda n, t: (row_blk, n + off))],
        [jax.ShapeDtypeStruct((out_rows, out_cols), BF16)], args, scratch_shapes=[pltpu.VMEM((K, tn), F32)],
        sem=("parallel", "arbitrary"), aliases=aliases, ride=ride)
    return outs[0], rode


def _mix_fwd(x, g, w_in, tm, ride=None):
    T = x.shape[0]

    def body(x_ref, g_ref, w_ref, uc_ref, qkv_ref, mq_ref, gl_ref, h_ref):
        xhat, _ = _rms_stats(x_ref[...])
        h = (xhat * g_ref[...]).astype(BF16)
        h_ref[...] = h
        uc_ref[...] = _dot(h, w_ref[:, 0:1024])
        qkv_ref[...] = _dot(h, w_ref[:, 1024:2560]).astype(BF16)
        mq_ref[...] = _dot(h, w_ref[:, 2560:3072]).astype(BF16)
        for j in range(3):
            gl_ref[:, j * D:(j + 1) * D] = _dot(h, w_ref[:, 3072 + j * D:3072 + (j + 1) * D])

    row = lambda w: pl.BlockSpec((tm, w), lambda t: (t, 0))
    return _pallas(
        body, "mix_fwd", (T // tm,), [row(D), _const((1, D)), _const((D, IN_COLS))],
        [row(1024), row(1536), row(512), row(3072), row(D)],
        [jax.ShapeDtypeStruct((T, 1024), F32), jax.ShapeDtypeStruct((T, 1536), BF16), jax.ShapeDtypeStruct((T, 512), BF16),
         jax.ShapeDtypeStruct((T, 3072), F32), jax.ShapeDtypeStruct((T, D), BF16)],
        (x, g, w_in), sem=("parallel",), ride=ride)


def _mix_bwd(x, dres, duc, dqkv, dmq, dgl, g, w_in, tm, ride=None):
    T = x.shape[0]

    def body(x_ref, dres_ref, duc_ref, dqkv_ref, dmq_ref, dgl_ref, g_ref, w_ref, dx_ref, dg_ref):
        xhat, r = _rms_stats(x_ref[...])
        dh = _dot_nt(duc_ref[...], w_ref[:, 0:1024])
        dh = dh + _dot_nt(dqkv_ref[...], w_ref[:, 1024:2560])
        dh = dh + _dot_nt(dmq_ref[...], w_ref[:, 2560:3072])
        dh = dh + _dot_nt(dgl_ref[...], w_ref[:, 3072:6144])
        dx_ref[...] = dres_ref[...] + _rms_bwd(dh, g_ref[...], xhat, r)

        @pl.when(pl.program_id(0) == 0)
        def _():
            dg_ref[...] = jnp.zeros_like(dg_ref)
        dg_ref[...] += jnp.sum(dh * xhat, axis=0, keepdims=True)

    row = lambda w: pl.BlockSpec((tm, w), lambda t: (t, 0))
    return _pallas(
        body, "mix_bwd", (T // tm,),
        [row(D), row(D), row(1024), row(1536), row(512), row(3072), _const((1, D)), _const((D, IN_COLS))],
        [row(D), pl.BlockSpec((1, D), lambda t: (0, 0))],
        [jax.ShapeDtypeStruct((T, D), F32), jax.ShapeDtypeStruct((1, D), F32)],
        (x, dres, duc, dqkv, dmq, dgl, g, w_in), sem=("arbitrary",), ride=ride)


def _shifted(win, base, copies):
    for k in range(8):
        copies[k] = win[base + k:base + k + CONV_CHUNK + 24]
    return copies


def _tap_slices(copies, tap):
    out = []
    for k in range(8):
        for a in range(4):
            j = tap(a, k)
            if 0 <= j < CONV_K:
                out.append((j, copies[k, pl.ds(8 * a, CONV_CHUNK), :]))
    return out


def _conv_taps(copies, w_ref, tap):
    acc = jnp.zeros((CONV_CHUNK, CONV_W), F32)
    for j, rows in _tap_slices(copies, tap):
        acc = acc + rows * w_ref[j:j + 1, :]
    return acc


def _fold8(v):
    acc = v[0:8]
    for r in range(8, CONV_CHUNK, 8):
        acc = acc + v[r:r + 8]
    return acc


def _glu_into(uc_ref, vpad, S):
    vpad[pl.ds(0, CONV_HALO), :] = jnp.zeros((CONV_HALO, CONV_W), F32)
    vpad[pl.ds(S + CONV_HALO, CONV_HALO), :] = jnp.zeros((CONV_HALO, CONV_W), F32)

    def glu(i, carry):
        r0 = pl.multiple_of(i * GLU_CHUNK, GLU_CHUNK)
        a = uc_ref[0, pl.ds(r0, GLU_CHUNK), 0:CONV_W]
        gt = uc_ref[0, pl.ds(r0, GLU_CHUNK), CONV_W:2 * CONV_W]
        vpad[pl.ds(pl.multiple_of(r0 + CONV_HALO, CONV_HALO), GLU_CHUNK), :] = a * _sigmoid(gt)
        return carry
    lax.fori_loop(0, S // GLU_CHUNK, glu, 0)


def _conv_ln(vpad, w_ref, vec_ref, r0, copies):
    win = vpad[pl.ds(r0, CONV_WIN), :]
    z = _conv_taps(_shifted(win, CONV_HALO - (CONV_K - 1), copies), w_ref, lambda a, k: 8 * a + k) + vec_ref[0:1, :]
    xc = z - jnp.mean(z, axis=-1, keepdims=True)
    rstd = lax.rsqrt(jnp.mean(xc * xc, axis=-1, keepdims=True) + EPS)
    xn = xc * rstd
    return xn, rstd, xn * vec_ref[1:2, :] + vec_ref[2:3, :]


def _conv_fwd(uc, dw_w, vec):
    NB, S, _ = uc.shape

    def body(uc_ref, w_ref, vec_ref, o_ref, vpad, copies):
        _glu_into(uc_ref, vpad, S)

        def conv(i, carry):
            r0 = pl.multiple_of(i * CONV_CHUNK, CONV_CHUNK)
            _, _, yln = _conv_ln(vpad, w_ref, vec_ref, r0, copies)
            o_ref[0, pl.ds(r0, CONV_CHUNK), :] = (yln * _sigmoid(yln)).astype(BF16)
            return carry
        lax.fori_loop(0, S // CONV_CHUNK, conv, 0, unroll=2)

    return pl.pallas_call(
        body, grid=(NB,), name="conv_fwd",
        in_specs=[pl.BlockSpec((1, S, 2 * CONV_W), lambda b: (b, 0, 0)), _const((CONV_K, CONV_W)), _const((8, CONV_W))],
        out_specs=pl.BlockSpec((1, S, CONV_W), lambda b: (b, 0, 0)),
        out_shape=jax.ShapeDtypeStruct((NB, S, CONV_W), BF16),
        scratch_shapes=[pltpu.VMEM((S + 2 * CONV_HALO, CONV_W), F32), pltpu.VMEM((8, CONV_CHUNK + 24, CONV_W), F32)],
        compiler_params=_params("parallel"),
    )(uc, dw_w, vec)


def _conv_bwd(uc, dcact, dw_w, vec, ride=None):
    NB, S, _ = uc.shape
    n_chunks = S // CONV_CHUNK

    def body(uc_ref, dc_ref, w_ref, vec_ref, duc_ref, dw_ref, dvec_ref, vpad, dzpad, dw8, dvec8, copies):
        @pl.when(pl.program_id(0) == 0)
        def _():
            dw8[...] = jnp.zeros_like(dw8)
            dvec8[...] = jnp.zeros_like(dvec8)
        _glu_into(uc_ref, vpad, S)
        dzpad[pl.ds(S, 2 * CONV_HALO), :] = jnp.zeros((2 * CONV_HALO, CONV_W), F32)

        def norm_bwd(i, carry):
            r0 = pl.multiple_of(i * CONV_CHUNK, CONV_CHUNK)
            xn, rstd, yln = _conv_ln(vpad, w_ref, vec_ref, r0, copies)
            sg = _sigmoid(yln)
            dyln = dc_ref[0, pl.ds(r0, CONV_CHUNK), :] * (sg * (1.0 + yln * (1.0 - sg)))
            dxn = dyln * vec_ref[1:2, :]
            dz = rstd * (dxn - jnp.mean(dxn, axis=-1, keepdims=True) - xn * jnp.mean(dxn * xn, axis=-1, keepdims=True))
            dzpad[pl.ds(r0, CONV_CHUNK), :] = dz
            dvec8[0] += _fold8(dz)
            dvec8[1] += _fold8(dyln * xn)
            dvec8[2] += _fold8(dyln)
            return carry
        lax.fori_loop(0, n_chunks, norm_bwd, 0, unroll=2)

        def taps_bwd(i, carry):
            r0 = pl.multiple_of(i * CONV_CHUNK, CONV_CHUNK)
            dzwin = dzpad[pl.ds(r0, CONV_WIN), :]
            dv = _conv_taps(_shifted(dzwin, 0, copies), w_ref, lambda a, k: CONV_K - 1 - 8 * a - k)
            dz = dzwin[0:CONV_CHUNK]
            vwin = vpad[pl.ds(r0, CONV_WIN), :]
            for j, rows in _tap_slices(_shifted(vwin, CONV_HALO - (CONV_K - 1), copies), lambda a, k: 8 * a + k):
                dw8[j] += _fold8(dz * rows)
            a = uc_ref[0, pl.ds(r0, CONV_CHUNK), 0:CONV_W]
            sg = _sigmoid(uc_ref[0, pl.ds(r0, CONV_CHUNK), CONV_W:2 * CONV_W])
            duc_ref[0, pl.ds(r0, CONV_CHUNK), 0:CONV_W] = (dv * sg).astype(BF16)
            duc_ref[0, pl.ds(r0, CONV_CHUNK), CONV_W:2 * CONV_W] = (dv * a * sg * (1.0 - sg)).astype(BF16)
            return carry
        lax.fori_loop(0, n_chunks, taps_bwd, 0, unroll=2)

        @pl.when(pl.program_id(0) == NB - 1)
        def _():
            dw_ref[...] = jnp.zeros_like(dw_ref)
            dvec_ref[...] = jnp.zeros_like(dvec_ref)
            for j in range(CONV_K):
                dw_ref[j:j + 1, :] = jnp.sum(dw8[j], axis=0, keepdims=True)
            for j in range(3):
                dvec_ref[j:j + 1, :] = jnp.sum(dvec8[j], axis=0, keepdims=True)

    return _pallas(
        body, "conv_bwd", (NB,),
        [pl.BlockSpec((1, S, 2 * CONV_W), lambda b: (b, 0, 0)), pl.BlockSpec((1, S, CONV_W), lambda b: (b, 0, 0)),
         _const((CONV_K, CONV_W)), _const((8, CONV_W))],
        [pl.BlockSpec((1, S, 2 * CONV_W), lambda b: (b, 0, 0)), pl.BlockSpec((32, CONV_W), lambda b: (0, 0)),
         pl.BlockSpec((8, CONV_W), lambda b: (0, 0))],
        [jax.ShapeDtypeStruct((NB, S, 2 * CONV_W), BF16), jax.ShapeDtypeStruct((32, CONV_W), F32),
         jax.ShapeDtypeStruct((8, CONV_W), F32)],
        (uc, dcact, dw_w, vec),
        scratch_shapes=[pltpu.VMEM((S + 2 * CONV_HALO, CONV_W), F32), pltpu.VMEM((S + 2 * CONV_HALO, CONV_W), F32),
                        pltpu.VMEM((CONV_K, 8, CONV_W), F32), pltpu.VMEM((3, 8, CONV_W), F32),
                        pltpu.VMEM((8, CONV_CHUNK + 24, CONV_W), F32)],
        sem=("arbitrary",), ride=ride)


def _rel_index_of_column(cols):
    offset = jnp.where(cols < KWIN, cols, cols - DS_LANES)
    return jnp.clip(KPAD - offset, -(CHUNK - 1), MAX_REL) + (CHUNK - 1)


def _bias_table(rel_bias):
    def body(rb_ref, o_ref, by_offset, first8):
        ridx = _rel_index_of_column(lax.broadcasted_iota(jnp.int32, (1, DS_LANES), 1))
        onehot = (ridx == lax.broadcasted_iota(jnp.int32, (N_REL, 1), 0)).astype(F32)
        by_offset[...] = jnp.dot(rb_ref[...], onehot, preferred_element_type=F32, precision=lax.Precision.HIGHEST)
        sub = lax.broadcasted_iota(jnp.int32, (8, 1), 0)
        kchunk = lax.broadcasted_iota(jnp.int32, (1, KWIN), 1) // CHUNK
        for head in range(ATT_HEADS):
            base = jnp.broadcast_to(by_offset[head:head + 1, :], (8, DS_LANES))
            rows = base
            for s in range(1, 8):
                rows = jnp.where(sub == s, pltpu.roll(base, s, 1), rows)
            first8[head] = rows

        def rows8(q8, carry):
            qchunk = (q8 * 8 + sub) // CHUNK
            band = (kchunk >= qchunk) & (kchunk <= qchunk + LEFT_CHUNKS)
            for head in range(ATT_HEADS):
                tile = pltpu.roll(first8[head], q8 * 8, 1)[:, 0:KWIN]
                o_ref[head, pl.ds(pl.multiple_of(q8 * 8, 8), 8), :] = jnp.where(band, tile, MASK_VALUE)
            return carry
        lax.fori_loop(0, QB // 8, rows8, 0)

    return pl.pallas_call(body, out_shape=jax.ShapeDtypeStruct((ATT_HEADS, QB, KWIN), F32), name="bias_table",
                          scratch_shapes=[pltpu.VMEM((ATT_HEADS, DS_LANES), F32), pltpu.VMEM((ATT_HEADS, 8, DS_LANES), F32)],
                          compiler_params=_params())(rel_bias)


def _load_keys(i, k_ref, v_ref, kpad, vpad, S):
    @pl.when(i == 0)
    def _():
        kpad[pl.ds(0, KPAD), :] = jnp.zeros((KPAD, ATT_W), BF16)
        vpad[pl.ds(0, KPAD), :] = jnp.zeros((KPAD, ATT_W), BF16)
        kpad[pl.ds(KPAD, S), :] = k_ref[0]
        vpad[pl.ds(KPAD, S), :] = v_ref[0]


def _att_probs(q2, k2, tab_ref, head, in_head, in_seq):
    qm = jnp.where(in_head, q2, jnp.zeros_like(q2))
    s = _dot_nt(qm, k2) * (ATT_HD ** -0.5) + tab_ref[head]
    s = jnp.where(in_seq, s, MASK_VALUE)
    e = jnp.exp(s - jnp.max(s, axis=-1, keepdims=True))
    return e * (1.0 / jnp.sum(e, axis=-1, keepdims=True))


def _att_fwd(qkv, tab, ride=None):
    NB, S, _ = qkv.shape

    def body(q_ref, k_ref, v_ref, tab_ref, o_ref, kpad, vpad):
        i = pl.program_id(1)
        _load_keys(i, k_ref, v_ref, kpad, vpad, S)
        koff = pl.multiple_of(i * QB, QB)
        lane = lax.broadcasted_iota(jnp.int32, (1, 128), 1)
        in_seq = (lax.broadcasted_iota(jnp.int32, (1, KWIN), 1) + i * QB) >= KPAD
        for pair in range(ATT_HEADS // 2):
            cols = slice(pair * 128, (pair + 1) * 128)
            q2 = q_ref[0, :, cols]
            k2 = kpad[pl.ds(koff, KWIN), cols]
            v2 = vpad[pl.ds(koff, KWIN), cols]
            o2 = jnp.zeros((QB, 128), F32)
            for hh in range(2):
                in_head = (lane // ATT_HD) == hh
                p = _att_probs(q2, k2, tab_ref, 2 * pair + hh, in_head, in_seq)
                o2 = jnp.where(in_head, _dot(p.astype(BF16), v2), o2)
            o_ref[0, :, cols] = o2.astype(BF16)

    seq = lambda col: pl.BlockSpec((1, S, ATT_W), lambda b, i: (b, 0, col), pipeline_mode=pl.Buffered(1))
    outs, rode = _pallas(
        body, "att_fwd", (NB, S // QB),
        [pl.BlockSpec((1, QB, ATT_W), lambda b, i: (b, i, 0)), seq(1), seq(2), _const((ATT_HEADS, QB, KWIN))],
        [pl.BlockSpec((1, QB, ATT_W), lambda b, i: (b, i, 0))], [jax.ShapeDtypeStruct((NB, S, ATT_W), BF16)],
        (qkv, qkv, qkv, tab),
        scratch_shapes=[pltpu.VMEM((S + KPAD, ATT_W), BF16), pltpu.VMEM((S + KPAD, ATT_W), BF16)],
        sem=("arbitrary", "arbitrary"), ride=ride)
    return outs[0], rode


def _att_bwd(qkv, do, tab, ride=None):
    NB, S, _ = qkv.shape
    nq = S // QB

    def body(q_ref, k_ref, v_ref, do_ref, tab_ref, dqkv_ref, ds_hbm, kpad, vpad, dkpad, dvpad, ds_acc, ds_sem):
        b, i = pl.program_id(0), pl.program_id(1)
        _load_keys(i, k_ref, v_ref, kpad, vpad, S)

        @pl.when(i == 0)
        def _():
            dkpad[...] = jnp.zeros_like(dkpad)
            dvpad[...] = jnp.zeros_like(dvpad)

        @pl.when((i == 0) & (b == 0))
        def _():
            ds_acc[...] = jnp.zeros_like(ds_acc)

        koff = pl.multiple_of(i * QB, QB)
        lane = lax.broadcasted_iota(jnp.int32, (1, 128), 1)
        in_seq = (lax.broadcasted_iota(jnp.int32, (1, KWIN), 1) + i * QB) >= KPAD
        for pair in range(ATT_HEADS // 2):
            cols = slice(pair * 128, (pair + 1) * 128)
            q2 = q_ref[0, :, cols]
            do2 = do_ref[0, :, cols]
            k2 = kpad[pl.ds(koff, KWIN), cols]
            v2 = vpad[pl.ds(koff, KWIN), cols]
            dq2 = jnp.zeros((QB, 128), F32)
            dk2 = jnp.zeros((KWIN, 128), F32)
            dv2 = jnp.zeros((KWIN, 128), F32)
            for hh in range(2):
                head = 2 * pair + hh
                in_head = (lane // ATT_HD) == hh
                p = _att_probs(q2, k2, tab_ref, head, in_head, in_seq)
                dom = jnp.where(in_head, do2, jnp.zeros_like(do2))
                dp = _dot_nt(dom, v2)
                ds = p * (dp - jnp.sum(p * dp, axis=-1, keepdims=True))
                ds_acc[head] += ds
                dss = (ds * (ATT_HD ** -0.5)).astype(BF16)
                dq2 = jnp.where(in_head, _dot(dss, k2), dq2)
                dk2 = jnp.where(in_head, _dot_tn(dss, q2), dk2)
                dv2 = jnp.where(in_head, _dot_tn(p.astype(BF16), do2), dv2)
            dqkv_ref[0, pl.ds(koff, QB), cols] = dq2.astype(BF16)
            dkpad[pl.ds(koff, KWIN), cols] += dk2
            dvpad[pl.ds(koff, KWIN), cols] += dv2

        @pl.when(i == nq - 1)
        def _():
            dqkv_ref[0, :, ATT_W:2 * ATT_W] = dkpad[pl.ds(KPAD, S), :].astype(BF16)
            dqkv_ref[0, :, 2 * ATT_W:3 * ATT_W] = dvpad[pl.ds(KPAD, S), :].astype(BF16)

        @pl.when((i == nq - 1) & (b == NB - 1))
        def _():
            out = pltpu.make_async_copy(ds_acc, ds_hbm, ds_sem)
            out.start()
            out.wait()

    seq = lambda col: pl.BlockSpec((1, S, ATT_W), lambda b, i: (b, 0, col), pipeline_mode=pl.Buffered(1))
    return _pallas(
        body, "att_bwd", (NB, nq),
        [pl.BlockSpec((1, QB, ATT_W), lambda b, i: (b, i, 0)), seq(1), seq(2),
         pl.BlockSpec((1, QB, ATT_W), lambda b, i: (b, i, 0)), _const((ATT_HEADS, QB, KWIN))],
        [pl.BlockSpec((1, S, 3 * ATT_W), lambda b, i: (b, 0, 0)), ANY],
        [jax.ShapeDtypeStruct((NB, S, 3 * ATT_W), BF16), jax.ShapeDtypeStruct((ATT_HEADS, QB, KWIN), F32)],
        (qkv, qkv, qkv, do, tab),
        scratch_shapes=[pltpu.VMEM((S + KPAD, ATT_W), BF16), pltpu.VMEM((S + KPAD, ATT_W), BF16),
                        pltpu.VMEM((S + KPAD, ATT_W), F32), pltpu.VMEM((S + KPAD, ATT_W), F32),
                        pltpu.VMEM((ATT_HEADS, QB, KWIN), F32), pltpu.SemaphoreType.DMA],
        sem=("arbitrary", "arbitrary"), ride=ride)


def _rel_bias_grad(ds):
    def body(ds_ref, o_ref):
        sub = lax.broadcasted_iota(jnp.int32, (8, 1), 0)
        ridx = _rel_index_of_column(lax.broadcasted_iota(jnp.int32, (DS_LANES, 1), 0))
        onehot = (ridx == lax.broadcasted_iota(jnp.int32, (1, N_REL), 1)).astype(F32)
        def rows8(q8, accs):
            shift = lax.rem(DS_LANES - q8 * 8, DS_LANES)
            out = []
            for head in range(ATT_HEADS):
                tile = ds_ref[head, pl.ds(pl.multiple_of(q8 * 8, 8), 8), :]
                tile = jnp.concatenate([tile, jnp.zeros((8, DS_LANES - KWIN), F32)], axis=1)
                out.append(accs[head] + pltpu.roll(tile, shift, 1))
            return tuple(out)
        accs = lax.fori_loop(0, QB // 8, rows8, tuple(jnp.zeros((8, DS_LANES), F32) for _ in range(ATT_HEADS)))
        for head in range(ATT_HEADS):
            acc = accs[head]
            diag = jnp.zeros((8, DS_LANES), F32)
            for s in range(8):
                shifted = acc if s == 0 else pltpu.roll(acc, DS_LANES - s, 1)
                diag = jnp.where(sub == s, shifted, diag)
            z = jnp.sum(diag, axis=0, keepdims=True)
            o_ref[head:head + 1, :] = jnp.dot(z, onehot, preferred_element_type=F32, precision=lax.Precision.HIGHEST)

    return pl.pallas_call(body, out_shape=jax.ShapeDtypeStruct((ATT_HEADS, N_REL), F32), name="rel_bias_grad",
                          compiler_params=_params())(ds)


def _memkv_fwd(mem, g, w_kv, tm):
    R = mem.shape[0]
    tm = min(tm, R)

    def body(m_ref, g_ref, w_ref, h_ref, kv_ref):
        xhat, _ = _rms_stats(m_ref[...])
        h = (xhat * g_ref[...]).astype(BF16)
        h_ref[...] = h
        kv_ref[...] = _dot(h, w_ref[...]).astype(BF16)

    row = pl.BlockSpec((tm, D), lambda t: (t, 0))
    return pl.pallas_call(
        body, grid=(R // tm,), name="memkv_fwd", in_specs=[row, _const((1, D)), _const((D, 2 * MEM_W))], out_specs=[row, row],
        out_shape=[jax.ShapeDtypeStruct((R, D), BF16), jax.ShapeDtypeStruct((R, 2 * MEM_W), BF16)],
        compiler_params=_params("parallel"),
    )(mem, g, w_kv)


def _memkv_bwd(mem, dkv, w_kv, tm):
    R = mem.shape[0]
    tm = min(tm, R)

    def body(m_ref, dkv_ref, w_ref, dg_ref):
        xhat, _ = _rms_stats(m_ref[...])
        dh = _dot_nt(dkv_ref[...].astype(BF16), w_ref[...])

        @pl.when(pl.program_id(0) == 0)
        def _():
            dg_ref[...] = jnp.zeros_like(dg_ref)
        dg_ref[...] += jnp.sum(dh * xhat, axis=0, keepdims=True)

    row = pl.BlockSpec((tm, D), lambda t: (t, 0))
    return pl.pallas_call(
        body, grid=(R // tm,), name="memkv_bwd", in_specs=[row, row, _const((D, 2 * MEM_W))],
        out_specs=pl.BlockSpec((1, D), lambda t: (0, 0)), out_shape=jax.ShapeDtypeStruct((1, D), F32),
        compiler_params=_params("arbitrary"),
    )(mem, dkv, w_kv)


def _mem_probs(qh, kh):
    s = _dot_nt(qh, kh) * (MEM_HD ** -0.5)
    e = jnp.exp(s - jnp.max(s, axis=-1, keepdims=True))
    return e * (1.0 / jnp.sum(e, axis=-1, keepdims=True))


def _mematt_fwd(mq, kv, tq):
    NB, S, _ = mq.shape
    M = kv.shape[1]

    def body(q_ref, kv_ref, o_ref):
        for h in range(MEM_HEADS):
            cols = slice(h * MEM_HD, (h + 1) * MEM_HD)
            p = _mem_probs(q_ref[0, :, cols], kv_ref[0, :, cols])
            o_ref[0, :, cols] = _dot(p.astype(BF16), kv_ref[0, :, MEM_W + h * MEM_HD:MEM_W + (h + 1) * MEM_HD]).astype(BF16)

    return pl.pallas_call(
        body, grid=(NB, S // tq), name="mematt_fwd",
        in_specs=[pl.BlockSpec((1, tq, MEM_W), lambda b, i: (b, i, 0)), pl.BlockSpec((1, M, 2 * MEM_W), lambda b, i: (b, 0, 0))],
        out_specs=pl.BlockSpec((1, tq, MEM_W), lambda b, i: (b, i, 0)),
        out_shape=jax.ShapeDtypeStruct((NB, S, MEM_W), BF16), compiler_params=_params("parallel", "parallel"),
    )(mq, kv)


def _mematt_bwd(mq, kv, do, tq):
    NB, S, _ = mq.shape
    M = kv.shape[1]

    def body(q_ref, kv_ref, do_ref, dq_ref, dkv_ref):
        @pl.when(pl.program_id(1) == 0)
        def _():
            dkv_ref[...] = jnp.zeros_like(dkv_ref)
        for h in range(MEM_HEADS):
            cols = slice(h * MEM_HD, (h + 1) * MEM_HD)
            vcols = slice(MEM_W + h * MEM_HD, MEM_W + (h + 1) * MEM_HD)
            qh, kh, vh, doh = q_ref[0, :, cols], kv_ref[0, :, cols], kv_ref[0, :, vcols], do_ref[0, :, cols]
            p = _mem_probs(qh, kh)
            dp = _dot_nt(doh, vh)
            ds = p * (dp - jnp.sum(p * dp, axis=-1, keepdims=True))
            dss = (ds * (MEM_HD ** -0.5)).astype(BF16)
            dq_ref[0, :, cols] = _dot(dss, kh).astype(BF16)
            dkv_ref[0, :, cols] += _dot_tn(dss, qh)
            dkv_ref[0, :, vcols] += _dot_tn(p.astype(BF16), doh)

    qspec = pl.BlockSpec((1, tq, MEM_W), lambda b, i: (b, i, 0))
    kvspec = pl.BlockSpec((1, M, 2 * MEM_W), lambda b, i: (b, 0, 0))
    return pl.pallas_call(
        body, grid=(NB, S // tq), name="mematt_bwd", in_specs=[qspec, kvspec, qspec], out_specs=[qspec, kvspec],
        out_shape=[jax.ShapeDtypeStruct((NB, S, MEM_W), BF16), jax.ShapeDtypeStruct((NB, M, 2 * MEM_W), F32)],
        compiler_params=_params("arbitrary", "arbitrary"),
    )(mq, kv, do)


def _branches(c_ref, a_ref, m_ref, gl_ref, bg_ref, wpw_ref, wo_ref, wmo_ref):
    ys = [_dot(c_ref[...], wpw_ref[...]), _dot(a_ref[...], wo_ref[...]), _dot(m_ref[...], wmo_ref[...])]
    gates = [_sigmoid(gl_ref[:, j * D:(j + 1) * D] + bg_ref[:, j * D:(j + 1) * D]) for j in range(3)]
    return ys, gates


def _combine_fwd(x, cact, oatt, omem, gl, bg, wpw, wo, wmo, wout, tm):
    T = x.shape[0]

    def body(x_ref, c_ref, a_ref, m_ref, gl_ref, bg_ref, wpw_ref, wo_ref, wmo_ref, wout_ref, xo_ref, y_ref):
        ys, gates = _branches(c_ref, a_ref, m_ref, gl_ref, bg_ref, wpw_ref, wo_ref, wmo_ref)
        y = (gates[0] * ys[0] + gates[1] * ys[1] + gates[2] * ys[2]).astype(BF16)
        y_ref[...] = y
        xo_ref[...] = x_ref[...] + _dot(y, wout_ref[...])

    row = lambda w: pl.BlockSpec((tm, w), lambda t: (t, 0))
    wbr = _const((512, D))
    return pl.pallas_call(
        body, grid=(T // tm,), name="combine_fwd",
        in_specs=[row(D), row(512), row(512), row(512), row(3 * D), _const((1, 3 * D)), wbr, wbr, wbr, _const((D, D))],
        out_specs=[row(D), row(D)],
        out_shape=[jax.ShapeDtypeStruct((T, D), F32), jax.ShapeDtypeStruct((T, D), BF16)],
        compiler_params=_params("parallel"),
    )(x, cact, oatt, omem, gl, bg, wpw, wo, wmo, wout)


def _combine_bwd(dx, cact, oatt, omem, gl, bg, wpw, wo, wmo, wout, tm, ride=None):
    T = dx.shape[0]

    def body(dx_ref, c_ref, a_ref, m_ref, gl_ref, bg_ref, wpw_ref, wo_ref, wmo_ref, wout_ref,
             dgl_ref, dc_ref, da_ref, dm_ref, dyc_ref, dya_ref, dym_ref, dbg_ref):
        ys, gates = _branches(c_ref, a_ref, m_ref, gl_ref, bg_ref, wpw_ref, wo_ref, wmo_ref)
        dy = _dot_nt(dx_ref[...].astype(BF16), wout_ref[...])

        @pl.when(pl.program_id(0) == 0)
        def _():
            dbg_ref[...] = jnp.zeros_like(dbg_ref)
        dyb = []
        for j in range(3):
            dlogit = dy * ys[j] * gates[j] * (1.0 - gates[j])
            dgl_ref[:, j * D:(j + 1) * D] = dlogit.astype(BF16)
            dbg_ref[:, j * D:(j + 1) * D] += jnp.sum(dlogit, axis=0, keepdims=True)
            dyb.append((dy * gates[j]).astype(BF16))
        dyc_ref[...], dya_ref[...], dym_ref[...] = dyb
        dc_ref[...] = _dot_nt(dyb[0], wpw_ref[...])
        da_ref[...] = _dot_nt(dyb[1], wo_ref[...]).astype(BF16)
        dm_ref[...] = _dot_nt(dyb[2], wmo_ref[...]).astype(BF16)

    row = lambda w: pl.BlockSpec((tm, w), lambda t: (t, 0))
    wbr = _const((512, D))
    sds = jax.ShapeDtypeStruct
    return _pallas(
        body, "combine_bwd", (T // tm,),
        [row(D), row(512), row(512), row(512), row(3 * D), _const((1, 3 * D)), wbr, wbr, wbr, _const((D, D))],
        [row(3 * D), row(512), row(512), row(512), row(D), row(D), row(D), pl.BlockSpec((1, 3 * D), lambda t: (0, 0))],
        [sds((T, 3 * D), BF16), sds((T, 512), F32), sds((T, 512), BF16), sds((T, 512), BF16),
         sds((T, D), BF16), sds((T, D), BF16), sds((T, D), BF16), sds((1, 3 * D), F32)],
        (dx, cact, oatt, omem, gl, bg, wpw, wo, wmo, wout), sem=("arbitrary",), ride=ride)


def _final(x, g, target, tm):
    T = x.shape[0]

    def body(x_ref, g_ref, t_ref, loss_ref, dx_ref, dg_ref):
        xhat, r = _rms_stats(x_ref[...])
        gain = g_ref[...]
        diff = xhat * gain - t_ref[...]
        dout = diff * (1.0 / D)

        @pl.when(pl.program_id(0) == 0)
        def _():
            loss_ref[...] = jnp.zeros_like(loss_ref)
            dg_ref[...] = jnp.zeros_like(dg_ref)
        sq = jnp.sum(jnp.sum(diff * diff, axis=0, keepdims=True), axis=1, keepdims=True)
        loss_ref[...] += jnp.broadcast_to(sq * (0.5 / D), (1, 128))
        dg_ref[...] += jnp.sum(dout * xhat, axis=0, keepdims=True)
        dx_ref[...] = _rms_bwd(dout, gain, xhat, r)

    row = pl.BlockSpec((tm, D), lambda t: (t, 0))
    return pl.pallas_call(
        body, grid=(T // tm,), name="final_loss", in_specs=[row, _const((1, D)), row],
        out_specs=[pl.BlockSpec((1, 128), lambda t: (0, 0)), row, pl.BlockSpec((1, D), lambda t: (0, 0))],
        out_shape=[jax.ShapeDtypeStruct((1, 128), F32), jax.ShapeDtypeStruct((T, D), F32), jax.ShapeDtypeStruct((1, D), F32)],
        compiler_params=_params("arbitrary"),
    )(x, g, target)


def _peer(x, y, c, rel):
    rx, ry, rc = (rel >> 2) & 1, (rel >> 1) & 1, rel & 1
    return ((1 - x) if rx else x, (1 - y) if ry else y, (1 - c) if rc else c)


def _all_sum_small(parts):
    n = len(parts)

    def body(*refs):
        p_refs, o_refs, slots = refs[:n], refs[n:2 * n], refs[2 * n:3 * n]
        send_sems, recv_sems = refs[3 * n:]
        x, y, c = _my_coords()
        me = _dev_index(x, y, c)

        def copy(i, rel, arrival):
            peer = _peer(x, y, c, rel)
            return pltpu.make_async_remote_copy(
                src_ref=p_refs[i], dst_ref=slots[i].at[_dev_index(*peer) if arrival else me],
                send_sem=send_sems.at[i, rel - 1], recv_sem=recv_sems.at[i, rel - 1], device_id=peer, device_id_type=MESH)

        for i in range(n):
            slots[i][me] = p_refs[i][...]
        for rel in range(1, NDEV):
            for i in range(n):
                copy(i, rel, False).start()
        for rel in range(1, NDEV):
            for i in range(n):
                copy(i, rel, True).wait_recv()
        for rel in range(1, NDEV):
            for i in range(n):
                copy(i, rel, False).wait_send()
        for i in range(n):
            total = slots[i][0]
            for d in range(1, NDEV):
                total = total + slots[i][d]
            o_refs[i][...] = total

    vmem = pl.BlockSpec(memory_space=pltpu.VMEM)
    return pl.pallas_call(
        body, out_shape=[jax.ShapeDtypeStruct(p.shape, F32) for p in parts], name="all_sum_small",
        in_specs=[vmem] * n, out_specs=[vmem] * n,
        scratch_shapes=[pltpu.VMEM((NDEV,) + p.shape, F32) for p in parts]
        + [pltpu.SemaphoreType.DMA((n, NDEV - 1)), pltpu.SemaphoreType.DMA((n, NDEV - 1))],
        compiler_params=pltpu.CompilerParams(has_side_effects=True),
    )(*parts)


def _adamw_math(w, g, m, v):
    m = ADAM_B1 * m + (1.0 - ADAM_B1) * g
    v = ADAM_B2 * v + (1.0 - ADAM_B2) * (g * g)
    m_hat = m / (1.0 - ADAM_B1 ** ADAM_STEP)
    v_hat = v / (1.0 - ADAM_B2 ** ADAM_STEP)
    delta = -ADAM_LR * (m_hat / (jnp.sqrt(v_hat) + ADAM_EPS) + ADAM_WD * w)
    return delta, m, v


def _sum_adamw(parts, w, m, v, name):
    R, C = w.shape
    n_parts = len(parts)
    cg = C // n_parts
    tr = max(t for t in range(8, 257, 8) if R % t == 0)

    def body(*refs):
        p_refs = refs[:n_parts]
        w_ref, m_ref, v_ref, g_ref, d_ref, mo_ref, vo_ref = refs[n_parts:]
        for k, p_ref in enumerate(p_refs):
            @pl.when(pl.program_id(0) == k)
            def _():
                g = p_ref[0].astype(F32)
                for d in range(1, NDEV):
                    g = g + p_ref[d].astype(F32)
                g_ref[...] = g
                d_ref[...], mo_ref[...], vo_ref[...] = _adamw_math(w_ref[...], g, m_ref[...], v_ref[...])

    part = pl.BlockSpec((NDEV, tr, cg), lambda k, t: (0, t, 0))
    blk = pl.BlockSpec((tr, cg), lambda k, t: (t, k))
    return pl.pallas_call(
        body, grid=(n_parts, R // tr), name=name, in_specs=[part] * n_parts + [blk, blk, blk],
        out_specs=[blk] * 4, out_shape=[jax.ShapeDtypeStruct((R, C), F32)] * 4, compiler_params=_params("parallel", "parallel"),
    )(*parts, w, m, v)


def _adamw_small(ws, gs, ms, vs):
    n = len(ws)

    def body(*refs):
        w_refs, g_refs, m_refs, v_refs = (refs[k * n:(k + 1) * n] for k in range(4))
        d_refs, mo_refs, vo_refs = (refs[(4 + k) * n:(5 + k) * n] for k in range(3))
        for i in range(n):
            d_refs[i][...], mo_refs[i][...], vo_refs[i][...] = _adamw_math(w_refs[i][...], g_refs[i][...], m_refs[i][...], v_refs[i][...])

    shapes = [jax.ShapeDtypeStruct(a.shape, F32) for a in ws]
    outs = pl.pallas_call(body, out_shape=shapes * 3, name="adamw_small", compiler_params=_params())(*ws, *gs, *ms, *vs)
    return outs[:n], outs[n:2 * n], outs[2 * n:]


def kernel(x, mem, ffn1_norm, ffn1_w_up, ffn1_w_down, mix_norm, mem_norm, w_in, b_gate, conv_dw_w, conv_dw_b, conv_ln_g, conv_ln_b, conv_w_pw, att_rel_bias, att_w_o, mem_w_kv, mem_w_o, w_out, ffn2_norm, ffn2_w_up, ffn2_w_down, final_norm, loss_target, m_ffn1_norm, m_ffn1_w_up, m_ffn1_w_down, m_mix_norm, m_mem_norm, m_w_in, m_b_gate, m_conv_dw_w, m_conv_dw_b, m_conv_ln_g, m_conv_ln_b, m_conv_w_pw, m_att_rel_bias, m_att_w_o, m_mem_w_kv, m_mem_w_o, m_w_out, m_ffn2_norm, m_ffn2_w_up, m_ffn2_w_down, m_final_norm, v_ffn1_norm, v_ffn1_w_up, v_ffn1_w_down, v_mix_norm, v_mem_norm, v_w_in, v_b_gate, v_conv_dw_w, v_conv_dw_b, v_conv_ln_g, v_conv_ln_b, v_conv_w_pw, v_att_rel_bias, v_att_w_o, v_mem_w_kv, v_mem_w_o, v_w_out, v_ffn2_norm, v_ffn2_w_up, v_ffn2_w_down, v_final_norm):
    given = dict(locals())
    w = {n: given[n] for n in WEIGHTS}
    mom = {n: given["m_" + n] for n in WEIGHTS}
    var = {n: given["v_" + n] for n in WEIGHTS}

    NB, S, _ = x.shape
    T = NB * S
    ML = mem.shape[1]
    x0 = x.reshape(T, D)
    target = loss_target.reshape(T, D)
    mem2 = mem.reshape(NB * ML, D)

    def block(t, n):
        return jnp.transpose(t[0]) if n in TRANSPOSED else t[0]

    sh = dict(zip(BIG_ORDER, _cast_shards([block(w[n], n) for n in BIG_ORDER])))
    dw_t = jnp.transpose(conv_dw_w[0])

    def gather(names, extra=(), extra_kinds=()):
        return _gather_ride([sh[n] for n in names] + list(extra), [BIG[n] for n in names] + list(extra_kinds))

    W = {}
    names0 = ['ffn1_w_up', 'ffn1_w_down']
    got = _exchange_alone(gather(names0, [dw_t], [('row', dw_t.shape[0])]), "gather_ffn1")
    W.update(zip(names0, got[:2]))
    dw_full = jnp.transpose(got[2])
    conv_vec = jnp.concatenate([conv_dw_b, conv_ln_g, conv_ln_b, jnp.zeros((5, CONV_W), F32)], axis=0)
    tab = _bias_table(att_rel_bias[0])
    fin_g = final_norm.reshape(1, D)

    names1 = ['w_in', 'conv_w_pw', 'att_w_o', 'mem_w_kv', 'mem_w_o', 'w_out']
    (x1, ab1), got = _ffn_fwd(x0, ffn1_norm, W['ffn1_w_up'], W['ffn1_w_down'], TILE_FFN, "ffn1_fwd", ride=gather(names1))
    W.update(zip(names1, got))
    (uc, qkv, mq, gl, hmix), _ = _mix_fwd(x1, mix_norm, W['w_in'], TILE_TOKENS)
    uc3 = uc.reshape(NB, S, 2 * CONV_W)
    qkv3 = qkv.reshape(NB, S, 3 * ATT_W)
    mq3 = mq.reshape(NB, S, MEM_W)
    cact = _conv_fwd(uc3, dw_full, conv_vec).reshape(T, CONV_W)
    names2 = ['ffn2_w_up', 'ffn2_w_down']
    oatt, got = _att_fwd(qkv3, tab, ride=gather(names2))
    W.update(zip(names2, got))
    oatt = oatt.reshape(T, ATT_W)
    memh, kv = _memkv_fwd(mem2, mem_norm, W['mem_w_kv'], TILE_TOKENS)
    kv3 = kv.reshape(NB, ML, 2 * MEM_W)
    omem = _mematt_fwd(mq3, kv3, TILE_TOKENS).reshape(T, MEM_W)
    branch_w = (W['conv_w_pw'], W['att_w_o'], W['mem_w_o'], W['w_out'])
    x2, ymix = _combine_fwd(x1, cact, oatt, omem, gl, b_gate, *branch_w, TILE_COMBINE)
    (x3, ab2), _ = _ffn_fwd(x2, ffn2_norm, W['ffn2_w_up'], W['ffn2_w_down'], TILE_FFN, "ffn2_fwd")
    loss_part, dx3, dg_final = _final(x3, fin_g, target, TILE_TOKENS)

    def scatter(grads, names):
        return _scatter_ride(grads, [BIG[n] for n in names])

    G, P = {}, {}
    dx2, dab2, act2, h2, dg_ffn2 = _ffn_bwd(x2, dx3, ab2, ffn2_norm, W['ffn2_w_up'], W['ffn2_w_down'], TILE_FFN, "ffn2_bwd")
    g_up, _ = _tn_matmul(dab2, h2, 512, "grad_ffn2_w_up_a", tt=TILE_GRAD_TOKENS_WIDE, x_part=(0, 2), out_rows=2 * FF)
    G['ffn2_w_up'], _ = _tn_matmul(dab2, h2, 512, "grad_ffn2_w_up_b", tt=TILE_GRAD_TOKENS_WIDE, x_part=(1, 2), out_rows=2 * FF,
                                   prev=g_up)
    G['ffn2_w_down'], _ = _tn_matmul(act2, dx3, 512, "grad_ffn2_w_down", scale=0.5, tt=TILE_GRAD_TOKENS_WIDE)
    (dgl, dcact, doatt, domem, dyc, dya, dym, dbg), got = _combine_bwd(
        dx2, cact, oatt, omem, gl, b_gate, *branch_w, TILE_COMBINE, ride=scatter([G['ffn2_w_up']], ['ffn2_w_up']))
    P['ffn2_w_up'] = got
    G['w_out'], _ = _tn_matmul(ymix, dx2, 512, "grad_w_out")
    G['conv_w_pw'], _ = _tn_matmul(cact, dyc, 512, "grad_conv_w_pw")
    G['att_w_o'], _ = _tn_matmul(oatt, dya, 512, "grad_att_w_o")
    G['mem_w_o'], _ = _tn_matmul(omem, dym, 512, "grad_mem_w_o")
    dmq3, dkv3 = _mematt_bwd(mq3, kv3, domem.reshape(NB, S, MEM_W), TILE_TOKENS)
    dkv = dkv3.reshape(NB * ML, 2 * MEM_W)
    dg_mem = _memkv_bwd(mem2, dkv, W['mem_w_kv'], TILE_TOKENS)
    G['mem_w_kv'], _ = _tn_matmul(memh, dkv, 512, "grad_mem_w_kv")
    names = ['ffn2_w_down', 'w_out', 'conv_w_pw', 'att_w_o', 'mem_w_o']
    (dqkv3, dscore), got = _att_bwd(qkv3, doatt.reshape(NB, S, ATT_W), tab, ride=scatter([G[n] for n in names], names))
    P.update((n, [p]) for n, p in zip(names, got))
    d_rel = _rel_bias_grad(dscore)
    (duc3, d_dw, d_cvec), got = _conv_bwd(uc3, dcact.reshape(NB, S, CONV_W), dw_full, conv_vec,
                                          ride=scatter([G['mem_w_kv']], ['mem_w_kv']))
    P['mem_w_kv'] = got
    duc, dqkv, dmq = duc3.reshape(T, 2 * CONV_W), dqkv3.reshape(T, 3 * ATT_W), dmq3.reshape(T, MEM_W)
    g_in, _ = _tn_matmul(hmix, duc, 512, "grad_w_in_conv", out_cols=IN_COLS, col_off=0)
    g_in, _ = _tn_matmul(hmix, dqkv, 512, "grad_w_in_qkv", out_cols=IN_COLS, col_off=1024, prev=g_in)
    g_in, _ = _tn_matmul(hmix, dmq, 512, "grad_w_in_mq", out_cols=IN_COLS, col_off=2560, prev=g_in)
    G['w_in'], _ = _tn_matmul(hmix, dgl, 512, "grad_w_in_gate", out_cols=IN_COLS, col_off=3072, prev=g_in)
    (dx1, dg_mix), got = _mix_bwd(x1, dx2, duc, dqkv, dmq, dgl, mix_norm, W['w_in'], TILE_TOKENS,
                                  ride=scatter([G['w_in']], ['w_in']))
    P['w_in'] = got
    dx0, dab1, act1, h1, dg_ffn1 = _ffn_bwd(x0, dx1, ab1, ffn1_norm, W['ffn1_w_up'], W['ffn1_w_down'], TILE_FFN, "ffn1_bwd")
    g_wd1, _ = _tn_matmul(act1, dx1, 512, "grad_ffn1_w_down", scale=0.5, tt=TILE_GRAD_TOKENS_WIDE)
    g_wu1a, got = _tn_matmul(dab1, h1, 512, "grad_ffn1_w_up_a", tt=TILE_GRAD_TOKENS_WIDEST, y_part=(0, 2),
                             ride=scatter([g_wd1], ['ffn1_w_down']))
    P['ffn1_w_down'] = got
    g_wu1b, got_a = _tn_matmul(dab1, h1, 512, "grad_ffn1_w_up_b", tt=TILE_GRAD_TOKENS_WIDEST, y_part=(1, 2),
                               ride=scatter([g_wu1a], ['ffn1_w_up']))
    got_b = _exchange_alone(scatter([g_wu1b], ['ffn1_w_up']), "scatter_last")
    P['ffn1_w_up'] = [got_a[0], got_b[0]]

    small_names = ['loss', 'ffn1_norm', 'mix_norm', 'mem_norm', 'b_gate', 'conv_dw_w', 'conv_vec', 'att_rel_bias', 'ffn2_norm',
                   'final_norm']
    small = dict(zip(small_names, _all_sum_small(
        [loss_part, dg_ffn1, dg_mix, dg_mem, dbg, d_dw, d_cvec, d_rel, dg_ffn2, dg_final])))
    loss = small['loss'][0, 0]
    me = _dev_index(*_my_coords())
    for i, n in enumerate(['conv_dw_b', 'conv_ln_g', 'conv_ln_b']):
        small[n] = small['conv_vec'][i:i + 1]
    small['conv_dw_w'] = lax.dynamic_slice(small['conv_dw_w'], (0, me * conv_dw_w.shape[2]), (CONV_K, conv_dw_w.shape[2]))
    little = [n for n in WEIGHTS if n not in BIG]
    as2d = lambda t, n: t.reshape(small[n].shape)
    d_s, m_s, v_s = _adamw_small([as2d(w[n], n) for n in little], [small[n] for n in little],
                                 [as2d(mom[n], n) for n in little], [as2d(var[n], n) for n in little])
    grad, delta, new_m, new_v = {}, {}, {}, {}
    for i, n in enumerate(little):
        grad[n], delta[n], new_m[n], new_v[n] = (t.reshape(w[n].shape) for t in (small[n], d_s[i], m_s[i], v_s[i]))
    for n in BIG_ORDER:
        outs = _sum_adamw(P[n], block(w[n], n), block(mom[n], n), block(var[n], n), "adamw_" + n)
        grad[n], delta[n], new_m[n], new_v[n] = ((jnp.transpose(t) if n in TRANSPOSED else t)[None] for t in outs)

    return (loss, dx0.reshape(NB, S, D), *[grad[n] for n in WEIGHTS], *[delta[n] for n in WEIGHTS],
            *[new_m[n] for n in WEIGHTS], *[new_v[n] for n in WEIGHTS])
```

```python
import functools

import jax
import jax.numpy as jnp
from jax import lax
from jax.experimental import pallas as pl
from jax.experimental.pallas import tpu as pltpu

F32 = jnp.float32
BF16 = jnp.bfloat16

EPS = 1e-6
MASK_VALUE = -1e30
D = 1024
NDEV = 8
FF = 2816
FF_SHARD = 704
FF_HALF_ROWS = 352
FF_BLOCK_EDGES = ()
IN_COLS = 6144
CONV_W = 512
CONV_K = 31
CONV_HALO = 32
CONV_CHUNK = 32
CONV_WIN = CONV_CHUNK + 40
GLU_CHUNK = 128
ATT_W = 512
ATT_HEADS = 8
ATT_HD = 64
CHUNK = 64
LEFT_CHUNKS = 8
MAX_REL = 128
N_REL = 192
QB = 256
KWIN = QB + LEFT_CHUNKS * CHUNK
KPAD = LEFT_CHUNKS * CHUNK
DS_LANES = 1024
MEM_W = 512
MEM_HEADS = 4
MEM_HD = 128
ADAM_LR = 0.001
ADAM_B1 = 0.9
ADAM_B2 = 0.999
ADAM_EPS = 1e-08
ADAM_WD = 0.01
ADAM_STEP = 10
VMEM_LIMIT = 60 * 1024 * 1024
TILE_FFN = 256
TILE_COMBINE = 256
TILE_TOKENS = 512
TILE_GRAD_TOKENS = 2048
TILE_GRAD_TOKENS_WIDE = 1024
TILE_GRAD_TOKENS_WIDEST = 512

MESH = pl.DeviceIdType.MESH
ANY = pl.BlockSpec(memory_space=pl.ANY)

WEIGHTS = ['ffn1_norm', 'ffn1_w_up', 'ffn1_w_down', 'mix_norm', 'mem_norm', 'w_in', 'b_gate', 'conv_dw_w', 'conv_dw_b',
           'conv_ln_g', 'conv_ln_b', 'conv_w_pw', 'att_rel_bias', 'att_w_o', 'mem_w_kv', 'mem_w_o', 'w_out', 'ffn2_norm',
           'ffn2_w_up', 'ffn2_w_down', 'final_norm']
BIG = {
    'ffn1_w_up': ('row', FF_SHARD), 'ffn1_w_down': ('row', FF_HALF_ROWS), 'w_in': ('col', 768),
    'conv_w_pw': ('col', 128), 'att_w_o': ('col', 128), 'mem_w_kv': ('row', 128), 'mem_w_o': ('col', 128),
    'w_out': ('row', 128), 'ffn2_w_up': ('row', FF_SHARD), 'ffn2_w_down': ('row', FF_HALF_ROWS),
}
BIG_ORDER = ['ffn1_w_up', 'ffn1_w_down', 'w_in', 'conv_w_pw', 'att_w_o', 'mem_w_kv', 'mem_w_o', 'w_out', 'ffn2_w_up', 'ffn2_w_down']
TRANSPOSED = ('ffn1_w_up', 'ffn2_w_up')


def _dot(a, b):
    return jnp.dot(a, b, preferred_element_type=F32)


def _dot_nt(a, b):
    return lax.dot_general(a, b, (((1,), (1,)), ((), ())), preferred_element_type=F32)


def _dot_tn(a, b):
    return lax.dot_general(a, b, (((0,), (0,)), ((), ())), preferred_element_type=F32)


def _sigmoid(v):
    return jax.nn.sigmoid(v)


def _const(shape):
    return pl.BlockSpec(shape, lambda *_: (0,) * len(shape), pipeline_mode=pl.Buffered(1))


def _params(*sem):
    return pltpu.CompilerParams(dimension_semantics=sem if sem else None, vmem_limit_bytes=VMEM_LIMIT)


def _my_coords():
    return lax.axis_index("x"), lax.axis_index("y"), lax.axis_index("c")


def _dev_index(px, py, pc):
    return 4 * px + 2 * py + pc


def _window(ref, kind, n, p):
    if kind == 'row':
        return ref.at[pl.ds(pl.multiple_of(p * n, n), n), :]
    return ref.at[:, pl.ds(pl.multiple_of(p * n, 128), n)]


def _full_shape(kind, n, shard_shape):
    if kind == 'row':
        return (NDEV * n, shard_shape[1])
    return (shard_shape[0], NDEV * n)


def _cast_shards(shards):
    n = len(shards)

    def body(*refs):
        for i in range(n):
            refs[n + i][...] = refs[i][...].astype(BF16)

    out_shape = [jax.ShapeDtypeStruct(s.shape, BF16) for s in shards]
    return pl.pallas_call(body, out_shape=out_shape, name="cast_shards", compiler_params=_params())(*shards)


class _Ride:
    def __init__(self, inputs, out_shape, scratch, start, finish, mid=None):
        self.inputs, self.out_shape, self.scratch = list(inputs), list(out_shape), list(scratch)
        self.start, self.finish, self.mid = start, finish, mid


def _pallas(body, name, grid, in_specs, out_specs, out_shape, args, scratch_shapes=(), sem=None, aliases=None, ride=None):
    if ride is None:
        outs = pl.pallas_call(body, grid=grid, name=name, in_specs=in_specs, out_specs=out_specs, out_shape=out_shape,
                              scratch_shapes=list(scratch_shapes), input_output_aliases=aliases or {},
                              compiler_params=_params(*sem))(*args)
        return list(outs), []
    n_in, n_out, n_scr = len(args), len(out_shape), len(scratch_shapes)
    r_in, r_out = len(ride.inputs), len(ride.out_shape)

    def wrapped(*refs):
        k_in, rin = refs[:n_in], refs[n_in:n_in + r_in]
        o0 = n_in + r_in
        k_out, rout = refs[o0:o0 + n_out], refs[o0 + n_out:o0 + n_out + r_out]
        s0 = o0 + n_out + r_out
        k_scr, rscr = refs[s0:s0 + n_scr], refs[s0 + n_scr:]
        ids = [pl.program_id(k) for k in range(len(grid))]
        first = functools.reduce(jnp.logical_and, [i == 0 for i in ids])
        last = functools.reduce(jnp.logical_and, [i == g - 1 for i, g in zip(ids, grid)])
        pl.when(first)(lambda: ride.start(rin, rout, rscr))
        if ride.mid is not None:
            at_mid = functools.reduce(jnp.logical_and, [ids[0] == (3 * grid[0]) // 4] + [i == 0 for i in ids[1:]])
            pl.when(at_mid)(lambda: ride.mid(rin, rout, rscr))
        body(*k_in, *k_out, *k_scr)
        pl.when(last)(lambda: ride.finish(rin, rout, rscr))

    outs = pl.pallas_call(
        wrapped, grid=grid, name=name, in_specs=list(in_specs) + [ANY] * r_in, out_specs=list(out_specs) + [ANY] * r_out,
        out_shape=list(out_shape) + ride.out_shape, scratch_shapes=list(scratch_shapes) + ride.scratch,
        input_output_aliases=aliases or {}, compiler_params=_params(*(["arbitrary"] * len(grid))),
    )(*args, *ride.inputs)
    return list(outs[:n_out]), list(outs[n_out:])


def _exchange_alone(ride, name):
    r_in, r_out = len(ride.inputs), len(ride.out_shape)

    def body(*refs):
        rin, rout, rscr = refs[:r_in], refs[r_in:r_in + r_out], refs[r_in + r_out:]
        ride.start(rin, rout, rscr)
        if ride.mid is not None:
            ride.mid(rin, rout, rscr)
        ride.finish(rin, rout, rscr)

    return pl.pallas_call(body, out_shape=ride.out_shape, in_specs=[ANY] * r_in, out_specs=[ANY] * r_out, name=name,
                          scratch_shapes=ride.scratch, compiler_params=pltpu.CompilerParams(has_side_effects=True))(*ride.inputs)


def _gather_ride(shards, kinds):
    n = len(shards)

    def plan(rin, out, sems):
        send_sems, recv_sems, local_sems = sems[:3]
        x, y, c = _my_coords()
        me, sibling = (x, y, c), (x, y, 1 - c)
        chips = [(1 - x, y), (x, 1 - y), (1 - x, 1 - y)]

        def win(i, dev):
            return _window(out[i], kinds[i][0], kinds[i][1], _dev_index(*dev))

        def copy(i, k, block, to, from_shard=False):
            return pltpu.make_async_remote_copy(
                src_ref=rin[i] if from_shard else win(i, block), dst_ref=win(i, block),
                send_sem=send_sems.at[i, k], recv_sem=recv_sems.at[i, k], device_id=to, device_id_type=MESH)

        def local():
            return [pltpu.make_async_copy(rin[i], win(i, me), local_sems.at[i]) for i in range(n)]

        def first():
            cps = []
            for i in range(n):
                cps.append(copy(i, 0, me, sibling, from_shard=True))
                cps += [copy(i, 1 + j, me, (*chip, c), from_shard=True) for j, chip in enumerate(chips)]
            return cps

        def arrived():
            return [copy(i, 1 + j, (*chip, c), me) for j, chip in enumerate(chips) for i in range(n)]

        def passed():
            return [copy(i, 4 + j, (*chip, c), sibling) for j, chip in enumerate(chips) for i in range(n)]

        def from_sibling():
            cps = [copy(i, 0, sibling, me) for i in range(n)]
            return cps + [copy(i, 4 + j, (*chip, 1 - c), me) for i in range(n) for j, chip in enumerate(chips)]

        return local, first, arrived, passed, from_sibling

    def start(rin, out, sems):
        local, first, _, _, _ = plan(rin, out, sems)
        for cp in local() + first():
            cp.start()

    def mid(rin, out, sems):
        _, _, arrived, passed, _ = plan(rin, out, sems)
        for got, fwd in zip(arrived(), passed()):
            got.wait_recv()
            fwd.start()

    def finish(rin, out, sems):
        local, first, _, passed, from_sibling = plan(rin, out, sems)
        for cp in from_sibling():
            cp.wait_recv()
        for cp in first() + passed():
            cp.wait_send()
        for cp in local():
            cp.wait()

    out_shape = [jax.ShapeDtypeStruct(_full_shape(k, m, s.shape), s.dtype) for s, (k, m) in zip(shards, kinds)]
    scratch = [pltpu.SemaphoreType.DMA((n, 7)), pltpu.SemaphoreType.DMA((n, 7)), pltpu.SemaphoreType.DMA((n,))]
    return _Ride(shards, out_shape, scratch, start, finish, mid)


def _scatter_ride(grads, kinds):
    n = len(grads)

    def plan(g, out, sems):
        send_sems, recv_sems, local_sems = sems
        x, y, c = _my_coords()
        me = _dev_index(x, y, c)

        def local():
            return [pltpu.make_async_copy(_window(g[i], kinds[i][0], kinds[i][1], me), out[i].at[me], local_sems.at[i])
                    for i in range(n)]

        def remote(arrival):
            cps = []
            for rel in range(1, NDEV):
                peer = _peer(x, y, c, rel)
                dev = _dev_index(*peer)
                for i in range(n):
                    kind, m = kinds[i]
                    cps.append(pltpu.make_async_remote_copy(
                        src_ref=_window(g[i], kind, m, me if arrival else dev), dst_ref=out[i].at[dev if arrival else me],
                        send_sem=send_sems.at[i, rel - 1], recv_sem=recv_sems.at[i, rel - 1], device_id=peer, device_id_type=MESH))
            return cps

        return local, remote

    def start(g, out, sems):
        local, remote = plan(g, out, sems)
        for cp in local() + remote(False):
            cp.start()

    def finish(g, out, sems):
        local, remote = plan(g, out, sems)
        for cp in remote(True):
            cp.wait_recv()
        for cp in remote(False):
            cp.wait_send()
        for cp in local():
            cp.wait()

    def block_shape(gr, kind, m):
        return (m, gr.shape[1]) if kind == 'row' else (gr.shape[0], m)

    out_shape = [jax.ShapeDtypeStruct((NDEV,) + block_shape(gr, k, m), gr.dtype) for gr, (k, m) in zip(grads, kinds)]
    scratch = [pltpu.SemaphoreType.DMA((n, NDEV - 1)), pltpu.SemaphoreType.DMA((n, NDEV - 1)), pltpu.SemaphoreType.DMA((n,))]
    return _Ride(grads, out_shape, scratch, start, finish)


def _rms_stats(xf):
    r = lax.rsqrt(jnp.mean(xf * xf, axis=-1, keepdims=True) + EPS)
    return xf * r, r


def _rms_bwd(dh, g, xhat, r):
    dxhat = dh * g
    return r * (dxhat - xhat * jnp.mean(dxhat * xhat, axis=-1, keepdims=True))


def _ffn_blocks():
    edges = (0,) + FF_BLOCK_EDGES + (FF,)
    return [(slice(lo, hi), slice(FF + lo, FF + hi)) for lo, hi in zip(edges[:-1], edges[1:])]


def _ffn_fwd(x, g, wut, wd, tm, name, ride=None):
    T = x.shape[0]

    def body(x_ref, g_ref, wut_ref, wd_ref, xo_ref, ab_ref):
        xf = x_ref[...]
        xhat, _ = _rms_stats(xf)
        h = (xhat * g_ref[...]).astype(BF16)
        acc = jnp.zeros((tm, D), F32)
        for ra, rb in _ffn_blocks():
            a = _dot_nt(h, wut_ref[ra, :])
            b = _dot_nt(h, wut_ref[rb, :])
            ab_ref[:, ra] = a.astype(BF16)
            ab_ref[:, rb] = b.astype(BF16)
            act = (a * _sigmoid(a) * b).astype(BF16)
            acc = acc + _dot(act, wd_ref[ra, :])
        xo_ref[...] = xf + 0.5 * acc

    return _pallas(
        body, name, (T // tm,),
        [pl.BlockSpec((tm, D), lambda t: (t, 0)), _const((1, D)), _const((2 * FF, D)), _const((FF, D))],
        [pl.BlockSpec((tm, D), lambda t: (t, 0)), pl.BlockSpec((tm, 2 * FF), lambda t: (t, 0))],
        [jax.ShapeDtypeStruct((T, D), F32), jax.ShapeDtypeStruct((T, 2 * FF), BF16)],
        (x, g, wut, wd), sem=("arbitrary",), ride=ride)


def _ffn_bwd(x, dy, ab, g, wut, wd, tm, name):
    T = x.shape[0]

    def body(x_ref, dy_ref, ab_ref, g_ref, wut_ref, wd_ref, dx_ref, dab_ref, act_ref, h_ref, dg_ref):
        xf = x_ref[...]
        xhat, r = _rms_stats(xf)
        gain = g_ref[...]
        h_ref[...] = (xhat * gain).astype(BF16)
        dy = dy_ref[...]
        dyh = (0.5 * dy).astype(BF16)
        dh = jnp.zeros((tm, D), F32)
        for ra, rb in _ffn_blocks():
            a = ab_ref[:, ra].astype(F32)
            b = ab_ref[:, rb].astype(F32)
            dact = _dot_nt(dyh, wd_ref[ra, :])
            sg = _sigmoid(a)
            sl = a * sg
            act_ref[:, ra] = (sl * b).astype(BF16)
            da = (dact * b * (sg * (1.0 + a * (1.0 - sg)))).astype(BF16)
            db = (dact * sl).astype(BF16)
            dab_ref[:, ra] = da
            dab_ref[:, rb] = db
            dh = dh + _dot(da, wut_ref[ra, :]) + _dot(db, wut_ref[rb, :])
        dx_ref[...] = dy + _rms_bwd(dh, gain, xhat, r)

        @pl.when(pl.program_id(0) == 0)
        def _():
            dg_ref[...] = jnp.zeros_like(dg_ref)
        dg_ref[...] += jnp.sum(dh * xhat, axis=0, keepdims=True)

    return pl.pallas_call(
        body, grid=(T // tm,), name=name,
        in_specs=[pl.BlockSpec((tm, D), lambda t: (t, 0)), pl.BlockSpec((tm, D), lambda t: (t, 0)),
                  pl.BlockSpec((tm, 2 * FF), lambda t: (t, 0)), _const((1, D)), _const((2 * FF, D)), _const((FF, D))],
        out_specs=[pl.BlockSpec((tm, D), lambda t: (t, 0)), pl.BlockSpec((tm, 2 * FF), lambda t: (t, 0)),
                   pl.BlockSpec((tm, FF), lambda t: (t, 0)), pl.BlockSpec((tm, D), lambda t: (t, 0)),
                   pl.BlockSpec((1, D), lambda t: (0, 0))],
        out_shape=[jax.ShapeDtypeStruct((T, D), F32), jax.ShapeDtypeStruct((T, 2 * FF), BF16),
                   jax.ShapeDtypeStruct((T, FF), BF16), jax.ShapeDtypeStruct((T, D), BF16), jax.ShapeDtypeStruct((1, D), F32)],
        compiler_params=_params("arbitrary"),
    )(x, dy, ab, g, wut, wd)


def _tn_matmul(xm, ym, tn, name, scale=None, out_cols=None, col_off=0, prev=None, tt=TILE_GRAD_TOKENS, x_part=(0, 1),
               out_rows=None, y_part=(0, 1), ride=None):
    T = xm.shape[0]
    xi, xn = x_part
    yi, yn = y_part
    K = xm.shape[1] // xn
    N = ym.shape[1] // yn
    out_cols = N if out_cols is None else out_cols
    row_blk = xi if out_rows is not None else 0
    out_rows = K if out_rows is None else out_rows
    tt = min(tt, T)
    nt = T // tt
    off = col_off // tn

    def body(*refs):
        x_ref, y_ref = refs[0], refs[1]
        o_ref, acc = refs[-2], refs[-1]

        @pl.when(pl.program_id(1) == 0)
        def _():
            acc[...] = jnp.zeros_like(acc)
        acc[...] += _dot_tn(x_ref[...].astype(BF16), y_ref[...].astype(BF16))

        @pl.when(pl.program_id(1) == nt - 1)
        def _():
            res = acc[...]
            o_ref[...] = (res if scale is None else res * scale).astype(BF16)

    ycol = yi * (N // tn)
    in_specs = [pl.BlockSpec((tt, K), lambda n, t: (t, xi)), pl.BlockSpec((tt, tn), lambda n, t: (t, n + ycol))]
    args = [xm, ym]
    aliases = {}
    if prev is not None:
        in_specs.append(ANY)
        args.append(prev)
        aliases = {2: 0}
    outs, rode = _pallas(
        body, name, (N // tn, nt), in_specs, [pl.BlockSpec((K, tn), lambda n, t: (row_blk, n + off))],
        [jax.ShapeDtypeStruct((out_rows, out_cols), BF16)], args, scratch_shapes=[pltpu.VMEM((K, tn), F32)],
        sem=("parallel", "arbitrary"), aliases=aliases, ride=ride)
    return outs[0], rode


def _mix_fwd(x, g, w_in, tm, ride=None):
    T = x.shape[0]

    def body(x_ref, g_ref, w_ref, uc_ref, qkv_ref, mq_ref, gl_ref, h_ref):
        xhat, _ = _rms_stats(x_ref[...])
        h = (xhat * g_ref[...]).astype(BF16)
        h_ref[...] = h
        uc_ref[...] = _dot(h, w_ref[:, 0:1024])
        qkv_ref[...] = _dot(h, w_ref[:, 1024:2560]).astype(BF16)
        mq_ref[...] = _dot(h, w_ref[:, 2560:3072]).astype(BF16)
        for j in range(3):
            gl_ref[:, j * D:(j + 1) * D] = _dot(h, w_ref[:, 3072 + j * D:3072 + (j + 1) * D])

    row = lambda w: pl.BlockSpec((tm, w), lambda t: (t, 0))
    return _pallas(
        body, "mix_fwd", (T // tm,), [row(D), _const((1, D)), _const((D, IN_COLS))],
        [row(1024), row(1536), row(512), row(3072), row(D)],
        [jax.ShapeDtypeStruct((T, 1024), F32), jax.ShapeDtypeStruct((T, 1536), BF16), jax.ShapeDtypeStruct((T, 512), BF16),
         jax.ShapeDtypeStruct((T, 3072), F32), jax.ShapeDtypeStruct((T, D), BF16)],
        (x, g, w_in), sem=("parallel",), ride=ride)


def _mix_bwd(x, dres, duc, dqkv, dmq, dgl, g, w_in, tm, ride=None):
    T = x.shape[0]

    def body(x_ref, dres_ref, duc_ref, dqkv_ref, dmq_ref, dgl_ref, g_ref, w_ref, dx_ref, dg_ref):
        xhat, r = _rms_stats(x_ref[...])
        dh = _dot_nt(duc_ref[...], w_ref[:, 0:1024])
        dh = dh + _dot_nt(dqkv_ref[...], w_ref[:, 1024:2560])
        dh = dh + _dot_nt(dmq_ref[...], w_ref[:, 2560:3072])
        dh = dh + _dot_nt(dgl_ref[...], w_ref[:, 3072:6144])
        dx_ref[...] = dres_ref[...] + _rms_bwd(dh, g_ref[...], xhat, r)

        @pl.when(pl.program_id(0) == 0)
        def _():
            dg_ref[...] = jnp.zeros_like(dg_ref)
        dg_ref[...] += jnp.sum(dh * xhat, axis=0, keepdims=True)

    row = lambda w: pl.BlockSpec((tm, w), lambda t: (t, 0))
    return _pallas(
        body, "mix_bwd", (T // tm,),
        [row(D), row(D), row(1024), row(1536), row(512), row(3072), _const((1, D)), _const((D, IN_COLS))],
        [row(D), pl.BlockSpec((1, D), lambda t: (0, 0))],
        [jax.ShapeDtypeStruct((T, D), F32), jax.ShapeDtypeStruct((1, D), F32)],
        (x, dres, duc, dqkv, dmq, dgl, g, w_in), sem=("arbitrary",), ride=ride)


def _shifted(win, base, copies):
    for k in range(8):
        copies[k] = win[base + k:base + k + CONV_CHUNK + 24]
    return copies


def _tap_slices(copies, tap):
    out = []
    for k in range(8):
        for a in range(4):
            j = tap(a, k)
            if 0 <= j < CONV_K:
                out.append((j, copies[k, pl.ds(8 * a, CONV_CHUNK), :]))
    return out


def _conv_taps(copies, w_ref, tap):
    acc = jnp.zeros((CONV_CHUNK, CONV_W), F32)
    for j, rows in _tap_slices(copies, tap):
        acc = acc + rows * w_ref[j:j + 1, :]
    return acc


def _fold8(v):
    acc = v[0:8]
    for r in range(8, CONV_CHUNK, 8):
        acc = acc + v[r:r + 8]
    return acc


def _glu_into(uc_ref, vpad, S):
    vpad[pl.ds(0, CONV_HALO), :] = jnp.zeros((CONV_HALO, CONV_W), F32)
    vpad[pl.ds(S + CONV_HALO, CONV_HALO), :] = jnp.zeros((CONV_HALO, CONV_W), F32)

    def glu(i, carry):
        r0 = pl.multiple_of(i * GLU_CHUNK, GLU_CHUNK)
        a = uc_ref[0, pl.ds(r0, GLU_CHUNK), 0:CONV_W]
        gt = uc_ref[0, pl.ds(r0, GLU_CHUNK), CONV_W:2 * CONV_W]
        vpad[pl.ds(pl.multiple_of(r0 + CONV_HALO, CONV_HALO), GLU_CHUNK), :] = a * _sigmoid(gt)
        return carry
    lax.fori_loop(0, S // GLU_CHUNK, glu, 0)


def _layer_norm(z, vec_ref):
    xc = z - jnp.mean(z, axis=-1, keepdims=True)
    rstd = lax.rsqrt(jnp.mean(xc * xc, axis=-1, keepdims=True) + EPS)
    xn = xc * rstd
    return xn, rstd, xn * vec_ref[1:2, :] + vec_ref[2:3, :]


def _conv_fwd(uc, dw_w, vec):
    NB, S, _ = uc.shape

    def body(uc_ref, w_ref, vec_ref, o_ref, z_ref, vpad, copies):
        _glu_into(uc_ref, vpad, S)

        def conv(i, carry):
            r0 = pl.multiple_of(i * CONV_CHUNK, CONV_CHUNK)
            win = vpad[pl.ds(r0, CONV_WIN), :]
            z = _conv_taps(_shifted(win, CONV_HALO - (CONV_K - 1), copies), w_ref, lambda a, k: 8 * a + k) + vec_ref[0:1, :]
            z_ref[0, pl.ds(r0, CONV_CHUNK), :] = z
            _, _, yln = _layer_norm(z, vec_ref)
            o_ref[0, pl.ds(r0, CONV_CHUNK), :] = (yln * _sigmoid(yln)).astype(BF16)
            return carry
        lax.fori_loop(0, S // CONV_CHUNK, conv, 0, unroll=2)

    seq = pl.BlockSpec((1, S, CONV_W), lambda b: (b, 0, 0))
    return pl.pallas_call(
        body, grid=(NB,), name="conv_fwd",
        in_specs=[pl.BlockSpec((1, S, 2 * CONV_W), lambda b: (b, 0, 0)), _const((CONV_K, CONV_W)), _const((8, CONV_W))],
        out_specs=[seq, seq],
        out_shape=[jax.ShapeDtypeStruct((NB, S, CONV_W), BF16), jax.ShapeDtypeStruct((NB, S, CONV_W), F32)],
        scratch_shapes=[pltpu.VMEM((S + 2 * CONV_HALO, CONV_W), F32), pltpu.VMEM((8, CONV_CHUNK + 24, CONV_W), F32)],
        compiler_params=_params("parallel"),
    )(uc, dw_w, vec)


def _conv_bwd(uc, z, dcact, dw_w, vec, ride=None):
    NB, S, _ = uc.shape
    n_chunks = S // CONV_CHUNK

    def body(uc_ref, z_ref, dc_ref, w_ref, vec_ref, duc_ref, dw_ref, dvec_ref, vpad, dzpad, dw8, dvec8, copies):
        @pl.when(pl.program_id(0) == 0)
        def _():
            dw8[...] = jnp.zeros_like(dw8)
            dvec8[...] = jnp.zeros_like(dvec8)
        _glu_into(uc_ref, vpad, S)
        dzpad[pl.ds(S, 2 * CONV_HALO), :] = jnp.zeros((2 * CONV_HALO, CONV_W), F32)

        def norm_bwd(i, carry):
            r0 = pl.multiple_of(i * CONV_CHUNK, CONV_CHUNK)
            xn, rstd, yln = _layer_norm(z_ref[0, pl.ds(r0, CONV_CHUNK), :], vec_ref)
            sg = _sigmoid(yln)
            dyln = dc_ref[0, pl.ds(r0, CONV_CHUNK), :] * (sg * (1.0 + yln * (1.0 - sg)))
            dxn = dyln * vec_ref[1:2, :]
            dz = rstd * (dxn - jnp.mean(dxn, axis=-1, keepdims=True) - xn * jnp.mean(dxn * xn, axis=-1, keepdims=True))
            dzpad[pl.ds(r0, CONV_CHUNK), :] = dz
            dvec8[0] += _fold8(dz)
            dvec8[1] += _fold8(dyln * xn)
            dvec8[2] += _fold8(dyln)
            return carry
        lax.fori_loop(0, n_chunks, norm_bwd, 0, unroll=2)

        def taps_bwd(i, carry):
            r0 = pl.multiple_of(i * CONV_CHUNK, CONV_CHUNK)
            dzwin = dzpad[pl.ds(r0, CONV_WIN), :]
            dv = _conv_taps(_shifted(dzwin, 0, copies), w_ref, lambda a, k: CONV_K - 1 - 8 * a - k)
            dz = dzwin[0:CONV_CHUNK]
            vwin = vpad[pl.ds(r0, CONV_WIN), :]
            for j, rows in _tap_slices(_shifted(vwin, CONV_HALO - (CONV_K - 1), copies), lambda a, k: 8 * a + k):
                dw8[j] += _fold8(dz * rows)
            a = uc_ref[0, pl.ds(r0, CONV_CHUNK), 0:CONV_W]
            sg = _sigmoid(uc_ref[0, pl.ds(r0, CONV_CHUNK), CONV_W:2 * CONV_W])
            duc_ref[0, pl.ds(r0, CONV_CHUNK), 0:CONV_W] = (dv * sg).astype(BF16)
            duc_ref[0, pl.ds(r0, CONV_CHUNK), CONV_W:2 * CONV_W] = (dv * a * sg * (1.0 - sg)).astype(BF16)
            return carry
        lax.fori_loop(0, n_chunks, taps_bwd, 0, unroll=2)

        @pl.when(pl.program_id(0) == NB - 1)
        def _():
            dw_ref[...] = jnp.zeros_like(dw_ref)
            dvec_ref[...] = jnp.zeros_like(dvec_ref)
            for j in range(CONV_K):
                dw_ref[j:j + 1, :] = jnp.sum(dw8[j], axis=0, keepdims=True)
            for j in range(3):
                dvec_ref[j:j + 1, :] = jnp.sum(dvec8[j], axis=0, keepdims=True)

    return _pallas(
        body, "conv_bwd", (NB,),
        [pl.BlockSpec((1, S, 2 * CONV_W), lambda b: (b, 0, 0)), pl.BlockSpec((1, S, CONV_W), lambda b: (b, 0, 0)),
         pl.BlockSpec((1, S, CONV_W), lambda b: (b, 0, 0)), _const((CONV_K, CONV_W)), _const((8, CONV_W))],
        [pl.BlockSpec((1, S, 2 * CONV_W), lambda b: (b, 0, 0)), pl.BlockSpec((32, CONV_W), lambda b: (0, 0)),
         pl.BlockSpec((8, CONV_W), lambda b: (0, 0))],
        [jax.ShapeDtypeStruct((NB, S, 2 * CONV_W), BF16), jax.ShapeDtypeStruct((32, CONV_W), F32),
         jax.ShapeDtypeStruct((8, CONV_W), F32)],
        (uc, z, dcact, dw_w, vec),
        scratch_shapes=[pltpu.VMEM((S + 2 * CONV_HALO, CONV_W), F32), pltpu.VMEM((S + 2 * CONV_HALO, CONV_W), F32),
                        pltpu.VMEM((CONV_K, 8, CONV_W), F32), pltpu.VMEM((3, 8, CONV_W), F32),
                        pltpu.VMEM((8, CONV_CHUNK + 24, CONV_W), F32)],
        sem=("arbitrary",), ride=ride)


def _rel_index_of_column(cols):
    offset = jnp.where(cols < KWIN, cols, cols - DS_LANES)
    return jnp.clip(KPAD - offset, -(CHUNK - 1), MAX_REL) + (CHUNK - 1)


def _bias_table(rel_bias):
    def body(rb_ref, o_ref, by_offset, first8):
        ridx = _rel_index_of_column(lax.broadcasted_iota(jnp.int32, (1, DS_LANES), 1))
        onehot = (ridx == lax.broadcasted_iota(jnp.int32, (N_REL, 1), 0)).astype(F32)
        by_offset[...] = jnp.dot(rb_ref[...], onehot, preferred_element_type=F32, precision=lax.Precision.HIGHEST)
        sub = lax.broadcasted_iota(jnp.int32, (8, 1), 0)
        kchunk = lax.broadcasted_iota(jnp.int32, (1, KWIN), 1) // CHUNK
        for head in range(ATT_HEADS):
            base = jnp.broadcast_to(by_offset[head:head + 1, :], (8, DS_LANES))
            rows = base
            for s in range(1, 8):
                rows = jnp.where(sub == s, pltpu.roll(base, s, 1), rows)
            first8[head] = rows

        def rows8(q8, carry):
            qchunk = (q8 * 8 + sub) // CHUNK
            band = (kchunk >= qchunk) & (kchunk <= qchunk + LEFT_CHUNKS)
            for head in range(ATT_HEADS):
                tile = pltpu.roll(first8[head], q8 * 8, 1)[:, 0:KWIN]
                o_ref[head, pl.ds(pl.multiple_of(q8 * 8, 8), 8), :] = jnp.where(band, tile, MASK_VALUE)
            return carry
        lax.fori_loop(0, QB // 8, rows8, 0)

    return pl.pallas_call(body, out_shape=jax.ShapeDtypeStruct((ATT_HEADS, QB, KWIN), F32), name="bias_table",
                          scratch_shapes=[pltpu.VMEM((ATT_HEADS, DS_LANES), F32), pltpu.VMEM((ATT_HEADS, 8, DS_LANES), F32)],
                          compiler_params=_params())(rel_bias)


def _load_keys(i, k_ref, v_ref, kpad, vpad, S):
    @pl.when(i == 0)
    def _():
        kpad[pl.ds(0, KPAD), :] = jnp.zeros((KPAD, ATT_W), BF16)
        vpad[pl.ds(0, KPAD), :] = jnp.zeros((KPAD, ATT_W), BF16)
        kpad[pl.ds(KPAD, S), :] = k_ref[0]
        vpad[pl.ds(KPAD, S), :] = v_ref[0]


def _att_probs(q2, k2, tab_ref, head, in_head, in_seq):
    qm = jnp.where(in_head, q2, jnp.zeros_like(q2))
    s = _dot_nt(qm, k2) * (ATT_HD ** -0.5) + tab_ref[head]
    s = jnp.where(in_seq, s, MASK_VALUE)
    e = jnp.exp(s - jnp.max(s, axis=-1, keepdims=True))
    return e * (1.0 / jnp.sum(e, axis=-1, keepdims=True))


def _att_fwd(qkv, tab, ride=None):
    NB, S, _ = qkv.shape

    def body(q_ref, k_ref, v_ref, tab_ref, o_ref, kpad, vpad):
        i = pl.program_id(1)
        _load_keys(i, k_ref, v_ref, kpad, vpad, S)
        koff = pl.multiple_of(i * QB, QB)
        lane = lax.broadcasted_iota(jnp.int32, (1, 128), 1)
        in_seq = (lax.broadcasted_iota(jnp.int32, (1, KWIN), 1) + i * QB) >= KPAD
        for pair in range(ATT_HEADS // 2):
            cols = slice(pair * 128, (pair + 1) * 128)
            q2 = q_ref[0, :, cols]
            k2 = kpad[pl.ds(koff, KWIN), cols]
            v2 = vpad[pl.ds(koff, KWIN), cols]
            o2 = jnp.zeros((QB, 128), F32)
            for hh in range(2):
                in_head = (lane // ATT_HD) == hh
                p = _att_probs(q2, k2, tab_ref, 2 * pair + hh, in_head, in_seq)
                o2 = jnp.where(in_head, _dot(p.astype(BF16), v2), o2)
            o_ref[0, :, cols] = o2.astype(BF16)

    seq = lambda col: pl.BlockSpec((1, S, ATT_W), lambda b, i: (b, 0, col), pipeline_mode=pl.Buffered(1))
    outs, rode = _pallas(
        body, "att_fwd", (NB, S // QB),
        [pl.BlockSpec((1, QB, ATT_W), lambda b, i: (b, i, 0)), seq(1), seq(2), _const((ATT_HEADS, QB, KWIN))],
        [pl.BlockSpec((1, QB, ATT_W), lambda b, i: (b, i, 0))], [jax.ShapeDtypeStruct((NB, S, ATT_W), BF16)],
        (qkv, qkv, qkv, tab),
        scratch_shapes=[pltpu.VMEM((S + KPAD, ATT_W), BF16), pltpu.VMEM((S + KPAD, ATT_W), BF16)],
        sem=("arbitrary", "arbitrary"), ride=ride)
    return outs[0], rode


def _att_bwd(qkv, do, tab, ride=None):
    NB, S, _ = qkv.shape
    nq = S // QB

    def body(q_ref, k_ref, v_ref, do_ref, tab_ref, dqkv_ref, ds_hbm, kpad, vpad, dkpad, dvpad, ds_acc, ds_sem):
        b, i = pl.program_id(0), pl.program_id(1)
        _load_keys(i, k_ref, v_ref, kpad, vpad, S)

        @pl.when(i == 0)
        def _():
            dkpad[...] = jnp.zeros_like(dkpad)
            dvpad[...] = jnp.zeros_like(dvpad)

        @pl.when((i == 0) & (b == 0))
        def _():
            ds_acc[...] = jnp.zeros_like(ds_acc)

        koff = pl.multiple_of(i * QB, QB)
        lane = lax.broadcasted_iota(jnp.int32, (1, 128), 1)
        in_seq = (lax.broadcasted_iota(jnp.int32, (1, KWIN), 1) + i * QB) >= KPAD
        for pair in range(ATT_HEADS // 2):
            cols = slice(pair * 128, (pair + 1) * 128)
            q2 = q_ref[0, :, cols]
            do2 = do_ref[0, :, cols]
            k2 = kpad[pl.ds(koff, KWIN), cols]
            v2 = vpad[pl.ds(koff, KWIN), cols]
            dq2 = jnp.zeros((QB, 128), F32)
            dk2 = jnp.zeros((KWIN, 128), F32)
            dv2 = jnp.zeros((KWIN, 128), F32)
            for hh in range(2):
                head = 2 * pair + hh
                in_head = (lane // ATT_HD) == hh
                p = _att_probs(q2, k2, tab_ref, head, in_head, in_seq)
                dom = jnp.where(in_head, do2, jnp.zeros_like(do2))
                dp = _dot_nt(dom, v2)
                ds = p * (dp - jnp.sum(p * dp, axis=-1, keepdims=True))
                ds_acc[head] += ds
                dss = (ds * (ATT_HD ** -0.5)).astype(BF16)
                dq2 = jnp.where(in_head, _dot(dss, k2), dq2)
                dk2 = jnp.where(in_head, _dot_tn(dss, q2), dk2)
                dv2 = jnp.where(in_head, _dot_tn(p.astype(BF16), do2), dv2)
            dqkv_ref[0, pl.ds(koff, QB), cols] = dq2.astype(BF16)
            dkpad[pl.ds(koff, KWIN), cols] += dk2
            dvpad[pl.ds(koff, KWIN), cols] += dv2

        @pl.when(i == nq - 1)
        def _():
            dqkv_ref[0, :, ATT_W:2 * ATT_W] = dkpad[pl.ds(KPAD, S), :].astype(BF16)
            dqkv_ref[0, :, 2 * ATT_W:3 * ATT_W] = dvpad[pl.ds(KPAD, S), :].astype(BF16)

        @pl.when((i == nq - 1) & (b == NB - 1))
        def _():
            out = pltpu.make_async_copy(ds_acc, ds_hbm, ds_sem)
            out.start()
            out.wait()

    seq = lambda col: pl.BlockSpec((1, S, ATT_W), lambda b, i: (b, 0, col), pipeline_mode=pl.Buffered(1))
    return _pallas(
        body, "att_bwd", (NB, nq),
        [pl.BlockSpec((1, QB, ATT_W), lambda b, i: (b, i, 0)), seq(1), seq(2),
         pl.BlockSpec((1, QB, ATT_W), lambda b, i: (b, i, 0)), _const((ATT_HEADS, QB, KWIN))],
        [pl.BlockSpec((1, S, 3 * ATT_W), lambda b, i: (b, 0, 0)), ANY],
        [jax.ShapeDtypeStruct((NB, S, 3 * ATT_W), BF16), jax.ShapeDtypeStruct((ATT_HEADS, QB, KWIN), F32)],
        (qkv, qkv, qkv, do, tab),
        scratch_shapes=[pltpu.VMEM((S + KPAD, ATT_W), BF16), pltpu.VMEM((S + KPAD, ATT_W), BF16),
                        pltpu.VMEM((S + KPAD, ATT_W), F32), pltpu.VMEM((S + KPAD, ATT_W), F32),
                        pltpu.VMEM((ATT_HEADS, QB, KWIN), F32), pltpu.SemaphoreType.DMA],
        sem=("arbitrary", "arbitrary"), ride=ride)


def _rel_bias_grad(ds):
    def body(ds_ref, o_ref):
        sub = lax.broadcasted_iota(jnp.int32, (8, 1), 0)
        ridx = _rel_index_of_column(lax.broadcasted_iota(jnp.int32, (DS_LANES, 1), 0))
        onehot = (ridx == lax.broadcasted_iota(jnp.int32, (1, N_REL), 1)).astype(F32)
        def rows8(q8, accs):
            shift = lax.rem(DS_LANES - q8 * 8, DS_LANES)
            out = []
            for head in range(ATT_HEADS):
                tile = ds_ref[head, pl.ds(pl.multiple_of(q8 * 8, 8), 8), :]
                tile = jnp.concatenate([tile, jnp.zeros((8, DS_LANES - KWIN), F32)], axis=1)
                out.append(accs[head] + pltpu.roll(tile, shift, 1))
            return tuple(out)
        accs = lax.fori_loop(0, QB // 8, rows8, tuple(jnp.zeros((8, DS_LANES), F32) for _ in range(ATT_HEADS)))
        for head in range(ATT_HEADS):
            acc = accs[head]
            diag = jnp.zeros((8, DS_LANES), F32)
            for s in range(8):
                shifted = acc if s == 0 else pltpu.roll(acc, DS_LANES - s, 1)
                diag = jnp.where(sub == s, shifted, diag)
            z = jnp.sum(diag, axis=0, keepdims=True)
            o_ref[head:head + 1, :] = jnp.dot(z, onehot, preferred_element_type=F32, precision=lax.Precision.HIGHEST)

    return pl.pallas_call(body, out_shape=jax.ShapeDtypeStruct((ATT_HEADS, N_REL), F32), name="rel_bias_grad",
                          compiler_params=_params())(ds)


def _memkv_fwd(mem, g, w_kv, tm):
    R = mem.shape[0]
    tm = min(tm, R)

    def body(m_ref, g_ref, w_ref, h_ref, kv_ref):
        xhat, _ = _rms_stats(m_ref[...])
        h = (xhat * g_ref[...]).astype(BF16)
        h_ref[...] = h
        kv_ref[...] = _dot(h, w_ref[...]).astype(BF16)

    row = pl.BlockSpec((tm, D), lambda t: (t, 0))
    return pl.pallas_call(
        body, grid=(R // tm,), name="memkv_fwd", in_specs=[row, _const((1, D)), _const((D, 2 * MEM_W))], out_specs=[row, row],
        out_shape=[jax.ShapeDtypeStruct((R, D), BF16), jax.ShapeDtypeStruct((R, 2 * MEM_W), BF16)],
        compiler_params=_params("parallel"),
    )(mem, g, w_kv)


def _memkv_bwd(mem, dkv, w_kv, tm):
    R = mem.shape[0]
    tm = min(tm, R)

    def body(m_ref, dkv_ref, w_ref, dg_ref):
        xhat, _ = _rms_stats(m_ref[...])
        dh = _dot_nt(dkv_ref[...].astype(BF16), w_ref[...])

        @pl.when(pl.program_id(0) == 0)
        def _():
            dg_ref[...] = jnp.zeros_like(dg_ref)
        dg_ref[...] += jnp.sum(dh * xhat, axis=0, keepdims=True)

    row = pl.BlockSpec((tm, D), lambda t: (t, 0))
    return pl.pallas_call(
        body, grid=(R // tm,), name="memkv_bwd", in_specs=[row, row, _const((D, 2 * MEM_W))],
        out_specs=pl.BlockSpec((1, D), lambda t: (0, 0)), out_shape=jax.ShapeDtypeStruct((1, D), F32),
        compiler_params=_params("arbitrary"),
    )(mem, dkv, w_kv)


def _mem_probs(qh, kh):
    s = _dot_nt(qh, kh) * (MEM_HD ** -0.5)
    e = jnp.exp(s - jnp.max(s, axis=-1, keepdims=True))
    return e * (1.0 / jnp.sum(e, axis=-1, keepdims=True))


def _mematt_fwd(mq, kv, tq):
    NB, S, _ = mq.shape
    M = kv.shape[1]

    def body(q_ref, kv_ref, o_ref):
        for h in range(MEM_HEADS):
            cols = slice(h * MEM_HD, (h + 1) * MEM_HD)
            p = _mem_probs(q_ref[0, :, cols], kv_ref[0, :, cols])
            o_ref[0, :, cols] = _dot(p.astype(BF16), kv_ref[0, :, MEM_W + h * MEM_HD:MEM_W + (h + 1) * MEM_HD]).astype(BF16)

    return pl.pallas_call(
        body, grid=(NB, S // tq), name="mematt_fwd",
        in_specs=[pl.BlockSpec((1, tq, MEM_W), lambda b, i: (b, i, 0)), pl.BlockSpec((1, M, 2 * MEM_W), lambda b, i: (b, 0, 0))],
        out_specs=pl.BlockSpec((1, tq, MEM_W), lambda b, i: (b, i, 0)),
        out_shape=jax.ShapeDtypeStruct((NB, S, MEM_W), BF16), compiler_params=_params("parallel", "parallel"),
    )(mq, kv)


def _mematt_bwd(mq, kv, do, tq):
    NB, S, _ = mq.shape
    M = kv.shape[1]

    def body(q_ref, kv_ref, do_ref, dq_ref, dkv_ref):
        @pl.when(pl.program_id(1) == 0)
        def _():
            dkv_ref[...] = jnp.zeros_like(dkv_ref)
        for h in range(MEM_HEADS):
            cols = slice(h * MEM_HD, (h + 1) * MEM_HD)
            vcols = slice(MEM_W + h * MEM_HD, MEM_W + (h + 1) * MEM_HD)
            qh, kh, vh, doh = q_ref[0, :, cols], kv_ref[0, :, cols], kv_ref[0, :, vcols], do_ref[0, :, cols]
            p = _mem_probs(qh, kh)
            dp = _dot_nt(doh, vh)
            ds = p * (dp - jnp.sum(p * dp, axis=-1, keepdims=True))
            dss = (ds * (MEM_HD ** -0.5)).astype(BF16)
            dq_ref[0, :, cols] = _dot(dss, kh).astype(BF16)
            dkv_ref[0, :, cols] += _dot_tn(dss, qh)
            dkv_ref[0, :, vcols] += _dot_tn(p.astype(BF16), doh)

    qspec = pl.BlockSpec((1, tq, MEM_W), lambda b, i: (b, i, 0))
    kvspec = pl.BlockSpec((1, M, 2 * MEM_W), lambda b, i: (b, 0, 0))
    return pl.pallas_call(
        body, grid=(NB, S // tq), name="mematt_bwd", in_specs=[qspec, kvspec, qspec], out_specs=[qspec, kvspec],
        out_shape=[jax.ShapeDtypeStruct((NB, S, MEM_W), BF16), jax.ShapeDtypeStruct((NB, M, 2 * MEM_W), F32)],
        compiler_params=_params("arbitrary", "arbitrary"),
    )(mq, kv, do)


def _branch(j, in_ref, w_ref, gl_ref, bg_ref):
    y = _dot(in_ref[...], w_ref[...])
    gate = _sigmoid(gl_ref[:, j * D:(j + 1) * D] + bg_ref[:, j * D:(j + 1) * D])
    return y, gate


def _combine_fwd(x, cact, oatt, omem, gl, bg, wpw, wo, wmo, wout, tm):
    T = x.shape[0]

    def body(x_ref, c_ref, a_ref, m_ref, gl_ref, bg_ref, wpw_ref, wo_ref, wmo_ref, wout_ref, xo_ref, y_ref):
        y = None
        for j, (in_ref, w_ref) in enumerate(((c_ref, wpw_ref), (a_ref, wo_ref), (m_ref, wmo_ref))):
            yj, gate = _branch(j, in_ref, w_ref, gl_ref, bg_ref)
            y = gate * yj if y is None else y + gate * yj
        y = y.astype(BF16)
        y_ref[...] = y
        xo_ref[...] = x_ref[...] + _dot(y, wout_ref[...])

    row = lambda w: pl.BlockSpec((tm, w), lambda t: (t, 0))
    wbr = _const((512, D))
    return pl.pallas_call(
        body, grid=(T // tm,), name="combine_fwd",
        in_specs=[row(D), row(512), row(512), row(512), row(3 * D), _const((1, 3 * D)), wbr, wbr, wbr, _const((D, D))],
        out_specs=[row(D), row(D)],
        out_shape=[jax.ShapeDtypeStruct((T, D), F32), jax.ShapeDtypeStruct((T, D), BF16)],
        compiler_params=_params("parallel"),
    )(x, cact, oatt, omem, gl, bg, wpw, wo, wmo, wout)


def _combine_bwd(dx, cact, oatt, omem, gl, bg, wpw, wo, wmo, wout, tm, ride=None):
    T = dx.shape[0]

    def body(dx_ref, c_ref, a_ref, m_ref, gl_ref, bg_ref, wpw_ref, wo_ref, wmo_ref, wout_ref,
             dgl_ref, dc_ref, da_ref, dm_ref, dyc_ref, dya_ref, dym_ref, dbg_ref):
        dy = _dot_nt(dx_ref[...].astype(BF16), wout_ref[...])

        @pl.when(pl.program_id(0) == 0)
        def _():
            dbg_ref[...] = jnp.zeros_like(dbg_ref)
        branches = ((c_ref, wpw_ref, dyc_ref, dc_ref), (a_ref, wo_ref, dya_ref, da_ref), (m_ref, wmo_ref, dym_ref, dm_ref))
        for j, (in_ref, w_ref, dyb_ref, din_ref) in enumerate(branches):
            yj, gate = _branch(j, in_ref, w_ref, gl_ref, bg_ref)
            dyg = dy * gate
            dlogit = dyg * yj * (1.0 - gate)
            dgl_ref[:, j * D:(j + 1) * D] = dlogit.astype(BF16)
            dbg_ref[:, j * D:(j + 1) * D] += jnp.sum(dlogit, axis=0, keepdims=True)
            dyb = dyg.astype(BF16)
            dyb_ref[...] = dyb
            din_ref[...] = _dot_nt(dyb, w_ref[...]).astype(din_ref.dtype)

    row = lambda w: pl.BlockSpec((tm, w), lambda t: (t, 0))
    wbr = _const((512, D))
    sds = jax.ShapeDtypeStruct
    return _pallas(
        body, "combine_bwd", (T // tm,),
        [row(D), row(512), row(512), row(512), row(3 * D), _const((1, 3 * D)), wbr, wbr, wbr, _const((D, D))],
        [row(3 * D), row(512), row(512), row(512), row(D), row(D), row(D), pl.BlockSpec((1, 3 * D), lambda t: (0, 0))],
        [sds((T, 3 * D), BF16), sds((T, 512), F32), sds((T, 512), BF16), sds((T, 512), BF16),
         sds((T, D), BF16), sds((T, D), BF16), sds((T, D), BF16), sds((1, 3 * D), F32)],
        (dx, cact, oatt, omem, gl, bg, wpw, wo, wmo, wout), sem=("arbitrary",), ride=ride)


def _final(x, g, target, tm):
    T = x.shape[0]

    def body(x_ref, g_ref, t_ref, loss_ref, dx_ref, dg_ref):
        xhat, r = _rms_stats(x_ref[...])
        gain = g_ref[...]
        diff = xhat * gain - t_ref[...]
        dout = diff * (1.0 / D)

        @pl.when(pl.program_id(0) == 0)
        def _():
            loss_ref[...] = jnp.zeros_like(loss_ref)
            dg_ref[...] = jnp.zeros_like(dg_ref)
        sq = jnp.sum(jnp.sum(diff * diff, axis=0, keepdims=True), axis=1, keepdims=True)
        loss_ref[...] += jnp.broadcast_to(sq * (0.5 / D), (1, 128))
        dg_ref[...] += jnp.sum(dout * xhat, axis=0, keepdims=True)
        dx_ref[...] = _rms_bwd(dout, gain, xhat, r)

    row = pl.BlockSpec((tm, D), lambda t: (t, 0))
    return pl.pallas_call(
        body, grid=(T // tm,), name="final_loss", in_specs=[row, _const((1, D)), row],
        out_specs=[pl.BlockSpec((1, 128), lambda t: (0, 0)), row, pl.BlockSpec((1, D), lambda t: (0, 0))],
        out_shape=[jax.ShapeDtypeStruct((1, 128), F32), jax.ShapeDtypeStruct((T, D), F32), jax.ShapeDtypeStruct((1, D), F32)],
        compiler_params=_params("arbitrary"),
    )(x, g, target)


def _peer(x, y, c, rel):
    rx, ry, rc = (rel >> 2) & 1, (rel >> 1) & 1, rel & 1
    return ((1 - x) if rx else x, (1 - y) if ry else y, (1 - c) if rc else c)


def _all_sum_small(parts):
    n = len(parts)

    def body(*refs):
        p_refs, o_refs, slots = refs[:n], refs[n:2 * n], refs[2 * n:3 * n]
        send_sems, recv_sems = refs[3 * n:]
        x, y, c = _my_coords()
        me = _dev_index(x, y, c)

        def copy(i, rel, arrival):
            peer = _peer(x, y, c, rel)
            return pltpu.make_async_remote_copy(
                src_ref=p_refs[i], dst_ref=slots[i].at[_dev_index(*peer) if arrival else me],
                send_sem=send_sems.at[i, rel - 1], recv_sem=recv_sems.at[i, rel - 1], device_id=peer, device_id_type=MESH)

        for i in range(n):
            slots[i][me] = p_refs[i][...]
        for rel in range(1, NDEV):
            for i in range(n):
                copy(i, rel, False).start()
        for rel in range(1, NDEV):
            for i in range(n):
                copy(i, rel, True).wait_recv()
        for rel in range(1, NDEV):
            for i in range(n):
                copy(i, rel, False).wait_send()
        for i in range(n):
            total = slots[i][0]
            for d in range(1, NDEV):
                total = total + slots[i][d]
            o_refs[i][...] = total

    vmem = pl.BlockSpec(memory_space=pltpu.VMEM)
    return pl.pallas_call(
        body, out_shape=[jax.ShapeDtypeStruct(p.shape, F32) for p in parts], name="all_sum_small",
        in_specs=[vmem] * n, out_specs=[vmem] * n,
        scratch_shapes=[pltpu.VMEM((NDEV,) + p.shape, F32) for p in parts]
        + [pltpu.SemaphoreType.DMA((n, NDEV - 1)), pltpu.SemaphoreType.DMA((n, NDEV - 1))],
        compiler_params=pltpu.CompilerParams(has_side_effects=True),
    )(*parts)


HBM = pl.BlockSpec(memory_space=pltpu.HBM)
SEM = pl.BlockSpec(memory_space=pltpu.SEMAPHORE)


def _own_block(g, kind, m):
    def body(g_ref, land_ref, sem):
        me = _dev_index(*_my_coords())
        cp = pltpu.make_async_copy(_window(g_ref, kind, m, me), land_ref.at[me], sem)
        cp.start()
        cp.wait()

    block = (m, g.shape[1]) if kind == 'row' else (g.shape[0], m)
    return pl.pallas_call(body, in_specs=[ANY], out_specs=ANY, out_shape=jax.ShapeDtypeStruct((NDEV,) + block, g.dtype),
                          scratch_shapes=[pltpu.SemaphoreType.DMA], name="own_block_last")(g)


def _scatter_start(g, land, kind, m):
    def body(g_ref, land_ref, send_sems, recv_sems, g_thru, land_thru, token):
        x, y, c = _my_coords()
        me = _dev_index(x, y, c)
        for rel in range(1, NDEV):
            peer = _peer(x, y, c, rel)
            pltpu.make_async_remote_copy(src_ref=_window(g_ref, kind, m, _dev_index(*peer)), dst_ref=land_ref.at[me],
                                         send_sem=send_sems.at[rel - 1], recv_sem=recv_sems.at[rel - 1],
                                         device_id=peer, device_id_type=MESH).start()
        token[...] = jnp.zeros_like(token)

    return pl.pallas_call(
        body, name="scatter_last_start",
        out_shape=(pltpu.SemaphoreType.DMA((NDEV - 1,)), pltpu.SemaphoreType.DMA((NDEV - 1,)), pltpu.HBM(g.shape, g.dtype),
                   pltpu.HBM(land.shape, land.dtype), jax.ShapeDtypeStruct((8, 128), F32)),
        in_specs=(HBM, HBM), out_specs=(SEM, SEM, HBM, HBM, pl.BlockSpec(memory_space=pltpu.VMEM)),
        input_output_aliases={0: 2, 1: 3},
        compiler_params=pltpu.CompilerParams(has_side_effects=pltpu.SideEffectType.DATAFLOW_SIDE_EFFECTING),
    )(pltpu.with_memory_space_constraint(g, pltpu.HBM), pltpu.with_memory_space_constraint(land, pltpu.HBM))


def _scatter_wait(send_sems, recv_sems, g_thru, land_thru, after, kind, m):
    n_after = len(after)

    def body(*refs):
        g_ref, land_ref, send_sems, recv_sems = refs[:4]
        x, y, c = _my_coords()
        me = _dev_index(x, y, c)
        for rel in range(1, NDEV):
            peer = _peer(x, y, c, rel)
            dev = _dev_index(*peer)
            cp = pltpu.make_async_remote_copy(src_ref=_window(g_ref, kind, m, me), dst_ref=land_ref.at[dev],
                                              send_sem=send_sems.at[rel - 1], recv_sem=recv_sems.at[rel - 1],
                                              device_id=peer, device_id_type=MESH)
            cp.wait_send()
            cp.wait_recv()

    return pl.pallas_call(
        body, name="scatter_last_wait",
        out_shape=(pltpu.HBM(g_thru.shape, g_thru.dtype), pltpu.HBM(land_thru.shape, land_thru.dtype)),
        in_specs=(HBM, HBM, SEM, SEM) + (ANY,) * n_after, out_specs=(HBM, HBM), input_output_aliases={0: 0, 1: 1},
        compiler_params=pltpu.CompilerParams(has_side_effects=pltpu.SideEffectType.DATAFLOW_SIDE_EFFECTING),
    )(g_thru, land_thru, send_sems, recv_sems, *after)[1]


def _adamw_math(w, g, m, v):
    m = ADAM_B1 * m + (1.0 - ADAM_B1) * g
    v = ADAM_B2 * v + (1.0 - ADAM_B2) * (g * g)
    m_hat = m / (1.0 - ADAM_B1 ** ADAM_STEP)
    v_hat = v / (1.0 - ADAM_B2 ** ADAM_STEP)
    delta = -ADAM_LR * (m_hat / (jnp.sqrt(v_hat) + ADAM_EPS) + ADAM_WD * w)
    return delta, m, v


def _sum_adamw(parts, w, m, v, name, after=None):
    R, C = w.shape
    n_parts = len(parts)
    cg = C // n_parts
    tr = max(t for t in range(8, 257, 8) if R % t == 0)
    deps = [] if after is None else [after]

    def body(*refs):
        p_refs = refs[:n_parts]
        w_ref, m_ref, v_ref = refs[n_parts:n_parts + 3]
        g_ref, d_ref, mo_ref, vo_ref = refs[n_parts + 3 + len(deps):]
        for k, p_ref in enumerate(p_refs):
            @pl.when(pl.program_id(0) == k)
            def _():
                g = p_ref[0].astype(F32)
                for d in range(1, NDEV):
                    g = g + p_ref[d].astype(F32)
                g_ref[...] = g
                d_ref[...], mo_ref[...], vo_ref[...] = _adamw_math(w_ref[...], g, m_ref[...], v_ref[...])

    part = pl.BlockSpec((NDEV, tr, cg), lambda k, t: (0, t, 0))
    blk = pl.BlockSpec((tr, cg), lambda k, t: (t, k))
    return pl.pallas_call(
        body, grid=(n_parts, R // tr), name=name, in_specs=[part] * n_parts + [blk, blk, blk] + [ANY] * len(deps),
        out_specs=[blk] * 4, out_shape=[jax.ShapeDtypeStruct((R, C), F32)] * 4, compiler_params=_params("parallel", "parallel"),
    )(*parts, w, m, v, *deps)


def _adamw_small(ws, gs, ms, vs):
    n = len(ws)

    def body(*refs):
        w_refs, g_refs, m_refs, v_refs = (refs[k * n:(k + 1) * n] for k in range(4))
        d_refs, mo_refs, vo_refs = (refs[(4 + k) * n:(5 + k) * n] for k in range(3))
        for i in range(n):
            d_refs[i][...], mo_refs[i][...], vo_refs[i][...] = _adamw_math(w_refs[i][...], g_refs[i][...], m_refs[i][...], v_refs[i][...])

    shapes = [jax.ShapeDtypeStruct(a.shape, F32) for a in ws]
    outs = pl.pallas_call(body, out_shape=shapes * 3, name="adamw_small", compiler_params=_params())(*ws, *gs, *ms, *vs)
    return outs[:n], outs[n:2 * n], outs[2 * n:]


def kernel(x, mem, ffn1_norm, ffn1_w_up, ffn1_w_down, mix_norm, mem_norm, w_in, b_gate, conv_dw_w, conv_dw_b, conv_ln_g, conv_ln_b, conv_w_pw, att_rel_bias, att_w_o, mem_w_kv, mem_w_o, w_out, ffn2_norm, ffn2_w_up, ffn2_w_down, final_norm, loss_target, m_ffn1_norm, m_ffn1_w_up, m_ffn1_w_down, m_mix_norm, m_mem_norm, m_w_in, m_b_gate, m_conv_dw_w, m_conv_dw_b, m_conv_ln_g, m_conv_ln_b, m_conv_w_pw, m_att_rel_bias, m_att_w_o, m_mem_w_kv, m_mem_w_o, m_w_out, m_ffn2_norm, m_ffn2_w_up, m_ffn2_w_down, m_final_norm, v_ffn1_norm, v_ffn1_w_up, v_ffn1_w_down, v_mix_norm, v_mem_norm, v_w_in, v_b_gate, v_conv_dw_w, v_conv_dw_b, v_conv_ln_g, v_conv_ln_b, v_conv_w_pw, v_att_rel_bias, v_att_w_o, v_mem_w_kv, v_mem_w_o, v_w_out, v_ffn2_norm, v_ffn2_w_up, v_ffn2_w_down, v_final_norm):
    given = dict(locals())
    w = {n: given[n] for n in WEIGHTS}
    mom = {n: given["m_" + n] for n in WEIGHTS}
    var = {n: given["v_" + n] for n in WEIGHTS}

    NB, S, _ = x.shape
    T = NB * S
    ML = mem.shape[1]
    x0 = x.reshape(T, D)
    target = loss_target.reshape(T, D)
    mem2 = mem.reshape(NB * ML, D)

    def block(t, n):
        return jnp.transpose(t[0]) if n in TRANSPOSED else t[0]

    sh = dict(zip(BIG_ORDER, _cast_shards([block(w[n], n) for n in BIG_ORDER])))
    dw_t = jnp.transpose(conv_dw_w[0])

    def gather(names, extra=(), extra_kinds=()):
        return _gather_ride([sh[n] for n in names] + list(extra), [BIG[n] for n in names] + list(extra_kinds))

    W = {}
    names0 = ['ffn1_w_up', 'ffn1_w_down']
    got = _exchange_alone(gather(names0, [dw_t], [('row', dw_t.shape[0])]), "gather_ffn1")
    W.update(zip(names0, got[:2]))
    dw_full = jnp.transpose(got[2])
    conv_vec = jnp.concatenate([conv_dw_b, conv_ln_g, conv_ln_b, jnp.zeros((5, CONV_W), F32)], axis=0)
    tab = _bias_table(att_rel_bias[0])
    fin_g = final_norm.reshape(1, D)

    names1 = ['w_in', 'conv_w_pw', 'att_w_o', 'mem_w_kv', 'mem_w_o', 'w_out']
    (x1, ab1), got = _ffn_fwd(x0, ffn1_norm, W['ffn1_w_up'], W['ffn1_w_down'], TILE_FFN, "ffn1_fwd", ride=gather(names1))
    W.update(zip(names1, got))
    (uc, qkv, mq, gl, hmix), _ = _mix_fwd(x1, mix_norm, W['w_in'], TILE_TOKENS)
    uc3 = uc.reshape(NB, S, 2 * CONV_W)
    qkv3 = qkv.reshape(NB, S, 3 * ATT_W)
    mq3 = mq.reshape(NB, S, MEM_W)
    cact, conv_z = _conv_fwd(uc3, dw_full, conv_vec)
    cact = cact.reshape(T, CONV_W)
    names2 = ['ffn2_w_up', 'ffn2_w_down']
    oatt, got = _att_fwd(qkv3, tab, ride=gather(names2))
    W.update(zip(names2, got))
    oatt = oatt.reshape(T, ATT_W)
    memh, kv = _memkv_fwd(mem2, mem_norm, W['mem_w_kv'], TILE_TOKENS)
    kv3 = kv.reshape(NB, ML, 2 * MEM_W)
    omem = _mematt_fwd(mq3, kv3, TILE_TOKENS).reshape(T, MEM_W)
    branch_w = (W['conv_w_pw'], W['att_w_o'], W['mem_w_o'], W['w_out'])
    x2, ymix = _combine_fwd(x1, cact, oatt, omem, gl, b_gate, *branch_w, TILE_COMBINE)
    (x3, ab2), _ = _ffn_fwd(x2, ffn2_norm, W['ffn2_w_up'], W['ffn2_w_down'], TILE_FFN, "ffn2_fwd")
    loss_part, dx3, dg_final = _final(x3, fin_g, target, TILE_TOKENS)

    def scatter(grads, names):
        return _scatter_ride(grads, [BIG[n] for n in names])

    G, P = {}, {}
    dx2, dab2, act2, h2, dg_ffn2 = _ffn_bwd(x2, dx3, ab2, ffn2_norm, W['ffn2_w_up'], W['ffn2_w_down'], TILE_FFN, "ffn2_bwd")
    g_up, _ = _tn_matmul(dab2, h2, 512, "grad_ffn2_w_up_a", tt=TILE_GRAD_TOKENS_WIDE, x_part=(0, 2), out_rows=2 * FF)
    G['ffn2_w_up'], _ = _tn_matmul(dab2, h2, 512, "grad_ffn2_w_up_b", tt=TILE_GRAD_TOKENS_WIDE, x_part=(1, 2), out_rows=2 * FF,
                                   prev=g_up)
    G['ffn2_w_down'], _ = _tn_matmul(act2, dx3, 512, "grad_ffn2_w_down", scale=0.5, tt=TILE_GRAD_TOKENS_WIDE)
    (dgl, dcact, doatt, domem, dyc, dya, dym, dbg), got = _combine_bwd(
        dx2, cact, oatt, omem, gl, b_gate, *branch_w, TILE_COMBINE, ride=scatter([G['ffn2_w_up']], ['ffn2_w_up']))
    P['ffn2_w_up'] = got
    G['w_out'], _ = _tn_matmul(ymix, dx2, 512, "grad_w_out")
    G['conv_w_pw'], _ = _tn_matmul(cact, dyc, 512, "grad_conv_w_pw")
    G['att_w_o'], _ = _tn_matmul(oatt, dya, 512, "grad_att_w_o")
    G['mem_w_o'], _ = _tn_matmul(omem, dym, 512, "grad_mem_w_o")
    dmq3, dkv3 = _mematt_bwd(mq3, kv3, domem.reshape(NB, S, MEM_W), TILE_TOKENS)
    dkv = dkv3.reshape(NB * ML, 2 * MEM_W)
    dg_mem = _memkv_bwd(mem2, dkv, W['mem_w_kv'], TILE_TOKENS)
    G['mem_w_kv'], _ = _tn_matmul(memh, dkv, 512, "grad_mem_w_kv")
    names = ['ffn2_w_down', 'w_out', 'conv_w_pw', 'att_w_o', 'mem_w_o']
    (dqkv3, dscore), got = _att_bwd(qkv3, doatt.reshape(NB, S, ATT_W), tab, ride=scatter([G[n] for n in names], names))
    P.update((n, [p]) for n, p in zip(names, got))
    d_rel = _rel_bias_grad(dscore)
    (duc3, d_dw, d_cvec), got = _conv_bwd(uc3, conv_z, dcact.reshape(NB, S, CONV_W), dw_full, conv_vec,
                                          ride=scatter([G['mem_w_kv']], ['mem_w_kv']))
    P['mem_w_kv'] = got
    duc, dqkv, dmq = duc3.reshape(T, 2 * CONV_W), dqkv3.reshape(T, 3 * ATT_W), dmq3.reshape(T, MEM_W)
    g_in, _ = _tn_matmul(hmix, duc, 512, "grad_w_in_conv", out_cols=IN_COLS, col_off=0)
    g_in, _ = _tn_matmul(hmix, dqkv, 512, "grad_w_in_qkv", out_cols=IN_COLS, col_off=1024, prev=g_in)
    g_in, _ = _tn_matmul(hmix, dmq, 512, "grad_w_in_mq", out_cols=IN_COLS, col_off=2560, prev=g_in)
    G['w_in'], _ = _tn_matmul(hmix, dgl, 1024, "grad_w_in_gate", out_cols=IN_COLS, col_off=3072, prev=g_in)
    (dx1, dg_mix), got = _mix_bwd(x1, dx2, duc, dqkv, dmq, dgl, mix_norm, W['w_in'], TILE_TOKENS,
                                  ride=scatter([G['w_in']], ['w_in']))
    P['w_in'] = got
    dx0, dab1, act1, h1, dg_ffn1 = _ffn_bwd(x0, dx1, ab1, ffn1_norm, W['ffn1_w_up'], W['ffn1_w_down'], TILE_FFN, "ffn1_bwd")
    g_wd1, _ = _tn_matmul(act1, dx1, 512, "grad_ffn1_w_down", scale=0.5, tt=TILE_GRAD_TOKENS_WIDE)
    g_wu1a, got = _tn_matmul(dab1, h1, 512, "grad_ffn1_w_up_a", tt=TILE_GRAD_TOKENS_WIDEST, y_part=(0, 2),
                             ride=scatter([g_wd1], ['ffn1_w_down']))
    P['ffn1_w_down'] = got
    g_wu1b, got_a = _tn_matmul(dab1, h1, 512, "grad_ffn1_w_up_b", tt=TILE_GRAD_TOKENS_WIDEST, y_part=(1, 2),
                               ride=scatter([g_wu1a], ['ffn1_w_up']))
    last_kind = BIG['ffn1_w_up']
    send_sems, recv_sems, g_thru, land_thru, token = _scatter_start(g_wu1b, _own_block(g_wu1b, *last_kind), *last_kind)

    small_names = ['loss', 'ffn1_norm', 'mix_norm', 'mem_norm', 'b_gate', 'conv_dw_w', 'conv_vec', 'att_rel_bias', 'ffn2_norm',
                   'final_norm']
    small = dict(zip(small_names, _all_sum_small(
        [loss_part + token[0:1], dg_ffn1, dg_mix, dg_mem, dbg, d_dw, d_cvec, d_rel, dg_ffn2, dg_final])))
    loss = small['loss'][0, 0]
    me = _dev_index(*_my_coords())
    for i, n in enumerate(['conv_dw_b', 'conv_ln_g', 'conv_ln_b']):
        small[n] = small['conv_vec'][i:i + 1]
    small['conv_dw_w'] = lax.dynamic_slice(small['conv_dw_w'], (0, me * conv_dw_w.shape[2]), (CONV_K, conv_dw_w.shape[2]))
    little = [n for n in WEIGHTS if n not in BIG]
    as2d = lambda t, n: t.reshape(small[n].shape)
    d_s, m_s, v_s = _adamw_small([as2d(w[n], n) for n in little], [small[n] for n in little],
                                 [as2d(mom[n], n) for n in little], [as2d(var[n], n) for n in little])
    grad, delta, new_m, new_v = {}, {}, {}, {}
    for i, n in enumerate(little):
        grad[n], delta[n], new_m[n], new_v[n] = (t.reshape(w[n].shape) for t in (small[n], d_s[i], m_s[i], v_s[i]))
    done = [d_s[0]]
    for n in BIG_ORDER[1:] + BIG_ORDER[:1]:
        if n == 'ffn1_w_up':
            P[n] = [got_a[0], _scatter_wait(send_sems, recv_sems, g_thru, land_thru, done, *last_kind)]
        outs = _sum_adamw(P[n], block(w[n], n), block(mom[n], n), block(var[n], n), "adamw_" + n,
                          after=None if n == 'ffn1_w_up' else token)
        done.append(outs[0])
        grad[n], delta[n], new_m[n], new_v[n] = ((jnp.transpose(t) if n in TRANSPOSED else t)[None] for t in outs)

    return (loss, dx0.reshape(NB, S, D), *[grad[n] for n in WEIGHTS], *[delta[n] for n in WEIGHTS],
            *[new_m[n] for n in WEIGHTS], *[new_v[n] for n in WEIGHTS])
```

```python
import functools

import jax
import jax.numpy as jnp
from jax import lax
from jax.experimental import pallas as pl
from jax.experimental.pallas import tpu as pltpu

F32 = jnp.float32
BF16 = jnp.bfloat16

EPS = 1e-6
MASK_VALUE = -1e30
D = 1024
NDEV = 8
FF = 2816
FF_SHARD = 704
FF_HALF_ROWS = 352
FF_BLOCK_EDGES = ()
IN_COLS = 6144
CONV_W = 512
CONV_K = 31
CONV_HALO = 32
CONV_CHUNK = 32
CONV_WIN = CONV_CHUNK + 40
GLU_CHUNK = 128
ATT_W = 512
ATT_HEADS = 8
ATT_HD = 64
CHUNK = 64
LEFT_CHUNKS = 8
MAX_REL = 128
N_REL = 192
QB = 256
KWIN = QB + LEFT_CHUNKS * CHUNK
KPAD = LEFT_CHUNKS * CHUNK
DS_LANES = 1024
MEM_W = 512
MEM_HEADS = 4
MEM_HD = 128
ADAM_LR = 0.001
ADAM_B1 = 0.9
ADAM_B2 = 0.999
ADAM_EPS = 1e-08
ADAM_WD = 0.01
ADAM_STEP = 10
VMEM_LIMIT = 60 * 1024 * 1024
TILE_FFN = 256
TILE_COMBINE = 256
TILE_TOKENS = 512
TILE_GRAD_TOKENS = 2048
TILE_GRAD_TOKENS_WIDE = 1024
TILE_GRAD_TOKENS_WIDEST = 512

MESH = pl.DeviceIdType.MESH
ANY = pl.BlockSpec(memory_space=pl.ANY)

WEIGHTS = ['ffn1_norm', 'ffn1_w_up', 'ffn1_w_down', 'mix_norm', 'mem_norm', 'w_in', 'b_gate', 'conv_dw_w', 'conv_dw_b',
           'conv_ln_g', 'conv_ln_b', 'conv_w_pw', 'att_rel_bias', 'att_w_o', 'mem_w_kv', 'mem_w_o', 'w_out', 'ffn2_norm',
           'ffn2_w_up', 'ffn2_w_down', 'final_norm']
BIG = {
    'ffn1_w_up': ('row', FF_SHARD), 'ffn1_w_down': ('row', FF_HALF_ROWS), 'w_in': ('col', 768),
    'conv_w_pw': ('col', 128), 'att_w_o': ('col', 128), 'mem_w_kv': ('row', 128), 'mem_w_o': ('col', 128),
    'w_out': ('row', 128), 'ffn2_w_up': ('row', FF_SHARD), 'ffn2_w_down': ('row', FF_HALF_ROWS),
}
BIG_ORDER = ['ffn1_w_up', 'ffn1_w_down', 'w_in', 'conv_w_pw', 'att_w_o', 'mem_w_kv', 'mem_w_o', 'w_out', 'ffn2_w_up', 'ffn2_w_down']
TRANSPOSED = ('ffn1_w_up', 'ffn2_w_up')


def _dot(a, b):
    return jnp.dot(a, b, preferred_element_type=F32)


def _dot_nt(a, b):
    return lax.dot_general(a, b, (((1,), (1,)), ((), ())), preferred_element_type=F32)


def _dot_tn(a, b):
    return lax.dot_general(a, b, (((0,), (0,)), ((), ())), preferred_element_type=F32)


def _sigmoid(v):
    return jax.nn.sigmoid(v)


def _const(shape):
    return pl.BlockSpec(shape, lambda *_: (0,) * len(shape), pipeline_mode=pl.Buffered(1))


def _params(*sem):
    return pltpu.CompilerParams(dimension_semantics=sem if sem else None, vmem_limit_bytes=VMEM_LIMIT)


def _my_coords():
    return lax.axis_index("x"), lax.axis_index("y"), lax.axis_index("c")


def _dev_index(px, py, pc):
    return 4 * px + 2 * py + pc


def _window(ref, kind, n, p):
    if kind == 'row':
        return ref.at[pl.ds(pl.multiple_of(p * n, n), n), :]
    return ref.at[:, pl.ds(pl.multiple_of(p * n, 128), n)]


def _full_shape(kind, n, shard_shape):
    if kind == 'row':
        return (NDEV * n, shard_shape[1])
    return (shard_shape[0], NDEV * n)


def _cast_shards(shards):
    n = len(shards)

    def body(*refs):
        for i in range(n):
            refs[n + i][...] = refs[i][...].astype(BF16)

    out_shape = [jax.ShapeDtypeStruct(s.shape, BF16) for s in shards]
    return pl.pallas_call(body, out_shape=out_shape, name="cast_shards", compiler_params=_params())(*shards)


class _Ride:
    def __init__(self, inputs, out_shape, scratch, start, finish, mid=None):
        self.inputs, self.out_shape, self.scratch = list(inputs), list(out_shape), list(scratch)
        self.start, self.finish, self.mid = start, finish, mid


def _pallas(body, name, grid, in_specs, out_specs, out_shape, args, scratch_shapes=(), sem=None, aliases=None, ride=None):
    if ride is None:
        outs = pl.pallas_call(body, grid=grid, name=name, in_specs=in_specs, out_specs=out_specs, out_shape=out_shape,
                              scratch_shapes=list(scratch_shapes), input_output_aliases=aliases or {},
                              compiler_params=_params(*sem))(*args)
        return list(outs), []
    n_in, n_out, n_scr = len(args), len(out_shape), len(scratch_shapes)
    r_in, r_out = len(ride.inputs), len(ride.out_shape)

    def wrapped(*refs):
        k_in, rin = refs[:n_in], refs[n_in:n_in + r_in]
        o0 = n_in + r_in
        k_out, rout = refs[o0:o0 + n_out], refs[o0 + n_out:o0 + n_out + r_out]
        s0 = o0 + n_out + r_out
        k_scr, rscr = refs[s0:s0 + n_scr], refs[s0 + n_scr:]
        ids = [pl.program_id(k) for k in range(len(grid))]
        first = functools.reduce(jnp.logical_and, [i == 0 for i in ids])
        last = functools.reduce(jnp.logical_and, [i == g - 1 for i, g in zip(ids, grid)])
        pl.when(first)(lambda: ride.start(rin, rout, rscr))
        if ride.mid is not None:
            at_mid = functools.reduce(jnp.logical_and, [ids[0] == (3 * grid[0]) // 4] + [i == 0 for i in ids[1:]])
            pl.when(at_mid)(lambda: ride.mid(rin, rout, rscr))
        body(*k_in, *k_out, *k_scr)
        pl.when(last)(lambda: ride.finish(rin, rout, rscr))

    outs = pl.pallas_call(
        wrapped, grid=grid, name=name, in_specs=list(in_specs) + [ANY] * r_in, out_specs=list(out_specs) + [ANY] * r_out,
        out_shape=list(out_shape) + ride.out_shape, scratch_shapes=list(scratch_shapes) + ride.scratch,
        input_output_aliases=aliases or {}, compiler_params=_params(*(["arbitrary"] * len(grid))),
    )(*args, *ride.inputs)
    return list(outs[:n_out]), list(outs[n_out:])


def _exchange_alone(ride, name):
    r_in, r_out = len(ride.inputs), len(ride.out_shape)

    def body(*refs):
        rin, rout, rscr = refs[:r_in], refs[r_in:r_in + r_out], refs[r_in + r_out:]
        ride.start(rin, rout, rscr)
        if ride.mid is not None:
            ride.mid(rin, rout, rscr)
        ride.finish(rin, rout, rscr)

    return pl.pallas_call(body, out_shape=ride.out_shape, in_specs=[ANY] * r_in, out_specs=[ANY] * r_out, name=name,
                          scratch_shapes=ride.scratch, compiler_params=pltpu.CompilerParams(has_side_effects=True))(*ride.inputs)


def _gather_ride(shards, kinds):
    n = len(shards)

    def plan(rin, out, sems):
        send_sems, recv_sems, local_sems = sems[:3]
        x, y, c = _my_coords()
        me, sibling = (x, y, c), (x, y, 1 - c)
        chips = [(1 - x, y), (x, 1 - y), (1 - x, 1 - y)]

        def win(i, dev):
            return _window(out[i], kinds[i][0], kinds[i][1], _dev_index(*dev))

        def copy(i, k, block, to, from_shard=False):
            return pltpu.make_async_remote_copy(
                src_ref=rin[i] if from_shard else win(i, block), dst_ref=win(i, block),
                send_sem=send_sems.at[i, k], recv_sem=recv_sems.at[i, k], device_id=to, device_id_type=MESH)

        def local():
            return [pltpu.make_async_copy(rin[i], win(i, me), local_sems.at[i]) for i in range(n)]

        def first():
            cps = []
            for i in range(n):
                cps.append(copy(i, 0, me, sibling, from_shard=True))
                cps += [copy(i, 1 + j, me, (*chip, c), from_shard=True) for j, chip in enumerate(chips)]
            return cps

        def arrived():
            return [copy(i, 1 + j, (*chip, c), me) for j, chip in enumerate(chips) for i in range(n)]

        def passed():
            return [copy(i, 4 + j, (*chip, c), sibling) for j, chip in enumerate(chips) for i in range(n)]

        def from_sibling():
            cps = [copy(i, 0, sibling, me) for i in range(n)]
            return cps + [copy(i, 4 + j, (*chip, 1 - c), me) for i in range(n) for j, chip in enumerate(chips)]

        return local, first, arrived, passed, from_sibling

    def start(rin, out, sems):
        local, first, _, _, _ = plan(rin, out, sems)
        for cp in local() + first():
            cp.start()

    def mid(rin, out, sems):
        _, _, arrived, passed, _ = plan(rin, out, sems)
        for got, fwd in zip(arrived(), passed()):
            got.wait_recv()
            fwd.start()

    def finish(rin, out, sems):
        local, first, _, passed, from_sibling = plan(rin, out, sems)
        for cp in from_sibling():
            cp.wait_recv()
        for cp in first() + passed():
            cp.wait_send()
        for cp in local():
            cp.wait()

    out_shape = [jax.ShapeDtypeStruct(_full_shape(k, m, s.shape), s.dtype) for s, (k, m) in zip(shards, kinds)]
    scratch = [pltpu.SemaphoreType.DMA((n, 7)), pltpu.SemaphoreType.DMA((n, 7)), pltpu.SemaphoreType.DMA((n,))]
    return _Ride(shards, out_shape, scratch, start, finish, mid)


def _scatter_ride(grads, kinds):
    n = len(grads)

    def plan(g, out, sems):
        send_sems, recv_sems, local_sems = sems
        x, y, c = _my_coords()
        me = _dev_index(x, y, c)

        def local():
            return [pltpu.make_async_copy(_window(g[i], kinds[i][0], kinds[i][1], me), out[i].at[me], local_sems.at[i])
                    for i in range(n)]

        def remote(arrival):
            cps = []
            for rel in range(1, NDEV):
                peer = _peer(x, y, c, rel)
                dev = _dev_index(*peer)
                for i in range(n):
                    kind, m = kinds[i]
                    cps.append(pltpu.make_async_remote_copy(
                        src_ref=_window(g[i], kind, m, me if arrival else dev), dst_ref=out[i].at[dev if arrival else me],
                        send_sem=send_sems.at[i, rel - 1], recv_sem=recv_sems.at[i, rel - 1], device_id=peer, device_id_type=MESH))
            return cps

        return local, remote

    def start(g, out, sems):
        local, remote = plan(g, out, sems)
        for cp in local() + remote(False):
            cp.start()

    def finish(g, out, sems):
        local, remote = plan(g, out, sems)
        for cp in remote(True):
            cp.wait_recv()
        for cp in remote(False):
            cp.wait_send()
        for cp in local():
            cp.wait()

    def block_shape(gr, kind, m):
        return (m, gr.shape[1]) if kind == 'row' else (gr.shape[0], m)

    out_shape = [jax.ShapeDtypeStruct((NDEV,) + block_shape(gr, k, m), gr.dtype) for gr, (k, m) in zip(grads, kinds)]
    scratch = [pltpu.SemaphoreType.DMA((n, NDEV - 1)), pltpu.SemaphoreType.DMA((n, NDEV - 1)), pltpu.SemaphoreType.DMA((n,))]
    return _Ride(grads, out_shape, scratch, start, finish)


def _rms_stats(xf):
    r = lax.rsqrt(jnp.mean(xf * xf, axis=-1, keepdims=True) + EPS)
    return xf * r, r


def _rms_bwd(dh, g, xhat, r):
    dxhat = dh * g
    return r * (dxhat - xhat * jnp.mean(dxhat * xhat, axis=-1, keepdims=True))


def _ffn_blocks():
    edges = (0,) + FF_BLOCK_EDGES + (FF,)
    return [(slice(lo, hi), slice(FF + lo, FF + hi)) for lo, hi in zip(edges[:-1], edges[1:])]


def _swiglu_tile(x_ref, g_ref, wut_ref, wd_ref, ab_ref):
    xf = x_ref[...]
    xhat, _ = _rms_stats(xf)
    h = (xhat * g_ref[...]).astype(BF16)
    acc = jnp.zeros(xf.shape, F32)
    for ra, rb in _ffn_blocks():
        a = _dot_nt(h, wut_ref[ra, :])
        b = _dot_nt(h, wut_ref[rb, :])
        ab_ref[:, ra] = a.astype(BF16)
        ab_ref[:, rb] = b.astype(BF16)
        act = (a * _sigmoid(a) * b).astype(BF16)
        acc = acc + _dot(act, wd_ref[ra, :])
    return xf + 0.5 * acc


def _ffn_fwd(x, g, wut, wd, tm, name, ride=None):
    T = x.shape[0]

    def body(x_ref, g_ref, wut_ref, wd_ref, xo_ref, ab_ref):
        xo_ref[...] = _swiglu_tile(x_ref, g_ref, wut_ref, wd_ref, ab_ref)

    return _pallas(
        body, name, (T // tm,),
        [pl.BlockSpec((tm, D), lambda t: (t, 0)), _const((1, D)), _const((2 * FF, D)), _const((FF, D))],
        [pl.BlockSpec((tm, D), lambda t: (t, 0)), pl.BlockSpec((tm, 2 * FF), lambda t: (t, 0))],
        [jax.ShapeDtypeStruct((T, D), F32), jax.ShapeDtypeStruct((T, 2 * FF), BF16)],
        (x, g, wut, wd), sem=("arbitrary",), ride=ride)


def _ffn_fwd_loss(x, g, wut, wd, g_final, target, tm, name):
    T = x.shape[0]

    def body(x_ref, g_ref, wut_ref, wd_ref, gf_ref, t_ref, dx_ref, ab_ref, loss_ref, dgf_ref):
        xhat, r = _rms_stats(_swiglu_tile(x_ref, g_ref, wut_ref, wd_ref, ab_ref))
        gain = gf_ref[...]
        diff = xhat * gain - t_ref[...]
        dout = diff * (1.0 / D)

        @pl.when(pl.program_id(0) == 0)
        def _():
            loss_ref[...] = jnp.zeros_like(loss_ref)
            dgf_ref[...] = jnp.zeros_like(dgf_ref)
        sq = jnp.sum(jnp.sum(diff * diff, axis=0, keepdims=True), axis=1, keepdims=True)
        loss_ref[...] += jnp.broadcast_to(sq * (0.5 / D), (1, 128))
        dgf_ref[...] += jnp.sum(dout * xhat, axis=0, keepdims=True)
        dx_ref[...] = _rms_bwd(dout, gain, xhat, r)

    row = pl.BlockSpec((tm, D), lambda t: (t, 0))
    return pl.pallas_call(
        body, grid=(T // tm,), name=name,
        in_specs=[row, _const((1, D)), _const((2 * FF, D)), _const((FF, D)), _const((1, D)), row],
        out_specs=[row, pl.BlockSpec((tm, 2 * FF), lambda t: (t, 0)), pl.BlockSpec((1, 128), lambda t: (0, 0)),
                   pl.BlockSpec((1, D), lambda t: (0, 0))],
        out_shape=[jax.ShapeDtypeStruct((T, D), F32), jax.ShapeDtypeStruct((T, 2 * FF), BF16),
                   jax.ShapeDtypeStruct((1, 128), F32), jax.ShapeDtypeStruct((1, D), F32)],
        compiler_params=_params("arbitrary"),
    )(x, g, wut, wd, g_final, target)


def _ffn_bwd(x, dy, ab, g, wut, wd, tm, name):
    T = x.shape[0]

    def body(x_ref, dy_ref, ab_ref, g_ref, wut_ref, wd_ref, dx_ref, dab_ref, act_ref, h_ref, dg_ref):
        xf = x_ref[...]
        xhat, r = _rms_stats(xf)
        gain = g_ref[...]
        h_ref[...] = (xhat * gain).astype(BF16)
        dy = dy_ref[...]
        dyh = (0.5 * dy).astype(BF16)
        dh = jnp.zeros((tm, D), F32)
        for ra, rb in _ffn_blocks():
            a = ab_ref[:, ra].astype(F32)
            b = ab_ref[:, rb].astype(F32)
            dact = _dot_nt(dyh, wd_ref[ra, :])
            sg = _sigmoid(a)
            sl = a * sg
            act_ref[:, ra] = (sl * b).astype(BF16)
            da = (dact * b * (sg * (1.0 + a * (1.0 - sg)))).astype(BF16)
            db = (dact * sl).astype(BF16)
            dab_ref[:, ra] = da
            dab_ref[:, rb] = db
            dh = dh + _dot(da, wut_ref[ra, :]) + _dot(db, wut_ref[rb, :])
        dx_ref[...] = dy + _rms_bwd(dh, gain, xhat, r)

        @pl.when(pl.program_id(0) == 0)
        def _():
            dg_ref[...] = jnp.zeros_like(dg_ref)
        dg_ref[...] += jnp.sum(dh * xhat, axis=0, keepdims=True)

    return pl.pallas_call(
        body, grid=(T // tm,), name=name,
        in_specs=[pl.BlockSpec((tm, D), lambda t: (t, 0)), pl.BlockSpec((tm, D), lambda t: (t, 0)),
                  pl.BlockSpec((tm, 2 * FF), lambda t: (t, 0)), _const((1, D)), _const((2 * FF, D)), _const((FF, D))],
        out_specs=[pl.BlockSpec((tm, D), lambda t: (t, 0)), pl.BlockSpec((tm, 2 * FF), lambda t: (t, 0)),
                   pl.BlockSpec((tm, FF), lambda t: (t, 0)), pl.BlockSpec((tm, D), lambda t: (t, 0)),
                   pl.BlockSpec((1, D), lambda t: (0, 0))],
        out_shape=[jax.ShapeDtypeStruct((T, D), F32), jax.ShapeDtypeStruct((T, 2 * FF), BF16),
                   jax.ShapeDtypeStruct((T, FF), BF16), jax.ShapeDtypeStruct((T, D), BF16), jax.ShapeDtypeStruct((1, D), F32)],
        compiler_params=_params("arbitrary"),
    )(x, dy, ab, g, wut, wd)


def _tn_matmul(xm, ym, tn, name, scale=None, out_cols=None, col_off=0, prev=None, tt=TILE_GRAD_TOKENS, x_part=(0, 1),
               out_rows=None, y_part=(0, 1), ride=None):
    T = xm.shape[0]
    xi, xn = x_part
    yi, yn = y_part
    K = xm.shape[1] // xn
    N = ym.shape[1] // yn
    out_cols = N if out_cols is None else out_cols
    row_blk = xi if out_rows is not None else 0
    out_rows = K if out_rows is None else out_rows
    tt = min(tt, T)
    nt = T // tt
    off = col_off // tn

    def body(*refs):
        x_ref, y_ref = refs[0], refs[1]
        o_ref, acc = refs[-2], refs[-1]

        @pl.when(pl.program_id(1) == 0)
        def _():
            acc[...] = jnp.zeros_like(acc)
        acc[...] += _dot_tn(x_ref[...].astype(BF16), y_ref[...].astype(BF16))

        @pl.when(pl.program_id(1) == nt - 1)
        def _():
            res = acc[...]
            o_ref[...] = (res if scale is None else res * scale).astype(BF16)

    ycol = yi * (N // tn)
    in_specs = [pl.BlockSpec((tt, K), lambda n, t: (t, xi)), pl.BlockSpec((tt, tn), lambda n, t: (t, n + ycol))]
    args = [xm, ym]
    aliases = {}
    if prev is not None:
        in_specs.append(ANY)
        args.append(prev)
        aliases = {2: 0}
    outs, rode = _pallas(
        body, name, (N // tn, nt), in_specs, [pl.BlockSpec((K, tn), lambda n, t: (row_blk, n + off))],
        [jax.ShapeDtypeStruct((out_rows, out_cols), BF16)], args, scratch_shapes=[pltpu.VMEM((K, tn), F32)],
        sem=("parallel", "arbitrary"), aliases=aliases, ride=ride)
    return outs[0], rode


def _mix_fwd(x, g, w_in, tm, ride=None):
    T = x.shape[0]

    def body(x_ref, g_ref, w_ref, uc_ref, qkv_ref, mq_ref, gl_ref, h_ref):
        xhat, _ = _rms_stats(x_ref[...])
        h = (xhat * g_ref[...]).astype(BF16)
        h_ref[...] = h
        uc_ref[...] = _dot(h, w_ref[:, 0:1024])
        qkv_ref[...] = _dot(h, w_ref[:, 1024:2560]).astype(BF16)
        mq_ref[...] = _dot(h, w_ref[:, 2560:3072]).astype(BF16)
        for j in range(3):
            gl_ref[:, j * D:(j + 1) * D] = _dot(h, w_ref[:, 3072 + j * D:3072 + (j + 1) * D]).astype(BF16)

    row = lambda w: pl.BlockSpec((tm, w), lambda t: (t, 0))
    return _pallas(
        body, "mix_fwd", (T // tm,), [row(D), _const((1, D)), _const((D, IN_COLS))],
        [row(1024), row(1536), row(512), row(3072), row(D)],
        [jax.ShapeDtypeStruct((T, 1024), F32), jax.ShapeDtypeStruct((T, 1536), BF16), jax.ShapeDtypeStruct((T, 512), BF16),
         jax.ShapeDtypeStruct((T, 3072), BF16), jax.ShapeDtypeStruct((T, D), BF16)],
        (x, g, w_in), sem=("parallel",), ride=ride)


def _mix_bwd(x, dres, duc, dqkv, dmq, dgl, g, w_in, tm, ride=None):
    T = x.shape[0]

    def body(x_ref, dres_ref, duc_ref, dqkv_ref, dmq_ref, dgl_ref, g_ref, w_ref, dx_ref, dg_ref):
        xhat, r = _rms_stats(x_ref[...])
        dh = _dot_nt(duc_ref[...], w_ref[:, 0:1024])
        dh = dh + _dot_nt(dqkv_ref[...], w_ref[:, 1024:2560])
        dh = dh + _dot_nt(dmq_ref[...], w_ref[:, 2560:3072])
        dh = dh + _dot_nt(dgl_ref[...], w_ref[:, 3072:6144])
        dx_ref[...] = dres_ref[...] + _rms_bwd(dh, g_ref[...], xhat, r)

        @pl.when(pl.program_id(0) == 0)
        def _():
            dg_ref[...] = jnp.zeros_like(dg_ref)
        dg_ref[...] += jnp.sum(dh * xhat, axis=0, keepdims=True)

    row = lambda w: pl.BlockSpec((tm, w), lambda t: (t, 0))
    return _pallas(
        body, "mix_bwd", (T // tm,),
        [row(D), row(D), row(1024), row(1536), row(512), row(3072), _const((1, D)), _const((D, IN_COLS))],
        [row(D), pl.BlockSpec((1, D), lambda t: (0, 0))],
        [jax.ShapeDtypeStruct((T, D), F32), jax.ShapeDtypeStruct((1, D), F32)],
        (x, dres, duc, dqkv, dmq, dgl, g, w_in), sem=("arbitrary",), ride=ride)


def _shifted(win, base, copies):
    for k in range(8):
        copies[k] = win[base + k:base + k + CONV_CHUNK + 24]
    return copies


def _tap_slices(copies, tap):
    out = []
    for k in range(8):
        for a in range(4):
            j = tap(a, k)
            if 0 <= j < CONV_K:
                out.append((j, copies[k, pl.ds(8 * a, CONV_CHUNK), :]))
    return out


def _conv_taps(copies, w_ref, tap):
    acc = jnp.zeros((CONV_CHUNK, CONV_W), F32)
    for j, rows in _tap_slices(copies, tap):
        acc = acc + rows * w_ref[j:j + 1, :]
    return acc


def _fold8(v):
    acc = v[0:8]
    for r in range(8, CONV_CHUNK, 8):
        acc = acc + v[r:r + 8]
    return acc


def _glu_into(uc_ref, vpad, S):
    vpad[pl.ds(0, CONV_HALO), :] = jnp.zeros((CONV_HALO, CONV_W), F32)
    vpad[pl.ds(S + CONV_HALO, CONV_HALO), :] = jnp.zeros((CONV_HALO, CONV_W), F32)

    def glu(i, carry):
        r0 = pl.multiple_of(i * GLU_CHUNK, GLU_CHUNK)
        a = uc_ref[0, pl.ds(r0, GLU_CHUNK), 0:CONV_W]
        gt = uc_ref[0, pl.ds(r0, GLU_CHUNK), CONV_W:2 * CONV_W]
        vpad[pl.ds(pl.multiple_of(r0 + CONV_HALO, CONV_HALO), GLU_CHUNK), :] = a * _sigmoid(gt)
        return carry
    lax.fori_loop(0, S // GLU_CHUNK, glu, 0)


def _layer_norm(z, vec_ref):
    xc = z - jnp.mean(z, axis=-1, keepdims=True)
    rstd = lax.rsqrt(jnp.mean(xc * xc, axis=-1, keepdims=True) + EPS)
    xn = xc * rstd
    return xn, rstd, xn * vec_ref[1:2, :] + vec_ref[2:3, :]


def _conv_fwd(uc, dw_w, vec):
    NB, S, _ = uc.shape

    def body(uc_ref, w_ref, vec_ref, o_ref, z_ref, vpad, copies):
        _glu_into(uc_ref, vpad, S)

        def conv(i, carry):
            r0 = pl.multiple_of(i * CONV_CHUNK, CONV_CHUNK)
            win = vpad[pl.ds(r0, CONV_WIN), :]
            z = _conv_taps(_shifted(win, CONV_HALO - (CONV_K - 1), copies), w_ref, lambda a, k: 8 * a + k) + vec_ref[0:1, :]
            z_ref[0, pl.ds(r0, CONV_CHUNK), :] = z
            _, _, yln = _layer_norm(z, vec_ref)
            o_ref[0, pl.ds(r0, CONV_CHUNK), :] = (yln * _sigmoid(yln)).astype(BF16)
            return carry
        lax.fori_loop(0, S // CONV_CHUNK, conv, 0, unroll=2)

    seq = pl.BlockSpec((1, S, CONV_W), lambda b: (b, 0, 0))
    return pl.pallas_call(
        body, grid=(NB,), name="conv_fwd",
        in_specs=[pl.BlockSpec((1, S, 2 * CONV_W), lambda b: (b, 0, 0)), _const((CONV_K, CONV_W)), _const((8, CONV_W))],
        out_specs=[seq, seq],
        out_shape=[jax.ShapeDtypeStruct((NB, S, CONV_W), BF16), jax.ShapeDtypeStruct((NB, S, CONV_W), F32)],
        scratch_shapes=[pltpu.VMEM((S + 2 * CONV_HALO, CONV_W), F32), pltpu.VMEM((8, CONV_CHUNK + 24, CONV_W), F32)],
        compiler_params=_params("parallel"),
    )(uc, dw_w, vec)


def _conv_bwd(uc, z, dcact, dw_w, vec, ride=None):
    NB, S, _ = uc.shape
    n_chunks = S // CONV_CHUNK

    def body(uc_ref, z_ref, dc_ref, w_ref, vec_ref, duc_ref, dw_ref, dvec_ref, vpad, dzpad, dw8, dvec8, copies):
        @pl.when(pl.program_id(0) == 0)
        def _():
            dw8[...] = jnp.zeros_like(dw8)
            dvec8[...] = jnp.zeros_like(dvec8)
        _glu_into(uc_ref, vpad, S)
        dzpad[pl.ds(S, 2 * CONV_HALO), :] = jnp.zeros((2 * CONV_HALO, CONV_W), F32)

        def norm_bwd(i, carry):
            r0 = pl.multiple_of(i * CONV_CHUNK, CONV_CHUNK)
            xn, rstd, yln = _layer_norm(z_ref[0, pl.ds(r0, CONV_CHUNK), :], vec_ref)
            sg = _sigmoid(yln)
            dyln = dc_ref[0, pl.ds(r0, CONV_CHUNK), :] * (sg * (1.0 + yln * (1.0 - sg)))
            dxn = dyln * vec_ref[1:2, :]
            dz = rstd * (dxn - jnp.mean(dxn, axis=-1, keepdims=True) - xn * jnp.mean(dxn * xn, axis=-1, keepdims=True))
            dzpad[pl.ds(r0, CONV_CHUNK), :] = dz
            dvec8[0] += _fold8(dz)
            dvec8[1] += _fold8(dyln * xn)
            dvec8[2] += _fold8(dyln)
            return carry
        lax.fori_loop(0, n_chunks, norm_bwd, 0, unroll=2)

        def taps_bwd(i, carry):
            r0 = pl.multiple_of(i * CONV_CHUNK, CONV_CHUNK)
            dzwin = dzpad[pl.ds(r0, CONV_WIN), :]
            dv = _conv_taps(_shifted(dzwin, 0, copies), w_ref, lambda a, k: CONV_K - 1 - 8 * a - k)
            dz = dzwin[0:CONV_CHUNK]
            vwin = vpad[pl.ds(r0, CONV_WIN), :]
            for j, rows in _tap_slices(_shifted(vwin, CONV_HALO - (CONV_K - 1), copies), lambda a, k: 8 * a + k):
                dw8[j] += _fold8(dz * rows)
            a = uc_ref[0, pl.ds(r0, CONV_CHUNK), 0:CONV_W]
            sg = _sigmoid(uc_ref[0, pl.ds(r0, CONV_CHUNK), CONV_W:2 * CONV_W])
            duc_ref[0, pl.ds(r0, CONV_CHUNK), 0:CONV_W] = (dv * sg).astype(BF16)
            duc_ref[0, pl.ds(r0, CONV_CHUNK), CONV_W:2 * CONV_W] = (dv * a * sg * (1.0 - sg)).astype(BF16)
            return carry
        lax.fori_loop(0, n_chunks, taps_bwd, 0, unroll=2)

        @pl.when(pl.program_id(0) == NB - 1)
        def _():
            dw_ref[...] = jnp.zeros_like(dw_ref)
            dvec_ref[...] = jnp.zeros_like(dvec_ref)
            for j in range(CONV_K):
                dw_ref[j:j + 1, :] = jnp.sum(dw8[j], axis=0, keepdims=True)
            for j in range(3):
                dvec_ref[j:j + 1, :] = jnp.sum(dvec8[j], axis=0, keepdims=True)

    return _pallas(
        body, "conv_bwd", (NB,),
        [pl.BlockSpec((1, S, 2 * CONV_W), lambda b: (b, 0, 0)), pl.BlockSpec((1, S, CONV_W), lambda b: (b, 0, 0)),
         pl.BlockSpec((1, S, CONV_W), lambda b: (b, 0, 0)), _const((CONV_K, CONV_W)), _const((8, CONV_W))],
        [pl.BlockSpec((1, S, 2 * CONV_W), lambda b: (b, 0, 0)), pl.BlockSpec((32, CONV_W), lambda b: (0, 0)),
         pl.BlockSpec((8, CONV_W), lambda b: (0, 0))],
        [jax.ShapeDtypeStruct((NB, S, 2 * CONV_W), BF16), jax.ShapeDtypeStruct((32, CONV_W), F32),
         jax.ShapeDtypeStruct((8, CONV_W), F32)],
        (uc, z, dcact, dw_w, vec),
        scratch_shapes=[pltpu.VMEM((S + 2 * CONV_HALO, CONV_W), F32), pltpu.VMEM((S + 2 * CONV_HALO, CONV_W), F32),
                        pltpu.VMEM((CONV_K, 8, CONV_W), F32), pltpu.VMEM((3, 8, CONV_W), F32),
                        pltpu.VMEM((8, CONV_CHUNK + 24, CONV_W), F32)],
        sem=("arbitrary",), ride=ride)


def _rel_index_of_column(cols):
    offset = jnp.where(cols < KWIN, cols, cols - DS_LANES)
    return jnp.clip(KPAD - offset, -(CHUNK - 1), MAX_REL) + (CHUNK - 1)


def _bias_table(rel_bias):
    def body(rb_ref, o_ref, by_offset, first8):
        ridx = _rel_index_of_column(lax.broadcasted_iota(jnp.int32, (1, DS_LANES), 1))
        onehot = (ridx == lax.broadcasted_iota(jnp.int32, (N_REL, 1), 0)).astype(F32)
        by_offset[...] = jnp.dot(rb_ref[...], onehot, preferred_element_type=F32, precision=lax.Precision.HIGHEST)
        sub = lax.broadcasted_iota(jnp.int32, (8, 1), 0)
        kchunk = lax.broadcasted_iota(jnp.int32, (1, KWIN), 1) // CHUNK
        for head in range(ATT_HEADS):
            base = jnp.broadcast_to(by_offset[head:head + 1, :], (8, DS_LANES))
            rows = base
            for s in range(1, 8):
                rows = jnp.where(sub == s, pltpu.roll(base, s, 1), rows)
            first8[head] = rows

        def rows8(q8, carry):
            qchunk = (q8 * 8 + sub) // CHUNK
            band = (kchunk >= qchunk) & (kchunk <= qchunk + LEFT_CHUNKS)
            for head in range(ATT_HEADS):
                tile = pltpu.roll(first8[head], q8 * 8, 1)[:, 0:KWIN]
                o_ref[head, pl.ds(pl.multiple_of(q8 * 8, 8), 8), :] = jnp.where(band, tile, MASK_VALUE)
            return carry
        lax.fori_loop(0, QB // 8, rows8, 0)

    return pl.pallas_call(body, out_shape=jax.ShapeDtypeStruct((ATT_HEADS, QB, KWIN), F32), name="bias_table",
                          scratch_shapes=[pltpu.VMEM((ATT_HEADS, DS_LANES), F32), pltpu.VMEM((ATT_HEADS, 8, DS_LANES), F32)],
                          compiler_params=_params())(rel_bias)


def _load_keys(i, k_ref, v_ref, kpad, vpad, S):
    @pl.when(i == 0)
    def _():
        kpad[pl.ds(0, KPAD), :] = jnp.zeros((KPAD, ATT_W), BF16)
        vpad[pl.ds(0, KPAD), :] = jnp.zeros((KPAD, ATT_W), BF16)
        kpad[pl.ds(KPAD, S), :] = k_ref[0]
        vpad[pl.ds(KPAD, S), :] = v_ref[0]


def _att_probs(q2, k2, tab_ref, head, in_head, in_seq):
    qm = jnp.where(in_head, q2, jnp.zeros_like(q2))
    s = _dot_nt(qm, k2) * (ATT_HD ** -0.5) + tab_ref[head]
    s = jnp.where(in_seq, s, MASK_VALUE)
    e = jnp.exp(s - jnp.max(s, axis=-1, keepdims=True))
    return e * (1.0 / jnp.sum(e, axis=-1, keepdims=True))


def _att_fwd(qkv, tab, ride=None):
    NB, S, _ = qkv.shape

    def body(q_ref, k_ref, v_ref, tab_ref, o_ref, kpad, vpad):
        i = pl.program_id(1)
        _load_keys(i, k_ref, v_ref, kpad, vpad, S)
        koff = pl.multiple_of(i * QB, QB)
        lane = lax.broadcasted_iota(jnp.int32, (1, 128), 1)
        in_seq = (lax.broadcasted_iota(jnp.int32, (1, KWIN), 1) + i * QB) >= KPAD
        for pair in range(ATT_HEADS // 2):
            cols = slice(pair * 128, (pair + 1) * 128)
            q2 = q_ref[0, :, cols]
            k2 = kpad[pl.ds(koff, KWIN), cols]
            v2 = vpad[pl.ds(koff, KWIN), cols]
            o2 = jnp.zeros((QB, 128), F32)
            for hh in range(2):
                in_head = (lane // ATT_HD) == hh
                p = _att_probs(q2, k2, tab_ref, 2 * pair + hh, in_head, in_seq)
                o2 = jnp.where(in_head, _dot(p.astype(BF16), v2), o2)
            o_ref[0, :, cols] = o2.astype(BF16)

    seq = lambda col: pl.BlockSpec((1, S, ATT_W), lambda b, i: (b, 0, col), pipeline_mode=pl.Buffered(1))
    outs, rode = _pallas(
        body, "att_fwd", (NB, S // QB),
        [pl.BlockSpec((1, QB, ATT_W), lambda b, i: (b, i, 0)), seq(1), seq(2), _const((ATT_HEADS, QB, KWIN))],
        [pl.BlockSpec((1, QB, ATT_W), lambda b, i: (b, i, 0))], [jax.ShapeDtypeStruct((NB, S, ATT_W), BF16)],
        (qkv, qkv, qkv, tab),
        scratch_shapes=[pltpu.VMEM((S + KPAD, ATT_W), BF16), pltpu.VMEM((S + KPAD, ATT_W), BF16)],
        sem=("arbitrary", "arbitrary"), ride=ride)
    return outs[0], rode


def _att_bwd(qkv, do, tab, ride=None):
    NB, S, _ = qkv.shape
    nq = S // QB

    def body(q_ref, k_ref, v_ref, do_ref, tab_ref, dqkv_ref, ds_hbm, kpad, vpad, dkpad, dvpad, ds_acc, ds_sem):
        b, i = pl.program_id(0), pl.program_id(1)
        _load_keys(i, k_ref, v_ref, kpad, vpad, S)

        @pl.when(i == 0)
        def _():
            dkpad[...] = jnp.zeros_like(dkpad)
            dvpad[...] = jnp.zeros_like(dvpad)

        @pl.when((i == 0) & (b == 0))
        def _():
            ds_acc[...] = jnp.zeros_like(ds_acc)

        koff = pl.multiple_of(i * QB, QB)
        lane = lax.broadcasted_iota(jnp.int32, (1, 128), 1)
        in_seq = (lax.broadcasted_iota(jnp.int32, (1, KWIN), 1) + i * QB) >= KPAD
        for pair in range(ATT_HEADS // 2):
            cols = slice(pair * 128, (pair + 1) * 128)
            q2 = q_ref[0, :, cols]
            do2 = do_ref[0, :, cols]
            k2 = kpad[pl.ds(koff, KWIN), cols]
            v2 = vpad[pl.ds(koff, KWIN), cols]
            dq2 = jnp.zeros((QB, 128), F32)
            dk2 = jnp.zeros((KWIN, 128), F32)
            dv2 = jnp.zeros((KWIN, 128), F32)
            for hh in range(2):
                head = 2 * pair + hh
                in_head = (lane // ATT_HD) == hh
                p = _att_probs(q2, k2, tab_ref, head, in_head, in_seq)
                dom = jnp.where(in_head, do2, jnp.zeros_like(do2))
                dp = _dot_nt(dom, v2)
                ds = p * (dp - jnp.sum(p * dp, axis=-1, keepdims=True))
                ds_acc[head] += ds
                dss = (ds * (ATT_HD ** -0.5)).astype(BF16)
                dq2 = jnp.where(in_head, _dot(dss, k2), dq2)
                dk2 = jnp.where(in_head, _dot_tn(dss, q2), dk2)
                dv2 = jnp.where(in_head, _dot_tn(p.astype(BF16), do2), dv2)
            dqkv_ref[0, pl.ds(koff, QB), cols] = dq2.astype(BF16)
            dkpad[pl.ds(koff, KWIN), cols] += dk2
            dvpad[pl.ds(koff, KWIN), cols] += dv2

        @pl.when(i == nq - 1)
        def _():
            dqkv_ref[0, :, ATT_W:2 * ATT_W] = dkpad[pl.ds(KPAD, S), :].astype(BF16)
            dqkv_ref[0, :, 2 * ATT_W:3 * ATT_W] = dvpad[pl.ds(KPAD, S), :].astype(BF16)

        @pl.when((i == nq - 1) & (b == NB - 1))
        def _():
            out = pltpu.make_async_copy(ds_acc, ds_hbm, ds_sem)
            out.start()
            out.wait()

    seq = lambda col: pl.BlockSpec((1, S, ATT_W), lambda b, i: (b, 0, col), pipeline_mode=pl.Buffered(1))
    return _pallas(
        body, "att_bwd", (NB, nq),
        [pl.BlockSpec((1, QB, ATT_W), lambda b, i: (b, i, 0)), seq(1), seq(2),
         pl.BlockSpec((1, QB, ATT_W), lambda b, i: (b, i, 0)), _const((ATT_HEADS, QB, KWIN))],
        [pl.BlockSpec((1, S, 3 * ATT_W), lambda b, i: (b, 0, 0)), ANY],
        [jax.ShapeDtypeStruct((NB, S, 3 * ATT_W), BF16), jax.ShapeDtypeStruct((ATT_HEADS, QB, KWIN), F32)],
        (qkv, qkv, qkv, do, tab),
        scratch_shapes=[pltpu.VMEM((S + KPAD, ATT_W), BF16), pltpu.VMEM((S + KPAD, ATT_W), BF16),
                        pltpu.VMEM((S + KPAD, ATT_W), F32), pltpu.VMEM((S + KPAD, ATT_W), F32),
                        pltpu.VMEM((ATT_HEADS, QB, KWIN), F32), pltpu.SemaphoreType.DMA],
        sem=("arbitrary", "arbitrary"), ride=ride)


def _rel_bias_grad(ds):
    def body(ds_ref, o_ref):
        sub = lax.broadcasted_iota(jnp.int32, (8, 1), 0)
        ridx = _rel_index_of_column(lax.broadcasted_iota(jnp.int32, (DS_LANES, 1), 0))
        onehot = (ridx == lax.broadcasted_iota(jnp.int32, (1, N_REL), 1)).astype(F32)
        def rows8(q8, accs):
            shift = lax.rem(DS_LANES - q8 * 8, DS_LANES)
            out = []
            for head in range(ATT_HEADS):
                tile = ds_ref[head, pl.ds(pl.multiple_of(q8 * 8, 8), 8), :]
                tile = jnp.concatenate([tile, jnp.zeros((8, DS_LANES - KWIN), F32)], axis=1)
                out.append(accs[head] + pltpu.roll(tile, shift, 1))
            return tuple(out)
        accs = lax.fori_loop(0, QB // 8, rows8, tuple(jnp.zeros((8, DS_LANES), F32) for _ in range(ATT_HEADS)))
        for head in range(ATT_HEADS):
            acc = accs[head]
            diag = jnp.zeros((8, DS_LANES), F32)
            for s in range(8):
                shifted = acc if s == 0 else pltpu.roll(acc, DS_LANES - s, 1)
                diag = jnp.where(sub == s, shifted, diag)
            z = jnp.sum(diag, axis=0, keepdims=True)
            o_ref[head:head + 1, :] = jnp.dot(z, onehot, preferred_element_type=F32, precision=lax.Precision.HIGHEST)

    return pl.pallas_call(body, out_shape=jax.ShapeDtypeStruct((ATT_HEADS, N_REL), F32), name="rel_bias_grad",
                          compiler_params=_params())(ds)


def _memkv_fwd(mem, g, w_kv, tm):
    R = mem.shape[0]
    tm = min(tm, R)

    def body(m_ref, g_ref, w_ref, h_ref, kv_ref):
        xhat, _ = _rms_stats(m_ref[...])
        h = (xhat * g_ref[...]).astype(BF16)
        h_ref[...] = h
        kv_ref[...] = _dot(h, w_ref[...]).astype(BF16)

    row = pl.BlockSpec((tm, D), lambda t: (t, 0))
    return pl.pallas_call(
        body, grid=(R // tm,), name="memkv_fwd", in_specs=[row, _const((1, D)), _const((D, 2 * MEM_W))], out_specs=[row, row],
        out_shape=[jax.ShapeDtypeStruct((R, D), BF16), jax.ShapeDtypeStruct((R, 2 * MEM_W), BF16)],
        compiler_params=_params("parallel"),
    )(mem, g, w_kv)


def _memkv_bwd(mem, dkv, w_kv, tm):
    R = mem.shape[0]
    tm = min(tm, R)

    def body(m_ref, dkv_ref, w_ref, dg_ref):
        xhat, _ = _rms_stats(m_ref[...])
        dh = _dot_nt(dkv_ref[...].astype(BF16), w_ref[...])

        @pl.when(pl.program_id(0) == 0)
        def _():
            dg_ref[...] = jnp.zeros_like(dg_ref)
        dg_ref[...] += jnp.sum(dh * xhat, axis=0, keepdims=True)

    row = pl.BlockSpec((tm, D), lambda t: (t, 0))
    return pl.pallas_call(
        body, grid=(R // tm,), name="memkv_bwd", in_specs=[row, row, _const((D, 2 * MEM_W))],
        out_specs=pl.BlockSpec((1, D), lambda t: (0, 0)), out_shape=jax.ShapeDtypeStruct((1, D), F32),
        compiler_params=_params("arbitrary"),
    )(mem, dkv, w_kv)


def _mem_probs(qh, kh):
    s = _dot_nt(qh, kh) * (MEM_HD ** -0.5)
    e = jnp.exp(s - jnp.max(s, axis=-1, keepdims=True))
    return e * (1.0 / jnp.sum(e, axis=-1, keepdims=True))


def _mematt_fwd(mq, kv, tq):
    NB, S, _ = mq.shape
    M = kv.shape[1]

    def body(q_ref, kv_ref, o_ref):
        for h in range(MEM_HEADS):
            cols = slice(h * MEM_HD, (h + 1) * MEM_HD)
            p = _mem_probs(q_ref[0, :, cols], kv_ref[0, :, cols])
            o_ref[0, :, cols] = _dot(p.astype(BF16), kv_ref[0, :, MEM_W + h * MEM_HD:MEM_W + (h + 1) * MEM_HD]).astype(BF16)

    return pl.pallas_call(
        body, grid=(NB, S // tq), name="mematt_fwd",
        in_specs=[pl.BlockSpec((1, tq, MEM_W), lambda b, i: (b, i, 0)), pl.BlockSpec((1, M, 2 * MEM_W), lambda b, i: (b, 0, 0))],
        out_specs=pl.BlockSpec((1, tq, MEM_W), lambda b, i: (b, i, 0)),
        out_shape=jax.ShapeDtypeStruct((NB, S, MEM_W), BF16), compiler_params=_params("parallel", "parallel"),
    )(mq, kv)


def _mematt_bwd(mq, kv, do, tq):
    NB, S, _ = mq.shape
    M = kv.shape[1]

    def body(q_ref, kv_ref, do_ref, dq_ref, dkv_ref):
        @pl.when(pl.program_id(1) == 0)
        def _():
            dkv_ref[...] = jnp.zeros_like(dkv_ref)
        for h in range(MEM_HEADS):
            cols = slice(h * MEM_HD, (h + 1) * MEM_HD)
            vcols = slice(MEM_W + h * MEM_HD, MEM_W + (h + 1) * MEM_HD)
            qh, kh, vh, doh = q_ref[0, :, cols], kv_ref[0, :, cols], kv_ref[0, :, vcols], do_ref[0, :, cols]
            p = _mem_probs(qh, kh)
            dp = _dot_nt(doh, vh)
            ds = p * (dp - jnp.sum(p * dp, axis=-1, keepdims=True))
            dss = (ds * (MEM_HD ** -0.5)).astype(BF16)
            dq_ref[0, :, cols] = _dot(dss, kh).astype(BF16)
            dkv_ref[0, :, cols] += _dot_tn(dss, qh)
            dkv_ref[0, :, vcols] += _dot_tn(p.astype(BF16), doh)

    qspec = pl.BlockSpec((1, tq, MEM_W), lambda b, i: (b, i, 0))
    kvspec = pl.BlockSpec((1, M, 2 * MEM_W), lambda b, i: (b, 0, 0))
    return pl.pallas_call(
        body, grid=(NB, S // tq), name="mematt_bwd", in_specs=[qspec, kvspec, qspec], out_specs=[qspec, kvspec],
        out_shape=[jax.ShapeDtypeStruct((NB, S, MEM_W), BF16), jax.ShapeDtypeStruct((NB, M, 2 * MEM_W), F32)],
        compiler_params=_params("arbitrary", "arbitrary"),
    )(mq, kv, do)


def _branch(j, in_ref, w_ref, gl_ref, bg_ref):
    y = _dot(in_ref[...], w_ref[...])
    gate = _sigmoid(gl_ref[:, j * D:(j + 1) * D].astype(F32) + bg_ref[:, j * D:(j + 1) * D])
    return y, gate


def _combine_fwd(x, cact, oatt, omem, gl, bg, wpw, wo, wmo, wout, tm):
    T = x.shape[0]

    def body(x_ref, c_ref, a_ref, m_ref, gl_ref, bg_ref, wpw_ref, wo_ref, wmo_ref, wout_ref, xo_ref, y_ref):
        y = None
        for j, (in_ref, w_ref) in enumerate(((c_ref, wpw_ref), (a_ref, wo_ref), (m_ref, wmo_ref))):
            yj, gate = _branch(j, in_ref, w_ref, gl_ref, bg_ref)
            y = gate * yj if y is None else y + gate * yj
        y = y.astype(BF16)
        y_ref[...] = y
        xo_ref[...] = x_ref[...] + _dot(y, wout_ref[...])

    row = lambda w: pl.BlockSpec((tm, w), lambda t: (t, 0))
    wbr = _const((512, D))
    return pl.pallas_call(
        body, grid=(T // tm,), name="combine_fwd",
        in_specs=[row(D), row(512), row(512), row(512), row(3 * D), _const((1, 3 * D)), wbr, wbr, wbr, _const((D, D))],
        out_specs=[row(D), row(D)],
        out_shape=[jax.ShapeDtypeStruct((T, D), F32), jax.ShapeDtypeStruct((T, D), BF16)],
        compiler_params=_params("parallel"),
    )(x, cact, oatt, omem, gl, bg, wpw, wo, wmo, wout)


def _combine_bwd(dx, cact, oatt, omem, gl, bg, wpw, wo, wmo, wout, tm, ride=None):
    T = dx.shape[0]

    def body(dx_ref, c_ref, a_ref, m_ref, gl_ref, bg_ref, wpw_ref, wo_ref, wmo_ref, wout_ref,
             dgl_ref, dc_ref, da_ref, dm_ref, dyc_ref, dya_ref, dym_ref, dbg_ref):
        dy = _dot_nt(dx_ref[...].astype(BF16), wout_ref[...])

        @pl.when(pl.program_id(0) == 0)
        def _():
            dbg_ref[...] = jnp.zeros_like(dbg_ref)
        branches = ((c_ref, wpw_ref, dyc_ref, dc_ref), (a_ref, wo_ref, dya_ref, da_ref), (m_ref, wmo_ref, dym_ref, dm_ref))
        for j, (in_ref, w_ref, dyb_ref, din_ref) in enumerate(branches):
            yj, gate = _branch(j, in_ref, w_ref, gl_ref, bg_ref)
            dyg = dy * gate
            dlogit = dyg * yj * (1.0 - gate)
            dgl_ref[:, j * D:(j + 1) * D] = dlogit.astype(BF16)
            dbg_ref[:, j * D:(j + 1) * D] += jnp.sum(dlogit, axis=0, keepdims=True)
            dyb = dyg.astype(BF16)
            dyb_ref[...] = dyb
            din_ref[...] = _dot_nt(dyb, w_ref[...]).astype(din_ref.dtype)

    row = lambda w: pl.BlockSpec((tm, w), lambda t: (t, 0))
    wbr = _const((512, D))
    sds = jax.ShapeDtypeStruct
    return _pallas(
        body, "combine_bwd", (T // tm,),
        [row(D), row(512), row(512), row(512), row(3 * D), _const((1, 3 * D)), wbr, wbr, wbr, _const((D, D))],
        [row(3 * D), row(512), row(512), row(512), row(D), row(D), row(D), pl.BlockSpec((1, 3 * D), lambda t: (0, 0))],
        [sds((T, 3 * D), BF16), sds((T, 512), F32), sds((T, 512), BF16), sds((T, 512), BF16),
         sds((T, D), BF16), sds((T, D), BF16), sds((T, D), BF16), sds((1, 3 * D), F32)],
        (dx, cact, oatt, omem, gl, bg, wpw, wo, wmo, wout), sem=("arbitrary",), ride=ride)


def _peer(x, y, c, rel):
    rx, ry, rc = (rel >> 2) & 1, (rel >> 1) & 1, rel & 1
    return ((1 - x) if rx else x, (1 - y) if ry else y, (1 - c) if rc else c)


def _all_sum_small(parts):
    n = len(parts)

    def body(*refs):
        p_refs, o_refs, slots = refs[:n], refs[n:2 * n], refs[2 * n:3 * n]
        send_sems, recv_sems = refs[3 * n:]
        x, y, c = _my_coords()
        me = _dev_index(x, y, c)

        def copy(i, rel, arrival):
            peer = _peer(x, y, c, rel)
            return pltpu.make_async_remote_copy(
                src_ref=p_refs[i], dst_ref=slots[i].at[_dev_index(*peer) if arrival else me],
                send_sem=send_sems.at[i, rel - 1], recv_sem=recv_sems.at[i, rel - 1], device_id=peer, device_id_type=MESH)

        for i in range(n):
            slots[i][me] = p_refs[i][...]
        for rel in range(1, NDEV):
            for i in range(n):
                copy(i, rel, False).start()
        for rel in range(1, NDEV):
            for i in range(n):
                copy(i, rel, True).wait_recv()
        for rel in range(1, NDEV):
            for i in range(n):
                copy(i, rel, False).wait_send()
        for i in range(n):
            total = slots[i][0]
            for d in range(1, NDEV):
                total = total + slots[i][d]
            o_refs[i][...] = total

    vmem = pl.BlockSpec(memory_space=pltpu.VMEM)
    return pl.pallas_call(
        body, out_shape=[jax.ShapeDtypeStruct(p.shape, F32) for p in parts], name="all_sum_small",
        in_specs=[vmem] * n, out_specs=[vmem] * n,
        scratch_shapes=[pltpu.VMEM((NDEV,) + p.shape, F32) for p in parts]
        + [pltpu.SemaphoreType.DMA((n, NDEV - 1)), pltpu.SemaphoreType.DMA((n, NDEV - 1))],
        compiler_params=pltpu.CompilerParams(has_side_effects=True),
    )(*parts)


HBM = pl.BlockSpec(memory_space=pltpu.HBM)
SEM = pl.BlockSpec(memory_space=pltpu.SEMAPHORE)


def _own_block(g, kind, m):
    def body(g_ref, land_ref, staged, sem):
        me = _dev_index(*_my_coords())
        for cp in (pltpu.make_async_copy(_window(g_ref, kind, m, me), staged, sem),
                   pltpu.make_async_copy(staged, land_ref.at[me], sem)):
            cp.start()
            cp.wait()

    block = (m, g.shape[1]) if kind == 'row' else (g.shape[0], m)
    return pl.pallas_call(body, in_specs=[ANY], out_specs=ANY, out_shape=jax.ShapeDtypeStruct((NDEV,) + block, g.dtype),
                          scratch_shapes=[pltpu.VMEM(block, g.dtype), pltpu.SemaphoreType.DMA], name="own_block_last")(g)


def _scatter_start(g, land, kind, m):
    def body(g_ref, land_ref, send_sems, recv_sems, g_thru, land_thru, token):
        x, y, c = _my_coords()
        me = _dev_index(x, y, c)
        for rel in range(1, NDEV):
            peer = _peer(x, y, c, rel)
            pltpu.make_async_remote_copy(src_ref=_window(g_ref, kind, m, _dev_index(*peer)), dst_ref=land_ref.at[me],
                                         send_sem=send_sems.at[rel - 1], recv_sem=recv_sems.at[rel - 1],
                                         device_id=peer, device_id_type=MESH).start()
        token[...] = jnp.zeros_like(token)

    return pl.pallas_call(
        body, name="scatter_last_start",
        out_shape=(pltpu.SemaphoreType.DMA((NDEV - 1,)), pltpu.SemaphoreType.DMA((NDEV - 1,)), pltpu.HBM(g.shape, g.dtype),
                   pltpu.HBM(land.shape, land.dtype), jax.ShapeDtypeStruct((8, 128), F32)),
        in_specs=(HBM, HBM), out_specs=(SEM, SEM, HBM, HBM, pl.BlockSpec(memory_space=pltpu.VMEM)),
        input_output_aliases={0: 2, 1: 3},
        compiler_params=pltpu.CompilerParams(has_side_effects=pltpu.SideEffectType.DATAFLOW_SIDE_EFFECTING),
    )(pltpu.with_memory_space_constraint(g, pltpu.HBM), pltpu.with_memory_space_constraint(land, pltpu.HBM))


def _scatter_wait(send_sems, recv_sems, g_thru, land_thru, after, kind, m):
    n_after = len(after)

    def body(*refs):
        g_ref, land_ref, send_sems, recv_sems = refs[:4]
        x, y, c = _my_coords()
        me = _dev_index(x, y, c)
        for rel in range(1, NDEV):
            peer = _peer(x, y, c, rel)
            dev = _dev_index(*peer)
            cp = pltpu.make_async_remote_copy(src_ref=_window(g_ref, kind, m, me), dst_ref=land_ref.at[dev],
                                              send_sem=send_sems.at[rel - 1], recv_sem=recv_sems.at[rel - 1],
                                              device_id=peer, device_id_type=MESH)
            cp.wait_send()
            cp.wait_recv()

    return pl.pallas_call(
        body, name="scatter_last_wait",
        out_shape=(pltpu.HBM(g_thru.shape, g_thru.dtype), pltpu.HBM(land_thru.shape, land_thru.dtype)),
        in_specs=(HBM, HBM, SEM, SEM) + (ANY,) * n_after, out_specs=(HBM, HBM), input_output_aliases={0: 0, 1: 1},
        compiler_params=pltpu.CompilerParams(has_side_effects=pltpu.SideEffectType.DATAFLOW_SIDE_EFFECTING),
    )(g_thru, land_thru, send_sems, recv_sems, *after)[1]


def _adamw_math(w, g, m, v):
    m = ADAM_B1 * m + (1.0 - ADAM_B1) * g
    v = ADAM_B2 * v + (1.0 - ADAM_B2) * (g * g)
    m_hat = m / (1.0 - ADAM_B1 ** ADAM_STEP)
    v_hat = v / (1.0 - ADAM_B2 ** ADAM_STEP)
    delta = -ADAM_LR * (m_hat / (jnp.sqrt(v_hat) + ADAM_EPS) + ADAM_WD * w)
    return delta, m, v


def _sum_adamw(parts, w, m, v, name, after=None):
    R, C = w.shape
    n_parts = len(parts)
    cg = C // n_parts
    tr = max(t for t in range(8, 257, 8) if R % t == 0)
    deps = [] if after is None else [after]

    def body(*refs):
        p_refs = refs[:n_parts]
        w_ref, m_ref, v_ref = refs[n_parts:n_parts + 3]
        g_ref, d_ref, mo_ref, vo_ref = refs[n_parts + 3 + len(deps):]
        for k, p_ref in enumerate(p_refs):
            @pl.when(pl.program_id(0) == k)
            def _():
                g = p_ref[0].astype(F32)
                for d in range(1, NDEV):
                    g = g + p_ref[d].astype(F32)
                g_ref[...] = g
                d_ref[...], mo_ref[...], vo_ref[...] = _adamw_math(w_ref[...], g, m_ref[...], v_ref[...])

    part = pl.BlockSpec((NDEV, tr, cg), lambda k, t: (0, t, 0))
    blk = pl.BlockSpec((tr, cg), lambda k, t: (t, k))
    return pl.pallas_call(
        body, grid=(n_parts, R // tr), name=name, in_specs=[part] * n_parts + [blk, blk, blk] + [ANY] * len(deps),
        out_specs=[blk] * 4, out_shape=[jax.ShapeDtypeStruct((R, C), F32)] * 4, compiler_params=_params("parallel", "parallel"),
    )(*parts, w, m, v, *deps)


def _adamw_small(ws, gs, ms, vs):
    n = len(ws)

    def body(*refs):
        w_refs, g_refs, m_refs, v_refs = (refs[k * n:(k + 1) * n] for k in range(4))
        d_refs, mo_refs, vo_refs = (refs[(4 + k) * n:(5 + k) * n] for k in range(3))
        for i in range(n):
            d_refs[i][...], mo_refs[i][...], vo_refs[i][...] = _adamw_math(w_refs[i][...], g_refs[i][...], m_refs[i][...], v_refs[i][...])

    shapes = [jax.ShapeDtypeStruct(a.shape, F32) for a in ws]
    outs = pl.pallas_call(body, out_shape=shapes * 3, name="adamw_small", compiler_params=_params())(*ws, *gs, *ms, *vs)
    return outs[:n], outs[n:2 * n], outs[2 * n:]


def kernel(x, mem, ffn1_norm, ffn1_w_up, ffn1_w_down, mix_norm, mem_norm, w_in, b_gate, conv_dw_w, conv_dw_b, conv_ln_g, conv_ln_b, conv_w_pw, att_rel_bias, att_w_o, mem_w_kv, mem_w_o, w_out, ffn2_norm, ffn2_w_up, ffn2_w_down, final_norm, loss_target, m_ffn1_norm, m_ffn1_w_up, m_ffn1_w_down, m_mix_norm, m_mem_norm, m_w_in, m_b_gate, m_conv_dw_w, m_conv_dw_b, m_conv_ln_g, m_conv_ln_b, m_conv_w_pw, m_att_rel_bias, m_att_w_o, m_mem_w_kv, m_mem_w_o, m_w_out, m_ffn2_norm, m_ffn2_w_up, m_ffn2_w_down, m_final_norm, v_ffn1_norm, v_ffn1_w_up, v_ffn1_w_down, v_mix_norm, v_mem_norm, v_w_in, v_b_gate, v_conv_dw_w, v_conv_dw_b, v_conv_ln_g, v_conv_ln_b, v_conv_w_pw, v_att_rel_bias, v_att_w_o, v_mem_w_kv, v_mem_w_o, v_w_out, v_ffn2_norm, v_ffn2_w_up, v_ffn2_w_down, v_final_norm):
    given = dict(locals())
    w = {n: given[n] for n in WEIGHTS}
    mom = {n: given["m_" + n] for n in WEIGHTS}
    var = {n: given["v_" + n] for n in WEIGHTS}

    NB, S, _ = x.shape
    T = NB * S
    ML = mem.shape[1]
    x0 = x.reshape(T, D)
    target = loss_target.reshape(T, D)
    mem2 = mem.reshape(NB * ML, D)

    def block(t, n):
        return jnp.transpose(t[0]) if n in TRANSPOSED else t[0]

    sh = dict(zip(BIG_ORDER, _cast_shards([block(w[n], n) for n in BIG_ORDER])))
    dw_t = jnp.transpose(conv_dw_w[0])

    def gather(names, extra=(), extra_kinds=()):
        return _gather_ride([sh[n] for n in names] + list(extra), [BIG[n] for n in names] + list(extra_kinds))

    W = {}
    names0 = ['ffn1_w_up', 'ffn1_w_down']
    got = _exchange_alone(gather(names0, [dw_t], [('row', dw_t.shape[0])]), "gather_ffn1")
    W.update(zip(names0, got[:2]))
    dw_full = jnp.transpose(got[2])
    conv_vec = jnp.concatenate([conv_dw_b, conv_ln_g, conv_ln_b, jnp.zeros((5, CONV_W), F32)], axis=0)
    tab = _bias_table(att_rel_bias[0])
    fin_g = final_norm.reshape(1, D)

    names1 = ['w_in', 'conv_w_pw', 'att_w_o', 'mem_w_kv', 'mem_w_o', 'w_out']
    (x1, ab1), got = _ffn_fwd(x0, ffn1_norm, W['ffn1_w_up'], W['ffn1_w_down'], TILE_FFN, "ffn1_fwd", ride=gather(names1))
    W.update(zip(names1, got))
    (uc, qkv, mq, gl, hmix), _ = _mix_fwd(x1, mix_norm, W['w_in'], TILE_TOKENS)
    uc3 = uc.reshape(NB, S, 2 * CONV_W)
    qkv3 = qkv.reshape(NB, S, 3 * ATT_W)
    mq3 = mq.reshape(NB, S, MEM_W)
    cact, conv_z = _conv_fwd(uc3, dw_full, conv_vec)
    cact = cact.reshape(T, CONV_W)
    names2 = ['ffn2_w_up', 'ffn2_w_down']
    oatt, got = _att_fwd(qkv3, tab, ride=gather(names2))
    W.update(zip(names2, got))
    oatt = oatt.reshape(T, ATT_W)
    memh, kv = _memkv_fwd(mem2, mem_norm, W['mem_w_kv'], TILE_TOKENS)
    kv3 = kv.reshape(NB, ML, 2 * MEM_W)
    omem = _mematt_fwd(mq3, kv3, TILE_TOKENS).reshape(T, MEM_W)
    branch_w = (W['conv_w_pw'], W['att_w_o'], W['mem_w_o'], W['w_out'])
    x2, ymix = _combine_fwd(x1, cact, oatt, omem, gl, b_gate, *branch_w, TILE_COMBINE)
    dx3, ab2, loss_part, dg_final = _ffn_fwd_loss(x2, ffn2_norm, W['ffn2_w_up'], W['ffn2_w_down'], fin_g, target, TILE_FFN,
                                                  "ffn2_fwd_loss")

    def scatter(grads, names):
        return _scatter_ride(grads, [BIG[n] for n in names])

    G, P = {}, {}
    dx2, dab2, act2, h2, dg_ffn2 = _ffn_bwd(x2, dx3, ab2, ffn2_norm, W['ffn2_w_up'], W['ffn2_w_down'], TILE_FFN, "ffn2_bwd")
    g_up, _ = _tn_matmul(dab2, h2, 512, "grad_ffn2_w_up_a", tt=TILE_GRAD_TOKENS_WIDE, x_part=(0, 2), out_rows=2 * FF)
    G['ffn2_w_up'], _ = _tn_matmul(dab2, h2, 512, "grad_ffn2_w_up_b", tt=TILE_GRAD_TOKENS_WIDE, x_part=(1, 2), out_rows=2 * FF,
                                   prev=g_up)
    G['ffn2_w_down'], _ = _tn_matmul(act2, dx3, 512, "grad_ffn2_w_down", scale=0.5, tt=TILE_GRAD_TOKENS_WIDE)
    (dgl, dcact, doatt, domem, dyc, dya, dym, dbg), got = _combine_bwd(
        dx2, cact, oatt, omem, gl, b_gate, *branch_w, TILE_COMBINE, ride=scatter([G['ffn2_w_up']], ['ffn2_w_up']))
    P['ffn2_w_up'] = got
    G['w_out'], _ = _tn_matmul(ymix, dx2, 512, "grad_w_out")
    G['conv_w_pw'], _ = _tn_matmul(cact, dyc, 512, "grad_conv_w_pw")
    G['att_w_o'], _ = _tn_matmul(oatt, dya, 512, "grad_att_w_o")
    G['mem_w_o'], _ = _tn_matmul(omem, dym, 512, "grad_mem_w_o")
    dmq3, dkv3 = _mematt_bwd(mq3, kv3, domem.reshape(NB, S, MEM_W), TILE_TOKENS)
    dkv = dkv3.reshape(NB * ML, 2 * MEM_W)
    dg_mem = _memkv_bwd(mem2, dkv, W['mem_w_kv'], TILE_TOKENS)
    G['mem_w_kv'], _ = _tn_matmul(memh, dkv, 512, "grad_mem_w_kv")
    names = ['ffn2_w_down', 'w_out', 'conv_w_pw', 'att_w_o', 'mem_w_o']
    (dqkv3, dscore), got = _att_bwd(qkv3, doatt.reshape(NB, S, ATT_W), tab, ride=scatter([G[n] for n in names], names))
    P.update((n, [p]) for n, p in zip(names, got))
    d_rel = _rel_bias_grad(dscore)
    (duc3, d_dw, d_cvec), got = _conv_bwd(uc3, conv_z, dcact.reshape(NB, S, CONV_W), dw_full, conv_vec,
                                          ride=scatter([G['mem_w_kv']], ['mem_w_kv']))
    P['mem_w_kv'] = got
    duc, dqkv, dmq = duc3.reshape(T, 2 * CONV_W), dqkv3.reshape(T, 3 * ATT_W), dmq3.reshape(T, MEM_W)
    g_in, _ = _tn_matmul(hmix, duc, 512, "grad_w_in_conv", out_cols=IN_COLS, col_off=0)
    g_in, _ = _tn_matmul(hmix, dqkv, 512, "grad_w_in_qkv", out_cols=IN_COLS, col_off=1024, prev=g_in)
    g_in, _ = _tn_matmul(hmix, dmq, 512, "grad_w_in_mq", out_cols=IN_COLS, col_off=2560, prev=g_in)
    G['w_in'], _ = _tn_matmul(hmix, dgl, 1024, "grad_w_in_gate", out_cols=IN_COLS, col_off=3072, prev=g_in)
    (dx1, dg_mix), got = _mix_bwd(x1, dx2, duc, dqkv, dmq, dgl, mix_norm, W['w_in'], TILE_TOKENS,
                                  ride=scatter([G['w_in']], ['w_in']))
    P['w_in'] = got
    dx0, dab1, act1, h1, dg_ffn1 = _ffn_bwd(x0, dx1, ab1, ffn1_norm, W['ffn1_w_up'], W['ffn1_w_down'], TILE_FFN, "ffn1_bwd")
    g_wd1, _ = _tn_matmul(act1, dx1, 512, "grad_ffn1_w_down", scale=0.5, tt=TILE_GRAD_TOKENS_WIDE)
    g_wu1a, got = _tn_matmul(dab1, h1, 512, "grad_ffn1_w_up_a", tt=TILE_GRAD_TOKENS_WIDEST, y_part=(0, 2),
                             ride=scatter([g_wd1], ['ffn1_w_down']))
    P['ffn1_w_down'] = got
    g_wu1b, got_a = _tn_matmul(dab1, h1, 512, "grad_ffn1_w_up_b", tt=TILE_GRAD_TOKENS_WIDEST, y_part=(1, 2),
                               ride=scatter([g_wu1a], ['ffn1_w_up']))
    last_kind = BIG['ffn1_w_up']
    send_sems, recv_sems, g_thru, land_thru, token = _scatter_start(g_wu1b, _own_block(g_wu1b, *last_kind), *last_kind)

    small_names = ['loss', 'ffn1_norm', 'mix_norm', 'mem_norm', 'b_gate', 'conv_dw_w', 'conv_vec', 'att_rel_bias', 'ffn2_norm',
                   'final_norm']
    small = dict(zip(small_names, _all_sum_small(
        [loss_part + token[0:1], dg_ffn1, dg_mix, dg_mem, dbg, d_dw, d_cvec, d_rel, dg_ffn2, dg_final])))
    loss = small['loss'][0, 0]
    me = _dev_index(*_my_coords())
    for i, n in enumerate(['conv_dw_b', 'conv_ln_g', 'conv_ln_b']):
        small[n] = small['conv_vec'][i:i + 1]
    small['conv_dw_w'] = lax.dynamic_slice(small['conv_dw_w'], (0, me * conv_dw_w.shape[2]), (CONV_K, conv_dw_w.shape[2]))
    little = [n for n in WEIGHTS if n not in BIG]
    as2d = lambda t, n: t.reshape(small[n].shape)
    d_s, m_s, v_s = _adamw_small([as2d(w[n], n) for n in little], [small[n] for n in little],
                                 [as2d(mom[n], n) for n in little], [as2d(var[n], n) for n in little])
    grad, delta, new_m, new_v = {}, {}, {}, {}
    for i, n in enumerate(little):
        grad[n], delta[n], new_m[n], new_v[n] = (t.reshape(w[n].shape) for t in (small[n], d_s[i], m_s[i], v_s[i]))
    done = [d_s[0]]
    for n in BIG_ORDER[1:] + BIG_ORDER[:1]:
        if n == 'ffn1_w_up':
            P[n] = [got_a[0], _scatter_wait(send_sems, recv_sems, g_thru, land_thru, done, *last_kind)]
        outs = _sum_adamw(P[n], block(w[n], n), block(mom[n], n), block(var[n], n), "adamw_" + n,
                          after=None if n == 'ffn1_w_up' else token)
        done.append(outs[0])
        grad[n], delta[n], new_m[n], new_v[n] = ((jnp.transpose(t) if n in TRANSPOSED else t)[None] for t in outs)

    return (loss, dx0.reshape(NB, S, D), *[grad[n] for n in WEIGHTS], *[delta[n] for n in WEIGHTS],
            *[new_m[n] for n in WEIGHTS], *[new_v[n] for n in WEIGHTS])
```

```python
import functools

import jax
import jax.numpy as jnp
from jax import lax
from jax.experimental import pallas as pl
from jax.experimental.pallas import tpu as pltpu

F32 = jnp.float32
BF16 = jnp.bfloat16

EPS = 1e-6
MASK_VALUE = -1e30
D = 1024
NDEV = 8
FF = 2816
FF_SHARD = 704
FF_HALF_ROWS = 352
FF_BLOCK_EDGES = ()
IN_COLS = 6144
CONV_W = 512
CONV_K = 31
CONV_HALO = 32
CONV_CHUNK = 32
CONV_WIN = CONV_CHUNK + 40
GLU_CHUNK = 128
ATT_W = 512
ATT_HEADS = 8
ATT_HD = 64
CHUNK = 64
LEFT_CHUNKS = 8
MAX_REL = 128
N_REL = 192
QB = 256
KWIN = QB + LEFT_CHUNKS * CHUNK
KPAD = LEFT_CHUNKS * CHUNK
DS_LANES = 1024
MEM_W = 512
MEM_HEADS = 4
MEM_HD = 128
ADAM_LR = 0.001
ADAM_B1 = 0.9
ADAM_B2 = 0.999
ADAM_EPS = 1e-08
ADAM_WD = 0.01
ADAM_STEP = 10
VMEM_LIMIT = 60 * 1024 * 1024
TILE_FFN = 256
TILE_COMBINE = 256
TILE_TOKENS = 512
TILE_GRAD_TOKENS = 2048
TILE_GRAD_TOKENS_WIDE = 1024
TILE_GRAD_TOKENS_WIDEST = 512

MESH = pl.DeviceIdType.MESH
ANY = pl.BlockSpec(memory_space=pl.ANY)

WEIGHTS = ['ffn1_norm', 'ffn1_w_up', 'ffn1_w_down', 'mix_norm', 'mem_norm', 'w_in', 'b_gate', 'conv_dw_w', 'conv_dw_b',
           'conv_ln_g', 'conv_ln_b', 'conv_w_pw', 'att_rel_bias', 'att_w_o', 'mem_w_kv', 'mem_w_o', 'w_out', 'ffn2_norm',
           'ffn2_w_up', 'ffn2_w_down', 'final_norm']
BIG = {
    'ffn1_w_up': ('row', FF_SHARD), 'ffn1_w_down': ('row', FF_HALF_ROWS), 'w_in': ('col', 768),
    'conv_w_pw': ('col', 128), 'att_w_o': ('col', 128), 'mem_w_kv': ('row', 128), 'mem_w_o': ('col', 128),
    'w_out': ('row', 128), 'ffn2_w_up': ('row', FF_SHARD), 'ffn2_w_down': ('row', FF_HALF_ROWS),
}
BIG_ORDER = ['ffn1_w_up', 'ffn1_w_down', 'w_in', 'conv_w_pw', 'att_w_o', 'mem_w_kv', 'mem_w_o', 'w_out', 'ffn2_w_up', 'ffn2_w_down']
TRANSPOSED = ('ffn1_w_up', 'ffn2_w_up')


def _dot(a, b):
    return jnp.dot(a, b, preferred_element_type=F32)


def _dot_nt(a, b):
    return lax.dot_general(a, b, (((1,), (1,)), ((), ())), preferred_element_type=F32)


def _dot_tn(a, b):
    return lax.dot_general(a, b, (((0,), (0,)), ((), ())), preferred_element_type=F32)


def _sigmoid(v):
    return jax.nn.sigmoid(v)


def _const(shape):
    return pl.BlockSpec(shape, lambda *_: (0,) * len(shape), pipeline_mode=pl.Buffered(1))


def _params(*sem):
    return pltpu.CompilerParams(dimension_semantics=sem if sem else None, vmem_limit_bytes=VMEM_LIMIT)


def _my_coords():
    return lax.axis_index("x"), lax.axis_index("y"), lax.axis_index("c")


def _dev_index(px, py, pc):
    return 4 * px + 2 * py + pc


def _window(ref, kind, n, p):
    if kind == 'row':
        return ref.at[pl.ds(pl.multiple_of(p * n, n), n), :]
    return ref.at[:, pl.ds(pl.multiple_of(p * n, 128), n)]


def _full_shape(kind, n, shard_shape):
    if kind == 'row':
        return (NDEV * n, shard_shape[1])
    return (shard_shape[0], NDEV * n)


def _cast_shards(shards):
    n = len(shards)

    def body(*refs):
        for i in range(n):
            refs[n + i][...] = refs[i][...].astype(BF16)

    out_shape = [jax.ShapeDtypeStruct(s.shape, BF16) for s in shards]
    return pl.pallas_call(body, out_shape=out_shape, name="cast_shards", compiler_params=_params())(*shards)


class _Ride:
    def __init__(self, inputs, out_shape, scratch, start, finish, mid=None):
        self.inputs, self.out_shape, self.scratch = list(inputs), list(out_shape), list(scratch)
        self.start, self.finish, self.mid = start, finish, mid


def _pallas(body, name, grid, in_specs, out_specs, out_shape, args, scratch_shapes=(), sem=None, aliases=None, ride=None):
    if ride is None:
        outs = pl.pallas_call(body, grid=grid, name=name, in_specs=in_specs, out_specs=out_specs, out_shape=out_shape,
                              scratch_shapes=list(scratch_shapes), input_output_aliases=aliases or {},
                              compiler_params=_params(*sem))(*args)
        return list(outs), []
    n_in, n_out, n_scr = len(args), len(out_shape), len(scratch_shapes)
    r_in, r_out = len(ride.inputs), len(ride.out_shape)

    def wrapped(*refs):
        k_in, rin = refs[:n_in], refs[n_in:n_in + r_in]
        o0 = n_in + r_in
        k_out, rout = refs[o0:o0 + n_out], refs[o0 + n_out:o0 + n_out + r_out]
        s0 = o0 + n_out + r_out
        k_scr, rscr = refs[s0:s0 + n_scr], refs[s0 + n_scr:]
        ids = [pl.program_id(k) for k in range(len(grid))]
        first = functools.reduce(jnp.logical_and, [i == 0 for i in ids])
        last = functools.reduce(jnp.logical_and, [i == g - 1 for i, g in zip(ids, grid)])
        pl.when(first)(lambda: ride.start(rin, rout, rscr))
        if ride.mid is not None:
            at_mid = functools.reduce(jnp.logical_and, [ids[0] == (3 * grid[0]) // 4] + [i == 0 for i in ids[1:]])
            pl.when(at_mid)(lambda: ride.mid(rin, rout, rscr))
        body(*k_in, *k_out, *k_scr)
        pl.when(last)(lambda: ride.finish(rin, rout, rscr))

    outs = pl.pallas_call(
        wrapped, grid=grid, name=name, in_specs=list(in_specs) + [ANY] * r_in, out_specs=list(out_specs) + [ANY] * r_out,
        out_shape=list(out_shape) + ride.out_shape, scratch_shapes=list(scratch_shapes) + ride.scratch,
        input_output_aliases=aliases or {}, compiler_params=_params(*(["arbitrary"] * len(grid))),
    )(*args, *ride.inputs)
    return list(outs[:n_out]), list(outs[n_out:])


def _exchange_alone(ride, name):
    r_in, r_out = len(ride.inputs), len(ride.out_shape)

    def body(*refs):
        rin, rout, rscr = refs[:r_in], refs[r_in:r_in + r_out], refs[r_in + r_out:]
        ride.start(rin, rout, rscr)
        if ride.mid is not None:
            ride.mid(rin, rout, rscr)
        ride.finish(rin, rout, rscr)

    return pl.pallas_call(body, out_shape=ride.out_shape, in_specs=[ANY] * r_in, out_specs=[ANY] * r_out, name=name,
                          scratch_shapes=ride.scratch, compiler_params=pltpu.CompilerParams(has_side_effects=True))(*ride.inputs)


def _gather_ride(shards, kinds):
    n = len(shards)

    def plan(rin, out, sems):
        send_sems, recv_sems, local_sems = sems[:3]
        x, y, c = _my_coords()
        me, sibling = (x, y, c), (x, y, 1 - c)
        chips = [(1 - x, y), (x, 1 - y), (1 - x, 1 - y)]

        def win(i, dev):
            return _window(out[i], kinds[i][0], kinds[i][1], _dev_index(*dev))

        def copy(i, k, block, to, from_shard=False):
            return pltpu.make_async_remote_copy(
                src_ref=rin[i] if from_shard else win(i, block), dst_ref=win(i, block),
                send_sem=send_sems.at[i, k], recv_sem=recv_sems.at[i, k], device_id=to, device_id_type=MESH)

        def local():
            return [pltpu.make_async_copy(rin[i], win(i, me), local_sems.at[i]) for i in range(n)]

        def first():
            cps = []
            for i in range(n):
                cps.append(copy(i, 0, me, sibling, from_shard=True))
                cps += [copy(i, 1 + j, me, (*chip, c), from_shard=True) for j, chip in enumerate(chips)]
            return cps

        def arrived():
            return [copy(i, 1 + j, (*chip, c), me) for j, chip in enumerate(chips) for i in range(n)]

        def passed():
            return [copy(i, 4 + j, (*chip, c), sibling) for j, chip in enumerate(chips) for i in range(n)]

        def from_sibling():
            cps = [copy(i, 0, sibling, me) for i in range(n)]
            return cps + [copy(i, 4 + j, (*chip, 1 - c), me) for i in range(n) for j, chip in enumerate(chips)]

        return local, first, arrived, passed, from_sibling

    def start(rin, out, sems):
        local, first, _, _, _ = plan(rin, out, sems)
        for cp in local() + first():
            cp.start()

    def mid(rin, out, sems):
        _, _, arrived, passed, _ = plan(rin, out, sems)
        for got, fwd in zip(arrived(), passed()):
            got.wait_recv()
            fwd.start()

    def finish(rin, out, sems):
        local, first, _, passed, from_sibling = plan(rin, out, sems)
        for cp in from_sibling():
            cp.wait_recv()
        for cp in first() + passed():
            cp.wait_send()
        for cp in local():
            cp.wait()

    out_shape = [jax.ShapeDtypeStruct(_full_shape(k, m, s.shape), s.dtype) for s, (k, m) in zip(shards, kinds)]
    scratch = [pltpu.SemaphoreType.DMA((n, 7)), pltpu.SemaphoreType.DMA((n, 7)), pltpu.SemaphoreType.DMA((n,))]
    return _Ride(shards, out_shape, scratch, start, finish, mid)


def _scatter_ride(grads, kinds):
    n = len(grads)

    def plan(g, out, sems):
        send_sems, recv_sems, local_sems = sems
        x, y, c = _my_coords()
        me = _dev_index(x, y, c)

        def local():
            return [pltpu.make_async_copy(_window(g[i], kinds[i][0], kinds[i][1], me), out[i].at[me], local_sems.at[i])
                    for i in range(n)]

        def remote(arrival):
            cps = []
            for rel in range(1, NDEV):
                peer = _peer(x, y, c, rel)
                dev = _dev_index(*peer)
                for i in range(n):
                    kind, m = kinds[i]
                    cps.append(pltpu.make_async_remote_copy(
                        src_ref=_window(g[i], kind, m, me if arrival else dev), dst_ref=out[i].at[dev if arrival else me],
                        send_sem=send_sems.at[i, rel - 1], recv_sem=recv_sems.at[i, rel - 1], device_id=peer, device_id_type=MESH))
            return cps

        return local, remote

    def start(g, out, sems):
        local, remote = plan(g, out, sems)
        for cp in local() + remote(False):
            cp.start()

    def finish(g, out, sems):
        local, remote = plan(g, out, sems)
        for cp in remote(True):
            cp.wait_recv()
        for cp in remote(False):
            cp.wait_send()
        for cp in local():
            cp.wait()

    def block_shape(gr, kind, m):
        return (m, gr.shape[1]) if kind == 'row' else (gr.shape[0], m)

    out_shape = [jax.ShapeDtypeStruct((NDEV,) + block_shape(gr, k, m), gr.dtype) for gr, (k, m) in zip(grads, kinds)]
    scratch = [pltpu.SemaphoreType.DMA((n, NDEV - 1)), pltpu.SemaphoreType.DMA((n, NDEV - 1)), pltpu.SemaphoreType.DMA((n,))]
    return _Ride(grads, out_shape, scratch, start, finish)


def _rms_stats(xf):
    r = lax.rsqrt(jnp.mean(xf * xf, axis=-1, keepdims=True) + EPS)
    return xf * r, r


def _rms_bwd(dh, g, xhat, r):
    dxhat = dh * g
    return r * (dxhat - xhat * jnp.mean(dxhat * xhat, axis=-1, keepdims=True))


def _ffn_blocks():
    edges = (0,) + FF_BLOCK_EDGES + (FF,)
    return [(slice(lo, hi), slice(FF + lo, FF + hi)) for lo, hi in zip(edges[:-1], edges[1:])]


def _swiglu_tile(x_ref, g_ref, wut_ref, wd_ref, ab_ref):
    xf = x_ref[...]
    xhat, _ = _rms_stats(xf)
    h = (xhat * g_ref[...]).astype(BF16)
    acc = jnp.zeros(xf.shape, F32)
    for ra, rb in _ffn_blocks():
        a = _dot_nt(h, wut_ref[ra, :])
        b = _dot_nt(h, wut_ref[rb, :])
        ab_ref[:, ra] = a.astype(BF16)
        ab_ref[:, rb] = b.astype(BF16)
        act = (a * _sigmoid(a) * b).astype(BF16)
        acc = acc + _dot(act, wd_ref[ra, :])
    return xf + 0.5 * acc


def _ffn_fwd(x, g, wut, wd, tm, name, ride=None):
    T = x.shape[0]

    def body(x_ref, g_ref, wut_ref, wd_ref, xo_ref, ab_ref):
        xo_ref[...] = _swiglu_tile(x_ref, g_ref, wut_ref, wd_ref, ab_ref)

    return _pallas(
        body, name, (T // tm,),
        [pl.BlockSpec((tm, D), lambda t: (t, 0)), _const((1, D)), _const((2 * FF, D)), _const((FF, D))],
        [pl.BlockSpec((tm, D), lambda t: (t, 0)), pl.BlockSpec((tm, 2 * FF), lambda t: (t, 0))],
        [jax.ShapeDtypeStruct((T, D), F32), jax.ShapeDtypeStruct((T, 2 * FF), BF16)],
        (x, g, wut, wd), sem=("arbitrary",), ride=ride)


def _ffn_fwd_loss(x, g, wut, wd, g_final, target, tm, name):
    T = x.shape[0]

    def body(x_ref, g_ref, wut_ref, wd_ref, gf_ref, t_ref, dx_ref, ab_ref, loss_ref, dgf_ref):
        xhat, r = _rms_stats(_swiglu_tile(x_ref, g_ref, wut_ref, wd_ref, ab_ref))
        gain = gf_ref[...]
        diff = xhat * gain - t_ref[...]
        dout = diff * (1.0 / D)

        @pl.when(pl.program_id(0) == 0)
        def _():
            loss_ref[...] = jnp.zeros_like(loss_ref)
            dgf_ref[...] = jnp.zeros_like(dgf_ref)
        sq = jnp.sum(jnp.sum(diff * diff, axis=0, keepdims=True), axis=1, keepdims=True)
        loss_ref[...] += jnp.broadcast_to(sq * (0.5 / D), (1, 128))
        dgf_ref[...] += jnp.sum(dout * xhat, axis=0, keepdims=True)
        dx_ref[...] = _rms_bwd(dout, gain, xhat, r)

    row = pl.BlockSpec((tm, D), lambda t: (t, 0))
    return pl.pallas_call(
        body, grid=(T // tm,), name=name,
        in_specs=[row, _const((1, D)), _const((2 * FF, D)), _const((FF, D)), _const((1, D)), row],
        out_specs=[row, pl.BlockSpec((tm, 2 * FF), lambda t: (t, 0)), pl.BlockSpec((1, 128), lambda t: (0, 0)),
                   pl.BlockSpec((1, D), lambda t: (0, 0))],
        out_shape=[jax.ShapeDtypeStruct((T, D), F32), jax.ShapeDtypeStruct((T, 2 * FF), BF16),
                   jax.ShapeDtypeStruct((1, 128), F32), jax.ShapeDtypeStruct((1, D), F32)],
        compiler_params=_params("arbitrary"),
    )(x, g, wut, wd, g_final, target)


def _ffn_bwd(x, dy, ab, g, wut, wd, tm, name):
    T = x.shape[0]

    def body(x_ref, dy_ref, ab_ref, g_ref, wut_ref, wd_ref, dx_ref, dab_ref, act_ref, h_ref, dg_ref):
        xf = x_ref[...]
        xhat, r = _rms_stats(xf)
        gain = g_ref[...]
        h_ref[...] = (xhat * gain).astype(BF16)
        dy = dy_ref[...]
        dyh = (0.5 * dy).astype(BF16)
        dh = jnp.zeros((tm, D), F32)
        for ra, rb in _ffn_blocks():
            a = ab_ref[:, ra].astype(F32)
            b = ab_ref[:, rb].astype(F32)
            dact = _dot_nt(dyh, wd_ref[ra, :])
            sg = _sigmoid(a)
            sl = a * sg
            act_ref[:, ra] = (sl * b).astype(BF16)
            da = (dact * b * (sg * (1.0 + a * (1.0 - sg)))).astype(BF16)
            db = (dact * sl).astype(BF16)
            dab_ref[:, ra] = da
            dab_ref[:, rb] = db
            dh = dh + _dot(da, wut_ref[ra, :]) + _dot(db, wut_ref[rb, :])
        dx_ref[...] = dy + _rms_bwd(dh, gain, xhat, r)

        @pl.when(pl.program_id(0) == 0)
        def _():
            dg_ref[...] = jnp.zeros_like(dg_ref)
        dg_ref[...] += jnp.sum(dh * xhat, axis=0, keepdims=True)

    return pl.pallas_call(
        body, grid=(T // tm,), name=name,
        in_specs=[pl.BlockSpec((tm, D), lambda t: (t, 0)), pl.BlockSpec((tm, D), lambda t: (t, 0)),
                  pl.BlockSpec((tm, 2 * FF), lambda t: (t, 0)), _const((1, D)), _const((2 * FF, D)), _const((FF, D))],
        out_specs=[pl.BlockSpec((tm, D), lambda t: (t, 0)), pl.BlockSpec((tm, 2 * FF), lambda t: (t, 0)),
                   pl.BlockSpec((tm, FF), lambda t: (t, 0)), pl.BlockSpec((tm, D), lambda t: (t, 0)),
                   pl.BlockSpec((1, D), lambda t: (0, 0))],
        out_shape=[jax.ShapeDtypeStruct((T, D), F32), jax.ShapeDtypeStruct((T, 2 * FF), BF16),
                   jax.ShapeDtypeStruct((T, FF), BF16), jax.ShapeDtypeStruct((T, D), BF16), jax.ShapeDtypeStruct((1, D), F32)],
        compiler_params=_params("arbitrary"),
    )(x, dy, ab, g, wut, wd)


def _tn_matmul(xm, ym, tn, name, scale=None, out_cols=None, col_off=0, prev=None, tt=TILE_GRAD_TOKENS, x_part=(0, 1),
               out_rows=None, y_part=(0, 1), ride=None):
    T = xm.shape[0]
    xi, xn = x_part
    yi, yn = y_part
    K = xm.shape[1] // xn
    N = ym.shape[1] // yn
    out_cols = N if out_cols is None else out_cols
    row_blk = xi if out_rows is not None else 0
    out_rows = K if out_rows is None else out_rows
    tt = min(tt, T)
    nt = T // tt
    off = col_off // tn

    def body(*refs):
        x_ref, y_ref = refs[0], refs[1]
        o_ref, acc = refs[-2], refs[-1]

        @pl.when(pl.program_id(1) == 0)
        def _():
            acc[...] = jnp.zeros_like(acc)
        acc[...] += _dot_tn(x_ref[...].astype(BF16), y_ref[...].astype(BF16))

        @pl.when(pl.program_id(1) == nt - 1)
        def _():
            res = acc[...]
            o_ref[...] = (res if scale is None else res * scale).astype(BF16)

    ycol = yi * (N // tn)
    in_specs = [pl.BlockSpec((tt, K), lambda n, t: (t, xi)), pl.BlockSpec((tt, tn), lambda n, t: (t, n + ycol))]
    args = [xm, ym]
    aliases = {}
    if prev is not None:
        in_specs.append(ANY)
        args.append(prev)
        aliases = {2: 0}
    outs, rode = _pallas(
        body, name, (N // tn, nt), in_specs, [pl.BlockSpec((K, tn), lambda n, t: (row_blk, n + off))],
        [jax.ShapeDtypeStruct((out_rows, out_cols), BF16)], args, scratch_shapes=[pltpu.VMEM((K, tn), F32)],
        sem=("parallel", "arbitrary"), aliases=aliases, ride=ride)
    return outs[0], rode


def _mix_fwd(x, g, w_in, tm, ride=None):
    T = x.shape[0]

    def body(x_ref, g_ref, w_ref, uc_ref, qkv_ref, mq_ref, gl_ref, h_ref):
        xhat, _ = _rms_stats(x_ref[...])
        h = (xhat * g_ref[...]).astype(BF16)
        h_ref[...] = h
        uc_ref[...] = _dot(h, w_ref[:, 0:1024])
        qkv_ref[...] = _dot(h, w_ref[:, 1024:2560]).astype(BF16)
        mq_ref[...] = _dot(h, w_ref[:, 2560:3072]).astype(BF16)
        for j in range(3):
            gl_ref[:, j * D:(j + 1) * D] = _dot(h, w_ref[:, 3072 + j * D:3072 + (j + 1) * D]).astype(BF16)

    row = lambda w: pl.BlockSpec((tm, w), lambda t: (t, 0))
    return _pallas(
        body, "mix_fwd", (T // tm,), [row(D), _const((1, D)), _const((D, IN_COLS))],
        [row(1024), row(1536), row(512), row(3072), row(D)],
        [jax.ShapeDtypeStruct((T, 1024), F32), jax.ShapeDtypeStruct((T, 1536), BF16), jax.ShapeDtypeStruct((T, 512), BF16),
         jax.ShapeDtypeStruct((T, 3072), BF16), jax.ShapeDtypeStruct((T, D), BF16)],
        (x, g, w_in), sem=("parallel",), ride=ride)


def _mix_bwd(x, dres, duc, dqkv, dmq, dgl, g, w_in, tm, ride=None):
    T = x.shape[0]

    def body(x_ref, dres_ref, duc_ref, dqkv_ref, dmq_ref, dgl_ref, g_ref, w_ref, dx_ref, dg_ref):
        xhat, r = _rms_stats(x_ref[...])
        dh = _dot_nt(duc_ref[...], w_ref[:, 0:1024])
        dh = dh + _dot_nt(dqkv_ref[...], w_ref[:, 1024:2560])
        dh = dh + _dot_nt(dmq_ref[...], w_ref[:, 2560:3072])
        dh = dh + _dot_nt(dgl_ref[...], w_ref[:, 3072:6144])
        dx_ref[...] = dres_ref[...] + _rms_bwd(dh, g_ref[...], xhat, r)

        @pl.when(pl.program_id(0) == 0)
        def _():
            dg_ref[...] = jnp.zeros_like(dg_ref)
        dg_ref[...] += jnp.sum(dh * xhat, axis=0, keepdims=True)

    row = lambda w: pl.BlockSpec((tm, w), lambda t: (t, 0))
    return _pallas(
        body, "mix_bwd", (T // tm,),
        [row(D), row(D), row(1024), row(1536), row(512), row(3072), _const((1, D)), _const((D, IN_COLS))],
        [row(D), pl.BlockSpec((1, D), lambda t: (0, 0))],
        [jax.ShapeDtypeStruct((T, D), F32), jax.ShapeDtypeStruct((1, D), F32)],
        (x, dres, duc, dqkv, dmq, dgl, g, w_in), sem=("arbitrary",), ride=ride)


def _shifted(win, base, copies):
    for k in range(8):
        copies[k] = win[base + k:base + k + CONV_CHUNK + 24]
    return copies


def _tap_slices(copies, tap):
    out = []
    for k in range(8):
        for a in range(4):
            j = tap(a, k)
            if 0 <= j < CONV_K:
                out.append((j, copies[k, pl.ds(8 * a, CONV_CHUNK), :]))
    return out


def _conv_taps(copies, w_ref, tap):
    acc = jnp.zeros((CONV_CHUNK, CONV_W), F32)
    for j, rows in _tap_slices(copies, tap):
        acc = acc + rows * w_ref[j:j + 1, :]
    return acc


def _fold8(v):
    acc = v[0:8]
    for r in range(8, CONV_CHUNK, 8):
        acc = acc + v[r:r + 8]
    return acc


def _glu_into(uc_ref, vpad, S):
    vpad[pl.ds(0, CONV_HALO), :] = jnp.zeros((CONV_HALO, CONV_W), F32)
    vpad[pl.ds(S + CONV_HALO, CONV_HALO), :] = jnp.zeros((CONV_HALO, CONV_W), F32)

    def glu(i, carry):
        r0 = pl.multiple_of(i * GLU_CHUNK, GLU_CHUNK)
        a = uc_ref[0, pl.ds(r0, GLU_CHUNK), 0:CONV_W]
        gt = uc_ref[0, pl.ds(r0, GLU_CHUNK), CONV_W:2 * CONV_W]
        vpad[pl.ds(pl.multiple_of(r0 + CONV_HALO, CONV_HALO), GLU_CHUNK), :] = a * _sigmoid(gt)
        return carry
    lax.fori_loop(0, S // GLU_CHUNK, glu, 0)


def _layer_norm(z, vec_ref):
    xc = z - jnp.mean(z, axis=-1, keepdims=True)
    rstd = lax.rsqrt(jnp.mean(xc * xc, axis=-1, keepdims=True) + EPS)
    xn = xc * rstd
    return xn, rstd, xn * vec_ref[1:2, :] + vec_ref[2:3, :]


def _conv_fwd(uc, dw_w, vec):
    NB, S, _ = uc.shape

    def body(uc_ref, w_ref, vec_ref, o_ref, z_ref, vpad, copies):
        _glu_into(uc_ref, vpad, S)

        def conv(i, carry):
            r0 = pl.multiple_of(i * CONV_CHUNK, CONV_CHUNK)
            win = vpad[pl.ds(r0, CONV_WIN), :]
            z = _conv_taps(_shifted(win, CONV_HALO - (CONV_K - 1), copies), w_ref, lambda a, k: 8 * a + k) + vec_ref[0:1, :]
            z_ref[0, pl.ds(r0, CONV_CHUNK), :] = z
            _, _, yln = _layer_norm(z, vec_ref)
            o_ref[0, pl.ds(r0, CONV_CHUNK), :] = (yln * _sigmoid(yln)).astype(BF16)
            return carry
        lax.fori_loop(0, S // CONV_CHUNK, conv, 0, unroll=2)

    seq = pl.BlockSpec((1, S, CONV_W), lambda b: (b, 0, 0))
    return pl.pallas_call(
        body, grid=(NB,), name="conv_fwd",
        in_specs=[pl.BlockSpec((1, S, 2 * CONV_W), lambda b: (b, 0, 0)), _const((CONV_K, CONV_W)), _const((8, CONV_W))],
        out_specs=[seq, seq],
        out_shape=[jax.ShapeDtypeStruct((NB, S, CONV_W), BF16), jax.ShapeDtypeStruct((NB, S, CONV_W), F32)],
        scratch_shapes=[pltpu.VMEM((S + 2 * CONV_HALO, CONV_W), F32), pltpu.VMEM((8, CONV_CHUNK + 24, CONV_W), F32)],
        compiler_params=_params("parallel"),
    )(uc, dw_w, vec)


def _conv_bwd(uc, z, dcact, dw_w, vec, ride=None):
    NB, S, _ = uc.shape
    n_chunks = S // CONV_CHUNK

    def body(uc_ref, z_ref, dc_ref, w_ref, vec_ref, duc_ref, dw_ref, dvec_ref, vpad, dzpad, dw8, dvec8, copies):
        @pl.when(pl.program_id(0) == 0)
        def _():
            dw8[...] = jnp.zeros_like(dw8)
            dvec8[...] = jnp.zeros_like(dvec8)
        _glu_into(uc_ref, vpad, S)
        dzpad[pl.ds(S, 2 * CONV_HALO), :] = jnp.zeros((2 * CONV_HALO, CONV_W), F32)

        def norm_bwd(i, carry):
            r0 = pl.multiple_of(i * CONV_CHUNK, CONV_CHUNK)
            xn, rstd, yln = _layer_norm(z_ref[0, pl.ds(r0, CONV_CHUNK), :], vec_ref)
            sg = _sigmoid(yln)
            dyln = dc_ref[0, pl.ds(r0, CONV_CHUNK), :] * (sg * (1.0 + yln * (1.0 - sg)))
            dxn = dyln * vec_ref[1:2, :]
            dz = rstd * (dxn - jnp.mean(dxn, axis=-1, keepdims=True) - xn * jnp.mean(dxn * xn, axis=-1, keepdims=True))
            dzpad[pl.ds(r0, CONV_CHUNK), :] = dz
            dvec8[0] += _fold8(dz)
            dvec8[1] += _fold8(dyln * xn)
            dvec8[2] += _fold8(dyln)
            return carry
        lax.fori_loop(0, n_chunks, norm_bwd, 0, unroll=2)

        def taps_bwd(i, carry):
            r0 = pl.multiple_of(i * CONV_CHUNK, CONV_CHUNK)
            dzwin = dzpad[pl.ds(r0, CONV_WIN), :]
            dv = _conv_taps(_shifted(dzwin, 0, copies), w_ref, lambda a, k: CONV_K - 1 - 8 * a - k)
            dz = dzwin[0:CONV_CHUNK]
            vwin = vpad[pl.ds(r0, CONV_WIN), :]
            for j, rows in _tap_slices(_shifted(vwin, CONV_HALO - (CONV_K - 1), copies), lambda a, k: 8 * a + k):
                dw8[j] += _fold8(dz * rows)
            a = uc_ref[0, pl.ds(r0, CONV_CHUNK), 0:CONV_W]
            sg = _sigmoid(uc_ref[0, pl.ds(r0, CONV_CHUNK), CONV_W:2 * CONV_W])
            duc_ref[0, pl.ds(r0, CONV_CHUNK), 0:CONV_W] = (dv * sg).astype(BF16)
            duc_ref[0, pl.ds(r0, CONV_CHUNK), CONV_W:2 * CONV_W] = (dv * a * sg * (1.0 - sg)).astype(BF16)
            return carry
        lax.fori_loop(0, n_chunks, taps_bwd, 0, unroll=2)

        @pl.when(pl.program_id(0) == NB - 1)
        def _():
            dw_ref[...] = jnp.zeros_like(dw_ref)
            dvec_ref[...] = jnp.zeros_like(dvec_ref)
            for j in range(CONV_K):
                dw_ref[j:j + 1, :] = jnp.sum(dw8[j], axis=0, keepdims=True)
            for j in range(3):
                dvec_ref[j:j + 1, :] = jnp.sum(dvec8[j], axis=0, keepdims=True)

    return _pallas(
        body, "conv_bwd", (NB,),
        [pl.BlockSpec((1, S, 2 * CONV_W), lambda b: (b, 0, 0)), pl.BlockSpec((1, S, CONV_W), lambda b: (b, 0, 0)),
         pl.BlockSpec((1, S, CONV_W), lambda b: (b, 0, 0)), _const((CONV_K, CONV_W)), _const((8, CONV_W))],
        [pl.BlockSpec((1, S, 2 * CONV_W), lambda b: (b, 0, 0)), pl.BlockSpec((32, CONV_W), lambda b: (0, 0)),
         pl.BlockSpec((8, CONV_W), lambda b: (0, 0))],
        [jax.ShapeDtypeStruct((NB, S, 2 * CONV_W), BF16), jax.ShapeDtypeStruct((32, CONV_W), F32),
         jax.ShapeDtypeStruct((8, CONV_W), F32)],
        (uc, z, dcact, dw_w, vec),
        scratch_shapes=[pltpu.VMEM((S + 2 * CONV_HALO, CONV_W), F32), pltpu.VMEM((S + 2 * CONV_HALO, CONV_W), F32),
                        pltpu.VMEM((CONV_K, 8, CONV_W), F32), pltpu.VMEM((3, 8, CONV_W), F32),
                        pltpu.VMEM((8, CONV_CHUNK + 24, CONV_W), F32)],
        sem=("arbitrary",), ride=ride)


def _rel_index_of_column(cols):
    offset = jnp.where(cols < KWIN, cols, cols - DS_LANES)
    return jnp.clip(KPAD - offset, -(CHUNK - 1), MAX_REL) + (CHUNK - 1)


def _bias_table(rel_bias):
    def body(rb_ref, o_ref, by_offset, first8):
        ridx = _rel_index_of_column(lax.broadcasted_iota(jnp.int32, (1, DS_LANES), 1))
        onehot = (ridx == lax.broadcasted_iota(jnp.int32, (N_REL, 1), 0)).astype(F32)
        by_offset[...] = jnp.dot(rb_ref[...], onehot, preferred_element_type=F32, precision=lax.Precision.HIGHEST)
        sub = lax.broadcasted_iota(jnp.int32, (8, 1), 0)
        kchunk = lax.broadcasted_iota(jnp.int32, (1, KWIN), 1) // CHUNK
        for head in range(ATT_HEADS):
            base = jnp.broadcast_to(by_offset[head:head + 1, :], (8, DS_LANES))
            rows = base
            for s in range(1, 8):
                rows = jnp.where(sub == s, pltpu.roll(base, s, 1), rows)
            first8[head] = rows

        def rows8(q8, carry):
            qchunk = (q8 * 8 + sub) // CHUNK
            band = (kchunk >= qchunk) & (kchunk <= qchunk + LEFT_CHUNKS)
            for head in range(ATT_HEADS):
                tile = pltpu.roll(first8[head], q8 * 8, 1)[:, 0:KWIN]
                o_ref[head, pl.ds(pl.multiple_of(q8 * 8, 8), 8), :] = jnp.where(band, tile, MASK_VALUE)
            return carry
        lax.fori_loop(0, QB // 8, rows8, 0)

    return pl.pallas_call(body, out_shape=jax.ShapeDtypeStruct((ATT_HEADS, QB, KWIN), F32), name="bias_table",
                          scratch_shapes=[pltpu.VMEM((ATT_HEADS, DS_LANES), F32), pltpu.VMEM((ATT_HEADS, 8, DS_LANES), F32)],
                          compiler_params=_params())(rel_bias)


def _load_keys(i, k_ref, v_ref, kpad, vpad, S):
    @pl.when(i == 0)
    def _():
        kpad[pl.ds(0, KPAD), :] = jnp.zeros((KPAD, ATT_W), BF16)
        vpad[pl.ds(0, KPAD), :] = jnp.zeros((KPAD, ATT_W), BF16)
        kpad[pl.ds(KPAD, S), :] = k_ref[0]
        vpad[pl.ds(KPAD, S), :] = v_ref[0]


def _att_scores(q2s, k2, tab_ref, head, in_head, in_seq):
    qm = jnp.where(in_head, q2s, jnp.zeros_like(q2s))
    return jnp.where(in_seq, _dot_nt(qm, k2) + tab_ref[head], MASK_VALUE)


def _scaled(q2):
    return q2 * jnp.asarray(ATT_HD ** -0.5, q2.dtype)


def _att_fwd(qkv, tab, ride=None):
    NB, S, _ = qkv.shape

    def body(q_ref, k_ref, v_ref, tab_ref, o_ref, lse_ref, kpad, vpad):
        i = pl.program_id(1)
        _load_keys(i, k_ref, v_ref, kpad, vpad, S)
        koff = pl.multiple_of(i * QB, QB)
        lane = lax.broadcasted_iota(jnp.int32, (1, 128), 1)
        in_seq = (lax.broadcasted_iota(jnp.int32, (1, KWIN), 1) + i * QB) >= KPAD
        lse = jnp.zeros((QB, 128), F32)
        for pair in range(ATT_HEADS // 2):
            cols = slice(pair * 128, (pair + 1) * 128)
            q2s = _scaled(q_ref[0, :, cols])
            k2 = kpad[pl.ds(koff, KWIN), cols]
            v2 = vpad[pl.ds(koff, KWIN), cols]
            o2 = jnp.zeros((QB, 128), F32)
            for hh in range(2):
                head = 2 * pair + hh
                in_head = (lane // ATT_HD) == hh
                s = _att_scores(q2s, k2, tab_ref, head, in_head, in_seq)
                m = jnp.max(s, axis=-1, keepdims=True)
                e = jnp.exp(s - m)
                l = jnp.sum(e, axis=-1, keepdims=True)
                p = e * (1.0 / l)
                o2 = jnp.where(in_head, _dot(p.astype(BF16), v2), o2)
                lse = jnp.where(lane == head, m + jnp.log(l), lse)
            o_ref[0, :, cols] = o2.astype(BF16)
        lse_ref[0] = lse

    seq = lambda col: pl.BlockSpec((1, S, ATT_W), lambda b, i: (b, 0, col), pipeline_mode=pl.Buffered(1))
    outs, rode = _pallas(
        body, "att_fwd", (NB, S // QB),
        [pl.BlockSpec((1, QB, ATT_W), lambda b, i: (b, i, 0)), seq(1), seq(2), _const((ATT_HEADS, QB, KWIN))],
        [pl.BlockSpec((1, QB, ATT_W), lambda b, i: (b, i, 0)), pl.BlockSpec((1, QB, 128), lambda b, i: (b, i, 0))],
        [jax.ShapeDtypeStruct((NB, S, ATT_W), BF16), jax.ShapeDtypeStruct((NB, S, 128), F32)],
        (qkv, qkv, qkv, tab),
        scratch_shapes=[pltpu.VMEM((S + KPAD, ATT_W), BF16), pltpu.VMEM((S + KPAD, ATT_W), BF16)],
        sem=("arbitrary", "arbitrary"), ride=ride)
    return outs[0], outs[1], rode


def _att_bwd(qkv, o, lse, do, tab, ride=None):
    NB, S, _ = qkv.shape
    nq = S // QB

    def body(q_ref, k_ref, v_ref, o_ref, lse_ref, do_ref, tab_ref, dqkv_ref, ds_hbm, kpad, vpad, dkpad, dvpad, ds_acc, ds_sem):
        b, i = pl.program_id(0), pl.program_id(1)
        _load_keys(i, k_ref, v_ref, kpad, vpad, S)

        @pl.when(i == 0)
        def _():
            dkpad[...] = jnp.zeros_like(dkpad)
            dvpad[...] = jnp.zeros_like(dvpad)

        @pl.when((i == 0) & (b == 0))
        def _():
            ds_acc[...] = jnp.zeros_like(ds_acc)

        koff = pl.multiple_of(i * QB, QB)
        lane = lax.broadcasted_iota(jnp.int32, (1, 128), 1)
        in_seq = (lax.broadcasted_iota(jnp.int32, (1, KWIN), 1) + i * QB) >= KPAD
        for pair in range(ATT_HEADS // 2):
            cols = slice(pair * 128, (pair + 1) * 128)
            q2s = _scaled(q_ref[0, :, cols])
            do2 = do_ref[0, :, cols]
            k2 = kpad[pl.ds(koff, KWIN), cols]
            v2 = vpad[pl.ds(koff, KWIN), cols]
            do_o = do2.astype(F32) * o_ref[0, :, cols].astype(F32)
            dq2 = jnp.zeros((QB, 128), F32)
            dk2 = jnp.zeros((KWIN, 128), F32)
            dv2 = jnp.zeros((KWIN, 128), F32)
            for hh in range(2):
                head = 2 * pair + hh
                in_head = (lane // ATT_HD) == hh
                p = jnp.exp(_att_scores(q2s, k2, tab_ref, head, in_head, in_seq) - lse_ref[0, :, head:head + 1])
                row_term = jnp.sum(jnp.where(in_head, do_o, 0.0), axis=-1, keepdims=True)
                dom = jnp.where(in_head, do2, jnp.zeros_like(do2))
                ds = p * (_dot_nt(dom, v2) - row_term)
                ds_acc[head] += ds
                dsb = ds.astype(BF16)
                dq2 = jnp.where(in_head, _dot(dsb, k2), dq2)
                dk2 = jnp.where(in_head, _dot_tn(dsb, q2s), dk2)
                dv2 = jnp.where(in_head, _dot_tn(p.astype(BF16), do2), dv2)
            dqkv_ref[0, pl.ds(koff, QB), cols] = (dq2 * (ATT_HD ** -0.5)).astype(BF16)
            dkpad[pl.ds(koff, KWIN), cols] += dk2
            dvpad[pl.ds(koff, KWIN), cols] += dv2

        @pl.when(i == nq - 1)
        def _():
            dqkv_ref[0, :, ATT_W:2 * ATT_W] = dkpad[pl.ds(KPAD, S), :].astype(BF16)
            dqkv_ref[0, :, 2 * ATT_W:3 * ATT_W] = dvpad[pl.ds(KPAD, S), :].astype(BF16)

        @pl.when((i == nq - 1) & (b == NB - 1))
        def _():
            out = pltpu.make_async_copy(ds_acc, ds_hbm, ds_sem)
            out.start()
            out.wait()

    seq = lambda col: pl.BlockSpec((1, S, ATT_W), lambda b, i: (b, 0, col), pipeline_mode=pl.Buffered(1))
    rows = pl.BlockSpec((1, QB, ATT_W), lambda b, i: (b, i, 0))
    return _pallas(
        body, "att_bwd", (NB, nq),
        [rows, seq(1), seq(2), rows, pl.BlockSpec((1, QB, 128), lambda b, i: (b, i, 0)), rows, _const((ATT_HEADS, QB, KWIN))],
        [pl.BlockSpec((1, S, 3 * ATT_W), lambda b, i: (b, 0, 0)), ANY],
        [jax.ShapeDtypeStruct((NB, S, 3 * ATT_W), BF16), jax.ShapeDtypeStruct((ATT_HEADS, QB, KWIN), F32)],
        (qkv, qkv, qkv, o, lse, do, tab),
        scratch_shapes=[pltpu.VMEM((S + KPAD, ATT_W), BF16), pltpu.VMEM((S + KPAD, ATT_W), BF16),
                        pltpu.VMEM((S + KPAD, ATT_W), F32), pltpu.VMEM((S + KPAD, ATT_W), F32),
                        pltpu.VMEM((ATT_HEADS, QB, KWIN), F32), pltpu.SemaphoreType.DMA],
        sem=("arbitrary", "arbitrary"), ride=ride)


def _rel_bias_grad(ds):
    def body(ds_ref, o_ref):
        sub = lax.broadcasted_iota(jnp.int32, (8, 1), 0)
        ridx = _rel_index_of_column(lax.broadcasted_iota(jnp.int32, (DS_LANES, 1), 0))
        onehot = (ridx == lax.broadcasted_iota(jnp.int32, (1, N_REL), 1)).astype(F32)
        def rows8(q8, accs):
            shift = lax.rem(DS_LANES - q8 * 8, DS_LANES)
            out = []
            for head in range(ATT_HEADS):
                tile = ds_ref[head, pl.ds(pl.multiple_of(q8 * 8, 8), 8), :]
                tile = jnp.concatenate([tile, jnp.zeros((8, DS_LANES - KWIN), F32)], axis=1)
                out.append(accs[head] + pltpu.roll(tile, shift, 1))
            return tuple(out)
        accs = lax.fori_loop(0, QB // 8, rows8, tuple(jnp.zeros((8, DS_LANES), F32) for _ in range(ATT_HEADS)))
        for head in range(ATT_HEADS):
            acc = accs[head]
            diag = jnp.zeros((8, DS_LANES), F32)
            for s in range(8):
                shifted = acc if s == 0 else pltpu.roll(acc, DS_LANES - s, 1)
                diag = jnp.where(sub == s, shifted, diag)
            z = jnp.sum(diag, axis=0, keepdims=True)
            o_ref[head:head + 1, :] = jnp.dot(z, onehot, preferred_element_type=F32, precision=lax.Precision.HIGHEST)

    return pl.pallas_call(body, out_shape=jax.ShapeDtypeStruct((ATT_HEADS, N_REL), F32), name="rel_bias_grad",
                          compiler_params=_params())(ds)


def _memkv_fwd(mem, g, w_kv, tm):
    R = mem.shape[0]
    tm = min(tm, R)

    def body(m_ref, g_ref, w_ref, h_ref, kv_ref):
        xhat, _ = _rms_stats(m_ref[...])
        h = (xhat * g_ref[...]).astype(BF16)
        h_ref[...] = h
        kv_ref[...] = _dot(h, w_ref[...]).astype(BF16)

    row = pl.BlockSpec((tm, D), lambda t: (t, 0))
    return pl.pallas_call(
        body, grid=(R // tm,), name="memkv_fwd", in_specs=[row, _const((1, D)), _const((D, 2 * MEM_W))], out_specs=[row, row],
        out_shape=[jax.ShapeDtypeStruct((R, D), BF16), jax.ShapeDtypeStruct((R, 2 * MEM_W), BF16)],
        compiler_params=_params("parallel"),
    )(mem, g, w_kv)


def _memkv_bwd(mem, dkv, w_kv, tm):
    R = mem.shape[0]
    tm = min(tm, R)

    def body(m_ref, dkv_ref, w_ref, dg_ref):
        xhat, _ = _rms_stats(m_ref[...])
        dh = _dot_nt(dkv_ref[...].astype(BF16), w_ref[...])

        @pl.when(pl.program_id(0) == 0)
        def _():
            dg_ref[...] = jnp.zeros_like(dg_ref)
        dg_ref[...] += jnp.sum(dh * xhat, axis=0, keepdims=True)

    row = pl.BlockSpec((tm, D), lambda t: (t, 0))
    return pl.pallas_call(
        body, grid=(R // tm,), name="memkv_bwd", in_specs=[row, row, _const((D, 2 * MEM_W))],
        out_specs=pl.BlockSpec((1, D), lambda t: (0, 0)), out_shape=jax.ShapeDtypeStruct((1, D), F32),
        compiler_params=_params("arbitrary"),
    )(mem, dkv, w_kv)


def _mem_probs(qh, kh):
    s = _dot_nt(qh, kh) * (MEM_HD ** -0.5)
    e = jnp.exp(s - jnp.max(s, axis=-1, keepdims=True))
    return e * (1.0 / jnp.sum(e, axis=-1, keepdims=True))


def _mematt_fwd(mq, kv, tq):
    NB, S, _ = mq.shape
    M = kv.shape[1]

    def body(q_ref, kv_ref, o_ref):
        for h in range(MEM_HEADS):
            cols = slice(h * MEM_HD, (h + 1) * MEM_HD)
            p = _mem_probs(q_ref[0, :, cols], kv_ref[0, :, cols])
            o_ref[0, :, cols] = _dot(p.astype(BF16), kv_ref[0, :, MEM_W + h * MEM_HD:MEM_W + (h + 1) * MEM_HD]).astype(BF16)

    return pl.pallas_call(
        body, grid=(NB, S // tq), name="mematt_fwd",
        in_specs=[pl.BlockSpec((1, tq, MEM_W), lambda b, i: (b, i, 0)), pl.BlockSpec((1, M, 2 * MEM_W), lambda b, i: (b, 0, 0))],
        out_specs=pl.BlockSpec((1, tq, MEM_W), lambda b, i: (b, i, 0)),
        out_shape=jax.ShapeDtypeStruct((NB, S, MEM_W), BF16), compiler_params=_params("parallel", "parallel"),
    )(mq, kv)


def _mematt_bwd(mq, kv, do, tq):
    NB, S, _ = mq.shape
    M = kv.shape[1]

    def body(q_ref, kv_ref, do_ref, dq_ref, dkv_ref):
        @pl.when(pl.program_id(1) == 0)
        def _():
            dkv_ref[...] = jnp.zeros_like(dkv_ref)
        for h in range(MEM_HEADS):
            cols = slice(h * MEM_HD, (h + 1) * MEM_HD)
            vcols = slice(MEM_W + h * MEM_HD, MEM_W + (h + 1) * MEM_HD)
            qh, kh, vh, doh = q_ref[0, :, cols], kv_ref[0, :, cols], kv_ref[0, :, vcols], do_ref[0, :, cols]
            p = _mem_probs(qh, kh)
            dp = _dot_nt(doh, vh)
            ds = p * (dp - jnp.sum(p * dp, axis=-1, keepdims=True))
            dss = (ds * (MEM_HD ** -0.5)).astype(BF16)
            dq_ref[0, :, cols] = _dot(dss, kh).astype(BF16)
            dkv_ref[0, :, cols] += _dot_tn(dss, qh)
            dkv_ref[0, :, vcols] += _dot_tn(p.astype(BF16), doh)

    qspec = pl.BlockSpec((1, tq, MEM_W), lambda b, i: (b, i, 0))
    kvspec = pl.BlockSpec((1, M, 2 * MEM_W), lambda b, i: (b, 0, 0))
    return pl.pallas_call(
        body, grid=(NB, S // tq), name="mematt_bwd", in_specs=[qspec, kvspec, qspec], out_specs=[qspec, kvspec],
        out_shape=[jax.ShapeDtypeStruct((NB, S, MEM_W), BF16), jax.ShapeDtypeStruct((NB, M, 2 * MEM_W), F32)],
        compiler_params=_params("arbitrary", "arbitrary"),
    )(mq, kv, do)


def _branch(j, in_ref, w_ref, gl_ref, bg_ref):
    y = _dot(in_ref[...], w_ref[...])
    gate = _sigmoid(gl_ref[:, j * D:(j + 1) * D].astype(F32) + bg_ref[:, j * D:(j + 1) * D])
    return y, gate


def _combine_fwd(x, cact, oatt, omem, gl, bg, wpw, wo, wmo, wout, tm):
    T = x.shape[0]

    def body(x_ref, c_ref, a_ref, m_ref, gl_ref, bg_ref, wpw_ref, wo_ref, wmo_ref, wout_ref, xo_ref, y_ref):
        y = None
        for j, (in_ref, w_ref) in enumerate(((c_ref, wpw_ref), (a_ref, wo_ref), (m_ref, wmo_ref))):
            yj, gate = _branch(j, in_ref, w_ref, gl_ref, bg_ref)
            y = gate * yj if y is None else y + gate * yj
        y = y.astype(BF16)
        y_ref[...] = y
        xo_ref[...] = x_ref[...] + _dot(y, wout_ref[...])

    row = lambda w: pl.BlockSpec((tm, w), lambda t: (t, 0))
    wbr = _const((512, D))
    return pl.pallas_call(
        body, grid=(T // tm,), name="combine_fwd",
        in_specs=[row(D), row(512), row(512), row(512), row(3 * D), _const((1, 3 * D)), wbr, wbr, wbr, _const((D, D))],
        out_specs=[row(D), row(D)],
        out_shape=[jax.ShapeDtypeStruct((T, D), F32), jax.ShapeDtypeStruct((T, D), BF16)],
        compiler_params=_params("parallel"),
    )(x, cact, oatt, omem, gl, bg, wpw, wo, wmo, wout)


def _combine_bwd(dx, cact, oatt, omem, gl, bg, wpw, wo, wmo, wout, tm, ride=None):
    T = dx.shape[0]

    def body(dx_ref, c_ref, a_ref, m_ref, gl_ref, bg_ref, wpw_ref, wo_ref, wmo_ref, wout_ref,
             dgl_ref, dc_ref, da_ref, dm_ref, dyc_ref, dya_ref, dym_ref, dbg_ref):
        dy = _dot_nt(dx_ref[...].astype(BF16), wout_ref[...])

        @pl.when(pl.program_id(0) == 0)
        def _():
            dbg_ref[...] = jnp.zeros_like(dbg_ref)
        branches = ((c_ref, wpw_ref, dyc_ref, dc_ref), (a_ref, wo_ref, dya_ref, da_ref), (m_ref, wmo_ref, dym_ref, dm_ref))
        for j, (in_ref, w_ref, dyb_ref, din_ref) in enumerate(branches):
            yj, gate = _branch(j, in_ref, w_ref, gl_ref, bg_ref)
            dyg = dy * gate
            dlogit = dyg * yj * (1.0 - gate)
            dgl_ref[:, j * D:(j + 1) * D] = dlogit.astype(BF16)
            dbg_ref[:, j * D:(j + 1) * D] += jnp.sum(dlogit, axis=0, keepdims=True)
            dyb = dyg.astype(BF16)
            dyb_ref[...] = dyb
            din_ref[...] = _dot_nt(dyb, w_ref[...]).astype(din_ref.dtype)

    row = lambda w: pl.BlockSpec((tm, w), lambda t: (t, 0))
    wbr = _const((512, D))
    sds = jax.ShapeDtypeStruct
    return _pallas(
        body, "combine_bwd", (T // tm,),
        [row(D), row(512), row(512), row(512), row(3 * D), _const((1, 3 * D)), wbr, wbr, wbr, _const((D, D))],
        [row(3 * D), row(512), row(512), row(512), row(D), row(D), row(D), pl.BlockSpec((1, 3 * D), lambda t: (0, 0))],
        [sds((T, 3 * D), BF16), sds((T, 512), F32), sds((T, 512), BF16), sds((T, 512), BF16),
         sds((T, D), BF16), sds((T, D), BF16), sds((T, D), BF16), sds((1, 3 * D), F32)],
        (dx, cact, oatt, omem, gl, bg, wpw, wo, wmo, wout), sem=("arbitrary",), ride=ride)


def _peer(x, y, c, rel):
    rx, ry, rc = (rel >> 2) & 1, (rel >> 1) & 1, rel & 1
    return ((1 - x) if rx else x, (1 - y) if ry else y, (1 - c) if rc else c)


def _all_sum_small(parts):
    n = len(parts)

    def body(*refs):
        p_refs, o_refs, slots = refs[:n], refs[n:2 * n], refs[2 * n:3 * n]
        send_sems, recv_sems = refs[3 * n:]
        x, y, c = _my_coords()
        me = _dev_index(x, y, c)

        def copy(i, rel, arrival):
            peer = _peer(x, y, c, rel)
            return pltpu.make_async_remote_copy(
                src_ref=p_refs[i], dst_ref=slots[i].at[_dev_index(*peer) if arrival else me],
                send_sem=send_sems.at[i, rel - 1], recv_sem=recv_sems.at[i, rel - 1], device_id=peer, device_id_type=MESH)

        for i in range(n):
            slots[i][me] = p_refs[i][...]
        for rel in range(1, NDEV):
            for i in range(n):
                copy(i, rel, False).start()
        for rel in range(1, NDEV):
            for i in range(n):
                copy(i, rel, True).wait_recv()
        for rel in range(1, NDEV):
            for i in range(n):
                copy(i, rel, False).wait_send()
        for i in range(n):
            total = slots[i][0]
            for d in range(1, NDEV):
                total = total + slots[i][d]
            o_refs[i][...] = total

    vmem = pl.BlockSpec(memory_space=pltpu.VMEM)
    return pl.pallas_call(
        body, out_shape=[jax.ShapeDtypeStruct(p.shape, F32) for p in parts], name="all_sum_small",
        in_specs=[vmem] * n, out_specs=[vmem] * n,
        scratch_shapes=[pltpu.VMEM((NDEV,) + p.shape, F32) for p in parts]
        + [pltpu.SemaphoreType.DMA((n, NDEV - 1)), pltpu.SemaphoreType.DMA((n, NDEV - 1))],
        compiler_params=pltpu.CompilerParams(has_side_effects=True),
    )(*parts)


HBM = pl.BlockSpec(memory_space=pltpu.HBM)
SEM = pl.BlockSpec(memory_space=pltpu.SEMAPHORE)


def _own_block(g, kind, m):
    def body(g_ref, land_ref, staged, sem):
        me = _dev_index(*_my_coords())
        for cp in (pltpu.make_async_copy(_window(g_ref, kind, m, me), staged, sem),
                   pltpu.make_async_copy(staged, land_ref.at[me], sem)):
            cp.start()
            cp.wait()

    block = (m, g.shape[1]) if kind == 'row' else (g.shape[0], m)
    return pl.pallas_call(body, in_specs=[ANY], out_specs=ANY, out_shape=jax.ShapeDtypeStruct((NDEV,) + block, g.dtype),
                          scratch_shapes=[pltpu.VMEM(block, g.dtype), pltpu.SemaphoreType.DMA], name="own_block_last")(g)


def _scatter_start(g, land, kind, m):
    def body(g_ref, land_ref, send_sems, recv_sems, g_thru, land_thru, token):
        x, y, c = _my_coords()
        me = _dev_index(x, y, c)
        for rel in range(1, NDEV):
            peer = _peer(x, y, c, rel)
            pltpu.make_async_remote_copy(src_ref=_window(g_ref, kind, m, _dev_index(*peer)), dst_ref=land_ref.at[me],
                                         send_sem=send_sems.at[rel - 1], recv_sem=recv_sems.at[rel - 1],
                                         device_id=peer, device_id_type=MESH).start()
        token[...] = jnp.zeros_like(token)

    return pl.pallas_call(
        body, name="scatter_last_start",
        out_shape=(pltpu.SemaphoreType.DMA((NDEV - 1,)), pltpu.SemaphoreType.DMA((NDEV - 1,)), pltpu.HBM(g.shape, g.dtype),
                   pltpu.HBM(land.shape, land.dtype), jax.ShapeDtypeStruct((8, 128), F32)),
        in_specs=(HBM, HBM), out_specs=(SEM, SEM, HBM, HBM, pl.BlockSpec(memory_space=pltpu.VMEM)),
        input_output_aliases={0: 2, 1: 3},
        compiler_params=pltpu.CompilerParams(has_side_effects=pltpu.SideEffectType.DATAFLOW_SIDE_EFFECTING),
    )(pltpu.with_memory_space_constraint(g, pltpu.HBM), pltpu.with_memory_space_constraint(land, pltpu.HBM))


def _scatter_wait(send_sems, recv_sems, g_thru, land_thru, after, kind, m):
    n_after = len(after)

    def body(*refs):
        g_ref, land_ref, send_sems, recv_sems = refs[:4]
        x, y, c = _my_coords()
        me = _dev_index(x, y, c)
        for rel in range(1, NDEV):
            peer = _peer(x, y, c, rel)
            dev = _dev_index(*peer)
            cp = pltpu.make_async_remote_copy(src_ref=_window(g_ref, kind, m, me), dst_ref=land_ref.at[dev],
                                              send_sem=send_sems.at[rel - 1], recv_sem=recv_sems.at[rel - 1],
                                              device_id=peer, device_id_type=MESH)
            cp.wait_send()
            cp.wait_recv()

    return pl.pallas_call(
        body, name="scatter_last_wait",
        out_shape=(pltpu.HBM(g_thru.shape, g_thru.dtype), pltpu.HBM(land_thru.shape, land_thru.dtype)),
        in_specs=(HBM, HBM, SEM, SEM) + (ANY,) * n_after, out_specs=(HBM, HBM), input_output_aliases={0: 0, 1: 1},
        compiler_params=pltpu.CompilerParams(has_side_effects=pltpu.SideEffectType.DATAFLOW_SIDE_EFFECTING),
    )(g_thru, land_thru, send_sems, recv_sems, *after)[1]


def _adamw_math(w, g, m, v):
    m = ADAM_B1 * m + (1.0 - ADAM_B1) * g
    v = ADAM_B2 * v + (1.0 - ADAM_B2) * (g * g)
    m_hat = m / (1.0 - ADAM_B1 ** ADAM_STEP)
    v_hat = v / (1.0 - ADAM_B2 ** ADAM_STEP)
    delta = -ADAM_LR * (m_hat / (jnp.sqrt(v_hat) + ADAM_EPS) + ADAM_WD * w)
    return delta, m, v


def _sum_adamw(parts, w, m, v, name, after=None):
    R, C = w.shape
    n_parts = len(parts)
    cg = C // n_parts
    tr = max(t for t in range(8, 257, 8) if R % t == 0)
    deps = [] if after is None else [after]

    def body(*refs):
        p_refs = refs[:n_parts]
        w_ref, m_ref, v_ref = refs[n_parts:n_parts + 3]
        g_ref, d_ref, mo_ref, vo_ref = refs[n_parts + 3 + len(deps):]
        for k, p_ref in enumerate(p_refs):
            @pl.when(pl.program_id(0) == k)
            def _():
                g = p_ref[0].astype(F32)
                for d in range(1, NDEV):
                    g = g + p_ref[d].astype(F32)
                g_ref[...] = g
                d_ref[...], mo_ref[...], vo_ref[...] = _adamw_math(w_ref[...], g, m_ref[...], v_ref[...])

    part = pl.BlockSpec((NDEV, tr, cg), lambda k, t: (0, t, 0))
    blk = pl.BlockSpec((tr, cg), lambda k, t: (t, k))
    return pl.pallas_call(
        body, grid=(n_parts, R // tr), name=name, in_specs=[part] * n_parts + [blk, blk, blk] + [ANY] * len(deps),
        out_specs=[blk] * 4, out_shape=[jax.ShapeDtypeStruct((R, C), F32)] * 4, compiler_params=_params("parallel", "parallel"),
    )(*parts, w, m, v, *deps)


def _adamw_small(ws, gs, ms, vs):
    n = len(ws)

    def body(*refs):
        w_refs, g_refs, m_refs, v_refs = (refs[k * n:(k + 1) * n] for k in range(4))
        d_refs, mo_refs, vo_refs = (refs[(4 + k) * n:(5 + k) * n] for k in range(3))
        for i in range(n):
            d_refs[i][...], mo_refs[i][...], vo_refs[i][...] = _adamw_math(w_refs[i][...], g_refs[i][...], m_refs[i][...], v_refs[i][...])

    shapes = [jax.ShapeDtypeStruct(a.shape, F32) for a in ws]
    outs = pl.pallas_call(body, out_shape=shapes * 3, name="adamw_small", compiler_params=_params())(*ws, *gs, *ms, *vs)
    return outs[:n], outs[n:2 * n], outs[2 * n:]


def kernel(x, mem, ffn1_norm, ffn1_w_up, ffn1_w_down, mix_norm, mem_norm, w_in, b_gate, conv_dw_w, conv_dw_b, conv_ln_g, conv_ln_b, conv_w_pw, att_rel_bias, att_w_o, mem_w_kv, mem_w_o, w_out, ffn2_norm, ffn2_w_up, ffn2_w_down, final_norm, loss_target, m_ffn1_norm, m_ffn1_w_up, m_ffn1_w_down, m_mix_norm, m_mem_norm, m_w_in, m_b_gate, m_conv_dw_w, m_conv_dw_b, m_conv_ln_g, m_conv_ln_b, m_conv_w_pw, m_att_rel_bias, m_att_w_o, m_mem_w_kv, m_mem_w_o, m_w_out, m_ffn2_norm, m_ffn2_w_up, m_ffn2_w_down, m_final_norm, v_ffn1_norm, v_ffn1_w_up, v_ffn1_w_down, v_mix_norm, v_mem_norm, v_w_in, v_b_gate, v_conv_dw_w, v_conv_dw_b, v_conv_ln_g, v_conv_ln_b, v_conv_w_pw, v_att_rel_bias, v_att_w_o, v_mem_w_kv, v_mem_w_o, v_w_out, v_ffn2_norm, v_ffn2_w_up, v_ffn2_w_down, v_final_norm):
    given = dict(locals())
    w = {n: given[n] for n in WEIGHTS}
    mom = {n: given["m_" + n] for n in WEIGHTS}
    var = {n: given["v_" + n] for n in WEIGHTS}

    NB, S, _ = x.shape
    T = NB * S
    ML = mem.shape[1]
    x0 = x.reshape(T, D)
    target = loss_target.reshape(T, D)
    mem2 = mem.reshape(NB * ML, D)

    def block(t, n):
        return jnp.transpose(t[0]) if n in TRANSPOSED else t[0]

    sh = dict(zip(BIG_ORDER, _cast_shards([block(w[n], n) for n in BIG_ORDER])))
    dw_t = jnp.transpose(conv_dw_w[0])

    def gather(names, extra=(), extra_kinds=()):
        return _gather_ride([sh[n] for n in names] + list(extra), [BIG[n] for n in names] + list(extra_kinds))

    W = {}
    names0 = ['ffn1_w_up', 'ffn1_w_down']
    got = _exchange_alone(gather(names0, [dw_t], [('row', dw_t.shape[0])]), "gather_ffn1")
    W.update(zip(names0, got[:2]))
    dw_full = jnp.transpose(got[2])
    conv_vec = jnp.concatenate([conv_dw_b, conv_ln_g, conv_ln_b, jnp.zeros((5, CONV_W), F32)], axis=0)
    tab = _bias_table(att_rel_bias[0])
    fin_g = final_norm.reshape(1, D)

    names1 = ['w_in', 'conv_w_pw', 'att_w_o', 'mem_w_kv', 'mem_w_o', 'w_out']
    (x1, ab1), got = _ffn_fwd(x0, ffn1_norm, W['ffn1_w_up'], W['ffn1_w_down'], TILE_FFN, "ffn1_fwd", ride=gather(names1))
    W.update(zip(names1, got))
    (uc, qkv, mq, gl, hmix), _ = _mix_fwd(x1, mix_norm, W['w_in'], TILE_TOKENS)
    uc3 = uc.reshape(NB, S, 2 * CONV_W)
    qkv3 = qkv.reshape(NB, S, 3 * ATT_W)
    mq3 = mq.reshape(NB, S, MEM_W)
    cact, conv_z = _conv_fwd(uc3, dw_full, conv_vec)
    cact = cact.reshape(T, CONV_W)
    names2 = ['ffn2_w_up', 'ffn2_w_down']
    oatt3, att_lse, got = _att_fwd(qkv3, tab, ride=gather(names2))
    W.update(zip(names2, got))
    oatt = oatt3.reshape(T, ATT_W)
    memh, kv = _memkv_fwd(mem2, mem_norm, W['mem_w_kv'], TILE_TOKENS)
    kv3 = kv.reshape(NB, ML, 2 * MEM_W)
    omem = _mematt_fwd(mq3, kv3, TILE_TOKENS).reshape(T, MEM_W)
    branch_w = (W['conv_w_pw'], W['att_w_o'], W['mem_w_o'], W['w_out'])
    x2, ymix = _combine_fwd(x1, cact, oatt, omem, gl, b_gate, *branch_w, TILE_COMBINE)
    dx3, ab2, loss_part, dg_final = _ffn_fwd_loss(x2, ffn2_norm, W['ffn2_w_up'], W['ffn2_w_down'], fin_g, target, TILE_FFN,
                                                  "ffn2_fwd_loss")

    def scatter(grads, names):
        return _scatter_ride(grads, [BIG[n] for n in names])

    G, P = {}, {}
    dx2, dab2, act2, h2, dg_ffn2 = _ffn_bwd(x2, dx3, ab2, ffn2_norm, W['ffn2_w_up'], W['ffn2_w_down'], TILE_FFN, "ffn2_bwd")
    g_up, _ = _tn_matmul(dab2, h2, 512, "grad_ffn2_w_up_a", tt=TILE_GRAD_TOKENS_WIDE, x_part=(0, 2), out_rows=2 * FF)
    G['ffn2_w_up'], _ = _tn_matmul(dab2, h2, 512, "grad_ffn2_w_up_b", tt=TILE_GRAD_TOKENS_WIDE, x_part=(1, 2), out_rows=2 * FF,
                                   prev=g_up)
    G['ffn2_w_down'], _ = _tn_matmul(act2, dx3, 512, "grad_ffn2_w_down", scale=0.5, tt=TILE_GRAD_TOKENS_WIDE)
    (dgl, dcact, doatt, domem, dyc, dya, dym, dbg), got = _combine_bwd(
        dx2, cact, oatt, omem, gl, b_gate, *branch_w, TILE_COMBINE, ride=scatter([G['ffn2_w_up']], ['ffn2_w_up']))
    P['ffn2_w_up'] = got
    G['w_out'], _ = _tn_matmul(ymix, dx2, 512, "grad_w_out")
    G['conv_w_pw'], _ = _tn_matmul(cact, dyc, 512, "grad_conv_w_pw")
    G['att_w_o'], _ = _tn_matmul(oatt, dya, 512, "grad_att_w_o")
    G['mem_w_o'], _ = _tn_matmul(omem, dym, 512, "grad_mem_w_o")
    dmq3, dkv3 = _mematt_bwd(mq3, kv3, domem.reshape(NB, S, MEM_W), TILE_TOKENS)
    dkv = dkv3.reshape(NB * ML, 2 * MEM_W)
    dg_mem = _memkv_bwd(mem2, dkv, W['mem_w_kv'], TILE_TOKENS)
    G['mem_w_kv'], _ = _tn_matmul(memh, dkv, 512, "grad_mem_w_kv")
    names = ['ffn2_w_down', 'w_out', 'conv_w_pw', 'att_w_o', 'mem_w_o']
    (dqkv3, dscore), got = _att_bwd(qkv3, oatt3, att_lse, doatt.reshape(NB, S, ATT_W), tab,
                                    ride=scatter([G[n] for n in names], names))
    P.update((n, [p]) for n, p in zip(names, got))
    d_rel = _rel_bias_grad(dscore)
    (duc3, d_dw, d_cvec), got = _conv_bwd(uc3, conv_z, dcact.reshape(NB, S, CONV_W), dw_full, conv_vec,
                                          ride=scatter([G['mem_w_kv']], ['mem_w_kv']))
    P['mem_w_kv'] = got
    duc, dqkv, dmq = duc3.reshape(T, 2 * CONV_W), dqkv3.reshape(T, 3 * ATT_W), dmq3.reshape(T, MEM_W)
    g_in, _ = _tn_matmul(hmix, duc, 512, "grad_w_in_conv", out_cols=IN_COLS, col_off=0)
    g_in, _ = _tn_matmul(hmix, dqkv, 512, "grad_w_in_qkv", out_cols=IN_COLS, col_off=1024, prev=g_in)
    g_in, _ = _tn_matmul(hmix, dmq, 512, "grad_w_in_mq", out_cols=IN_COLS, col_off=2560, prev=g_in)
    G['w_in'], _ = _tn_matmul(hmix, dgl, 1024, "grad_w_in_gate", out_cols=IN_COLS, col_off=3072, prev=g_in)
    (dx1, dg_mix), got = _mix_bwd(x1, dx2, duc, dqkv, dmq, dgl, mix_norm, W['w_in'], TILE_TOKENS,
                                  ride=scatter([G['w_in']], ['w_in']))
    P['w_in'] = got
    dx0, dab1, act1, h1, dg_ffn1 = _ffn_bwd(x0, dx1, ab1, ffn1_norm, W['ffn1_w_up'], W['ffn1_w_down'], TILE_FFN, "ffn1_bwd")
    g_wd1, _ = _tn_matmul(act1, dx1, 512, "grad_ffn1_w_down", scale=0.5, tt=TILE_GRAD_TOKENS_WIDE)
    g_wu1a, got = _tn_matmul(dab1, h1, 512, "grad_ffn1_w_up_a", tt=TILE_GRAD_TOKENS_WIDEST, y_part=(0, 2),
                             ride=scatter([g_wd1], ['ffn1_w_down']))
    P['ffn1_w_down'] = got
    g_wu1b, got_a = _tn_matmul(dab1, h1, 512, "grad_ffn1_w_up_b", tt=TILE_GRAD_TOKENS_WIDEST, y_part=(1, 2),
                               ride=scatter([g_wu1a], ['ffn1_w_up']))
    last_kind = BIG['ffn1_w_up']
    send_sems, recv_sems, g_thru, land_thru, token = _scatter_start(g_wu1b, _own_block(g_wu1b, *last_kind), *last_kind)

    small_names = ['loss', 'ffn1_norm', 'mix_norm', 'mem_norm', 'b_gate', 'conv_dw_w', 'conv_vec', 'att_rel_bias', 'ffn2_norm',
                   'final_norm']
    small = dict(zip(small_names, _all_sum_small(
        [loss_part + token[0:1], dg_ffn1, dg_mix, dg_mem, dbg, d_dw, d_cvec, d_rel, dg_ffn2, dg_final])))
    loss = small['loss'][0, 0]
    me = _dev_index(*_my_coords())
    for i, n in enumerate(['conv_dw_b', 'conv_ln_g', 'conv_ln_b']):
        small[n] = small['conv_vec'][i:i + 1]
    small['conv_dw_w'] = lax.dynamic_slice(small['conv_dw_w'], (0, me * conv_dw_w.shape[2]), (CONV_K, conv_dw_w.shape[2]))
    little = [n for n in WEIGHTS if n not in BIG]
    as2d = lambda t, n: t.reshape(small[n].shape)
    d_s, m_s, v_s = _adamw_small([as2d(w[n], n) for n in little], [small[n] for n in little],
                                 [as2d(mom[n], n) for n in little], [as2d(var[n], n) for n in little])
    grad, delta, new_m, new_v = {}, {}, {}, {}
    for i, n in enumerate(little):
        grad[n], delta[n], new_m[n], new_v[n] = (t.reshape(w[n].shape) for t in (small[n], d_s[i], m_s[i], v_s[i]))
    done = [d_s[0]]
    for n in BIG_ORDER[1:] + BIG_ORDER[:1]:
        if n == 'ffn1_w_up':
            P[n] = [got_a[0], _scatter_wait(send_sems, recv_sems, g_thru, land_thru, done, *last_kind)]
        outs = _sum_adamw(P[n], block(w[n], n), block(mom[n], n), block(var[n], n), "adamw_" + n,
                          after=None if n == 'ffn1_w_up' else token)
        done.append(outs[0])
        grad[n], delta[n], new_m[n], new_v[n] = ((jnp.transpose(t) if n in TRANSPOSED else t)[None] for t in outs)

    return (loss, dx0.reshape(NB, S, D), *[grad[n] for n in WEIGHTS], *[delta[n] for n in WEIGHTS],
            *[new_m[n] for n in WEIGHTS], *[new_v[n] for n in WEIGHTS])
```

```python
import functools

import jax
import jax.numpy as jnp
from jax import lax
from jax.experimental import pallas as pl
from jax.experimental.pallas import tpu as pltpu

F32 = jnp.float32
BF16 = jnp.bfloat16

EPS = 1e-6
MASK_VALUE = -1e30
D = 1024
NDEV = 8
FF = 2816
FF_SHARD = 704
FF_HALF_ROWS = 352
FF_BLOCK_EDGES = ()
IN_COLS = 6144
CONV_W = 512
CONV_K = 31
CONV_HALO = 32
CONV_CHUNK = 32
CONV_WIN = CONV_CHUNK + 40
GLU_CHUNK = 128
ATT_W = 512
ATT_HEADS = 8
ATT_HD = 64
CHUNK = 64
LEFT_CHUNKS = 8
MAX_REL = 128
N_REL = 192
QB = 256
KWIN = QB + LEFT_CHUNKS * CHUNK
KPAD = LEFT_CHUNKS * CHUNK
DS_LANES = 1024
MEM_W = 512
MEM_HEADS = 4
MEM_HD = 128
ADAM_LR = 0.001
ADAM_B1 = 0.9
ADAM_B2 = 0.999
ADAM_EPS = 1e-08
ADAM_WD = 0.01
ADAM_STEP = 10
VMEM_LIMIT = 60 * 1024 * 1024
TILE_FFN = 256
TILE_COMBINE = 256
TILE_TOKENS = 512
TILE_GRAD_TOKENS = 2048
TILE_GRAD_TOKENS_WIDE = 1024
TILE_GRAD_TOKENS_WIDEST = 512

MESH = pl.DeviceIdType.MESH
ANY = pl.BlockSpec(memory_space=pl.ANY)

WEIGHTS = ['ffn1_norm', 'ffn1_w_up', 'ffn1_w_down', 'mix_norm', 'mem_norm', 'w_in', 'b_gate', 'conv_dw_w', 'conv_dw_b',
           'conv_ln_g', 'conv_ln_b', 'conv_w_pw', 'att_rel_bias', 'att_w_o', 'mem_w_kv', 'mem_w_o', 'w_out', 'ffn2_norm',
           'ffn2_w_up', 'ffn2_w_down', 'final_norm']
BIG = {
    'ffn1_w_up': ('row', FF_SHARD), 'ffn1_w_down': ('row', FF_HALF_ROWS), 'w_in': ('col', 768),
    'conv_w_pw': ('col', 128), 'att_w_o': ('col', 128), 'mem_w_kv': ('row', 128), 'mem_w_o': ('col', 128),
    'w_out': ('row', 128), 'ffn2_w_up': ('row', FF_SHARD), 'ffn2_w_down': ('row', FF_HALF_ROWS),
}
BIG_ORDER = ['ffn1_w_up', 'ffn1_w_down', 'w_in', 'conv_w_pw', 'att_w_o', 'mem_w_kv', 'mem_w_o', 'w_out', 'ffn2_w_up', 'ffn2_w_down']
TRANSPOSED = ('ffn1_w_up', 'ffn2_w_up')


def _dot(a, b):
    return jnp.dot(a, b, preferred_element_type=F32)


def _dot_nt(a, b):
    return lax.dot_general(a, b, (((1,), (1,)), ((), ())), preferred_element_type=F32)


def _dot_tn(a, b):
    return lax.dot_general(a, b, (((0,), (0,)), ((), ())), preferred_element_type=F32)


def _sigmoid(v):
    return jax.nn.sigmoid(v)


def _const(shape):
    return pl.BlockSpec(shape, lambda *_: (0,) * len(shape), pipeline_mode=pl.Buffered(1))


def _params(*sem):
    return pltpu.CompilerParams(dimension_semantics=sem if sem else None, vmem_limit_bytes=VMEM_LIMIT)


def _my_coords():
    return lax.axis_index("x"), lax.axis_index("y"), lax.axis_index("c")


def _dev_index(px, py, pc):
    return 4 * px + 2 * py + pc


def _window(ref, kind, n, p):
    if kind == 'row':
        return ref.at[pl.ds(pl.multiple_of(p * n, n), n), :]
    return ref.at[:, pl.ds(pl.multiple_of(p * n, 128), n)]


def _full_shape(kind, n, shard_shape):
    if kind == 'row':
        return (NDEV * n, shard_shape[1])
    return (shard_shape[0], NDEV * n)


def _cast_shards(shards):
    n = len(shards)

    def body(*refs):
        for i in range(n):
            refs[n + i][...] = refs[i][...].astype(BF16)

    out_shape = [jax.ShapeDtypeStruct(s.shape, BF16) for s in shards]
    return pl.pallas_call(body, out_shape=out_shape, name="cast_shards", compiler_params=_params())(*shards)


class _Ride:
    def __init__(self, inputs, out_shape, scratch, start, finish, mid=None):
        self.inputs, self.out_shape, self.scratch = list(inputs), list(out_shape), list(scratch)
        self.start, self.finish, self.mid = start, finish, mid


def _pallas(body, name, grid, in_specs, out_specs, out_shape, args, scratch_shapes=(), sem=None, aliases=None, ride=None,
            after=None):
    if ride is None:
        n_in, n_dep = len(args), 0 if after is None else 1

        def kernel_body(*refs):
            body(*refs[:n_in], *refs[n_in + n_dep:])

        outs = pl.pallas_call(kernel_body if n_dep else body, grid=grid, name=name, in_specs=list(in_specs) + [ANY] * n_dep,
                              out_specs=out_specs, out_shape=out_shape, scratch_shapes=list(scratch_shapes),
                              input_output_aliases=aliases or {}, compiler_params=_params(*sem),
                              )(*args, *([after] if n_dep else []))
        return list(outs), []
    n_in, n_out, n_scr = len(args), len(out_shape), len(scratch_shapes)
    r_in, r_out = len(ride.inputs), len(ride.out_shape)

    def wrapped(*refs):
        k_in, rin = refs[:n_in], refs[n_in:n_in + r_in]
        o0 = n_in + r_in
        k_out, rout = refs[o0:o0 + n_out], refs[o0 + n_out:o0 + n_out + r_out]
        s0 = o0 + n_out + r_out
        k_scr, rscr = refs[s0:s0 + n_scr], refs[s0 + n_scr:]
        ids = [pl.program_id(k) for k in range(len(grid))]
        first = functools.reduce(jnp.logical_and, [i == 0 for i in ids])
        last = functools.reduce(jnp.logical_and, [i == g - 1 for i, g in zip(ids, grid)])
        pl.when(first)(lambda: ride.start(rin, rout, rscr))
        if ride.mid is not None:
            at_mid = functools.reduce(jnp.logical_and, [ids[0] == (3 * grid[0]) // 4] + [i == 0 for i in ids[1:]])
            pl.when(at_mid)(lambda: ride.mid(rin, rout, rscr))
        body(*k_in, *k_out, *k_scr)
        pl.when(last)(lambda: ride.finish(rin, rout, rscr))

    outs = pl.pallas_call(
        wrapped, grid=grid, name=name, in_specs=list(in_specs) + [ANY] * r_in, out_specs=list(out_specs) + [ANY] * r_out,
        out_shape=list(out_shape) + ride.out_shape, scratch_shapes=list(scratch_shapes) + ride.scratch,
        input_output_aliases=aliases or {}, compiler_params=_params(*(["arbitrary"] * len(grid))),
    )(*args, *ride.inputs)
    return list(outs[:n_out]), list(outs[n_out:])


def _exchange_alone(ride, name):
    r_in, r_out = len(ride.inputs), len(ride.out_shape)

    def body(*refs):
        rin, rout, rscr = refs[:r_in], refs[r_in:r_in + r_out], refs[r_in + r_out:]
        ride.start(rin, rout, rscr)
        if ride.mid is not None:
            ride.mid(rin, rout, rscr)
        ride.finish(rin, rout, rscr)

    return pl.pallas_call(body, out_shape=ride.out_shape, in_specs=[ANY] * r_in, out_specs=[ANY] * r_out, name=name,
                          scratch_shapes=ride.scratch, compiler_params=pltpu.CompilerParams(has_side_effects=True))(*ride.inputs)


def _gather_ride(shards, kinds):
    n = len(shards)

    def plan(rin, out, sems):
        send_sems, recv_sems, local_sems = sems[:3]
        x, y, c = _my_coords()
        me, sibling = (x, y, c), (x, y, 1 - c)
        chips = [(1 - x, y), (x, 1 - y), (1 - x, 1 - y)]

        def win(i, dev):
            return _window(out[i], kinds[i][0], kinds[i][1], _dev_index(*dev))

        def copy(i, k, block, to, from_shard=False):
            return pltpu.make_async_remote_copy(
                src_ref=rin[i] if from_shard else win(i, block), dst_ref=win(i, block),
                send_sem=send_sems.at[i, k], recv_sem=recv_sems.at[i, k], device_id=to, device_id_type=MESH)

        def local():
            return [pltpu.make_async_copy(rin[i], win(i, me), local_sems.at[i]) for i in range(n)]

        def first():
            cps = []
            for i in range(n):
                cps.append(copy(i, 0, me, sibling, from_shard=True))
                cps += [copy(i, 1 + j, me, (*chip, c), from_shard=True) for j, chip in enumerate(chips)]
            return cps

        def arrived():
            return [copy(i, 1 + j, (*chip, c), me) for j, chip in enumerate(chips) for i in range(n)]

        def passed():
            return [copy(i, 4 + j, (*chip, c), sibling) for j, chip in enumerate(chips) for i in range(n)]

        def from_sibling():
            cps = [copy(i, 0, sibling, me) for i in range(n)]
            return cps + [copy(i, 4 + j, (*chip, 1 - c), me) for i in range(n) for j, chip in enumerate(chips)]

        return local, first, arrived, passed, from_sibling

    def start(rin, out, sems):
        local, first, _, _, _ = plan(rin, out, sems)
        for cp in local() + first():
            cp.start()

    def mid(rin, out, sems):
        _, _, arrived, passed, _ = plan(rin, out, sems)
        for got, fwd in zip(arrived(), passed()):
            got.wait_recv()
            fwd.start()

    def finish(rin, out, sems):
        local, first, _, passed, from_sibling = plan(rin, out, sems)
        for cp in from_sibling():
            cp.wait_recv()
        for cp in first() + passed():
            cp.wait_send()
        for cp in local():
            cp.wait()

    out_shape = [jax.ShapeDtypeStruct(_full_shape(k, m, s.shape), s.dtype) for s, (k, m) in zip(shards, kinds)]
    scratch = [pltpu.SemaphoreType.DMA((n, 7)), pltpu.SemaphoreType.DMA((n, 7)), pltpu.SemaphoreType.DMA((n,))]
    return _Ride(shards, out_shape, scratch, start, finish, mid)


def _scatter_ride(grads, kinds):
    n = len(grads)

    def plan(g, out, sems):
        send_sems, recv_sems, local_sems = sems
        x, y, c = _my_coords()
        me = _dev_index(x, y, c)

        def local():
            return [pltpu.make_async_copy(_window(g[i], kinds[i][0], kinds[i][1], me), out[i].at[me], local_sems.at[i])
                    for i in range(n)]

        def remote(arrival):
            cps = []
            for rel in range(1, NDEV):
                peer = _peer(x, y, c, rel)
                dev = _dev_index(*peer)
                for i in range(n):
                    kind, m = kinds[i]
                    cps.append(pltpu.make_async_remote_copy(
                        src_ref=_window(g[i], kind, m, me if arrival else dev), dst_ref=out[i].at[dev if arrival else me],
                        send_sem=send_sems.at[i, rel - 1], recv_sem=recv_sems.at[i, rel - 1], device_id=peer, device_id_type=MESH))
            return cps

        return local, remote

    def start(g, out, sems):
        local, remote = plan(g, out, sems)
        for cp in local() + remote(False):
            cp.start()

    def finish(g, out, sems):
        local, remote = plan(g, out, sems)
        for cp in remote(True):
            cp.wait_recv()
        for cp in remote(False):
            cp.wait_send()
        for cp in local():
            cp.wait()

    def block_shape(gr, kind, m):
        return (m, gr.shape[1]) if kind == 'row' else (gr.shape[0], m)

    out_shape = [jax.ShapeDtypeStruct((NDEV,) + block_shape(gr, k, m), gr.dtype) for gr, (k, m) in zip(grads, kinds)]
    scratch = [pltpu.SemaphoreType.DMA((n, NDEV - 1)), pltpu.SemaphoreType.DMA((n, NDEV - 1)), pltpu.SemaphoreType.DMA((n,))]
    return _Ride(grads, out_shape, scratch, start, finish)


def _rms_stats(xf):
    r = lax.rsqrt(jnp.mean(xf * xf, axis=-1, keepdims=True) + EPS)
    return xf * r, r


def _rms_bwd(dh, g, xhat, r):
    dxhat = dh * g
    return r * (dxhat - xhat * jnp.mean(dxhat * xhat, axis=-1, keepdims=True))


def _ffn_blocks():
    edges = (0,) + FF_BLOCK_EDGES + (FF,)
    return [(slice(lo, hi), slice(FF + lo, FF + hi)) for lo, hi in zip(edges[:-1], edges[1:])]


def _swiglu_tile(x_ref, g_ref, wut_ref, wd_ref, ab_ref):
    xf = x_ref[...]
    xhat, _ = _rms_stats(xf)
    h = (xhat * g_ref[...]).astype(BF16)
    acc = jnp.zeros(xf.shape, F32)
    for ra, rb in _ffn_blocks():
        a = _dot_nt(h, wut_ref[ra, :])
        b = _dot_nt(h, wut_ref[rb, :])
        ab_ref[:, ra] = a.astype(BF16)
        ab_ref[:, rb] = b.astype(BF16)
        act = (a * _sigmoid(a) * b).astype(BF16)
        acc = acc + _dot(act, wd_ref[ra, :])
    return xf + 0.5 * acc


def _ffn_fwd(x, g, wut, wd, tm, name, ride=None):
    T = x.shape[0]

    def body(x_ref, g_ref, wut_ref, wd_ref, xo_ref, ab_ref):
        xo_ref[...] = _swiglu_tile(x_ref, g_ref, wut_ref, wd_ref, ab_ref)

    return _pallas(
        body, name, (T // tm,),
        [pl.BlockSpec((tm, D), lambda t: (t, 0)), _const((1, D)), _const((2 * FF, D)), _const((FF, D))],
        [pl.BlockSpec((tm, D), lambda t: (t, 0)), pl.BlockSpec((tm, 2 * FF), lambda t: (t, 0))],
        [jax.ShapeDtypeStruct((T, D), F32), jax.ShapeDtypeStruct((T, 2 * FF), BF16)],
        (x, g, wut, wd), sem=("arbitrary",), ride=ride)


def _ffn_fwd_loss(x, g, wut, wd, g_final, target, tm, name):
    T = x.shape[0]

    def body(x_ref, g_ref, wut_ref, wd_ref, gf_ref, t_ref, dx_ref, ab_ref, loss_ref, dgf_ref):
        xhat, r = _rms_stats(_swiglu_tile(x_ref, g_ref, wut_ref, wd_ref, ab_ref))
        gain = gf_ref[...]
        diff = xhat * gain - t_ref[...]
        dout = diff * (1.0 / D)

        @pl.when(pl.program_id(0) == 0)
        def _():
            loss_ref[...] = jnp.zeros_like(loss_ref)
            dgf_ref[...] = jnp.zeros_like(dgf_ref)
        sq = jnp.sum(jnp.sum(diff * diff, axis=0, keepdims=True), axis=1, keepdims=True)
        loss_ref[...] += jnp.broadcast_to(sq * (0.5 / D), (1, 128))
        dgf_ref[...] += jnp.sum(dout * xhat, axis=0, keepdims=True)
        dx_ref[...] = _rms_bwd(dout, gain, xhat, r)

    row = pl.BlockSpec((tm, D), lambda t: (t, 0))
    return pl.pallas_call(
        body, grid=(T // tm,), name=name,
        in_specs=[row, _const((1, D)), _const((2 * FF, D)), _const((FF, D)), _const((1, D)), row],
        out_specs=[row, pl.BlockSpec((tm, 2 * FF), lambda t: (t, 0)), pl.BlockSpec((1, 128), lambda t: (0, 0)),
                   pl.BlockSpec((1, D), lambda t: (0, 0))],
        out_shape=[jax.ShapeDtypeStruct((T, D), F32), jax.ShapeDtypeStruct((T, 2 * FF), BF16),
                   jax.ShapeDtypeStruct((1, 128), F32), jax.ShapeDtypeStruct((1, D), F32)],
        compiler_params=_params("arbitrary"),
    )(x, g, wut, wd, g_final, target)


def _ffn_bwd(x, dy, ab, g, wut, wd, tm, name):
    T = x.shape[0]

    def body(x_ref, dy_ref, ab_ref, g_ref, wut_ref, wd_ref, dx_ref, dab_ref, act_ref, h_ref, dg_ref):
        xf = x_ref[...]
        xhat, r = _rms_stats(xf)
        gain = g_ref[...]
        h_ref[...] = (xhat * gain).astype(BF16)
        dy = dy_ref[...]
        dyh = (0.5 * dy).astype(BF16)
        dh = jnp.zeros((tm, D), F32)
        for ra, rb in _ffn_blocks():
            a = ab_ref[:, ra].astype(F32)
            b = ab_ref[:, rb].astype(F32)
            dact = _dot_nt(dyh, wd_ref[ra, :])
            sg = _sigmoid(a)
            sl = a * sg
            act_ref[:, ra] = (sl * b).astype(BF16)
            da = (dact * b * (sg * (1.0 + a * (1.0 - sg)))).astype(BF16)
            db = (dact * sl).astype(BF16)
            dab_ref[:, ra] = da
            dab_ref[:, rb] = db
            dh = dh + _dot(da, wut_ref[ra, :]) + _dot(db, wut_ref[rb, :])
        dx_ref[...] = dy + _rms_bwd(dh, gain, xhat, r)

        @pl.when(pl.program_id(0) == 0)
        def _():
            dg_ref[...] = jnp.zeros_like(dg_ref)
        dg_ref[...] += jnp.sum(dh * xhat, axis=0, keepdims=True)

    return pl.pallas_call(
        body, grid=(T // tm,), name=name,
        in_specs=[pl.BlockSpec((tm, D), lambda t: (t, 0)), pl.BlockSpec((tm, D), lambda t: (t, 0)),
                  pl.BlockSpec((tm, 2 * FF), lambda t: (t, 0)), _const((1, D)), _const((2 * FF, D)), _const((FF, D))],
        out_specs=[pl.BlockSpec((tm, D), lambda t: (t, 0)), pl.BlockSpec((tm, 2 * FF), lambda t: (t, 0)),
                   pl.BlockSpec((tm, FF), lambda t: (t, 0)), pl.BlockSpec((tm, D), lambda t: (t, 0)),
                   pl.BlockSpec((1, D), lambda t: (0, 0))],
        out_shape=[jax.ShapeDtypeStruct((T, D), F32), jax.ShapeDtypeStruct((T, 2 * FF), BF16),
                   jax.ShapeDtypeStruct((T, FF), BF16), jax.ShapeDtypeStruct((T, D), BF16), jax.ShapeDtypeStruct((1, D), F32)],
        compiler_params=_params("arbitrary"),
    )(x, dy, ab, g, wut, wd)


def _tn_matmul(xm, ym, tn, name, scale=None, out_cols=None, col_off=0, prev=None, tt=TILE_GRAD_TOKENS, x_part=(0, 1),
               out_rows=None, y_part=(0, 1), ride=None, after=None):
    T = xm.shape[0]
    xi, xn = x_part
    yi, yn = y_part
    K = xm.shape[1] // xn
    N = ym.shape[1] // yn
    out_cols = N if out_cols is None else out_cols
    row_blk = xi if out_rows is not None else 0
    out_rows = K if out_rows is None else out_rows
    tt = min(tt, T)
    nt = T // tt
    off = col_off // tn

    def body(*refs):
        x_ref, y_ref = refs[0], refs[1]
        o_ref, acc = refs[-2], refs[-1]

        @pl.when(pl.program_id(1) == 0)
        def _():
            acc[...] = jnp.zeros_like(acc)
        acc[...] += _dot_tn(x_ref[...].astype(BF16), y_ref[...].astype(BF16))

        @pl.when(pl.program_id(1) == nt - 1)
        def _():
            res = acc[...]
            o_ref[...] = (res if scale is None else res * scale).astype(BF16)

    ycol = yi * (N // tn)
    in_specs = [pl.BlockSpec((tt, K), lambda n, t: (t, xi)), pl.BlockSpec((tt, tn), lambda n, t: (t, n + ycol))]
    args = [xm, ym]
    aliases = {}
    if prev is not None:
        in_specs.append(ANY)
        args.append(prev)
        aliases = {2: 0}
    outs, rode = _pallas(
        body, name, (N // tn, nt), in_specs, [pl.BlockSpec((K, tn), lambda n, t: (row_blk, n + off))],
        [jax.ShapeDtypeStruct((out_rows, out_cols), BF16)], args, scratch_shapes=[pltpu.VMEM((K, tn), F32)],
        sem=("parallel", "arbitrary"), aliases=aliases, ride=ride, after=after)
    return outs[0], rode


def _mix_fwd(x, g, w_in, tm, ride=None):
    T = x.shape[0]

    def body(x_ref, g_ref, w_ref, uc_ref, qkv_ref, mq_ref, gl_ref, h_ref):
        xhat, _ = _rms_stats(x_ref[...])
        h = (xhat * g_ref[...]).astype(BF16)
        h_ref[...] = h
        uc_ref[...] = _dot(h, w_ref[:, 0:1024])
        qkv_ref[...] = _dot(h, w_ref[:, 1024:2560]).astype(BF16)
        mq_ref[...] = _dot(h, w_ref[:, 2560:3072]).astype(BF16)
        for j in range(3):
            gl_ref[:, j * D:(j + 1) * D] = _dot(h, w_ref[:, 3072 + j * D:3072 + (j + 1) * D]).astype(BF16)

    row = lambda w: pl.BlockSpec((tm, w), lambda t: (t, 0))
    return _pallas(
        body, "mix_fwd", (T // tm,), [row(D), _const((1, D)), _const((D, IN_COLS))],
        [row(1024), row(1536), row(512), row(3072), row(D)],
        [jax.ShapeDtypeStruct((T, 1024), F32), jax.ShapeDtypeStruct((T, 1536), BF16), jax.ShapeDtypeStruct((T, 512), BF16),
         jax.ShapeDtypeStruct((T, 3072), BF16), jax.ShapeDtypeStruct((T, D), BF16)],
        (x, g, w_in), sem=("parallel",), ride=ride)


def _mix_bwd(x, dres, duc, dqkv, dmq, dgl, g, w_in, tm, ride=None, after=None):
    T = x.shape[0]

    def body(x_ref, dres_ref, duc_ref, dqkv_ref, dmq_ref, dgl_ref, g_ref, w_ref, dx_ref, dg_ref):
        xhat, r = _rms_stats(x_ref[...])
        dh = _dot_nt(duc_ref[...], w_ref[:, 0:1024])
        dh = dh + _dot_nt(dqkv_ref[...], w_ref[:, 1024:2560])
        dh = dh + _dot_nt(dmq_ref[...], w_ref[:, 2560:3072])
        dh = dh + _dot_nt(dgl_ref[...], w_ref[:, 3072:6144])
        dx_ref[...] = dres_ref[...] + _rms_bwd(dh, g_ref[...], xhat, r)

        @pl.when(pl.program_id(0) == 0)
        def _():
            dg_ref[...] = jnp.zeros_like(dg_ref)
        dg_ref[...] += jnp.sum(dh * xhat, axis=0, keepdims=True)

    row = lambda w: pl.BlockSpec((tm, w), lambda t: (t, 0))
    return _pallas(
        body, "mix_bwd", (T // tm,),
        [row(D), row(D), row(1024), row(1536), row(512), row(3072), _const((1, D)), _const((D, IN_COLS))],
        [row(D), pl.BlockSpec((1, D), lambda t: (0, 0))],
        [jax.ShapeDtypeStruct((T, D), F32), jax.ShapeDtypeStruct((1, D), F32)],
        (x, dres, duc, dqkv, dmq, dgl, g, w_in), sem=("arbitrary",), ride=ride, after=after)


def _shifted(win, base, copies):
    for k in range(8):
        copies[k] = win[base + k:base + k + CONV_CHUNK + 24]
    return copies


def _tap_slices(copies, tap):
    out = []
    for k in range(8):
        for a in range(4):
            j = tap(a, k)
            if 0 <= j < CONV_K:
                out.append((j, copies[k, pl.ds(8 * a, CONV_CHUNK), :]))
    return out


def _conv_taps(copies, w_ref, tap):
    acc = jnp.zeros((CONV_CHUNK, CONV_W), F32)
    for j, rows in _tap_slices(copies, tap):
        acc = acc + rows * w_ref[j:j + 1, :]
    return acc


def _fold8(v):
    acc = v[0:8]
    for r in range(8, CONV_CHUNK, 8):
        acc = acc + v[r:r + 8]
    return acc


def _glu_into(uc_ref, vpad, S):
    vpad[pl.ds(0, CONV_HALO), :] = jnp.zeros((CONV_HALO, CONV_W), F32)
    vpad[pl.ds(S + CONV_HALO, CONV_HALO), :] = jnp.zeros((CONV_HALO, CONV_W), F32)

    def glu(i, carry):
        r0 = pl.multiple_of(i * GLU_CHUNK, GLU_CHUNK)
        a = uc_ref[0, pl.ds(r0, GLU_CHUNK), 0:CONV_W]
        gt = uc_ref[0, pl.ds(r0, GLU_CHUNK), CONV_W:2 * CONV_W]
        vpad[pl.ds(pl.multiple_of(r0 + CONV_HALO, CONV_HALO), GLU_CHUNK), :] = a * _sigmoid(gt)
        return carry
    lax.fori_loop(0, S // GLU_CHUNK, glu, 0)


def _layer_norm(z, vec_ref):
    xc = z - jnp.mean(z, axis=-1, keepdims=True)
    rstd = lax.rsqrt(jnp.mean(xc * xc, axis=-1, keepdims=True) + EPS)
    xn = xc * rstd
    return xn, rstd, xn * vec_ref[1:2, :] + vec_ref[2:3, :]


def _conv_fwd(uc, dw_w, vec):
    NB, S, _ = uc.shape

    def body(uc_ref, w_ref, vec_ref, o_ref, z_ref, vpad, copies):
        _glu_into(uc_ref, vpad, S)

        def conv(i, carry):
            r0 = pl.multiple_of(i * CONV_CHUNK, CONV_CHUNK)
            win = vpad[pl.ds(r0, CONV_WIN), :]
            z = _conv_taps(_shifted(win, CONV_HALO - (CONV_K - 1), copies), w_ref, lambda a, k: 8 * a + k) + vec_ref[0:1, :]
            z_ref[0, pl.ds(r0, CONV_CHUNK), :] = z
            _, _, yln = _layer_norm(z, vec_ref)
            o_ref[0, pl.ds(r0, CONV_CHUNK), :] = (yln * _sigmoid(yln)).astype(BF16)
            return carry
        lax.fori_loop(0, S // CONV_CHUNK, conv, 0, unroll=2)

    seq = pl.BlockSpec((1, S, CONV_W), lambda b: (b, 0, 0))
    return pl.pallas_call(
        body, grid=(NB,), name="conv_fwd",
        in_specs=[pl.BlockSpec((1, S, 2 * CONV_W), lambda b: (b, 0, 0)), _const((CONV_K, CONV_W)), _const((8, CONV_W))],
        out_specs=[seq, seq],
        out_shape=[jax.ShapeDtypeStruct((NB, S, CONV_W), BF16), jax.ShapeDtypeStruct((NB, S, CONV_W), F32)],
        scratch_shapes=[pltpu.VMEM((S + 2 * CONV_HALO, CONV_W), F32), pltpu.VMEM((8, CONV_CHUNK + 24, CONV_W), F32)],
        compiler_params=_params("parallel"),
    )(uc, dw_w, vec)


def _conv_bwd(uc, z, dcact, dw_w, vec, ride=None):
    NB, S, _ = uc.shape
    n_chunks = S // CONV_CHUNK

    def body(uc_ref, z_ref, dc_ref, w_ref, vec_ref, duc_ref, dw_ref, dvec_ref, vpad, dzpad, dw8, dvec8, copies):
        @pl.when(pl.program_id(0) == 0)
        def _():
            dw8[...] = jnp.zeros_like(dw8)
            dvec8[...] = jnp.zeros_like(dvec8)
        _glu_into(uc_ref, vpad, S)
        dzpad[pl.ds(S, 2 * CONV_HALO), :] = jnp.zeros((2 * CONV_HALO, CONV_W), F32)

        def norm_bwd(i, carry):
            r0 = pl.multiple_of(i * CONV_CHUNK, CONV_CHUNK)
            xn, rstd, yln = _layer_norm(z_ref[0, pl.ds(r0, CONV_CHUNK), :], vec_ref)
            sg = _sigmoid(yln)
            dyln = dc_ref[0, pl.ds(r0, CONV_CHUNK), :] * (sg * (1.0 + yln * (1.0 - sg)))
            dxn = dyln * vec_ref[1:2, :]
            dz = rstd * (dxn - jnp.mean(dxn, axis=-1, keepdims=True) - xn * jnp.mean(dxn * xn, axis=-1, keepdims=True))
            dzpad[pl.ds(r0, CONV_CHUNK), :] = dz
            dvec8[0] += _fold8(dz)
            dvec8[1] += _fold8(dyln * xn)
            dvec8[2] += _fold8(dyln)
            return carry
        lax.fori_loop(0, n_chunks, norm_bwd, 0, unroll=2)

        def taps_bwd(i, carry):
            r0 = pl.multiple_of(i * CONV_CHUNK, CONV_CHUNK)
            dzwin = dzpad[pl.ds(r0, CONV_WIN), :]
            dv = _conv_taps(_shifted(dzwin, 0, copies), w_ref, lambda a, k: CONV_K - 1 - 8 * a - k)
            dz = dzwin[0:CONV_CHUNK]
            vwin = vpad[pl.ds(r0, CONV_WIN), :]
            for j, rows in _tap_slices(_shifted(vwin, CONV_HALO - (CONV_K - 1), copies), lambda a, k: 8 * a + k):
                dw8[j] += _fold8(dz * rows)
            a = uc_ref[0, pl.ds(r0, CONV_CHUNK), 0:CONV_W]
            sg = _sigmoid(uc_ref[0, pl.ds(r0, CONV_CHUNK), CONV_W:2 * CONV_W])
            duc_ref[0, pl.ds(r0, CONV_CHUNK), 0:CONV_W] = (dv * sg).astype(BF16)
            duc_ref[0, pl.ds(r0, CONV_CHUNK), CONV_W:2 * CONV_W] = (dv * a * sg * (1.0 - sg)).astype(BF16)
            return carry
        lax.fori_loop(0, n_chunks, taps_bwd, 0, unroll=2)

        @pl.when(pl.program_id(0) == NB - 1)
        def _():
            dw_ref[...] = jnp.zeros_like(dw_ref)
            dvec_ref[...] = jnp.zeros_like(dvec_ref)
            for j in range(CONV_K):
                dw_ref[j:j + 1, :] = jnp.sum(dw8[j], axis=0, keepdims=True)
            for j in range(3):
                dvec_ref[j:j + 1, :] = jnp.sum(dvec8[j], axis=0, keepdims=True)

    return _pallas(
        body, "conv_bwd", (NB,),
        [pl.BlockSpec((1, S, 2 * CONV_W), lambda b: (b, 0, 0)), pl.BlockSpec((1, S, CONV_W), lambda b: (b, 0, 0)),
         pl.BlockSpec((1, S, CONV_W), lambda b: (b, 0, 0)), _const((CONV_K, CONV_W)), _const((8, CONV_W))],
        [pl.BlockSpec((1, S, 2 * CONV_W), lambda b: (b, 0, 0)), pl.BlockSpec((32, CONV_W), lambda b: (0, 0)),
         pl.BlockSpec((8, CONV_W), lambda b: (0, 0))],
        [jax.ShapeDtypeStruct((NB, S, 2 * CONV_W), BF16), jax.ShapeDtypeStruct((32, CONV_W), F32),
         jax.ShapeDtypeStruct((8, CONV_W), F32)],
        (uc, z, dcact, dw_w, vec),
        scratch_shapes=[pltpu.VMEM((S + 2 * CONV_HALO, CONV_W), F32), pltpu.VMEM((S + 2 * CONV_HALO, CONV_W), F32),
                        pltpu.VMEM((CONV_K, 8, CONV_W), F32), pltpu.VMEM((3, 8, CONV_W), F32),
                        pltpu.VMEM((8, CONV_CHUNK + 24, CONV_W), F32)],
        sem=("arbitrary",), ride=ride)


def _rel_index_of_column(cols):
    offset = jnp.where(cols < KWIN, cols, cols - DS_LANES)
    return jnp.clip(KPAD - offset, -(CHUNK - 1), MAX_REL) + (CHUNK - 1)


def _bias_table(rel_bias):
    def body(rb_ref, o_ref, by_offset, first8):
        ridx = _rel_index_of_column(lax.broadcasted_iota(jnp.int32, (1, DS_LANES), 1))
        onehot = (ridx == lax.broadcasted_iota(jnp.int32, (N_REL, 1), 0)).astype(F32)
        by_offset[...] = jnp.dot(rb_ref[...], onehot, preferred_element_type=F32, precision=lax.Precision.HIGHEST)
        sub = lax.broadcasted_iota(jnp.int32, (8, 1), 0)
        kchunk = lax.broadcasted_iota(jnp.int32, (1, KWIN), 1) // CHUNK
        for head in range(ATT_HEADS):
            base = jnp.broadcast_to(by_offset[head:head + 1, :], (8, DS_LANES))
            rows = base
            for s in range(1, 8):
                rows = jnp.where(sub == s, pltpu.roll(base, s, 1), rows)
            first8[head] = rows

        def rows8(q8, carry):
            qchunk = (q8 * 8 + sub) // CHUNK
            band = (kchunk >= qchunk) & (kchunk <= qchunk + LEFT_CHUNKS)
            for head in range(ATT_HEADS):
                tile = pltpu.roll(first8[head], q8 * 8, 1)[:, 0:KWIN]
                o_ref[head, pl.ds(pl.multiple_of(q8 * 8, 8), 8), :] = jnp.where(band, tile, MASK_VALUE)
            return carry
        lax.fori_loop(0, QB // 8, rows8, 0)

    return pl.pallas_call(body, out_shape=jax.ShapeDtypeStruct((ATT_HEADS, QB, KWIN), F32), name="bias_table",
                          scratch_shapes=[pltpu.VMEM((ATT_HEADS, DS_LANES), F32), pltpu.VMEM((ATT_HEADS, 8, DS_LANES), F32)],
                          compiler_params=_params())(rel_bias)


def _load_keys(i, k_ref, v_ref, kpad, vpad, S):
    @pl.when(i == 0)
    def _():
        kpad[pl.ds(0, KPAD), :] = jnp.zeros((KPAD, ATT_W), BF16)
        vpad[pl.ds(0, KPAD), :] = jnp.zeros((KPAD, ATT_W), BF16)
        kpad[pl.ds(KPAD, S), :] = k_ref[0]
        vpad[pl.ds(KPAD, S), :] = v_ref[0]


def _att_scores(q2s, k2, tab_ref, head, in_head, in_seq):
    qm = jnp.where(in_head, q2s, jnp.zeros_like(q2s))
    return jnp.where(in_seq, _dot_nt(qm, k2) + tab_ref[head], MASK_VALUE)


def _scaled(q2):
    return q2 * jnp.asarray(ATT_HD ** -0.5, q2.dtype)


def _att_fwd(qkv, tab, ride=None):
    NB, S, _ = qkv.shape

    def body(q_ref, k_ref, v_ref, tab_ref, o_ref, lse_ref, kpad, vpad):
        i = pl.program_id(1)
        _load_keys(i, k_ref, v_ref, kpad, vpad, S)
        koff = pl.multiple_of(i * QB, QB)
        lane = lax.broadcasted_iota(jnp.int32, (1, 128), 1)
        in_seq = (lax.broadcasted_iota(jnp.int32, (1, KWIN), 1) + i * QB) >= KPAD
        lse = jnp.zeros((QB, 128), F32)
        for pair in range(ATT_HEADS // 2):
            cols = slice(pair * 128, (pair + 1) * 128)
            q2s = _scaled(q_ref[0, :, cols])
            k2 = kpad[pl.ds(koff, KWIN), cols]
            v2 = vpad[pl.ds(koff, KWIN), cols]
            o2 = jnp.zeros((QB, 128), F32)
            for hh in range(2):
                head = 2 * pair + hh
                in_head = (lane // ATT_HD) == hh
                s = _att_scores(q2s, k2, tab_ref, head, in_head, in_seq)
                m = jnp.max(s, axis=-1, keepdims=True)
                e = jnp.exp(s - m)
                l = jnp.sum(e, axis=-1, keepdims=True)
                p = e * (1.0 / l)
                o2 = jnp.where(in_head, _dot(p.astype(BF16), v2), o2)
                lse = jnp.where(lane == head, m + jnp.log(l), lse)
            o_ref[0, :, cols] = o2.astype(BF16)
        lse_ref[0] = lse

    seq = lambda col: pl.BlockSpec((1, S, ATT_W), lambda b, i: (b, 0, col), pipeline_mode=pl.Buffered(1))
    outs, rode = _pallas(
        body, "att_fwd", (NB, S // QB),
        [pl.BlockSpec((1, QB, ATT_W), lambda b, i: (b, i, 0)), seq(1), seq(2), _const((ATT_HEADS, QB, KWIN))],
        [pl.BlockSpec((1, QB, ATT_W), lambda b, i: (b, i, 0)), pl.BlockSpec((1, QB, 128), lambda b, i: (b, i, 0))],
        [jax.ShapeDtypeStruct((NB, S, ATT_W), BF16), jax.ShapeDtypeStruct((NB, S, 128), F32)],
        (qkv, qkv, qkv, tab),
        scratch_shapes=[pltpu.VMEM((S + KPAD, ATT_W), BF16), pltpu.VMEM((S + KPAD, ATT_W), BF16)],
        sem=("arbitrary", "arbitrary"), ride=ride)
    return outs[0], outs[1], rode


def _att_bwd(qkv, o, lse, do, tab, ride=None):
    NB, S, _ = qkv.shape
    nq = S // QB

    def body(q_ref, k_ref, v_ref, o_ref, lse_ref, do_ref, tab_ref, dqkv_ref, ds_hbm, kpad, vpad, dkpad, dvpad, ds_acc, ds_sem):
        b, i = pl.program_id(0), pl.program_id(1)
        _load_keys(i, k_ref, v_ref, kpad, vpad, S)

        @pl.when(i == 0)
        def _():
            dkpad[...] = jnp.zeros_like(dkpad)
            dvpad[...] = jnp.zeros_like(dvpad)

        @pl.when((i == 0) & (b == 0))
        def _():
            ds_acc[...] = jnp.zeros_like(ds_acc)

        koff = pl.multiple_of(i * QB, QB)
        lane = lax.broadcasted_iota(jnp.int32, (1, 128), 1)
        in_seq = (lax.broadcasted_iota(jnp.int32, (1, KWIN), 1) + i * QB) >= KPAD
        for pair in range(ATT_HEADS // 2):
            cols = slice(pair * 128, (pair + 1) * 128)
            q2s = _scaled(q_ref[0, :, cols])
            do2 = do_ref[0, :, cols]
            k2 = kpad[pl.ds(koff, KWIN), cols]
            v2 = vpad[pl.ds(koff, KWIN), cols]
            do_o = do2.astype(F32) * o_ref[0, :, cols].astype(F32)
            dq2 = jnp.zeros((QB, 128), F32)
            dk2 = jnp.zeros((KWIN, 128), F32)
            dv2 = jnp.zeros((KWIN, 128), F32)
            for hh in range(2):
                head = 2 * pair + hh
                in_head = (lane // ATT_HD) == hh
                p = jnp.exp(_att_scores(q2s, k2, tab_ref, head, in_head, in_seq) - lse_ref[0, :, head:head + 1])
                row_term = jnp.sum(jnp.where(in_head, do_o, 0.0), axis=-1, keepdims=True)
                dom = jnp.where(in_head, do2, jnp.zeros_like(do2))
                ds = p * (_dot_nt(dom, v2) - row_term)
                ds_acc[head] += ds
                dsb = ds.astype(BF16)
                dq2 = jnp.where(in_head, _dot(dsb, k2), dq2)
                dk2 = jnp.where(in_head, _dot_tn(dsb, q2s), dk2)
                dv2 = jnp.where(in_head, _dot_tn(p.astype(BF16), do2), dv2)
            dqkv_ref[0, pl.ds(koff, QB), cols] = (dq2 * (ATT_HD ** -0.5)).astype(BF16)
            dkpad[pl.ds(koff, KWIN), cols] += dk2
            dvpad[pl.ds(koff, KWIN), cols] += dv2

        @pl.when(i == nq - 1)
        def _():
            dqkv_ref[0, :, ATT_W:2 * ATT_W] = dkpad[pl.ds(KPAD, S), :].astype(BF16)
            dqkv_ref[0, :, 2 * ATT_W:3 * ATT_W] = dvpad[pl.ds(KPAD, S), :].astype(BF16)

        @pl.when((i == nq - 1) & (b == NB - 1))
        def _():
            out = pltpu.make_async_copy(ds_acc, ds_hbm, ds_sem)
            out.start()
            out.wait()

    seq = lambda col: pl.BlockSpec((1, S, ATT_W), lambda b, i: (b, 0, col), pipeline_mode=pl.Buffered(1))
    rows = pl.BlockSpec((1, QB, ATT_W), lambda b, i: (b, i, 0))
    return _pallas(
        body, "att_bwd", (NB, nq),
        [rows, seq(1), seq(2), rows, pl.BlockSpec((1, QB, 128), lambda b, i: (b, i, 0)), rows, _const((ATT_HEADS, QB, KWIN))],
        [pl.BlockSpec((1, S, 3 * ATT_W), lambda b, i: (b, 0, 0)), ANY],
        [jax.ShapeDtypeStruct((NB, S, 3 * ATT_W), BF16), jax.ShapeDtypeStruct((ATT_HEADS, QB, KWIN), F32)],
        (qkv, qkv, qkv, o, lse, do, tab),
        scratch_shapes=[pltpu.VMEM((S + KPAD, ATT_W), BF16), pltpu.VMEM((S + KPAD, ATT_W), BF16),
                        pltpu.VMEM((S + KPAD, ATT_W), F32), pltpu.VMEM((S + KPAD, ATT_W), F32),
                        pltpu.VMEM((ATT_HEADS, QB, KWIN), F32), pltpu.SemaphoreType.DMA],
        sem=("arbitrary", "arbitrary"), ride=ride)


def _rel_bias_grad(ds):
    def body(ds_ref, o_ref):
        sub = lax.broadcasted_iota(jnp.int32, (8, 1), 0)
        ridx = _rel_index_of_column(lax.broadcasted_iota(jnp.int32, (DS_LANES, 1), 0))
        onehot = (ridx == lax.broadcasted_iota(jnp.int32, (1, N_REL), 1)).astype(F32)
        def rows8(q8, accs):
            shift = lax.rem(DS_LANES - q8 * 8, DS_LANES)
            out = []
            for head in range(ATT_HEADS):
                tile = ds_ref[head, pl.ds(pl.multiple_of(q8 * 8, 8), 8), :]
                tile = jnp.concatenate([tile, jnp.zeros((8, DS_LANES - KWIN), F32)], axis=1)
                out.append(accs[head] + pltpu.roll(tile, shift, 1))
            return tuple(out)
        accs = lax.fori_loop(0, QB // 8, rows8, tuple(jnp.zeros((8, DS_LANES), F32) for _ in range(ATT_HEADS)))
        for head in range(ATT_HEADS):
            acc = accs[head]
            diag = jnp.zeros((8, DS_LANES), F32)
            for s in range(8):
                shifted = acc if s == 0 else pltpu.roll(acc, DS_LANES - s, 1)
                diag = jnp.where(sub == s, shifted, diag)
            z = jnp.sum(diag, axis=0, keepdims=True)
            o_ref[head:head + 1, :] = jnp.dot(z, onehot, preferred_element_type=F32, precision=lax.Precision.HIGHEST)

    return pl.pallas_call(body, out_shape=jax.ShapeDtypeStruct((ATT_HEADS, N_REL), F32), name="rel_bias_grad",
                          compiler_params=_params())(ds)


def _memkv_fwd(mem, g, w_kv, tm):
    R = mem.shape[0]
    tm = min(tm, R)

    def body(m_ref, g_ref, w_ref, h_ref, kv_ref):
        xhat, _ = _rms_stats(m_ref[...])
        h = (xhat * g_ref[...]).astype(BF16)
        h_ref[...] = h
        kv_ref[...] = _dot(h, w_ref[...]).astype(BF16)

    row = pl.BlockSpec((tm, D), lambda t: (t, 0))
    return pl.pallas_call(
        body, grid=(R // tm,), name="memkv_fwd", in_specs=[row, _const((1, D)), _const((D, 2 * MEM_W))], out_specs=[row, row],
        out_shape=[jax.ShapeDtypeStruct((R, D), BF16), jax.ShapeDtypeStruct((R, 2 * MEM_W), BF16)],
        compiler_params=_params("parallel"),
    )(mem, g, w_kv)


def _memkv_bwd(mem, dkv, w_kv, tm):
    R = mem.shape[0]
    tm = min(tm, R)

    def body(m_ref, dkv_ref, w_ref, dg_ref):
        xhat, _ = _rms_stats(m_ref[...])
        dh = _dot_nt(dkv_ref[...].astype(BF16), w_ref[...])

        @pl.when(pl.program_id(0) == 0)
        def _():
            dg_ref[...] = jnp.zeros_like(dg_ref)
        dg_ref[...] += jnp.sum(dh * xhat, axis=0, keepdims=True)

    row = pl.BlockSpec((tm, D), lambda t: (t, 0))
    return pl.pallas_call(
        body, grid=(R // tm,), name="memkv_bwd", in_specs=[row, row, _const((D, 2 * MEM_W))],
        out_specs=pl.BlockSpec((1, D), lambda t: (0, 0)), out_shape=jax.ShapeDtypeStruct((1, D), F32),
        compiler_params=_params("arbitrary"),
    )(mem, dkv, w_kv)


def _mem_probs(qh, kh):
    s = _dot_nt(qh, kh) * (MEM_HD ** -0.5)
    e = jnp.exp(s - jnp.max(s, axis=-1, keepdims=True))
    return e * (1.0 / jnp.sum(e, axis=-1, keepdims=True))


def _mematt_fwd(mq, kv, tq):
    NB, S, _ = mq.shape
    M = kv.shape[1]

    def body(q_ref, kv_ref, o_ref):
        for h in range(MEM_HEADS):
            cols = slice(h * MEM_HD, (h + 1) * MEM_HD)
            p = _mem_probs(q_ref[0, :, cols], kv_ref[0, :, cols])
            o_ref[0, :, cols] = _dot(p.astype(BF16), kv_ref[0, :, MEM_W + h * MEM_HD:MEM_W + (h + 1) * MEM_HD]).astype(BF16)

    return pl.pallas_call(
        body, grid=(NB, S // tq), name="mematt_fwd",
        in_specs=[pl.BlockSpec((1, tq, MEM_W), lambda b, i: (b, i, 0)), pl.BlockSpec((1, M, 2 * MEM_W), lambda b, i: (b, 0, 0))],
        out_specs=pl.BlockSpec((1, tq, MEM_W), lambda b, i: (b, i, 0)),
        out_shape=jax.ShapeDtypeStruct((NB, S, MEM_W), BF16), compiler_params=_params("parallel", "parallel"),
    )(mq, kv)


def _mematt_bwd(mq, kv, do, tq):
    NB, S, _ = mq.shape
    M = kv.shape[1]

    def body(q_ref, kv_ref, do_ref, dq_ref, dkv_ref):
        @pl.when(pl.program_id(1) == 0)
        def _():
            dkv_ref[...] = jnp.zeros_like(dkv_ref)
        for h in range(MEM_HEADS):
            cols = slice(h * MEM_HD, (h + 1) * MEM_HD)
            vcols = slice(MEM_W + h * MEM_HD, MEM_W + (h + 1) * MEM_HD)
            qh, kh, vh, doh = q_ref[0, :, cols], kv_ref[0, :, cols], kv_ref[0, :, vcols], do_ref[0, :, cols]
            p = _mem_probs(qh, kh)
            dp = _dot_nt(doh, vh)
            ds = p * (dp - jnp.sum(p * dp, axis=-1, keepdims=True))
            dss = (ds * (MEM_HD ** -0.5)).astype(BF16)
            dq_ref[0, :, cols] = _dot(dss, kh).astype(BF16)
            dkv_ref[0, :, cols] += _dot_tn(dss, qh)
            dkv_ref[0, :, vcols] += _dot_tn(p.astype(BF16), doh)

    qspec = pl.BlockSpec((1, tq, MEM_W), lambda b, i: (b, i, 0))
    kvspec = pl.BlockSpec((1, M, 2 * MEM_W), lambda b, i: (b, 0, 0))
    return pl.pallas_call(
        body, grid=(NB, S // tq), name="mematt_bwd", in_specs=[qspec, kvspec, qspec], out_specs=[qspec, kvspec],
        out_shape=[jax.ShapeDtypeStruct((NB, S, MEM_W), BF16), jax.ShapeDtypeStruct((NB, M, 2 * MEM_W), F32)],
        compiler_params=_params("arbitrary", "arbitrary"),
    )(mq, kv, do)


def _branch(j, in_ref, w_ref, gl_ref, bg_ref):
    y = _dot(in_ref[...], w_ref[...])
    gate = _sigmoid(gl_ref[:, j * D:(j + 1) * D].astype(F32) + bg_ref[:, j * D:(j + 1) * D])
    return y, gate


def _combine_fwd(x, cact, oatt, omem, gl, bg, wpw, wo, wmo, wout, tm):
    T = x.shape[0]

    def body(x_ref, c_ref, a_ref, m_ref, gl_ref, bg_ref, wpw_ref, wo_ref, wmo_ref, wout_ref, xo_ref, y_ref):
        y = None
        for j, (in_ref, w_ref) in enumerate(((c_ref, wpw_ref), (a_ref, wo_ref), (m_ref, wmo_ref))):
            yj, gate = _branch(j, in_ref, w_ref, gl_ref, bg_ref)
            y = gate * yj if y is None else y + gate * yj
        y = y.astype(BF16)
        y_ref[...] = y
        xo_ref[...] = x_ref[...] + _dot(y, wout_ref[...])

    row = lambda w: pl.BlockSpec((tm, w), lambda t: (t, 0))
    wbr = _const((512, D))
    return pl.pallas_call(
        body, grid=(T // tm,), name="combine_fwd",
        in_specs=[row(D), row(512), row(512), row(512), row(3 * D), _const((1, 3 * D)), wbr, wbr, wbr, _const((D, D))],
        out_specs=[row(D), row(D)],
        out_shape=[jax.ShapeDtypeStruct((T, D), F32), jax.ShapeDtypeStruct((T, D), BF16)],
        compiler_params=_params("parallel"),
    )(x, cact, oatt, omem, gl, bg, wpw, wo, wmo, wout)


def _combine_bwd(dx, cact, oatt, omem, gl, bg, wpw, wo, wmo, wout, tm, ride=None):
    T = dx.shape[0]

    def body(dx_ref, c_ref, a_ref, m_ref, gl_ref, bg_ref, wpw_ref, wo_ref, wmo_ref, wout_ref,
             dgl_ref, dc_ref, da_ref, dm_ref, dyc_ref, dya_ref, dym_ref, dbg_ref):
        dy = _dot_nt(dx_ref[...].astype(BF16), wout_ref[...])

        @pl.when(pl.program_id(0) == 0)
        def _():
            dbg_ref[...] = jnp.zeros_like(dbg_ref)
        branches = ((c_ref, wpw_ref, dyc_ref, dc_ref), (a_ref, wo_ref, dya_ref, da_ref), (m_ref, wmo_ref, dym_ref, dm_ref))
        for j, (in_ref, w_ref, dyb_ref, din_ref) in enumerate(branches):
            yj, gate = _branch(j, in_ref, w_ref, gl_ref, bg_ref)
            dyg = dy * gate
            dlogit = dyg * yj * (1.0 - gate)
            dgl_ref[:, j * D:(j + 1) * D] = dlogit.astype(BF16)
            dbg_ref[:, j * D:(j + 1) * D] += jnp.sum(dlogit, axis=0, keepdims=True)
            dyb = dyg.astype(BF16)
            dyb_ref[...] = dyb
            din_ref[...] = _dot_nt(dyb, w_ref[...]).astype(din_ref.dtype)

    row = lambda w: pl.BlockSpec((tm, w), lambda t: (t, 0))
    wbr = _const((512, D))
    sds = jax.ShapeDtypeStruct
    return _pallas(
        body, "combine_bwd", (T // tm,),
        [row(D), row(512), row(512), row(512), row(3 * D), _const((1, 3 * D)), wbr, wbr, wbr, _const((D, D))],
        [row(3 * D), row(512), row(512), row(512), row(D), row(D), row(D), pl.BlockSpec((1, 3 * D), lambda t: (0, 0))],
        [sds((T, 3 * D), BF16), sds((T, 512), F32), sds((T, 512), BF16), sds((T, 512), BF16),
         sds((T, D), BF16), sds((T, D), BF16), sds((T, D), BF16), sds((1, 3 * D), F32)],
        (dx, cact, oatt, omem, gl, bg, wpw, wo, wmo, wout), sem=("arbitrary",), ride=ride)


def _peer(x, y, c, rel):
    rx, ry, rc = (rel >> 2) & 1, (rel >> 1) & 1, rel & 1
    return ((1 - x) if rx else x, (1 - y) if ry else y, (1 - c) if rc else c)


def _all_sum_small(parts):
    n = len(parts)

    def body(*refs):
        p_refs, o_refs, slots = refs[:n], refs[n:2 * n], refs[2 * n:3 * n]
        send_sems, recv_sems = refs[3 * n:]
        x, y, c = _my_coords()
        me = _dev_index(x, y, c)

        def copy(i, rel, arrival):
            peer = _peer(x, y, c, rel)
            return pltpu.make_async_remote_copy(
                src_ref=p_refs[i], dst_ref=slots[i].at[_dev_index(*peer) if arrival else me],
                send_sem=send_sems.at[i, rel - 1], recv_sem=recv_sems.at[i, rel - 1], device_id=peer, device_id_type=MESH)

        for i in range(n):
            slots[i][me] = p_refs[i][...]
        for rel in range(1, NDEV):
            for i in range(n):
                copy(i, rel, False).start()
        for rel in range(1, NDEV):
            for i in range(n):
                copy(i, rel, True).wait_recv()
        for rel in range(1, NDEV):
            for i in range(n):
                copy(i, rel, False).wait_send()
        for i in range(n):
            total = slots[i][0]
            for d in range(1, NDEV):
                total = total + slots[i][d]
            o_refs[i][...] = total

    vmem = pl.BlockSpec(memory_space=pltpu.VMEM)
    return pl.pallas_call(
        body, out_shape=[jax.ShapeDtypeStruct(p.shape, F32) for p in parts], name="all_sum_small",
        in_specs=[vmem] * n, out_specs=[vmem] * n,
        scratch_shapes=[pltpu.VMEM((NDEV,) + p.shape, F32) for p in parts]
        + [pltpu.SemaphoreType.DMA((n, NDEV - 1)), pltpu.SemaphoreType.DMA((n, NDEV - 1))],
        compiler_params=pltpu.CompilerParams(has_side_effects=True),
    )(*parts)


HBM = pl.BlockSpec(memory_space=pltpu.HBM)
SEM = pl.BlockSpec(memory_space=pltpu.SEMAPHORE)


def _own_block(g, kind, m, tag):
    def body(g_ref, land_ref, staged, sem):
        me = _dev_index(*_my_coords())
        for cp in (pltpu.make_async_copy(_window(g_ref, kind, m, me), staged, sem),
                   pltpu.make_async_copy(staged, land_ref.at[me], sem)):
            cp.start()
            cp.wait()

    block = (m, g.shape[1]) if kind == 'row' else (g.shape[0], m)
    return pl.pallas_call(body, in_specs=[ANY], out_specs=ANY, out_shape=jax.ShapeDtypeStruct((NDEV,) + block, g.dtype),
                          scratch_shapes=[pltpu.VMEM(block, g.dtype), pltpu.SemaphoreType.DMA], name="own_block_" + tag)(g)


def _scatter_start(g, land, kind, m, tag):
    def body(g_ref, land_ref, send_sems, recv_sems, g_thru, land_thru, token):
        x, y, c = _my_coords()
        me = _dev_index(x, y, c)
        for rel in range(1, NDEV):
            peer = _peer(x, y, c, rel)
            pltpu.make_async_remote_copy(src_ref=_window(g_ref, kind, m, _dev_index(*peer)), dst_ref=land_ref.at[me],
                                         send_sem=send_sems.at[rel - 1], recv_sem=recv_sems.at[rel - 1],
                                         device_id=peer, device_id_type=MESH).start()
        token[...] = jnp.zeros_like(token)

    return pl.pallas_call(
        body, name="scatter_start_" + tag,
        out_shape=(pltpu.SemaphoreType.DMA((NDEV - 1,)), pltpu.SemaphoreType.DMA((NDEV - 1,)), pltpu.HBM(g.shape, g.dtype),
                   pltpu.HBM(land.shape, land.dtype), jax.ShapeDtypeStruct((8, 128), F32)),
        in_specs=(HBM, HBM), out_specs=(SEM, SEM, HBM, HBM, pl.BlockSpec(memory_space=pltpu.VMEM)),
        input_output_aliases={0: 2, 1: 3},
        compiler_params=pltpu.CompilerParams(has_side_effects=pltpu.SideEffectType.DATAFLOW_SIDE_EFFECTING),
    )(pltpu.with_memory_space_constraint(g, pltpu.HBM), pltpu.with_memory_space_constraint(land, pltpu.HBM))


def _scatter_wait(send_sems, recv_sems, g_thru, land_thru, after, kind, m, tag):
    n_after = len(after)

    def body(*refs):
        g_ref, land_ref, send_sems, recv_sems = refs[:4]
        x, y, c = _my_coords()
        me = _dev_index(x, y, c)
        for rel in range(1, NDEV):
            peer = _peer(x, y, c, rel)
            dev = _dev_index(*peer)
            cp = pltpu.make_async_remote_copy(src_ref=_window(g_ref, kind, m, me), dst_ref=land_ref.at[dev],
                                              send_sem=send_sems.at[rel - 1], recv_sem=recv_sems.at[rel - 1],
                                              device_id=peer, device_id_type=MESH)
            cp.wait_send()
            cp.wait_recv()

    return pl.pallas_call(
        body, name="scatter_wait_" + tag,
        out_shape=(pltpu.HBM(g_thru.shape, g_thru.dtype), pltpu.HBM(land_thru.shape, land_thru.dtype)),
        in_specs=(HBM, HBM, SEM, SEM) + (ANY,) * n_after, out_specs=(HBM, HBM), input_output_aliases={0: 0, 1: 1},
        compiler_params=pltpu.CompilerParams(has_side_effects=pltpu.SideEffectType.DATAFLOW_SIDE_EFFECTING),
    )(g_thru, land_thru, send_sems, recv_sems, *after)[1]


def _adamw_math(w, g, m, v):
    m = ADAM_B1 * m + (1.0 - ADAM_B1) * g
    v = ADAM_B2 * v + (1.0 - ADAM_B2) * (g * g)
    m_hat = m / (1.0 - ADAM_B1 ** ADAM_STEP)
    v_hat = v / (1.0 - ADAM_B2 ** ADAM_STEP)
    delta = -ADAM_LR * (m_hat / (jnp.sqrt(v_hat) + ADAM_EPS) + ADAM_WD * w)
    return delta, m, v


def _sum_adamw(parts, w, m, v, name, after=None):
    R, C = w.shape
    n_parts = len(parts)
    cg = C // n_parts
    tr = max(t for t in range(8, 257, 8) if R % t == 0)
    deps = [] if after is None else [after]

    def body(*refs):
        p_refs = refs[:n_parts]
        w_ref, m_ref, v_ref = refs[n_parts:n_parts + 3]
        g_ref, d_ref, mo_ref, vo_ref = refs[n_parts + 3 + len(deps):]
        for k, p_ref in enumerate(p_refs):
            @pl.when(pl.program_id(0) == k)
            def _():
                g = p_ref[0].astype(F32)
                for d in range(1, NDEV):
                    g = g + p_ref[d].astype(F32)
                g_ref[...] = g
                d_ref[...], mo_ref[...], vo_ref[...] = _adamw_math(w_ref[...], g, m_ref[...], v_ref[...])

    part = pl.BlockSpec((NDEV, tr, cg), lambda k, t: (0, t, 0))
    blk = pl.BlockSpec((tr, cg), lambda k, t: (t, k))
    return pl.pallas_call(
        body, grid=(n_parts, R // tr), name=name, in_specs=[part] * n_parts + [blk, blk, blk] + [ANY] * len(deps),
        out_specs=[blk] * 4, out_shape=[jax.ShapeDtypeStruct((R, C), F32)] * 4, compiler_params=_params("parallel", "parallel"),
    )(*parts, w, m, v, *deps)


def _adamw_small(ws, gs, ms, vs):
    n = len(ws)

    def body(*refs):
        w_refs, g_refs, m_refs, v_refs = (refs[k * n:(k + 1) * n] for k in range(4))
        d_refs, mo_refs, vo_refs = (refs[(4 + k) * n:(5 + k) * n] for k in range(3))
        for i in range(n):
            d_refs[i][...], mo_refs[i][...], vo_refs[i][...] = _adamw_math(w_refs[i][...], g_refs[i][...], m_refs[i][...], v_refs[i][...])

    shapes = [jax.ShapeDtypeStruct(a.shape, F32) for a in ws]
    outs = pl.pallas_call(body, out_shape=shapes * 3, name="adamw_small", compiler_params=_params())(*ws, *gs, *ms, *vs)
    return outs[:n], outs[n:2 * n], outs[2 * n:]


def kernel(x, mem, ffn1_norm, ffn1_w_up, ffn1_w_down, mix_norm, mem_norm, w_in, b_gate, conv_dw_w, conv_dw_b, conv_ln_g, conv_ln_b, conv_w_pw, att_rel_bias, att_w_o, mem_w_kv, mem_w_o, w_out, ffn2_norm, ffn2_w_up, ffn2_w_down, final_norm, loss_target, m_ffn1_norm, m_ffn1_w_up, m_ffn1_w_down, m_mix_norm, m_mem_norm, m_w_in, m_b_gate, m_conv_dw_w, m_conv_dw_b, m_conv_ln_g, m_conv_ln_b, m_conv_w_pw, m_att_rel_bias, m_att_w_o, m_mem_w_kv, m_mem_w_o, m_w_out, m_ffn2_norm, m_ffn2_w_up, m_ffn2_w_down, m_final_norm, v_ffn1_norm, v_ffn1_w_up, v_ffn1_w_down, v_mix_norm, v_mem_norm, v_w_in, v_b_gate, v_conv_dw_w, v_conv_dw_b, v_conv_ln_g, v_conv_ln_b, v_conv_w_pw, v_att_rel_bias, v_att_w_o, v_mem_w_kv, v_mem_w_o, v_w_out, v_ffn2_norm, v_ffn2_w_up, v_ffn2_w_down, v_final_norm):
    given = dict(locals())
    w = {n: given[n] for n in WEIGHTS}
    mom = {n: given["m_" + n] for n in WEIGHTS}
    var = {n: given["v_" + n] for n in WEIGHTS}

    NB, S, _ = x.shape
    T = NB * S
    ML = mem.shape[1]
    x0 = x.reshape(T, D)
    target = loss_target.reshape(T, D)
    mem2 = mem.reshape(NB * ML, D)

    def block(t, n):
        return jnp.transpose(t[0]) if n in TRANSPOSED else t[0]

    sh = dict(zip(BIG_ORDER, _cast_shards([block(w[n], n) for n in BIG_ORDER])))
    dw_t = jnp.transpose(conv_dw_w[0])

    def gather(names, extra=(), extra_kinds=()):
        return _gather_ride([sh[n] for n in names] + list(extra), [BIG[n] for n in names] + list(extra_kinds))

    W = {}
    names0 = ['ffn1_w_up', 'ffn1_w_down']
    got = _exchange_alone(gather(names0, [dw_t], [('row', dw_t.shape[0])]), "gather_ffn1")
    W.update(zip(names0, got[:2]))
    dw_full = jnp.transpose(got[2])
    conv_vec = jnp.concatenate([conv_dw_b, conv_ln_g, conv_ln_b, jnp.zeros((5, CONV_W), F32)], axis=0)
    tab = _bias_table(att_rel_bias[0])
    fin_g = final_norm.reshape(1, D)

    names1 = ['w_in', 'conv_w_pw', 'att_w_o', 'mem_w_kv', 'mem_w_o', 'w_out']
    (x1, ab1), got = _ffn_fwd(x0, ffn1_norm, W['ffn1_w_up'], W['ffn1_w_down'], TILE_FFN, "ffn1_fwd", ride=gather(names1))
    W.update(zip(names1, got))
    (uc, qkv, mq, gl, hmix), _ = _mix_fwd(x1, mix_norm, W['w_in'], TILE_TOKENS)
    uc3 = uc.reshape(NB, S, 2 * CONV_W)
    qkv3 = qkv.reshape(NB, S, 3 * ATT_W)
    mq3 = mq.reshape(NB, S, MEM_W)
    cact, conv_z = _conv_fwd(uc3, dw_full, conv_vec)
    cact = cact.reshape(T, CONV_W)
    names2 = ['ffn2_w_up', 'ffn2_w_down']
    oatt3, att_lse, got = _att_fwd(qkv3, tab, ride=gather(names2))
    W.update(zip(names2, got))
    oatt = oatt3.reshape(T, ATT_W)
    memh, kv = _memkv_fwd(mem2, mem_norm, W['mem_w_kv'], TILE_TOKENS)
    kv3 = kv.reshape(NB, ML, 2 * MEM_W)
    omem = _mematt_fwd(mq3, kv3, TILE_TOKENS).reshape(T, MEM_W)
    branch_w = (W['conv_w_pw'], W['att_w_o'], W['mem_w_o'], W['w_out'])
    x2, ymix = _combine_fwd(x1, cact, oatt, omem, gl, b_gate, *branch_w, TILE_COMBINE)
    dx3, ab2, loss_part, dg_final = _ffn_fwd_loss(x2, ffn2_norm, W['ffn2_w_up'], W['ffn2_w_down'], fin_g, target, TILE_FFN,
                                                  "ffn2_fwd_loss")

    def scatter(grads, names):
        return _scatter_ride(grads, [BIG[n] for n in names])

    G, P = {}, {}
    dx2, dab2, act2, h2, dg_ffn2 = _ffn_bwd(x2, dx3, ab2, ffn2_norm, W['ffn2_w_up'], W['ffn2_w_down'], TILE_FFN, "ffn2_bwd")
    g_up, _ = _tn_matmul(dab2, h2, 512, "grad_ffn2_w_up_a", tt=TILE_GRAD_TOKENS_WIDE, x_part=(0, 2), out_rows=2 * FF)
    G['ffn2_w_up'], _ = _tn_matmul(dab2, h2, 512, "grad_ffn2_w_up_b", tt=TILE_GRAD_TOKENS_WIDE, x_part=(1, 2), out_rows=2 * FF,
                                   prev=g_up)
    G['ffn2_w_down'], _ = _tn_matmul(act2, dx3, 512, "grad_ffn2_w_down", scale=0.5, tt=TILE_GRAD_TOKENS_WIDE)
    (dgl, dcact, doatt, domem, dyc, dya, dym, dbg), got = _combine_bwd(
        dx2, cact, oatt, omem, gl, b_gate, *branch_w, TILE_COMBINE, ride=scatter([G['ffn2_w_up']], ['ffn2_w_up']))
    P['ffn2_w_up'] = got
    G['w_out'], _ = _tn_matmul(ymix, dx2, 512, "grad_w_out")
    G['conv_w_pw'], _ = _tn_matmul(cact, dyc, 512, "grad_conv_w_pw")
    G['att_w_o'], _ = _tn_matmul(oatt, dya, 512, "grad_att_w_o")
    G['mem_w_o'], _ = _tn_matmul(omem, dym, 512, "grad_mem_w_o")
    dmq3, dkv3 = _mematt_bwd(mq3, kv3, domem.reshape(NB, S, MEM_W), TILE_TOKENS)
    dkv = dkv3.reshape(NB * ML, 2 * MEM_W)
    dg_mem = _memkv_bwd(mem2, dkv, W['mem_w_kv'], TILE_TOKENS)
    G['mem_w_kv'], _ = _tn_matmul(memh, dkv, 512, "grad_mem_w_kv")
    names = ['ffn2_w_down', 'w_out', 'conv_w_pw', 'att_w_o', 'mem_w_o']
    (dqkv3, dscore), got = _att_bwd(qkv3, oatt3, att_lse, doatt.reshape(NB, S, ATT_W), tab,
                                    ride=scatter([G[n] for n in names], names))
    P.update((n, [p]) for n, p in zip(names, got))
    d_rel = _rel_bias_grad(dscore)
    (duc3, d_dw, d_cvec), got = _conv_bwd(uc3, conv_z, dcact.reshape(NB, S, CONV_W), dw_full, conv_vec,
                                          ride=scatter([G['mem_w_kv']], ['mem_w_kv']))
    P['mem_w_kv'] = got
    duc, dqkv, dmq = duc3.reshape(T, 2 * CONV_W), dqkv3.reshape(T, 3 * ATT_W), dmq3.reshape(T, MEM_W)
    g_in, _ = _tn_matmul(hmix, duc, 512, "grad_w_in_conv", out_cols=IN_COLS, col_off=0)
    g_in, _ = _tn_matmul(hmix, dqkv, 512, "grad_w_in_qkv", out_cols=IN_COLS, col_off=1024, prev=g_in)
    g_in, _ = _tn_matmul(hmix, dmq, 512, "grad_w_in_mq", out_cols=IN_COLS, col_off=2560, prev=g_in)
    G['w_in'], _ = _tn_matmul(hmix, dgl, 1024, "grad_w_in_gate", out_cols=IN_COLS, col_off=3072, prev=g_in)
    def start_scatter(g, name, tag):
        kind = BIG[name]
        return _scatter_start(g, _own_block(g, *kind, tag), *kind, tag) + (kind, tag)

    def wait_scatter(started, after):
        send_sems, recv_sems, g_thru, land_thru, _, kind, tag = started
        return _scatter_wait(send_sems, recv_sems, g_thru, land_thru, after, *kind, tag)

    ex_in = start_scatter(G['w_in'], 'w_in', "w_in")
    (dx1, dg_mix), _ = _mix_bwd(x1, dx2, duc, dqkv, dmq, dgl, mix_norm, W['w_in'], TILE_TOKENS, after=ex_in[4])
    dx0, dab1, act1, h1, dg_ffn1 = _ffn_bwd(x0, dx1, ab1, ffn1_norm, W['ffn1_w_up'], W['ffn1_w_down'], TILE_FFN, "ffn1_bwd")
    g_wd1, _ = _tn_matmul(act1, dx1, 512, "grad_ffn1_w_down", scale=0.5, tt=TILE_GRAD_TOKENS_WIDE)
    ex_wd = start_scatter(g_wd1, 'ffn1_w_down', "ffn1_w_down")
    g_wu1a, _ = _tn_matmul(dab1, h1, 512, "grad_ffn1_w_up_a", tt=TILE_GRAD_TOKENS_WIDEST, y_part=(0, 2), after=ex_wd[4])
    ex_a = start_scatter(g_wu1a, 'ffn1_w_up', "ffn1_w_up_a")
    g_wu1b, _ = _tn_matmul(dab1, h1, 512, "grad_ffn1_w_up_b", tt=TILE_GRAD_TOKENS_WIDEST, y_part=(1, 2), after=ex_a[4])
    ex_b = start_scatter(g_wu1b, 'ffn1_w_up', "ffn1_w_up_b")
    token = ex_b[4]

    small_names = ['loss', 'ffn1_norm', 'mix_norm', 'mem_norm', 'b_gate', 'conv_dw_w', 'conv_vec', 'att_rel_bias', 'ffn2_norm',
                   'final_norm']
    small = dict(zip(small_names, _all_sum_small(
        [loss_part + token[0:1], dg_ffn1, dg_mix, dg_mem, dbg, d_dw, d_cvec, d_rel, dg_ffn2, dg_final])))
    loss = small['loss'][0, 0]
    me = _dev_index(*_my_coords())
    for i, n in enumerate(['conv_dw_b', 'conv_ln_g', 'conv_ln_b']):
        small[n] = small['conv_vec'][i:i + 1]
    small['conv_dw_w'] = lax.dynamic_slice(small['conv_dw_w'], (0, me * conv_dw_w.shape[2]), (CONV_K, conv_dw_w.shape[2]))
    little = [n for n in WEIGHTS if n not in BIG]
    as2d = lambda t, n: t.reshape(small[n].shape)
    d_s, m_s, v_s = _adamw_small([as2d(w[n], n) for n in little], [small[n] for n in little],
                                 [as2d(mom[n], n) for n in little], [as2d(var[n], n) for n in little])
    grad, delta, new_m, new_v = {}, {}, {}, {}
    for i, n in enumerate(little):
        grad[n], delta[n], new_m[n], new_v[n] = (t.reshape(w[n].shape) for t in (small[n], d_s[i], m_s[i], v_s[i]))
    done = [d_s[0]]
    waited = {'w_in': [ex_in], 'ffn1_w_down': [ex_wd], 'ffn1_w_up': [ex_a, ex_b]}
    order = [n for n in BIG_ORDER if n not in waited] + list(waited)
    for n in order:
        if n in waited:
            P[n] = [wait_scatter(ex, done) for ex in waited[n]]
        outs = _sum_adamw(P[n], block(w[n], n), block(mom[n], n), block(var[n], n), "adamw_" + n,
                          after=None if n in waited else token)
        done.append(outs[0])
        grad[n], delta[n], new_m[n], new_v[n] = ((jnp.transpose(t) if n in TRANSPOSED else t)[None] for t in outs)

    return (loss, dx0.reshape(NB, S, D), *[grad[n] for n in WEIGHTS], *[delta[n] for n in WEIGHTS],
            *[new_m[n] for n in WEIGHTS], *[new_v[n] for n in WEIGHTS])
```

```python
import functools

import jax
import jax.numpy as jnp
from jax import lax
from jax.experimental import pallas as pl
from jax.experimental.pallas import tpu as pltpu

F32 = jnp.float32
BF16 = jnp.bfloat16

EPS = 1e-6
MASK_VALUE = -1e30
D = 1024
NDEV = 8
FF = 2816
FF_SHARD = 704
FF_HALF_ROWS = 352
FF_BLOCK_EDGES = ()
IN_COLS = 6144
CONV_W = 512
CONV_K = 31
CONV_HALO = 32
CONV_CHUNK = 32
CONV_WIN = CONV_CHUNK + 40
GLU_CHUNK = 128
ATT_W = 512
ATT_HEADS = 8
ATT_HD = 64
CHUNK = 64
LEFT_CHUNKS = 8
MAX_REL = 128
N_REL = 192
QB = 256
KWIN = QB + LEFT_CHUNKS * CHUNK
KPAD = LEFT_CHUNKS * CHUNK
DS_LANES = 1024
MEM_W = 512
MEM_HEADS = 4
MEM_HD = 128
ADAM_LR = 0.001
ADAM_B1 = 0.9
ADAM_B2 = 0.999
ADAM_EPS = 1e-08
ADAM_WD = 0.01
ADAM_STEP = 10
VMEM_LIMIT = 60 * 1024 * 1024
TILE_FFN = 256
TILE_COMBINE = 256
TILE_TOKENS = 512
TILE_GRAD_TOKENS = 2048
TILE_GRAD_TOKENS_WIDE = 1024
TILE_GRAD_TOKENS_WIDEST = 512

MESH = pl.DeviceIdType.MESH
ANY = pl.BlockSpec(memory_space=pl.ANY)

WEIGHTS = ['ffn1_norm', 'ffn1_w_up', 'ffn1_w_down', 'mix_norm', 'mem_norm', 'w_in', 'b_gate', 'conv_dw_w', 'conv_dw_b',
           'conv_ln_g', 'conv_ln_b', 'conv_w_pw', 'att_rel_bias', 'att_w_o', 'mem_w_kv', 'mem_w_o', 'w_out', 'ffn2_norm',
           'ffn2_w_up', 'ffn2_w_down', 'final_norm']
BIG = {
    'ffn1_w_up': ('row', FF_SHARD), 'ffn1_w_down': ('row', FF_HALF_ROWS), 'w_in': ('col', 768),
    'conv_w_pw': ('col', 128), 'att_w_o': ('col', 128), 'mem_w_kv': ('row', 128), 'mem_w_o': ('col', 128),
    'w_out': ('row', 128), 'ffn2_w_up': ('row', FF_SHARD), 'ffn2_w_down': ('row', FF_HALF_ROWS),
}
BIG_ORDER = ['ffn1_w_up', 'ffn1_w_down', 'w_in', 'conv_w_pw', 'att_w_o', 'mem_w_kv', 'mem_w_o', 'w_out', 'ffn2_w_up', 'ffn2_w_down']
TRANSPOSED = ('ffn1_w_up', 'ffn2_w_up')


def _dot(a, b):
    return jnp.dot(a, b, preferred_element_type=F32)


def _dot_nt(a, b):
    return lax.dot_general(a, b, (((1,), (1,)), ((), ())), preferred_element_type=F32)


def _dot_tn(a, b):
    return lax.dot_general(a, b, (((0,), (0,)), ((), ())), preferred_element_type=F32)


def _sigmoid(v):
    return jax.nn.sigmoid(v)


def _const(shape):
    return pl.BlockSpec(shape, lambda *_: (0,) * len(shape), pipeline_mode=pl.Buffered(1))


def _params(*sem):
    return pltpu.CompilerParams(dimension_semantics=sem if sem else None, vmem_limit_bytes=VMEM_LIMIT)


def _my_coords():
    return lax.axis_index("x"), lax.axis_index("y"), lax.axis_index("c")


def _dev_index(px, py, pc):
    return 4 * px + 2 * py + pc


def _window(ref, kind, n, p):
    if kind == 'row':
        return ref.at[pl.ds(pl.multiple_of(p * n, n), n), :]
    return ref.at[:, pl.ds(pl.multiple_of(p * n, 128), n)]


def _full_shape(kind, n, shard_shape):
    if kind == 'row':
        return (NDEV * n, shard_shape[1])
    return (shard_shape[0], NDEV * n)


def _cast_shards(shards):
    n = len(shards)

    def body(*refs):
        for i in range(n):
            refs[n + i][...] = refs[i][...].astype(BF16)

    out_shape = [jax.ShapeDtypeStruct(s.shape, BF16) for s in shards]
    return pl.pallas_call(body, out_shape=out_shape, name="cast_shards", compiler_params=_params())(*shards)


class _Ride:
    def __init__(self, inputs, out_shape, scratch, start, finish, mid=None):
        self.inputs, self.out_shape, self.scratch = list(inputs), list(out_shape), list(scratch)
        self.start, self.finish, self.mid = start, finish, mid


def _pallas(body, name, grid, in_specs, out_specs, out_shape, args, scratch_shapes=(), sem=None, aliases=None, ride=None,
            after=None):
    if ride is None:
        n_in, n_dep = len(args), 0 if after is None else 1

        def kernel_body(*refs):
            body(*refs[:n_in], *refs[n_in + n_dep:])

        outs = pl.pallas_call(kernel_body if n_dep else body, grid=grid, name=name, in_specs=list(in_specs) + [ANY] * n_dep,
                              out_specs=out_specs, out_shape=out_shape, scratch_shapes=list(scratch_shapes),
                              input_output_aliases=aliases or {}, compiler_params=_params(*sem),
                              )(*args, *([after] if n_dep else []))
        return list(outs), []
    n_in, n_out, n_scr = len(args), len(out_shape), len(scratch_shapes)
    r_in, r_out = len(ride.inputs), len(ride.out_shape)

    def wrapped(*refs):
        k_in, rin = refs[:n_in], refs[n_in:n_in + r_in]
        o0 = n_in + r_in
        k_out, rout = refs[o0:o0 + n_out], refs[o0 + n_out:o0 + n_out + r_out]
        s0 = o0 + n_out + r_out
        k_scr, rscr = refs[s0:s0 + n_scr], refs[s0 + n_scr:]
        ids = [pl.program_id(k) for k in range(len(grid))]
        first = functools.reduce(jnp.logical_and, [i == 0 for i in ids])
        last = functools.reduce(jnp.logical_and, [i == g - 1 for i, g in zip(ids, grid)])
        pl.when(first)(lambda: ride.start(rin, rout, rscr))
        single_step = all(g == 1 for g in grid)
        if ride.mid is not None and not single_step:
            at_mid = functools.reduce(jnp.logical_and, [ids[0] == (3 * grid[0]) // 4] + [i == 0 for i in ids[1:]])
            pl.when(at_mid)(lambda: ride.mid(rin, rout, rscr))
        body(*k_in, *k_out, *k_scr)
        if ride.mid is not None and single_step:
            ride.mid(rin, rout, rscr)
        pl.when(last)(lambda: ride.finish(rin, rout, rscr))

    outs = pl.pallas_call(
        wrapped, grid=grid, name=name, in_specs=list(in_specs) + [ANY] * r_in, out_specs=list(out_specs) + [ANY] * r_out,
        out_shape=list(out_shape) + ride.out_shape, scratch_shapes=list(scratch_shapes) + ride.scratch,
        input_output_aliases=aliases or {}, compiler_params=_params(*(["arbitrary"] * len(grid))),
    )(*args, *ride.inputs)
    return list(outs[:n_out]), list(outs[n_out:])


def _gather_ride(shards, kinds):
    n = len(shards)

    def plan(rin, out, sems):
        send_sems, recv_sems, local_sems = sems[:3]
        x, y, c = _my_coords()
        me, sibling = (x, y, c), (x, y, 1 - c)
        chips = [(1 - x, y), (x, 1 - y), (1 - x, 1 - y)]

        def win(i, dev):
            return _window(out[i], kinds[i][0], kinds[i][1], _dev_index(*dev))

        def copy(i, k, block, to, from_shard=False):
            return pltpu.make_async_remote_copy(
                src_ref=rin[i] if from_shard else win(i, block), dst_ref=win(i, block),
                send_sem=send_sems.at[i, k], recv_sem=recv_sems.at[i, k], device_id=to, device_id_type=MESH)

        def local():
            return [pltpu.make_async_copy(rin[i], win(i, me), local_sems.at[i]) for i in range(n)]

        def first():
            cps = []
            for i in range(n):
                cps.append(copy(i, 0, me, sibling, from_shard=True))
                cps += [copy(i, 1 + j, me, (*chip, c), from_shard=True) for j, chip in enumerate(chips)]
            return cps

        def arrived():
            return [copy(i, 1 + j, (*chip, c), me) for j, chip in enumerate(chips) for i in range(n)]

        def passed():
            return [copy(i, 4 + j, (*chip, c), sibling) for j, chip in enumerate(chips) for i in range(n)]

        def from_sibling():
            cps = [copy(i, 0, sibling, me) for i in range(n)]
            return cps + [copy(i, 4 + j, (*chip, 1 - c), me) for i in range(n) for j, chip in enumerate(chips)]

        return local, first, arrived, passed, from_sibling

    def start(rin, out, sems):
        local, first, _, _, _ = plan(rin, out, sems)
        for cp in local() + first():
            cp.start()

    def mid(rin, out, sems):
        _, _, arrived, passed, _ = plan(rin, out, sems)
        for got, fwd in zip(arrived(), passed()):
            got.wait_recv()
            fwd.start()

    def finish(rin, out, sems):
        local, first, _, passed, from_sibling = plan(rin, out, sems)
        for cp in from_sibling():
            cp.wait_recv()
        for cp in first() + passed():
            cp.wait_send()
        for cp in local():
            cp.wait()

    out_shape = [jax.ShapeDtypeStruct(_full_shape(k, m, s.shape), s.dtype) for s, (k, m) in zip(shards, kinds)]
    scratch = [pltpu.SemaphoreType.DMA((n, 7)), pltpu.SemaphoreType.DMA((n, 7)), pltpu.SemaphoreType.DMA((n,))]
    return _Ride(shards, out_shape, scratch, start, finish, mid)


def _scatter_ride(grads, kinds):
    n = len(grads)

    def plan(g, out, sems):
        send_sems, recv_sems, local_sems = sems
        x, y, c = _my_coords()
        me = _dev_index(x, y, c)

        def local():
            return [pltpu.make_async_copy(_window(g[i], kinds[i][0], kinds[i][1], me), out[i].at[me], local_sems.at[i])
                    for i in range(n)]

        def remote(arrival):
            cps = []
            for rel in range(1, NDEV):
                peer = _peer(x, y, c, rel)
                dev = _dev_index(*peer)
                for i in range(n):
                    kind, m = kinds[i]
                    cps.append(pltpu.make_async_remote_copy(
                        src_ref=_window(g[i], kind, m, me if arrival else dev), dst_ref=out[i].at[dev if arrival else me],
                        send_sem=send_sems.at[i, rel - 1], recv_sem=recv_sems.at[i, rel - 1], device_id=peer, device_id_type=MESH))
            return cps

        return local, remote

    def start(g, out, sems):
        local, remote = plan(g, out, sems)
        for cp in local() + remote(False):
            cp.start()

    def finish(g, out, sems):
        local, remote = plan(g, out, sems)
        for cp in remote(True):
            cp.wait_recv()
        for cp in remote(False):
            cp.wait_send()
        for cp in local():
            cp.wait()

    def block_shape(gr, kind, m):
        return (m, gr.shape[1]) if kind == 'row' else (gr.shape[0], m)

    out_shape = [jax.ShapeDtypeStruct((NDEV,) + block_shape(gr, k, m), gr.dtype) for gr, (k, m) in zip(grads, kinds)]
    scratch = [pltpu.SemaphoreType.DMA((n, NDEV - 1)), pltpu.SemaphoreType.DMA((n, NDEV - 1)), pltpu.SemaphoreType.DMA((n,))]
    return _Ride(grads, out_shape, scratch, start, finish)


def _rms_stats(xf):
    r = lax.rsqrt(jnp.mean(xf * xf, axis=-1, keepdims=True) + EPS)
    return xf * r, r


def _rms_bwd(dh, g, xhat, r):
    dxhat = dh * g
    return r * (dxhat - xhat * jnp.mean(dxhat * xhat, axis=-1, keepdims=True))


def _ffn_blocks():
    edges = (0,) + FF_BLOCK_EDGES + (FF,)
    return [(slice(lo, hi), slice(FF + lo, FF + hi)) for lo, hi in zip(edges[:-1], edges[1:])]


def _swiglu_tile(x_ref, g_ref, wut_ref, wd_ref, ab_ref):
    xf = x_ref[...]
    xhat, _ = _rms_stats(xf)
    h = (xhat * g_ref[...]).astype(BF16)
    acc = jnp.zeros(xf.shape, F32)
    for ra, rb in _ffn_blocks():
        a = _dot_nt(h, wut_ref[ra, :])
        b = _dot_nt(h, wut_ref[rb, :])
        ab_ref[:, ra] = a.astype(BF16)
        ab_ref[:, rb] = b.astype(BF16)
        act = (a * _sigmoid(a) * b).astype(BF16)
        acc = acc + _dot(act, wd_ref[ra, :])
    return xf + 0.5 * acc


def _ffn_fwd(x, g, wut, wd, tm, name, ride=None):
    T = x.shape[0]

    def body(x_ref, g_ref, wut_ref, wd_ref, xo_ref, ab_ref):
        xo_ref[...] = _swiglu_tile(x_ref, g_ref, wut_ref, wd_ref, ab_ref)

    return _pallas(
        body, name, (T // tm,),
        [pl.BlockSpec((tm, D), lambda t: (t, 0)), _const((1, D)), _const((2 * FF, D)), _const((FF, D))],
        [pl.BlockSpec((tm, D), lambda t: (t, 0)), pl.BlockSpec((tm, 2 * FF), lambda t: (t, 0))],
        [jax.ShapeDtypeStruct((T, D), F32), jax.ShapeDtypeStruct((T, 2 * FF), BF16)],
        (x, g, wut, wd), sem=("arbitrary",), ride=ride)


def _ffn_fwd_loss(x, g, wut, wd, g_final, target, tm, name):
    T = x.shape[0]

    def body(x_ref, g_ref, wut_ref, wd_ref, gf_ref, t_ref, dx_ref, ab_ref, loss_ref, dgf_ref):
        xhat, r = _rms_stats(_swiglu_tile(x_ref, g_ref, wut_ref, wd_ref, ab_ref))
        gain = gf_ref[...]
        diff = xhat * gain - t_ref[...]
        dout = diff * (1.0 / D)

        @pl.when(pl.program_id(0) == 0)
        def _():
            loss_ref[...] = jnp.zeros_like(loss_ref)
            dgf_ref[...] = jnp.zeros_like(dgf_ref)
        sq = jnp.sum(jnp.sum(diff * diff, axis=0, keepdims=True), axis=1, keepdims=True)
        loss_ref[...] += jnp.broadcast_to(sq * (0.5 / D), (1, 128))
        dgf_ref[...] += jnp.sum(dout * xhat, axis=0, keepdims=True)
        dx_ref[...] = _rms_bwd(dout, gain, xhat, r)

    row = pl.BlockSpec((tm, D), lambda t: (t, 0))
    return pl.pallas_call(
        body, grid=(T // tm,), name=name,
        in_specs=[row, _const((1, D)), _const((2 * FF, D)), _const((FF, D)), _const((1, D)), row],
        out_specs=[row, pl.BlockSpec((tm, 2 * FF), lambda t: (t, 0)), pl.BlockSpec((1, 128), lambda t: (0, 0)),
                   pl.BlockSpec((1, D), lambda t: (0, 0))],
        out_shape=[jax.ShapeDtypeStruct((T, D), F32), jax.ShapeDtypeStruct((T, 2 * FF), BF16),
                   jax.ShapeDtypeStruct((1, 128), F32), jax.ShapeDtypeStruct((1, D), F32)],
        compiler_params=_params("arbitrary"),
    )(x, g, wut, wd, g_final, target)


def _ffn_bwd(x, dy, ab, g, wut, wd, tm, name):
    T = x.shape[0]

    def body(x_ref, dy_ref, ab_ref, g_ref, wut_ref, wd_ref, dx_ref, dab_ref, act_ref, h_ref, dg_ref):
        xf = x_ref[...]
        xhat, r = _rms_stats(xf)
        gain = g_ref[...]
        h_ref[...] = (xhat * gain).astype(BF16)
        dy = dy_ref[...]
        dyh = (0.5 * dy).astype(BF16)
        dh = jnp.zeros((tm, D), F32)
        for ra, rb in _ffn_blocks():
            a = ab_ref[:, ra].astype(F32)
            b = ab_ref[:, rb].astype(F32)
            dact = _dot_nt(dyh, wd_ref[ra, :])
            sg = _sigmoid(a)
            sl = a * sg
            act_ref[:, ra] = (sl * b).astype(BF16)
            da = (dact * b * (sg * (1.0 + a * (1.0 - sg)))).astype(BF16)
            db = (dact * sl).astype(BF16)
            dab_ref[:, ra] = da
            dab_ref[:, rb] = db
            dh = dh + _dot(da, wut_ref[ra, :]) + _dot(db, wut_ref[rb, :])
        dx_ref[...] = dy + _rms_bwd(dh, gain, xhat, r)

        @pl.when(pl.program_id(0) == 0)
        def _():
            dg_ref[...] = jnp.zeros_like(dg_ref)
        dg_ref[...] += jnp.sum(dh * xhat, axis=0, keepdims=True)

    return pl.pallas_call(
        body, grid=(T // tm,), name=name,
        in_specs=[pl.BlockSpec((tm, D), lambda t: (t, 0)), pl.BlockSpec((tm, D), lambda t: (t, 0)),
                  pl.BlockSpec((tm, 2 * FF), lambda t: (t, 0)), _const((1, D)), _const((2 * FF, D)), _const((FF, D))],
        out_specs=[pl.BlockSpec((tm, D), lambda t: (t, 0)), pl.BlockSpec((tm, 2 * FF), lambda t: (t, 0)),
                   pl.BlockSpec((tm, FF), lambda t: (t, 0)), pl.BlockSpec((tm, D), lambda t: (t, 0)),
                   pl.BlockSpec((1, D), lambda t: (0, 0))],
        out_shape=[jax.ShapeDtypeStruct((T, D), F32), jax.ShapeDtypeStruct((T, 2 * FF), BF16),
                   jax.ShapeDtypeStruct((T, FF), BF16), jax.ShapeDtypeStruct((T, D), BF16), jax.ShapeDtypeStruct((1, D), F32)],
        compiler_params=_params("arbitrary"),
    )(x, dy, ab, g, wut, wd)


def _tn_matmul(xm, ym, tn, name, scale=None, out_cols=None, col_off=0, prev=None, tt=TILE_GRAD_TOKENS, x_part=(0, 1),
               out_rows=None, y_part=(0, 1), ride=None, after=None):
    T = xm.shape[0]
    xi, xn = x_part
    yi, yn = y_part
    K = xm.shape[1] // xn
    N = ym.shape[1] // yn
    out_cols = N if out_cols is None else out_cols
    row_blk = xi if out_rows is not None else 0
    out_rows = K if out_rows is None else out_rows
    tt = min(tt, T)
    nt = T // tt
    off = col_off // tn

    def body(*refs):
        x_ref, y_ref = refs[0], refs[1]
        o_ref, acc = refs[-2], refs[-1]

        @pl.when(pl.program_id(1) == 0)
        def _():
            acc[...] = jnp.zeros_like(acc)
        acc[...] += _dot_tn(x_ref[...].astype(BF16), y_ref[...].astype(BF16))

        @pl.when(pl.program_id(1) == nt - 1)
        def _():
            res = acc[...]
            o_ref[...] = (res if scale is None else res * scale).astype(BF16)

    ycol = yi * (N // tn)
    in_specs = [pl.BlockSpec((tt, K), lambda n, t: (t, xi)), pl.BlockSpec((tt, tn), lambda n, t: (t, n + ycol))]
    args = [xm, ym]
    aliases = {}
    if prev is not None:
        in_specs.append(ANY)
        args.append(prev)
        aliases = {2: 0}
    outs, rode = _pallas(
        body, name, (N // tn, nt), in_specs, [pl.BlockSpec((K, tn), lambda n, t: (row_blk, n + off))],
        [jax.ShapeDtypeStruct((out_rows, out_cols), BF16)], args, scratch_shapes=[pltpu.VMEM((K, tn), F32)],
        sem=("parallel", "arbitrary"), aliases=aliases, ride=ride, after=after)
    return outs[0], rode


def _mix_fwd(x, g, w_in, tm, ride=None):
    T = x.shape[0]

    def body(x_ref, g_ref, w_ref, uc_ref, qkv_ref, mq_ref, gl_ref, h_ref):
        xhat, _ = _rms_stats(x_ref[...])
        h = (xhat * g_ref[...]).astype(BF16)
        h_ref[...] = h
        uc_ref[...] = _dot(h, w_ref[:, 0:1024])
        qkv_ref[...] = _dot(h, w_ref[:, 1024:2560]).astype(BF16)
        mq_ref[...] = _dot(h, w_ref[:, 2560:3072]).astype(BF16)
        for j in range(3):
            gl_ref[:, j * D:(j + 1) * D] = _dot(h, w_ref[:, 3072 + j * D:3072 + (j + 1) * D]).astype(BF16)

    row = lambda w: pl.BlockSpec((tm, w), lambda t: (t, 0))
    return _pallas(
        body, "mix_fwd", (T // tm,), [row(D), _const((1, D)), _const((D, IN_COLS))],
        [row(1024), row(1536), row(512), row(3072), row(D)],
        [jax.ShapeDtypeStruct((T, 1024), F32), jax.ShapeDtypeStruct((T, 1536), BF16), jax.ShapeDtypeStruct((T, 512), BF16),
         jax.ShapeDtypeStruct((T, 3072), BF16), jax.ShapeDtypeStruct((T, D), BF16)],
        (x, g, w_in), sem=("parallel",), ride=ride)


def _mix_bwd(x, dres, duc, dqkv, dmq, dgl, g, w_in, tm, ride=None, after=None):
    T = x.shape[0]

    def body(x_ref, dres_ref, duc_ref, dqkv_ref, dmq_ref, dgl_ref, g_ref, w_ref, dx_ref, dg_ref):
        xhat, r = _rms_stats(x_ref[...])
        dh = _dot_nt(duc_ref[...], w_ref[:, 0:1024])
        dh = dh + _dot_nt(dqkv_ref[...], w_ref[:, 1024:2560])
        dh = dh + _dot_nt(dmq_ref[...], w_ref[:, 2560:3072])
        dh = dh + _dot_nt(dgl_ref[...], w_ref[:, 3072:6144])
        dx_ref[...] = dres_ref[...] + _rms_bwd(dh, g_ref[...], xhat, r)

        @pl.when(pl.program_id(0) == 0)
        def _():
            dg_ref[...] = jnp.zeros_like(dg_ref)
        dg_ref[...] += jnp.sum(dh * xhat, axis=0, keepdims=True)

    row = lambda w: pl.BlockSpec((tm, w), lambda t: (t, 0))
    return _pallas(
        body, "mix_bwd", (T // tm,),
        [row(D), row(D), row(1024), row(1536), row(512), row(3072), _const((1, D)), _const((D, IN_COLS))],
        [row(D), pl.BlockSpec((1, D), lambda t: (0, 0))],
        [jax.ShapeDtypeStruct((T, D), F32), jax.ShapeDtypeStruct((1, D), F32)],
        (x, dres, duc, dqkv, dmq, dgl, g, w_in), sem=("arbitrary",), ride=ride, after=after)


def _shifted(win, base, copies):
    for k in range(8):
        copies[k] = win[base + k:base + k + CONV_CHUNK + 24]
    return copies


def _tap_slices(copies, tap):
    out = []
    for k in range(8):
        for a in range(4):
            j = tap(a, k)
            if 0 <= j < CONV_K:
                out.append((j, copies[k, pl.ds(8 * a, CONV_CHUNK), :]))
    return out


def _conv_taps(copies, w_ref, tap):
    acc = jnp.zeros((CONV_CHUNK, CONV_W), F32)
    for j, rows in _tap_slices(copies, tap):
        acc = acc + rows * w_ref[j:j + 1, :]
    return acc


def _fold8(v):
    acc = v[0:8]
    for r in range(8, CONV_CHUNK, 8):
        acc = acc + v[r:r + 8]
    return acc


def _glu_into(uc_ref, vpad, S):
    vpad[pl.ds(0, CONV_HALO), :] = jnp.zeros((CONV_HALO, CONV_W), F32)
    vpad[pl.ds(S + CONV_HALO, CONV_HALO), :] = jnp.zeros((CONV_HALO, CONV_W), F32)

    def glu(i, carry):
        r0 = pl.multiple_of(i * GLU_CHUNK, GLU_CHUNK)
        a = uc_ref[0, pl.ds(r0, GLU_CHUNK), 0:CONV_W]
        gt = uc_ref[0, pl.ds(r0, GLU_CHUNK), CONV_W:2 * CONV_W]
        vpad[pl.ds(pl.multiple_of(r0 + CONV_HALO, CONV_HALO), GLU_CHUNK), :] = a * _sigmoid(gt)
        return carry
    lax.fori_loop(0, S // GLU_CHUNK, glu, 0)


def _layer_norm(z, vec_ref):
    xc = z - jnp.mean(z, axis=-1, keepdims=True)
    rstd = lax.rsqrt(jnp.mean(xc * xc, axis=-1, keepdims=True) + EPS)
    xn = xc * rstd
    return xn, rstd, xn * vec_ref[1:2, :] + vec_ref[2:3, :]


def _conv_fwd(uc, dw_w, vec):
    NB, S, _ = uc.shape

    def body(uc_ref, w_ref, vec_ref, o_ref, z_ref, vpad, copies):
        _glu_into(uc_ref, vpad, S)

        def conv(i, carry):
            r0 = pl.multiple_of(i * CONV_CHUNK, CONV_CHUNK)
            win = vpad[pl.ds(r0, CONV_WIN), :]
            z = _conv_taps(_shifted(win, CONV_HALO - (CONV_K - 1), copies), w_ref, lambda a, k: 8 * a + k) + vec_ref[0:1, :]
            z_ref[0, pl.ds(r0, CONV_CHUNK), :] = z
            _, _, yln = _layer_norm(z, vec_ref)
            o_ref[0, pl.ds(r0, CONV_CHUNK), :] = (yln * _sigmoid(yln)).astype(BF16)
            return carry
        lax.fori_loop(0, S // CONV_CHUNK, conv, 0, unroll=2)

    seq = pl.BlockSpec((1, S, CONV_W), lambda b: (b, 0, 0))
    return pl.pallas_call(
        body, grid=(NB,), name="conv_fwd",
        in_specs=[pl.BlockSpec((1, S, 2 * CONV_W), lambda b: (b, 0, 0)), _const((CONV_K, CONV_W)), _const((8, CONV_W))],
        out_specs=[seq, seq],
        out_shape=[jax.ShapeDtypeStruct((NB, S, CONV_W), BF16), jax.ShapeDtypeStruct((NB, S, CONV_W), F32)],
        scratch_shapes=[pltpu.VMEM((S + 2 * CONV_HALO, CONV_W), F32), pltpu.VMEM((8, CONV_CHUNK + 24, CONV_W), F32)],
        compiler_params=_params("parallel"),
    )(uc, dw_w, vec)


def _conv_bwd(uc, z, dcact, dw_w, vec, ride=None):
    NB, S, _ = uc.shape
    n_chunks = S // CONV_CHUNK

    def body(uc_ref, z_ref, dc_ref, w_ref, vec_ref, duc_ref, dw_ref, dvec_ref, vpad, dzpad, dw8, dvec8, copies):
        @pl.when(pl.program_id(0) == 0)
        def _():
            dw8[...] = jnp.zeros_like(dw8)
            dvec8[...] = jnp.zeros_like(dvec8)
        _glu_into(uc_ref, vpad, S)
        dzpad[pl.ds(S, 2 * CONV_HALO), :] = jnp.zeros((2 * CONV_HALO, CONV_W), F32)

        def norm_bwd(i, carry):
            r0 = pl.multiple_of(i * CONV_CHUNK, CONV_CHUNK)
            xn, rstd, yln = _layer_norm(z_ref[0, pl.ds(r0, CONV_CHUNK), :], vec_ref)
            sg = _sigmoid(yln)
            dyln = dc_ref[0, pl.ds(r0, CONV_CHUNK), :] * (sg * (1.0 + yln * (1.0 - sg)))
            dxn = dyln * vec_ref[1:2, :]
            dz = rstd * (dxn - jnp.mean(dxn, axis=-1, keepdims=True) - xn * jnp.mean(dxn * xn, axis=-1, keepdims=True))
            dzpad[pl.ds(r0, CONV_CHUNK), :] = dz
            dvec8[0] += _fold8(dz)
            dvec8[1] += _fold8(dyln * xn)
            dvec8[2] += _fold8(dyln)
            return carry
        lax.fori_loop(0, n_chunks, norm_bwd, 0, unroll=2)

        def taps_bwd(i, carry):
            r0 = pl.multiple_of(i * CONV_CHUNK, CONV_CHUNK)
            dzwin = dzpad[pl.ds(r0, CONV_WIN), :]
            dv = _conv_taps(_shifted(dzwin, 0, copies), w_ref, lambda a, k: CONV_K - 1 - 8 * a - k)
            dz = dzwin[0:CONV_CHUNK]
            vwin = vpad[pl.ds(r0, CONV_WIN), :]
            for j, rows in _tap_slices(_shifted(vwin, CONV_HALO - (CONV_K - 1), copies), lambda a, k: 8 * a + k):
                dw8[j] += _fold8(dz * rows)
            a = uc_ref[0, pl.ds(r0, CONV_CHUNK), 0:CONV_W]
            sg = _sigmoid(uc_ref[0, pl.ds(r0, CONV_CHUNK), CONV_W:2 * CONV_W])
            duc_ref[0, pl.ds(r0, CONV_CHUNK), 0:CONV_W] = (dv * sg).astype(BF16)
            duc_ref[0, pl.ds(r0, CONV_CHUNK), CONV_W:2 * CONV_W] = (dv * a * sg * (1.0 - sg)).astype(BF16)
            return carry
        lax.fori_loop(0, n_chunks, taps_bwd, 0, unroll=2)

        @pl.when(pl.program_id(0) == NB - 1)
        def _():
            dw_ref[...] = jnp.zeros_like(dw_ref)
            dvec_ref[...] = jnp.zeros_like(dvec_ref)
            for j in range(CONV_K):
                dw_ref[j:j + 1, :] = jnp.sum(dw8[j], axis=0, keepdims=True)
            for j in range(3):
                dvec_ref[j:j + 1, :] = jnp.sum(dvec8[j], axis=0, keepdims=True)

    return _pallas(
        body, "conv_bwd", (NB,),
        [pl.BlockSpec((1, S, 2 * CONV_W), lambda b: (b, 0, 0)), pl.BlockSpec((1, S, CONV_W), lambda b: (b, 0, 0)),
         pl.BlockSpec((1, S, CONV_W), lambda b: (b, 0, 0)), _const((CONV_K, CONV_W)), _const((8, CONV_W))],
        [pl.BlockSpec((1, S, 2 * CONV_W), lambda b: (b, 0, 0)), pl.BlockSpec((32, CONV_W), lambda b: (0, 0)),
         pl.BlockSpec((8, CONV_W), lambda b: (0, 0))],
        [jax.ShapeDtypeStruct((NB, S, 2 * CONV_W), BF16), jax.ShapeDtypeStruct((32, CONV_W), F32),
         jax.ShapeDtypeStruct((8, CONV_W), F32)],
        (uc, z, dcact, dw_w, vec),
        scratch_shapes=[pltpu.VMEM((S + 2 * CONV_HALO, CONV_W), F32), pltpu.VMEM((S + 2 * CONV_HALO, CONV_W), F32),
                        pltpu.VMEM((CONV_K, 8, CONV_W), F32), pltpu.VMEM((3, 8, CONV_W), F32),
                        pltpu.VMEM((8, CONV_CHUNK + 24, CONV_W), F32)],
        sem=("arbitrary",), ride=ride)


def _rel_index_of_column(cols):
    offset = jnp.where(cols < KWIN, cols, cols - DS_LANES)
    return jnp.clip(KPAD - offset, -(CHUNK - 1), MAX_REL) + (CHUNK - 1)


def _bias_table(rel_bias, ride=None):
    def body(rb_ref, o_ref, by_offset, first8):
        ridx = _rel_index_of_column(lax.broadcasted_iota(jnp.int32, (1, DS_LANES), 1))
        onehot = (ridx == lax.broadcasted_iota(jnp.int32, (N_REL, 1), 0)).astype(F32)
        by_offset[...] = jnp.dot(rb_ref[...], onehot, preferred_element_type=F32, precision=lax.Precision.HIGHEST)
        sub = lax.broadcasted_iota(jnp.int32, (8, 1), 0)
        kchunk = lax.broadcasted_iota(jnp.int32, (1, KWIN), 1) // CHUNK
        for head in range(ATT_HEADS):
            base = jnp.broadcast_to(by_offset[head:head + 1, :], (8, DS_LANES))
            rows = base
            for s in range(1, 8):
                rows = jnp.where(sub == s, pltpu.roll(base, s, 1), rows)
            first8[head] = rows

        def rows8(q8, carry):
            qchunk = (q8 * 8 + sub) // CHUNK
            band = (kchunk >= qchunk) & (kchunk <= qchunk + LEFT_CHUNKS)
            for head in range(ATT_HEADS):
                tile = pltpu.roll(first8[head], q8 * 8, 1)[:, 0:KWIN]
                o_ref[head, pl.ds(pl.multiple_of(q8 * 8, 8), 8), :] = jnp.where(band, tile, MASK_VALUE)
            return carry
        lax.fori_loop(0, QB // 8, rows8, 0)

    outs, rode = _pallas(
        body, "bias_table", (1,), [pl.BlockSpec((ATT_HEADS, N_REL), lambda i: (0, 0))],
        [pl.BlockSpec((ATT_HEADS, QB, KWIN), lambda i: (0, 0, 0))], [jax.ShapeDtypeStruct((ATT_HEADS, QB, KWIN), F32)], (rel_bias,),
        scratch_shapes=[pltpu.VMEM((ATT_HEADS, DS_LANES), F32), pltpu.VMEM((ATT_HEADS, 8, DS_LANES), F32)],
        sem=("arbitrary",), ride=ride)
    return outs[0], rode


def _load_keys(i, k_ref, v_ref, kpad, vpad, S):
    @pl.when(i == 0)
    def _():
        kpad[pl.ds(0, KPAD), :] = jnp.zeros((KPAD, ATT_W), BF16)
        vpad[pl.ds(0, KPAD), :] = jnp.zeros((KPAD, ATT_W), BF16)
        kpad[pl.ds(KPAD, S), :] = k_ref[0]
        vpad[pl.ds(KPAD, S), :] = v_ref[0]


def _att_scores(q2s, k2, tab_ref, head, in_head, in_seq):
    qm = jnp.where(in_head, q2s, jnp.zeros_like(q2s))
    return jnp.where(in_seq, _dot_nt(qm, k2) + tab_ref[head], MASK_VALUE)


def _scaled(q2):
    return q2 * jnp.asarray(ATT_HD ** -0.5, q2.dtype)


def _att_fwd(qkv, tab, ride=None):
    NB, S, _ = qkv.shape

    def body(q_ref, k_ref, v_ref, tab_ref, o_ref, lse_ref, kpad, vpad):
        i = pl.program_id(1)
        _load_keys(i, k_ref, v_ref, kpad, vpad, S)
        koff = pl.multiple_of(i * QB, QB)
        lane = lax.broadcasted_iota(jnp.int32, (1, 128), 1)
        in_seq = (lax.broadcasted_iota(jnp.int32, (1, KWIN), 1) + i * QB) >= KPAD
        lse = jnp.zeros((QB, 128), F32)
        for pair in range(ATT_HEADS // 2):
            cols = slice(pair * 128, (pair + 1) * 128)
            q2s = _scaled(q_ref[0, :, cols])
            k2 = kpad[pl.ds(koff, KWIN), cols]
            v2 = vpad[pl.ds(koff, KWIN), cols]
            o2 = jnp.zeros((QB, 128), F32)
            for hh in range(2):
                head = 2 * pair + hh
                in_head = (lane // ATT_HD) == hh
                s = _att_scores(q2s, k2, tab_ref, head, in_head, in_seq)
                m = jnp.max(s, axis=-1, keepdims=True)
                e = jnp.exp(s - m)
                l = jnp.sum(e, axis=-1, keepdims=True)
                o2 = jnp.where(in_head, _dot(e.astype(BF16), v2) * (1.0 / l), o2)
                lse = jnp.where(lane == head, m + jnp.log(l), lse)
            o_ref[0, :, cols] = o2.astype(BF16)
        lse_ref[0] = lse

    seq = lambda col: pl.BlockSpec((1, S, ATT_W), lambda b, i: (b, 0, col), pipeline_mode=pl.Buffered(1))
    outs, rode = _pallas(
        body, "att_fwd", (NB, S // QB),
        [pl.BlockSpec((1, QB, ATT_W), lambda b, i: (b, i, 0)), seq(1), seq(2), _const((ATT_HEADS, QB, KWIN))],
        [pl.BlockSpec((1, QB, ATT_W), lambda b, i: (b, i, 0)), pl.BlockSpec((1, QB, 128), lambda b, i: (b, i, 0))],
        [jax.ShapeDtypeStruct((NB, S, ATT_W), BF16), jax.ShapeDtypeStruct((NB, S, 128), F32)],
        (qkv, qkv, qkv, tab),
        scratch_shapes=[pltpu.VMEM((S + KPAD, ATT_W), BF16), pltpu.VMEM((S + KPAD, ATT_W), BF16)],
        sem=("arbitrary", "arbitrary"), ride=ride)
    return outs[0], outs[1], rode


def _att_bwd(qkv, o, lse, do, tab, ride=None):
    NB, S, _ = qkv.shape
    nq = S // QB

    def body(q_ref, k_ref, v_ref, o_ref, lse_ref, do_ref, tab_ref, dqkv_ref, ds_hbm, kpad, vpad, dkpad, dvpad, ds_acc, ds_sem):
        b, i = pl.program_id(0), pl.program_id(1)
        _load_keys(i, k_ref, v_ref, kpad, vpad, S)

        @pl.when(i == 0)
        def _():
            dkpad[...] = jnp.zeros_like(dkpad)
            dvpad[...] = jnp.zeros_like(dvpad)

        @pl.when((i == 0) & (b == 0))
        def _():
            ds_acc[...] = jnp.zeros_like(ds_acc)

        koff = pl.multiple_of(i * QB, QB)
        lane = lax.broadcasted_iota(jnp.int32, (1, 128), 1)
        in_seq = (lax.broadcasted_iota(jnp.int32, (1, KWIN), 1) + i * QB) >= KPAD
        for pair in range(ATT_HEADS // 2):
            cols = slice(pair * 128, (pair + 1) * 128)
            q2s = _scaled(q_ref[0, :, cols])
            do2 = do_ref[0, :, cols]
            k2 = kpad[pl.ds(koff, KWIN), cols]
            v2 = vpad[pl.ds(koff, KWIN), cols]
            do_o = do2.astype(F32) * o_ref[0, :, cols].astype(F32)
            dq2 = jnp.zeros((QB, 128), F32)
            dk2 = jnp.zeros((KWIN, 128), F32)
            dv2 = jnp.zeros((KWIN, 128), F32)
            for hh in range(2):
                head = 2 * pair + hh
                in_head = (lane // ATT_HD) == hh
                p = jnp.exp(_att_scores(q2s, k2, tab_ref, head, in_head, in_seq) - lse_ref[0, :, head:head + 1])
                row_term = jnp.sum(jnp.where(in_head, do_o, 0.0), axis=-1, keepdims=True)
                dom = jnp.where(in_head, do2, jnp.zeros_like(do2))
                ds = p * (_dot_nt(dom, v2) - row_term)
                ds_acc[head] += ds
                dsb = ds.astype(BF16)
                dq2 = jnp.where(in_head, _dot(dsb, k2), dq2)
                dk2 = jnp.where(in_head, _dot_tn(dsb, q2s), dk2)
                dv2 = jnp.where(in_head, _dot_tn(p.astype(BF16), do2), dv2)
            dqkv_ref[0, pl.ds(koff, QB), cols] = (dq2 * (ATT_HD ** -0.5)).astype(BF16)
            dkpad[pl.ds(koff, KWIN), cols] += dk2
            dvpad[pl.ds(koff, KWIN), cols] += dv2

        @pl.when(i == nq - 1)
        def _():
            dqkv_ref[0, :, ATT_W:2 * ATT_W] = dkpad[pl.ds(KPAD, S), :].astype(BF16)
            dqkv_ref[0, :, 2 * ATT_W:3 * ATT_W] = dvpad[pl.ds(KPAD, S), :].astype(BF16)

        @pl.when((i == nq - 1) & (b == NB - 1))
        def _():
            out = pltpu.make_async_copy(ds_acc, ds_hbm, ds_sem)
            out.start()
            out.wait()

    seq = lambda col: pl.BlockSpec((1, S, ATT_W), lambda b, i: (b, 0, col), pipeline_mode=pl.Buffered(1))
    rows = pl.BlockSpec((1, QB, ATT_W), lambda b, i: (b, i, 0))
    return _pallas(
        body, "att_bwd", (NB, nq),
        [rows, seq(1), seq(2), rows, pl.BlockSpec((1, QB, 128), lambda b, i: (b, i, 0)), rows, _const((ATT_HEADS, QB, KWIN))],
        [pl.BlockSpec((1, S, 3 * ATT_W), lambda b, i: (b, 0, 0)), ANY],
        [jax.ShapeDtypeStruct((NB, S, 3 * ATT_W), BF16), jax.ShapeDtypeStruct((ATT_HEADS, QB, KWIN), F32)],
        (qkv, qkv, qkv, o, lse, do, tab),
        scratch_shapes=[pltpu.VMEM((S + KPAD, ATT_W), BF16), pltpu.VMEM((S + KPAD, ATT_W), BF16),
                        pltpu.VMEM((S + KPAD, ATT_W), F32), pltpu.VMEM((S + KPAD, ATT_W), F32),
                        pltpu.VMEM((ATT_HEADS, QB, KWIN), F32), pltpu.SemaphoreType.DMA],
        sem=("arbitrary", "arbitrary"), ride=ride)


def _rel_bias_grad(ds):
    def body(ds_ref, o_ref):
        sub = lax.broadcasted_iota(jnp.int32, (8, 1), 0)
        ridx = _rel_index_of_column(lax.broadcasted_iota(jnp.int32, (DS_LANES, 1), 0))
        onehot = (ridx == lax.broadcasted_iota(jnp.int32, (1, N_REL), 1)).astype(F32)
        def rows8(q8, accs):
            shift = lax.rem(DS_LANES - q8 * 8, DS_LANES)
            out = []
            for head in range(ATT_HEADS):
                tile = ds_ref[head, pl.ds(pl.multiple_of(q8 * 8, 8), 8), :]
                tile = jnp.concatenate([tile, jnp.zeros((8, DS_LANES - KWIN), F32)], axis=1)
                out.append(accs[head] + pltpu.roll(tile, shift, 1))
            return tuple(out)
        accs = lax.fori_loop(0, QB // 8, rows8, tuple(jnp.zeros((8, DS_LANES), F32) for _ in range(ATT_HEADS)))
        for head in range(ATT_HEADS):
            acc = accs[head]
            diag = jnp.zeros((8, DS_LANES), F32)
            for s in range(8):
                shifted = acc if s == 0 else pltpu.roll(acc, DS_LANES - s, 1)
                diag = jnp.where(sub == s, shifted, diag)
            z = jnp.sum(diag, axis=0, keepdims=True)
            o_ref[head:head + 1, :] = jnp.dot(z, onehot, preferred_element_type=F32, precision=lax.Precision.HIGHEST)

    return pl.pallas_call(body, out_shape=jax.ShapeDtypeStruct((ATT_HEADS, N_REL), F32), name="rel_bias_grad",
                          compiler_params=_params())(ds)


def _memkv_fwd(mem, g, w_kv, tm):
    R = mem.shape[0]
    tm = min(tm, R)

    def body(m_ref, g_ref, w_ref, h_ref, kv_ref):
        xhat, _ = _rms_stats(m_ref[...])
        h = (xhat * g_ref[...]).astype(BF16)
        h_ref[...] = h
        kv_ref[...] = _dot(h, w_ref[...]).astype(BF16)

    row = pl.BlockSpec((tm, D), lambda t: (t, 0))
    return pl.pallas_call(
        body, grid=(R // tm,), name="memkv_fwd", in_specs=[row, _const((1, D)), _const((D, 2 * MEM_W))], out_specs=[row, row],
        out_shape=[jax.ShapeDtypeStruct((R, D), BF16), jax.ShapeDtypeStruct((R, 2 * MEM_W), BF16)],
        compiler_params=_params("parallel"),
    )(mem, g, w_kv)


def _memkv_bwd(mem, dkv, w_kv, tm):
    R = mem.shape[0]
    tm = min(tm, R)

    def body(m_ref, dkv_ref, w_ref, dg_ref):
        xhat, _ = _rms_stats(m_ref[...])
        dh = _dot_nt(dkv_ref[...].astype(BF16), w_ref[...])

        @pl.when(pl.program_id(0) == 0)
        def _():
            dg_ref[...] = jnp.zeros_like(dg_ref)
        dg_ref[...] += jnp.sum(dh * xhat, axis=0, keepdims=True)

    row = pl.BlockSpec((tm, D), lambda t: (t, 0))
    return pl.pallas_call(
        body, grid=(R // tm,), name="memkv_bwd", in_specs=[row, row, _const((D, 2 * MEM_W))],
        out_specs=pl.BlockSpec((1, D), lambda t: (0, 0)), out_shape=jax.ShapeDtypeStruct((1, D), F32),
        compiler_params=_params("arbitrary"),
    )(mem, dkv, w_kv)


def _mem_probs(qh, kh):
    s = _dot_nt(qh, kh) * (MEM_HD ** -0.5)
    e = jnp.exp(s - jnp.max(s, axis=-1, keepdims=True))
    return e * (1.0 / jnp.sum(e, axis=-1, keepdims=True))


def _mematt_fwd(mq, kv, tq):
    NB, S, _ = mq.shape
    M = kv.shape[1]

    def body(q_ref, kv_ref, o_ref):
        for h in range(MEM_HEADS):
            cols = slice(h * MEM_HD, (h + 1) * MEM_HD)
            p = _mem_probs(q_ref[0, :, cols], kv_ref[0, :, cols])
            o_ref[0, :, cols] = _dot(p.astype(BF16), kv_ref[0, :, MEM_W + h * MEM_HD:MEM_W + (h + 1) * MEM_HD]).astype(BF16)

    return pl.pallas_call(
        body, grid=(NB, S // tq), name="mematt_fwd",
        in_specs=[pl.BlockSpec((1, tq, MEM_W), lambda b, i: (b, i, 0)), pl.BlockSpec((1, M, 2 * MEM_W), lambda b, i: (b, 0, 0))],
        out_specs=pl.BlockSpec((1, tq, MEM_W), lambda b, i: (b, i, 0)),
        out_shape=jax.ShapeDtypeStruct((NB, S, MEM_W), BF16), compiler_params=_params("parallel", "parallel"),
    )(mq, kv)


def _mematt_bwd(mq, kv, do, tq):
    NB, S, _ = mq.shape
    M = kv.shape[1]

    def body(q_ref, kv_ref, do_ref, dq_ref, dkv_ref):
        @pl.when(pl.program_id(1) == 0)
        def _():
            dkv_ref[...] = jnp.zeros_like(dkv_ref)
        for h in range(MEM_HEADS):
            cols = slice(h * MEM_HD, (h + 1) * MEM_HD)
            vcols = slice(MEM_W + h * MEM_HD, MEM_W + (h + 1) * MEM_HD)
            qh, kh, vh, doh = q_ref[0, :, cols], kv_ref[0, :, cols], kv_ref[0, :, vcols], do_ref[0, :, cols]
            p = _mem_probs(qh, kh)
            dp = _dot_nt(doh, vh)
            ds = p * (dp - jnp.sum(p * dp, axis=-1, keepdims=True))
            dss = (ds * (MEM_HD ** -0.5)).astype(BF16)
            dq_ref[0, :, cols] = _dot(dss, kh).astype(BF16)
            dkv_ref[0, :, cols] += _dot_tn(dss, qh)
            dkv_ref[0, :, vcols] += _dot_tn(p.astype(BF16), doh)

    qspec = pl.BlockSpec((1, tq, MEM_W), lambda b, i: (b, i, 0))
    kvspec = pl.BlockSpec((1, M, 2 * MEM_W), lambda b, i: (b, 0, 0))
    return pl.pallas_call(
        body, grid=(NB, S // tq), name="mematt_bwd", in_specs=[qspec, kvspec, qspec], out_specs=[qspec, kvspec],
        out_shape=[jax.ShapeDtypeStruct((NB, S, MEM_W), BF16), jax.ShapeDtypeStruct((NB, M, 2 * MEM_W), F32)],
        compiler_params=_params("arbitrary", "arbitrary"),
    )(mq, kv, do)


def _branch(j, in_ref, w_ref, gl_ref, bg_ref):
    y = _dot(in_ref[...], w_ref[...])
    gate = _sigmoid(gl_ref[:, j * D:(j + 1) * D].astype(F32) + bg_ref[:, j * D:(j + 1) * D])
    return y, gate


def _combine_fwd(x, cact, oatt, omem, gl, bg, wpw, wo, wmo, wout, tm):
    T = x.shape[0]

    def body(x_ref, c_ref, a_ref, m_ref, gl_ref, bg_ref, wpw_ref, wo_ref, wmo_ref, wout_ref, xo_ref, y_ref):
        y = None
        for j, (in_ref, w_ref) in enumerate(((c_ref, wpw_ref), (a_ref, wo_ref), (m_ref, wmo_ref))):
            yj, gate = _branch(j, in_ref, w_ref, gl_ref, bg_ref)
            y = gate * yj if y is None else y + gate * yj
        y = y.astype(BF16)
        y_ref[...] = y
        xo_ref[...] = x_ref[...] + _dot(y, wout_ref[...])

    row = lambda w: pl.BlockSpec((tm, w), lambda t: (t, 0))
    wbr = _const((512, D))
    return pl.pallas_call(
        body, grid=(T // tm,), name="combine_fwd",
        in_specs=[row(D), row(512), row(512), row(512), row(3 * D), _const((1, 3 * D)), wbr, wbr, wbr, _const((D, D))],
        out_specs=[row(D), row(D)],
        out_shape=[jax.ShapeDtypeStruct((T, D), F32), jax.ShapeDtypeStruct((T, D), BF16)],
        compiler_params=_params("parallel"),
    )(x, cact, oatt, omem, gl, bg, wpw, wo, wmo, wout)


def _combine_bwd(dx, cact, oatt, omem, gl, bg, wpw, wo, wmo, wout, tm, ride=None):
    T = dx.shape[0]

    def body(dx_ref, c_ref, a_ref, m_ref, gl_ref, bg_ref, wpw_ref, wo_ref, wmo_ref, wout_ref,
             dgl_ref, dc_ref, da_ref, dm_ref, dyc_ref, dya_ref, dym_ref, dbg_ref):
        dy = _dot_nt(dx_ref[...].astype(BF16), wout_ref[...])

        @pl.when(pl.program_id(0) == 0)
        def _():
            dbg_ref[...] = jnp.zeros_like(dbg_ref)
        branches = ((c_ref, wpw_ref, dyc_ref, dc_ref), (a_ref, wo_ref, dya_ref, da_ref), (m_ref, wmo_ref, dym_ref, dm_ref))
        for j, (in_ref, w_ref, dyb_ref, din_ref) in enumerate(branches):
            yj, gate = _branch(j, in_ref, w_ref, gl_ref, bg_ref)
            dyg = dy * gate
            dlogit = dyg * yj * (1.0 - gate)
            dgl_ref[:, j * D:(j + 1) * D] = dlogit.astype(BF16)
            dbg_ref[:, j * D:(j + 1) * D] += jnp.sum(dlogit, axis=0, keepdims=True)
            dyb = dyg.astype(BF16)
            dyb_ref[...] = dyb
            din_ref[...] = _dot_nt(dyb, w_ref[...]).astype(din_ref.dtype)

    row = lambda w: pl.BlockSpec((tm, w), lambda t: (t, 0))
    wbr = _const((512, D))
    sds = jax.ShapeDtypeStruct
    return _pallas(
        body, "combine_bwd", (T // tm,),
        [row(D), row(512), row(512), row(512), row(3 * D), _const((1, 3 * D)), wbr, wbr, wbr, _const((D, D))],
        [row(3 * D), row(512), row(512), row(512), row(D), row(D), row(D), pl.BlockSpec((1, 3 * D), lambda t: (0, 0))],
        [sds((T, 3 * D), BF16), sds((T, 512), F32), sds((T, 512), BF16), sds((T, 512), BF16),
         sds((T, D), BF16), sds((T, D), BF16), sds((T, D), BF16), sds((1, 3 * D), F32)],
        (dx, cact, oatt, omem, gl, bg, wpw, wo, wmo, wout), sem=("arbitrary",), ride=ride)


def _peer(x, y, c, rel):
    rx, ry, rc = (rel >> 2) & 1, (rel >> 1) & 1, rel & 1
    return ((1 - x) if rx else x, (1 - y) if ry else y, (1 - c) if rc else c)


def _all_sum_small(parts):
    n = len(parts)

    def body(*refs):
        p_refs, o_refs, slots = refs[:n], refs[n:2 * n], refs[2 * n:3 * n]
        send_sems, recv_sems = refs[3 * n:]
        x, y, c = _my_coords()
        me = _dev_index(x, y, c)

        def copy(i, rel, arrival):
            peer = _peer(x, y, c, rel)
            return pltpu.make_async_remote_copy(
                src_ref=p_refs[i], dst_ref=slots[i].at[_dev_index(*peer) if arrival else me],
                send_sem=send_sems.at[i, rel - 1], recv_sem=recv_sems.at[i, rel - 1], device_id=peer, device_id_type=MESH)

        for i in range(n):
            slots[i][me] = p_refs[i][...]
        for rel in range(1, NDEV):
            for i in range(n):
                copy(i, rel, False).start()
        for rel in range(1, NDEV):
            for i in range(n):
                copy(i, rel, True).wait_recv()
        for rel in range(1, NDEV):
            for i in range(n):
                copy(i, rel, False).wait_send()
        for i in range(n):
            total = slots[i][0]
            for d in range(1, NDEV):
                total = total + slots[i][d]
            o_refs[i][...] = total

    vmem = pl.BlockSpec(memory_space=pltpu.VMEM)
    return pl.pallas_call(
        body, out_shape=[jax.ShapeDtypeStruct(p.shape, F32) for p in parts], name="all_sum_small",
        in_specs=[vmem] * n, out_specs=[vmem] * n,
        scratch_shapes=[pltpu.VMEM((NDEV,) + p.shape, F32) for p in parts]
        + [pltpu.SemaphoreType.DMA((n, NDEV - 1)), pltpu.SemaphoreType.DMA((n, NDEV - 1))],
        compiler_params=pltpu.CompilerParams(has_side_effects=True),
    )(*parts)


HBM = pl.BlockSpec(memory_space=pltpu.HBM)
SEM = pl.BlockSpec(memory_space=pltpu.SEMAPHORE)


def _own_block(g, kind, m, tag):
    def body(g_ref, land_ref, staged, sem):
        me = _dev_index(*_my_coords())
        for cp in (pltpu.make_async_copy(_window(g_ref, kind, m, me), staged, sem),
                   pltpu.make_async_copy(staged, land_ref.at[me], sem)):
            cp.start()
            cp.wait()

    block = (m, g.shape[1]) if kind == 'row' else (g.shape[0], m)
    return pl.pallas_call(body, in_specs=[ANY], out_specs=ANY, out_shape=jax.ShapeDtypeStruct((NDEV,) + block, g.dtype),
                          scratch_shapes=[pltpu.VMEM(block, g.dtype), pltpu.SemaphoreType.DMA], name="own_block_" + tag)(g)


def _scatter_start(g, land, kind, m, tag):
    def body(g_ref, land_ref, send_sems, recv_sems, g_thru, land_thru, token):
        x, y, c = _my_coords()
        me = _dev_index(x, y, c)
        for rel in range(1, NDEV):
            peer = _peer(x, y, c, rel)
            pltpu.make_async_remote_copy(src_ref=_window(g_ref, kind, m, _dev_index(*peer)), dst_ref=land_ref.at[me],
                                         send_sem=send_sems.at[rel - 1], recv_sem=recv_sems.at[rel - 1],
                                         device_id=peer, device_id_type=MESH).start()
        token[...] = jnp.zeros_like(token)

    return pl.pallas_call(
        body, name="scatter_start_" + tag,
        out_shape=(pltpu.SemaphoreType.DMA((NDEV - 1,)), pltpu.SemaphoreType.DMA((NDEV - 1,)), pltpu.HBM(g.shape, g.dtype),
                   pltpu.HBM(land.shape, land.dtype), jax.ShapeDtypeStruct((8, 128), F32)),
        in_specs=(HBM, HBM), out_specs=(SEM, SEM, HBM, HBM, pl.BlockSpec(memory_space=pltpu.VMEM)),
        input_output_aliases={0: 2, 1: 3},
        compiler_params=pltpu.CompilerParams(has_side_effects=pltpu.SideEffectType.DATAFLOW_SIDE_EFFECTING),
    )(pltpu.with_memory_space_constraint(g, pltpu.HBM), pltpu.with_memory_space_constraint(land, pltpu.HBM))


def _scatter_wait(send_sems, recv_sems, g_thru, land_thru, after, kind, m, tag):
    n_after = len(after)

    def body(*refs):
        g_ref, land_ref, send_sems, recv_sems = refs[:4]
        x, y, c = _my_coords()
        me = _dev_index(x, y, c)
        for rel in range(1, NDEV):
            peer = _peer(x, y, c, rel)
            dev = _dev_index(*peer)
            cp = pltpu.make_async_remote_copy(src_ref=_window(g_ref, kind, m, me), dst_ref=land_ref.at[dev],
                                              send_sem=send_sems.at[rel - 1], recv_sem=recv_sems.at[rel - 1],
                                              device_id=peer, device_id_type=MESH)
            cp.wait_send()
            cp.wait_recv()

    return pl.pallas_call(
        body, name="scatter_wait_" + tag,
        out_shape=(pltpu.HBM(g_thru.shape, g_thru.dtype), pltpu.HBM(land_thru.shape, land_thru.dtype)),
        in_specs=(HBM, HBM, SEM, SEM) + (ANY,) * n_after, out_specs=(HBM, HBM), input_output_aliases={0: 0, 1: 1},
        compiler_params=pltpu.CompilerParams(has_side_effects=pltpu.SideEffectType.DATAFLOW_SIDE_EFFECTING),
    )(g_thru, land_thru, send_sems, recv_sems, *after)[1]


def _adamw_math(w, g, m, v):
    m = ADAM_B1 * m + (1.0 - ADAM_B1) * g
    v = ADAM_B2 * v + (1.0 - ADAM_B2) * (g * g)
    m_hat = m / (1.0 - ADAM_B1 ** ADAM_STEP)
    v_hat = v / (1.0 - ADAM_B2 ** ADAM_STEP)
    delta = -ADAM_LR * (m_hat / (jnp.sqrt(v_hat) + ADAM_EPS) + ADAM_WD * w)
    return delta, m, v


def _sum_adamw(parts, w, m, v, name, after=None):
    R, C = w.shape
    n_parts = len(parts)
    cg = C // n_parts
    tr = max(t for t in range(8, 257, 8) if R % t == 0)
    deps = [] if after is None else [after]

    def body(*refs):
        p_refs = refs[:n_parts]
        w_ref, m_ref, v_ref = refs[n_parts:n_parts + 3]
        g_ref, d_ref, mo_ref, vo_ref = refs[n_parts + 3 + len(deps):]
        for k, p_ref in enumerate(p_refs):
            @pl.when(pl.program_id(0) == k)
            def _():
                g = p_ref[0].astype(F32)
                for d in range(1, NDEV):
                    g = g + p_ref[d].astype(F32)
                g_ref[...] = g
                d_ref[...], mo_ref[...], vo_ref[...] = _adamw_math(w_ref[...], g, m_ref[...], v_ref[...])

    part = pl.BlockSpec((NDEV, tr, cg), lambda k, t: (0, t, 0))
    blk = pl.BlockSpec((tr, cg), lambda k, t: (t, k))
    return pl.pallas_call(
        body, grid=(n_parts, R // tr), name=name, in_specs=[part] * n_parts + [blk, blk, blk] + [ANY] * len(deps),
        out_specs=[blk] * 4, out_shape=[jax.ShapeDtypeStruct((R, C), F32)] * 4, compiler_params=_params("parallel", "parallel"),
    )(*parts, w, m, v, *deps)


def _adamw_small(ws, gs, ms, vs):
    n = len(ws)

    def body(*refs):
        w_refs, g_refs, m_refs, v_refs = (refs[k * n:(k + 1) * n] for k in range(4))
        d_refs, mo_refs, vo_refs = (refs[(4 + k) * n:(5 + k) * n] for k in range(3))
        for i in range(n):
            d_refs[i][...], mo_refs[i][...], vo_refs[i][...] = _adamw_math(w_refs[i][...], g_refs[i][...], m_refs[i][...], v_refs[i][...])

    shapes = [jax.ShapeDtypeStruct(a.shape, F32) for a in ws]
    outs = pl.pallas_call(body, out_shape=shapes * 3, name="adamw_small", compiler_params=_params())(*ws, *gs, *ms, *vs)
    return outs[:n], outs[n:2 * n], outs[2 * n:]


def kernel(x, mem, ffn1_norm, ffn1_w_up, ffn1_w_down, mix_norm, mem_norm, w_in, b_gate, conv_dw_w, conv_dw_b, conv_ln_g, conv_ln_b, conv_w_pw, att_rel_bias, att_w_o, mem_w_kv, mem_w_o, w_out, ffn2_norm, ffn2_w_up, ffn2_w_down, final_norm, loss_target, m_ffn1_norm, m_ffn1_w_up, m_ffn1_w_down, m_mix_norm, m_mem_norm, m_w_in, m_b_gate, m_conv_dw_w, m_conv_dw_b, m_conv_ln_g, m_conv_ln_b, m_conv_w_pw, m_att_rel_bias, m_att_w_o, m_mem_w_kv, m_mem_w_o, m_w_out, m_ffn2_norm, m_ffn2_w_up, m_ffn2_w_down, m_final_norm, v_ffn1_norm, v_ffn1_w_up, v_ffn1_w_down, v_mix_norm, v_mem_norm, v_w_in, v_b_gate, v_conv_dw_w, v_conv_dw_b, v_conv_ln_g, v_conv_ln_b, v_conv_w_pw, v_att_rel_bias, v_att_w_o, v_mem_w_kv, v_mem_w_o, v_w_out, v_ffn2_norm, v_ffn2_w_up, v_ffn2_w_down, v_final_norm):
    given = dict(locals())
    w = {n: given[n] for n in WEIGHTS}
    mom = {n: given["m_" + n] for n in WEIGHTS}
    var = {n: given["v_" + n] for n in WEIGHTS}

    NB, S, _ = x.shape
    T = NB * S
    ML = mem.shape[1]
    x0 = x.reshape(T, D)
    target = loss_target.reshape(T, D)
    mem2 = mem.reshape(NB * ML, D)

    def block(t, n):
        return jnp.transpose(t[0]) if n in TRANSPOSED else t[0]

    sh = dict(zip(BIG_ORDER, _cast_shards([block(w[n], n) for n in BIG_ORDER])))
    dw_t = jnp.transpose(conv_dw_w[0])

    def gather(names, extra=(), extra_kinds=()):
        return _gather_ride([sh[n] for n in names] + list(extra), [BIG[n] for n in names] + list(extra_kinds))

    W = {}
    names0 = ['ffn1_w_up', 'ffn1_w_down']
    tab, got = _bias_table(att_rel_bias[0], ride=gather(names0, [dw_t], [('row', dw_t.shape[0])]))
    W.update(zip(names0, got[:2]))
    dw_full = jnp.transpose(got[2])
    conv_vec = jnp.concatenate([conv_dw_b, conv_ln_g, conv_ln_b, jnp.zeros((5, CONV_W), F32)], axis=0)
    fin_g = final_norm.reshape(1, D)

    names1 = ['w_in', 'conv_w_pw', 'att_w_o', 'mem_w_kv', 'mem_w_o', 'w_out']
    (x1, ab1), got = _ffn_fwd(x0, ffn1_norm, W['ffn1_w_up'], W['ffn1_w_down'], TILE_FFN, "ffn1_fwd", ride=gather(names1))
    W.update(zip(names1, got))
    (uc, qkv, mq, gl, hmix), _ = _mix_fwd(x1, mix_norm, W['w_in'], TILE_TOKENS)
    uc3 = uc.reshape(NB, S, 2 * CONV_W)
    qkv3 = qkv.reshape(NB, S, 3 * ATT_W)
    mq3 = mq.reshape(NB, S, MEM_W)
    cact, conv_z = _conv_fwd(uc3, dw_full, conv_vec)
    cact = cact.reshape(T, CONV_W)
    names2 = ['ffn2_w_up', 'ffn2_w_down']
    oatt3, att_lse, got = _att_fwd(qkv3, tab, ride=gather(names2))
    W.update(zip(names2, got))
    oatt = oatt3.reshape(T, ATT_W)
    memh, kv = _memkv_fwd(mem2, mem_norm, W['mem_w_kv'], TILE_TOKENS)
    kv3 = kv.reshape(NB, ML, 2 * MEM_W)
    omem = _mematt_fwd(mq3, kv3, TILE_TOKENS).reshape(T, MEM_W)
    branch_w = (W['conv_w_pw'], W['att_w_o'], W['mem_w_o'], W['w_out'])
    x2, ymix = _combine_fwd(x1, cact, oatt, omem, gl, b_gate, *branch_w, TILE_COMBINE)
    dx3, ab2, loss_part, dg_final = _ffn_fwd_loss(x2, ffn2_norm, W['ffn2_w_up'], W['ffn2_w_down'], fin_g, target, TILE_FFN,
                                                  "ffn2_fwd_loss")

    def scatter(grads, names):
        return _scatter_ride(grads, [BIG[n] for n in names])

    G, P = {}, {}
    dx2, dab2, act2, h2, dg_ffn2 = _ffn_bwd(x2, dx3, ab2, ffn2_norm, W['ffn2_w_up'], W['ffn2_w_down'], TILE_FFN, "ffn2_bwd")
    g_up, _ = _tn_matmul(dab2, h2, 512, "grad_ffn2_w_up_a", tt=TILE_GRAD_TOKENS_WIDE, x_part=(0, 2), out_rows=2 * FF)
    G['ffn2_w_up'], _ = _tn_matmul(dab2, h2, 512, "grad_ffn2_w_up_b", tt=TILE_GRAD_TOKENS_WIDE, x_part=(1, 2), out_rows=2 * FF,
                                   prev=g_up)
    G['ffn2_w_down'], _ = _tn_matmul(act2, dx3, 512, "grad_ffn2_w_down", scale=0.5, tt=TILE_GRAD_TOKENS_WIDE)
    (dgl, dcact, doatt, domem, dyc, dya, dym, dbg), got = _combine_bwd(
        dx2, cact, oatt, omem, gl, b_gate, *branch_w, TILE_COMBINE, ride=scatter([G['ffn2_w_up']], ['ffn2_w_up']))
    P['ffn2_w_up'] = got
    G['w_out'], _ = _tn_matmul(ymix, dx2, D, "grad_w_out", tt=TILE_GRAD_TOKENS_WIDE)
    G['conv_w_pw'], _ = _tn_matmul(cact, dyc, D, "grad_conv_w_pw")
    G['att_w_o'], _ = _tn_matmul(oatt, dya, D, "grad_att_w_o")
    G['mem_w_o'], _ = _tn_matmul(omem, dym, D, "grad_mem_w_o")
    dmq3, dkv3 = _mematt_bwd(mq3, kv3, domem.reshape(NB, S, MEM_W), TILE_TOKENS)
    dkv = dkv3.reshape(NB * ML, 2 * MEM_W)
    dg_mem = _memkv_bwd(mem2, dkv, W['mem_w_kv'], TILE_TOKENS)
    G['mem_w_kv'], _ = _tn_matmul(memh, dkv, 512, "grad_mem_w_kv")
    names = ['ffn2_w_down', 'w_out', 'conv_w_pw', 'att_w_o', 'mem_w_o']
    (dqkv3, dscore), got = _att_bwd(qkv3, oatt3, att_lse, doatt.reshape(NB, S, ATT_W), tab,
                                    ride=scatter([G[n] for n in names], names))
    P.update((n, [p]) for n, p in zip(names, got))
    d_rel = _rel_bias_grad(dscore)
    (duc3, d_dw, d_cvec), got = _conv_bwd(uc3, conv_z, dcact.reshape(NB, S, CONV_W), dw_full, conv_vec,
                                          ride=scatter([G['mem_w_kv']], ['mem_w_kv']))
    P['mem_w_kv'] = got
    duc, dqkv, dmq = duc3.reshape(T, 2 * CONV_W), dqkv3.reshape(T, 3 * ATT_W), dmq3.reshape(T, MEM_W)
    g_in, _ = _tn_matmul(hmix, duc, 1024, "grad_w_in_conv", out_cols=IN_COLS, col_off=0)
    g_in, _ = _tn_matmul(hmix, dqkv, 512, "grad_w_in_qkv", out_cols=IN_COLS, col_off=1024, prev=g_in)
    g_in, _ = _tn_matmul(hmix, dmq, 512, "grad_w_in_mq", out_cols=IN_COLS, col_off=2560, prev=g_in)
    G['w_in'], _ = _tn_matmul(hmix, dgl, 1024, "grad_w_in_gate", out_cols=IN_COLS, col_off=3072, prev=g_in)
    def start_scatter(g, name, tag):
        kind = BIG[name]
        return _scatter_start(g, _own_block(g, *kind, tag), *kind, tag) + (kind, tag)

    def wait_scatter(started, after):
        send_sems, recv_sems, g_thru, land_thru, _, kind, tag = started
        return _scatter_wait(send_sems, recv_sems, g_thru, land_thru, after, *kind, tag)

    ex_in = start_scatter(G['w_in'], 'w_in', "w_in")
    (dx1, dg_mix), _ = _mix_bwd(x1, dx2, duc, dqkv, dmq, dgl, mix_norm, W['w_in'], TILE_TOKENS, after=ex_in[4])
    dx0, dab1, act1, h1, dg_ffn1 = _ffn_bwd(x0, dx1, ab1, ffn1_norm, W['ffn1_w_up'], W['ffn1_w_down'], TILE_FFN, "ffn1_bwd")
    g_wd1, _ = _tn_matmul(act1, dx1, 512, "grad_ffn1_w_down", scale=0.5, tt=TILE_GRAD_TOKENS_WIDE)
    ex_wd = start_scatter(g_wd1, 'ffn1_w_down', "ffn1_w_down")
    g_wu1a, _ = _tn_matmul(dab1, h1, 512, "grad_ffn1_w_up_a", tt=TILE_GRAD_TOKENS_WIDEST, y_part=(0, 2), after=ex_wd[4])
    ex_a = start_scatter(g_wu1a, 'ffn1_w_up', "ffn1_w_up_a")
    g_wu1b, _ = _tn_matmul(dab1, h1, 512, "grad_ffn1_w_up_b", tt=TILE_GRAD_TOKENS_WIDEST, y_part=(1, 2), after=ex_a[4])
    ex_b = start_scatter(g_wu1b, 'ffn1_w_up', "ffn1_w_up_b")
    token = ex_b[4]

    small_names = ['loss', 'ffn1_norm', 'mix_norm', 'mem_norm', 'b_gate', 'conv_dw_w', 'conv_vec', 'att_rel_bias', 'ffn2_norm',
                   'final_norm']
    small = dict(zip(small_names, _all_sum_small(
        [loss_part + token[0:1], dg_ffn1, dg_mix, dg_mem, dbg, d_dw, d_cvec, d_rel, dg_ffn2, dg_final])))
    loss = small['loss'][0, 0]
    me = _dev_index(*_my_coords())
    for i, n in enumerate(['conv_dw_b', 'conv_ln_g', 'conv_ln_b']):
        small[n] = small['conv_vec'][i:i + 1]
    small['conv_dw_w'] = lax.dynamic_slice(small['conv_dw_w'], (0, me * conv_dw_w.shape[2]), (CONV_K, conv_dw_w.shape[2]))
    little = [n for n in WEIGHTS if n not in BIG]
    as2d = lambda t, n: t.reshape(small[n].shape)
    d_s, m_s, v_s = _adamw_small([as2d(w[n], n) for n in little], [small[n] for n in little],
                                 [as2d(mom[n], n) for n in little], [as2d(var[n], n) for n in little])
    grad, delta, new_m, new_v = {}, {}, {}, {}
    for i, n in enumerate(little):
        grad[n], delta[n], new_m[n], new_v[n] = (t.reshape(w[n].shape) for t in (small[n], d_s[i], m_s[i], v_s[i]))
    done = [d_s[0]]
    waited = {'w_in': [ex_in], 'ffn1_w_down': [ex_wd], 'ffn1_w_up': [ex_a, ex_b]}
    order = [n for n in BIG_ORDER if n not in waited] + list(waited)
    for n in order:
        if n in waited:
            P[n] = [wait_scatter(ex, done) for ex in waited[n]]
        outs = _sum_adamw(P[n], block(w[n], n), block(mom[n], n), block(var[n], n), "adamw_" + n,
                          after=None if n in waited else token)
        done.append(outs[0])
        grad[n], delta[n], new_m[n], new_v[n] = ((jnp.transpose(t) if n in TRANSPOSED else t)[None] for t in outs)

    return (loss, dx0.reshape(NB, S, D), *[grad[n] for n in WEIGHTS], *[delta[n] for n in WEIGHTS],
            *[new_m[n] for n in WEIGHTS], *[new_v[n] for n in WEIGHTS])
```

```python
import functools

import jax
import jax.numpy as jnp
from jax import lax
from jax.experimental import pallas as pl
from jax.experimental.pallas import tpu as pltpu

F32 = jnp.float32
BF16 = jnp.bfloat16

EPS = 1e-6
MASK_VALUE = -1e30
D = 1024
NDEV = 8
FF = 2816
FF_SHARD = 704
FF_HALF_ROWS = 352
FF_BLOCK_EDGES = ()
IN_COLS = 6144
CONV_W = 512
CONV_K = 31
CONV_HALO = 32
CONV_CHUNK = 32
CONV_WIN = CONV_CHUNK + 40
GLU_CHUNK = 128
ATT_W = 512
ATT_HEADS = 8
ATT_HD = 64
CHUNK = 64
LEFT_CHUNKS = 8
MAX_REL = 128
N_REL = 192
QB = 256
KWIN = QB + LEFT_CHUNKS * CHUNK
KPAD = LEFT_CHUNKS * CHUNK
DS_LANES = 1024
MEM_W = 512
MEM_HEADS = 4
MEM_HD = 128
ADAM_LR = 0.001
ADAM_B1 = 0.9
ADAM_B2 = 0.999
ADAM_EPS = 1e-08
ADAM_WD = 0.01
ADAM_STEP = 10
VMEM_LIMIT = 60 * 1024 * 1024
TILE_FFN = 256
TILE_COMBINE = 256
TILE_TOKENS = 512
TILE_GRAD_TOKENS = 2048
TILE_GRAD_TOKENS_WIDE = 1024
TILE_GRAD_TOKENS_WIDEST = 512

MESH = pl.DeviceIdType.MESH
ANY = pl.BlockSpec(memory_space=pl.ANY)

WEIGHTS = ['ffn1_norm', 'ffn1_w_up', 'ffn1_w_down', 'mix_norm', 'mem_norm', 'w_in', 'b_gate', 'conv_dw_w', 'conv_dw_b',
           'conv_ln_g', 'conv_ln_b', 'conv_w_pw', 'att_rel_bias', 'att_w_o', 'mem_w_kv', 'mem_w_o', 'w_out', 'ffn2_norm',
           'ffn2_w_up', 'ffn2_w_down', 'final_norm']
BIG = {
    'ffn1_w_up': ('row', FF_SHARD), 'ffn1_w_down': ('row', FF_HALF_ROWS), 'w_in': ('col', 768),
    'conv_w_pw': ('col', 128), 'att_w_o': ('col', 128), 'mem_w_kv': ('row', 128), 'mem_w_o': ('col', 128),
    'w_out': ('row', 128), 'ffn2_w_up': ('row', FF_SHARD), 'ffn2_w_down': ('row', FF_HALF_ROWS),
}
BIG_ORDER = ['ffn1_w_up', 'ffn1_w_down', 'w_in', 'conv_w_pw', 'att_w_o', 'mem_w_kv', 'mem_w_o', 'w_out', 'ffn2_w_up', 'ffn2_w_down']
TRANSPOSED = ('ffn1_w_up', 'ffn2_w_up')


def _dot(a, b):
    return jnp.dot(a, b, preferred_element_type=F32)


def _dot_nt(a, b):
    return lax.dot_general(a, b, (((1,), (1,)), ((), ())), preferred_element_type=F32)


def _dot_tn(a, b):
    return lax.dot_general(a, b, (((0,), (0,)), ((), ())), preferred_element_type=F32)


def _sigmoid(v):
    return jax.nn.sigmoid(v)


def _const(shape):
    return pl.BlockSpec(shape, lambda *_: (0,) * len(shape), pipeline_mode=pl.Buffered(1))


def _params(*sem):
    return pltpu.CompilerParams(dimension_semantics=sem if sem else None, vmem_limit_bytes=VMEM_LIMIT)


def _my_coords():
    return lax.axis_index("x"), lax.axis_index("y"), lax.axis_index("c")


def _dev_index(px, py, pc):
    return 4 * px + 2 * py + pc


def _window(ref, kind, n, p):
    if kind == 'row':
        return ref.at[pl.ds(pl.multiple_of(p * n, n), n), :]
    return ref.at[:, pl.ds(pl.multiple_of(p * n, 128), n)]


def _full_shape(kind, n, shard_shape):
    if kind == 'row':
        return (NDEV * n, shard_shape[1])
    return (shard_shape[0], NDEV * n)


def _cast_shards(shards):
    n = len(shards)

    def body(*refs):
        for i in range(n):
            refs[n + i][...] = refs[i][...].astype(BF16)

    out_shape = [jax.ShapeDtypeStruct(s.shape, BF16) for s in shards]
    return pl.pallas_call(body, out_shape=out_shape, name="cast_shards", compiler_params=_params())(*shards)


class _Ride:
    def __init__(self, inputs, out_shape, scratch, start, finish, mids=()):
        self.inputs, self.out_shape, self.scratch = list(inputs), list(out_shape), list(scratch)
        self.start, self.finish, self.mids = start, finish, tuple(mids)


def _pallas(body, name, grid, in_specs, out_specs, out_shape, args, scratch_shapes=(), sem=None, aliases=None, ride=None,
            after=None):
    if ride is None:
        n_in, n_dep = len(args), 0 if after is None else 1

        def kernel_body(*refs):
            body(*refs[:n_in], *refs[n_in + n_dep:])

        outs = pl.pallas_call(kernel_body if n_dep else body, grid=grid, name=name, in_specs=list(in_specs) + [ANY] * n_dep,
                              out_specs=out_specs, out_shape=out_shape, scratch_shapes=list(scratch_shapes),
                              input_output_aliases=aliases or {}, compiler_params=_params(*sem),
                              )(*args, *([after] if n_dep else []))
        return list(outs), []
    n_in, n_out, n_scr = len(args), len(out_shape), len(scratch_shapes)
    r_in, r_out = len(ride.inputs), len(ride.out_shape)

    def wrapped(*refs):
        k_in, rin = refs[:n_in], refs[n_in:n_in + r_in]
        o0 = n_in + r_in
        k_out, rout = refs[o0:o0 + n_out], refs[o0 + n_out:o0 + n_out + r_out]
        s0 = o0 + n_out + r_out
        k_scr, rscr = refs[s0:s0 + n_scr], refs[s0 + n_scr:]
        ids = [pl.program_id(k) for k in range(len(grid))]
        first = functools.reduce(jnp.logical_and, [i == 0 for i in ids])
        last = functools.reduce(jnp.logical_and, [i == g - 1 for i, g in zip(ids, grid)])
        pl.when(first)(lambda: ride.start(rin, rout, rscr))
        single_step = all(g == 1 for g in grid)
        for quarter, mid in ride.mids:
            if not single_step:
                at_mid = functools.reduce(jnp.logical_and, [ids[0] == (quarter * grid[0]) // 4] + [i == 0 for i in ids[1:]])
                pl.when(at_mid)(functools.partial(mid, rin, rout, rscr))
        body(*k_in, *k_out, *k_scr)
        for _, mid in ride.mids:
            if single_step:
                mid(rin, rout, rscr)
        pl.when(last)(lambda: ride.finish(rin, rout, rscr))

    outs = pl.pallas_call(
        wrapped, grid=grid, name=name, in_specs=list(in_specs) + [ANY] * r_in, out_specs=list(out_specs) + [ANY] * r_out,
        out_shape=list(out_shape) + ride.out_shape, scratch_shapes=list(scratch_shapes) + ride.scratch,
        input_output_aliases=aliases or {}, compiler_params=_params(*(["arbitrary"] * len(grid))),
    )(*args, *ride.inputs)
    return list(outs[:n_out]), list(outs[n_out:])


def _gather_ride(shards, kinds):
    n = len(shards)

    def plan(rin, out, sems):
        send_sems, recv_sems, local_sems = sems[:3]
        x, y, c = _my_coords()
        me, sibling = (x, y, c), (x, y, 1 - c)
        xn, yn, diag = (1 - x, y), (x, 1 - y), (1 - x, 1 - y)

        def win(i, dev):
            return _window(out[i], kinds[i][0], kinds[i][1], _dev_index(*dev))

        def copy(i, k, block, to, from_shard=False):
            return pltpu.make_async_remote_copy(
                src_ref=rin[i] if from_shard else win(i, block), dst_ref=win(i, block),
                send_sem=send_sems.at[i, k], recv_sem=recv_sems.at[i, k], device_id=to, device_id_type=MESH)

        def each(fn):
            return [fn(i) for i in range(n)]

        return dict(
            local=lambda: each(lambda i: pltpu.make_async_copy(rin[i], win(i, me), local_sems.at[i])),
            own=lambda: [cp for i in range(n) for cp in (copy(i, 0, me, sibling, True), copy(i, 1, me, (*xn, c), True),
                                                         copy(i, 2, me, (*yn, c), True))],
            from_x=lambda: each(lambda i: copy(i, 1, (*xn, c), me)),
            from_y=lambda: each(lambda i: copy(i, 2, (*yn, c), me)),
            x_block_on_to_y=lambda: each(lambda i: copy(i, 3, (*xn, c), (*yn, c))),
            y_block_on_to_x=lambda: each(lambda i: copy(i, 3, (*yn, c), (*xn, c))),
            from_diag=lambda: each(lambda i: copy(i, 3, (*diag, c), me)),
            to_sibling=lambda: [copy(i, 4 + j, (*chip, c), sibling) for j, chip in enumerate((xn, yn, diag)) for i in range(n)],
            from_sibling=lambda: [cp for i in range(n) for cp in
                                  [copy(i, 0, sibling, me)] + [copy(i, 4 + j, (*chip, 1 - c), me) for j, chip in enumerate((xn, yn, diag))]],
            north=c == 1)

    def start(rin, out, sems):
        p = plan(rin, out, sems)
        for cp in p['local']() + p['own']():
            cp.start()

    def pass_diagonal(rin, out, sems):
        p = plan(rin, out, sems)

        @pl.when(p['north'])
        def _():
            for got, fwd in zip(p['from_x'](), p['x_block_on_to_y']()):
                got.wait_recv()
                fwd.start()

        @pl.when(jnp.logical_not(p['north']))
        def _():
            for got, fwd in zip(p['from_y'](), p['y_block_on_to_x']()):
                got.wait_recv()
                fwd.start()

    def pass_to_sibling(rin, out, sems):
        p = plan(rin, out, sems)

        @pl.when(p['north'])
        def _():
            for cp in p['from_y']():
                cp.wait_recv()

        @pl.when(jnp.logical_not(p['north']))
        def _():
            for cp in p['from_x']():
                cp.wait_recv()
        for cp in p['from_diag']():
            cp.wait_recv()
        for cp in p['to_sibling']():
            cp.start()

    def finish(rin, out, sems):
        p = plan(rin, out, sems)
        for cp in p['from_sibling']():
            cp.wait_recv()
        for cp in p['own']() + p['to_sibling']():
            cp.wait_send()

        @pl.when(p['north'])
        def _():
            for cp in p['x_block_on_to_y']():
                cp.wait_send()

        @pl.when(jnp.logical_not(p['north']))
        def _():
            for cp in p['y_block_on_to_x']():
                cp.wait_send()
        for cp in p['local']():
            cp.wait()

    out_shape = [jax.ShapeDtypeStruct(_full_shape(k, m, s.shape), s.dtype) for s, (k, m) in zip(shards, kinds)]
    scratch = [pltpu.SemaphoreType.DMA((n, 7)), pltpu.SemaphoreType.DMA((n, 7)), pltpu.SemaphoreType.DMA((n,))]
    return _Ride(shards, out_shape, scratch, start, finish, mids=((1, pass_diagonal), (3, pass_to_sibling)))


def _scatter_ride(grads, kinds):
    n = len(grads)

    def plan(g, out, sems):
        send_sems, recv_sems, local_sems = sems
        x, y, c = _my_coords()
        me = _dev_index(x, y, c)

        def local():
            return [pltpu.make_async_copy(_window(g[i], kinds[i][0], kinds[i][1], me), out[i].at[me], local_sems.at[i])
                    for i in range(n)]

        def remote(arrival):
            cps = []
            for rel in range(1, NDEV):
                peer = _peer(x, y, c, rel)
                dev = _dev_index(*peer)
                for i in range(n):
                    kind, m = kinds[i]
                    cps.append(pltpu.make_async_remote_copy(
                        src_ref=_window(g[i], kind, m, me if arrival else dev), dst_ref=out[i].at[dev if arrival else me],
                        send_sem=send_sems.at[i, rel - 1], recv_sem=recv_sems.at[i, rel - 1], device_id=peer, device_id_type=MESH))
            return cps

        return local, remote

    def start(g, out, sems):
        local, remote = plan(g, out, sems)
        for cp in local() + remote(False):
            cp.start()

    def finish(g, out, sems):
        local, remote = plan(g, out, sems)
        for cp in remote(True):
            cp.wait_recv()
        for cp in remote(False):
            cp.wait_send()
        for cp in local():
            cp.wait()

    def block_shape(gr, kind, m):
        return (m, gr.shape[1]) if kind == 'row' else (gr.shape[0], m)

    out_shape = [jax.ShapeDtypeStruct((NDEV,) + block_shape(gr, k, m), gr.dtype) for gr, (k, m) in zip(grads, kinds)]
    scratch = [pltpu.SemaphoreType.DMA((n, NDEV - 1)), pltpu.SemaphoreType.DMA((n, NDEV - 1)), pltpu.SemaphoreType.DMA((n,))]
    return _Ride(grads, out_shape, scratch, start, finish)


def _rms_stats(xf):
    r = lax.rsqrt(jnp.mean(xf * xf, axis=-1, keepdims=True) + EPS)
    return xf * r, r


def _rms_bwd(dh, g, xhat, r):
    dxhat = dh * g
    return r * (dxhat - xhat * jnp.mean(dxhat * xhat, axis=-1, keepdims=True))


def _ffn_blocks():
    edges = (0,) + FF_BLOCK_EDGES + (FF,)
    return [(slice(lo, hi), slice(FF + lo, FF + hi)) for lo, hi in zip(edges[:-1], edges[1:])]


def _swiglu_tile(x_ref, g_ref, wut_ref, wd_ref, ab_ref):
    xf = x_ref[...]
    xhat, _ = _rms_stats(xf)
    h = (xhat * g_ref[...]).astype(BF16)
    acc = jnp.zeros(xf.shape, F32)
    for ra, rb in _ffn_blocks():
        a = _dot_nt(h, wut_ref[ra, :])
        b = _dot_nt(h, wut_ref[rb, :])
        ab_ref[:, ra] = a.astype(BF16)
        ab_ref[:, rb] = b.astype(BF16)
        act = (a * _sigmoid(a) * b).astype(BF16)
        acc = acc + _dot(act, wd_ref[ra, :])
    return xf + 0.5 * acc


def _ffn_fwd(x, g, wut, wd, tm, name, ride=None):
    T = x.shape[0]

    def body(x_ref, g_ref, wut_ref, wd_ref, xo_ref, ab_ref):
        xo_ref[...] = _swiglu_tile(x_ref, g_ref, wut_ref, wd_ref, ab_ref)

    return _pallas(
        body, name, (T // tm,),
        [pl.BlockSpec((tm, D), lambda t: (t, 0)), _const((1, D)), _const((2 * FF, D)), _const((FF, D))],
        [pl.BlockSpec((tm, D), lambda t: (t, 0)), pl.BlockSpec((tm, 2 * FF), lambda t: (t, 0))],
        [jax.ShapeDtypeStruct((T, D), F32), jax.ShapeDtypeStruct((T, 2 * FF), BF16)],
        (x, g, wut, wd), sem=("arbitrary",), ride=ride)


def _ffn_fwd_loss(x, g, wut, wd, g_final, target, tm, name):
    T = x.shape[0]

    def body(x_ref, g_ref, wut_ref, wd_ref, gf_ref, t_ref, dx_ref, ab_ref, loss_ref, dgf_ref):
        xhat, r = _rms_stats(_swiglu_tile(x_ref, g_ref, wut_ref, wd_ref, ab_ref))
        gain = gf_ref[...]
        diff = xhat * gain - t_ref[...]
        dout = diff * (1.0 / D)

        @pl.when(pl.program_id(0) == 0)
        def _():
            loss_ref[...] = jnp.zeros_like(loss_ref)
            dgf_ref[...] = jnp.zeros_like(dgf_ref)
        sq = jnp.sum(jnp.sum(diff * diff, axis=0, keepdims=True), axis=1, keepdims=True)
        loss_ref[...] += jnp.broadcast_to(sq * (0.5 / D), (1, 128))
        dgf_ref[...] += jnp.sum(dout * xhat, axis=0, keepdims=True)
        dx_ref[...] = _rms_bwd(dout, gain, xhat, r)

    row = pl.BlockSpec((tm, D), lambda t: (t, 0))
    return pl.pallas_call(
        body, grid=(T // tm,), name=name,
        in_specs=[row, _const((1, D)), _const((2 * FF, D)), _const((FF, D)), _const((1, D)), row],
        out_specs=[row, pl.BlockSpec((tm, 2 * FF), lambda t: (t, 0)), pl.BlockSpec((1, 128), lambda t: (0, 0)),
                   pl.BlockSpec((1, D), lambda t: (0, 0))],
        out_shape=[jax.ShapeDtypeStruct((T, D), F32), jax.ShapeDtypeStruct((T, 2 * FF), BF16),
                   jax.ShapeDtypeStruct((1, 128), F32), jax.ShapeDtypeStruct((1, D), F32)],
        compiler_params=_params("arbitrary"),
    )(x, g, wut, wd, g_final, target)


def _ffn_bwd(x, dy, ab, g, wut, wd, tm, name):
    T = x.shape[0]

    def body(x_ref, dy_ref, ab_ref, g_ref, wut_ref, wd_ref, dx_ref, dab_ref, act_ref, h_ref, dg_ref):
        xf = x_ref[...]
        xhat, r = _rms_stats(xf)
        gain = g_ref[...]
        h_ref[...] = (xhat * gain).astype(BF16)
        dy = dy_ref[...]
        dyh = (0.5 * dy).astype(BF16)
        dh = jnp.zeros((tm, D), F32)
        for ra, rb in _ffn_blocks():
            a = ab_ref[:, ra].astype(F32)
            b = ab_ref[:, rb].astype(F32)
            dact = _dot_nt(dyh, wd_ref[ra, :])
            sg = _sigmoid(a)
            sl = a * sg
            act_ref[:, ra] = (sl * b).astype(BF16)
            da = (dact * b * (sg * (1.0 + a * (1.0 - sg)))).astype(BF16)
            db = (dact * sl).astype(BF16)
            dab_ref[:, ra] = da
            dab_ref[:, rb] = db
            dh = dh + _dot(da, wut_ref[ra, :]) + _dot(db, wut_ref[rb, :])
        dx_ref[...] = dy + _rms_bwd(dh, gain, xhat, r)

        @pl.when(pl.program_id(0) == 0)
        def _():
            dg_ref[...] = jnp.zeros_like(dg_ref)
        dg_ref[...] += jnp.sum(dh * xhat, axis=0, keepdims=True)

    return pl.pallas_call(
        body, grid=(T // tm,), name=name,
        in_specs=[pl.BlockSpec((tm, D), lambda t: (t, 0)), pl.BlockSpec((tm, D), lambda t: (t, 0)),
                  pl.BlockSpec((tm, 2 * FF), lambda t: (t, 0)), _const((1, D)), _const((2 * FF, D)), _const((FF, D))],
        out_specs=[pl.BlockSpec((tm, D), lambda t: (t, 0)), pl.BlockSpec((tm, 2 * FF), lambda t: (t, 0)),
                   pl.BlockSpec((tm, FF), lambda t: (t, 0)), pl.BlockSpec((tm, D), lambda t: (t, 0)),
                   pl.BlockSpec((1, D), lambda t: (0, 0))],
        out_shape=[jax.ShapeDtypeStruct((T, D), F32), jax.ShapeDtypeStruct((T, 2 * FF), BF16),
                   jax.ShapeDtypeStruct((T, FF), BF16), jax.ShapeDtypeStruct((T, D), BF16), jax.ShapeDtypeStruct((1, D), F32)],
        compiler_params=_params("arbitrary"),
    )(x, dy, ab, g, wut, wd)


def _tn_matmul(xm, ym, tn, name, scale=None, out_cols=None, col_off=0, prev=None, tt=TILE_GRAD_TOKENS, x_part=(0, 1),
               out_rows=None, y_part=(0, 1), ride=None, after=None):
    T = xm.shape[0]
    xi, xn = x_part
    yi, yn = y_part
    K = xm.shape[1] // xn
    N = ym.shape[1] // yn
    out_cols = N if out_cols is None else out_cols
    row_blk = xi if out_rows is not None else 0
    out_rows = K if out_rows is None else out_rows
    tt = min(tt, T)
    nt = T // tt
    off = col_off // tn

    def body(*refs):
        x_ref, y_ref = refs[0], refs[1]
        o_ref, acc = refs[-2], refs[-1]

        @pl.when(pl.program_id(1) == 0)
        def _():
            acc[...] = jnp.zeros_like(acc)
        acc[...] += _dot_tn(x_ref[...].astype(BF16), y_ref[...].astype(BF16))

        @pl.when(pl.program_id(1) == nt - 1)
        def _():
            res = acc[...]
            o_ref[...] = (res if scale is None else res * scale).astype(BF16)

    ycol = yi * (N // tn)
    in_specs = [pl.BlockSpec((tt, K), lambda n, t: (t, xi)), pl.BlockSpec((tt, tn), lambda n, t: (t, n + ycol))]
    args = [xm, ym]
    aliases = {}
    if prev is not None:
        in_specs.append(ANY)
        args.append(prev)
        aliases = {2: 0}
    outs, rode = _pallas(
        body, name, (N // tn, nt), in_specs, [pl.BlockSpec((K, tn), lambda n, t: (row_blk, n + off))],
        [jax.ShapeDtypeStruct((out_rows, out_cols), BF16)], args, scratch_shapes=[pltpu.VMEM((K, tn), F32)],
        sem=("parallel", "arbitrary"), aliases=aliases, ride=ride, after=after)
    return outs[0], rode


def _mix_fwd(x, g, w_in, tm, ride=None):
    T = x.shape[0]

    def body(x_ref, g_ref, w_ref, uc_ref, qkv_ref, mq_ref, gl_ref, h_ref):
        xhat, _ = _rms_stats(x_ref[...])
        h = (xhat * g_ref[...]).astype(BF16)
        h_ref[...] = h
        uc_ref[...] = _dot(h, w_ref[:, 0:1024])
        qkv_ref[...] = _dot(h, w_ref[:, 1024:2560]).astype(BF16)
        mq_ref[...] = _dot(h, w_ref[:, 2560:3072]).astype(BF16)
        for j in range(3):
            gl_ref[:, j * D:(j + 1) * D] = _dot(h, w_ref[:, 3072 + j * D:3072 + (j + 1) * D]).astype(BF16)

    row = lambda w: pl.BlockSpec((tm, w), lambda t: (t, 0))
    return _pallas(
        body, "mix_fwd", (T // tm,), [row(D), _const((1, D)), _const((D, IN_COLS))],
        [row(1024), row(1536), row(512), row(3072), row(D)],
        [jax.ShapeDtypeStruct((T, 1024), F32), jax.ShapeDtypeStruct((T, 1536), BF16), jax.ShapeDtypeStruct((T, 512), BF16),
         jax.ShapeDtypeStruct((T, 3072), BF16), jax.ShapeDtypeStruct((T, D), BF16)],
        (x, g, w_in), sem=("parallel",), ride=ride)


def _mix_bwd(x, dres, duc, dqkv, dmq, dgl, g, w_in, tm, ride=None, after=None):
    T = x.shape[0]

    def body(x_ref, dres_ref, duc_ref, dqkv_ref, dmq_ref, dgl_ref, g_ref, w_ref, dx_ref, dg_ref):
        xhat, r = _rms_stats(x_ref[...])
        dh = _dot_nt(duc_ref[...], w_ref[:, 0:1024])
        dh = dh + _dot_nt(dqkv_ref[...], w_ref[:, 1024:2560])
        dh = dh + _dot_nt(dmq_ref[...], w_ref[:, 2560:3072])
        dh = dh + _dot_nt(dgl_ref[...], w_ref[:, 3072:6144])
        dx_ref[...] = dres_ref[...] + _rms_bwd(dh, g_ref[...], xhat, r)

        @pl.when(pl.program_id(0) == 0)
        def _():
            dg_ref[...] = jnp.zeros_like(dg_ref)
        dg_ref[...] += jnp.sum(dh * xhat, axis=0, keepdims=True)

    row = lambda w: pl.BlockSpec((tm, w), lambda t: (t, 0))
    return _pallas(
        body, "mix_bwd", (T // tm,),
        [row(D), row(D), row(1024), row(1536), row(512), row(3072), _const((1, D)), _const((D, IN_COLS))],
        [row(D), pl.BlockSpec((1, D), lambda t: (0, 0))],
        [jax.ShapeDtypeStruct((T, D), F32), jax.ShapeDtypeStruct((1, D), F32)],
        (x, dres, duc, dqkv, dmq, dgl, g, w_in), sem=("arbitrary",), ride=ride, after=after)


def _shifted(win, base, copies):
    for k in range(8):
        copies[k] = win[base + k:base + k + CONV_CHUNK + 24]
    return copies


def _tap_slices(copies, tap):
    out = []
    for k in range(8):
        for a in range(4):
            j = tap(a, k)
            if 0 <= j < CONV_K:
                out.append((j, copies[k, pl.ds(8 * a, CONV_CHUNK), :]))
    return out


def _conv_taps(copies, w_ref, tap):
    acc = jnp.zeros((CONV_CHUNK, CONV_W), F32)
    for j, rows in _tap_slices(copies, tap):
        acc = acc + rows * w_ref[j:j + 1, :]
    return acc


def _fold8(v):
    acc = v[0:8]
    for r in range(8, CONV_CHUNK, 8):
        acc = acc + v[r:r + 8]
    return acc


def _glu_into(uc_ref, vpad, S):
    vpad[pl.ds(0, CONV_HALO), :] = jnp.zeros((CONV_HALO, CONV_W), F32)
    vpad[pl.ds(S + CONV_HALO, CONV_HALO), :] = jnp.zeros((CONV_HALO, CONV_W), F32)

    def glu(i, carry):
        r0 = pl.multiple_of(i * GLU_CHUNK, GLU_CHUNK)
        a = uc_ref[0, pl.ds(r0, GLU_CHUNK), 0:CONV_W]
        gt = uc_ref[0, pl.ds(r0, GLU_CHUNK), CONV_W:2 * CONV_W]
        vpad[pl.ds(pl.multiple_of(r0 + CONV_HALO, CONV_HALO), GLU_CHUNK), :] = a * _sigmoid(gt)
        return carry
    lax.fori_loop(0, S // GLU_CHUNK, glu, 0)


def _layer_norm(z, vec_ref):
    xc = z - jnp.mean(z, axis=-1, keepdims=True)
    rstd = lax.rsqrt(jnp.mean(xc * xc, axis=-1, keepdims=True) + EPS)
    xn = xc * rstd
    return xn, rstd, xn * vec_ref[1:2, :] + vec_ref[2:3, :]


def _conv_fwd(uc, dw_w, vec):
    NB, S, _ = uc.shape

    def body(uc_ref, w_ref, vec_ref, o_ref, z_ref, vpad, copies):
        _glu_into(uc_ref, vpad, S)

        def conv(i, carry):
            r0 = pl.multiple_of(i * CONV_CHUNK, CONV_CHUNK)
            win = vpad[pl.ds(r0, CONV_WIN), :]
            z = _conv_taps(_shifted(win, CONV_HALO - (CONV_K - 1), copies), w_ref, lambda a, k: 8 * a + k) + vec_ref[0:1, :]
            z_ref[0, pl.ds(r0, CONV_CHUNK), :] = z
            _, _, yln = _layer_norm(z, vec_ref)
            o_ref[0, pl.ds(r0, CONV_CHUNK), :] = (yln * _sigmoid(yln)).astype(BF16)
            return carry
        lax.fori_loop(0, S // CONV_CHUNK, conv, 0, unroll=4)

    seq = pl.BlockSpec((1, S, CONV_W), lambda b: (b, 0, 0))
    return pl.pallas_call(
        body, grid=(NB,), name="conv_fwd",
        in_specs=[pl.BlockSpec((1, S, 2 * CONV_W), lambda b: (b, 0, 0)), _const((CONV_K, CONV_W)), _const((8, CONV_W))],
        out_specs=[seq, seq],
        out_shape=[jax.ShapeDtypeStruct((NB, S, CONV_W), BF16), jax.ShapeDtypeStruct((NB, S, CONV_W), F32)],
        scratch_shapes=[pltpu.VMEM((S + 2 * CONV_HALO, CONV_W), F32), pltpu.VMEM((8, CONV_CHUNK + 24, CONV_W), F32)],
        compiler_params=_params("parallel"),
    )(uc, dw_w, vec)


def _conv_bwd(uc, z, dcact, dw_w, vec, ride=None):
    NB, S, _ = uc.shape
    n_chunks = S // CONV_CHUNK

    def body(uc_ref, z_ref, dc_ref, w_ref, vec_ref, duc_ref, dw_ref, dvec_ref, vpad, dzpad, dw8, dvec8, copies):
        @pl.when(pl.program_id(0) == 0)
        def _():
            dw8[...] = jnp.zeros_like(dw8)
            dvec8[...] = jnp.zeros_like(dvec8)
        _glu_into(uc_ref, vpad, S)
        dzpad[pl.ds(S, 2 * CONV_HALO), :] = jnp.zeros((2 * CONV_HALO, CONV_W), F32)

        def norm_bwd(i, carry):
            r0 = pl.multiple_of(i * CONV_CHUNK, CONV_CHUNK)
            xn, rstd, yln = _layer_norm(z_ref[0, pl.ds(r0, CONV_CHUNK), :], vec_ref)
            sg = _sigmoid(yln)
            dyln = dc_ref[0, pl.ds(r0, CONV_CHUNK), :] * (sg * (1.0 + yln * (1.0 - sg)))
            dxn = dyln * vec_ref[1:2, :]
            dz = rstd * (dxn - jnp.mean(dxn, axis=-1, keepdims=True) - xn * jnp.mean(dxn * xn, axis=-1, keepdims=True))
            dzpad[pl.ds(r0, CONV_CHUNK), :] = dz
            dvec8[0] += _fold8(dz)
            dvec8[1] += _fold8(dyln * xn)
            dvec8[2] += _fold8(dyln)
            return carry
        lax.fori_loop(0, n_chunks, norm_bwd, 0, unroll=4)

        def taps_bwd(i, carry):
            r0 = pl.multiple_of(i * CONV_CHUNK, CONV_CHUNK)
            dzwin = dzpad[pl.ds(r0, CONV_WIN), :]
            dv = _conv_taps(_shifted(dzwin, 0, copies), w_ref, lambda a, k: CONV_K - 1 - 8 * a - k)
            dz = dzwin[0:CONV_CHUNK]
            vwin = vpad[pl.ds(r0, CONV_WIN), :]
            for j, rows in _tap_slices(_shifted(vwin, CONV_HALO - (CONV_K - 1), copies), lambda a, k: 8 * a + k):
                dw8[j] += _fold8(dz * rows)
            a = uc_ref[0, pl.ds(r0, CONV_CHUNK), 0:CONV_W]
            sg = _sigmoid(uc_ref[0, pl.ds(r0, CONV_CHUNK), CONV_W:2 * CONV_W])
            duc_ref[0, pl.ds(r0, CONV_CHUNK), 0:CONV_W] = (dv * sg).astype(BF16)
            duc_ref[0, pl.ds(r0, CONV_CHUNK), CONV_W:2 * CONV_W] = (dv * a * sg * (1.0 - sg)).astype(BF16)
            return carry
        lax.fori_loop(0, n_chunks, taps_bwd, 0, unroll=2)

        @pl.when(pl.program_id(0) == NB - 1)
        def _():
            dw_ref[...] = jnp.zeros_like(dw_ref)
            dvec_ref[...] = jnp.zeros_like(dvec_ref)
            for j in range(CONV_K):
                dw_ref[j:j + 1, :] = jnp.sum(dw8[j], axis=0, keepdims=True)
            for j in range(3):
                dvec_ref[j:j + 1, :] = jnp.sum(dvec8[j], axis=0, keepdims=True)

    return _pallas(
        body, "conv_bwd", (NB,),
        [pl.BlockSpec((1, S, 2 * CONV_W), lambda b: (b, 0, 0)), pl.BlockSpec((1, S, CONV_W), lambda b: (b, 0, 0)),
         pl.BlockSpec((1, S, CONV_W), lambda b: (b, 0, 0)), _const((CONV_K, CONV_W)), _const((8, CONV_W))],
        [pl.BlockSpec((1, S, 2 * CONV_W), lambda b: (b, 0, 0)), pl.BlockSpec((32, CONV_W), lambda b: (0, 0)),
         pl.BlockSpec((8, CONV_W), lambda b: (0, 0))],
        [jax.ShapeDtypeStruct((NB, S, 2 * CONV_W), BF16), jax.ShapeDtypeStruct((32, CONV_W), F32),
         jax.ShapeDtypeStruct((8, CONV_W), F32)],
        (uc, z, dcact, dw_w, vec),
        scratch_shapes=[pltpu.VMEM((S + 2 * CONV_HALO, CONV_W), F32), pltpu.VMEM((S + 2 * CONV_HALO, CONV_W), F32),
                        pltpu.VMEM((CONV_K, 8, CONV_W), F32), pltpu.VMEM((3, 8, CONV_W), F32),
                        pltpu.VMEM((8, CONV_CHUNK + 24, CONV_W), F32)],
        sem=("arbitrary",), ride=ride)


def _rel_index_of_column(cols):
    offset = jnp.where(cols < KWIN, cols, cols - DS_LANES)
    return jnp.clip(KPAD - offset, -(CHUNK - 1), MAX_REL) + (CHUNK - 1)


def _bias_table(rel_bias, ride=None):
    def body(rb_ref, o_ref, by_offset, first8):
        ridx = _rel_index_of_column(lax.broadcasted_iota(jnp.int32, (1, DS_LANES), 1))
        onehot = (ridx == lax.broadcasted_iota(jnp.int32, (N_REL, 1), 0)).astype(F32)
        by_offset[...] = jnp.dot(rb_ref[...], onehot, preferred_element_type=F32, precision=lax.Precision.HIGHEST)
        sub = lax.broadcasted_iota(jnp.int32, (8, 1), 0)
        kchunk = lax.broadcasted_iota(jnp.int32, (1, KWIN), 1) // CHUNK
        for head in range(ATT_HEADS):
            base = jnp.broadcast_to(by_offset[head:head + 1, :], (8, DS_LANES))
            rows = base
            for s in range(1, 8):
                rows = jnp.where(sub == s, pltpu.roll(base, s, 1), rows)
            first8[head] = rows

        def rows8(q8, carry):
            qchunk = (q8 * 8 + sub) // CHUNK
            band = (kchunk >= qchunk) & (kchunk <= qchunk + LEFT_CHUNKS)
            for head in range(ATT_HEADS):
                tile = pltpu.roll(first8[head], q8 * 8, 1)[:, 0:KWIN]
                o_ref[head, pl.ds(pl.multiple_of(q8 * 8, 8), 8), :] = jnp.where(band, tile, MASK_VALUE)
            return carry
        lax.fori_loop(0, QB // 8, rows8, 0)

    outs, rode = _pallas(
        body, "bias_table", (1,), [pl.BlockSpec((ATT_HEADS, N_REL), lambda i: (0, 0))],
        [pl.BlockSpec((ATT_HEADS, QB, KWIN), lambda i: (0, 0, 0))], [jax.ShapeDtypeStruct((ATT_HEADS, QB, KWIN), F32)], (rel_bias,),
        scratch_shapes=[pltpu.VMEM((ATT_HEADS, DS_LANES), F32), pltpu.VMEM((ATT_HEADS, 8, DS_LANES), F32)],
        sem=("arbitrary",), ride=ride)
    return outs[0], rode


def _load_keys(i, k_ref, v_ref, kpad, vpad, S):
    @pl.when(i == 0)
    def _():
        kpad[pl.ds(0, KPAD), :] = jnp.zeros((KPAD, ATT_W), BF16)
        vpad[pl.ds(0, KPAD), :] = jnp.zeros((KPAD, ATT_W), BF16)
        kpad[pl.ds(KPAD, S), :] = k_ref[0]
        vpad[pl.ds(KPAD, S), :] = v_ref[0]


def _att_scores(q2s, k2, tab_ref, head, in_head, in_seq):
    qm = jnp.where(in_head, q2s, jnp.zeros_like(q2s))
    return jnp.where(in_seq, _dot_nt(qm, k2) + tab_ref[head], MASK_VALUE)


def _scaled(q2):
    return q2 * jnp.asarray(ATT_HD ** -0.5, q2.dtype)


def _att_fwd(qkv, tab, ride=None):
    NB, S, _ = qkv.shape

    def body(q_ref, k_ref, v_ref, tab_ref, o_ref, lse_ref, kpad, vpad):
        i = pl.program_id(1)
        _load_keys(i, k_ref, v_ref, kpad, vpad, S)
        koff = pl.multiple_of(i * QB, QB)
        lane = lax.broadcasted_iota(jnp.int32, (1, 128), 1)
        in_seq = (lax.broadcasted_iota(jnp.int32, (1, KWIN), 1) + i * QB) >= KPAD
        lse = jnp.zeros((QB, 128), F32)
        for pair in range(ATT_HEADS // 2):
            cols = slice(pair * 128, (pair + 1) * 128)
            q2s = _scaled(q_ref[0, :, cols])
            k2 = kpad[pl.ds(koff, KWIN), cols]
            v2 = vpad[pl.ds(koff, KWIN), cols]
            o2 = jnp.zeros((QB, 128), F32)
            for hh in range(2):
                head = 2 * pair + hh
                in_head = (lane // ATT_HD) == hh
                s = _att_scores(q2s, k2, tab_ref, head, in_head, in_seq)
                m = jnp.max(s, axis=-1, keepdims=True)
                e = jnp.exp(s - m)
                l = jnp.sum(e, axis=-1, keepdims=True)
                o2 = jnp.where(in_head, _dot(e.astype(BF16), v2) * (1.0 / l), o2)
                lse = jnp.where(lane == head, m + jnp.log(l), lse)
            o_ref[0, :, cols] = o2.astype(BF16)
        lse_ref[0] = lse

    seq = lambda col: pl.BlockSpec((1, S, ATT_W), lambda b, i: (b, 0, col), pipeline_mode=pl.Buffered(1))
    outs, rode = _pallas(
        body, "att_fwd", (NB, S // QB),
        [pl.BlockSpec((1, QB, ATT_W), lambda b, i: (b, i, 0)), seq(1), seq(2), _const((ATT_HEADS, QB, KWIN))],
        [pl.BlockSpec((1, QB, ATT_W), lambda b, i: (b, i, 0)), pl.BlockSpec((1, QB, 128), lambda b, i: (b, i, 0))],
        [jax.ShapeDtypeStruct((NB, S, ATT_W), BF16), jax.ShapeDtypeStruct((NB, S, 128), F32)],
        (qkv, qkv, qkv, tab),
        scratch_shapes=[pltpu.VMEM((S + KPAD, ATT_W), BF16), pltpu.VMEM((S + KPAD, ATT_W), BF16)],
        sem=("arbitrary", "arbitrary"), ride=ride)
    return outs[0], outs[1], rode


def _att_bwd(qkv, o, lse, do, tab, ride=None):
    NB, S, _ = qkv.shape
    nq = S // QB

    def body(q_ref, k_ref, v_ref, o_ref, lse_ref, do_ref, tab_ref, dqkv_ref, ds_hbm, kpad, vpad, dkpad, dvpad, ds_acc, ds_sem):
        b, i = pl.program_id(0), pl.program_id(1)
        _load_keys(i, k_ref, v_ref, kpad, vpad, S)

        @pl.when(i == 0)
        def _():
            dkpad[...] = jnp.zeros_like(dkpad)
            dvpad[...] = jnp.zeros_like(dvpad)

        @pl.when((i == 0) & (b == 0))
        def _():
            ds_acc[...] = jnp.zeros_like(ds_acc)

        koff = pl.multiple_of(i * QB, QB)
        lane = lax.broadcasted_iota(jnp.int32, (1, 128), 1)
        in_seq = (lax.broadcasted_iota(jnp.int32, (1, KWIN), 1) + i * QB) >= KPAD
        for pair in range(ATT_HEADS // 2):
            cols = slice(pair * 128, (pair + 1) * 128)
            q2s = _scaled(q_ref[0, :, cols])
            do2 = do_ref[0, :, cols]
            k2 = kpad[pl.ds(koff, KWIN), cols]
            v2 = vpad[pl.ds(koff, KWIN), cols]
            do_o = do2.astype(F32) * o_ref[0, :, cols].astype(F32)
            dq2 = jnp.zeros((QB, 128), F32)
            dk2 = jnp.zeros((KWIN, 128), F32)
            dv2 = jnp.zeros((KWIN, 128), F32)
            for hh in range(2):
                head = 2 * pair + hh
                in_head = (lane // ATT_HD) == hh
                p = jnp.exp(_att_scores(q2s, k2, tab_ref, head, in_head, in_seq) - lse_ref[0, :, head:head + 1])
                row_term = jnp.sum(jnp.where(in_head, do_o, 0.0), axis=-1, keepdims=True)
                dom = jnp.where(in_head, do2, jnp.zeros_like(do2))
                ds = p * (_dot_nt(dom, v2) - row_term)
                ds_acc[head] += ds
                dsb = ds.astype(BF16)
                dq2 = jnp.where(in_head, _dot(dsb, k2), dq2)
                dk2 = jnp.where(in_head, _dot_tn(dsb, q2s), dk2)
                dv2 = jnp.where(in_head, _dot_tn(p.astype(BF16), do2), dv2)
            dqkv_ref[0, pl.ds(koff, QB), cols] = (dq2 * (ATT_HD ** -0.5)).astype(BF16)
            dkpad[pl.ds(koff, KWIN), cols] += dk2
            dvpad[pl.ds(koff, KWIN), cols] += dv2

        @pl.when(i == nq - 1)
        def _():
            dqkv_ref[0, :, ATT_W:2 * ATT_W] = dkpad[pl.ds(KPAD, S), :].astype(BF16)
            dqkv_ref[0, :, 2 * ATT_W:3 * ATT_W] = dvpad[pl.ds(KPAD, S), :].astype(BF16)

        @pl.when((i == nq - 1) & (b == NB - 1))
        def _():
            out = pltpu.make_async_copy(ds_acc, ds_hbm, ds_sem)
            out.start()
            out.wait()

    seq = lambda col: pl.BlockSpec((1, S, ATT_W), lambda b, i: (b, 0, col), pipeline_mode=pl.Buffered(1))
    rows = pl.BlockSpec((1, QB, ATT_W), lambda b, i: (b, i, 0))
    return _pallas(
        body, "att_bwd", (NB, nq),
        [rows, seq(1), seq(2), rows, pl.BlockSpec((1, QB, 128), lambda b, i: (b, i, 0)), rows, _const((ATT_HEADS, QB, KWIN))],
        [pl.BlockSpec((1, S, 3 * ATT_W), lambda b, i: (b, 0, 0)), ANY],
        [jax.ShapeDtypeStruct((NB, S, 3 * ATT_W), BF16), jax.ShapeDtypeStruct((ATT_HEADS, QB, KWIN), F32)],
        (qkv, qkv, qkv, o, lse, do, tab),
        scratch_shapes=[pltpu.VMEM((S + KPAD, ATT_W), BF16), pltpu.VMEM((S + KPAD, ATT_W), BF16),
                        pltpu.VMEM((S + KPAD, ATT_W), F32), pltpu.VMEM((S + KPAD, ATT_W), F32),
                        pltpu.VMEM((ATT_HEADS, QB, KWIN), F32), pltpu.SemaphoreType.DMA],
        sem=("arbitrary", "arbitrary"), ride=ride)


def _rel_bias_grad(ds):
    def body(ds_ref, o_ref):
        sub = lax.broadcasted_iota(jnp.int32, (8, 1), 0)
        ridx = _rel_index_of_column(lax.broadcasted_iota(jnp.int32, (DS_LANES, 1), 0))
        onehot = (ridx == lax.broadcasted_iota(jnp.int32, (1, N_REL), 1)).astype(F32)
        def rows8(q8, accs):
            shift = lax.rem(DS_LANES - q8 * 8, DS_LANES)
            out = []
            for head in range(ATT_HEADS):
                tile = ds_ref[head, pl.ds(pl.multiple_of(q8 * 8, 8), 8), :]
                tile = jnp.concatenate([tile, jnp.zeros((8, DS_LANES - KWIN), F32)], axis=1)
                out.append(accs[head] + pltpu.roll(tile, shift, 1))
            return tuple(out)
        accs = lax.fori_loop(0, QB // 8, rows8, tuple(jnp.zeros((8, DS_LANES), F32) for _ in range(ATT_HEADS)))
        for head in range(ATT_HEADS):
            acc = accs[head]
            diag = jnp.zeros((8, DS_LANES), F32)
            for s in range(8):
                shifted = acc if s == 0 else pltpu.roll(acc, DS_LANES - s, 1)
                diag = jnp.where(sub == s, shifted, diag)
            z = jnp.sum(diag, axis=0, keepdims=True)
            o_ref[head:head + 1, :] = jnp.dot(z, onehot, preferred_element_type=F32, precision=lax.Precision.HIGHEST)

    return pl.pallas_call(body, out_shape=jax.ShapeDtypeStruct((ATT_HEADS, N_REL), F32), name="rel_bias_grad",
                          compiler_params=_params())(ds)


def _memkv_fwd(mem, g, w_kv, tm):
    R = mem.shape[0]
    tm = min(tm, R)

    def body(m_ref, g_ref, w_ref, h_ref, kv_ref):
        xhat, _ = _rms_stats(m_ref[...])
        h = (xhat * g_ref[...]).astype(BF16)
        h_ref[...] = h
        kv_ref[...] = _dot(h, w_ref[...]).astype(BF16)

    row = pl.BlockSpec((tm, D), lambda t: (t, 0))
    return pl.pallas_call(
        body, grid=(R // tm,), name="memkv_fwd", in_specs=[row, _const((1, D)), _const((D, 2 * MEM_W))], out_specs=[row, row],
        out_shape=[jax.ShapeDtypeStruct((R, D), BF16), jax.ShapeDtypeStruct((R, 2 * MEM_W), BF16)],
        compiler_params=_params("parallel"),
    )(mem, g, w_kv)


def _memkv_bwd(mem, dkv, w_kv, tm):
    R = mem.shape[0]
    tm = min(tm, R)

    def body(m_ref, dkv_ref, w_ref, dg_ref):
        xhat, _ = _rms_stats(m_ref[...])
        dh = _dot_nt(dkv_ref[...].astype(BF16), w_ref[...])

        @pl.when(pl.program_id(0) == 0)
        def _():
            dg_ref[...] = jnp.zeros_like(dg_ref)
        dg_ref[...] += jnp.sum(dh * xhat, axis=0, keepdims=True)

    row = pl.BlockSpec((tm, D), lambda t: (t, 0))
    return pl.pallas_call(
        body, grid=(R // tm,), name="memkv_bwd", in_specs=[row, row, _const((D, 2 * MEM_W))],
        out_specs=pl.BlockSpec((1, D), lambda t: (0, 0)), out_shape=jax.ShapeDtypeStruct((1, D), F32),
        compiler_params=_params("arbitrary"),
    )(mem, dkv, w_kv)


def _mem_probs(qh, kh):
    s = _dot_nt(qh, kh) * (MEM_HD ** -0.5)
    e = jnp.exp(s - jnp.max(s, axis=-1, keepdims=True))
    return e * (1.0 / jnp.sum(e, axis=-1, keepdims=True))


def _mematt_fwd(mq, kv, tq):
    NB, S, _ = mq.shape
    M = kv.shape[1]

    def body(q_ref, kv_ref, o_ref):
        for h in range(MEM_HEADS):
            cols = slice(h * MEM_HD, (h + 1) * MEM_HD)
            p = _mem_probs(q_ref[0, :, cols], kv_ref[0, :, cols])
            o_ref[0, :, cols] = _dot(p.astype(BF16), kv_ref[0, :, MEM_W + h * MEM_HD:MEM_W + (h + 1) * MEM_HD]).astype(BF16)

    return pl.pallas_call(
        body, grid=(NB, S // tq), name="mematt_fwd",
        in_specs=[pl.BlockSpec((1, tq, MEM_W), lambda b, i: (b, i, 0)), pl.BlockSpec((1, M, 2 * MEM_W), lambda b, i: (b, 0, 0))],
        out_specs=pl.BlockSpec((1, tq, MEM_W), lambda b, i: (b, i, 0)),
        out_shape=jax.ShapeDtypeStruct((NB, S, MEM_W), BF16), compiler_params=_params("parallel", "parallel"),
    )(mq, kv)


def _mematt_bwd(mq, kv, do, tq):
    NB, S, _ = mq.shape
    M = kv.shape[1]

    def body(q_ref, kv_ref, do_ref, dq_ref, dkv_ref):
        @pl.when(pl.program_id(1) == 0)
        def _():
            dkv_ref[...] = jnp.zeros_like(dkv_ref)
        for h in range(MEM_HEADS):
            cols = slice(h * MEM_HD, (h + 1) * MEM_HD)
            vcols = slice(MEM_W + h * MEM_HD, MEM_W + (h + 1) * MEM_HD)
            qh, kh, vh, doh = q_ref[0, :, cols], kv_ref[0, :, cols], kv_ref[0, :, vcols], do_ref[0, :, cols]
            p = _mem_probs(qh, kh)
            dp = _dot_nt(doh, vh)
            ds = p * (dp - jnp.sum(p * dp, axis=-1, keepdims=True))
            dss = (ds * (MEM_HD ** -0.5)).astype(BF16)
            dq_ref[0, :, cols] = _dot(dss, kh).astype(BF16)
            dkv_ref[0, :, cols] += _dot_tn(dss, qh)
            dkv_ref[0, :, vcols] += _dot_tn(p.astype(BF16), doh)

    qspec = pl.BlockSpec((1, tq, MEM_W), lambda b, i: (b, i, 0))
    kvspec = pl.BlockSpec((1, M, 2 * MEM_W), lambda b, i: (b, 0, 0))
    return pl.pallas_call(
        body, grid=(NB, S // tq), name="mematt_bwd", in_specs=[qspec, kvspec, qspec], out_specs=[qspec, kvspec],
        out_shape=[jax.ShapeDtypeStruct((NB, S, MEM_W), BF16), jax.ShapeDtypeStruct((NB, M, 2 * MEM_W), F32)],
        compiler_params=_params("arbitrary", "arbitrary"),
    )(mq, kv, do)


def _branch(j, in_ref, w_ref, gl_ref, bg_ref):
    y = _dot(in_ref[...], w_ref[...])
    gate = _sigmoid(gl_ref[:, j * D:(j + 1) * D].astype(F32) + bg_ref[:, j * D:(j + 1) * D])
    return y, gate


def _combine_fwd(x, cact, oatt, omem, gl, bg, wpw, wo, wmo, wout, tm):
    T = x.shape[0]

    def body(x_ref, c_ref, a_ref, m_ref, gl_ref, bg_ref, wpw_ref, wo_ref, wmo_ref, wout_ref, xo_ref, y_ref):
        y = None
        for j, (in_ref, w_ref) in enumerate(((c_ref, wpw_ref), (a_ref, wo_ref), (m_ref, wmo_ref))):
            yj, gate = _branch(j, in_ref, w_ref, gl_ref, bg_ref)
            y = gate * yj if y is None else y + gate * yj
        y = y.astype(BF16)
        y_ref[...] = y
        xo_ref[...] = x_ref[...] + _dot(y, wout_ref[...])

    row = lambda w: pl.BlockSpec((tm, w), lambda t: (t, 0))
    wbr = _const((512, D))
    return pl.pallas_call(
        body, grid=(T // tm,), name="combine_fwd",
        in_specs=[row(D), row(512), row(512), row(512), row(3 * D), _const((1, 3 * D)), wbr, wbr, wbr, _const((D, D))],
        out_specs=[row(D), row(D)],
        out_shape=[jax.ShapeDtypeStruct((T, D), F32), jax.ShapeDtypeStruct((T, D), BF16)],
        compiler_params=_params("parallel"),
    )(x, cact, oatt, omem, gl, bg, wpw, wo, wmo, wout)


def _combine_bwd(dx, cact, oatt, omem, gl, bg, wpw, wo, wmo, wout, tm, ride=None):
    T = dx.shape[0]

    def body(dx_ref, c_ref, a_ref, m_ref, gl_ref, bg_ref, wpw_ref, wo_ref, wmo_ref, wout_ref,
             dgl_ref, dc_ref, da_ref, dm_ref, dyc_ref, dya_ref, dym_ref, dbg_ref):
        dy = _dot_nt(dx_ref[...].astype(BF16), wout_ref[...])

        @pl.when(pl.program_id(0) == 0)
        def _():
            dbg_ref[...] = jnp.zeros_like(dbg_ref)
        branches = ((c_ref, wpw_ref, dyc_ref, dc_ref), (a_ref, wo_ref, dya_ref, da_ref), (m_ref, wmo_ref, dym_ref, dm_ref))
        for j, (in_ref, w_ref, dyb_ref, din_ref) in enumerate(branches):
            yj, gate = _branch(j, in_ref, w_ref, gl_ref, bg_ref)
            dyg = dy * gate
            dlogit = dyg * yj * (1.0 - gate)
            dgl_ref[:, j * D:(j + 1) * D] = dlogit.astype(BF16)
            dbg_ref[:, j * D:(j + 1) * D] += jnp.sum(dlogit, axis=0, keepdims=True)
            dyb = dyg.astype(BF16)
            dyb_ref[...] = dyb
            din_ref[...] = _dot_nt(dyb, w_ref[...]).astype(din_ref.dtype)

    row = lambda w: pl.BlockSpec((tm, w), lambda t: (t, 0))
    wbr = _const((512, D))
    sds = jax.ShapeDtypeStruct
    return _pallas(
        body, "combine_bwd", (T // tm,),
        [row(D), row(512), row(512), row(512), row(3 * D), _const((1, 3 * D)), wbr, wbr, wbr, _const((D, D))],
        [row(3 * D), row(512), row(512), row(512), row(D), row(D), row(D), pl.BlockSpec((1, 3 * D), lambda t: (0, 0))],
        [sds((T, 3 * D), BF16), sds((T, 512), F32), sds((T, 512), BF16), sds((T, 512), BF16),
         sds((T, D), BF16), sds((T, D), BF16), sds((T, D), BF16), sds((1, 3 * D), F32)],
        (dx, cact, oatt, omem, gl, bg, wpw, wo, wmo, wout), sem=("arbitrary",), ride=ride)


def _peer(x, y, c, rel):
    rx, ry, rc = (rel >> 2) & 1, (rel >> 1) & 1, rel & 1
    return ((1 - x) if rx else x, (1 - y) if ry else y, (1 - c) if rc else c)


def _all_sum_small(parts):
    n = len(parts)

    def body(*refs):
        p_refs, o_refs, slots = refs[:n], refs[n:2 * n], refs[2 * n:3 * n]
        send_sems, recv_sems = refs[3 * n:]
        x, y, c = _my_coords()
        me = _dev_index(x, y, c)

        def copy(i, rel, arrival):
            peer = _peer(x, y, c, rel)
            return pltpu.make_async_remote_copy(
                src_ref=p_refs[i], dst_ref=slots[i].at[_dev_index(*peer) if arrival else me],
                send_sem=send_sems.at[i, rel - 1], recv_sem=recv_sems.at[i, rel - 1], device_id=peer, device_id_type=MESH)

        for i in range(n):
            slots[i][me] = p_refs[i][...]
        for rel in range(1, NDEV):
            for i in range(n):
                copy(i, rel, False).start()
        for rel in range(1, NDEV):
            for i in range(n):
                copy(i, rel, True).wait_recv()
        for rel in range(1, NDEV):
            for i in range(n):
                copy(i, rel, False).wait_send()
        for i in range(n):
            total = slots[i][0]
            for d in range(1, NDEV):
                total = total + slots[i][d]
            o_refs[i][...] = total

    vmem = pl.BlockSpec(memory_space=pltpu.VMEM)
    return pl.pallas_call(
        body, out_shape=[jax.ShapeDtypeStruct(p.shape, F32) for p in parts], name="all_sum_small",
        in_specs=[vmem] * n, out_specs=[vmem] * n,
        scratch_shapes=[pltpu.VMEM((NDEV,) + p.shape, F32) for p in parts]
        + [pltpu.SemaphoreType.DMA((n, NDEV - 1)), pltpu.SemaphoreType.DMA((n, NDEV - 1))],
        compiler_params=pltpu.CompilerParams(has_side_effects=True),
    )(*parts)


HBM = pl.BlockSpec(memory_space=pltpu.HBM)
SEM = pl.BlockSpec(memory_space=pltpu.SEMAPHORE)


def _own_block(g, kind, m, tag):
    def body(g_ref, land_ref, staged, sem):
        me = _dev_index(*_my_coords())
        for cp in (pltpu.make_async_copy(_window(g_ref, kind, m, me), staged, sem),
                   pltpu.make_async_copy(staged, land_ref.at[me], sem)):
            cp.start()
            cp.wait()

    block = (m, g.shape[1]) if kind == 'row' else (g.shape[0], m)
    return pl.pallas_call(body, in_specs=[ANY], out_specs=ANY, out_shape=jax.ShapeDtypeStruct((NDEV,) + block, g.dtype),
                          scratch_shapes=[pltpu.VMEM(block, g.dtype), pltpu.SemaphoreType.DMA], name="own_block_" + tag)(g)


def _scatter_start(g, land, kind, m, tag):
    def body(g_ref, land_ref, send_sems, recv_sems, g_thru, land_thru, token):
        x, y, c = _my_coords()
        me = _dev_index(x, y, c)
        for rel in range(1, NDEV):
            peer = _peer(x, y, c, rel)
            pltpu.make_async_remote_copy(src_ref=_window(g_ref, kind, m, _dev_index(*peer)), dst_ref=land_ref.at[me],
                                         send_sem=send_sems.at[rel - 1], recv_sem=recv_sems.at[rel - 1],
                                         device_id=peer, device_id_type=MESH).start()
        token[...] = jnp.zeros_like(token)

    return pl.pallas_call(
        body, name="scatter_start_" + tag,
        out_shape=(pltpu.SemaphoreType.DMA((NDEV - 1,)), pltpu.SemaphoreType.DMA((NDEV - 1,)), pltpu.HBM(g.shape, g.dtype),
                   pltpu.HBM(land.shape, land.dtype), jax.ShapeDtypeStruct((8, 128), F32)),
        in_specs=(HBM, HBM), out_specs=(SEM, SEM, HBM, HBM, pl.BlockSpec(memory_space=pltpu.VMEM)),
        input_output_aliases={0: 2, 1: 3},
        compiler_params=pltpu.CompilerParams(has_side_effects=pltpu.SideEffectType.DATAFLOW_SIDE_EFFECTING),
    )(pltpu.with_memory_space_constraint(g, pltpu.HBM), pltpu.with_memory_space_constraint(land, pltpu.HBM))


def _scatter_wait(send_sems, recv_sems, g_thru, land_thru, after, kind, m, tag):
    n_after = len(after)

    def body(*refs):
        g_ref, land_ref, send_sems, recv_sems = refs[:4]
        x, y, c = _my_coords()
        me = _dev_index(x, y, c)
        for rel in range(1, NDEV):
            peer = _peer(x, y, c, rel)
            dev = _dev_index(*peer)
            cp = pltpu.make_async_remote_copy(src_ref=_window(g_ref, kind, m, me), dst_ref=land_ref.at[dev],
                                              send_sem=send_sems.at[rel - 1], recv_sem=recv_sems.at[rel - 1],
                                              device_id=peer, device_id_type=MESH)
            cp.wait_send()
            cp.wait_recv()

    return pl.pallas_call(
        body, name="scatter_wait_" + tag,
        out_shape=(pltpu.HBM(g_thru.shape, g_thru.dtype), pltpu.HBM(land_thru.shape, land_thru.dtype)),
        in_specs=(HBM, HBM, SEM, SEM) + (ANY,) * n_after, out_specs=(HBM, HBM), input_output_aliases={0: 0, 1: 1},
        compiler_params=pltpu.CompilerParams(has_side_effects=pltpu.SideEffectType.DATAFLOW_SIDE_EFFECTING),
    )(g_thru, land_thru, send_sems, recv_sems, *after)[1]


def _adamw_math(w, g, m, v):
    m = ADAM_B1 * m + (1.0 - ADAM_B1) * g
    v = ADAM_B2 * v + (1.0 - ADAM_B2) * (g * g)
    m_hat = m / (1.0 - ADAM_B1 ** ADAM_STEP)
    v_hat = v / (1.0 - ADAM_B2 ** ADAM_STEP)
    delta = -ADAM_LR * (m_hat / (jnp.sqrt(v_hat) + ADAM_EPS) + ADAM_WD * w)
    return delta, m, v


def _sum_adamw(parts, w, m, v, name, after=None):
    R, C = w.shape
    n_parts = len(parts)
    cg = C // n_parts
    tr = max(t for t in range(8, 257, 8) if R % t == 0)
    deps = [] if after is None else [after]

    def body(*refs):
        p_refs = refs[:n_parts]
        w_ref, m_ref, v_ref = refs[n_parts:n_parts + 3]
        g_ref, d_ref, mo_ref, vo_ref = refs[n_parts + 3 + len(deps):]
        for k, p_ref in enumerate(p_refs):
            @pl.when(pl.program_id(0) == k)
            def _():
                g = p_ref[0].astype(F32)
                for d in range(1, NDEV):
                    g = g + p_ref[d].astype(F32)
                g_ref[...] = g
                d_ref[...], mo_ref[...], vo_ref[...] = _adamw_math(w_ref[...], g, m_ref[...], v_ref[...])

    part = pl.BlockSpec((NDEV, tr, cg), lambda k, t: (0, t, 0))
    blk = pl.BlockSpec((tr, cg), lambda k, t: (t, k))
    return pl.pallas_call(
        body, grid=(n_parts, R // tr), name=name, in_specs=[part] * n_parts + [blk, blk, blk] + [ANY] * len(deps),
        out_specs=[blk] * 4, out_shape=[jax.ShapeDtypeStruct((R, C), F32)] * 4, compiler_params=_params("parallel", "parallel"),
    )(*parts, w, m, v, *deps)


def _adamw_small(ws, gs, ms, vs):
    n = len(ws)

    def body(*refs):
        w_refs, g_refs, m_refs, v_refs = (refs[k * n:(k + 1) * n] for k in range(4))
        d_refs, mo_refs, vo_refs = (refs[(4 + k) * n:(5 + k) * n] for k in range(3))
        for i in range(n):
            d_refs[i][...], mo_refs[i][...], vo_refs[i][...] = _adamw_math(w_refs[i][...], g_refs[i][...], m_refs[i][...], v_refs[i][...])

    shapes = [jax.ShapeDtypeStruct(a.shape, F32) for a in ws]
    outs = pl.pallas_call(body, out_shape=shapes * 3, name="adamw_small", compiler_params=_params())(*ws, *gs, *ms, *vs)
    return outs[:n], outs[n:2 * n], outs[2 * n:]


def kernel(x, mem, ffn1_norm, ffn1_w_up, ffn1_w_down, mix_norm, mem_norm, w_in, b_gate, conv_dw_w, conv_dw_b, conv_ln_g, conv_ln_b, conv_w_pw, att_rel_bias, att_w_o, mem_w_kv, mem_w_o, w_out, ffn2_norm, ffn2_w_up, ffn2_w_down, final_norm, loss_target, m_ffn1_norm, m_ffn1_w_up, m_ffn1_w_down, m_mix_norm, m_mem_norm, m_w_in, m_b_gate, m_conv_dw_w, m_conv_dw_b, m_conv_ln_g, m_conv_ln_b, m_conv_w_pw, m_att_rel_bias, m_att_w_o, m_mem_w_kv, m_mem_w_o, m_w_out, m_ffn2_norm, m_ffn2_w_up, m_ffn2_w_down, m_final_norm, v_ffn1_norm, v_ffn1_w_up, v_ffn1_w_down, v_mix_norm, v_mem_norm, v_w_in, v_b_gate, v_conv_dw_w, v_conv_dw_b, v_conv_ln_g, v_conv_ln_b, v_conv_w_pw, v_att_rel_bias, v_att_w_o, v_mem_w_kv, v_mem_w_o, v_w_out, v_ffn2_norm, v_ffn2_w_up, v_ffn2_w_down, v_final_norm):
    given = dict(locals())
    w = {n: given[n] for n in WEIGHTS}
    mom = {n: given["m_" + n] for n in WEIGHTS}
    var = {n: given["v_" + n] for n in WEIGHTS}

    NB, S, _ = x.shape
    T = NB * S
    ML = mem.shape[1]
    x0 = x.reshape(T, D)
    target = loss_target.reshape(T, D)
    mem2 = mem.reshape(NB * ML, D)

    def block(t, n):
        return jnp.transpose(t[0]) if n in TRANSPOSED else t[0]

    sh = dict(zip(BIG_ORDER, _cast_shards([block(w[n], n) for n in BIG_ORDER])))
    dw_t = jnp.transpose(conv_dw_w[0])

    def gather(names, extra=(), extra_kinds=()):
        return _gather_ride([sh[n] for n in names] + list(extra), [BIG[n] for n in names] + list(extra_kinds))

    W = {}
    names0 = ['ffn1_w_up', 'ffn1_w_down']
    tab, got = _bias_table(att_rel_bias[0], ride=gather(names0, [dw_t], [('row', dw_t.shape[0])]))
    W.update(zip(names0, got[:2]))
    dw_full = jnp.transpose(got[2])
    conv_vec = jnp.concatenate([conv_dw_b, conv_ln_g, conv_ln_b, jnp.zeros((5, CONV_W), F32)], axis=0)
    fin_g = final_norm.reshape(1, D)

    names1 = ['w_in', 'conv_w_pw', 'att_w_o', 'mem_w_kv', 'mem_w_o', 'w_out']
    (x1, ab1), got = _ffn_fwd(x0, ffn1_norm, W['ffn1_w_up'], W['ffn1_w_down'], TILE_FFN, "ffn1_fwd", ride=gather(names1))
    W.update(zip(names1, got))
    (uc, qkv, mq, gl, hmix), _ = _mix_fwd(x1, mix_norm, W['w_in'], TILE_TOKENS)
    uc3 = uc.reshape(NB, S, 2 * CONV_W)
    qkv3 = qkv.reshape(NB, S, 3 * ATT_W)
    mq3 = mq.reshape(NB, S, MEM_W)
    cact, conv_z = _conv_fwd(uc3, dw_full, conv_vec)
    cact = cact.reshape(T, CONV_W)
    names2 = ['ffn2_w_up', 'ffn2_w_down']
    oatt3, att_lse, got = _att_fwd(qkv3, tab, ride=gather(names2))
    W.update(zip(names2, got))
    oatt = oatt3.reshape(T, ATT_W)
    memh, kv = _memkv_fwd(mem2, mem_norm, W['mem_w_kv'], TILE_TOKENS)
    kv3 = kv.reshape(NB, ML, 2 * MEM_W)
    omem = _mematt_fwd(mq3, kv3, TILE_TOKENS).reshape(T, MEM_W)
    branch_w = (W['conv_w_pw'], W['att_w_o'], W['mem_w_o'], W['w_out'])
    x2, ymix = _combine_fwd(x1, cact, oatt, omem, gl, b_gate, *branch_w, TILE_COMBINE)
    dx3, ab2, loss_part, dg_final = _ffn_fwd_loss(x2, ffn2_norm, W['ffn2_w_up'], W['ffn2_w_down'], fin_g, target, TILE_FFN,
                                                  "ffn2_fwd_loss")

    def scatter(grads, names):
        return _scatter_ride(grads, [BIG[n] for n in names])

    G, P = {}, {}
    dx2, dab2, act2, h2, dg_ffn2 = _ffn_bwd(x2, dx3, ab2, ffn2_norm, W['ffn2_w_up'], W['ffn2_w_down'], TILE_FFN, "ffn2_bwd")
    g_up, _ = _tn_matmul(dab2, h2, 512, "grad_ffn2_w_up_a", tt=TILE_GRAD_TOKENS_WIDE, x_part=(0, 2), out_rows=2 * FF)
    G['ffn2_w_up'], _ = _tn_matmul(dab2, h2, 512, "grad_ffn2_w_up_b", tt=TILE_GRAD_TOKENS_WIDE, x_part=(1, 2), out_rows=2 * FF,
                                   prev=g_up)
    G['ffn2_w_down'], _ = _tn_matmul(act2, dx3, 512, "grad_ffn2_w_down", scale=0.5, tt=TILE_GRAD_TOKENS_WIDE)
    (dgl, dcact, doatt, domem, dyc, dya, dym, dbg), got = _combine_bwd(
        dx2, cact, oatt, omem, gl, b_gate, *branch_w, TILE_COMBINE, ride=scatter([G['ffn2_w_up']], ['ffn2_w_up']))
    P['ffn2_w_up'] = got
    G['w_out'], _ = _tn_matmul(ymix, dx2, D, "grad_w_out", tt=TILE_GRAD_TOKENS_WIDE)
    G['conv_w_pw'], _ = _tn_matmul(cact, dyc, D, "grad_conv_w_pw")
    G['att_w_o'], _ = _tn_matmul(oatt, dya, D, "grad_att_w_o")
    G['mem_w_o'], _ = _tn_matmul(omem, dym, D, "grad_mem_w_o")
    dmq3, dkv3 = _mematt_bwd(mq3, kv3, domem.reshape(NB, S, MEM_W), TILE_TOKENS)
    dkv = dkv3.reshape(NB * ML, 2 * MEM_W)
    dg_mem = _memkv_bwd(mem2, dkv, W['mem_w_kv'], TILE_TOKENS)
    G['mem_w_kv'], _ = _tn_matmul(memh, dkv, 512, "grad_mem_w_kv")
    names = ['ffn2_w_down', 'w_out', 'conv_w_pw', 'att_w_o', 'mem_w_o']
    (dqkv3, dscore), got = _att_bwd(qkv3, oatt3, att_lse, doatt.reshape(NB, S, ATT_W), tab,
                                    ride=scatter([G[n] for n in names], names))
    P.update((n, [p]) for n, p in zip(names, got))
    d_rel = _rel_bias_grad(dscore)
    (duc3, d_dw, d_cvec), got = _conv_bwd(uc3, conv_z, dcact.reshape(NB, S, CONV_W), dw_full, conv_vec,
                                          ride=scatter([G['mem_w_kv']], ['mem_w_kv']))
    P['mem_w_kv'] = got
    duc, dqkv, dmq = duc3.reshape(T, 2 * CONV_W), dqkv3.reshape(T, 3 * ATT_W), dmq3.reshape(T, MEM_W)
    g_in, _ = _tn_matmul(hmix, duc, 1024, "grad_w_in_conv", out_cols=IN_COLS, col_off=0)
    g_in, _ = _tn_matmul(hmix, dqkv, 512, "grad_w_in_qkv", out_cols=IN_COLS, col_off=1024, prev=g_in)
    g_in, _ = _tn_matmul(hmix, dmq, 512, "grad_w_in_mq", out_cols=IN_COLS, col_off=2560, prev=g_in)
    G['w_in'], _ = _tn_matmul(hmix, dgl, 1024, "grad_w_in_gate", out_cols=IN_COLS, col_off=3072, prev=g_in)
    def start_scatter(g, name, tag):
        kind = BIG[name]
        return _scatter_start(g, _own_block(g, *kind, tag), *kind, tag) + (kind, tag)

    def wait_scatter(started, after):
        send_sems, recv_sems, g_thru, land_thru, _, kind, tag = started
        return _scatter_wait(send_sems, recv_sems, g_thru, land_thru, after, *kind, tag)

    ex_in = start_scatter(G['w_in'], 'w_in', "w_in")
    (dx1, dg_mix), _ = _mix_bwd(x1, dx2, duc, dqkv, dmq, dgl, mix_norm, W['w_in'], TILE_TOKENS, after=ex_in[4])
    dx0, dab1, act1, h1, dg_ffn1 = _ffn_bwd(x0, dx1, ab1, ffn1_norm, W['ffn1_w_up'], W['ffn1_w_down'], TILE_FFN, "ffn1_bwd")
    g_wd1, _ = _tn_matmul(act1, dx1, 512, "grad_ffn1_w_down", scale=0.5, tt=TILE_GRAD_TOKENS_WIDE)
    ex_wd = start_scatter(g_wd1, 'ffn1_w_down', "ffn1_w_down")
    g_wu1a, _ = _tn_matmul(dab1, h1, 512, "grad_ffn1_w_up_a", tt=TILE_GRAD_TOKENS_WIDEST, y_part=(0, 2), after=ex_wd[4])
    ex_a = start_scatter(g_wu1a, 'ffn1_w_up', "ffn1_w_up_a")
    g_wu1b, _ = _tn_matmul(dab1, h1, 512, "grad_ffn1_w_up_b", tt=TILE_GRAD_TOKENS_WIDEST, y_part=(1, 2), after=ex_a[4])
    ex_b = start_scatter(g_wu1b, 'ffn1_w_up', "ffn1_w_up_b")
    token = ex_b[4]

    small_names = ['loss', 'ffn1_norm', 'mix_norm', 'mem_norm', 'b_gate', 'conv_dw_w', 'conv_vec', 'att_rel_bias', 'ffn2_norm',
                   'final_norm']
    small = dict(zip(small_names, _all_sum_small(
        [loss_part + token[0:1], dg_ffn1, dg_mix, dg_mem, dbg, d_dw, d_cvec, d_rel, dg_ffn2, dg_final])))
    loss = small['loss'][0, 0]
    me = _dev_index(*_my_coords())
    for i, n in enumerate(['conv_dw_b', 'conv_ln_g', 'conv_ln_b']):
        small[n] = small['conv_vec'][i:i + 1]
    small['conv_dw_w'] = lax.dynamic_slice(small['conv_dw_w'], (0, me * conv_dw_w.shape[2]), (CONV_K, conv_dw_w.shape[2]))
    little = [n for n in WEIGHTS if n not in BIG]
    as2d = lambda t, n: t.reshape(small[n].shape)
    d_s, m_s, v_s = _adamw_small([as2d(w[n], n) for n in little], [small[n] for n in little],
                                 [as2d(mom[n], n) for n in little], [as2d(var[n], n) for n in little])
    grad, delta, new_m, new_v = {}, {}, {}, {}
    for i, n in enumerate(little):
        grad[n], delta[n], new_m[n], new_v[n] = (t.reshape(w[n].shape) for t in (small[n], d_s[i], m_s[i], v_s[i]))
    done = [d_s[0]]
    waited = {'w_in': [ex_in], 'ffn1_w_down': [ex_wd], 'ffn1_w_up': [ex_a, ex_b]}
    order = [n for n in BIG_ORDER if n not in waited] + list(waited)
    for n in order:
        if n in waited:
            P[n] = [wait_scatter(ex, done) for ex in waited[n]]
        outs = _sum_adamw(P[n], block(w[n], n), block(mom[n], n), block(var[n], n), "adamw_" + n,
                          after=None if n in waited else token)
        done.append(outs[0])
        grad[n], delta[n], new_m[n], new_v[n] = ((jnp.transpose(t) if n in TRANSPOSED else t)[None] for t in outs)

    return (loss, dx0.reshape(NB, S, D), *[grad[n] for n in WEIGHTS], *[delta[n] for n in WEIGHTS],
            *[new_m[n] for n in WEIGHTS], *[new_v[n] for n in WEIGHTS])
```

```python
import functools

import jax
import jax.numpy as jnp
from jax import lax
from jax.experimental import pallas as pl
from jax.experimental.pallas import tpu as pltpu

F32 = jnp.float32
BF16 = jnp.bfloat16

EPS = 1e-6
MASK_VALUE = -1e30
D = 1024
NDEV = 8
FF = 2816
FF_SHARD = 704
FF_HALF_ROWS = 352
FF_BLOCK_EDGES = ()
IN_COLS = 6144
CONV_W = 512
CONV_K = 31
CONV_HALO = 32
CONV_CHUNK = 32
CONV_WIN = CONV_CHUNK + 40
GLU_CHUNK = 128
ATT_W = 512
ATT_HEADS = 8
ATT_HD = 64
CHUNK = 64
LEFT_CHUNKS = 8
MAX_REL = 128
N_REL = 192
QB = 256
KWIN = QB + LEFT_CHUNKS * CHUNK
KPAD = LEFT_CHUNKS * CHUNK
DS_LANES = 1024
MEM_W = 512
MEM_HEADS = 4
MEM_HD = 128
ADAM_LR = 0.001
ADAM_B1 = 0.9
ADAM_B2 = 0.999
ADAM_EPS = 1e-08
ADAM_WD = 0.01
ADAM_STEP = 10
VMEM_LIMIT = 60 * 1024 * 1024
TILE_FFN = 256
TILE_FFN_FWD = 512
TILE_COMBINE = 256
TILE_TOKENS = 512
TILE_GRAD_TOKENS = 2048
TILE_GRAD_TOKENS_WIDE = 1024
TILE_GRAD_TOKENS_WIDEST = 512

MESH = pl.DeviceIdType.MESH
ANY = pl.BlockSpec(memory_space=pl.ANY)

WEIGHTS = ['ffn1_norm', 'ffn1_w_up', 'ffn1_w_down', 'mix_norm', 'mem_norm', 'w_in', 'b_gate', 'conv_dw_w', 'conv_dw_b',
           'conv_ln_g', 'conv_ln_b', 'conv_w_pw', 'att_rel_bias', 'att_w_o', 'mem_w_kv', 'mem_w_o', 'w_out', 'ffn2_norm',
           'ffn2_w_up', 'ffn2_w_down', 'final_norm']
BIG = {
    'ffn1_w_up': ('row', FF_SHARD), 'ffn1_w_down': ('row', FF_HALF_ROWS), 'w_in': ('col', 768),
    'conv_w_pw': ('col', 128), 'att_w_o': ('col', 128), 'mem_w_kv': ('row', 128), 'mem_w_o': ('col', 128),
    'w_out': ('row', 128), 'ffn2_w_up': ('row', FF_SHARD), 'ffn2_w_down': ('row', FF_HALF_ROWS),
}
BIG_ORDER = ['ffn1_w_up', 'ffn1_w_down', 'w_in', 'conv_w_pw', 'att_w_o', 'mem_w_kv', 'mem_w_o', 'w_out', 'ffn2_w_up', 'ffn2_w_down']
TRANSPOSED = ('ffn1_w_up', 'ffn2_w_up')


def _dot(a, b):
    return jnp.dot(a, b, preferred_element_type=F32)


def _dot_nt(a, b):
    return lax.dot_general(a, b, (((1,), (1,)), ((), ())), preferred_element_type=F32)


def _dot_tn(a, b):
    return lax.dot_general(a, b, (((0,), (0,)), ((), ())), preferred_element_type=F32)


def _sigmoid(v):
    return jax.nn.sigmoid(v)


def _const(shape):
    return pl.BlockSpec(shape, lambda *_: (0,) * len(shape), pipeline_mode=pl.Buffered(1))


def _params(*sem):
    return pltpu.CompilerParams(dimension_semantics=sem if sem else None, vmem_limit_bytes=VMEM_LIMIT)


def _my_coords():
    return lax.axis_index("x"), lax.axis_index("y"), lax.axis_index("c")


def _dev_index(px, py, pc):
    return 4 * px + 2 * py + pc


def _window(ref, kind, n, p):
    if kind == 'row':
        return ref.at[pl.ds(pl.multiple_of(p * n, n), n), :]
    return ref.at[:, pl.ds(pl.multiple_of(p * n, 128), n)]


def _full_shape(kind, n, shard_shape):
    if kind == 'row':
        return (NDEV * n, shard_shape[1])
    return (shard_shape[0], NDEV * n)


def _cast_shards(shards):
    n = len(shards)

    def body(*refs):
        for i in range(n):
            refs[n + i][...] = refs[i][...].astype(BF16)

    out_shape = [jax.ShapeDtypeStruct(s.shape, BF16) for s in shards]
    return pl.pallas_call(body, out_shape=out_shape, name="cast_shards", compiler_params=_params())(*shards)


class _Ride:
    def __init__(self, inputs, out_shape, scratch, start, finish, mids=()):
        self.inputs, self.out_shape, self.scratch = list(inputs), list(out_shape), list(scratch)
        self.start, self.finish, self.mids = start, finish, tuple(mids)


def _pallas(body, name, grid, in_specs, out_specs, out_shape, args, scratch_shapes=(), sem=None, aliases=None, ride=None,
            after=None):
    if ride is None:
        n_in, n_dep = len(args), 0 if after is None else 1

        def kernel_body(*refs):
            body(*refs[:n_in], *refs[n_in + n_dep:])

        outs = pl.pallas_call(kernel_body if n_dep else body, grid=grid, name=name, in_specs=list(in_specs) + [ANY] * n_dep,
                              out_specs=out_specs, out_shape=out_shape, scratch_shapes=list(scratch_shapes),
                              input_output_aliases=aliases or {}, compiler_params=_params(*sem),
                              )(*args, *([after] if n_dep else []))
        return list(outs), []
    n_in, n_out, n_scr = len(args), len(out_shape), len(scratch_shapes)
    r_in, r_out = len(ride.inputs), len(ride.out_shape)

    def wrapped(*refs):
        k_in, rin = refs[:n_in], refs[n_in:n_in + r_in]
        o0 = n_in + r_in
        k_out, rout = refs[o0:o0 + n_out], refs[o0 + n_out:o0 + n_out + r_out]
        s0 = o0 + n_out + r_out
        k_scr, rscr = refs[s0:s0 + n_scr], refs[s0 + n_scr:]
        ids = [pl.program_id(k) for k in range(len(grid))]
        first = functools.reduce(jnp.logical_and, [i == 0 for i in ids])
        last = functools.reduce(jnp.logical_and, [i == g - 1 for i, g in zip(ids, grid)])
        pl.when(first)(lambda: ride.start(rin, rout, rscr))
        single_step = all(g == 1 for g in grid)
        for quarter, mid in ride.mids:
            if not single_step:
                at_mid = functools.reduce(jnp.logical_and, [ids[0] == (quarter * grid[0]) // 4] + [i == 0 for i in ids[1:]])
                pl.when(at_mid)(functools.partial(mid, rin, rout, rscr))
        body(*k_in, *k_out, *k_scr)
        for _, mid in ride.mids:
            if single_step:
                mid(rin, rout, rscr)
        pl.when(last)(lambda: ride.finish(rin, rout, rscr))

    outs = pl.pallas_call(
        wrapped, grid=grid, name=name, in_specs=list(in_specs) + [ANY] * r_in, out_specs=list(out_specs) + [ANY] * r_out,
        out_shape=list(out_shape) + ride.out_shape, scratch_shapes=list(scratch_shapes) + ride.scratch,
        input_output_aliases=aliases or {}, compiler_params=_params(*(["arbitrary"] * len(grid))),
    )(*args, *ride.inputs)
    return list(outs[:n_out]), list(outs[n_out:])


def _gather_ride(shards, kinds):
    n = len(shards)

    def plan(rin, out, sems):
        send_sems, recv_sems, local_sems = sems[:3]
        x, y, c = _my_coords()
        me, sibling = (x, y, c), (x, y, 1 - c)
        xn, yn, diag = (1 - x, y), (x, 1 - y), (1 - x, 1 - y)

        def win(i, dev):
            return _window(out[i], kinds[i][0], kinds[i][1], _dev_index(*dev))

        def copy(i, k, block, to, from_shard=False):
            return pltpu.make_async_remote_copy(
                src_ref=rin[i] if from_shard else win(i, block), dst_ref=win(i, block),
                send_sem=send_sems.at[i, k], recv_sem=recv_sems.at[i, k], device_id=to, device_id_type=MESH)

        def each(fn):
            return [fn(i) for i in range(n)]

        return dict(
            local=lambda: each(lambda i: pltpu.make_async_copy(rin[i], win(i, me), local_sems.at[i])),
            own=lambda: [cp for i in range(n) for cp in (copy(i, 0, me, sibling, True), copy(i, 1, me, (*xn, c), True),
                                                         copy(i, 2, me, (*yn, c), True))],
            from_x=lambda: each(lambda i: copy(i, 1, (*xn, c), me)),
            from_y=lambda: each(lambda i: copy(i, 2, (*yn, c), me)),
            x_block_on_to_y=lambda: each(lambda i: copy(i, 3, (*xn, c), (*yn, c))),
            y_block_on_to_x=lambda: each(lambda i: copy(i, 3, (*yn, c), (*xn, c))),
            from_diag=lambda: each(lambda i: copy(i, 3, (*diag, c), me)),
            to_sibling=lambda: [copy(i, 4 + j, (*chip, c), sibling) for j, chip in enumerate((xn, yn, diag)) for i in range(n)],
            from_sibling=lambda: [cp for i in range(n) for cp in
                                  [copy(i, 0, sibling, me)] + [copy(i, 4 + j, (*chip, 1 - c), me) for j, chip in enumerate((xn, yn, diag))]],
            north=c == 1)

    def start(rin, out, sems):
        p = plan(rin, out, sems)
        for cp in p['local']() + p['own']():
            cp.start()

    def pass_diagonal(rin, out, sems):
        p = plan(rin, out, sems)

        @pl.when(p['north'])
        def _():
            for got, fwd in zip(p['from_x'](), p['x_block_on_to_y']()):
                got.wait_recv()
                fwd.start()

        @pl.when(jnp.logical_not(p['north']))
        def _():
            for got, fwd in zip(p['from_y'](), p['y_block_on_to_x']()):
                got.wait_recv()
                fwd.start()

    def pass_to_sibling(rin, out, sems):
        p = plan(rin, out, sems)

        @pl.when(p['north'])
        def _():
            for cp in p['from_y']():
                cp.wait_recv()

        @pl.when(jnp.logical_not(p['north']))
        def _():
            for cp in p['from_x']():
                cp.wait_recv()
        for cp in p['from_diag']():
            cp.wait_recv()
        for cp in p['to_sibling']():
            cp.start()

    def finish(rin, out, sems):
        p = plan(rin, out, sems)
        for cp in p['from_sibling']():
            cp.wait_recv()
        for cp in p['own']() + p['to_sibling']():
            cp.wait_send()

        @pl.when(p['north'])
        def _():
            for cp in p['x_block_on_to_y']():
                cp.wait_send()

        @pl.when(jnp.logical_not(p['north']))
        def _():
            for cp in p['y_block_on_to_x']():
                cp.wait_send()
        for cp in p['local']():
            cp.wait()

    out_shape = [jax.ShapeDtypeStruct(_full_shape(k, m, s.shape), s.dtype) for s, (k, m) in zip(shards, kinds)]
    scratch = [pltpu.SemaphoreType.DMA((n, 7)), pltpu.SemaphoreType.DMA((n, 7)), pltpu.SemaphoreType.DMA((n,))]
    return _Ride(shards, out_shape, scratch, start, finish, mids=((2, pass_diagonal), (3, pass_to_sibling)))


def _scatter_ride(grads, kinds):
    n = len(grads)

    def plan(g, out, sems):
        send_sems, recv_sems, local_sems = sems
        x, y, c = _my_coords()
        me = _dev_index(x, y, c)

        def local():
            return [pltpu.make_async_copy(_window(g[i], kinds[i][0], kinds[i][1], me), out[i].at[me], local_sems.at[i])
                    for i in range(n)]

        def remote(arrival):
            cps = []
            for rel in range(1, NDEV):
                peer = _peer(x, y, c, rel)
                dev = _dev_index(*peer)
                for i in range(n):
                    kind, m = kinds[i]
                    cps.append(pltpu.make_async_remote_copy(
                        src_ref=_window(g[i], kind, m, me if arrival else dev), dst_ref=out[i].at[dev if arrival else me],
                        send_sem=send_sems.at[i, rel - 1], recv_sem=recv_sems.at[i, rel - 1], device_id=peer, device_id_type=MESH))
            return cps

        return local, remote

    def start(g, out, sems):
        local, remote = plan(g, out, sems)
        for cp in local() + remote(False):
            cp.start()

    def finish(g, out, sems):
        local, remote = plan(g, out, sems)
        for cp in remote(True):
            cp.wait_recv()
        for cp in remote(False):
            cp.wait_send()
        for cp in local():
            cp.wait()

    def block_shape(gr, kind, m):
        return (m, gr.shape[1]) if kind == 'row' else (gr.shape[0], m)

    out_shape = [jax.ShapeDtypeStruct((NDEV,) + block_shape(gr, k, m), gr.dtype) for gr, (k, m) in zip(grads, kinds)]
    scratch = [pltpu.SemaphoreType.DMA((n, NDEV - 1)), pltpu.SemaphoreType.DMA((n, NDEV - 1)), pltpu.SemaphoreType.DMA((n,))]
    return _Ride(grads, out_shape, scratch, start, finish)


def _rms_stats(xf):
    r = lax.rsqrt(jnp.mean(xf * xf, axis=-1, keepdims=True) + EPS)
    return xf * r, r


def _rms_bwd(dh, g, xhat, r):
    dxhat = dh * g
    return r * (dxhat - xhat * jnp.mean(dxhat * xhat, axis=-1, keepdims=True))


def _ffn_blocks():
    edges = (0,) + FF_BLOCK_EDGES + (FF,)
    return [(slice(lo, hi), slice(FF + lo, FF + hi)) for lo, hi in zip(edges[:-1], edges[1:])]


def _swiglu_tile(x_ref, g_ref, wut_ref, wd_ref, ab_ref):
    xf = x_ref[...]
    xhat, _ = _rms_stats(xf)
    h = (xhat * g_ref[...]).astype(BF16)
    acc = jnp.zeros(xf.shape, F32)
    for ra, rb in _ffn_blocks():
        a = _dot_nt(h, wut_ref[ra, :])
        b = _dot_nt(h, wut_ref[rb, :])
        ab_ref[:, ra] = a.astype(BF16)
        ab_ref[:, rb] = b.astype(BF16)
        act = (a * _sigmoid(a) * b).astype(BF16)
        acc = acc + _dot(act, wd_ref[ra, :])
    return xf + 0.5 * acc


def _ffn_fwd(x, g, wut, wd, tm, name, ride=None):
    T = x.shape[0]

    def body(x_ref, g_ref, wut_ref, wd_ref, xo_ref, ab_ref):
        xo_ref[...] = _swiglu_tile(x_ref, g_ref, wut_ref, wd_ref, ab_ref)

    return _pallas(
        body, name, (T // tm,),
        [pl.BlockSpec((tm, D), lambda t: (t, 0)), _const((1, D)), _const((2 * FF, D)), _const((FF, D))],
        [pl.BlockSpec((tm, D), lambda t: (t, 0)), pl.BlockSpec((tm, 2 * FF), lambda t: (t, 0))],
        [jax.ShapeDtypeStruct((T, D), F32), jax.ShapeDtypeStruct((T, 2 * FF), BF16)],
        (x, g, wut, wd), sem=("arbitrary",), ride=ride)


def _ffn_fwd_loss(x, g, wut, wd, g_final, target, tm, name):
    T = x.shape[0]

    def body(x_ref, g_ref, wut_ref, wd_ref, gf_ref, t_ref, dx_ref, ab_ref, loss_ref, dgf_ref):
        xhat, r = _rms_stats(_swiglu_tile(x_ref, g_ref, wut_ref, wd_ref, ab_ref))
        gain = gf_ref[...]
        diff = xhat * gain - t_ref[...]
        dout = diff * (1.0 / D)

        @pl.when(pl.program_id(0) == 0)
        def _():
            loss_ref[...] = jnp.zeros_like(loss_ref)
            dgf_ref[...] = jnp.zeros_like(dgf_ref)
        sq = jnp.sum(jnp.sum(diff * diff, axis=0, keepdims=True), axis=1, keepdims=True)
        loss_ref[...] += jnp.broadcast_to(sq * (0.5 / D), (1, 128))
        dgf_ref[...] += jnp.sum(dout * xhat, axis=0, keepdims=True)
        dx_ref[...] = _rms_bwd(dout, gain, xhat, r)

    row = pl.BlockSpec((tm, D), lambda t: (t, 0))
    return pl.pallas_call(
        body, grid=(T // tm,), name=name,
        in_specs=[row, _const((1, D)), _const((2 * FF, D)), _const((FF, D)), _const((1, D)), row],
        out_specs=[row, pl.BlockSpec((tm, 2 * FF), lambda t: (t, 0)), pl.BlockSpec((1, 128), lambda t: (0, 0)),
                   pl.BlockSpec((1, D), lambda t: (0, 0))],
        out_shape=[jax.ShapeDtypeStruct((T, D), F32), jax.ShapeDtypeStruct((T, 2 * FF), BF16),
                   jax.ShapeDtypeStruct((1, 128), F32), jax.ShapeDtypeStruct((1, D), F32)],
        compiler_params=_params("arbitrary"),
    )(x, g, wut, wd, g_final, target)


def _ffn_bwd(x, dy, ab, g, wut, wd, tm, name):
    T = x.shape[0]

    def body(x_ref, dy_ref, ab_ref, g_ref, wut_ref, wd_ref, dx_ref, dab_ref, act_ref, h_ref, dg_ref):
        xf = x_ref[...]
        xhat, r = _rms_stats(xf)
        gain = g_ref[...]
        h_ref[...] = (xhat * gain).astype(BF16)
        dy = dy_ref[...]
        dyh = (0.5 * dy).astype(BF16)
        dh = jnp.zeros((tm, D), F32)
        for ra, rb in _ffn_blocks():
            a = ab_ref[:, ra].astype(F32)
            b = ab_ref[:, rb].astype(F32)
            dact = _dot_nt(dyh, wd_ref[ra, :])
            sg = _sigmoid(a)
            sl = a * sg
            act_ref[:, ra] = (sl * b).astype(BF16)
            da = (dact * b * (sg * (1.0 + a * (1.0 - sg)))).astype(BF16)
            db = (dact * sl).astype(BF16)
            dab_ref[:, ra] = da
            dab_ref[:, rb] = db
            dh = dh + _dot(da, wut_ref[ra, :]) + _dot(db, wut_ref[rb, :])
        dx_ref[...] = dy + _rms_bwd(dh, gain, xhat, r)

        @pl.when(pl.program_id(0) == 0)
        def _():
            dg_ref[...] = jnp.zeros_like(dg_ref)
        dg_ref[...] += jnp.sum(dh * xhat, axis=0, keepdims=True)

    return pl.pallas_call(
        body, grid=(T // tm,), name=name,
        in_specs=[pl.BlockSpec((tm, D), lambda t: (t, 0)), pl.BlockSpec((tm, D), lambda t: (t, 0)),
                  pl.BlockSpec((tm, 2 * FF), lambda t: (t, 0)), _const((1, D)), _const((2 * FF, D)), _const((FF, D))],
        out_specs=[pl.BlockSpec((tm, D), lambda t: (t, 0)), pl.BlockSpec((tm, 2 * FF), lambda t: (t, 0)),
                   pl.BlockSpec((tm, FF), lambda t: (t, 0)), pl.BlockSpec((tm, D), lambda t: (t, 0)),
                   pl.BlockSpec((1, D), lambda t: (0, 0))],
        out_shape=[jax.ShapeDtypeStruct((T, D), F32), jax.ShapeDtypeStruct((T, 2 * FF), BF16),
                   jax.ShapeDtypeStruct((T, FF), BF16), jax.ShapeDtypeStruct((T, D), BF16), jax.ShapeDtypeStruct((1, D), F32)],
        compiler_params=_params("arbitrary"),
    )(x, dy, ab, g, wut, wd)


def _tn_matmul(xm, ym, tn, name, scale=None, out_cols=None, col_off=0, prev=None, tt=TILE_GRAD_TOKENS, x_part=(0, 1),
               out_rows=None, y_part=(0, 1), ride=None, after=None):
    T = xm.shape[0]
    xi, xn = x_part
    yi, yn = y_part
    K = xm.shape[1] // xn
    N = ym.shape[1] // yn
    out_cols = N if out_cols is None else out_cols
    row_blk = xi if out_rows is not None else 0
    out_rows = K if out_rows is None else out_rows
    tt = min(tt, T)
    nt = T // tt
    off = col_off // tn

    def body(*refs):
        x_ref, y_ref = refs[0], refs[1]
        o_ref, acc = refs[-2], refs[-1]

        @pl.when(pl.program_id(1) == 0)
        def _():
            acc[...] = jnp.zeros_like(acc)
        acc[...] += _dot_tn(x_ref[...].astype(BF16), y_ref[...].astype(BF16))

        @pl.when(pl.program_id(1) == nt - 1)
        def _():
            res = acc[...]
            o_ref[...] = (res if scale is None else res * scale).astype(BF16)

    ycol = yi * (N // tn)
    in_specs = [pl.BlockSpec((tt, K), lambda n, t: (t, xi)), pl.BlockSpec((tt, tn), lambda n, t: (t, n + ycol))]
    args = [xm, ym]
    aliases = {}
    if prev is not None:
        in_specs.append(ANY)
        args.append(prev)
        aliases = {2: 0}
    outs, rode = _pallas(
        body, name, (N // tn, nt), in_specs, [pl.BlockSpec((K, tn), lambda n, t: (row_blk, n + off))],
        [jax.ShapeDtypeStruct((out_rows, out_cols), BF16)], args, scratch_shapes=[pltpu.VMEM((K, tn), F32)],
        sem=("parallel", "arbitrary"), aliases=aliases, ride=ride, after=after)
    return outs[0], rode


def _mix_fwd(x, g, w_in, tm, ride=None):
    T = x.shape[0]

    def body(x_ref, g_ref, w_ref, uc_ref, qkv_ref, mq_ref, gl_ref, h_ref):
        xhat, _ = _rms_stats(x_ref[...])
        h = (xhat * g_ref[...]).astype(BF16)
        h_ref[...] = h
        uc_ref[...] = _dot(h, w_ref[:, 0:1024])
        qkv_ref[...] = _dot(h, w_ref[:, 1024:2560]).astype(BF16)
        mq_ref[...] = _dot(h, w_ref[:, 2560:3072]).astype(BF16)
        for j in range(3):
            gl_ref[:, j * D:(j + 1) * D] = _dot(h, w_ref[:, 3072 + j * D:3072 + (j + 1) * D]).astype(BF16)

    row = lambda w: pl.BlockSpec((tm, w), lambda t: (t, 0))
    return _pallas(
        body, "mix_fwd", (T // tm,), [row(D), _const((1, D)), _const((D, IN_COLS))],
        [row(1024), row(1536), row(512), row(3072), row(D)],
        [jax.ShapeDtypeStruct((T, 1024), F32), jax.ShapeDtypeStruct((T, 1536), BF16), jax.ShapeDtypeStruct((T, 512), BF16),
         jax.ShapeDtypeStruct((T, 3072), BF16), jax.ShapeDtypeStruct((T, D), BF16)],
        (x, g, w_in), sem=("parallel",), ride=ride)


def _mix_bwd(x, dres, duc, dqkv, dmq, dgl, g, w_in, tm, ride=None, after=None):
    T = x.shape[0]

    def body(x_ref, dres_ref, duc_ref, dqkv_ref, dmq_ref, dgl_ref, g_ref, w_ref, dx_ref, dg_ref):
        xhat, r = _rms_stats(x_ref[...])
        dh = _dot_nt(duc_ref[...], w_ref[:, 0:1024])
        dh = dh + _dot_nt(dqkv_ref[...], w_ref[:, 1024:2560])
        dh = dh + _dot_nt(dmq_ref[...], w_ref[:, 2560:3072])
        dh = dh + _dot_nt(dgl_ref[...], w_ref[:, 3072:6144])
        dx_ref[...] = dres_ref[...] + _rms_bwd(dh, g_ref[...], xhat, r)

        @pl.when(pl.program_id(0) == 0)
        def _():
            dg_ref[...] = jnp.zeros_like(dg_ref)
        dg_ref[...] += jnp.sum(dh * xhat, axis=0, keepdims=True)

    row = lambda w: pl.BlockSpec((tm, w), lambda t: (t, 0))
    return _pallas(
        body, "mix_bwd", (T // tm,),
        [row(D), row(D), row(1024), row(1536), row(512), row(3072), _const((1, D)), _const((D, IN_COLS))],
        [row(D), pl.BlockSpec((1, D), lambda t: (0, 0))],
        [jax.ShapeDtypeStruct((T, D), F32), jax.ShapeDtypeStruct((1, D), F32)],
        (x, dres, duc, dqkv, dmq, dgl, g, w_in), sem=("arbitrary",), ride=ride, after=after)


def _shifted(win, base, copies):
    for k in range(8):
        copies[k] = win[base + k:base + k + CONV_CHUNK + 24]
    return copies


def _tap_slices(copies, tap):
    out = []
    for k in range(8):
        for a in range(4):
            j = tap(a, k)
            if 0 <= j < CONV_K:
                out.append((j, copies[k, pl.ds(8 * a, CONV_CHUNK), :]))
    return out


def _conv_taps(copies, w_ref, tap):
    acc = jnp.zeros((CONV_CHUNK, CONV_W), F32)
    for j, rows in _tap_slices(copies, tap):
        acc = acc + rows * w_ref[j:j + 1, :]
    return acc


def _fold8(v):
    acc = v[0:8]
    for r in range(8, CONV_CHUNK, 8):
        acc = acc + v[r:r + 8]
    return acc


def _glu_into(uc_ref, vpad, S):
    vpad[pl.ds(0, CONV_HALO), :] = jnp.zeros((CONV_HALO, CONV_W), F32)
    vpad[pl.ds(S + CONV_HALO, CONV_HALO), :] = jnp.zeros((CONV_HALO, CONV_W), F32)

    def glu(i, carry):
        r0 = pl.multiple_of(i * GLU_CHUNK, GLU_CHUNK)
        a = uc_ref[0, pl.ds(r0, GLU_CHUNK), 0:CONV_W]
        gt = uc_ref[0, pl.ds(r0, GLU_CHUNK), CONV_W:2 * CONV_W]
        vpad[pl.ds(pl.multiple_of(r0 + CONV_HALO, CONV_HALO), GLU_CHUNK), :] = a * _sigmoid(gt)
        return carry
    lax.fori_loop(0, S // GLU_CHUNK, glu, 0)


def _layer_norm(z, vec_ref):
    xc = z - jnp.mean(z, axis=-1, keepdims=True)
    rstd = lax.rsqrt(jnp.mean(xc * xc, axis=-1, keepdims=True) + EPS)
    xn = xc * rstd
    return xn, rstd, xn * vec_ref[1:2, :] + vec_ref[2:3, :]


def _conv_fwd(uc, dw_w, vec):
    NB, S, _ = uc.shape

    def body(uc_ref, w_ref, vec_ref, o_ref, z_ref, vpad, copies):
        _glu_into(uc_ref, vpad, S)

        def conv(i, carry):
            r0 = pl.multiple_of(i * CONV_CHUNK, CONV_CHUNK)
            win = vpad[pl.ds(r0, CONV_WIN), :]
            z = _conv_taps(_shifted(win, CONV_HALO - (CONV_K - 1), copies), w_ref, lambda a, k: 8 * a + k) + vec_ref[0:1, :]
            z_ref[0, pl.ds(r0, CONV_CHUNK), :] = z
            _, _, yln = _layer_norm(z, vec_ref)
            o_ref[0, pl.ds(r0, CONV_CHUNK), :] = (yln * _sigmoid(yln)).astype(BF16)
            return carry
        lax.fori_loop(0, S // CONV_CHUNK, conv, 0, unroll=4)

    seq = pl.BlockSpec((1, S, CONV_W), lambda b: (b, 0, 0))
    return pl.pallas_call(
        body, grid=(NB,), name="conv_fwd",
        in_specs=[pl.BlockSpec((1, S, 2 * CONV_W), lambda b: (b, 0, 0)), _const((CONV_K, CONV_W)), _const((8, CONV_W))],
        out_specs=[seq, seq],
        out_shape=[jax.ShapeDtypeStruct((NB, S, CONV_W), BF16), jax.ShapeDtypeStruct((NB, S, CONV_W), F32)],
        scratch_shapes=[pltpu.VMEM((S + 2 * CONV_HALO, CONV_W), F32), pltpu.VMEM((8, CONV_CHUNK + 24, CONV_W), F32)],
        compiler_params=_params("parallel"),
    )(uc, dw_w, vec)


def _conv_bwd(uc, z, dcact, dw_w, vec, ride=None):
    NB, S, _ = uc.shape
    n_chunks = S // CONV_CHUNK

    def body(uc_ref, z_ref, dc_ref, w_ref, vec_ref, duc_ref, dw_ref, dvec_ref, vpad, dzpad, dw8, dvec8, copies):
        @pl.when(pl.program_id(0) == 0)
        def _():
            dw8[...] = jnp.zeros_like(dw8)
            dvec8[...] = jnp.zeros_like(dvec8)
        _glu_into(uc_ref, vpad, S)
        dzpad[pl.ds(S, 2 * CONV_HALO), :] = jnp.zeros((2 * CONV_HALO, CONV_W), F32)

        def norm_bwd(i, carry):
            r0 = pl.multiple_of(i * CONV_CHUNK, CONV_CHUNK)
            xn, rstd, yln = _layer_norm(z_ref[0, pl.ds(r0, CONV_CHUNK), :], vec_ref)
            sg = _sigmoid(yln)
            dyln = dc_ref[0, pl.ds(r0, CONV_CHUNK), :] * (sg * (1.0 + yln * (1.0 - sg)))
            dxn = dyln * vec_ref[1:2, :]
            dz = rstd * (dxn - jnp.mean(dxn, axis=-1, keepdims=True) - xn * jnp.mean(dxn * xn, axis=-1, keepdims=True))
            dzpad[pl.ds(r0, CONV_CHUNK), :] = dz
            dvec8[0] += _fold8(dz)
            dvec8[1] += _fold8(dyln * xn)
            dvec8[2] += _fold8(dyln)
            return carry
        lax.fori_loop(0, n_chunks, norm_bwd, 0, unroll=4)

        def taps_bwd(i, carry):
            r0 = pl.multiple_of(i * CONV_CHUNK, CONV_CHUNK)
            dzwin = dzpad[pl.ds(r0, CONV_WIN), :]
            dv = _conv_taps(_shifted(dzwin, 0, copies), w_ref, lambda a, k: CONV_K - 1 - 8 * a - k)
            dz = dzwin[0:CONV_CHUNK]
            vwin = vpad[pl.ds(r0, CONV_WIN), :]
            for j, rows in _tap_slices(_shifted(vwin, CONV_HALO - (CONV_K - 1), copies), lambda a, k: 8 * a + k):
                dw8[j] += _fold8(dz * rows)
            a = uc_ref[0, pl.ds(r0, CONV_CHUNK), 0:CONV_W]
            sg = _sigmoid(uc_ref[0, pl.ds(r0, CONV_CHUNK), CONV_W:2 * CONV_W])
            duc_ref[0, pl.ds(r0, CONV_CHUNK), 0:CONV_W] = (dv * sg).astype(BF16)
            duc_ref[0, pl.ds(r0, CONV_CHUNK), CONV_W:2 * CONV_W] = (dv * a * sg * (1.0 - sg)).astype(BF16)
            return carry
        lax.fori_loop(0, n_chunks, taps_bwd, 0, unroll=2)

        @pl.when(pl.program_id(0) == NB - 1)
        def _():
            dw_ref[...] = jnp.zeros_like(dw_ref)
            dvec_ref[...] = jnp.zeros_like(dvec_ref)
            for j in range(CONV_K):
                dw_ref[j:j + 1, :] = jnp.sum(dw8[j], axis=0, keepdims=True)
            for j in range(3):
                dvec_ref[j:j + 1, :] = jnp.sum(dvec8[j], axis=0, keepdims=True)

    return _pallas(
        body, "conv_bwd", (NB,),
        [pl.BlockSpec((1, S, 2 * CONV_W), lambda b: (b, 0, 0)), pl.BlockSpec((1, S, CONV_W), lambda b: (b, 0, 0)),
         pl.BlockSpec((1, S, CONV_W), lambda b: (b, 0, 0)), _const((CONV_K, CONV_W)), _const((8, CONV_W))],
        [pl.BlockSpec((1, S, 2 * CONV_W), lambda b: (b, 0, 0)), pl.BlockSpec((32, CONV_W), lambda b: (0, 0)),
         pl.BlockSpec((8, CONV_W), lambda b: (0, 0))],
        [jax.ShapeDtypeStruct((NB, S, 2 * CONV_W), BF16), jax.ShapeDtypeStruct((32, CONV_W), F32),
         jax.ShapeDtypeStruct((8, CONV_W), F32)],
        (uc, z, dcact, dw_w, vec),
        scratch_shapes=[pltpu.VMEM((S + 2 * CONV_HALO, CONV_W), F32), pltpu.VMEM((S + 2 * CONV_HALO, CONV_W), F32),
                        pltpu.VMEM((CONV_K, 8, CONV_W), F32), pltpu.VMEM((3, 8, CONV_W), F32),
                        pltpu.VMEM((8, CONV_CHUNK + 24, CONV_W), F32)],
        sem=("arbitrary",), ride=ride)


def _rel_index_of_column(cols):
    offset = jnp.where(cols < KWIN, cols, cols - DS_LANES)
    return jnp.clip(KPAD - offset, -(CHUNK - 1), MAX_REL) + (CHUNK - 1)


def _bias_table(rel_bias, ride=None):
    def body(rb_ref, o_ref, by_offset, first8):
        ridx = _rel_index_of_column(lax.broadcasted_iota(jnp.int32, (1, DS_LANES), 1))
        onehot = (ridx == lax.broadcasted_iota(jnp.int32, (N_REL, 1), 0)).astype(F32)
        by_offset[...] = jnp.dot(rb_ref[...], onehot, preferred_element_type=F32, precision=lax.Precision.HIGHEST)
        sub = lax.broadcasted_iota(jnp.int32, (8, 1), 0)
        kchunk = lax.broadcasted_iota(jnp.int32, (1, KWIN), 1) // CHUNK
        for head in range(ATT_HEADS):
            base = jnp.broadcast_to(by_offset[head:head + 1, :], (8, DS_LANES))
            rows = base
            for s in range(1, 8):
                rows = jnp.where(sub == s, pltpu.roll(base, s, 1), rows)
            first8[head] = rows

        def rows8(q8, carry):
            qchunk = (q8 * 8 + sub) // CHUNK
            band = (kchunk >= qchunk) & (kchunk <= qchunk + LEFT_CHUNKS)
            for head in range(ATT_HEADS):
                tile = pltpu.roll(first8[head], q8 * 8, 1)[:, 0:KWIN]
                o_ref[head, pl.ds(pl.multiple_of(q8 * 8, 8), 8), :] = jnp.where(band, tile, MASK_VALUE)
            return carry
        lax.fori_loop(0, QB // 8, rows8, 0)

    outs, rode = _pallas(
        body, "bias_table", (1,), [pl.BlockSpec((ATT_HEADS, N_REL), lambda i: (0, 0))],
        [pl.BlockSpec((ATT_HEADS, QB, KWIN), lambda i: (0, 0, 0))], [jax.ShapeDtypeStruct((ATT_HEADS, QB, KWIN), F32)], (rel_bias,),
        scratch_shapes=[pltpu.VMEM((ATT_HEADS, DS_LANES), F32), pltpu.VMEM((ATT_HEADS, 8, DS_LANES), F32)],
        sem=("arbitrary",), ride=ride)
    return outs[0], rode


def _load_keys(i, k_ref, v_ref, kpad, vpad, S):
    @pl.when(i == 0)
    def _():
        kpad[pl.ds(0, KPAD), :] = jnp.zeros((KPAD, ATT_W), BF16)
        vpad[pl.ds(0, KPAD), :] = jnp.zeros((KPAD, ATT_W), BF16)
        kpad[pl.ds(KPAD, S), :] = k_ref[0]
        vpad[pl.ds(KPAD, S), :] = v_ref[0]


def _att_scores(q2s, k2, tab_ref, head, in_head, in_seq):
    qm = jnp.where(in_head, q2s, jnp.zeros_like(q2s))
    return jnp.where(in_seq, _dot_nt(qm, k2) + tab_ref[head], MASK_VALUE)


def _scaled(q2):
    return q2 * jnp.asarray(ATT_HD ** -0.5, q2.dtype)


def _att_fwd(qkv, tab, ride=None):
    NB, S, _ = qkv.shape

    def body(q_ref, k_ref, v_ref, tab_ref, o_ref, lse_ref, kpad, vpad):
        i = pl.program_id(1)
        _load_keys(i, k_ref, v_ref, kpad, vpad, S)
        koff = pl.multiple_of(i * QB, QB)
        lane = lax.broadcasted_iota(jnp.int32, (1, 128), 1)
        in_seq = (lax.broadcasted_iota(jnp.int32, (1, KWIN), 1) + i * QB) >= KPAD
        lse = jnp.zeros((QB, 128), F32)
        for pair in range(ATT_HEADS // 2):
            cols = slice(pair * 128, (pair + 1) * 128)
            q2s = _scaled(q_ref[0, :, cols])
            k2 = kpad[pl.ds(koff, KWIN), cols]
            v2 = vpad[pl.ds(koff, KWIN), cols]
            o2 = jnp.zeros((QB, 128), F32)
            for hh in range(2):
                head = 2 * pair + hh
                in_head = (lane // ATT_HD) == hh
                s = _att_scores(q2s, k2, tab_ref, head, in_head, in_seq)
                m = jnp.max(s, axis=-1, keepdims=True)
                e = jnp.exp(s - m)
                l = jnp.sum(e, axis=-1, keepdims=True)
                o2 = jnp.where(in_head, _dot(e.astype(BF16), v2) * (1.0 / l), o2)
                lse = jnp.where(lane == head, m + jnp.log(l), lse)
            o_ref[0, :, cols] = o2.astype(BF16)
        lse_ref[0] = lse

    seq = lambda col: pl.BlockSpec((1, S, ATT_W), lambda b, i: (b, 0, col), pipeline_mode=pl.Buffered(1))
    outs, rode = _pallas(
        body, "att_fwd", (NB, S // QB),
        [pl.BlockSpec((1, QB, ATT_W), lambda b, i: (b, i, 0)), seq(1), seq(2), _const((ATT_HEADS, QB, KWIN))],
        [pl.BlockSpec((1, QB, ATT_W), lambda b, i: (b, i, 0)), pl.BlockSpec((1, QB, 128), lambda b, i: (b, i, 0))],
        [jax.ShapeDtypeStruct((NB, S, ATT_W), BF16), jax.ShapeDtypeStruct((NB, S, 128), F32)],
        (qkv, qkv, qkv, tab),
        scratch_shapes=[pltpu.VMEM((S + KPAD, ATT_W), BF16), pltpu.VMEM((S + KPAD, ATT_W), BF16)],
        sem=("arbitrary", "arbitrary"), ride=ride)
    return outs[0], outs[1], rode


def _att_bwd(qkv, o, lse, do, tab, ride=None):
    NB, S, _ = qkv.shape
    nq = S // QB

    def body(q_ref, k_ref, v_ref, o_ref, lse_ref, do_ref, tab_ref, dqkv_ref, ds_hbm, kpad, vpad, dkpad, dvpad, ds_acc, ds_sem):
        b, i = pl.program_id(0), pl.program_id(1)
        _load_keys(i, k_ref, v_ref, kpad, vpad, S)

        @pl.when(i == 0)
        def _():
            dkpad[...] = jnp.zeros_like(dkpad)
            dvpad[...] = jnp.zeros_like(dvpad)

        @pl.when((i == 0) & (b == 0))
        def _():
            ds_acc[...] = jnp.zeros_like(ds_acc)

        koff = pl.multiple_of(i * QB, QB)
        lane = lax.broadcasted_iota(jnp.int32, (1, 128), 1)
        in_seq = (lax.broadcasted_iota(jnp.int32, (1, KWIN), 1) + i * QB) >= KPAD
        for pair in range(ATT_HEADS // 2):
            cols = slice(pair * 128, (pair + 1) * 128)
            q2s = _scaled(q_ref[0, :, cols])
            do2 = do_ref[0, :, cols]
            k2 = kpad[pl.ds(koff, KWIN), cols]
            v2 = vpad[pl.ds(koff, KWIN), cols]
            do_o = do2.astype(F32) * o_ref[0, :, cols].astype(F32)
            dq2 = jnp.zeros((QB, 128), F32)
            dk2 = jnp.zeros((KWIN, 128), F32)
            dv2 = jnp.zeros((KWIN, 128), F32)
            for hh in range(2):
                head = 2 * pair + hh
                in_head = (lane // ATT_HD) == hh
                p = jnp.exp(_att_scores(q2s, k2, tab_ref, head, in_head, in_seq) - lse_ref[0, :, head:head + 1])
                row_term = jnp.sum(jnp.where(in_head, do_o, 0.0), axis=-1, keepdims=True)
                dom = jnp.where(in_head, do2, jnp.zeros_like(do2))
                ds = p * (_dot_nt(dom, v2) - row_term)
                ds_acc[head] += ds
                dsb = ds.astype(BF16)
                dq2 = jnp.where(in_head, _dot(dsb, k2), dq2)
                dk2 = jnp.where(in_head, _dot_tn(dsb, q2s), dk2)
                dv2 = jnp.where(in_head, _dot_tn(p.astype(BF16), do2), dv2)
            dqkv_ref[0, pl.ds(koff, QB), cols] = (dq2 * (ATT_HD ** -0.5)).astype(BF16)
            dkpad[pl.ds(koff, KWIN), cols] += dk2
            dvpad[pl.ds(koff, KWIN), cols] += dv2

        @pl.when(i == nq - 1)
        def _():
            dqkv_ref[0, :, ATT_W:2 * ATT_W] = dkpad[pl.ds(KPAD, S), :].astype(BF16)
            dqkv_ref[0, :, 2 * ATT_W:3 * ATT_W] = dvpad[pl.ds(KPAD, S), :].astype(BF16)

        @pl.when((i == nq - 1) & (b == NB - 1))
        def _():
            out = pltpu.make_async_copy(ds_acc, ds_hbm, ds_sem)
            out.start()
            out.wait()

    seq = lambda col: pl.BlockSpec((1, S, ATT_W), lambda b, i: (b, 0, col), pipeline_mode=pl.Buffered(1))
    rows = pl.BlockSpec((1, QB, ATT_W), lambda b, i: (b, i, 0))
    return _pallas(
        body, "att_bwd", (NB, nq),
        [rows, seq(1), seq(2), rows, pl.BlockSpec((1, QB, 128), lambda b, i: (b, i, 0)), rows, _const((ATT_HEADS, QB, KWIN))],
        [pl.BlockSpec((1, S, 3 * ATT_W), lambda b, i: (b, 0, 0)), ANY],
        [jax.ShapeDtypeStruct((NB, S, 3 * ATT_W), BF16), jax.ShapeDtypeStruct((ATT_HEADS, QB, KWIN), F32)],
        (qkv, qkv, qkv, o, lse, do, tab),
        scratch_shapes=[pltpu.VMEM((S + KPAD, ATT_W), BF16), pltpu.VMEM((S + KPAD, ATT_W), BF16),
                        pltpu.VMEM((S + KPAD, ATT_W), F32), pltpu.VMEM((S + KPAD, ATT_W), F32),
                        pltpu.VMEM((ATT_HEADS, QB, KWIN), F32), pltpu.SemaphoreType.DMA],
        sem=("arbitrary", "arbitrary"), ride=ride)


def _rel_bias_grad(ds):
    def body(ds_ref, o_ref):
        sub = lax.broadcasted_iota(jnp.int32, (8, 1), 0)
        ridx = _rel_index_of_column(lax.broadcasted_iota(jnp.int32, (DS_LANES, 1), 0))
        onehot = (ridx == lax.broadcasted_iota(jnp.int32, (1, N_REL), 1)).astype(F32)
        def rows8(q8, accs):
            shift = lax.rem(DS_LANES - q8 * 8, DS_LANES)
            out = []
            for head in range(ATT_HEADS):
                tile = ds_ref[head, pl.ds(pl.multiple_of(q8 * 8, 8), 8), :]
                tile = jnp.concatenate([tile, jnp.zeros((8, DS_LANES - KWIN), F32)], axis=1)
                out.append(accs[head] + pltpu.roll(tile, shift, 1))
            return tuple(out)
        accs = lax.fori_loop(0, QB // 8, rows8, tuple(jnp.zeros((8, DS_LANES), F32) for _ in range(ATT_HEADS)))
        for head in range(ATT_HEADS):
            acc = accs[head]
            diag = jnp.zeros((8, DS_LANES), F32)
            for s in range(8):
                shifted = acc if s == 0 else pltpu.roll(acc, DS_LANES - s, 1)
                diag = jnp.where(sub == s, shifted, diag)
            z = jnp.sum(diag, axis=0, keepdims=True)
            o_ref[head:head + 1, :] = jnp.dot(z, onehot, preferred_element_type=F32, precision=lax.Precision.HIGHEST)

    return pl.pallas_call(body, out_shape=jax.ShapeDtypeStruct((ATT_HEADS, N_REL), F32), name="rel_bias_grad",
                          compiler_params=_params())(ds)


def _memkv_fwd(mem, g, w_kv, tm):
    R = mem.shape[0]
    tm = min(tm, R)

    def body(m_ref, g_ref, w_ref, h_ref, kv_ref):
        xhat, _ = _rms_stats(m_ref[...])
        h = (xhat * g_ref[...]).astype(BF16)
        h_ref[...] = h
        kv_ref[...] = _dot(h, w_ref[...]).astype(BF16)

    row = pl.BlockSpec((tm, D), lambda t: (t, 0))
    return pl.pallas_call(
        body, grid=(R // tm,), name="memkv_fwd", in_specs=[row, _const((1, D)), _const((D, 2 * MEM_W))], out_specs=[row, row],
        out_shape=[jax.ShapeDtypeStruct((R, D), BF16), jax.ShapeDtypeStruct((R, 2 * MEM_W), BF16)],
        compiler_params=_params("parallel"),
    )(mem, g, w_kv)


def _memkv_bwd(mem, dkv, w_kv, tm):
    R = mem.shape[0]
    tm = min(tm, R)

    def body(m_ref, dkv_ref, w_ref, dg_ref):
        xhat, _ = _rms_stats(m_ref[...])
        dh = _dot_nt(dkv_ref[...].astype(BF16), w_ref[...])

        @pl.when(pl.program_id(0) == 0)
        def _():
            dg_ref[...] = jnp.zeros_like(dg_ref)
        dg_ref[...] += jnp.sum(dh * xhat, axis=0, keepdims=True)

    row = pl.BlockSpec((tm, D), lambda t: (t, 0))
    return pl.pallas_call(
        body, grid=(R // tm,), name="memkv_bwd", in_specs=[row, row, _const((D, 2 * MEM_W))],
        out_specs=pl.BlockSpec((1, D), lambda t: (0, 0)), out_shape=jax.ShapeDtypeStruct((1, D), F32),
        compiler_params=_params("arbitrary"),
    )(mem, dkv, w_kv)


def _mem_probs(qh, kh):
    s = _dot_nt(qh, kh) * (MEM_HD ** -0.5)
    e = jnp.exp(s - jnp.max(s, axis=-1, keepdims=True))
    return e * (1.0 / jnp.sum(e, axis=-1, keepdims=True))


def _mematt_fwd(mq, kv, tq):
    NB, S, _ = mq.shape
    M = kv.shape[1]

    def body(q_ref, kv_ref, o_ref):
        for h in range(MEM_HEADS):
            cols = slice(h * MEM_HD, (h + 1) * MEM_HD)
            p = _mem_probs(q_ref[0, :, cols], kv_ref[0, :, cols])
            o_ref[0, :, cols] = _dot(p.astype(BF16), kv_ref[0, :, MEM_W + h * MEM_HD:MEM_W + (h + 1) * MEM_HD]).astype(BF16)

    return pl.pallas_call(
        body, grid=(NB, S // tq), name="mematt_fwd",
        in_specs=[pl.BlockSpec((1, tq, MEM_W), lambda b, i: (b, i, 0)), pl.BlockSpec((1, M, 2 * MEM_W), lambda b, i: (b, 0, 0))],
        out_specs=pl.BlockSpec((1, tq, MEM_W), lambda b, i: (b, i, 0)),
        out_shape=jax.ShapeDtypeStruct((NB, S, MEM_W), BF16), compiler_params=_params("parallel", "parallel"),
    )(mq, kv)


def _mematt_bwd(mq, kv, do, tq):
    NB, S, _ = mq.shape
    M = kv.shape[1]

    def body(q_ref, kv_ref, do_ref, dq_ref, dkv_ref):
        @pl.when(pl.program_id(1) == 0)
        def _():
            dkv_ref[...] = jnp.zeros_like(dkv_ref)
        for h in range(MEM_HEADS):
            cols = slice(h * MEM_HD, (h + 1) * MEM_HD)
            vcols = slice(MEM_W + h * MEM_HD, MEM_W + (h + 1) * MEM_HD)
            qh, kh, vh, doh = q_ref[0, :, cols], kv_ref[0, :, cols], kv_ref[0, :, vcols], do_ref[0, :, cols]
            p = _mem_probs(qh, kh)
            dp = _dot_nt(doh, vh)
            ds = p * (dp - jnp.sum(p * dp, axis=-1, keepdims=True))
            dss = (ds * (MEM_HD ** -0.5)).astype(BF16)
            dq_ref[0, :, cols] = _dot(dss, kh).astype(BF16)
            dkv_ref[0, :, cols] += _dot_tn(dss, qh)
            dkv_ref[0, :, vcols] += _dot_tn(p.astype(BF16), doh)

    qspec = pl.BlockSpec((1, tq, MEM_W), lambda b, i: (b, i, 0))
    kvspec = pl.BlockSpec((1, M, 2 * MEM_W), lambda b, i: (b, 0, 0))
    return pl.pallas_call(
        body, grid=(NB, S // tq), name="mematt_bwd", in_specs=[qspec, kvspec, qspec], out_specs=[qspec, kvspec],
        out_shape=[jax.ShapeDtypeStruct((NB, S, MEM_W), BF16), jax.ShapeDtypeStruct((NB, M, 2 * MEM_W), F32)],
        compiler_params=_params("arbitrary", "arbitrary"),
    )(mq, kv, do)


def _branch(j, in_ref, w_ref, gl_ref, bg_ref):
    y = _dot(in_ref[...], w_ref[...])
    gate = _sigmoid(gl_ref[:, j * D:(j + 1) * D].astype(F32) + bg_ref[:, j * D:(j + 1) * D])
    return y, gate


def _combine_fwd(x, cact, oatt, omem, gl, bg, wpw, wo, wmo, wout, tm):
    T = x.shape[0]

    def body(x_ref, c_ref, a_ref, m_ref, gl_ref, bg_ref, wpw_ref, wo_ref, wmo_ref, wout_ref, xo_ref, y_ref):
        y = None
        for j, (in_ref, w_ref) in enumerate(((c_ref, wpw_ref), (a_ref, wo_ref), (m_ref, wmo_ref))):
            yj, gate = _branch(j, in_ref, w_ref, gl_ref, bg_ref)
            y = gate * yj if y is None else y + gate * yj
        y = y.astype(BF16)
        y_ref[...] = y
        xo_ref[...] = x_ref[...] + _dot(y, wout_ref[...])

    row = lambda w: pl.BlockSpec((tm, w), lambda t: (t, 0))
    wbr = _const((512, D))
    return pl.pallas_call(
        body, grid=(T // tm,), name="combine_fwd",
        in_specs=[row(D), row(512), row(512), row(512), row(3 * D), _const((1, 3 * D)), wbr, wbr, wbr, _const((D, D))],
        out_specs=[row(D), row(D)],
        out_shape=[jax.ShapeDtypeStruct((T, D), F32), jax.ShapeDtypeStruct((T, D), BF16)],
        compiler_params=_params("parallel"),
    )(x, cact, oatt, omem, gl, bg, wpw, wo, wmo, wout)


def _combine_bwd(dx, cact, oatt, omem, gl, bg, wpw, wo, wmo, wout, tm, ride=None):
    T = dx.shape[0]

    def body(dx_ref, c_ref, a_ref, m_ref, gl_ref, bg_ref, wpw_ref, wo_ref, wmo_ref, wout_ref,
             dgl_ref, dc_ref, da_ref, dm_ref, dyc_ref, dya_ref, dym_ref, dbg_ref):
        dy = _dot_nt(dx_ref[...].astype(BF16), wout_ref[...])

        @pl.when(pl.program_id(0) == 0)
        def _():
            dbg_ref[...] = jnp.zeros_like(dbg_ref)
        branches = ((c_ref, wpw_ref, dyc_ref, dc_ref), (a_ref, wo_ref, dya_ref, da_ref), (m_ref, wmo_ref, dym_ref, dm_ref))
        for j, (in_ref, w_ref, dyb_ref, din_ref) in enumerate(branches):
            yj, gate = _branch(j, in_ref, w_ref, gl_ref, bg_ref)
            dyg = dy * gate
            dlogit = dyg * yj * (1.0 - gate)
            dgl_ref[:, j * D:(j + 1) * D] = dlogit.astype(BF16)
            dbg_ref[:, j * D:(j + 1) * D] += jnp.sum(dlogit, axis=0, keepdims=True)
            dyb = dyg.astype(BF16)
            dyb_ref[...] = dyb
            din_ref[...] = _dot_nt(dyb, w_ref[...]).astype(din_ref.dtype)

    row = lambda w: pl.BlockSpec((tm, w), lambda t: (t, 0))
    wbr = _const((512, D))
    sds = jax.ShapeDtypeStruct
    return _pallas(
        body, "combine_bwd", (T // tm,),
        [row(D), row(512), row(512), row(512), row(3 * D), _const((1, 3 * D)), wbr, wbr, wbr, _const((D, D))],
        [row(3 * D), row(512), row(512), row(512), row(D), row(D), row(D), pl.BlockSpec((1, 3 * D), lambda t: (0, 0))],
        [sds((T, 3 * D), BF16), sds((T, 512), F32), sds((T, 512), BF16), sds((T, 512), BF16),
         sds((T, D), BF16), sds((T, D), BF16), sds((T, D), BF16), sds((1, 3 * D), F32)],
        (dx, cact, oatt, omem, gl, bg, wpw, wo, wmo, wout), sem=("arbitrary",), ride=ride)


def _peer(x, y, c, rel):
    rx, ry, rc = (rel >> 2) & 1, (rel >> 1) & 1, rel & 1
    return ((1 - x) if rx else x, (1 - y) if ry else y, (1 - c) if rc else c)


def _all_sum_small(parts):
    n = len(parts)

    def body(*refs):
        p_refs, o_refs, slots = refs[:n], refs[n:2 * n], refs[2 * n:3 * n]
        send_sems, recv_sems = refs[3 * n:]
        x, y, c = _my_coords()
        me = _dev_index(x, y, c)

        def copy(i, rel, arrival):
            peer = _peer(x, y, c, rel)
            return pltpu.make_async_remote_copy(
                src_ref=p_refs[i], dst_ref=slots[i].at[_dev_index(*peer) if arrival else me],
                send_sem=send_sems.at[i, rel - 1], recv_sem=recv_sems.at[i, rel - 1], device_id=peer, device_id_type=MESH)

        for i in range(n):
            slots[i][me] = p_refs[i][...]
        for rel in range(1, NDEV):
            for i in range(n):
                copy(i, rel, False).start()
        for rel in range(1, NDEV):
            for i in range(n):
                copy(i, rel, True).wait_recv()
        for rel in range(1, NDEV):
            for i in range(n):
                copy(i, rel, False).wait_send()
        for i in range(n):
            total = slots[i][0]
            for d in range(1, NDEV):
                total = total + slots[i][d]
            o_refs[i][...] = total

    vmem = pl.BlockSpec(memory_space=pltpu.VMEM)
    return pl.pallas_call(
        body, out_shape=[jax.ShapeDtypeStruct(p.shape, F32) for p in parts], name="all_sum_small",
        in_specs=[vmem] * n, out_specs=[vmem] * n,
        scratch_shapes=[pltpu.VMEM((NDEV,) + p.shape, F32) for p in parts]
        + [pltpu.SemaphoreType.DMA((n, NDEV - 1)), pltpu.SemaphoreType.DMA((n, NDEV - 1))],
        compiler_params=pltpu.CompilerParams(has_side_effects=True),
    )(*parts)


HBM = pl.BlockSpec(memory_space=pltpu.HBM)
SEM = pl.BlockSpec(memory_space=pltpu.SEMAPHORE)


def _own_block(g, kind, m, tag):
    def body(g_ref, land_ref, staged, sem):
        me = _dev_index(*_my_coords())
        for cp in (pltpu.make_async_copy(_window(g_ref, kind, m, me), staged, sem),
                   pltpu.make_async_copy(staged, land_ref.at[me], sem)):
            cp.start()
            cp.wait()

    block = (m, g.shape[1]) if kind == 'row' else (g.shape[0], m)
    return pl.pallas_call(body, in_specs=[ANY], out_specs=ANY, out_shape=jax.ShapeDtypeStruct((NDEV,) + block, g.dtype),
                          scratch_shapes=[pltpu.VMEM(block, g.dtype), pltpu.SemaphoreType.DMA], name="own_block_" + tag)(g)


def _scatter_start(g, land, kind, m, tag):
    def body(g_ref, land_ref, send_sems, recv_sems, g_thru, land_thru, token):
        x, y, c = _my_coords()
        me = _dev_index(x, y, c)
        for rel in range(1, NDEV):
            peer = _peer(x, y, c, rel)
            pltpu.make_async_remote_copy(src_ref=_window(g_ref, kind, m, _dev_index(*peer)), dst_ref=land_ref.at[me],
                                         send_sem=send_sems.at[rel - 1], recv_sem=recv_sems.at[rel - 1],
                                         device_id=peer, device_id_type=MESH).start()
        token[...] = jnp.zeros_like(token)

    return pl.pallas_call(
        body, name="scatter_start_" + tag,
        out_shape=(pltpu.SemaphoreType.DMA((NDEV - 1,)), pltpu.SemaphoreType.DMA((NDEV - 1,)), pltpu.HBM(g.shape, g.dtype),
                   pltpu.HBM(land.shape, land.dtype), jax.ShapeDtypeStruct((8, 128), F32)),
        in_specs=(HBM, HBM), out_specs=(SEM, SEM, HBM, HBM, pl.BlockSpec(memory_space=pltpu.VMEM)),
        input_output_aliases={0: 2, 1: 3},
        compiler_params=pltpu.CompilerParams(has_side_effects=pltpu.SideEffectType.DATAFLOW_SIDE_EFFECTING),
    )(pltpu.with_memory_space_constraint(g, pltpu.HBM), pltpu.with_memory_space_constraint(land, pltpu.HBM))


def _scatter_wait(send_sems, recv_sems, g_thru, land_thru, after, kind, m, tag):
    n_after = len(after)

    def body(*refs):
        g_ref, land_ref, send_sems, recv_sems = refs[:4]
        x, y, c = _my_coords()
        me = _dev_index(x, y, c)
        for rel in range(1, NDEV):
            peer = _peer(x, y, c, rel)
            dev = _dev_index(*peer)
            cp = pltpu.make_async_remote_copy(src_ref=_window(g_ref, kind, m, me), dst_ref=land_ref.at[dev],
                                              send_sem=send_sems.at[rel - 1], recv_sem=recv_sems.at[rel - 1],
                                              device_id=peer, device_id_type=MESH)
            cp.wait_send()
            cp.wait_recv()

    return pl.pallas_call(
        body, name="scatter_wait_" + tag,
        out_shape=(pltpu.HBM(g_thru.shape, g_thru.dtype), pltpu.HBM(land_thru.shape, land_thru.dtype)),
        in_specs=(HBM, HBM, SEM, SEM) + (ANY,) * n_after, out_specs=(HBM, HBM), input_output_aliases={0: 0, 1: 1},
        compiler_params=pltpu.CompilerParams(has_side_effects=pltpu.SideEffectType.DATAFLOW_SIDE_EFFECTING),
    )(g_thru, land_thru, send_sems, recv_sems, *after)[1]


def _adamw_math(w, g, m, v):
    m = ADAM_B1 * m + (1.0 - ADAM_B1) * g
    v = ADAM_B2 * v + (1.0 - ADAM_B2) * (g * g)
    m_hat = m / (1.0 - ADAM_B1 ** ADAM_STEP)
    v_hat = v / (1.0 - ADAM_B2 ** ADAM_STEP)
    delta = -ADAM_LR * (m_hat / (jnp.sqrt(v_hat) + ADAM_EPS) + ADAM_WD * w)
    return delta, m, v


def _sum_adamw(parts, w, m, v, name, after=None):
    R, C = w.shape
    n_parts = len(parts)
    cg = C // n_parts
    tr = max(t for t in range(8, 257, 8) if R % t == 0)
    deps = [] if after is None else [after]

    def body(*refs):
        p_refs = refs[:n_parts]
        w_ref, m_ref, v_ref = refs[n_parts:n_parts + 3]
        g_ref, d_ref, mo_ref, vo_ref = refs[n_parts + 3 + len(deps):]
        for k, p_ref in enumerate(p_refs):
            @pl.when(pl.program_id(0) == k)
            def _():
                g = p_ref[0].astype(F32)
                for d in range(1, NDEV):
                    g = g + p_ref[d].astype(F32)
                g_ref[...] = g
                d_ref[...], mo_ref[...], vo_ref[...] = _adamw_math(w_ref[...], g, m_ref[...], v_ref[...])

    part = pl.BlockSpec((NDEV, tr, cg), lambda k, t: (0, t, 0))
    blk = pl.BlockSpec((tr, cg), lambda k, t: (t, k))
    return pl.pallas_call(
        body, grid=(n_parts, R // tr), name=name, in_specs=[part] * n_parts + [blk, blk, blk] + [ANY] * len(deps),
        out_specs=[blk] * 4, out_shape=[jax.ShapeDtypeStruct((R, C), F32)] * 4, compiler_params=_params("parallel", "parallel"),
    )(*parts, w, m, v, *deps)


def _adamw_small(ws, gs, ms, vs):
    n = len(ws)

    def body(*refs):
        w_refs, g_refs, m_refs, v_refs = (refs[k * n:(k + 1) * n] for k in range(4))
        d_refs, mo_refs, vo_refs = (refs[(4 + k) * n:(5 + k) * n] for k in range(3))
        for i in range(n):
            d_refs[i][...], mo_refs[i][...], vo_refs[i][...] = _adamw_math(w_refs[i][...], g_refs[i][...], m_refs[i][...], v_refs[i][...])

    shapes = [jax.ShapeDtypeStruct(a.shape, F32) for a in ws]
    outs = pl.pallas_call(body, out_shape=shapes * 3, name="adamw_small", compiler_params=_params())(*ws, *gs, *ms, *vs)
    return outs[:n], outs[n:2 * n], outs[2 * n:]


def kernel(x, mem, ffn1_norm, ffn1_w_up, ffn1_w_down, mix_norm, mem_norm, w_in, b_gate, conv_dw_w, conv_dw_b, conv_ln_g, conv_ln_b, conv_w_pw, att_rel_bias, att_w_o, mem_w_kv, mem_w_o, w_out, ffn2_norm, ffn2_w_up, ffn2_w_down, final_norm, loss_target, m_ffn1_norm, m_ffn1_w_up, m_ffn1_w_down, m_mix_norm, m_mem_norm, m_w_in, m_b_gate, m_conv_dw_w, m_conv_dw_b, m_conv_ln_g, m_conv_ln_b, m_conv_w_pw, m_att_rel_bias, m_att_w_o, m_mem_w_kv, m_mem_w_o, m_w_out, m_ffn2_norm, m_ffn2_w_up, m_ffn2_w_down, m_final_norm, v_ffn1_norm, v_ffn1_w_up, v_ffn1_w_down, v_mix_norm, v_mem_norm, v_w_in, v_b_gate, v_conv_dw_w, v_conv_dw_b, v_conv_ln_g, v_conv_ln_b, v_conv_w_pw, v_att_rel_bias, v_att_w_o, v_mem_w_kv, v_mem_w_o, v_w_out, v_ffn2_norm, v_ffn2_w_up, v_ffn2_w_down, v_final_norm):
    given = dict(locals())
    w = {n: given[n] for n in WEIGHTS}
    mom = {n: given["m_" + n] for n in WEIGHTS}
    var = {n: given["v_" + n] for n in WEIGHTS}

    NB, S, _ = x.shape
    T = NB * S
    ML = mem.shape[1]
    x0 = x.reshape(T, D)
    target = loss_target.reshape(T, D)
    mem2 = mem.reshape(NB * ML, D)

    def block(t, n):
        return jnp.transpose(t[0]) if n in TRANSPOSED else t[0]

    sh = dict(zip(BIG_ORDER, _cast_shards([block(w[n], n) for n in BIG_ORDER])))
    dw_t = jnp.transpose(conv_dw_w[0])

    def gather(names, extra=(), extra_kinds=()):
        return _gather_ride([sh[n] for n in names] + list(extra), [BIG[n] for n in names] + list(extra_kinds))

    W = {}
    names0 = ['ffn1_w_up', 'ffn1_w_down']
    tab, got = _bias_table(att_rel_bias[0], ride=gather(names0, [dw_t], [('row', dw_t.shape[0])]))
    W.update(zip(names0, got[:2]))
    dw_full = jnp.transpose(got[2])
    conv_vec = jnp.concatenate([conv_dw_b, conv_ln_g, conv_ln_b, jnp.zeros((5, CONV_W), F32)], axis=0)
    fin_g = final_norm.reshape(1, D)

    names1 = ['w_in', 'conv_w_pw', 'att_w_o', 'mem_w_kv', 'mem_w_o', 'w_out']
    (x1, ab1), got = _ffn_fwd(x0, ffn1_norm, W['ffn1_w_up'], W['ffn1_w_down'], TILE_FFN_FWD, "ffn1_fwd", ride=gather(names1))
    W.update(zip(names1, got))
    (uc, qkv, mq, gl, hmix), _ = _mix_fwd(x1, mix_norm, W['w_in'], TILE_TOKENS)
    uc3 = uc.reshape(NB, S, 2 * CONV_W)
    qkv3 = qkv.reshape(NB, S, 3 * ATT_W)
    mq3 = mq.reshape(NB, S, MEM_W)
    cact, conv_z = _conv_fwd(uc3, dw_full, conv_vec)
    cact = cact.reshape(T, CONV_W)
    names2 = ['ffn2_w_up', 'ffn2_w_down']
    oatt3, att_lse, got = _att_fwd(qkv3, tab, ride=gather(names2))
    W.update(zip(names2, got))
    oatt = oatt3.reshape(T, ATT_W)
    memh, kv = _memkv_fwd(mem2, mem_norm, W['mem_w_kv'], TILE_TOKENS)
    kv3 = kv.reshape(NB, ML, 2 * MEM_W)
    omem = _mematt_fwd(mq3, kv3, TILE_TOKENS).reshape(T, MEM_W)
    branch_w = (W['conv_w_pw'], W['att_w_o'], W['mem_w_o'], W['w_out'])
    x2, ymix = _combine_fwd(x1, cact, oatt, omem, gl, b_gate, *branch_w, TILE_COMBINE)
    dx3, ab2, loss_part, dg_final = _ffn_fwd_loss(x2, ffn2_norm, W['ffn2_w_up'], W['ffn2_w_down'], fin_g, target, TILE_FFN_FWD,
                                                  "ffn2_fwd_loss")

    def scatter(grads, names):
        return _scatter_ride(grads, [BIG[n] for n in names])

    G, P = {}, {}
    dx2, dab2, act2, h2, dg_ffn2 = _ffn_bwd(x2, dx3, ab2, ffn2_norm, W['ffn2_w_up'], W['ffn2_w_down'], TILE_FFN, "ffn2_bwd")
    g_up, _ = _tn_matmul(dab2, h2, 512, "grad_ffn2_w_up_a", tt=TILE_GRAD_TOKENS_WIDE, x_part=(0, 2), out_rows=2 * FF)
    G['ffn2_w_up'], _ = _tn_matmul(dab2, h2, 512, "grad_ffn2_w_up_b", tt=TILE_GRAD_TOKENS_WIDE, x_part=(1, 2), out_rows=2 * FF,
                                   prev=g_up)
    G['ffn2_w_down'], _ = _tn_matmul(act2, dx3, 512, "grad_ffn2_w_down", scale=0.5, tt=TILE_GRAD_TOKENS_WIDE)
    (dgl, dcact, doatt, domem, dyc, dya, dym, dbg), got = _combine_bwd(
        dx2, cact, oatt, omem, gl, b_gate, *branch_w, TILE_COMBINE, ride=scatter([G['ffn2_w_up']], ['ffn2_w_up']))
    P['ffn2_w_up'] = got
    G['w_out'], _ = _tn_matmul(ymix, dx2, D, "grad_w_out", tt=TILE_GRAD_TOKENS_WIDE)
    G['conv_w_pw'], _ = _tn_matmul(cact, dyc, D, "grad_conv_w_pw")
    G['att_w_o'], _ = _tn_matmul(oatt, dya, D, "grad_att_w_o")
    G['mem_w_o'], _ = _tn_matmul(omem, dym, D, "grad_mem_w_o")
    dmq3, dkv3 = _mematt_bwd(mq3, kv3, domem.reshape(NB, S, MEM_W), TILE_TOKENS)
    dkv = dkv3.reshape(NB * ML, 2 * MEM_W)
    dg_mem = _memkv_bwd(mem2, dkv, W['mem_w_kv'], TILE_TOKENS)
    G['mem_w_kv'], _ = _tn_matmul(memh, dkv, 512, "grad_mem_w_kv")
    names = ['ffn2_w_down', 'w_out', 'conv_w_pw', 'att_w_o', 'mem_w_o']
    (dqkv3, dscore), got = _att_bwd(qkv3, oatt3, att_lse, doatt.reshape(NB, S, ATT_W), tab,
                                    ride=scatter([G[n] for n in names], names))
    P.update((n, [p]) for n, p in zip(names, got))
    d_rel = _rel_bias_grad(dscore)
    (duc3, d_dw, d_cvec), got = _conv_bwd(uc3, conv_z, dcact.reshape(NB, S, CONV_W), dw_full, conv_vec,
                                          ride=scatter([G['mem_w_kv']], ['mem_w_kv']))
    P['mem_w_kv'] = got
    duc, dqkv, dmq = duc3.reshape(T, 2 * CONV_W), dqkv3.reshape(T, 3 * ATT_W), dmq3.reshape(T, MEM_W)
    g_in, _ = _tn_matmul(hmix, duc, 1024, "grad_w_in_conv", out_cols=IN_COLS, col_off=0)
    g_in, _ = _tn_matmul(hmix, dqkv, 512, "grad_w_in_qkv", out_cols=IN_COLS, col_off=1024, prev=g_in)
    g_in, _ = _tn_matmul(hmix, dmq, 512, "grad_w_in_mq", out_cols=IN_COLS, col_off=2560, prev=g_in)
    G['w_in'], _ = _tn_matmul(hmix, dgl, 1024, "grad_w_in_gate", out_cols=IN_COLS, col_off=3072, prev=g_in)
    def start_scatter(g, name, tag):
        kind = BIG[name]
        return _scatter_start(g, _own_block(g, *kind, tag), *kind, tag) + (kind, tag)

    def wait_scatter(started, after):
        send_sems, recv_sems, g_thru, land_thru, _, kind, tag = started
        return _scatter_wait(send_sems, recv_sems, g_thru, land_thru, after, *kind, tag)

    ex_in = start_scatter(G['w_in'], 'w_in', "w_in")
    (dx1, dg_mix), _ = _mix_bwd(x1, dx2, duc, dqkv, dmq, dgl, mix_norm, W['w_in'], TILE_TOKENS, after=ex_in[4])
    dx0, dab1, act1, h1, dg_ffn1 = _ffn_bwd(x0, dx1, ab1, ffn1_norm, W['ffn1_w_up'], W['ffn1_w_down'], TILE_FFN, "ffn1_bwd")
    g_wd1, _ = _tn_matmul(act1, dx1, 512, "grad_ffn1_w_down", scale=0.5, tt=TILE_GRAD_TOKENS_WIDE)
    ex_wd = start_scatter(g_wd1, 'ffn1_w_down', "ffn1_w_down")
    g_wu1a, _ = _tn_matmul(dab1, h1, 512, "grad_ffn1_w_up_a", tt=TILE_GRAD_TOKENS_WIDEST, y_part=(0, 2), after=ex_wd[4])
    ex_a = start_scatter(g_wu1a, 'ffn1_w_up', "ffn1_w_up_a")
    g_wu1b, _ = _tn_matmul(dab1, h1, 512, "grad_ffn1_w_up_b", tt=TILE_GRAD_TOKENS_WIDEST, y_part=(1, 2), after=ex_a[4])
    ex_b = start_scatter(g_wu1b, 'ffn1_w_up', "ffn1_w_up_b")
    token = ex_b[4]

    small_names = ['loss', 'ffn1_norm', 'mix_norm', 'mem_norm', 'b_gate', 'conv_dw_w', 'conv_vec', 'att_rel_bias', 'ffn2_norm',
                   'final_norm']
    small = dict(zip(small_names, _all_sum_small(
        [loss_part + token[0:1], dg_ffn1, dg_mix, dg_mem, dbg, d_dw, d_cvec, d_rel, dg_ffn2, dg_final])))
    loss = small['loss'][0, 0]
    me = _dev_index(*_my_coords())
    for i, n in enumerate(['conv_dw_b', 'conv_ln_g', 'conv_ln_b']):
        small[n] = small['conv_vec'][i:i + 1]
    small['conv_dw_w'] = lax.dynamic_slice(small['conv_dw_w'], (0, me * conv_dw_w.shape[2]), (CONV_K, conv_dw_w.shape[2]))
    little = [n for n in WEIGHTS if n not in BIG]
    as2d = lambda t, n: t.reshape(small[n].shape)
    d_s, m_s, v_s = _adamw_small([as2d(w[n], n) for n in little], [small[n] for n in little],
                                 [as2d(mom[n], n) for n in little], [as2d(var[n], n) for n in little])
    grad, delta, new_m, new_v = {}, {}, {}, {}
    for i, n in enumerate(little):
        grad[n], delta[n], new_m[n], new_v[n] = (t.reshape(w[n].shape) for t in (small[n], d_s[i], m_s[i], v_s[i]))
    done = [d_s[0]]
    waited = {'w_in': [ex_in], 'ffn1_w_down': [ex_wd], 'ffn1_w_up': [ex_a, ex_b]}
    order = [n for n in BIG_ORDER if n not in waited] + list(waited)
    for n in order:
        if n in waited:
            P[n] = [wait_scatter(ex, done) for ex in waited[n]]
        outs = _sum_adamw(P[n], block(w[n], n), block(mom[n], n), block(var[n], n), "adamw_" + n,
                          after=None if n in waited else token)
        done.append(outs[0])
        grad[n], delta[n], new_m[n], new_v[n] = ((jnp.transpose(t) if n in TRANSPOSED else t)[None] for t in outs)

    return (loss, dx0.reshape(NB, S, D), *[grad[n] for n in WEIGHTS], *[delta[n] for n in WEIGHTS],
            *[new_m[n] for n in WEIGHTS], *[new_v[n] for n in WEIGHTS])
```

```python
import functools

import jax
import jax.numpy as jnp
from jax import lax
from jax.experimental import pallas as pl
from jax.experimental.pallas import tpu as pltpu

F32 = jnp.float32
BF16 = jnp.bfloat16

EPS = 1e-6
MASK_VALUE = -1e30
D = 1024
NDEV = 8
FF = 2816
FF_SHARD = 704
FF_HALF_ROWS = 352
FF_BLOCK_EDGES = ()
IN_COLS = 6144
CONV_W = 512
CONV_K = 31
CONV_HALO = 32
CONV_CHUNK = 32
CONV_WIN = CONV_CHUNK + 40
GLU_CHUNK = 128
ATT_W = 512
ATT_HEADS = 8
ATT_HD = 64
CHUNK = 64
LEFT_CHUNKS = 8
MAX_REL = 128
N_REL = 192
QB = 256
KWIN = QB + LEFT_CHUNKS * CHUNK
KPAD = LEFT_CHUNKS * CHUNK
DS_LANES = 1024
MEM_W = 512
MEM_HEADS = 4
MEM_HD = 128
ADAM_LR = 0.001
ADAM_B1 = 0.9
ADAM_B2 = 0.999
ADAM_EPS = 1e-08
ADAM_WD = 0.01
ADAM_STEP = 10
VMEM_LIMIT = 60 * 1024 * 1024
TILE_FFN = 256
TILE_FFN_FWD = 512
TILE_COMBINE = 256
TILE_TOKENS = 512
TILE_GRAD_TOKENS = 2048
TILE_GRAD_TOKENS_WIDE = 1024
TILE_GRAD_TOKENS_WIDEST = 512

MESH = pl.DeviceIdType.MESH
ANY = pl.BlockSpec(memory_space=pl.ANY)

WEIGHTS = ['ffn1_norm', 'ffn1_w_up', 'ffn1_w_down', 'mix_norm', 'mem_norm', 'w_in', 'b_gate', 'conv_dw_w', 'conv_dw_b',
           'conv_ln_g', 'conv_ln_b', 'conv_w_pw', 'att_rel_bias', 'att_w_o', 'mem_w_kv', 'mem_w_o', 'w_out', 'ffn2_norm',
           'ffn2_w_up', 'ffn2_w_down', 'final_norm']
BIG = {
    'ffn1_w_up': ('row', FF_SHARD), 'ffn1_w_down': ('row', FF_HALF_ROWS), 'w_in': ('col', 768),
    'conv_w_pw': ('col', 128), 'att_w_o': ('col', 128), 'mem_w_kv': ('row', 128), 'mem_w_o': ('col', 128),
    'w_out': ('row', 128), 'ffn2_w_up': ('row', FF_SHARD), 'ffn2_w_down': ('row', FF_HALF_ROWS),
}
BIG_ORDER = ['ffn1_w_up', 'ffn1_w_down', 'w_in', 'conv_w_pw', 'att_w_o', 'mem_w_kv', 'mem_w_o', 'w_out', 'ffn2_w_up', 'ffn2_w_down']
TRANSPOSED = ('ffn1_w_up', 'ffn2_w_up')


def _dot(a, b):
    return jnp.dot(a, b, preferred_element_type=F32)


def _dot_nt(a, b):
    return lax.dot_general(a, b, (((1,), (1,)), ((), ())), preferred_element_type=F32)


def _dot_tn(a, b):
    return lax.dot_general(a, b, (((0,), (0,)), ((), ())), preferred_element_type=F32)


def _sigmoid(v):
    return jax.nn.sigmoid(v)


def _const(shape):
    return pl.BlockSpec(shape, lambda *_: (0,) * len(shape), pipeline_mode=pl.Buffered(1))


def _params(*sem):
    return pltpu.CompilerParams(dimension_semantics=sem if sem else None, vmem_limit_bytes=VMEM_LIMIT)


def _my_coords():
    return lax.axis_index("x"), lax.axis_index("y"), lax.axis_index("c")


def _dev_index(px, py, pc):
    return 4 * px + 2 * py + pc


def _window(ref, kind, n, p):
    if kind == 'row':
        return ref.at[pl.ds(pl.multiple_of(p * n, n), n), :]
    return ref.at[:, pl.ds(pl.multiple_of(p * n, 128), n)]


def _full_shape(kind, n, shard_shape):
    if kind == 'row':
        return (NDEV * n, shard_shape[1])
    return (shard_shape[0], NDEV * n)


def _cast_shards(shards):
    n = len(shards)

    def body(*refs):
        for i in range(n):
            refs[n + i][...] = refs[i][...].astype(BF16)

    out_shape = [jax.ShapeDtypeStruct(s.shape, BF16) for s in shards]
    return pl.pallas_call(body, out_shape=out_shape, name="cast_shards", compiler_params=_params())(*shards)


class _Ride:
    def __init__(self, inputs, out_shape, scratch, start, finish, mids=()):
        self.inputs, self.out_shape, self.scratch = list(inputs), list(out_shape), list(scratch)
        self.start, self.finish, self.mids = start, finish, tuple(mids)


def _pallas(body, name, grid, in_specs, out_specs, out_shape, args, scratch_shapes=(), sem=None, aliases=None, ride=None,
            after=None):
    if ride is None:
        n_in, n_dep = len(args), 0 if after is None else 1

        def kernel_body(*refs):
            body(*refs[:n_in], *refs[n_in + n_dep:])

        outs = pl.pallas_call(kernel_body if n_dep else body, grid=grid, name=name, in_specs=list(in_specs) + [ANY] * n_dep,
                              out_specs=out_specs, out_shape=out_shape, scratch_shapes=list(scratch_shapes),
                              input_output_aliases=aliases or {}, compiler_params=_params(*sem),
                              )(*args, *([after] if n_dep else []))
        return list(outs), []
    n_in, n_out, n_scr = len(args), len(out_shape), len(scratch_shapes)
    r_in, r_out = len(ride.inputs), len(ride.out_shape)

    def wrapped(*refs):
        k_in, rin = refs[:n_in], refs[n_in:n_in + r_in]
        o0 = n_in + r_in
        k_out, rout = refs[o0:o0 + n_out], refs[o0 + n_out:o0 + n_out + r_out]
        s0 = o0 + n_out + r_out
        k_scr, rscr = refs[s0:s0 + n_scr], refs[s0 + n_scr:]
        ids = [pl.program_id(k) for k in range(len(grid))]
        first = functools.reduce(jnp.logical_and, [i == 0 for i in ids])
        last = functools.reduce(jnp.logical_and, [i == g - 1 for i, g in zip(ids, grid)])
        pl.when(first)(lambda: ride.start(rin, rout, rscr))
        single_step = all(g == 1 for g in grid)
        for quarter, mid in ride.mids:
            if not single_step:
                at_mid = functools.reduce(jnp.logical_and, [ids[0] == (quarter * grid[0]) // 4] + [i == 0 for i in ids[1:]])
                pl.when(at_mid)(functools.partial(mid, rin, rout, rscr))
        body(*k_in, *k_out, *k_scr)
        for _, mid in ride.mids:
            if single_step:
                mid(rin, rout, rscr)
        pl.when(last)(lambda: ride.finish(rin, rout, rscr))

    outs = pl.pallas_call(
        wrapped, grid=grid, name=name, in_specs=list(in_specs) + [ANY] * r_in, out_specs=list(out_specs) + [ANY] * r_out,
        out_shape=list(out_shape) + ride.out_shape, scratch_shapes=list(scratch_shapes) + ride.scratch,
        input_output_aliases=aliases or {}, compiler_params=_params(*(["arbitrary"] * len(grid))),
    )(*args, *ride.inputs)
    return list(outs[:n_out]), list(outs[n_out:])


def _gather_ride(shards, kinds):
    n = len(shards)

    def plan(rin, out, sems):
        send_sems, recv_sems, local_sems = sems[:3]
        x, y, c = _my_coords()
        me, sibling = (x, y, c), (x, y, 1 - c)
        xn, yn, diag = (1 - x, y), (x, 1 - y), (1 - x, 1 - y)

        def win(i, dev):
            return _window(out[i], kinds[i][0], kinds[i][1], _dev_index(*dev))

        def copy(i, k, block, to, from_shard=False):
            return pltpu.make_async_remote_copy(
                src_ref=rin[i] if from_shard else win(i, block), dst_ref=win(i, block),
                send_sem=send_sems.at[i, k], recv_sem=recv_sems.at[i, k], device_id=to, device_id_type=MESH)

        def each(fn):
            return [fn(i) for i in range(n)]

        return dict(
            local=lambda: each(lambda i: pltpu.make_async_copy(rin[i], win(i, me), local_sems.at[i])),
            own=lambda: [cp for i in range(n) for cp in (copy(i, 0, me, sibling, True), copy(i, 1, me, (*xn, c), True),
                                                         copy(i, 2, me, (*yn, c), True))],
            from_x=lambda: each(lambda i: copy(i, 1, (*xn, c), me)),
            from_y=lambda: each(lambda i: copy(i, 2, (*yn, c), me)),
            x_block_on_to_y=lambda: each(lambda i: copy(i, 3, (*xn, c), (*yn, c))),
            y_block_on_to_x=lambda: each(lambda i: copy(i, 3, (*yn, c), (*xn, c))),
            from_diag=lambda: each(lambda i: copy(i, 3, (*diag, c), me)),
            to_sibling=lambda: [copy(i, 4 + j, (*chip, c), sibling) for j, chip in enumerate((xn, yn, diag)) for i in range(n)],
            from_sibling=lambda: [cp for i in range(n) for cp in
                                  [copy(i, 0, sibling, me)] + [copy(i, 4 + j, (*chip, 1 - c), me) for j, chip in enumerate((xn, yn, diag))]],
            north=c == 1)

    def start(rin, out, sems):
        p = plan(rin, out, sems)
        for cp in p['local']() + p['own']():
            cp.start()

    def pass_diagonal(rin, out, sems):
        p = plan(rin, out, sems)

        @pl.when(p['north'])
        def _():
            for got, fwd in zip(p['from_x'](), p['x_block_on_to_y']()):
                got.wait_recv()
                fwd.start()

        @pl.when(jnp.logical_not(p['north']))
        def _():
            for got, fwd in zip(p['from_y'](), p['y_block_on_to_x']()):
                got.wait_recv()
                fwd.start()

    def pass_to_sibling(rin, out, sems):
        p = plan(rin, out, sems)

        @pl.when(p['north'])
        def _():
            for cp in p['from_y']():
                cp.wait_recv()

        @pl.when(jnp.logical_not(p['north']))
        def _():
            for cp in p['from_x']():
                cp.wait_recv()
        for cp in p['from_diag']():
            cp.wait_recv()
        for cp in p['to_sibling']():
            cp.start()

    def finish(rin, out, sems):
        p = plan(rin, out, sems)
        for cp in p['from_sibling']():
            cp.wait_recv()
        for cp in p['own']() + p['to_sibling']():
            cp.wait_send()

        @pl.when(p['north'])
        def _():
            for cp in p['x_block_on_to_y']():
                cp.wait_send()

        @pl.when(jnp.logical_not(p['north']))
        def _():
            for cp in p['y_block_on_to_x']():
                cp.wait_send()
        for cp in p['local']():
            cp.wait()

    out_shape = [jax.ShapeDtypeStruct(_full_shape(k, m, s.shape), s.dtype) for s, (k, m) in zip(shards, kinds)]
    scratch = [pltpu.SemaphoreType.DMA((n, 7)), pltpu.SemaphoreType.DMA((n, 7)), pltpu.SemaphoreType.DMA((n,))]
    return _Ride(shards, out_shape, scratch, start, finish, mids=((2, pass_diagonal), (3, pass_to_sibling)))


def _scatter_ride(grads, kinds):
    n = len(grads)

    def plan(g, out, sems):
        send_sems, recv_sems, local_sems = sems
        x, y, c = _my_coords()
        me = _dev_index(x, y, c)

        def local():
            return [pltpu.make_async_copy(_window(g[i], kinds[i][0], kinds[i][1], me), out[i].at[me], local_sems.at[i])
                    for i in range(n)]

        def remote(arrival):
            cps = []
            for rel in range(1, NDEV):
                peer = _peer(x, y, c, rel)
                dev = _dev_index(*peer)
                for i in range(n):
                    kind, m = kinds[i]
                    cps.append(pltpu.make_async_remote_copy(
                        src_ref=_window(g[i], kind, m, me if arrival else dev), dst_ref=out[i].at[dev if arrival else me],
                        send_sem=send_sems.at[i, rel - 1], recv_sem=recv_sems.at[i, rel - 1], device_id=peer, device_id_type=MESH))
            return cps

        return local, remote

    def start(g, out, sems):
        local, remote = plan(g, out, sems)
        for cp in local() + remote(False):
            cp.start()

    def finish(g, out, sems):
        local, remote = plan(g, out, sems)
        for cp in remote(True):
            cp.wait_recv()
        for cp in remote(False):
            cp.wait_send()
        for cp in local():
            cp.wait()

    def block_shape(gr, kind, m):
        return (m, gr.shape[1]) if kind == 'row' else (gr.shape[0], m)

    out_shape = [jax.ShapeDtypeStruct((NDEV,) + block_shape(gr, k, m), gr.dtype) for gr, (k, m) in zip(grads, kinds)]
    scratch = [pltpu.SemaphoreType.DMA((n, NDEV - 1)), pltpu.SemaphoreType.DMA((n, NDEV - 1)), pltpu.SemaphoreType.DMA((n,))]
    return _Ride(grads, out_shape, scratch, start, finish)


def _rms_stats(xf):
    r = lax.rsqrt(jnp.mean(xf * xf, axis=-1, keepdims=True) + EPS)
    return xf * r, r


def _rms_bwd(dh, g, xhat, r):
    dxhat = dh * g
    return r * (dxhat - xhat * jnp.mean(dxhat * xhat, axis=-1, keepdims=True))


def _ffn_blocks():
    edges = (0,) + FF_BLOCK_EDGES + (FF,)
    return [(slice(lo, hi), slice(FF + lo, FF + hi)) for lo, hi in zip(edges[:-1], edges[1:])]


def _swiglu_tile(x_ref, g_ref, wut_ref, wd_ref, ab_ref):
    xf = x_ref[...]
    xhat, _ = _rms_stats(xf)
    h = (xhat * g_ref[...]).astype(BF16)
    acc = jnp.zeros(xf.shape, F32)
    for ra, rb in _ffn_blocks():
        a = _dot_nt(h, wut_ref[ra, :])
        b = _dot_nt(h, wut_ref[rb, :])
        ab_ref[:, ra] = a.astype(BF16)
        ab_ref[:, rb] = b.astype(BF16)
        act = (a * _sigmoid(a) * b).astype(BF16)
        acc = acc + _dot(act, wd_ref[ra, :])
    return xf + 0.5 * acc


def _ffn_up(x, g, wut, tm, name, ride=None):
    T = x.shape[0]

    def body(x_ref, g_ref, wut_ref, ab_ref, act_ref):
        xhat, _ = _rms_stats(x_ref[...])
        h = (xhat * g_ref[...]).astype(BF16)
        for ra, rb in _ffn_blocks():
            a = _dot_nt(h, wut_ref[ra, :])
            b = _dot_nt(h, wut_ref[rb, :])
            ab_ref[:, ra] = a.astype(BF16)
            ab_ref[:, rb] = b.astype(BF16)
            act_ref[:, ra] = (a * _sigmoid(a) * b).astype(BF16)

    return _pallas(
        body, name, (T // tm,), [pl.BlockSpec((tm, D), lambda t: (t, 0)), _const((1, D)), _const((2 * FF, D))],
        [pl.BlockSpec((tm, 2 * FF), lambda t: (t, 0)), pl.BlockSpec((tm, FF), lambda t: (t, 0))],
        [jax.ShapeDtypeStruct((T, 2 * FF), BF16), jax.ShapeDtypeStruct((T, FF), BF16)],
        (x, g, wut), sem=("arbitrary",), ride=ride)


def _ffn_down(x, act, wd, tm, name):
    T = x.shape[0]

    def body(x_ref, act_ref, wd_ref, xo_ref):
        xo_ref[...] = x_ref[...] + 0.5 * _dot(act_ref[...], wd_ref[...])

    row = pl.BlockSpec((tm, D), lambda t: (t, 0))
    return pl.pallas_call(
        body, grid=(T // tm,), name=name, in_specs=[row, pl.BlockSpec((tm, FF), lambda t: (t, 0)), _const((FF, D))],
        out_specs=row, out_shape=jax.ShapeDtypeStruct((T, D), F32), compiler_params=_params("parallel"))(x, act, wd)


def _ffn_fwd_loss(x, g, wut, wd, g_final, target, tm, name):
    T = x.shape[0]

    def body(x_ref, g_ref, wut_ref, wd_ref, gf_ref, t_ref, dx_ref, ab_ref, loss_ref, dgf_ref):
        xhat, r = _rms_stats(_swiglu_tile(x_ref, g_ref, wut_ref, wd_ref, ab_ref))
        gain = gf_ref[...]
        diff = xhat * gain - t_ref[...]
        dout = diff * (1.0 / D)

        @pl.when(pl.program_id(0) == 0)
        def _():
            loss_ref[...] = jnp.zeros_like(loss_ref)
            dgf_ref[...] = jnp.zeros_like(dgf_ref)
        sq = jnp.sum(jnp.sum(diff * diff, axis=0, keepdims=True), axis=1, keepdims=True)
        loss_ref[...] += jnp.broadcast_to(sq * (0.5 / D), (1, 128))
        dgf_ref[...] += jnp.sum(dout * xhat, axis=0, keepdims=True)
        dx_ref[...] = _rms_bwd(dout, gain, xhat, r)

    row = pl.BlockSpec((tm, D), lambda t: (t, 0))
    return pl.pallas_call(
        body, grid=(T // tm,), name=name,
        in_specs=[row, _const((1, D)), _const((2 * FF, D)), _const((FF, D)), _const((1, D)), row],
        out_specs=[row, pl.BlockSpec((tm, 2 * FF), lambda t: (t, 0)), pl.BlockSpec((1, 128), lambda t: (0, 0)),
                   pl.BlockSpec((1, D), lambda t: (0, 0))],
        out_shape=[jax.ShapeDtypeStruct((T, D), F32), jax.ShapeDtypeStruct((T, 2 * FF), BF16),
                   jax.ShapeDtypeStruct((1, 128), F32), jax.ShapeDtypeStruct((1, D), F32)],
        compiler_params=_params("arbitrary"),
    )(x, g, wut, wd, g_final, target)


def _ffn_bwd(x, dy, ab, g, wut, wd, tm, name):
    T = x.shape[0]

    def body(x_ref, dy_ref, ab_ref, g_ref, wut_ref, wd_ref, dx_ref, dab_ref, act_ref, h_ref, dg_ref):
        xf = x_ref[...]
        xhat, r = _rms_stats(xf)
        gain = g_ref[...]
        h_ref[...] = (xhat * gain).astype(BF16)
        dy = dy_ref[...]
        dyh = (0.5 * dy).astype(BF16)
        dh = jnp.zeros((tm, D), F32)
        for ra, rb in _ffn_blocks():
            a = ab_ref[:, ra].astype(F32)
            b = ab_ref[:, rb].astype(F32)
            dact = _dot_nt(dyh, wd_ref[ra, :])
            sg = _sigmoid(a)
            sl = a * sg
            act_ref[:, ra] = (sl * b).astype(BF16)
            da = (dact * b * (sg * (1.0 + a * (1.0 - sg)))).astype(BF16)
            db = (dact * sl).astype(BF16)
            dab_ref[:, ra] = da
            dab_ref[:, rb] = db
            dh = dh + _dot(da, wut_ref[ra, :]) + _dot(db, wut_ref[rb, :])
        dx_ref[...] = dy + _rms_bwd(dh, gain, xhat, r)

        @pl.when(pl.program_id(0) == 0)
        def _():
            dg_ref[...] = jnp.zeros_like(dg_ref)
        dg_ref[...] += jnp.sum(dh * xhat, axis=0, keepdims=True)

    return pl.pallas_call(
        body, grid=(T // tm,), name=name,
        in_specs=[pl.BlockSpec((tm, D), lambda t: (t, 0)), pl.BlockSpec((tm, D), lambda t: (t, 0)),
                  pl.BlockSpec((tm, 2 * FF), lambda t: (t, 0)), _const((1, D)), _const((2 * FF, D)), _const((FF, D))],
        out_specs=[pl.BlockSpec((tm, D), lambda t: (t, 0)), pl.BlockSpec((tm, 2 * FF), lambda t: (t, 0)),
                   pl.BlockSpec((tm, FF), lambda t: (t, 0)), pl.BlockSpec((tm, D), lambda t: (t, 0)),
                   pl.BlockSpec((1, D), lambda t: (0, 0))],
        out_shape=[jax.ShapeDtypeStruct((T, D), F32), jax.ShapeDtypeStruct((T, 2 * FF), BF16),
                   jax.ShapeDtypeStruct((T, FF), BF16), jax.ShapeDtypeStruct((T, D), BF16), jax.ShapeDtypeStruct((1, D), F32)],
        compiler_params=_params("arbitrary"),
    )(x, dy, ab, g, wut, wd)


def _tn_matmul(xm, ym, tn, name, scale=None, out_cols=None, col_off=0, prev=None, tt=TILE_GRAD_TOKENS, x_part=(0, 1),
               out_rows=None, y_part=(0, 1), ride=None, after=None):
    T = xm.shape[0]
    xi, xn = x_part
    yi, yn = y_part
    K = xm.shape[1] // xn
    N = ym.shape[1] // yn
    out_cols = N if out_cols is None else out_cols
    row_blk = xi if out_rows is not None else 0
    out_rows = K if out_rows is None else out_rows
    tt = min(tt, T)
    nt = T // tt
    off = col_off // tn

    def body(*refs):
        x_ref, y_ref = refs[0], refs[1]
        o_ref, acc = refs[-2], refs[-1]

        @pl.when(pl.program_id(1) == 0)
        def _():
            acc[...] = jnp.zeros_like(acc)
        acc[...] += _dot_tn(x_ref[...].astype(BF16), y_ref[...].astype(BF16))

        @pl.when(pl.program_id(1) == nt - 1)
        def _():
            res = acc[...]
            o_ref[...] = (res if scale is None else res * scale).astype(BF16)

    ycol = yi * (N // tn)
    in_specs = [pl.BlockSpec((tt, K), lambda n, t: (t, xi)), pl.BlockSpec((tt, tn), lambda n, t: (t, n + ycol))]
    args = [xm, ym]
    aliases = {}
    if prev is not None:
        in_specs.append(ANY)
        args.append(prev)
        aliases = {2: 0}
    outs, rode = _pallas(
        body, name, (N // tn, nt), in_specs, [pl.BlockSpec((K, tn), lambda n, t: (row_blk, n + off))],
        [jax.ShapeDtypeStruct((out_rows, out_cols), BF16)], args, scratch_shapes=[pltpu.VMEM((K, tn), F32)],
        sem=("parallel", "arbitrary"), aliases=aliases, ride=ride, after=after)
    return outs[0], rode


def _mix_fwd(x, g, w_in, tm, ride=None):
    T = x.shape[0]

    def body(x_ref, g_ref, w_ref, uc_ref, qkv_ref, mq_ref, gl_ref, h_ref):
        xhat, _ = _rms_stats(x_ref[...])
        h = (xhat * g_ref[...]).astype(BF16)
        h_ref[...] = h
        uc_ref[...] = _dot(h, w_ref[:, 0:1024])
        qkv_ref[...] = _dot(h, w_ref[:, 1024:2560]).astype(BF16)
        mq_ref[...] = _dot(h, w_ref[:, 2560:3072]).astype(BF16)
        for j in range(3):
            gl_ref[:, j * D:(j + 1) * D] = _dot(h, w_ref[:, 3072 + j * D:3072 + (j + 1) * D]).astype(BF16)

    row = lambda w: pl.BlockSpec((tm, w), lambda t: (t, 0))
    return _pallas(
        body, "mix_fwd", (T // tm,), [row(D), _const((1, D)), _const((D, IN_COLS))],
        [row(1024), row(1536), row(512), row(3072), row(D)],
        [jax.ShapeDtypeStruct((T, 1024), F32), jax.ShapeDtypeStruct((T, 1536), BF16), jax.ShapeDtypeStruct((T, 512), BF16),
         jax.ShapeDtypeStruct((T, 3072), BF16), jax.ShapeDtypeStruct((T, D), BF16)],
        (x, g, w_in), sem=("parallel",), ride=ride)


def _mix_bwd(x, dres, duc, dqkv, dmq, dgl, g, w_in, tm, ride=None, after=None):
    T = x.shape[0]

    def body(x_ref, dres_ref, duc_ref, dqkv_ref, dmq_ref, dgl_ref, g_ref, w_ref, dx_ref, dg_ref):
        xhat, r = _rms_stats(x_ref[...])
        dh = _dot_nt(duc_ref[...], w_ref[:, 0:1024])
        dh = dh + _dot_nt(dqkv_ref[...], w_ref[:, 1024:2560])
        dh = dh + _dot_nt(dmq_ref[...], w_ref[:, 2560:3072])
        dh = dh + _dot_nt(dgl_ref[...], w_ref[:, 3072:6144])
        dx_ref[...] = dres_ref[...] + _rms_bwd(dh, g_ref[...], xhat, r)

        @pl.when(pl.program_id(0) == 0)
        def _():
            dg_ref[...] = jnp.zeros_like(dg_ref)
        dg_ref[...] += jnp.sum(dh * xhat, axis=0, keepdims=True)

    row = lambda w: pl.BlockSpec((tm, w), lambda t: (t, 0))
    return _pallas(
        body, "mix_bwd", (T // tm,),
        [row(D), row(D), row(1024), row(1536), row(512), row(3072), _const((1, D)), _const((D, IN_COLS))],
        [row(D), pl.BlockSpec((1, D), lambda t: (0, 0))],
        [jax.ShapeDtypeStruct((T, D), F32), jax.ShapeDtypeStruct((1, D), F32)],
        (x, dres, duc, dqkv, dmq, dgl, g, w_in), sem=("arbitrary",), ride=ride, after=after)


def _shifted(win, base, copies):
    for k in range(8):
        copies[k] = win[base + k:base + k + CONV_CHUNK + 24]
    return copies


def _tap_slices(copies, tap):
    out = []
    for k in range(8):
        for a in range(4):
            j = tap(a, k)
            if 0 <= j < CONV_K:
                out.append((j, copies[k, pl.ds(8 * a, CONV_CHUNK), :]))
    return out


def _conv_taps(copies, w_ref, tap):
    acc = jnp.zeros((CONV_CHUNK, CONV_W), F32)
    for j, rows in _tap_slices(copies, tap):
        acc = acc + rows * w_ref[j:j + 1, :]
    return acc


def _fold8(v):
    acc = v[0:8]
    for r in range(8, CONV_CHUNK, 8):
        acc = acc + v[r:r + 8]
    return acc


def _glu_into(uc_ref, vpad, S):
    vpad[pl.ds(0, CONV_HALO), :] = jnp.zeros((CONV_HALO, CONV_W), F32)
    vpad[pl.ds(S + CONV_HALO, CONV_HALO), :] = jnp.zeros((CONV_HALO, CONV_W), F32)

    def glu(i, carry):
        r0 = pl.multiple_of(i * GLU_CHUNK, GLU_CHUNK)
        a = uc_ref[0, pl.ds(r0, GLU_CHUNK), 0:CONV_W]
        gt = uc_ref[0, pl.ds(r0, GLU_CHUNK), CONV_W:2 * CONV_W]
        vpad[pl.ds(pl.multiple_of(r0 + CONV_HALO, CONV_HALO), GLU_CHUNK), :] = a * _sigmoid(gt)
        return carry
    lax.fori_loop(0, S // GLU_CHUNK, glu, 0)


def _layer_norm(z, vec_ref):
    xc = z - jnp.mean(z, axis=-1, keepdims=True)
    rstd = lax.rsqrt(jnp.mean(xc * xc, axis=-1, keepdims=True) + EPS)
    xn = xc * rstd
    return xn, rstd, xn * vec_ref[1:2, :] + vec_ref[2:3, :]


def _conv_fwd(uc, dw_w, vec):
    NB, S, _ = uc.shape

    def body(uc_ref, w_ref, vec_ref, o_ref, z_ref, vpad, copies):
        _glu_into(uc_ref, vpad, S)

        def conv(i, carry):
            r0 = pl.multiple_of(i * CONV_CHUNK, CONV_CHUNK)
            win = vpad[pl.ds(r0, CONV_WIN), :]
            z = _conv_taps(_shifted(win, CONV_HALO - (CONV_K - 1), copies), w_ref, lambda a, k: 8 * a + k) + vec_ref[0:1, :]
            z_ref[0, pl.ds(r0, CONV_CHUNK), :] = z
            _, _, yln = _layer_norm(z, vec_ref)
            o_ref[0, pl.ds(r0, CONV_CHUNK), :] = (yln * _sigmoid(yln)).astype(BF16)
            return carry
        lax.fori_loop(0, S // CONV_CHUNK, conv, 0, unroll=4)

    seq = pl.BlockSpec((1, S, CONV_W), lambda b: (b, 0, 0))
    return pl.pallas_call(
        body, grid=(NB,), name="conv_fwd",
        in_specs=[pl.BlockSpec((1, S, 2 * CONV_W), lambda b: (b, 0, 0)), _const((CONV_K, CONV_W)), _const((8, CONV_W))],
        out_specs=[seq, seq],
        out_shape=[jax.ShapeDtypeStruct((NB, S, CONV_W), BF16), jax.ShapeDtypeStruct((NB, S, CONV_W), F32)],
        scratch_shapes=[pltpu.VMEM((S + 2 * CONV_HALO, CONV_W), F32), pltpu.VMEM((8, CONV_CHUNK + 24, CONV_W), F32)],
        compiler_params=_params("parallel"),
    )(uc, dw_w, vec)


def _conv_bwd(uc, z, dcact, dw_w, vec, ride=None):
    NB, S, _ = uc.shape
    n_chunks = S // CONV_CHUNK

    def body(uc_ref, z_ref, dc_ref, w_ref, vec_ref, duc_ref, dw_ref, dvec_ref, vpad, dzpad, dw8, dvec8, copies):
        @pl.when(pl.program_id(0) == 0)
        def _():
            dw8[...] = jnp.zeros_like(dw8)
            dvec8[...] = jnp.zeros_like(dvec8)
        _glu_into(uc_ref, vpad, S)
        dzpad[pl.ds(S, 2 * CONV_HALO), :] = jnp.zeros((2 * CONV_HALO, CONV_W), F32)

        def norm_bwd(i, carry):
            r0 = pl.multiple_of(i * CONV_CHUNK, CONV_CHUNK)
            xn, rstd, yln = _layer_norm(z_ref[0, pl.ds(r0, CONV_CHUNK), :], vec_ref)
            sg = _sigmoid(yln)
            dyln = dc_ref[0, pl.ds(r0, CONV_CHUNK), :] * (sg * (1.0 + yln * (1.0 - sg)))
            dxn = dyln * vec_ref[1:2, :]
            dz = rstd * (dxn - jnp.mean(dxn, axis=-1, keepdims=True) - xn * jnp.mean(dxn * xn, axis=-1, keepdims=True))
            dzpad[pl.ds(r0, CONV_CHUNK), :] = dz
            dvec8[0] += _fold8(dz)
            dvec8[1] += _fold8(dyln * xn)
            dvec8[2] += _fold8(dyln)
            return carry
        lax.fori_loop(0, n_chunks, norm_bwd, 0, unroll=4)

        def taps_bwd(i, carry):
            r0 = pl.multiple_of(i * CONV_CHUNK, CONV_CHUNK)
            dzwin = dzpad[pl.ds(r0, CONV_WIN), :]
            dv = _conv_taps(_shifted(dzwin, 0, copies), w_ref, lambda a, k: CONV_K - 1 - 8 * a - k)
            dz = dzwin[0:CONV_CHUNK]
            vwin = vpad[pl.ds(r0, CONV_WIN), :]
            for j, rows in _tap_slices(_shifted(vwin, CONV_HALO - (CONV_K - 1), copies), lambda a, k: 8 * a + k):
                dw8[j] += _fold8(dz * rows)
            a = uc_ref[0, pl.ds(r0, CONV_CHUNK), 0:CONV_W]
            sg = _sigmoid(uc_ref[0, pl.ds(r0, CONV_CHUNK), CONV_W:2 * CONV_W])
            duc_ref[0, pl.ds(r0, CONV_CHUNK), 0:CONV_W] = (dv * sg).astype(BF16)
            duc_ref[0, pl.ds(r0, CONV_CHUNK), CONV_W:2 * CONV_W] = (dv * a * sg * (1.0 - sg)).astype(BF16)
            return carry
        lax.fori_loop(0, n_chunks, taps_bwd, 0, unroll=2)

        @pl.when(pl.program_id(0) == NB - 1)
        def _():
            dw_ref[...] = jnp.zeros_like(dw_ref)
            dvec_ref[...] = jnp.zeros_like(dvec_ref)
            for j in range(CONV_K):
                dw_ref[j:j + 1, :] = jnp.sum(dw8[j], axis=0, keepdims=True)
            for j in range(3):
                dvec_ref[j:j + 1, :] = jnp.sum(dvec8[j], axis=0, keepdims=True)

    return _pallas(
        body, "conv_bwd", (NB,),
        [pl.BlockSpec((1, S, 2 * CONV_W), lambda b: (b, 0, 0)), pl.BlockSpec((1, S, CONV_W), lambda b: (b, 0, 0)),
         pl.BlockSpec((1, S, CONV_W), lambda b: (b, 0, 0)), _const((CONV_K, CONV_W)), _const((8, CONV_W))],
        [pl.BlockSpec((1, S, 2 * CONV_W), lambda b: (b, 0, 0)), pl.BlockSpec((32, CONV_W), lambda b: (0, 0)),
         pl.BlockSpec((8, CONV_W), lambda b: (0, 0))],
        [jax.ShapeDtypeStruct((NB, S, 2 * CONV_W), BF16), jax.ShapeDtypeStruct((32, CONV_W), F32),
         jax.ShapeDtypeStruct((8, CONV_W), F32)],
        (uc, z, dcact, dw_w, vec),
        scratch_shapes=[pltpu.VMEM((S + 2 * CONV_HALO, CONV_W), F32), pltpu.VMEM((S + 2 * CONV_HALO, CONV_W), F32),
                        pltpu.VMEM((CONV_K, 8, CONV_W), F32), pltpu.VMEM((3, 8, CONV_W), F32),
                        pltpu.VMEM((8, CONV_CHUNK + 24, CONV_W), F32)],
        sem=("arbitrary",), ride=ride)


def _rel_index_of_column(cols):
    offset = jnp.where(cols < KWIN, cols, cols - DS_LANES)
    return jnp.clip(KPAD - offset, -(CHUNK - 1), MAX_REL) + (CHUNK - 1)


def _bias_table(rel_bias, ride=None):
    def body(rb_ref, o_ref, by_offset, first8):
        ridx = _rel_index_of_column(lax.broadcasted_iota(jnp.int32, (1, DS_LANES), 1))
        onehot = (ridx == lax.broadcasted_iota(jnp.int32, (N_REL, 1), 0)).astype(F32)
        by_offset[...] = jnp.dot(rb_ref[...], onehot, preferred_element_type=F32, precision=lax.Precision.HIGHEST)
        sub = lax.broadcasted_iota(jnp.int32, (8, 1), 0)
        kchunk = lax.broadcasted_iota(jnp.int32, (1, KWIN), 1) // CHUNK
        for head in range(ATT_HEADS):
            base = jnp.broadcast_to(by_offset[head:head + 1, :], (8, DS_LANES))
            rows = base
            for s in range(1, 8):
                rows = jnp.where(sub == s, pltpu.roll(base, s, 1), rows)
            first8[head] = rows

        def rows8(q8, carry):
            qchunk = (q8 * 8 + sub) // CHUNK
            band = (kchunk >= qchunk) & (kchunk <= qchunk + LEFT_CHUNKS)
            for head in range(ATT_HEADS):
                tile = pltpu.roll(first8[head], q8 * 8, 1)[:, 0:KWIN]
                o_ref[head, pl.ds(pl.multiple_of(q8 * 8, 8), 8), :] = jnp.where(band, tile, MASK_VALUE)
            return carry
        lax.fori_loop(0, QB // 8, rows8, 0)

    outs, rode = _pallas(
        body, "bias_table", (1,), [pl.BlockSpec((ATT_HEADS, N_REL), lambda i: (0, 0))],
        [pl.BlockSpec((ATT_HEADS, QB, KWIN), lambda i: (0, 0, 0))], [jax.ShapeDtypeStruct((ATT_HEADS, QB, KWIN), F32)], (rel_bias,),
        scratch_shapes=[pltpu.VMEM((ATT_HEADS, DS_LANES), F32), pltpu.VMEM((ATT_HEADS, 8, DS_LANES), F32)],
        sem=("arbitrary",), ride=ride)
    return outs[0], rode


def _load_keys(i, k_ref, v_ref, kpad, vpad, S):
    @pl.when(i == 0)
    def _():
        kpad[pl.ds(0, KPAD), :] = jnp.zeros((KPAD, ATT_W), BF16)
        vpad[pl.ds(0, KPAD), :] = jnp.zeros((KPAD, ATT_W), BF16)
        kpad[pl.ds(KPAD, S), :] = k_ref[0]
        vpad[pl.ds(KPAD, S), :] = v_ref[0]


def _att_scores(q2s, k2, tab_ref, head, in_head, in_seq):
    qm = jnp.where(in_head, q2s, jnp.zeros_like(q2s))
    return jnp.where(in_seq, _dot_nt(qm, k2) + tab_ref[head], MASK_VALUE)


def _scaled(q2):
    return q2 * jnp.asarray(ATT_HD ** -0.5, q2.dtype)


def _att_fwd(qkv, tab, ride=None):
    NB, S, _ = qkv.shape

    def body(q_ref, k_ref, v_ref, tab_ref, o_ref, lse_ref, kpad, vpad):
        i = pl.program_id(1)
        _load_keys(i, k_ref, v_ref, kpad, vpad, S)
        koff = pl.multiple_of(i * QB, QB)
        lane = lax.broadcasted_iota(jnp.int32, (1, 128), 1)
        in_seq = (lax.broadcasted_iota(jnp.int32, (1, KWIN), 1) + i * QB) >= KPAD
        lse = jnp.zeros((QB, 128), F32)
        for pair in range(ATT_HEADS // 2):
            cols = slice(pair * 128, (pair + 1) * 128)
            q2s = _scaled(q_ref[0, :, cols])
            k2 = kpad[pl.ds(koff, KWIN), cols]
            v2 = vpad[pl.ds(koff, KWIN), cols]
            o2 = jnp.zeros((QB, 128), F32)
            for hh in range(2):
                head = 2 * pair + hh
                in_head = (lane // ATT_HD) == hh
                s = _att_scores(q2s, k2, tab_ref, head, in_head, in_seq)
                m = jnp.max(s, axis=-1, keepdims=True)
                e = jnp.exp(s - m)
                l = jnp.sum(e, axis=-1, keepdims=True)
                o2 = jnp.where(in_head, _dot(e.astype(BF16), v2) * (1.0 / l), o2)
                lse = jnp.where(lane == head, m + jnp.log(l), lse)
            o_ref[0, :, cols] = o2.astype(BF16)
        lse_ref[0] = lse

    seq = lambda col: pl.BlockSpec((1, S, ATT_W), lambda b, i: (b, 0, col), pipeline_mode=pl.Buffered(1))
    outs, rode = _pallas(
        body, "att_fwd", (NB, S // QB),
        [pl.BlockSpec((1, QB, ATT_W), lambda b, i: (b, i, 0)), seq(1), seq(2), _const((ATT_HEADS, QB, KWIN))],
        [pl.BlockSpec((1, QB, ATT_W), lambda b, i: (b, i, 0)), pl.BlockSpec((1, QB, 128), lambda b, i: (b, i, 0))],
        [jax.ShapeDtypeStruct((NB, S, ATT_W), BF16), jax.ShapeDtypeStruct((NB, S, 128), F32)],
        (qkv, qkv, qkv, tab),
        scratch_shapes=[pltpu.VMEM((S + KPAD, ATT_W), BF16), pltpu.VMEM((S + KPAD, ATT_W), BF16)],
        sem=("arbitrary", "arbitrary"), ride=ride)
    return outs[0], outs[1], rode


def _att_bwd(qkv, o, lse, do, tab, ride=None):
    NB, S, _ = qkv.shape
    nq = S // QB

    def body(q_ref, k_ref, v_ref, o_ref, lse_ref, do_ref, tab_ref, dqkv_ref, ds_hbm, kpad, vpad, dkpad, dvpad, ds_acc, ds_sem):
        b, i = pl.program_id(0), pl.program_id(1)
        _load_keys(i, k_ref, v_ref, kpad, vpad, S)

        @pl.when(i == 0)
        def _():
            dkpad[...] = jnp.zeros_like(dkpad)
            dvpad[...] = jnp.zeros_like(dvpad)

        @pl.when((i == 0) & (b == 0))
        def _():
            ds_acc[...] = jnp.zeros_like(ds_acc)

        koff = pl.multiple_of(i * QB, QB)
        lane = lax.broadcasted_iota(jnp.int32, (1, 128), 1)
        in_seq = (lax.broadcasted_iota(jnp.int32, (1, KWIN), 1) + i * QB) >= KPAD
        for pair in range(ATT_HEADS // 2):
            cols = slice(pair * 128, (pair + 1) * 128)
            q2s = _scaled(q_ref[0, :, cols])
            do2 = do_ref[0, :, cols]
            k2 = kpad[pl.ds(koff, KWIN), cols]
            v2 = vpad[pl.ds(koff, KWIN), cols]
            do_o = do2.astype(F32) * o_ref[0, :, cols].astype(F32)
            dq2 = jnp.zeros((QB, 128), F32)
            dk2 = jnp.zeros((KWIN, 128), F32)
            dv2 = jnp.zeros((KWIN, 128), F32)
            for hh in range(2):
                head = 2 * pair + hh
                in_head = (lane // ATT_HD) == hh
                p = jnp.exp(_att_scores(q2s, k2, tab_ref, head, in_head, in_seq) - lse_ref[0, :, head:head + 1])
                row_term = jnp.sum(jnp.where(in_head, do_o, 0.0), axis=-1, keepdims=True)
                dom = jnp.where(in_head, do2, jnp.zeros_like(do2))
                ds = p * (_dot_nt(dom, v2) - row_term)
                ds_acc[head] += ds
                dsb = ds.astype(BF16)
                dq2 = jnp.where(in_head, _dot(dsb, k2), dq2)
                dk2 = jnp.where(in_head, _dot_tn(dsb, q2s), dk2)
                dv2 = jnp.where(in_head, _dot_tn(p.astype(BF16), do2), dv2)
            dqkv_ref[0, pl.ds(koff, QB), cols] = (dq2 * (ATT_HD ** -0.5)).astype(BF16)
            dkpad[pl.ds(koff, KWIN), cols] += dk2
            dvpad[pl.ds(koff, KWIN), cols] += dv2

        @pl.when(i == nq - 1)
        def _():
            dqkv_ref[0, :, ATT_W:2 * ATT_W] = dkpad[pl.ds(KPAD, S), :].astype(BF16)
            dqkv_ref[0, :, 2 * ATT_W:3 * ATT_W] = dvpad[pl.ds(KPAD, S), :].astype(BF16)

        @pl.when((i == nq - 1) & (b == NB - 1))
        def _():
            out = pltpu.make_async_copy(ds_acc, ds_hbm, ds_sem)
            out.start()
            out.wait()

    seq = lambda col: pl.BlockSpec((1, S, ATT_W), lambda b, i: (b, 0, col), pipeline_mode=pl.Buffered(1))
    rows = pl.BlockSpec((1, QB, ATT_W), lambda b, i: (b, i, 0))
    return _pallas(
        body, "att_bwd", (NB, nq),
        [rows, seq(1), seq(2), rows, pl.BlockSpec((1, QB, 128), lambda b, i: (b, i, 0)), rows, _const((ATT_HEADS, QB, KWIN))],
        [pl.BlockSpec((1, S, 3 * ATT_W), lambda b, i: (b, 0, 0)), ANY],
        [jax.ShapeDtypeStruct((NB, S, 3 * ATT_W), BF16), jax.ShapeDtypeStruct((ATT_HEADS, QB, KWIN), F32)],
        (qkv, qkv, qkv, o, lse, do, tab),
        scratch_shapes=[pltpu.VMEM((S + KPAD, ATT_W), BF16), pltpu.VMEM((S + KPAD, ATT_W), BF16),
                        pltpu.VMEM((S + KPAD, ATT_W), F32), pltpu.VMEM((S + KPAD, ATT_W), F32),
                        pltpu.VMEM((ATT_HEADS, QB, KWIN), F32), pltpu.SemaphoreType.DMA],
        sem=("arbitrary", "arbitrary"), ride=ride)


def _rel_bias_grad(ds):
    def body(ds_ref, o_ref):
        sub = lax.broadcasted_iota(jnp.int32, (8, 1), 0)
        ridx = _rel_index_of_column(lax.broadcasted_iota(jnp.int32, (DS_LANES, 1), 0))
        onehot = (ridx == lax.broadcasted_iota(jnp.int32, (1, N_REL), 1)).astype(F32)
        def rows8(q8, accs):
            shift = lax.rem(DS_LANES - q8 * 8, DS_LANES)
            out = []
            for head in range(ATT_HEADS):
                tile = ds_ref[head, pl.ds(pl.multiple_of(q8 * 8, 8), 8), :]
                tile = jnp.concatenate([tile, jnp.zeros((8, DS_LANES - KWIN), F32)], axis=1)
                out.append(accs[head] + pltpu.roll(tile, shift, 1))
            return tuple(out)
        accs = lax.fori_loop(0, QB // 8, rows8, tuple(jnp.zeros((8, DS_LANES), F32) for _ in range(ATT_HEADS)))
        for head in range(ATT_HEADS):
            acc = accs[head]
            diag = jnp.zeros((8, DS_LANES), F32)
            for s in range(8):
                shifted = acc if s == 0 else pltpu.roll(acc, DS_LANES - s, 1)
                diag = jnp.where(sub == s, shifted, diag)
            z = jnp.sum(diag, axis=0, keepdims=True)
            o_ref[head:head + 1, :] = jnp.dot(z, onehot, preferred_element_type=F32, precision=lax.Precision.HIGHEST)

    return pl.pallas_call(body, out_shape=jax.ShapeDtypeStruct((ATT_HEADS, N_REL), F32), name="rel_bias_grad",
                          compiler_params=_params())(ds)


def _memkv_fwd(mem, g, w_kv, tm):
    R = mem.shape[0]
    tm = min(tm, R)

    def body(m_ref, g_ref, w_ref, h_ref, kv_ref):
        xhat, _ = _rms_stats(m_ref[...])
        h = (xhat * g_ref[...]).astype(BF16)
        h_ref[...] = h
        kv_ref[...] = _dot(h, w_ref[...]).astype(BF16)

    row = pl.BlockSpec((tm, D), lambda t: (t, 0))
    return pl.pallas_call(
        body, grid=(R // tm,), name="memkv_fwd", in_specs=[row, _const((1, D)), _const((D, 2 * MEM_W))], out_specs=[row, row],
        out_shape=[jax.ShapeDtypeStruct((R, D), BF16), jax.ShapeDtypeStruct((R, 2 * MEM_W), BF16)],
        compiler_params=_params("parallel"),
    )(mem, g, w_kv)


def _memkv_bwd(mem, dkv, w_kv, tm):
    R = mem.shape[0]
    tm = min(tm, R)

    def body(m_ref, dkv_ref, w_ref, dg_ref):
        xhat, _ = _rms_stats(m_ref[...])
        dh = _dot_nt(dkv_ref[...].astype(BF16), w_ref[...])

        @pl.when(pl.program_id(0) == 0)
        def _():
            dg_ref[...] = jnp.zeros_like(dg_ref)
        dg_ref[...] += jnp.sum(dh * xhat, axis=0, keepdims=True)

    row = pl.BlockSpec((tm, D), lambda t: (t, 0))
    return pl.pallas_call(
        body, grid=(R // tm,), name="memkv_bwd", in_specs=[row, row, _const((D, 2 * MEM_W))],
        out_specs=pl.BlockSpec((1, D), lambda t: (0, 0)), out_shape=jax.ShapeDtypeStruct((1, D), F32),
        compiler_params=_params("arbitrary"),
    )(mem, dkv, w_kv)


def _mem_probs(qh, kh):
    s = _dot_nt(qh, kh) * (MEM_HD ** -0.5)
    e = jnp.exp(s - jnp.max(s, axis=-1, keepdims=True))
    return e * (1.0 / jnp.sum(e, axis=-1, keepdims=True))


def _mematt_fwd(mq, kv, tq):
    NB, S, _ = mq.shape
    M = kv.shape[1]

    def body(q_ref, kv_ref, o_ref):
        for h in range(MEM_HEADS):
            cols = slice(h * MEM_HD, (h + 1) * MEM_HD)
            p = _mem_probs(q_ref[0, :, cols], kv_ref[0, :, cols])
            o_ref[0, :, cols] = _dot(p.astype(BF16), kv_ref[0, :, MEM_W + h * MEM_HD:MEM_W + (h + 1) * MEM_HD]).astype(BF16)

    return pl.pallas_call(
        body, grid=(NB, S // tq), name="mematt_fwd",
        in_specs=[pl.BlockSpec((1, tq, MEM_W), lambda b, i: (b, i, 0)), pl.BlockSpec((1, M, 2 * MEM_W), lambda b, i: (b, 0, 0))],
        out_specs=pl.BlockSpec((1, tq, MEM_W), lambda b, i: (b, i, 0)),
        out_shape=jax.ShapeDtypeStruct((NB, S, MEM_W), BF16), compiler_params=_params("parallel", "parallel"),
    )(mq, kv)


def _mematt_bwd(mq, kv, do, tq):
    NB, S, _ = mq.shape
    M = kv.shape[1]

    def body(q_ref, kv_ref, do_ref, dq_ref, dkv_ref):
        @pl.when(pl.program_id(1) == 0)
        def _():
            dkv_ref[...] = jnp.zeros_like(dkv_ref)
        for h in range(MEM_HEADS):
            cols = slice(h * MEM_HD, (h + 1) * MEM_HD)
            vcols = slice(MEM_W + h * MEM_HD, MEM_W + (h + 1) * MEM_HD)
            qh, kh, vh, doh = q_ref[0, :, cols], kv_ref[0, :, cols], kv_ref[0, :, vcols], do_ref[0, :, cols]
            p = _mem_probs(qh, kh)
            dp = _dot_nt(doh, vh)
            ds = p * (dp - jnp.sum(p * dp, axis=-1, keepdims=True))
            dss = (ds * (MEM_HD ** -0.5)).astype(BF16)
            dq_ref[0, :, cols] = _dot(dss, kh).astype(BF16)
            dkv_ref[0, :, cols] += _dot_tn(dss, qh)
            dkv_ref[0, :, vcols] += _dot_tn(p.astype(BF16), doh)

    qspec = pl.BlockSpec((1, tq, MEM_W), lambda b, i: (b, i, 0))
    kvspec = pl.BlockSpec((1, M, 2 * MEM_W), lambda b, i: (b, 0, 0))
    return pl.pallas_call(
        body, grid=(NB, S // tq), name="mematt_bwd", in_specs=[qspec, kvspec, qspec], out_specs=[qspec, kvspec],
        out_shape=[jax.ShapeDtypeStruct((NB, S, MEM_W), BF16), jax.ShapeDtypeStruct((NB, M, 2 * MEM_W), F32)],
        compiler_params=_params("arbitrary", "arbitrary"),
    )(mq, kv, do)


def _branch(j, in_ref, w_ref, gl_ref, bg_ref):
    y = _dot(in_ref[...], w_ref[...])
    gate = _sigmoid(gl_ref[:, j * D:(j + 1) * D].astype(F32) + bg_ref[:, j * D:(j + 1) * D])
    return y, gate


def _combine_fwd(x, cact, oatt, omem, gl, bg, wpw, wo, wmo, wout, tm):
    T = x.shape[0]

    def body(x_ref, c_ref, a_ref, m_ref, gl_ref, bg_ref, wpw_ref, wo_ref, wmo_ref, wout_ref, xo_ref, y_ref):
        y = None
        for j, (in_ref, w_ref) in enumerate(((c_ref, wpw_ref), (a_ref, wo_ref), (m_ref, wmo_ref))):
            yj, gate = _branch(j, in_ref, w_ref, gl_ref, bg_ref)
            y = gate * yj if y is None else y + gate * yj
        y = y.astype(BF16)
        y_ref[...] = y
        xo_ref[...] = x_ref[...] + _dot(y, wout_ref[...])

    row = lambda w: pl.BlockSpec((tm, w), lambda t: (t, 0))
    wbr = _const((512, D))
    return pl.pallas_call(
        body, grid=(T // tm,), name="combine_fwd",
        in_specs=[row(D), row(512), row(512), row(512), row(3 * D), _const((1, 3 * D)), wbr, wbr, wbr, _const((D, D))],
        out_specs=[row(D), row(D)],
        out_shape=[jax.ShapeDtypeStruct((T, D), F32), jax.ShapeDtypeStruct((T, D), BF16)],
        compiler_params=_params("parallel"),
    )(x, cact, oatt, omem, gl, bg, wpw, wo, wmo, wout)


def _combine_bwd(dx, cact, oatt, omem, gl, bg, wpw, wo, wmo, wout, tm, ride=None):
    T = dx.shape[0]

    def body(dx_ref, c_ref, a_ref, m_ref, gl_ref, bg_ref, wpw_ref, wo_ref, wmo_ref, wout_ref,
             dgl_ref, dc_ref, da_ref, dm_ref, dyc_ref, dya_ref, dym_ref, dbg_ref):
        dy = _dot_nt(dx_ref[...].astype(BF16), wout_ref[...])

        @pl.when(pl.program_id(0) == 0)
        def _():
            dbg_ref[...] = jnp.zeros_like(dbg_ref)
        branches = ((c_ref, wpw_ref, dyc_ref, dc_ref), (a_ref, wo_ref, dya_ref, da_ref), (m_ref, wmo_ref, dym_ref, dm_ref))
        for j, (in_ref, w_ref, dyb_ref, din_ref) in enumerate(branches):
            yj, gate = _branch(j, in_ref, w_ref, gl_ref, bg_ref)
            dyg = dy * gate
            dlogit = dyg * yj * (1.0 - gate)
            dgl_ref[:, j * D:(j + 1) * D] = dlogit.astype(BF16)
            dbg_ref[:, j * D:(j + 1) * D] += jnp.sum(dlogit, axis=0, keepdims=True)
            dyb = dyg.astype(BF16)
            dyb_ref[...] = dyb
            din_ref[...] = _dot_nt(dyb, w_ref[...]).astype(din_ref.dtype)

    row = lambda w: pl.BlockSpec((tm, w), lambda t: (t, 0))
    wbr = _const((512, D))
    sds = jax.ShapeDtypeStruct
    return _pallas(
        body, "combine_bwd", (T // tm,),
        [row(D), row(512), row(512), row(512), row(3 * D), _const((1, 3 * D)), wbr, wbr, wbr, _const((D, D))],
        [row(3 * D), row(512), row(512), row(512), row(D), row(D), row(D), pl.BlockSpec((1, 3 * D), lambda t: (0, 0))],
        [sds((T, 3 * D), BF16), sds((T, 512), F32), sds((T, 512), BF16), sds((T, 512), BF16),
         sds((T, D), BF16), sds((T, D), BF16), sds((T, D), BF16), sds((1, 3 * D), F32)],
        (dx, cact, oatt, omem, gl, bg, wpw, wo, wmo, wout), sem=("arbitrary",), ride=ride)


def _peer(x, y, c, rel):
    rx, ry, rc = (rel >> 2) & 1, (rel >> 1) & 1, rel & 1
    return ((1 - x) if rx else x, (1 - y) if ry else y, (1 - c) if rc else c)


def _all_sum_small(parts):
    n = len(parts)

    def body(*refs):
        p_refs, o_refs, slots = refs[:n], refs[n:2 * n], refs[2 * n:3 * n]
        send_sems, recv_sems = refs[3 * n:]
        x, y, c = _my_coords()
        me = _dev_index(x, y, c)

        def copy(i, rel, arrival):
            peer = _peer(x, y, c, rel)
            return pltpu.make_async_remote_copy(
                src_ref=p_refs[i], dst_ref=slots[i].at[_dev_index(*peer) if arrival else me],
                send_sem=send_sems.at[i, rel - 1], recv_sem=recv_sems.at[i, rel - 1], device_id=peer, device_id_type=MESH)

        for i in range(n):
            slots[i][me] = p_refs[i][...]
        for rel in range(1, NDEV):
            for i in range(n):
                copy(i, rel, False).start()
        for rel in range(1, NDEV):
            for i in range(n):
                copy(i, rel, True).wait_recv()
        for rel in range(1, NDEV):
            for i in range(n):
                copy(i, rel, False).wait_send()
        for i in range(n):
            total = slots[i][0]
            for d in range(1, NDEV):
                total = total + slots[i][d]
            o_refs[i][...] = total

    vmem = pl.BlockSpec(memory_space=pltpu.VMEM)
    return pl.pallas_call(
        body, out_shape=[jax.ShapeDtypeStruct(p.shape, F32) for p in parts], name="all_sum_small",
        in_specs=[vmem] * n, out_specs=[vmem] * n,
        scratch_shapes=[pltpu.VMEM((NDEV,) + p.shape, F32) for p in parts]
        + [pltpu.SemaphoreType.DMA((n, NDEV - 1)), pltpu.SemaphoreType.DMA((n, NDEV - 1))],
        compiler_params=pltpu.CompilerParams(has_side_effects=True),
    )(*parts)


HBM = pl.BlockSpec(memory_space=pltpu.HBM)
SEM = pl.BlockSpec(memory_space=pltpu.SEMAPHORE)


def _own_block(g, kind, m, tag):
    def body(g_ref, land_ref, staged, sem):
        me = _dev_index(*_my_coords())
        for cp in (pltpu.make_async_copy(_window(g_ref, kind, m, me), staged, sem),
                   pltpu.make_async_copy(staged, land_ref.at[me], sem)):
            cp.start()
            cp.wait()

    block = (m, g.shape[1]) if kind == 'row' else (g.shape[0], m)
    return pl.pallas_call(body, in_specs=[ANY], out_specs=ANY, out_shape=jax.ShapeDtypeStruct((NDEV,) + block, g.dtype),
                          scratch_shapes=[pltpu.VMEM(block, g.dtype), pltpu.SemaphoreType.DMA], name="own_block_" + tag)(g)


def _scatter_start(g, land, kind, m, tag):
    def body(g_ref, land_ref, send_sems, recv_sems, g_thru, land_thru, token):
        x, y, c = _my_coords()
        me = _dev_index(x, y, c)
        for rel in range(1, NDEV):
            peer = _peer(x, y, c, rel)
            pltpu.make_async_remote_copy(src_ref=_window(g_ref, kind, m, _dev_index(*peer)), dst_ref=land_ref.at[me],
                                         send_sem=send_sems.at[rel - 1], recv_sem=recv_sems.at[rel - 1],
                                         device_id=peer, device_id_type=MESH).start()
        token[...] = jnp.zeros_like(token)

    return pl.pallas_call(
        body, name="scatter_start_" + tag,
        out_shape=(pltpu.SemaphoreType.DMA((NDEV - 1,)), pltpu.SemaphoreType.DMA((NDEV - 1,)), pltpu.HBM(g.shape, g.dtype),
                   pltpu.HBM(land.shape, land.dtype), jax.ShapeDtypeStruct((8, 128), F32)),
        in_specs=(HBM, HBM), out_specs=(SEM, SEM, HBM, HBM, pl.BlockSpec(memory_space=pltpu.VMEM)),
        input_output_aliases={0: 2, 1: 3},
        compiler_params=pltpu.CompilerParams(has_side_effects=pltpu.SideEffectType.DATAFLOW_SIDE_EFFECTING),
    )(pltpu.with_memory_space_constraint(g, pltpu.HBM), pltpu.with_memory_space_constraint(land, pltpu.HBM))


def _scatter_wait(send_sems, recv_sems, g_thru, land_thru, after, kind, m, tag):
    n_after = len(after)

    def body(*refs):
        g_ref, land_ref, send_sems, recv_sems = refs[:4]
        x, y, c = _my_coords()
        me = _dev_index(x, y, c)
        for rel in range(1, NDEV):
            peer = _peer(x, y, c, rel)
            dev = _dev_index(*peer)
            cp = pltpu.make_async_remote_copy(src_ref=_window(g_ref, kind, m, me), dst_ref=land_ref.at[dev],
                                              send_sem=send_sems.at[rel - 1], recv_sem=recv_sems.at[rel - 1],
                                              device_id=peer, device_id_type=MESH)
            cp.wait_send()
            cp.wait_recv()

    return pl.pallas_call(
        body, name="scatter_wait_" + tag,
        out_shape=(pltpu.HBM(g_thru.shape, g_thru.dtype), pltpu.HBM(land_thru.shape, land_thru.dtype)),
        in_specs=(HBM, HBM, SEM, SEM) + (ANY,) * n_after, out_specs=(HBM, HBM), input_output_aliases={0: 0, 1: 1},
        compiler_params=pltpu.CompilerParams(has_side_effects=pltpu.SideEffectType.DATAFLOW_SIDE_EFFECTING),
    )(g_thru, land_thru, send_sems, recv_sems, *after)[1]


def _adamw_math(w, g, m, v):
    m = ADAM_B1 * m + (1.0 - ADAM_B1) * g
    v = ADAM_B2 * v + (1.0 - ADAM_B2) * (g * g)
    m_hat = m / (1.0 - ADAM_B1 ** ADAM_STEP)
    v_hat = v / (1.0 - ADAM_B2 ** ADAM_STEP)
    delta = -ADAM_LR * (m_hat / (jnp.sqrt(v_hat) + ADAM_EPS) + ADAM_WD * w)
    return delta, m, v


def _sum_adamw(parts, w, m, v, name, after=None):
    R, C = w.shape
    n_parts = len(parts)
    cg = C // n_parts
    tr = max(t for t in range(8, 257, 8) if R % t == 0)
    deps = [] if after is None else [after]

    def body(*refs):
        p_refs = refs[:n_parts]
        w_ref, m_ref, v_ref = refs[n_parts:n_parts + 3]
        g_ref, d_ref, mo_ref, vo_ref = refs[n_parts + 3 + len(deps):]
        for k, p_ref in enumerate(p_refs):
            @pl.when(pl.program_id(0) == k)
            def _():
                g = p_ref[0].astype(F32)
                for d in range(1, NDEV):
                    g = g + p_ref[d].astype(F32)
                g_ref[...] = g
                d_ref[...], mo_ref[...], vo_ref[...] = _adamw_math(w_ref[...], g, m_ref[...], v_ref[...])

    part = pl.BlockSpec((NDEV, tr, cg), lambda k, t: (0, t, 0))
    blk = pl.BlockSpec((tr, cg), lambda k, t: (t, k))
    return pl.pallas_call(
        body, grid=(n_parts, R // tr), name=name, in_specs=[part] * n_parts + [blk, blk, blk] + [ANY] * len(deps),
        out_specs=[blk] * 4, out_shape=[jax.ShapeDtypeStruct((R, C), F32)] * 4, compiler_params=_params("parallel", "parallel"),
    )(*parts, w, m, v, *deps)


def _adamw_small(ws, gs, ms, vs):
    n = len(ws)

    def body(*refs):
        w_refs, g_refs, m_refs, v_refs = (refs[k * n:(k + 1) * n] for k in range(4))
        d_refs, mo_refs, vo_refs = (refs[(4 + k) * n:(5 + k) * n] for k in range(3))
        for i in range(n):
            d_refs[i][...], mo_refs[i][...], vo_refs[i][...] = _adamw_math(w_refs[i][...], g_refs[i][...], m_refs[i][...], v_refs[i][...])

    shapes = [jax.ShapeDtypeStruct(a.shape, F32) for a in ws]
    outs = pl.pallas_call(body, out_shape=shapes * 3, name="adamw_small", compiler_params=_params())(*ws, *gs, *ms, *vs)
    return outs[:n], outs[n:2 * n], outs[2 * n:]


def kernel(x, mem, ffn1_norm, ffn1_w_up, ffn1_w_down, mix_norm, mem_norm, w_in, b_gate, conv_dw_w, conv_dw_b, conv_ln_g, conv_ln_b, conv_w_pw, att_rel_bias, att_w_o, mem_w_kv, mem_w_o, w_out, ffn2_norm, ffn2_w_up, ffn2_w_down, final_norm, loss_target, m_ffn1_norm, m_ffn1_w_up, m_ffn1_w_down, m_mix_norm, m_mem_norm, m_w_in, m_b_gate, m_conv_dw_w, m_conv_dw_b, m_conv_ln_g, m_conv_ln_b, m_conv_w_pw, m_att_rel_bias, m_att_w_o, m_mem_w_kv, m_mem_w_o, m_w_out, m_ffn2_norm, m_ffn2_w_up, m_ffn2_w_down, m_final_norm, v_ffn1_norm, v_ffn1_w_up, v_ffn1_w_down, v_mix_norm, v_mem_norm, v_w_in, v_b_gate, v_conv_dw_w, v_conv_dw_b, v_conv_ln_g, v_conv_ln_b, v_conv_w_pw, v_att_rel_bias, v_att_w_o, v_mem_w_kv, v_mem_w_o, v_w_out, v_ffn2_norm, v_ffn2_w_up, v_ffn2_w_down, v_final_norm):
    given = dict(locals())
    w = {n: given[n] for n in WEIGHTS}
    mom = {n: given["m_" + n] for n in WEIGHTS}
    var = {n: given["v_" + n] for n in WEIGHTS}

    NB, S, _ = x.shape
    T = NB * S
    ML = mem.shape[1]
    x0 = x.reshape(T, D)
    target = loss_target.reshape(T, D)
    mem2 = mem.reshape(NB * ML, D)

    def block(t, n):
        return jnp.transpose(t[0]) if n in TRANSPOSED else t[0]

    sh = dict(zip(BIG_ORDER, _cast_shards([block(w[n], n) for n in BIG_ORDER])))
    dw_t = jnp.transpose(conv_dw_w[0])

    def gather(names, extra=(), extra_kinds=()):
        return _gather_ride([sh[n] for n in names] + list(extra), [BIG[n] for n in names] + list(extra_kinds))

    W = {}
    names0 = ['ffn1_w_up']
    tab, got = _bias_table(att_rel_bias[0], ride=gather(names0, [dw_t], [('row', dw_t.shape[0])]))
    W.update(zip(names0, got[:1]))
    dw_full = jnp.transpose(got[1])
    conv_vec = jnp.concatenate([conv_dw_b, conv_ln_g, conv_ln_b, jnp.zeros((5, CONV_W), F32)], axis=0)
    fin_g = final_norm.reshape(1, D)

    names1 = ['ffn1_w_down', 'w_in', 'conv_w_pw', 'att_w_o', 'mem_w_kv', 'mem_w_o', 'w_out']
    (ab1, act_ffn1), got = _ffn_up(x0, ffn1_norm, W['ffn1_w_up'], TILE_FFN_FWD, "ffn1_up", ride=gather(names1))
    W.update(zip(names1, got))
    x1 = _ffn_down(x0, act_ffn1, W['ffn1_w_down'], TILE_FFN_FWD, "ffn1_down")
    (uc, qkv, mq, gl, hmix), _ = _mix_fwd(x1, mix_norm, W['w_in'], TILE_TOKENS)
    uc3 = uc.reshape(NB, S, 2 * CONV_W)
    qkv3 = qkv.reshape(NB, S, 3 * ATT_W)
    mq3 = mq.reshape(NB, S, MEM_W)
    cact, conv_z = _conv_fwd(uc3, dw_full, conv_vec)
    cact = cact.reshape(T, CONV_W)
    names2 = ['ffn2_w_up', 'ffn2_w_down']
    oatt3, att_lse, got = _att_fwd(qkv3, tab, ride=gather(names2))
    W.update(zip(names2, got))
    oatt = oatt3.reshape(T, ATT_W)
    memh, kv = _memkv_fwd(mem2, mem_norm, W['mem_w_kv'], TILE_TOKENS)
    kv3 = kv.reshape(NB, ML, 2 * MEM_W)
    omem = _mematt_fwd(mq3, kv3, TILE_TOKENS).reshape(T, MEM_W)
    branch_w = (W['conv_w_pw'], W['att_w_o'], W['mem_w_o'], W['w_out'])
    x2, ymix = _combine_fwd(x1, cact, oatt, omem, gl, b_gate, *branch_w, TILE_COMBINE)
    dx3, ab2, loss_part, dg_final = _ffn_fwd_loss(x2, ffn2_norm, W['ffn2_w_up'], W['ffn2_w_down'], fin_g, target, TILE_FFN_FWD,
                                                  "ffn2_fwd_loss")

    def scatter(grads, names):
        return _scatter_ride(grads, [BIG[n] for n in names])

    G, P = {}, {}
    dx2, dab2, act2, h2, dg_ffn2 = _ffn_bwd(x2, dx3, ab2, ffn2_norm, W['ffn2_w_up'], W['ffn2_w_down'], TILE_FFN, "ffn2_bwd")
    g_up, _ = _tn_matmul(dab2, h2, 512, "grad_ffn2_w_up_a", tt=TILE_GRAD_TOKENS_WIDE, x_part=(0, 2), out_rows=2 * FF)
    G['ffn2_w_up'], _ = _tn_matmul(dab2, h2, 512, "grad_ffn2_w_up_b", tt=TILE_GRAD_TOKENS_WIDE, x_part=(1, 2), out_rows=2 * FF,
                                   prev=g_up)
    G['ffn2_w_down'], _ = _tn_matmul(act2, dx3, 512, "grad_ffn2_w_down", scale=0.5, tt=TILE_GRAD_TOKENS_WIDE)
    (dgl, dcact, doatt, domem, dyc, dya, dym, dbg), got = _combine_bwd(
        dx2, cact, oatt, omem, gl, b_gate, *branch_w, TILE_COMBINE, ride=scatter([G['ffn2_w_up']], ['ffn2_w_up']))
    P['ffn2_w_up'] = got
    G['w_out'], _ = _tn_matmul(ymix, dx2, D, "grad_w_out", tt=TILE_GRAD_TOKENS_WIDE)
    G['conv_w_pw'], _ = _tn_matmul(cact, dyc, D, "grad_conv_w_pw")
    G['att_w_o'], _ = _tn_matmul(oatt, dya, D, "grad_att_w_o")
    G['mem_w_o'], _ = _tn_matmul(omem, dym, D, "grad_mem_w_o")
    dmq3, dkv3 = _mematt_bwd(mq3, kv3, domem.reshape(NB, S, MEM_W), TILE_TOKENS)
    dkv = dkv3.reshape(NB * ML, 2 * MEM_W)
    dg_mem = _memkv_bwd(mem2, dkv, W['mem_w_kv'], TILE_TOKENS)
    G['mem_w_kv'], _ = _tn_matmul(memh, dkv, 512, "grad_mem_w_kv")
    names = ['ffn2_w_down', 'w_out', 'conv_w_pw', 'att_w_o', 'mem_w_o']
    (dqkv3, dscore), got = _att_bwd(qkv3, oatt3, att_lse, doatt.reshape(NB, S, ATT_W), tab,
                                    ride=scatter([G[n] for n in names], names))
    P.update((n, [p]) for n, p in zip(names, got))
    d_rel = _rel_bias_grad(dscore)
    (duc3, d_dw, d_cvec), got = _conv_bwd(uc3, conv_z, dcact.reshape(NB, S, CONV_W), dw_full, conv_vec,
                                          ride=scatter([G['mem_w_kv']], ['mem_w_kv']))
    P['mem_w_kv'] = got
    duc, dqkv, dmq = duc3.reshape(T, 2 * CONV_W), dqkv3.reshape(T, 3 * ATT_W), dmq3.reshape(T, MEM_W)
    g_in, _ = _tn_matmul(hmix, duc, 1024, "grad_w_in_conv", out_cols=IN_COLS, col_off=0)
    g_in, _ = _tn_matmul(hmix, dqkv, 512, "grad_w_in_qkv", out_cols=IN_COLS, col_off=1024, prev=g_in)
    g_in, _ = _tn_matmul(hmix, dmq, 512, "grad_w_in_mq", out_cols=IN_COLS, col_off=2560, prev=g_in)
    G['w_in'], _ = _tn_matmul(hmix, dgl, 1024, "grad_w_in_gate", out_cols=IN_COLS, col_off=3072, prev=g_in)
    def start_scatter(g, name, tag):
        kind = BIG[name]
        return _scatter_start(g, _own_block(g, *kind, tag), *kind, tag) + (kind, tag)

    def wait_scatter(started, after):
        send_sems, recv_sems, g_thru, land_thru, _, kind, tag = started
        return _scatter_wait(send_sems, recv_sems, g_thru, land_thru, after, *kind, tag)

    ex_in = start_scatter(G['w_in'], 'w_in', "w_in")
    (dx1, dg_mix), _ = _mix_bwd(x1, dx2, duc, dqkv, dmq, dgl, mix_norm, W['w_in'], TILE_TOKENS, after=ex_in[4])
    dx0, dab1, act1, h1, dg_ffn1 = _ffn_bwd(x0, dx1, ab1, ffn1_norm, W['ffn1_w_up'], W['ffn1_w_down'], TILE_FFN, "ffn1_bwd")
    g_wd1, _ = _tn_matmul(act1, dx1, 512, "grad_ffn1_w_down", scale=0.5, tt=TILE_GRAD_TOKENS_WIDE)
    ex_wd = start_scatter(g_wd1, 'ffn1_w_down', "ffn1_w_down")
    g_wu1a, _ = _tn_matmul(dab1, h1, 512, "grad_ffn1_w_up_a", tt=TILE_GRAD_TOKENS_WIDEST, y_part=(0, 2), after=ex_wd[4])
    ex_a = start_scatter(g_wu1a, 'ffn1_w_up', "ffn1_w_up_a")
    g_wu1b, _ = _tn_matmul(dab1, h1, 512, "grad_ffn1_w_up_b", tt=TILE_GRAD_TOKENS_WIDEST, y_part=(1, 2), after=ex_a[4])
    ex_b = start_scatter(g_wu1b, 'ffn1_w_up', "ffn1_w_up_b")
    token = ex_b[4]

    small_names = ['loss', 'ffn1_norm', 'mix_norm', 'mem_norm', 'b_gate', 'conv_dw_w', 'conv_vec', 'att_rel_bias', 'ffn2_norm',
                   'final_norm']
    small = dict(zip(small_names, _all_sum_small(
        [loss_part + token[0:1], dg_ffn1, dg_mix, dg_mem, dbg, d_dw, d_cvec, d_rel, dg_ffn2, dg_final])))
    loss = small['loss'][0, 0]
    me = _dev_index(*_my_coords())
    for i, n in enumerate(['conv_dw_b', 'conv_ln_g', 'conv_ln_b']):
        small[n] = small['conv_vec'][i:i + 1]
    small['conv_dw_w'] = lax.dynamic_slice(small['conv_dw_w'], (0, me * conv_dw_w.shape[2]), (CONV_K, conv_dw_w.shape[2]))
    little = [n for n in WEIGHTS if n not in BIG]
    as2d = lambda t, n: t.reshape(small[n].shape)
    d_s, m_s, v_s = _adamw_small([as2d(w[n], n) for n in little], [small[n] for n in little],
                                 [as2d(mom[n], n) for n in little], [as2d(var[n], n) for n in little])
    grad, delta, new_m, new_v = {}, {}, {}, {}
    for i, n in enumerate(little):
        grad[n], delta[n], new_m[n], new_v[n] = (t.reshape(w[n].shape) for t in (small[n], d_s[i], m_s[i], v_s[i]))
    done = [d_s[0]]
    waited = {'w_in': [ex_in], 'ffn1_w_down': [ex_wd], 'ffn1_w_up': [ex_a, ex_b]}
    order = [n for n in BIG_ORDER if n not in waited] + list(waited)
    for n in order:
        if n in waited:
            P[n] = [wait_scatter(ex, done) for ex in waited[n]]
        outs = _sum_adamw(P[n], block(w[n], n), block(mom[n], n), block(var[n], n), "adamw_" + n,
                          after=None if n in waited else token)
        done.append(outs[0])
        grad[n], delta[n], new_m[n], new_v[n] = ((jnp.transpose(t) if n in TRANSPOSED else t)[None] for t in outs)

    return (loss, dx0.reshape(NB, S, D), *[grad[n] for n in WEIGHTS], *[delta[n] for n in WEIGHTS],
            *[new_m[n] for n in WEIGHTS], *[new_v[n] for n in WEIGHTS])
```

```python
import functools

import jax
import jax.numpy as jnp
from jax import lax
from jax.experimental import pallas as pl
from jax.experimental.pallas import tpu as pltpu

F32 = jnp.float32
BF16 = jnp.bfloat16

EPS = 1e-6
MASK_VALUE = -1e30
D = 1024
NDEV = 8
FF = 2816
FF_SHARD = 704
FF_HALF_ROWS = 352
FF_BLOCK_EDGES = ()
IN_COLS = 6144
CONV_W = 512
CONV_K = 31
CONV_HALO = 32
CONV_CHUNK = 32
CONV_WIN = CONV_CHUNK + 40
GLU_CHUNK = 128
ATT_W = 512
ATT_HEADS = 8
ATT_HD = 64
CHUNK = 64
LEFT_CHUNKS = 8
MAX_REL = 128
N_REL = 192
QB = 256
KWIN = QB + LEFT_CHUNKS * CHUNK
KPAD = LEFT_CHUNKS * CHUNK
DS_LANES = 1024
MEM_W = 512
MEM_HEADS = 4
MEM_HD = 128
ADAM_LR = 0.001
ADAM_B1 = 0.9
ADAM_B2 = 0.999
ADAM_EPS = 1e-08
ADAM_WD = 0.01
ADAM_STEP = 10
VMEM_LIMIT = 60 * 1024 * 1024
TILE_FFN = 256
TILE_FFN_FWD = 512
TILE_COMBINE = 256
TILE_TOKENS = 512
TILE_GRAD_TOKENS = 2048
TILE_GRAD_TOKENS_WIDE = 1024
TILE_GRAD_TOKENS_WIDEST = 512

MESH = pl.DeviceIdType.MESH
ANY = pl.BlockSpec(memory_space=pl.ANY)

WEIGHTS = ['ffn1_norm', 'ffn1_w_up', 'ffn1_w_down', 'mix_norm', 'mem_norm', 'w_in', 'b_gate', 'conv_dw_w', 'conv_dw_b',
           'conv_ln_g', 'conv_ln_b', 'conv_w_pw', 'att_rel_bias', 'att_w_o', 'mem_w_kv', 'mem_w_o', 'w_out', 'ffn2_norm',
           'ffn2_w_up', 'ffn2_w_down', 'final_norm']
BIG = {
    'ffn1_w_up': ('row', FF_SHARD), 'ffn1_w_down': ('row', FF_HALF_ROWS), 'w_in': ('col', 768),
    'conv_w_pw': ('col', 128), 'att_w_o': ('col', 128), 'mem_w_kv': ('row', 128), 'mem_w_o': ('col', 128),
    'w_out': ('row', 128), 'ffn2_w_up': ('row', FF_SHARD), 'ffn2_w_down': ('row', FF_HALF_ROWS),
}
BIG_ORDER = ['ffn1_w_up', 'ffn1_w_down', 'w_in', 'conv_w_pw', 'att_w_o', 'mem_w_kv', 'mem_w_o', 'w_out', 'ffn2_w_up', 'ffn2_w_down']
TRANSPOSED = ('ffn1_w_up', 'ffn2_w_up')


def _dot(a, b):
    return jnp.dot(a, b, preferred_element_type=F32)


def _dot_nt(a, b):
    return lax.dot_general(a, b, (((1,), (1,)), ((), ())), preferred_element_type=F32)


def _dot_tn(a, b):
    return lax.dot_general(a, b, (((0,), (0,)), ((), ())), preferred_element_type=F32)


def _sigmoid(v):
    return jax.nn.sigmoid(v)


def _const(shape):
    return pl.BlockSpec(shape, lambda *_: (0,) * len(shape), pipeline_mode=pl.Buffered(1))


def _params(*sem):
    return pltpu.CompilerParams(dimension_semantics=sem if sem else None, vmem_limit_bytes=VMEM_LIMIT)


def _my_coords():
    return lax.axis_index("x"), lax.axis_index("y"), lax.axis_index("c")


def _dev_index(px, py, pc):
    return 4 * px + 2 * py + pc


def _window(ref, kind, n, p):
    if kind == 'row':
        return ref.at[pl.ds(pl.multiple_of(p * n, n), n), :]
    return ref.at[:, pl.ds(pl.multiple_of(p * n, 128), n)]


def _full_shape(kind, n, shard_shape):
    if kind == 'row':
        return (NDEV * n, shard_shape[1])
    return (shard_shape[0], NDEV * n)


def _cast_shards(shards):
    n = len(shards)

    def body(*refs):
        for i in range(n):
            refs[n + i][...] = refs[i][...].astype(BF16)

    out_shape = [jax.ShapeDtypeStruct(s.shape, BF16) for s in shards]
    return pl.pallas_call(body, out_shape=out_shape, name="cast_shards", compiler_params=_params())(*shards)


class _Ride:
    def __init__(self, inputs, out_shape, scratch, start, finish, mids=()):
        self.inputs, self.out_shape, self.scratch = list(inputs), list(out_shape), list(scratch)
        self.start, self.finish, self.mids = start, finish, tuple(mids)


def _pallas(body, name, grid, in_specs, out_specs, out_shape, args, scratch_shapes=(), sem=None, aliases=None, ride=None,
            after=None):
    if ride is None:
        n_in, n_dep = len(args), 0 if after is None else 1

        def kernel_body(*refs):
            body(*refs[:n_in], *refs[n_in + n_dep:])

        outs = pl.pallas_call(kernel_body if n_dep else body, grid=grid, name=name, in_specs=list(in_specs) + [ANY] * n_dep,
                              out_specs=out_specs, out_shape=out_shape, scratch_shapes=list(scratch_shapes),
                              input_output_aliases=aliases or {}, compiler_params=_params(*sem),
                              )(*args, *([after] if n_dep else []))
        return list(outs), []
    n_in, n_out, n_scr = len(args), len(out_shape), len(scratch_shapes)
    r_in, r_out = len(ride.inputs), len(ride.out_shape)

    def wrapped(*refs):
        k_in, rin = refs[:n_in], refs[n_in:n_in + r_in]
        o0 = n_in + r_in
        k_out, rout = refs[o0:o0 + n_out], refs[o0 + n_out:o0 + n_out + r_out]
        s0 = o0 + n_out + r_out
        k_scr, rscr = refs[s0:s0 + n_scr], refs[s0 + n_scr:]
        ids = [pl.program_id(k) for k in range(len(grid))]
        first = functools.reduce(jnp.logical_and, [i == 0 for i in ids])
        last = functools.reduce(jnp.logical_and, [i == g - 1 for i, g in zip(ids, grid)])
        pl.when(first)(lambda: ride.start(rin, rout, rscr))
        single_step = all(g == 1 for g in grid)
        for quarter, mid in ride.mids:
            if not single_step:
                at_mid = functools.reduce(jnp.logical_and, [ids[0] == (quarter * grid[0]) // 4] + [i == 0 for i in ids[1:]])
                pl.when(at_mid)(functools.partial(mid, rin, rout, rscr))
        body(*k_in, *k_out, *k_scr)
        for _, mid in ride.mids:
            if single_step:
                mid(rin, rout, rscr)
        pl.when(last)(lambda: ride.finish(rin, rout, rscr))

    outs = pl.pallas_call(
        wrapped, grid=grid, name=name, in_specs=list(in_specs) + [ANY] * r_in, out_specs=list(out_specs) + [ANY] * r_out,
        out_shape=list(out_shape) + ride.out_shape, scratch_shapes=list(scratch_shapes) + ride.scratch,
        input_output_aliases=aliases or {}, compiler_params=_params(*(["arbitrary"] * len(grid))),
    )(*args, *ride.inputs)
    return list(outs[:n_out]), list(outs[n_out:])


def _gather_ride(shards, kinds):
    n = len(shards)

    def plan(rin, out, sems):
        send_sems, recv_sems, local_sems = sems[:3]
        x, y, c = _my_coords()
        me, sibling = (x, y, c), (x, y, 1 - c)
        xn, yn, diag = (1 - x, y), (x, 1 - y), (1 - x, 1 - y)

        def win(i, dev):
            return _window(out[i], kinds[i][0], kinds[i][1], _dev_index(*dev))

        def copy(i, k, block, to, from_shard=False):
            return pltpu.make_async_remote_copy(
                src_ref=rin[i] if from_shard else win(i, block), dst_ref=win(i, block),
                send_sem=send_sems.at[i, k], recv_sem=recv_sems.at[i, k], device_id=to, device_id_type=MESH)

        def each(fn):
            return [fn(i) for i in range(n)]

        return dict(
            local=lambda: each(lambda i: pltpu.make_async_copy(rin[i], win(i, me), local_sems.at[i])),
            own=lambda: [cp for i in range(n) for cp in (copy(i, 0, me, sibling, True), copy(i, 1, me, (*xn, c), True),
                                                         copy(i, 2, me, (*yn, c), True))],
            from_x=lambda: each(lambda i: copy(i, 1, (*xn, c), me)),
            from_y=lambda: each(lambda i: copy(i, 2, (*yn, c), me)),
            x_block_on_to_y=lambda: each(lambda i: copy(i, 3, (*xn, c), (*yn, c))),
            y_block_on_to_x=lambda: each(lambda i: copy(i, 3, (*yn, c), (*xn, c))),
            from_diag=lambda: each(lambda i: copy(i, 3, (*diag, c), me)),
            to_sibling=lambda: [copy(i, 4 + j, (*chip, c), sibling) for j, chip in enumerate((xn, yn, diag)) for i in range(n)],
            from_sibling=lambda: [cp for i in range(n) for cp in
                                  [copy(i, 0, sibling, me)] + [copy(i, 4 + j, (*chip, 1 - c), me) for j, chip in enumerate((xn, yn, diag))]],
            north=c == 1)

    def start(rin, out, sems):
        p = plan(rin, out, sems)
        for cp in p['local']() + p['own']():
            cp.start()

    def pass_diagonal(rin, out, sems):
        p = plan(rin, out, sems)

        @pl.when(p['north'])
        def _():
            for got, fwd in zip(p['from_x'](), p['x_block_on_to_y']()):
                got.wait_recv()
                fwd.start()

        @pl.when(jnp.logical_not(p['north']))
        def _():
            for got, fwd in zip(p['from_y'](), p['y_block_on_to_x']()):
                got.wait_recv()
                fwd.start()

    def pass_to_sibling(rin, out, sems):
        p = plan(rin, out, sems)

        @pl.when(p['north'])
        def _():
            for cp in p['from_y']():
                cp.wait_recv()

        @pl.when(jnp.logical_not(p['north']))
        def _():
            for cp in p['from_x']():
                cp.wait_recv()
        for cp in p['from_diag']():
            cp.wait_recv()
        for cp in p['to_sibling']():
            cp.start()

    def finish(rin, out, sems):
        p = plan(rin, out, sems)
        for cp in p['from_sibling']():
            cp.wait_recv()
        for cp in p['own']() + p['to_sibling']():
            cp.wait_send()

        @pl.when(p['north'])
        def _():
            for cp in p['x_block_on_to_y']():
                cp.wait_send()

        @pl.when(jnp.logical_not(p['north']))
        def _():
            for cp in p['y_block_on_to_x']():
                cp.wait_send()
        for cp in p['local']():
            cp.wait()

    out_shape = [jax.ShapeDtypeStruct(_full_shape(k, m, s.shape), s.dtype) for s, (k, m) in zip(shards, kinds)]
    scratch = [pltpu.SemaphoreType.DMA((n, 7)), pltpu.SemaphoreType.DMA((n, 7)), pltpu.SemaphoreType.DMA((n,))]
    return _Ride(shards, out_shape, scratch, start, finish, mids=((2, pass_diagonal), (3, pass_to_sibling)))


def _scatter_ride(grads, kinds):
    n = len(grads)

    def plan(g, out, sems):
        send_sems, recv_sems, local_sems = sems
        x, y, c = _my_coords()
        me = _dev_index(x, y, c)

        def local():
            return [pltpu.make_async_copy(_window(g[i], kinds[i][0], kinds[i][1], me), out[i].at[me], local_sems.at[i])
                    for i in range(n)]

        def remote(arrival):
            cps = []
            for rel in range(1, NDEV):
                peer = _peer(x, y, c, rel)
                dev = _dev_index(*peer)
                for i in range(n):
                    kind, m = kinds[i]
                    cps.append(pltpu.make_async_remote_copy(
                        src_ref=_window(g[i], kind, m, me if arrival else dev), dst_ref=out[i].at[dev if arrival else me],
                        send_sem=send_sems.at[i, rel - 1], recv_sem=recv_sems.at[i, rel - 1], device_id=peer, device_id_type=MESH))
            return cps

        return local, remote

    def start(g, out, sems):
        local, remote = plan(g, out, sems)
        for cp in local() + remote(False):
            cp.start()

    def finish(g, out, sems):
        local, remote = plan(g, out, sems)
        for cp in remote(True):
            cp.wait_recv()
        for cp in remote(False):
            cp.wait_send()
        for cp in local():
            cp.wait()

    def block_shape(gr, kind, m):
        return (m, gr.shape[1]) if kind == 'row' else (gr.shape[0], m)

    out_shape = [jax.ShapeDtypeStruct((NDEV,) + block_shape(gr, k, m), gr.dtype) for gr, (k, m) in zip(grads, kinds)]
    scratch = [pltpu.SemaphoreType.DMA((n, NDEV - 1)), pltpu.SemaphoreType.DMA((n, NDEV - 1)), pltpu.SemaphoreType.DMA((n,))]
    return _Ride(grads, out_shape, scratch, start, finish)


def _rms_stats(xf):
    r = lax.rsqrt(jnp.mean(xf * xf, axis=-1, keepdims=True) + EPS)
    return xf * r, r


def _rms_bwd(dh, g, xhat, r):
    dxhat = dh * g
    return r * (dxhat - xhat * jnp.mean(dxhat * xhat, axis=-1, keepdims=True))


def _ffn_blocks():
    edges = (0,) + FF_BLOCK_EDGES + (FF,)
    return [(slice(lo, hi), slice(FF + lo, FF + hi)) for lo, hi in zip(edges[:-1], edges[1:])]


def _swiglu_tile(x_ref, g_ref, wut_ref, wd_ref, ab_ref):
    xf = x_ref[...]
    xhat, _ = _rms_stats(xf)
    h = (xhat * g_ref[...]).astype(BF16)
    acc = jnp.zeros(xf.shape, F32)
    for ra, rb in _ffn_blocks():
        a = _dot_nt(h, wut_ref[ra, :])
        b = _dot_nt(h, wut_ref[rb, :])
        ab_ref[:, ra] = a.astype(BF16)
        ab_ref[:, rb] = b.astype(BF16)
        act = (a * _sigmoid(a) * b).astype(BF16)
        acc = acc + _dot(act, wd_ref[ra, :])
    return xf + 0.5 * acc


def _ffn_fwd(x, g, wut, wd, tm, name, ride=None):
    T = x.shape[0]

    def body(x_ref, g_ref, wut_ref, wd_ref, xo_ref, ab_ref):
        xo_ref[...] = _swiglu_tile(x_ref, g_ref, wut_ref, wd_ref, ab_ref)

    return _pallas(
        body, name, (T // tm,),
        [pl.BlockSpec((tm, D), lambda t: (t, 0)), _const((1, D)), _const((2 * FF, D)), _const((FF, D))],
        [pl.BlockSpec((tm, D), lambda t: (t, 0)), pl.BlockSpec((tm, 2 * FF), lambda t: (t, 0))],
        [jax.ShapeDtypeStruct((T, D), F32), jax.ShapeDtypeStruct((T, 2 * FF), BF16)],
        (x, g, wut, wd), sem=("arbitrary",), ride=ride)


def _ffn_fwd_loss(x, g, wut, wd, g_final, target, tm, name):
    T = x.shape[0]

    def body(x_ref, g_ref, wut_ref, wd_ref, gf_ref, t_ref, dx_ref, ab_ref, loss_ref, dgf_ref):
        xhat, r = _rms_stats(_swiglu_tile(x_ref, g_ref, wut_ref, wd_ref, ab_ref))
        gain = gf_ref[...]
        diff = xhat * gain - t_ref[...]
        dout = diff * (1.0 / D)

        @pl.when(pl.program_id(0) == 0)
        def _():
            loss_ref[...] = jnp.zeros_like(loss_ref)
            dgf_ref[...] = jnp.zeros_like(dgf_ref)
        sq = jnp.sum(jnp.sum(diff * diff, axis=0, keepdims=True), axis=1, keepdims=True)
        loss_ref[...] += jnp.broadcast_to(sq * (0.5 / D), (1, 128))
        dgf_ref[...] += jnp.sum(dout * xhat, axis=0, keepdims=True)
        dx_ref[...] = _rms_bwd(dout, gain, xhat, r)

    row = pl.BlockSpec((tm, D), lambda t: (t, 0))
    return pl.pallas_call(
        body, grid=(T // tm,), name=name,
        in_specs=[row, _const((1, D)), _const((2 * FF, D)), _const((FF, D)), _const((1, D)), row],
        out_specs=[row, pl.BlockSpec((tm, 2 * FF), lambda t: (t, 0)), pl.BlockSpec((1, 128), lambda t: (0, 0)),
                   pl.BlockSpec((1, D), lambda t: (0, 0))],
        out_shape=[jax.ShapeDtypeStruct((T, D), F32), jax.ShapeDtypeStruct((T, 2 * FF), BF16),
                   jax.ShapeDtypeStruct((1, 128), F32), jax.ShapeDtypeStruct((1, D), F32)],
        compiler_params=_params("arbitrary"),
    )(x, g, wut, wd, g_final, target)


def _ffn_bwd(x, dy, ab, g, wut, wd, tm, name):
    T = x.shape[0]

    def body(x_ref, dy_ref, ab_ref, g_ref, wut_ref, wd_ref, dx_ref, dab_ref, act_ref, h_ref, dg_ref):
        xf = x_ref[...]
        xhat, r = _rms_stats(xf)
        gain = g_ref[...]
        h_ref[...] = (xhat * gain).astype(BF16)
        dy = dy_ref[...]
        dyh = (0.5 * dy).astype(BF16)
        dh = jnp.zeros((tm, D), F32)
        for ra, rb in _ffn_blocks():
            a = ab_ref[:, ra].astype(F32)
            b = ab_ref[:, rb].astype(F32)
            dact = _dot_nt(dyh, wd_ref[ra, :])
            sg = _sigmoid(a)
            sl = a * sg
            act_ref[:, ra] = (sl * b).astype(BF16)
            da = (dact * b * (sg * (1.0 + a * (1.0 - sg)))).astype(BF16)
            db = (dact * sl).astype(BF16)
            dab_ref[:, ra] = da
            dab_ref[:, rb] = db
            dh = dh + _dot(da, wut_ref[ra, :]) + _dot(db, wut_ref[rb, :])
        dx_ref[...] = dy + _rms_bwd(dh, gain, xhat, r)

        @pl.when(pl.program_id(0) == 0)
        def _():
            dg_ref[...] = jnp.zeros_like(dg_ref)
        dg_ref[...] += jnp.sum(dh * xhat, axis=0, keepdims=True)

    return pl.pallas_call(
        body, grid=(T // tm,), name=name,
        in_specs=[pl.BlockSpec((tm, D), lambda t: (t, 0)), pl.BlockSpec((tm, D), lambda t: (t, 0)),
                  pl.BlockSpec((tm, 2 * FF), lambda t: (t, 0)), _const((1, D)), _const((2 * FF, D)), _const((FF, D))],
        out_specs=[pl.BlockSpec((tm, D), lambda t: (t, 0)), pl.BlockSpec((tm, 2 * FF), lambda t: (t, 0)),
                   pl.BlockSpec((tm, FF), lambda t: (t, 0)), pl.BlockSpec((tm, D), lambda t: (t, 0)),
                   pl.BlockSpec((1, D), lambda t: (0, 0))],
        out_shape=[jax.ShapeDtypeStruct((T, D), F32), jax.ShapeDtypeStruct((T, 2 * FF), BF16),
                   jax.ShapeDtypeStruct((T, FF), BF16), jax.ShapeDtypeStruct((T, D), BF16), jax.ShapeDtypeStruct((1, D), F32)],
        compiler_params=_params("arbitrary"),
    )(x, dy, ab, g, wut, wd)


def _tn_matmul(xm, ym, tn, name, scale=None, out_cols=None, col_off=0, prev=None, tt=TILE_GRAD_TOKENS, x_part=(0, 1),
               out_rows=None, y_part=(0, 1), ride=None, after=None):
    T = xm.shape[0]
    xi, xn = x_part
    yi, yn = y_part
    K = xm.shape[1] // xn
    N = ym.shape[1] // yn
    out_cols = N if out_cols is None else out_cols
    row_blk = xi if out_rows is not None else 0
    out_rows = K if out_rows is None else out_rows
    tt = min(tt, T)
    nt = T // tt
    off = col_off // tn

    def body(*refs):
        x_ref, y_ref = refs[0], refs[1]
        o_ref, acc = refs[-2], refs[-1]

        @pl.when(pl.program_id(1) == 0)
        def _():
            acc[...] = jnp.zeros_like(acc)
        acc[...] += _dot_tn(x_ref[...].astype(BF16), y_ref[...].astype(BF16))

        @pl.when(pl.program_id(1) == nt - 1)
        def _():
            res = acc[...]
            o_ref[...] = (res if scale is None else res * scale).astype(BF16)

    ycol = yi * (N // tn)
    in_specs = [pl.BlockSpec((tt, K), lambda n, t: (t, xi)), pl.BlockSpec((tt, tn), lambda n, t: (t, n + ycol))]
    args = [xm, ym]
    aliases = {}
    if prev is not None:
        in_specs.append(ANY)
        args.append(prev)
        aliases = {2: 0}
    outs, rode = _pallas(
        body, name, (N // tn, nt), in_specs, [pl.BlockSpec((K, tn), lambda n, t: (row_blk, n + off))],
        [jax.ShapeDtypeStruct((out_rows, out_cols), BF16)], args, scratch_shapes=[pltpu.VMEM((K, tn), F32)],
        sem=("parallel", "arbitrary"), aliases=aliases, ride=ride, after=after)
    return outs[0], rode


def _mix_fwd(x, g, w_in, tm, ride=None):
    T = x.shape[0]

    def body(x_ref, g_ref, w_ref, uc_ref, qkv_ref, mq_ref, gl_ref, h_ref):
        xhat, _ = _rms_stats(x_ref[...])
        h = (xhat * g_ref[...]).astype(BF16)
        h_ref[...] = h
        uc_ref[...] = _dot(h, w_ref[:, 0:1024])
        qkv_ref[...] = _dot(h, w_ref[:, 1024:2560]).astype(BF16)
        mq_ref[...] = _dot(h, w_ref[:, 2560:3072]).astype(BF16)
        for j in range(3):
            gl_ref[:, j * D:(j + 1) * D] = _dot(h, w_ref[:, 3072 + j * D:3072 + (j + 1) * D]).astype(BF16)

    row = lambda w: pl.BlockSpec((tm, w), lambda t: (t, 0))
    return _pallas(
        body, "mix_fwd", (T // tm,), [row(D), _const((1, D)), _const((D, IN_COLS))],
        [row(1024), row(1536), row(512), row(3072), row(D)],
        [jax.ShapeDtypeStruct((T, 1024), F32), jax.ShapeDtypeStruct((T, 1536), BF16), jax.ShapeDtypeStruct((T, 512), BF16),
         jax.ShapeDtypeStruct((T, 3072), BF16), jax.ShapeDtypeStruct((T, D), BF16)],
        (x, g, w_in), sem=("parallel",), ride=ride)


def _mix_bwd(x, dres, duc, dqkv, dmq, dgl, g, w_in, tm, ride=None, after=None):
    T = x.shape[0]

    def body(x_ref, dres_ref, duc_ref, dqkv_ref, dmq_ref, dgl_ref, g_ref, w_ref, dx_ref, dg_ref):
        xhat, r = _rms_stats(x_ref[...])
        dh = _dot_nt(duc_ref[...], w_ref[:, 0:1024])
        dh = dh + _dot_nt(dqkv_ref[...], w_ref[:, 1024:2560])
        dh = dh + _dot_nt(dmq_ref[...], w_ref[:, 2560:3072])
        dh = dh + _dot_nt(dgl_ref[...], w_ref[:, 3072:6144])
        dx_ref[...] = dres_ref[...] + _rms_bwd(dh, g_ref[...], xhat, r)

        @pl.when(pl.program_id(0) == 0)
        def _():
            dg_ref[...] = jnp.zeros_like(dg_ref)
        dg_ref[...] += jnp.sum(dh * xhat, axis=0, keepdims=True)

    row = lambda w: pl.BlockSpec((tm, w), lambda t: (t, 0))
    return _pallas(
        body, "mix_bwd", (T // tm,),
        [row(D), row(D), row(1024), row(1536), row(512), row(3072), _const((1, D)), _const((D, IN_COLS))],
        [row(D), pl.BlockSpec((1, D), lambda t: (0, 0))],
        [jax.ShapeDtypeStruct((T, D), F32), jax.ShapeDtypeStruct((1, D), F32)],
        (x, dres, duc, dqkv, dmq, dgl, g, w_in), sem=("arbitrary",), ride=ride, after=after)


def _shifted(win, base, copies):
    for k in range(8):
        copies[k] = win[base + k:base + k + CONV_CHUNK + 24]
    return copies


def _tap_slices(copies, tap):
    out = []
    for k in range(8):
        for a in range(4):
            j = tap(a, k)
            if 0 <= j < CONV_K:
                out.append((j, copies[k, pl.ds(8 * a, CONV_CHUNK), :]))
    return out


def _conv_taps(copies, w_ref, tap):
    acc = jnp.zeros((CONV_CHUNK, CONV_W), F32)
    for j, rows in _tap_slices(copies, tap):
        acc = acc + rows * w_ref[j:j + 1, :]
    return acc


def _fold8(v):
    acc = v[0:8]
    for r in range(8, CONV_CHUNK, 8):
        acc = acc + v[r:r + 8]
    return acc


def _glu_into(uc_ref, vpad, S):
    vpad[pl.ds(0, CONV_HALO), :] = jnp.zeros((CONV_HALO, CONV_W), F32)
    vpad[pl.ds(S + CONV_HALO, CONV_HALO), :] = jnp.zeros((CONV_HALO, CONV_W), F32)

    def glu(i, carry):
        r0 = pl.multiple_of(i * GLU_CHUNK, GLU_CHUNK)
        a = uc_ref[0, pl.ds(r0, GLU_CHUNK), 0:CONV_W]
        gt = uc_ref[0, pl.ds(r0, GLU_CHUNK), CONV_W:2 * CONV_W]
        vpad[pl.ds(pl.multiple_of(r0 + CONV_HALO, CONV_HALO), GLU_CHUNK), :] = a * _sigmoid(gt)
        return carry
    lax.fori_loop(0, S // GLU_CHUNK, glu, 0)


def _layer_norm(z, vec_ref):
    xc = z - jnp.mean(z, axis=-1, keepdims=True)
    rstd = lax.rsqrt(jnp.mean(xc * xc, axis=-1, keepdims=True) + EPS)
    xn = xc * rstd
    return xn, rstd, xn * vec_ref[1:2, :] + vec_ref[2:3, :]


def _conv_fwd(uc, dw_w, vec):
    NB, S, _ = uc.shape

    def body(uc_ref, w_ref, vec_ref, o_ref, z_ref, vpad, copies):
        _glu_into(uc_ref, vpad, S)

        def conv(i, carry):
            r0 = pl.multiple_of(i * CONV_CHUNK, CONV_CHUNK)
            win = vpad[pl.ds(r0, CONV_WIN), :]
            z = _conv_taps(_shifted(win, CONV_HALO - (CONV_K - 1), copies), w_ref, lambda a, k: 8 * a + k) + vec_ref[0:1, :]
            z_ref[0, pl.ds(r0, CONV_CHUNK), :] = z
            _, _, yln = _layer_norm(z, vec_ref)
            o_ref[0, pl.ds(r0, CONV_CHUNK), :] = (yln * _sigmoid(yln)).astype(BF16)
            return carry
        lax.fori_loop(0, S // CONV_CHUNK, conv, 0, unroll=4)

    seq = pl.BlockSpec((1, S, CONV_W), lambda b: (b, 0, 0))
    return pl.pallas_call(
        body, grid=(NB,), name="conv_fwd",
        in_specs=[pl.BlockSpec((1, S, 2 * CONV_W), lambda b: (b, 0, 0)), _const((CONV_K, CONV_W)), _const((8, CONV_W))],
        out_specs=[seq, seq],
        out_shape=[jax.ShapeDtypeStruct((NB, S, CONV_W), BF16), jax.ShapeDtypeStruct((NB, S, CONV_W), F32)],
        scratch_shapes=[pltpu.VMEM((S + 2 * CONV_HALO, CONV_W), F32), pltpu.VMEM((8, CONV_CHUNK + 24, CONV_W), F32)],
        compiler_params=_params("parallel"),
    )(uc, dw_w, vec)


def _conv_bwd(uc, z, dcact, dw_w, vec, ride=None):
    NB, S, _ = uc.shape
    n_chunks = S // CONV_CHUNK

    def body(uc_ref, z_ref, dc_ref, w_ref, vec_ref, duc_ref, dw_ref, dvec_ref, vpad, dzpad, dw8, dvec8, copies):
        @pl.when(pl.program_id(0) == 0)
        def _():
            dw8[...] = jnp.zeros_like(dw8)
            dvec8[...] = jnp.zeros_like(dvec8)
        _glu_into(uc_ref, vpad, S)
        dzpad[pl.ds(S, 2 * CONV_HALO), :] = jnp.zeros((2 * CONV_HALO, CONV_W), F32)

        def norm_bwd(i, carry):
            r0 = pl.multiple_of(i * CONV_CHUNK, CONV_CHUNK)
            xn, rstd, yln = _layer_norm(z_ref[0, pl.ds(r0, CONV_CHUNK), :], vec_ref)
            sg = _sigmoid(yln)
            dyln = dc_ref[0, pl.ds(r0, CONV_CHUNK), :] * (sg * (1.0 + yln * (1.0 - sg)))
            dxn = dyln * vec_ref[1:2, :]
            dz = rstd * (dxn - jnp.mean(dxn, axis=-1, keepdims=True) - xn * jnp.mean(dxn * xn, axis=-1, keepdims=True))
            dzpad[pl.ds(r0, CONV_CHUNK), :] = dz
            dvec8[0] += _fold8(dz)
            dvec8[1] += _fold8(dyln * xn)
            dvec8[2] += _fold8(dyln)
            return carry
        lax.fori_loop(0, n_chunks, norm_bwd, 0, unroll=4)

        def taps_bwd(i, carry):
            r0 = pl.multiple_of(i * CONV_CHUNK, CONV_CHUNK)
            dzwin = dzpad[pl.ds(r0, CONV_WIN), :]
            dv = _conv_taps(_shifted(dzwin, 0, copies), w_ref, lambda a, k: CONV_K - 1 - 8 * a - k)
            dz = dzwin[0:CONV_CHUNK]
            vwin = vpad[pl.ds(r0, CONV_WIN), :]
            for j, rows in _tap_slices(_shifted(vwin, CONV_HALO - (CONV_K - 1), copies), lambda a, k: 8 * a + k):
                dw8[j] += _fold8(dz * rows)
            a = uc_ref[0, pl.ds(r0, CONV_CHUNK), 0:CONV_W]
            sg = _sigmoid(uc_ref[0, pl.ds(r0, CONV_CHUNK), CONV_W:2 * CONV_W])
            duc_ref[0, pl.ds(r0, CONV_CHUNK), 0:CONV_W] = (dv * sg).astype(BF16)
            duc_ref[0, pl.ds(r0, CONV_CHUNK), CONV_W:2 * CONV_W] = (dv * a * sg * (1.0 - sg)).astype(BF16)
            return carry
        lax.fori_loop(0, n_chunks, taps_bwd, 0, unroll=2)

        @pl.when(pl.program_id(0) == NB - 1)
        def _():
            dw_ref[...] = jnp.zeros_like(dw_ref)
            dvec_ref[...] = jnp.zeros_like(dvec_ref)
            for j in range(CONV_K):
                dw_ref[j:j + 1, :] = jnp.sum(dw8[j], axis=0, keepdims=True)
            for j in range(3):
                dvec_ref[j:j + 1, :] = jnp.sum(dvec8[j], axis=0, keepdims=True)

    return _pallas(
        body, "conv_bwd", (NB,),
        [pl.BlockSpec((1, S, 2 * CONV_W), lambda b: (b, 0, 0)), pl.BlockSpec((1, S, CONV_W), lambda b: (b, 0, 0)),
         pl.BlockSpec((1, S, CONV_W), lambda b: (b, 0, 0)), _const((CONV_K, CONV_W)), _const((8, CONV_W))],
        [pl.BlockSpec((1, S, 2 * CONV_W), lambda b: (b, 0, 0)), pl.BlockSpec((32, CONV_W), lambda b: (0, 0)),
         pl.BlockSpec((8, CONV_W), lambda b: (0, 0))],
        [jax.ShapeDtypeStruct((NB, S, 2 * CONV_W), BF16), jax.ShapeDtypeStruct((32, CONV_W), F32),
         jax.ShapeDtypeStruct((8, CONV_W), F32)],
        (uc, z, dcact, dw_w, vec),
        scratch_shapes=[pltpu.VMEM((S + 2 * CONV_HALO, CONV_W), F32), pltpu.VMEM((S + 2 * CONV_HALO, CONV_W), F32),
                        pltpu.VMEM((CONV_K, 8, CONV_W), F32), pltpu.VMEM((3, 8, CONV_W), F32),
                        pltpu.VMEM((8, CONV_CHUNK + 24, CONV_W), F32)],
        sem=("arbitrary",), ride=ride)


def _rel_index_of_column(cols):
    offset = jnp.where(cols < KWIN, cols, cols - DS_LANES)
    return jnp.clip(KPAD - offset, -(CHUNK - 1), MAX_REL) + (CHUNK - 1)


def _bias_table(rel_bias, ride=None):
    def body(rb_ref, o_ref, by_offset, first8):
        ridx = _rel_index_of_column(lax.broadcasted_iota(jnp.int32, (1, DS_LANES), 1))
        onehot = (ridx == lax.broadcasted_iota(jnp.int32, (N_REL, 1), 0)).astype(F32)
        by_offset[...] = jnp.dot(rb_ref[...], onehot, preferred_element_type=F32, precision=lax.Precision.HIGHEST)
        sub = lax.broadcasted_iota(jnp.int32, (8, 1), 0)
        kchunk = lax.broadcasted_iota(jnp.int32, (1, KWIN), 1) // CHUNK
        for head in range(ATT_HEADS):
            base = jnp.broadcast_to(by_offset[head:head + 1, :], (8, DS_LANES))
            rows = base
            for s in range(1, 8):
                rows = jnp.where(sub == s, pltpu.roll(base, s, 1), rows)
            first8[head] = rows

        def rows8(q8, carry):
            qchunk = (q8 * 8 + sub) // CHUNK
            band = (kchunk >= qchunk) & (kchunk <= qchunk + LEFT_CHUNKS)
            for head in range(ATT_HEADS):
                tile = pltpu.roll(first8[head], q8 * 8, 1)[:, 0:KWIN]
                o_ref[head, pl.ds(pl.multiple_of(q8 * 8, 8), 8), :] = jnp.where(band, tile, MASK_VALUE)
            return carry
        lax.fori_loop(0, QB // 8, rows8, 0)

    outs, rode = _pallas(
        body, "bias_table", (1,), [pl.BlockSpec((ATT_HEADS, N_REL), lambda i: (0, 0))],
        [pl.BlockSpec((ATT_HEADS, QB, KWIN), lambda i: (0, 0, 0))], [jax.ShapeDtypeStruct((ATT_HEADS, QB, KWIN), F32)], (rel_bias,),
        scratch_shapes=[pltpu.VMEM((ATT_HEADS, DS_LANES), F32), pltpu.VMEM((ATT_HEADS, 8, DS_LANES), F32)],
        sem=("arbitrary",), ride=ride)
    return outs[0], rode


def _load_keys(i, k_ref, v_ref, kpad, vpad, S):
    @pl.when(i == 0)
    def _():
        kpad[pl.ds(0, KPAD), :] = jnp.zeros((KPAD, ATT_W), BF16)
        vpad[pl.ds(0, KPAD), :] = jnp.zeros((KPAD, ATT_W), BF16)
        kpad[pl.ds(KPAD, S), :] = k_ref[0]
        vpad[pl.ds(KPAD, S), :] = v_ref[0]


def _att_scores(q2s, k2, tab_ref, head, in_head, in_seq):
    qm = jnp.where(in_head, q2s, jnp.zeros_like(q2s))
    return jnp.where(in_seq, _dot_nt(qm, k2) + tab_ref[head], MASK_VALUE)


def _scaled(q2):
    return q2 * jnp.asarray(ATT_HD ** -0.5, q2.dtype)


def _att_fwd(qkv, tab, ride=None):
    NB, S, _ = qkv.shape

    def body(q_ref, k_ref, v_ref, tab_ref, o_ref, lse_ref, kpad, vpad):
        i = pl.program_id(1)
        _load_keys(i, k_ref, v_ref, kpad, vpad, S)
        koff = pl.multiple_of(i * QB, QB)
        lane = lax.broadcasted_iota(jnp.int32, (1, 128), 1)
        in_seq = (lax.broadcasted_iota(jnp.int32, (1, KWIN), 1) + i * QB) >= KPAD
        lse = jnp.zeros((QB, 128), F32)
        for pair in range(ATT_HEADS // 2):
            cols = slice(pair * 128, (pair + 1) * 128)
            q2s = _scaled(q_ref[0, :, cols])
            k2 = kpad[pl.ds(koff, KWIN), cols]
            v2 = vpad[pl.ds(koff, KWIN), cols]
            o2 = jnp.zeros((QB, 128), F32)
            for hh in range(2):
                head = 2 * pair + hh
                in_head = (lane // ATT_HD) == hh
                s = _att_scores(q2s, k2, tab_ref, head, in_head, in_seq)
                m = jnp.max(s, axis=-1, keepdims=True)
                e = jnp.exp(s - m)
                l = jnp.sum(e, axis=-1, keepdims=True)
                o2 = jnp.where(in_head, _dot(e.astype(BF16), v2) * (1.0 / l), o2)
                lse = jnp.where(lane == head, m + jnp.log(l), lse)
            o_ref[0, :, cols] = o2.astype(BF16)
        lse_ref[0] = lse

    seq = lambda col: pl.BlockSpec((1, S, ATT_W), lambda b, i: (b, 0, col), pipeline_mode=pl.Buffered(1))
    outs, rode = _pallas(
        body, "att_fwd", (NB, S // QB),
        [pl.BlockSpec((1, QB, ATT_W), lambda b, i: (b, i, 0)), seq(1), seq(2), _const((ATT_HEADS, QB, KWIN))],
        [pl.BlockSpec((1, QB, ATT_W), lambda b, i: (b, i, 0)), pl.BlockSpec((1, QB, 128), lambda b, i: (b, i, 0))],
        [jax.ShapeDtypeStruct((NB, S, ATT_W), BF16), jax.ShapeDtypeStruct((NB, S, 128), F32)],
        (qkv, qkv, qkv, tab),
        scratch_shapes=[pltpu.VMEM((S + KPAD, ATT_W), BF16), pltpu.VMEM((S + KPAD, ATT_W), BF16)],
        sem=("arbitrary", "arbitrary"), ride=ride)
    return outs[0], outs[1], rode


def _att_bwd(qkv, o, lse, do, tab, ride=None):
    NB, S, _ = qkv.shape
    nq = S // QB

    def body(q_ref, k_ref, v_ref, o_ref, lse_ref, do_ref, tab_ref, dqkv_ref, ds_hbm, kpad, vpad, dkpad, dvpad, ds_acc, ds_sem):
        b, i = pl.program_id(0), pl.program_id(1)
        _load_keys(i, k_ref, v_ref, kpad, vpad, S)

        @pl.when(i == 0)
        def _():
            dkpad[...] = jnp.zeros_like(dkpad)
            dvpad[...] = jnp.zeros_like(dvpad)

        @pl.when((i == 0) & (b == 0))
        def _():
            ds_acc[...] = jnp.zeros_like(ds_acc)

        koff = pl.multiple_of(i * QB, QB)
        lane = lax.broadcasted_iota(jnp.int32, (1, 128), 1)
        in_seq = (lax.broadcasted_iota(jnp.int32, (1, KWIN), 1) + i * QB) >= KPAD
        for pair in range(ATT_HEADS // 2):
            cols = slice(pair * 128, (pair + 1) * 128)
            q2s = _scaled(q_ref[0, :, cols])
            do2 = do_ref[0, :, cols]
            k2 = kpad[pl.ds(koff, KWIN), cols]
            v2 = vpad[pl.ds(koff, KWIN), cols]
            do_o = do2.astype(F32) * o_ref[0, :, cols].astype(F32)
            dq2 = jnp.zeros((QB, 128), F32)
            dk2 = jnp.zeros((KWIN, 128), F32)
            dv2 = jnp.zeros((KWIN, 128), F32)
            for hh in range(2):
                head = 2 * pair + hh
                in_head = (lane // ATT_HD) == hh
                p = jnp.exp(_att_scores(q2s, k2, tab_ref, head, in_head, in_seq) - lse_ref[0, :, head:head + 1])
                row_term = jnp.sum(jnp.where(in_head, do_o, 0.0), axis=-1, keepdims=True)
                dom = jnp.where(in_head, do2, jnp.zeros_like(do2))
                ds = p * (_dot_nt(dom, v2) - row_term)
                ds_acc[head] += ds
                dsb = ds.astype(BF16)
                dq2 = jnp.where(in_head, _dot(dsb, k2), dq2)
                dk2 = jnp.where(in_head, _dot_tn(dsb, q2s), dk2)
                dv2 = jnp.where(in_head, _dot_tn(p.astype(BF16), do2), dv2)
            dqkv_ref[0, pl.ds(koff, QB), cols] = (dq2 * (ATT_HD ** -0.5)).astype(BF16)
            dkpad[pl.ds(koff, KWIN), cols] += dk2
            dvpad[pl.ds(koff, KWIN), cols] += dv2

        @pl.when(i == nq - 1)
        def _():
            dqkv_ref[0, :, ATT_W:2 * ATT_W] = dkpad[pl.ds(KPAD, S), :].astype(BF16)
            dqkv_ref[0, :, 2 * ATT_W:3 * ATT_W] = dvpad[pl.ds(KPAD, S), :].astype(BF16)

        @pl.when((i == nq - 1) & (b == NB - 1))
        def _():
            out = pltpu.make_async_copy(ds_acc, ds_hbm, ds_sem)
            out.start()
            out.wait()

    seq = lambda col: pl.BlockSpec((1, S, ATT_W), lambda b, i: (b, 0, col), pipeline_mode=pl.Buffered(1))
    rows = pl.BlockSpec((1, QB, ATT_W), lambda b, i: (b, i, 0))
    return _pallas(
        body, "att_bwd", (NB, nq),
        [rows, seq(1), seq(2), rows, pl.BlockSpec((1, QB, 128), lambda b, i: (b, i, 0)), rows, _const((ATT_HEADS, QB, KWIN))],
        [pl.BlockSpec((1, S, 3 * ATT_W), lambda b, i: (b, 0, 0)), ANY],
        [jax.ShapeDtypeStruct((NB, S, 3 * ATT_W), BF16), jax.ShapeDtypeStruct((ATT_HEADS, QB, KWIN), F32)],
        (qkv, qkv, qkv, o, lse, do, tab),
        scratch_shapes=[pltpu.VMEM((S + KPAD, ATT_W), BF16), pltpu.VMEM((S + KPAD, ATT_W), BF16),
                        pltpu.VMEM((S + KPAD, ATT_W), F32), pltpu.VMEM((S + KPAD, ATT_W), F32),
                        pltpu.VMEM((ATT_HEADS, QB, KWIN), F32), pltpu.SemaphoreType.DMA],
        sem=("arbitrary", "arbitrary"), ride=ride)


def _rel_bias_grad(ds):
    def body(ds_ref, o_ref):
        sub = lax.broadcasted_iota(jnp.int32, (8, 1), 0)
        ridx = _rel_index_of_column(lax.broadcasted_iota(jnp.int32, (DS_LANES, 1), 0))
        onehot = (ridx == lax.broadcasted_iota(jnp.int32, (1, N_REL), 1)).astype(F32)
        def rows8(q8, accs):
            shift = lax.rem(DS_LANES - q8 * 8, DS_LANES)
            out = []
            for head in range(ATT_HEADS):
                tile = ds_ref[head, pl.ds(pl.multiple_of(q8 * 8, 8), 8), :]
                tile = jnp.concatenate([tile, jnp.zeros((8, DS_LANES - KWIN), F32)], axis=1)
                out.append(accs[head] + pltpu.roll(tile, shift, 1))
            return tuple(out)
        accs = lax.fori_loop(0, QB // 8, rows8, tuple(jnp.zeros((8, DS_LANES), F32) for _ in range(ATT_HEADS)))
        for head in range(ATT_HEADS):
            acc = accs[head]
            diag = jnp.zeros((8, DS_LANES), F32)
            for s in range(8):
                shifted = acc if s == 0 else pltpu.roll(acc, DS_LANES - s, 1)
                diag = jnp.where(sub == s, shifted, diag)
            z = jnp.sum(diag, axis=0, keepdims=True)
            o_ref[head:head + 1, :] = jnp.dot(z, onehot, preferred_element_type=F32, precision=lax.Precision.HIGHEST)

    return pl.pallas_call(body, out_shape=jax.ShapeDtypeStruct((ATT_HEADS, N_REL), F32), name="rel_bias_grad",
                          compiler_params=_params())(ds)


def _memkv_fwd(mem, g, w_kv, tm):
    R = mem.shape[0]
    tm = min(tm, R)

    def body(m_ref, g_ref, w_ref, h_ref, kv_ref):
        xhat, _ = _rms_stats(m_ref[...])
        h = (xhat * g_ref[...]).astype(BF16)
        h_ref[...] = h
        kv_ref[...] = _dot(h, w_ref[...]).astype(BF16)

    row = pl.BlockSpec((tm, D), lambda t: (t, 0))
    return pl.pallas_call(
        body, grid=(R // tm,), name="memkv_fwd", in_specs=[row, _const((1, D)), _const((D, 2 * MEM_W))], out_specs=[row, row],
        out_shape=[jax.ShapeDtypeStruct((R, D), BF16), jax.ShapeDtypeStruct((R, 2 * MEM_W), BF16)],
        compiler_params=_params("parallel"),
    )(mem, g, w_kv)


def _memkv_bwd(mem, dkv, w_kv, tm):
    R = mem.shape[0]
    tm = min(tm, R)

    def body(m_ref, dkv_ref, w_ref, dg_ref):
        xhat, _ = _rms_stats(m_ref[...])
        dh = _dot_nt(dkv_ref[...].astype(BF16), w_ref[...])

        @pl.when(pl.program_id(0) == 0)
        def _():
            dg_ref[...] = jnp.zeros_like(dg_ref)
        dg_ref[...] += jnp.sum(dh * xhat, axis=0, keepdims=True)

    row = pl.BlockSpec((tm, D), lambda t: (t, 0))
    return pl.pallas_call(
        body, grid=(R // tm,), name="memkv_bwd", in_specs=[row, row, _const((D, 2 * MEM_W))],
        out_specs=pl.BlockSpec((1, D), lambda t: (0, 0)), out_shape=jax.ShapeDtypeStruct((1, D), F32),
        compiler_params=_params("arbitrary"),
    )(mem, dkv, w_kv)


def _mematt_fwd(mq, kv, tq):
    NB, S, _ = mq.shape
    M = kv.shape[1]

    def body(q_ref, kv_ref, o_ref, lse_ref):
        lane = lax.broadcasted_iota(jnp.int32, (1, 128), 1)
        lse = jnp.zeros((tq, 128), F32)
        for h in range(MEM_HEADS):
            cols = slice(h * MEM_HD, (h + 1) * MEM_HD)
            s = _dot_nt(q_ref[0, :, cols], kv_ref[0, :, cols]) * (MEM_HD ** -0.5)
            m = jnp.max(s, axis=-1, keepdims=True)
            e = jnp.exp(s - m)
            l = jnp.sum(e, axis=-1, keepdims=True)
            o = _dot(e.astype(BF16), kv_ref[0, :, MEM_W + h * MEM_HD:MEM_W + (h + 1) * MEM_HD]) * (1.0 / l)
            o_ref[0, :, cols] = o.astype(BF16)
            lse = jnp.where(lane == h, m + jnp.log(l), lse)
        lse_ref[0] = lse

    return pl.pallas_call(
        body, grid=(NB, S // tq), name="mematt_fwd",
        in_specs=[pl.BlockSpec((1, tq, MEM_W), lambda b, i: (b, i, 0)), pl.BlockSpec((1, M, 2 * MEM_W), lambda b, i: (b, 0, 0))],
        out_specs=[pl.BlockSpec((1, tq, MEM_W), lambda b, i: (b, i, 0)), pl.BlockSpec((1, tq, 128), lambda b, i: (b, i, 0))],
        out_shape=[jax.ShapeDtypeStruct((NB, S, MEM_W), BF16), jax.ShapeDtypeStruct((NB, S, 128), F32)],
        compiler_params=_params("parallel", "parallel"),
    )(mq, kv)


def _mematt_bwd(mq, kv, o, lse, do, tq):
    NB, S, _ = mq.shape
    M = kv.shape[1]

    def body(q_ref, kv_ref, o_ref, lse_ref, do_ref, dq_ref, dkv_ref):
        @pl.when(pl.program_id(1) == 0)
        def _():
            dkv_ref[...] = jnp.zeros_like(dkv_ref)
        for h in range(MEM_HEADS):
            cols = slice(h * MEM_HD, (h + 1) * MEM_HD)
            vcols = slice(MEM_W + h * MEM_HD, MEM_W + (h + 1) * MEM_HD)
            qh, kh, vh, doh = q_ref[0, :, cols], kv_ref[0, :, cols], kv_ref[0, :, vcols], do_ref[0, :, cols]
            p = jnp.exp(_dot_nt(qh, kh) * (MEM_HD ** -0.5) - lse_ref[0, :, h:h + 1])
            row_term = jnp.sum(doh.astype(F32) * o_ref[0, :, cols].astype(F32), axis=-1, keepdims=True)
            ds = p * (_dot_nt(doh, vh) - row_term)
            dss = (ds * (MEM_HD ** -0.5)).astype(BF16)
            dq_ref[0, :, cols] = _dot(dss, kh).astype(BF16)
            dkv_ref[0, :, cols] += _dot_tn(dss, qh)
            dkv_ref[0, :, vcols] += _dot_tn(p.astype(BF16), doh)

    qspec = pl.BlockSpec((1, tq, MEM_W), lambda b, i: (b, i, 0))
    kvspec = pl.BlockSpec((1, M, 2 * MEM_W), lambda b, i: (b, 0, 0))
    return pl.pallas_call(
        body, grid=(NB, S // tq), name="mematt_bwd",
        in_specs=[qspec, kvspec, qspec, pl.BlockSpec((1, tq, 128), lambda b, i: (b, i, 0)), qspec], out_specs=[qspec, kvspec],
        out_shape=[jax.ShapeDtypeStruct((NB, S, MEM_W), BF16), jax.ShapeDtypeStruct((NB, M, 2 * MEM_W), F32)],
        compiler_params=_params("arbitrary", "arbitrary"),
    )(mq, kv, o, lse, do)


def _branch(j, in_ref, w_ref, gl_ref, bg_ref):
    y = _dot(in_ref[...], w_ref[...])
    gate = _sigmoid(gl_ref[:, j * D:(j + 1) * D].astype(F32) + bg_ref[:, j * D:(j + 1) * D])
    return y, gate


def _combine_fwd(x, cact, oatt, omem, gl, bg, wpw, wo, wmo, wout, tm):
    T = x.shape[0]

    def body(x_ref, c_ref, a_ref, m_ref, gl_ref, bg_ref, wpw_ref, wo_ref, wmo_ref, wout_ref, xo_ref, y_ref):
        y = None
        for j, (in_ref, w_ref) in enumerate(((c_ref, wpw_ref), (a_ref, wo_ref), (m_ref, wmo_ref))):
            yj, gate = _branch(j, in_ref, w_ref, gl_ref, bg_ref)
            y = gate * yj if y is None else y + gate * yj
        y = y.astype(BF16)
        y_ref[...] = y
        xo_ref[...] = x_ref[...] + _dot(y, wout_ref[...])

    row = lambda w: pl.BlockSpec((tm, w), lambda t: (t, 0))
    wbr = _const((512, D))
    return pl.pallas_call(
        body, grid=(T // tm,), name="combine_fwd",
        in_specs=[row(D), row(512), row(512), row(512), row(3 * D), _const((1, 3 * D)), wbr, wbr, wbr, _const((D, D))],
        out_specs=[row(D), row(D)],
        out_shape=[jax.ShapeDtypeStruct((T, D), F32), jax.ShapeDtypeStruct((T, D), BF16)],
        compiler_params=_params("parallel"),
    )(x, cact, oatt, omem, gl, bg, wpw, wo, wmo, wout)


def _combine_bwd(dx, cact, oatt, omem, gl, bg, wpw, wo, wmo, wout, tm, ride=None):
    T = dx.shape[0]

    def body(dx_ref, c_ref, a_ref, m_ref, gl_ref, bg_ref, wpw_ref, wo_ref, wmo_ref, wout_ref,
             dgl_ref, dc_ref, da_ref, dm_ref, dyc_ref, dya_ref, dym_ref, dbg_ref):
        dy = _dot_nt(dx_ref[...].astype(BF16), wout_ref[...])

        @pl.when(pl.program_id(0) == 0)
        def _():
            dbg_ref[...] = jnp.zeros_like(dbg_ref)
        branches = ((c_ref, wpw_ref, dyc_ref, dc_ref), (a_ref, wo_ref, dya_ref, da_ref), (m_ref, wmo_ref, dym_ref, dm_ref))
        for j, (in_ref, w_ref, dyb_ref, din_ref) in enumerate(branches):
            yj, gate = _branch(j, in_ref, w_ref, gl_ref, bg_ref)
            dyg = dy * gate
            dlogit = dyg * yj * (1.0 - gate)
            dgl_ref[:, j * D:(j + 1) * D] = dlogit.astype(BF16)
            dbg_ref[:, j * D:(j + 1) * D] += jnp.sum(dlogit, axis=0, keepdims=True)
            dyb = dyg.astype(BF16)
            dyb_ref[...] = dyb
            din_ref[...] = _dot_nt(dyb, w_ref[...]).astype(din_ref.dtype)

    row = lambda w: pl.BlockSpec((tm, w), lambda t: (t, 0))
    wbr = _const((512, D))
    sds = jax.ShapeDtypeStruct
    return _pallas(
        body, "combine_bwd", (T // tm,),
        [row(D), row(512), row(512), row(512), row(3 * D), _const((1, 3 * D)), wbr, wbr, wbr, _const((D, D))],
        [row(3 * D), row(512), row(512), row(512), row(D), row(D), row(D), pl.BlockSpec((1, 3 * D), lambda t: (0, 0))],
        [sds((T, 3 * D), BF16), sds((T, 512), F32), sds((T, 512), BF16), sds((T, 512), BF16),
         sds((T, D), BF16), sds((T, D), BF16), sds((T, D), BF16), sds((1, 3 * D), F32)],
        (dx, cact, oatt, omem, gl, bg, wpw, wo, wmo, wout), sem=("arbitrary",), ride=ride)


def _peer(x, y, c, rel):
    rx, ry, rc = (rel >> 2) & 1, (rel >> 1) & 1, rel & 1
    return ((1 - x) if rx else x, (1 - y) if ry else y, (1 - c) if rc else c)


def _all_sum_small(parts):
    n = len(parts)

    def body(*refs):
        p_refs, o_refs, slots = refs[:n], refs[n:2 * n], refs[2 * n:3 * n]
        send_sems, recv_sems = refs[3 * n:]
        x, y, c = _my_coords()
        me = _dev_index(x, y, c)

        def copy(i, rel, arrival):
            peer = _peer(x, y, c, rel)
            return pltpu.make_async_remote_copy(
                src_ref=p_refs[i], dst_ref=slots[i].at[_dev_index(*peer) if arrival else me],
                send_sem=send_sems.at[i, rel - 1], recv_sem=recv_sems.at[i, rel - 1], device_id=peer, device_id_type=MESH)

        for i in range(n):
            slots[i][me] = p_refs[i][...]
        for rel in range(1, NDEV):
            for i in range(n):
                copy(i, rel, False).start()
        for rel in range(1, NDEV):
            for i in range(n):
                copy(i, rel, True).wait_recv()
        for rel in range(1, NDEV):
            for i in range(n):
                copy(i, rel, False).wait_send()
        for i in range(n):
            total = slots[i][0]
            for d in range(1, NDEV):
                total = total + slots[i][d]
            o_refs[i][...] = total

    vmem = pl.BlockSpec(memory_space=pltpu.VMEM)
    return pl.pallas_call(
        body, out_shape=[jax.ShapeDtypeStruct(p.shape, F32) for p in parts], name="all_sum_small",
        in_specs=[vmem] * n, out_specs=[vmem] * n,
        scratch_shapes=[pltpu.VMEM((NDEV,) + p.shape, F32) for p in parts]
        + [pltpu.SemaphoreType.DMA((n, NDEV - 1)), pltpu.SemaphoreType.DMA((n, NDEV - 1))],
        compiler_params=pltpu.CompilerParams(has_side_effects=True),
    )(*parts)


HBM = pl.BlockSpec(memory_space=pltpu.HBM)
SEM = pl.BlockSpec(memory_space=pltpu.SEMAPHORE)


def _own_block(g, kind, m, tag):
    def body(g_ref, land_ref, staged, sem):
        me = _dev_index(*_my_coords())
        for cp in (pltpu.make_async_copy(_window(g_ref, kind, m, me), staged, sem),
                   pltpu.make_async_copy(staged, land_ref.at[me], sem)):
            cp.start()
            cp.wait()

    block = (m, g.shape[1]) if kind == 'row' else (g.shape[0], m)
    return pl.pallas_call(body, in_specs=[ANY], out_specs=ANY, out_shape=jax.ShapeDtypeStruct((NDEV,) + block, g.dtype),
                          scratch_shapes=[pltpu.VMEM(block, g.dtype), pltpu.SemaphoreType.DMA], name="own_block_" + tag)(g)


def _scatter_start(g, land, kind, m, tag):
    def body(g_ref, land_ref, send_sems, recv_sems, g_thru, land_thru, token):
        x, y, c = _my_coords()
        me = _dev_index(x, y, c)
        for rel in range(1, NDEV):
            peer = _peer(x, y, c, rel)
            pltpu.make_async_remote_copy(src_ref=_window(g_ref, kind, m, _dev_index(*peer)), dst_ref=land_ref.at[me],
                                         send_sem=send_sems.at[rel - 1], recv_sem=recv_sems.at[rel - 1],
                                         device_id=peer, device_id_type=MESH).start()
        token[...] = jnp.zeros_like(token)

    return pl.pallas_call(
        body, name="scatter_start_" + tag,
        out_shape=(pltpu.SemaphoreType.DMA((NDEV - 1,)), pltpu.SemaphoreType.DMA((NDEV - 1,)), pltpu.HBM(g.shape, g.dtype),
                   pltpu.HBM(land.shape, land.dtype), jax.ShapeDtypeStruct((8, 128), F32)),
        in_specs=(HBM, HBM), out_specs=(SEM, SEM, HBM, HBM, pl.BlockSpec(memory_space=pltpu.VMEM)),
        input_output_aliases={0: 2, 1: 3},
        compiler_params=pltpu.CompilerParams(has_side_effects=pltpu.SideEffectType.DATAFLOW_SIDE_EFFECTING),
    )(pltpu.with_memory_space_constraint(g, pltpu.HBM), pltpu.with_memory_space_constraint(land, pltpu.HBM))


def _scatter_wait(send_sems, recv_sems, g_thru, land_thru, after, kind, m, tag):
    n_after = len(after)

    def body(*refs):
        g_ref, land_ref, send_sems, recv_sems = refs[:4]
        x, y, c = _my_coords()
        me = _dev_index(x, y, c)
        for rel in range(1, NDEV):
            peer = _peer(x, y, c, rel)
            dev = _dev_index(*peer)
            cp = pltpu.make_async_remote_copy(src_ref=_window(g_ref, kind, m, me), dst_ref=land_ref.at[dev],
                                              send_sem=send_sems.at[rel - 1], recv_sem=recv_sems.at[rel - 1],
                                              device_id=peer, device_id_type=MESH)
            cp.wait_send()
            cp.wait_recv()

    return pl.pallas_call(
        body, name="scatter_wait_" + tag,
        out_shape=(pltpu.HBM(g_thru.shape, g_thru.dtype), pltpu.HBM(land_thru.shape, land_thru.dtype)),
        in_specs=(HBM, HBM, SEM, SEM) + (ANY,) * n_after, out_specs=(HBM, HBM), input_output_aliases={0: 0, 1: 1},
        compiler_params=pltpu.CompilerParams(has_side_effects=pltpu.SideEffectType.DATAFLOW_SIDE_EFFECTING),
    )(g_thru, land_thru, send_sems, recv_sems, *after)[1]


def _adamw_math(w, g, m, v):
    m = ADAM_B1 * m + (1.0 - ADAM_B1) * g
    v = ADAM_B2 * v + (1.0 - ADAM_B2) * (g * g)
    m_hat = m / (1.0 - ADAM_B1 ** ADAM_STEP)
    v_hat = v / (1.0 - ADAM_B2 ** ADAM_STEP)
    delta = -ADAM_LR * (m_hat / (jnp.sqrt(v_hat) + ADAM_EPS) + ADAM_WD * w)
    return delta, m, v


def _sum_adamw(parts, w, m, v, name, after=None):
    R, C = w.shape
    n_parts = len(parts)
    cg = C // n_parts
    tr = max(t for t in range(8, 257, 8) if R % t == 0)
    deps = [] if after is None else [after]

    def body(*refs):
        p_refs = refs[:n_parts]
        w_ref, m_ref, v_ref = refs[n_parts:n_parts + 3]
        g_ref, d_ref, mo_ref, vo_ref = refs[n_parts + 3 + len(deps):]
        for k, p_ref in enumerate(p_refs):
            @pl.when(pl.program_id(0) == k)
            def _():
                g = p_ref[0].astype(F32)
                for d in range(1, NDEV):
                    g = g + p_ref[d].astype(F32)
                g_ref[...] = g
                d_ref[...], mo_ref[...], vo_ref[...] = _adamw_math(w_ref[...], g, m_ref[...], v_ref[...])

    part = pl.BlockSpec((NDEV, tr, cg), lambda k, t: (0, t, 0))
    blk = pl.BlockSpec((tr, cg), lambda k, t: (t, k))
    return pl.pallas_call(
        body, grid=(n_parts, R // tr), name=name, in_specs=[part] * n_parts + [blk, blk, blk] + [ANY] * len(deps),
        out_specs=[blk] * 4, out_shape=[jax.ShapeDtypeStruct((R, C), F32)] * 4, compiler_params=_params("parallel", "parallel"),
    )(*parts, w, m, v, *deps)


def _adamw_small(ws, gs, ms, vs):
    n = len(ws)

    def body(*refs):
        w_refs, g_refs, m_refs, v_refs = (refs[k * n:(k + 1) * n] for k in range(4))
        d_refs, mo_refs, vo_refs = (refs[(4 + k) * n:(5 + k) * n] for k in range(3))
        for i in range(n):
            d_refs[i][...], mo_refs[i][...], vo_refs[i][...] = _adamw_math(w_refs[i][...], g_refs[i][...], m_refs[i][...], v_refs[i][...])

    shapes = [jax.ShapeDtypeStruct(a.shape, F32) for a in ws]
    outs = pl.pallas_call(body, out_shape=shapes * 3, name="adamw_small", compiler_params=_params())(*ws, *gs, *ms, *vs)
    return outs[:n], outs[n:2 * n], outs[2 * n:]


def kernel(x, mem, ffn1_norm, ffn1_w_up, ffn1_w_down, mix_norm, mem_norm, w_in, b_gate, conv_dw_w, conv_dw_b, conv_ln_g, conv_ln_b, conv_w_pw, att_rel_bias, att_w_o, mem_w_kv, mem_w_o, w_out, ffn2_norm, ffn2_w_up, ffn2_w_down, final_norm, loss_target, m_ffn1_norm, m_ffn1_w_up, m_ffn1_w_down, m_mix_norm, m_mem_norm, m_w_in, m_b_gate, m_conv_dw_w, m_conv_dw_b, m_conv_ln_g, m_conv_ln_b, m_conv_w_pw, m_att_rel_bias, m_att_w_o, m_mem_w_kv, m_mem_w_o, m_w_out, m_ffn2_norm, m_ffn2_w_up, m_ffn2_w_down, m_final_norm, v_ffn1_norm, v_ffn1_w_up, v_ffn1_w_down, v_mix_norm, v_mem_norm, v_w_in, v_b_gate, v_conv_dw_w, v_conv_dw_b, v_conv_ln_g, v_conv_ln_b, v_conv_w_pw, v_att_rel_bias, v_att_w_o, v_mem_w_kv, v_mem_w_o, v_w_out, v_ffn2_norm, v_ffn2_w_up, v_ffn2_w_down, v_final_norm):
    given = dict(locals())
    w = {n: given[n] for n in WEIGHTS}
    mom = {n: given["m_" + n] for n in WEIGHTS}
    var = {n: given["v_" + n] for n in WEIGHTS}

    NB, S, _ = x.shape
    T = NB * S
    ML = mem.shape[1]
    x0 = x.reshape(T, D)
    target = loss_target.reshape(T, D)
    mem2 = mem.reshape(NB * ML, D)

    def block(t, n):
        return jnp.transpose(t[0]) if n in TRANSPOSED else t[0]

    sh = dict(zip(BIG_ORDER, _cast_shards([block(w[n], n) for n in BIG_ORDER])))
    dw_t = jnp.transpose(conv_dw_w[0])

    def gather(names, extra=(), extra_kinds=()):
        return _gather_ride([sh[n] for n in names] + list(extra), [BIG[n] for n in names] + list(extra_kinds))

    W = {}
    names0 = ['ffn1_w_up', 'ffn1_w_down']
    tab, got = _bias_table(att_rel_bias[0], ride=gather(names0, [dw_t], [('row', dw_t.shape[0])]))
    W.update(zip(names0, got[:2]))
    dw_full = jnp.transpose(got[2])
    conv_vec = jnp.concatenate([conv_dw_b, conv_ln_g, conv_ln_b, jnp.zeros((5, CONV_W), F32)], axis=0)
    fin_g = final_norm.reshape(1, D)

    names1 = ['w_in', 'conv_w_pw', 'att_w_o', 'mem_w_kv', 'mem_w_o', 'w_out']
    (x1, ab1), got = _ffn_fwd(x0, ffn1_norm, W['ffn1_w_up'], W['ffn1_w_down'], TILE_FFN_FWD, "ffn1_fwd", ride=gather(names1))
    W.update(zip(names1, got))
    (uc, qkv, mq, gl, hmix), _ = _mix_fwd(x1, mix_norm, W['w_in'], TILE_TOKENS)
    uc3 = uc.reshape(NB, S, 2 * CONV_W)
    qkv3 = qkv.reshape(NB, S, 3 * ATT_W)
    mq3 = mq.reshape(NB, S, MEM_W)
    cact, conv_z = _conv_fwd(uc3, dw_full, conv_vec)
    cact = cact.reshape(T, CONV_W)
    names2 = ['ffn2_w_up', 'ffn2_w_down']
    oatt3, att_lse, got = _att_fwd(qkv3, tab, ride=gather(names2))
    W.update(zip(names2, got))
    oatt = oatt3.reshape(T, ATT_W)
    memh, kv = _memkv_fwd(mem2, mem_norm, W['mem_w_kv'], TILE_TOKENS)
    kv3 = kv.reshape(NB, ML, 2 * MEM_W)
    omem3, mem_lse = _mematt_fwd(mq3, kv3, TILE_TOKENS)
    omem = omem3.reshape(T, MEM_W)
    branch_w = (W['conv_w_pw'], W['att_w_o'], W['mem_w_o'], W['w_out'])
    x2, ymix = _combine_fwd(x1, cact, oatt, omem, gl, b_gate, *branch_w, TILE_TOKENS)
    dx3, ab2, loss_part, dg_final = _ffn_fwd_loss(x2, ffn2_norm, W['ffn2_w_up'], W['ffn2_w_down'], fin_g, target, TILE_FFN_FWD,
                                                  "ffn2_fwd_loss")

    def scatter(grads, names):
        return _scatter_ride(grads, [BIG[n] for n in names])

    G, P = {}, {}
    dx2, dab2, act2, h2, dg_ffn2 = _ffn_bwd(x2, dx3, ab2, ffn2_norm, W['ffn2_w_up'], W['ffn2_w_down'], TILE_FFN, "ffn2_bwd")
    g_up, _ = _tn_matmul(dab2, h2, 512, "grad_ffn2_w_up_a", tt=TILE_GRAD_TOKENS_WIDE, x_part=(0, 2), out_rows=2 * FF)
    G['ffn2_w_up'], _ = _tn_matmul(dab2, h2, 512, "grad_ffn2_w_up_b", tt=TILE_GRAD_TOKENS_WIDE, x_part=(1, 2), out_rows=2 * FF,
                                   prev=g_up)
    G['ffn2_w_down'], _ = _tn_matmul(act2, dx3, 512, "grad_ffn2_w_down", scale=0.5, tt=TILE_GRAD_TOKENS_WIDE)
    (dgl, dcact, doatt, domem, dyc, dya, dym, dbg), got = _combine_bwd(
        dx2, cact, oatt, omem, gl, b_gate, *branch_w, TILE_COMBINE, ride=scatter([G['ffn2_w_up']], ['ffn2_w_up']))
    P['ffn2_w_up'] = got
    G['w_out'], _ = _tn_matmul(ymix, dx2, D, "grad_w_out", tt=TILE_GRAD_TOKENS_WIDE)
    G['conv_w_pw'], _ = _tn_matmul(cact, dyc, D, "grad_conv_w_pw")
    G['att_w_o'], _ = _tn_matmul(oatt, dya, D, "grad_att_w_o")
    G['mem_w_o'], _ = _tn_matmul(omem, dym, D, "grad_mem_w_o")
    dmq3, dkv3 = _mematt_bwd(mq3, kv3, omem3, mem_lse, domem.reshape(NB, S, MEM_W), TILE_TOKENS)
    dkv = dkv3.reshape(NB * ML, 2 * MEM_W)
    dg_mem = _memkv_bwd(mem2, dkv, W['mem_w_kv'], TILE_TOKENS)
    G['mem_w_kv'], _ = _tn_matmul(memh, dkv, 512, "grad_mem_w_kv")
    names = ['ffn2_w_down', 'w_out', 'conv_w_pw', 'att_w_o', 'mem_w_o']
    (dqkv3, dscore), got = _att_bwd(qkv3, oatt3, att_lse, doatt.reshape(NB, S, ATT_W), tab,
                                    ride=scatter([G[n] for n in names], names))
    P.update((n, [p]) for n, p in zip(names, got))
    d_rel = _rel_bias_grad(dscore)
    (duc3, d_dw, d_cvec), got = _conv_bwd(uc3, conv_z, dcact.reshape(NB, S, CONV_W), dw_full, conv_vec,
                                          ride=scatter([G['mem_w_kv']], ['mem_w_kv']))
    P['mem_w_kv'] = got
    duc, dqkv, dmq = duc3.reshape(T, 2 * CONV_W), dqkv3.reshape(T, 3 * ATT_W), dmq3.reshape(T, MEM_W)
    g_in, _ = _tn_matmul(hmix, duc, 1024, "grad_w_in_conv", out_cols=IN_COLS, col_off=0)
    g_in, _ = _tn_matmul(hmix, dqkv, 512, "grad_w_in_qkv", out_cols=IN_COLS, col_off=1024, prev=g_in)
    g_in, _ = _tn_matmul(hmix, dmq, 512, "grad_w_in_mq", out_cols=IN_COLS, col_off=2560, prev=g_in)
    G['w_in'], _ = _tn_matmul(hmix, dgl, 1024, "grad_w_in_gate", out_cols=IN_COLS, col_off=3072, prev=g_in)
    def start_scatter(g, name, tag):
        kind = BIG[name]
        return _scatter_start(g, _own_block(g, *kind, tag), *kind, tag) + (kind, tag)

    def wait_scatter(started, after):
        send_sems, recv_sems, g_thru, land_thru, _, kind, tag = started
        return _scatter_wait(send_sems, recv_sems, g_thru, land_thru, after, *kind, tag)

    ex_in = start_scatter(G['w_in'], 'w_in', "w_in")
    (dx1, dg_mix), _ = _mix_bwd(x1, dx2, duc, dqkv, dmq, dgl, mix_norm, W['w_in'], TILE_TOKENS, after=ex_in[4])
    dx0, dab1, act1, h1, dg_ffn1 = _ffn_bwd(x0, dx1, ab1, ffn1_norm, W['ffn1_w_up'], W['ffn1_w_down'], TILE_FFN, "ffn1_bwd")
    g_wd1, _ = _tn_matmul(act1, dx1, 512, "grad_ffn1_w_down", scale=0.5, tt=TILE_GRAD_TOKENS_WIDE)
    ex_wd = start_scatter(g_wd1, 'ffn1_w_down', "ffn1_w_down")
    g_wu1a, _ = _tn_matmul(dab1, h1, 512, "grad_ffn1_w_up_a", tt=TILE_GRAD_TOKENS_WIDEST, y_part=(0, 2), after=ex_wd[4])
    ex_a = start_scatter(g_wu1a, 'ffn1_w_up', "ffn1_w_up_a")
    g_wu1b, _ = _tn_matmul(dab1, h1, 512, "grad_ffn1_w_up_b", tt=TILE_GRAD_TOKENS_WIDEST, y_part=(1, 2), after=ex_a[4])
    ex_b = start_scatter(g_wu1b, 'ffn1_w_up', "ffn1_w_up_b")
    token = ex_b[4]

    small_names = ['loss', 'ffn1_norm', 'mix_norm', 'mem_norm', 'b_gate', 'conv_dw_w', 'conv_vec', 'att_rel_bias', 'ffn2_norm',
                   'final_norm']
    small = dict(zip(small_names, _all_sum_small(
        [loss_part + token[0:1], dg_ffn1, dg_mix, dg_mem, dbg, d_dw, d_cvec, d_rel, dg_ffn2, dg_final])))
    loss = small['loss'][0, 0]
    me = _dev_index(*_my_coords())
    for i, n in enumerate(['conv_dw_b', 'conv_ln_g', 'conv_ln_b']):
        small[n] = small['conv_vec'][i:i + 1]
    small['conv_dw_w'] = lax.dynamic_slice(small['conv_dw_w'], (0, me * conv_dw_w.shape[2]), (CONV_K, conv_dw_w.shape[2]))
    little = [n for n in WEIGHTS if n not in BIG]
    as2d = lambda t, n: t.reshape(small[n].shape)
    d_s, m_s, v_s = _adamw_small([as2d(w[n], n) for n in little], [small[n] for n in little],
                                 [as2d(mom[n], n) for n in little], [as2d(var[n], n) for n in little])
    grad, delta, new_m, new_v = {}, {}, {}, {}
    for i, n in enumerate(little):
        grad[n], delta[n], new_m[n], new_v[n] = (t.reshape(w[n].shape) for t in (small[n], d_s[i], m_s[i], v_s[i]))
    done = [d_s[0]]
    waited = {'w_in': [ex_in], 'ffn1_w_down': [ex_wd], 'ffn1_w_up': [ex_a, ex_b]}
    order = [n for n in BIG_ORDER if n not in waited] + list(waited)
    for n in order:
        if n in waited:
            P[n] = [wait_scatter(ex, done) for ex in waited[n]]
        outs = _sum_adamw(P[n], block(w[n], n), block(mom[n], n), block(var[n], n), "adamw_" + n,
                          after=None if n in waited else token)
        done.append(outs[0])
        grad[n], delta[n], new_m[n], new_v[n] = ((jnp.transpose(t) if n in TRANSPOSED else t)[None] for t in outs)

    return (loss, dx0.reshape(NB, S, D), *[grad[n] for n in WEIGHTS], *[delta[n] for n in WEIGHTS],
            *[new_m[n] for n in WEIGHTS], *[new_v[n] for n in WEIGHTS])
```

```python
import functools

import jax
import jax.numpy as jnp
from jax import lax
from jax.experimental import pallas as pl
from jax.experimental.pallas import tpu as pltpu

F32 = jnp.float32
BF16 = jnp.bfloat16

EPS = 1e-6
MASK_VALUE = -1e30
D = 1024
NDEV = 8
FF = 2816
FF_SHARD = 704
FF_HALF_ROWS = 352
FF_BLOCK_EDGES = ()
IN_COLS = 6144
CONV_W = 512
CONV_K = 31
CONV_HALO = 32
CONV_CHUNK = 32
CONV_WIN = CONV_CHUNK + 40
GLU_CHUNK = 128
ATT_W = 512
ATT_HEADS = 8
ATT_HD = 64
CHUNK = 64
LEFT_CHUNKS = 8
MAX_REL = 128
N_REL = 192
QB = 256
KWIN = QB + LEFT_CHUNKS * CHUNK
KPAD = LEFT_CHUNKS * CHUNK
DS_LANES = 1024
MEM_W = 512
MEM_HEADS = 4
MEM_HD = 128
ADAM_LR = 0.001
ADAM_B1 = 0.9
ADAM_B2 = 0.999
ADAM_EPS = 1e-08
ADAM_WD = 0.01
ADAM_STEP = 10
VMEM_LIMIT = 60 * 1024 * 1024
TILE_FFN = 256
TILE_FFN_FWD = 512
TILE_COMBINE = 256
TILE_TOKENS = 512
TILE_GRAD_TOKENS = 2048
TILE_GRAD_TOKENS_WIDE = 1024
TILE_GRAD_TOKENS_WIDEST = 512

MESH = pl.DeviceIdType.MESH
ANY = pl.BlockSpec(memory_space=pl.ANY)

WEIGHTS = ['ffn1_norm', 'ffn1_w_up', 'ffn1_w_down', 'mix_norm', 'mem_norm', 'w_in', 'b_gate', 'conv_dw_w', 'conv_dw_b',
           'conv_ln_g', 'conv_ln_b', 'conv_w_pw', 'att_rel_bias', 'att_w_o', 'mem_w_kv', 'mem_w_o', 'w_out', 'ffn2_norm',
           'ffn2_w_up', 'ffn2_w_down', 'final_norm']
BIG = {
    'ffn1_w_up': ('row', FF_SHARD), 'ffn1_w_down': ('row', FF_HALF_ROWS), 'w_in': ('col', 768),
    'conv_w_pw': ('col', 128), 'att_w_o': ('col', 128), 'mem_w_kv': ('row', 128), 'mem_w_o': ('col', 128),
    'w_out': ('row', 128), 'ffn2_w_up': ('row', FF_SHARD), 'ffn2_w_down': ('row', FF_HALF_ROWS),
}
BIG_ORDER = ['ffn1_w_up', 'ffn1_w_down', 'w_in', 'conv_w_pw', 'att_w_o', 'mem_w_kv', 'mem_w_o', 'w_out', 'ffn2_w_up', 'ffn2_w_down']
TRANSPOSED = ('ffn1_w_up', 'ffn2_w_up')


def _dot(a, b):
    return jnp.dot(a, b, preferred_element_type=F32)


def _dot_nt(a, b):
    return lax.dot_general(a, b, (((1,), (1,)), ((), ())), preferred_element_type=F32)


def _dot_tn(a, b):
    return lax.dot_general(a, b, (((0,), (0,)), ((), ())), preferred_element_type=F32)


def _sigmoid(v):
    return jax.nn.sigmoid(v)


def _const(shape):
    return pl.BlockSpec(shape, lambda *_: (0,) * len(shape), pipeline_mode=pl.Buffered(1))


def _params(*sem):
    return pltpu.CompilerParams(dimension_semantics=sem if sem else None, vmem_limit_bytes=VMEM_LIMIT)


def _my_coords():
    return lax.axis_index("x"), lax.axis_index("y"), lax.axis_index("c")


def _dev_index(px, py, pc):
    return 4 * px + 2 * py + pc


def _window(ref, kind, n, p):
    if kind == 'row':
        return ref.at[pl.ds(pl.multiple_of(p * n, n), n), :]
    return ref.at[:, pl.ds(pl.multiple_of(p * n, 128), n)]


def _full_shape(kind, n, shard_shape):
    if kind == 'row':
        return (NDEV * n, shard_shape[1])
    return (shard_shape[0], NDEV * n)


def _cast_shards(shards):
    n = len(shards)

    def body(*refs):
        for i in range(n):
            refs[n + i][...] = refs[i][...].astype(BF16)

    out_shape = [jax.ShapeDtypeStruct(s.shape, BF16) for s in shards]
    return pl.pallas_call(body, out_shape=out_shape, name="cast_shards", compiler_params=_params())(*shards)


class _Ride:
    def __init__(self, inputs, out_shape, scratch, start, finish, mids=()):
        self.inputs, self.out_shape, self.scratch = list(inputs), list(out_shape), list(scratch)
        self.start, self.finish, self.mids = start, finish, tuple(mids)


def _pallas(body, name, grid, in_specs, out_specs, out_shape, args, scratch_shapes=(), sem=None, aliases=None, ride=None,
            after=None):
    if ride is None:
        n_in, n_dep = len(args), 0 if after is None else 1

        def kernel_body(*refs):
            body(*refs[:n_in], *refs[n_in + n_dep:])

        outs = pl.pallas_call(kernel_body if n_dep else body, grid=grid, name=name, in_specs=list(in_specs) + [ANY] * n_dep,
                              out_specs=out_specs, out_shape=out_shape, scratch_shapes=list(scratch_shapes),
                              input_output_aliases=aliases or {}, compiler_params=_params(*sem),
                              )(*args, *([after] if n_dep else []))
        return list(outs), []
    n_in, n_out, n_scr = len(args), len(out_shape), len(scratch_shapes)
    r_in, r_out = len(ride.inputs), len(ride.out_shape)

    def wrapped(*refs):
        k_in, rin = refs[:n_in], refs[n_in:n_in + r_in]
        o0 = n_in + r_in
        k_out, rout = refs[o0:o0 + n_out], refs[o0 + n_out:o0 + n_out + r_out]
        s0 = o0 + n_out + r_out
        k_scr, rscr = refs[s0:s0 + n_scr], refs[s0 + n_scr:]
        ids = [pl.program_id(k) for k in range(len(grid))]
        first = functools.reduce(jnp.logical_and, [i == 0 for i in ids])
        last = functools.reduce(jnp.logical_and, [i == g - 1 for i, g in zip(ids, grid)])
        pl.when(first)(lambda: ride.start(rin, rout, rscr))
        single_step = all(g == 1 for g in grid)
        for quarter, mid in ride.mids:
            if not single_step:
                at_mid = functools.reduce(jnp.logical_and, [ids[0] == (quarter * grid[0]) // 4] + [i == 0 for i in ids[1:]])
                pl.when(at_mid)(functools.partial(mid, rin, rout, rscr))
        body(*k_in, *k_out, *k_scr)
        for _, mid in ride.mids:
            if single_step:
                mid(rin, rout, rscr)
        pl.when(last)(lambda: ride.finish(rin, rout, rscr))

    outs = pl.pallas_call(
        wrapped, grid=grid, name=name, in_specs=list(in_specs) + [ANY] * r_in, out_specs=list(out_specs) + [ANY] * r_out,
        out_shape=list(out_shape) + ride.out_shape, scratch_shapes=list(scratch_shapes) + ride.scratch,
        input_output_aliases=aliases or {}, compiler_params=_params(*(["arbitrary"] * len(grid))),
    )(*args, *ride.inputs)
    return list(outs[:n_out]), list(outs[n_out:])


def _gather_ride(shards, kinds):
    n = len(shards)

    def plan(rin, out, sems):
        send_sems, recv_sems, local_sems = sems[:3]
        x, y, c = _my_coords()
        me, sibling = (x, y, c), (x, y, 1 - c)
        xn, yn, diag = (1 - x, y), (x, 1 - y), (1 - x, 1 - y)

        def win(i, dev):
            return _window(out[i], kinds[i][0], kinds[i][1], _dev_index(*dev))

        def copy(i, k, block, to, from_shard=False):
            return pltpu.make_async_remote_copy(
                src_ref=rin[i] if from_shard else win(i, block), dst_ref=win(i, block),
                send_sem=send_sems.at[i, k], recv_sem=recv_sems.at[i, k], device_id=to, device_id_type=MESH)

        def each(fn):
            return [fn(i) for i in range(n)]

        return dict(
            local=lambda: each(lambda i: pltpu.make_async_copy(rin[i], win(i, me), local_sems.at[i])),
            own=lambda: [cp for i in range(n) for cp in (copy(i, 0, me, sibling, True), copy(i, 1, me, (*xn, c), True),
                                                         copy(i, 2, me, (*yn, c), True))],
            from_x=lambda: each(lambda i: copy(i, 1, (*xn, c), me)),
            from_y=lambda: each(lambda i: copy(i, 2, (*yn, c), me)),
            x_block_on_to_y=lambda: each(lambda i: copy(i, 3, (*xn, c), (*yn, c))),
            y_block_on_to_x=lambda: each(lambda i: copy(i, 3, (*yn, c), (*xn, c))),
            from_diag=lambda: each(lambda i: copy(i, 3, (*diag, c), me)),
            to_sibling=lambda: [copy(i, 4 + j, (*chip, c), sibling) for j, chip in enumerate((xn, yn, diag)) for i in range(n)],
            from_sibling=lambda: [cp for i in range(n) for cp in
                                  [copy(i, 0, sibling, me)] + [copy(i, 4 + j, (*chip, 1 - c), me) for j, chip in enumerate((xn, yn, diag))]],
            north=c == 1)

    def start(rin, out, sems):
        p = plan(rin, out, sems)
        for cp in p['local']() + p['own']():
            cp.start()

    def pass_diagonal(rin, out, sems):
        p = plan(rin, out, sems)

        @pl.when(p['north'])
        def _():
            for got, fwd in zip(p['from_x'](), p['x_block_on_to_y']()):
                got.wait_recv()
                fwd.start()

        @pl.when(jnp.logical_not(p['north']))
        def _():
            for got, fwd in zip(p['from_y'](), p['y_block_on_to_x']()):
                got.wait_recv()
                fwd.start()

    def pass_to_sibling(rin, out, sems):
        p = plan(rin, out, sems)

        @pl.when(p['north'])
        def _():
            for cp in p['from_y']():
                cp.wait_recv()

        @pl.when(jnp.logical_not(p['north']))
        def _():
            for cp in p['from_x']():
                cp.wait_recv()
        for cp in p['from_diag']():
            cp.wait_recv()
        for cp in p['to_sibling']():
            cp.start()

    def finish(rin, out, sems):
        p = plan(rin, out, sems)
        for cp in p['from_sibling']():
            cp.wait_recv()
        for cp in p['own']() + p['to_sibling']():
            cp.wait_send()

        @pl.when(p['north'])
        def _():
            for cp in p['x_block_on_to_y']():
                cp.wait_send()

        @pl.when(jnp.logical_not(p['north']))
        def _():
            for cp in p['y_block_on_to_x']():
                cp.wait_send()
        for cp in p['local']():
            cp.wait()

    out_shape = [jax.ShapeDtypeStruct(_full_shape(k, m, s.shape), s.dtype) for s, (k, m) in zip(shards, kinds)]
    scratch = [pltpu.SemaphoreType.DMA((n, 7)), pltpu.SemaphoreType.DMA((n, 7)), pltpu.SemaphoreType.DMA((n,))]
    return _Ride(shards, out_shape, scratch, start, finish, mids=((2, pass_diagonal), (3, pass_to_sibling)))


def _scatter_ride(grads, kinds):
    n = len(grads)

    def plan(g, out, sems):
        send_sems, recv_sems, local_sems = sems
        x, y, c = _my_coords()
        me = _dev_index(x, y, c)

        def local():
            return [pltpu.make_async_copy(_window(g[i], kinds[i][0], kinds[i][1], me), out[i].at[me], local_sems.at[i])
                    for i in range(n)]

        def remote(arrival):
            cps = []
            for rel in range(1, NDEV):
                peer = _peer(x, y, c, rel)
                dev = _dev_index(*peer)
                for i in range(n):
                    kind, m = kinds[i]
                    cps.append(pltpu.make_async_remote_copy(
                        src_ref=_window(g[i], kind, m, me if arrival else dev), dst_ref=out[i].at[dev if arrival else me],
                        send_sem=send_sems.at[i, rel - 1], recv_sem=recv_sems.at[i, rel - 1], device_id=peer, device_id_type=MESH))
            return cps

        return local, remote

    def start(g, out, sems):
        local, remote = plan(g, out, sems)
        for cp in local() + remote(False):
            cp.start()

    def finish(g, out, sems):
        local, remote = plan(g, out, sems)
        for cp in remote(True):
            cp.wait_recv()
        for cp in remote(False):
            cp.wait_send()
        for cp in local():
            cp.wait()

    def block_shape(gr, kind, m):
        return (m, gr.shape[1]) if kind == 'row' else (gr.shape[0], m)

    out_shape = [jax.ShapeDtypeStruct((NDEV,) + block_shape(gr, k, m), gr.dtype) for gr, (k, m) in zip(grads, kinds)]
    scratch = [pltpu.SemaphoreType.DMA((n, NDEV - 1)), pltpu.SemaphoreType.DMA((n, NDEV - 1)), pltpu.SemaphoreType.DMA((n,))]
    return _Ride(grads, out_shape, scratch, start, finish)


def _rms_stats(xf):
    r = lax.rsqrt(jnp.mean(xf * xf, axis=-1, keepdims=True) + EPS)
    return xf * r, r


def _rms_bwd(dh, g, xhat, r):
    dxhat = dh * g
    return r * (dxhat - xhat * jnp.mean(dxhat * xhat, axis=-1, keepdims=True))


def _ffn_blocks():
    edges = (0,) + FF_BLOCK_EDGES + (FF,)
    return [(slice(lo, hi), slice(FF + lo, FF + hi)) for lo, hi in zip(edges[:-1], edges[1:])]


def _swiglu_tile(x_ref, g_ref, wut_ref, wd_ref, ab_ref):
    xf = x_ref[...]
    xhat, _ = _rms_stats(xf)
    h = (xhat * g_ref[...]).astype(BF16)
    acc = jnp.zeros(xf.shape, F32)
    for ra, rb in _ffn_blocks():
        a = _dot_nt(h, wut_ref[ra, :])
        b = _dot_nt(h, wut_ref[rb, :])
        ab_ref[:, ra] = a.astype(BF16)
        ab_ref[:, rb] = b.astype(BF16)
        act = (a * _sigmoid(a) * b).astype(BF16)
        acc = acc + _dot(act, wd_ref[ra, :])
    return xf + 0.5 * acc


def _ffn_fwd(x, g, wut, wd, tm, name, ride=None):
    T = x.shape[0]

    def body(x_ref, g_ref, wut_ref, wd_ref, xo_ref, ab_ref):
        xo_ref[...] = _swiglu_tile(x_ref, g_ref, wut_ref, wd_ref, ab_ref)

    return _pallas(
        body, name, (T // tm,),
        [pl.BlockSpec((tm, D), lambda t: (t, 0)), _const((1, D)), _const((2 * FF, D)), _const((FF, D))],
        [pl.BlockSpec((tm, D), lambda t: (t, 0)), pl.BlockSpec((tm, 2 * FF), lambda t: (t, 0))],
        [jax.ShapeDtypeStruct((T, D), F32), jax.ShapeDtypeStruct((T, 2 * FF), BF16)],
        (x, g, wut, wd), sem=("arbitrary",), ride=ride)


def _ffn_fwd_loss(x, g, wut, wd, g_final, target, tm, name):
    T = x.shape[0]

    def body(x_ref, g_ref, wut_ref, wd_ref, gf_ref, t_ref, dx_ref, ab_ref, loss_ref, dgf_ref):
        xhat, r = _rms_stats(_swiglu_tile(x_ref, g_ref, wut_ref, wd_ref, ab_ref))
        gain = gf_ref[...]
        diff = xhat * gain - t_ref[...]
        dout = diff * (1.0 / D)

        @pl.when(pl.program_id(0) == 0)
        def _():
            loss_ref[...] = jnp.zeros_like(loss_ref)
            dgf_ref[...] = jnp.zeros_like(dgf_ref)
        sq = jnp.sum(jnp.sum(diff * diff, axis=0, keepdims=True), axis=1, keepdims=True)
        loss_ref[...] += jnp.broadcast_to(sq * (0.5 / D), (1, 128))
        dgf_ref[...] += jnp.sum(dout * xhat, axis=0, keepdims=True)
        dx_ref[...] = _rms_bwd(dout, gain, xhat, r)

    row = pl.BlockSpec((tm, D), lambda t: (t, 0))
    return pl.pallas_call(
        body, grid=(T // tm,), name=name,
        in_specs=[row, _const((1, D)), _const((2 * FF, D)), _const((FF, D)), _const((1, D)), row],
        out_specs=[row, pl.BlockSpec((tm, 2 * FF), lambda t: (t, 0)), pl.BlockSpec((1, 128), lambda t: (0, 0)),
                   pl.BlockSpec((1, D), lambda t: (0, 0))],
        out_shape=[jax.ShapeDtypeStruct((T, D), F32), jax.ShapeDtypeStruct((T, 2 * FF), BF16),
                   jax.ShapeDtypeStruct((1, 128), F32), jax.ShapeDtypeStruct((1, D), F32)],
        compiler_params=_params("arbitrary"),
    )(x, g, wut, wd, g_final, target)


def _ffn_bwd(x, dy, ab, g, wut, wd, tm, name):
    T = x.shape[0]

    def body(x_ref, dy_ref, ab_ref, g_ref, wut_ref, wd_ref, dx_ref, dab_ref, act_ref, h_ref, dg_ref):
        xf = x_ref[...]
        xhat, r = _rms_stats(xf)
        gain = g_ref[...]
        h_ref[...] = (xhat * gain).astype(BF16)
        dy = dy_ref[...]
        dyh = (0.5 * dy).astype(BF16)
        dh = jnp.zeros((tm, D), F32)
        for ra, rb in _ffn_blocks():
            a = ab_ref[:, ra].astype(F32)
            b = ab_ref[:, rb].astype(F32)
            dact = _dot_nt(dyh, wd_ref[ra, :])
            sg = _sigmoid(a)
            sl = a * sg
            act_ref[:, ra] = (sl * b).astype(BF16)
            da = (dact * b * (sg * (1.0 + a * (1.0 - sg)))).astype(BF16)
            db = (dact * sl).astype(BF16)
            dab_ref[:, ra] = da
            dab_ref[:, rb] = db
            dh = dh + _dot(da, wut_ref[ra, :]) + _dot(db, wut_ref[rb, :])
        dx_ref[...] = dy + _rms_bwd(dh, gain, xhat, r)

        @pl.when(pl.program_id(0) == 0)
        def _():
            dg_ref[...] = jnp.zeros_like(dg_ref)
        dg_ref[...] += jnp.sum(dh * xhat, axis=0, keepdims=True)

    return pl.pallas_call(
        body, grid=(T // tm,), name=name,
        in_specs=[pl.BlockSpec((tm, D), lambda t: (t, 0)), pl.BlockSpec((tm, D), lambda t: (t, 0)),
                  pl.BlockSpec((tm, 2 * FF), lambda t: (t, 0)), _const((1, D)), _const((2 * FF, D)), _const((FF, D))],
        out_specs=[pl.BlockSpec((tm, D), lambda t: (t, 0)), pl.BlockSpec((tm, 2 * FF), lambda t: (t, 0)),
                   pl.BlockSpec((tm, FF), lambda t: (t, 0)), pl.BlockSpec((tm, D), lambda t: (t, 0)),
                   pl.BlockSpec((1, D), lambda t: (0, 0))],
        out_shape=[jax.ShapeDtypeStruct((T, D), F32), jax.ShapeDtypeStruct((T, 2 * FF), BF16),
                   jax.ShapeDtypeStruct((T, FF), BF16), jax.ShapeDtypeStruct((T, D), BF16), jax.ShapeDtypeStruct((1, D), F32)],
        compiler_params=_params("arbitrary"),
    )(x, dy, ab, g, wut, wd)


def _tn_matmul(xm, ym, tn, name, scale=None, out_cols=None, col_off=0, prev=None, tt=TILE_GRAD_TOKENS, x_part=(0, 1),
               out_rows=None, y_part=(0, 1), ride=None, after=None):
    T = xm.shape[0]
    xi, xn = x_part
    yi, yn = y_part
    K = xm.shape[1] // xn
    N = ym.shape[1] // yn
    out_cols = N if out_cols is None else out_cols
    row_blk = xi if out_rows is not None else 0
    out_rows = K if out_rows is None else out_rows
    tt = min(tt, T)
    nt = T // tt
    off = col_off // tn

    def body(*refs):
        x_ref, y_ref = refs[0], refs[1]
        o_ref, acc = refs[-2], refs[-1]

        @pl.when(pl.program_id(1) == 0)
        def _():
            acc[...] = jnp.zeros_like(acc)
        acc[...] += _dot_tn(x_ref[...].astype(BF16), y_ref[...].astype(BF16))

        @pl.when(pl.program_id(1) == nt - 1)
        def _():
            res = acc[...]
            o_ref[...] = (res if scale is None else res * scale).astype(BF16)

    ycol = yi * (N // tn)
    in_specs = [pl.BlockSpec((tt, K), lambda n, t: (t, xi)), pl.BlockSpec((tt, tn), lambda n, t: (t, n + ycol))]
    args = [xm, ym]
    aliases = {}
    if prev is not None:
        in_specs.append(ANY)
        args.append(prev)
        aliases = {2: 0}
    outs, rode = _pallas(
        body, name, (N // tn, nt), in_specs, [pl.BlockSpec((K, tn), lambda n, t: (row_blk, n + off))],
        [jax.ShapeDtypeStruct((out_rows, out_cols), BF16)], args, scratch_shapes=[pltpu.VMEM((K, tn), F32)],
        sem=("parallel", "arbitrary"), aliases=aliases, ride=ride, after=after)
    return outs[0], rode


def _mix_fwd(x, g, w_in, tm, ride=None):
    T = x.shape[0]

    def body(x_ref, g_ref, w_ref, uc_ref, qkv_ref, mq_ref, gl_ref, h_ref):
        xhat, _ = _rms_stats(x_ref[...])
        h = (xhat * g_ref[...]).astype(BF16)
        h_ref[...] = h
        uc_ref[...] = _dot(h, w_ref[:, 0:1024])
        qkv_ref[...] = _dot(h, w_ref[:, 1024:2560]).astype(BF16)
        mq_ref[...] = _dot(h, w_ref[:, 2560:3072]).astype(BF16)
        for j in range(3):
            gl_ref[:, j * D:(j + 1) * D] = _dot(h, w_ref[:, 3072 + j * D:3072 + (j + 1) * D]).astype(BF16)

    row = lambda w: pl.BlockSpec((tm, w), lambda t: (t, 0))
    return _pallas(
        body, "mix_fwd", (T // tm,), [row(D), _const((1, D)), _const((D, IN_COLS))],
        [row(1024), row(1536), row(512), row(3072), row(D)],
        [jax.ShapeDtypeStruct((T, 1024), F32), jax.ShapeDtypeStruct((T, 1536), BF16), jax.ShapeDtypeStruct((T, 512), BF16),
         jax.ShapeDtypeStruct((T, 3072), BF16), jax.ShapeDtypeStruct((T, D), BF16)],
        (x, g, w_in), sem=("parallel",), ride=ride)


def _mix_bwd(x, dres, duc, dqkv, dmq, dgl, g, w_in, tm, ride=None, after=None):
    T = x.shape[0]

    def body(x_ref, dres_ref, duc_ref, dqkv_ref, dmq_ref, dgl_ref, g_ref, w_ref, dx_ref, dg_ref):
        xhat, r = _rms_stats(x_ref[...])
        dh = _dot_nt(duc_ref[...], w_ref[:, 0:1024])
        dh = dh + _dot_nt(dqkv_ref[...], w_ref[:, 1024:2560])
        dh = dh + _dot_nt(dmq_ref[...], w_ref[:, 2560:3072])
        dh = dh + _dot_nt(dgl_ref[...], w_ref[:, 3072:6144])
        dx_ref[...] = dres_ref[...] + _rms_bwd(dh, g_ref[...], xhat, r)

        @pl.when(pl.program_id(0) == 0)
        def _():
            dg_ref[...] = jnp.zeros_like(dg_ref)
        dg_ref[...] += jnp.sum(dh * xhat, axis=0, keepdims=True)

    row = lambda w: pl.BlockSpec((tm, w), lambda t: (t, 0))
    return _pallas(
        body, "mix_bwd", (T // tm,),
        [row(D), row(D), row(1024), row(1536), row(512), row(3072), _const((1, D)), _const((D, IN_COLS))],
        [row(D), pl.BlockSpec((1, D), lambda t: (0, 0))],
        [jax.ShapeDtypeStruct((T, D), F32), jax.ShapeDtypeStruct((1, D), F32)],
        (x, dres, duc, dqkv, dmq, dgl, g, w_in), sem=("arbitrary",), ride=ride, after=after)


def _shifted(win, base, copies):
    for k in range(8):
        copies[k] = win[base + k:base + k + CONV_CHUNK + 24]
    return copies


def _tap_slices(copies, tap):
    out = []
    for k in range(8):
        for a in range(4):
            j = tap(a, k)
            if 0 <= j < CONV_K:
                out.append((j, copies[k, pl.ds(8 * a, CONV_CHUNK), :]))
    return out


def _conv_taps(copies, w_ref, tap):
    acc = jnp.zeros((CONV_CHUNK, CONV_W), F32)
    for j, rows in _tap_slices(copies, tap):
        acc = acc + rows * w_ref[j:j + 1, :]
    return acc


def _fold8(v):
    acc = v[0:8]
    for r in range(8, CONV_CHUNK, 8):
        acc = acc + v[r:r + 8]
    return acc


def _glu_into(uc_ref, vpad, S):
    vpad[pl.ds(0, CONV_HALO), :] = jnp.zeros((CONV_HALO, CONV_W), F32)
    vpad[pl.ds(S + CONV_HALO, CONV_HALO), :] = jnp.zeros((CONV_HALO, CONV_W), F32)

    def glu(i, carry):
        r0 = pl.multiple_of(i * GLU_CHUNK, GLU_CHUNK)
        a = uc_ref[0, pl.ds(r0, GLU_CHUNK), 0:CONV_W]
        gt = uc_ref[0, pl.ds(r0, GLU_CHUNK), CONV_W:2 * CONV_W]
        vpad[pl.ds(pl.multiple_of(r0 + CONV_HALO, CONV_HALO), GLU_CHUNK), :] = a * _sigmoid(gt)
        return carry
    lax.fori_loop(0, S // GLU_CHUNK, glu, 0)


def _layer_norm(z, vec_ref):
    xc = z - jnp.mean(z, axis=-1, keepdims=True)
    rstd = lax.rsqrt(jnp.mean(xc * xc, axis=-1, keepdims=True) + EPS)
    xn = xc * rstd
    return xn, rstd, xn * vec_ref[1:2, :] + vec_ref[2:3, :]


def _conv_fwd(uc, dw_w, vec):
    NB, S, _ = uc.shape

    def body(uc_ref, w_ref, vec_ref, o_ref, z_ref, vpad, copies):
        _glu_into(uc_ref, vpad, S)

        def conv(i, carry):
            r0 = pl.multiple_of(i * CONV_CHUNK, CONV_CHUNK)
            win = vpad[pl.ds(r0, CONV_WIN), :]
            z = _conv_taps(_shifted(win, CONV_HALO - (CONV_K - 1), copies), w_ref, lambda a, k: 8 * a + k) + vec_ref[0:1, :]
            z_ref[0, pl.ds(r0, CONV_CHUNK), :] = z
            _, _, yln = _layer_norm(z, vec_ref)
            o_ref[0, pl.ds(r0, CONV_CHUNK), :] = (yln * _sigmoid(yln)).astype(BF16)
            return carry
        lax.fori_loop(0, S // CONV_CHUNK, conv, 0, unroll=4)

    seq = pl.BlockSpec((1, S, CONV_W), lambda b: (b, 0, 0))
    return pl.pallas_call(
        body, grid=(NB,), name="conv_fwd",
        in_specs=[pl.BlockSpec((1, S, 2 * CONV_W), lambda b: (b, 0, 0)), _const((CONV_K, CONV_W)), _const((8, CONV_W))],
        out_specs=[seq, seq],
        out_shape=[jax.ShapeDtypeStruct((NB, S, CONV_W), BF16), jax.ShapeDtypeStruct((NB, S, CONV_W), F32)],
        scratch_shapes=[pltpu.VMEM((S + 2 * CONV_HALO, CONV_W), F32), pltpu.VMEM((8, CONV_CHUNK + 24, CONV_W), F32)],
        compiler_params=_params("parallel"),
    )(uc, dw_w, vec)


def _conv_bwd(uc, z, dcact, dw_w, vec, ride=None):
    NB, S, _ = uc.shape
    n_chunks = S // CONV_CHUNK

    def body(uc_ref, z_ref, dc_ref, w_ref, vec_ref, duc_ref, dw_ref, dvec_ref, vpad, dzpad, dw8, dvec8, copies):
        @pl.when(pl.program_id(0) == 0)
        def _():
            dw8[...] = jnp.zeros_like(dw8)
            dvec8[...] = jnp.zeros_like(dvec8)
        _glu_into(uc_ref, vpad, S)
        dzpad[pl.ds(S, 2 * CONV_HALO), :] = jnp.zeros((2 * CONV_HALO, CONV_W), F32)

        def norm_bwd(i, carry):
            r0 = pl.multiple_of(i * CONV_CHUNK, CONV_CHUNK)
            xn, rstd, yln = _layer_norm(z_ref[0, pl.ds(r0, CONV_CHUNK), :], vec_ref)
            sg = _sigmoid(yln)
            dyln = dc_ref[0, pl.ds(r0, CONV_CHUNK), :] * (sg * (1.0 + yln * (1.0 - sg)))
            dxn = dyln * vec_ref[1:2, :]
            dz = rstd * (dxn - jnp.mean(dxn, axis=-1, keepdims=True) - xn * jnp.mean(dxn * xn, axis=-1, keepdims=True))
            dzpad[pl.ds(r0, CONV_CHUNK), :] = dz
            dvec8[0] += _fold8(dz)
            dvec8[1] += _fold8(dyln * xn)
            dvec8[2] += _fold8(dyln)
            return carry
        lax.fori_loop(0, n_chunks, norm_bwd, 0, unroll=4)

        def taps_bwd(i, carry):
            r0 = pl.multiple_of(i * CONV_CHUNK, CONV_CHUNK)
            dzwin = dzpad[pl.ds(r0, CONV_WIN), :]
            dv = _conv_taps(_shifted(dzwin, 0, copies), w_ref, lambda a, k: CONV_K - 1 - 8 * a - k)
            dz = dzwin[0:CONV_CHUNK]
            vwin = vpad[pl.ds(r0, CONV_WIN), :]
            for j, rows in _tap_slices(_shifted(vwin, CONV_HALO - (CONV_K - 1), copies), lambda a, k: 8 * a + k):
                dw8[j] += _fold8(dz * rows)
            a = uc_ref[0, pl.ds(r0, CONV_CHUNK), 0:CONV_W]
            sg = _sigmoid(uc_ref[0, pl.ds(r0, CONV_CHUNK), CONV_W:2 * CONV_W])
            duc_ref[0, pl.ds(r0, CONV_CHUNK), 0:CONV_W] = (dv * sg).astype(BF16)
            duc_ref[0, pl.ds(r0, CONV_CHUNK), CONV_W:2 * CONV_W] = (dv * a * sg * (1.0 - sg)).astype(BF16)
            return carry
        lax.fori_loop(0, n_chunks, taps_bwd, 0, unroll=2)

        @pl.when(pl.program_id(0) == NB - 1)
        def _():
            dw_ref[...] = jnp.zeros_like(dw_ref)
            dvec_ref[...] = jnp.zeros_like(dvec_ref)
            for j in range(CONV_K):
                dw_ref[j:j + 1, :] = jnp.sum(dw8[j], axis=0, keepdims=True)
            for j in range(3):
                dvec_ref[j:j + 1, :] = jnp.sum(dvec8[j], axis=0, keepdims=True)

    return _pallas(
        body, "conv_bwd", (NB,),
        [pl.BlockSpec((1, S, 2 * CONV_W), lambda b: (b, 0, 0)), pl.BlockSpec((1, S, CONV_W), lambda b: (b, 0, 0)),
         pl.BlockSpec((1, S, CONV_W), lambda b: (b, 0, 0)), _const((CONV_K, CONV_W)), _const((8, CONV_W))],
        [pl.BlockSpec((1, S, 2 * CONV_W), lambda b: (b, 0, 0)), pl.BlockSpec((32, CONV_W), lambda b: (0, 0)),
         pl.BlockSpec((8, CONV_W), lambda b: (0, 0))],
        [jax.ShapeDtypeStruct((NB, S, 2 * CONV_W), BF16), jax.ShapeDtypeStruct((32, CONV_W), F32),
         jax.ShapeDtypeStruct((8, CONV_W), F32)],
        (uc, z, dcact, dw_w, vec),
        scratch_shapes=[pltpu.VMEM((S + 2 * CONV_HALO, CONV_W), F32), pltpu.VMEM((S + 2 * CONV_HALO, CONV_W), F32),
                        pltpu.VMEM((CONV_K, 8, CONV_W), F32), pltpu.VMEM((3, 8, CONV_W), F32),
                        pltpu.VMEM((8, CONV_CHUNK + 24, CONV_W), F32)],
        sem=("arbitrary",), ride=ride)


def _rel_index_of_column(cols):
    offset = jnp.where(cols < KWIN, cols, cols - DS_LANES)
    return jnp.clip(KPAD - offset, -(CHUNK - 1), MAX_REL) + (CHUNK - 1)


def _bias_table(rel_bias, ride=None):
    def body(rb_ref, o_ref, by_offset, first8):
        ridx = _rel_index_of_column(lax.broadcasted_iota(jnp.int32, (1, DS_LANES), 1))
        onehot = (ridx == lax.broadcasted_iota(jnp.int32, (N_REL, 1), 0)).astype(F32)
        by_offset[...] = jnp.dot(rb_ref[...], onehot, preferred_element_type=F32, precision=lax.Precision.HIGHEST)
        sub = lax.broadcasted_iota(jnp.int32, (8, 1), 0)
        kchunk = lax.broadcasted_iota(jnp.int32, (1, KWIN), 1) // CHUNK
        for head in range(ATT_HEADS):
            base = jnp.broadcast_to(by_offset[head:head + 1, :], (8, DS_LANES))
            rows = base
            for s in range(1, 8):
                rows = jnp.where(sub == s, pltpu.roll(base, s, 1), rows)
            first8[head] = rows

        def rows8(q8, carry):
            qchunk = (q8 * 8 + sub) // CHUNK
            band = (kchunk >= qchunk) & (kchunk <= qchunk + LEFT_CHUNKS)
            for head in range(ATT_HEADS):
                tile = pltpu.roll(first8[head], q8 * 8, 1)[:, 0:KWIN]
                o_ref[head, pl.ds(pl.multiple_of(q8 * 8, 8), 8), :] = jnp.where(band, tile, MASK_VALUE)
            return carry
        lax.fori_loop(0, QB // 8, rows8, 0)

    outs, rode = _pallas(
        body, "bias_table", (1,), [pl.BlockSpec((ATT_HEADS, N_REL), lambda i: (0, 0))],
        [pl.BlockSpec((ATT_HEADS, QB, KWIN), lambda i: (0, 0, 0))], [jax.ShapeDtypeStruct((ATT_HEADS, QB, KWIN), F32)], (rel_bias,),
        scratch_shapes=[pltpu.VMEM((ATT_HEADS, DS_LANES), F32), pltpu.VMEM((ATT_HEADS, 8, DS_LANES), F32)],
        sem=("arbitrary",), ride=ride)
    return outs[0], rode


def _load_keys(i, k_ref, v_ref, kpad, vpad, S):
    @pl.when(i == 0)
    def _():
        kpad[pl.ds(0, KPAD), :] = jnp.zeros((KPAD, ATT_W), BF16)
        vpad[pl.ds(0, KPAD), :] = jnp.zeros((KPAD, ATT_W), BF16)
        kpad[pl.ds(KPAD, S), :] = k_ref[0]
        vpad[pl.ds(KPAD, S), :] = v_ref[0]


def _att_scores(q2s, k2, tab_ref, head, in_head, in_seq):
    qm = jnp.where(in_head, q2s, jnp.zeros_like(q2s))
    return jnp.where(in_seq, _dot_nt(qm, k2) + tab_ref[head], MASK_VALUE)


def _scaled(q2):
    return q2 * jnp.asarray(ATT_HD ** -0.5, q2.dtype)


def _att_fwd(qkv, tab, ride=None):
    NB, S, _ = qkv.shape

    def body(q_ref, k_ref, v_ref, tab_ref, o_ref, lse_ref, kpad, vpad):
        i = pl.program_id(1)
        _load_keys(i, k_ref, v_ref, kpad, vpad, S)
        koff = pl.multiple_of(i * QB, QB)
        lane = lax.broadcasted_iota(jnp.int32, (1, 128), 1)
        in_seq = (lax.broadcasted_iota(jnp.int32, (1, KWIN), 1) + i * QB) >= KPAD
        lse = jnp.zeros((QB, 128), F32)
        for pair in range(ATT_HEADS // 2):
            cols = slice(pair * 128, (pair + 1) * 128)
            q2s = _scaled(q_ref[0, :, cols])
            k2 = kpad[pl.ds(koff, KWIN), cols]
            v2 = vpad[pl.ds(koff, KWIN), cols]
            o2 = jnp.zeros((QB, 128), F32)
            for hh in range(2):
                head = 2 * pair + hh
                in_head = (lane // ATT_HD) == hh
                s = _att_scores(q2s, k2, tab_ref, head, in_head, in_seq)
                m = jnp.max(s, axis=-1, keepdims=True)
                e = jnp.exp(s - m)
                l = jnp.sum(e, axis=-1, keepdims=True)
                o2 = jnp.where(in_head, _dot(e.astype(BF16), v2) * (1.0 / l), o2)
                lse = jnp.where(lane == head, m + jnp.log(l), lse)
            o_ref[0, :, cols] = o2.astype(BF16)
        lse_ref[0] = lse

    seq = lambda col: pl.BlockSpec((1, S, ATT_W), lambda b, i: (b, 0, col), pipeline_mode=pl.Buffered(1))
    outs, rode = _pallas(
        body, "att_fwd", (NB, S // QB),
        [pl.BlockSpec((1, QB, ATT_W), lambda b, i: (b, i, 0)), seq(1), seq(2), _const((ATT_HEADS, QB, KWIN))],
        [pl.BlockSpec((1, QB, ATT_W), lambda b, i: (b, i, 0)), pl.BlockSpec((1, QB, 128), lambda b, i: (b, i, 0))],
        [jax.ShapeDtypeStruct((NB, S, ATT_W), BF16), jax.ShapeDtypeStruct((NB, S, 128), F32)],
        (qkv, qkv, qkv, tab),
        scratch_shapes=[pltpu.VMEM((S + KPAD, ATT_W), BF16), pltpu.VMEM((S + KPAD, ATT_W), BF16)],
        sem=("arbitrary", "arbitrary"), ride=ride)
    return outs[0], outs[1], rode


def _att_bwd(qkv, o, lse, do, tab, ride=None):
    NB, S, _ = qkv.shape
    nq = S // QB

    def body(q_ref, k_ref, v_ref, o_ref, lse_ref, do_ref, tab_ref, dqkv_ref, ds_hbm, kpad, vpad, dkpad, dvpad, ds_acc, ds_sem):
        b, i = pl.program_id(0), pl.program_id(1)
        _load_keys(i, k_ref, v_ref, kpad, vpad, S)

        @pl.when(i == 0)
        def _():
            dkpad[...] = jnp.zeros_like(dkpad)
            dvpad[...] = jnp.zeros_like(dvpad)

        @pl.when((i == 0) & (b == 0))
        def _():
            ds_acc[...] = jnp.zeros_like(ds_acc)

        koff = pl.multiple_of(i * QB, QB)
        lane = lax.broadcasted_iota(jnp.int32, (1, 128), 1)
        in_seq = (lax.broadcasted_iota(jnp.int32, (1, KWIN), 1) + i * QB) >= KPAD
        for pair in range(ATT_HEADS // 2):
            cols = slice(pair * 128, (pair + 1) * 128)
            q2s = _scaled(q_ref[0, :, cols])
            do2 = do_ref[0, :, cols]
            k2 = kpad[pl.ds(koff, KWIN), cols]
            v2 = vpad[pl.ds(koff, KWIN), cols]
            do_o = do2.astype(F32) * o_ref[0, :, cols].astype(F32)
            dq2 = jnp.zeros((QB, 128), F32)
            dk2 = jnp.zeros((KWIN, 128), F32)
            dv2 = jnp.zeros((KWIN, 128), F32)
            for hh in range(2):
                head = 2 * pair + hh
                in_head = (lane // ATT_HD) == hh
                p = jnp.exp(_att_scores(q2s, k2, tab_ref, head, in_head, in_seq) - lse_ref[0, :, head:head + 1])
                row_term = jnp.sum(jnp.where(in_head, do_o, 0.0), axis=-1, keepdims=True)
                dom = jnp.where(in_head, do2, jnp.zeros_like(do2))
                ds = p * (_dot_nt(dom, v2) - row_term)
                ds_acc[head] += ds
                dsb = ds.astype(BF16)
                dq2 = jnp.where(in_head, _dot(dsb, k2), dq2)
                dk2 = jnp.where(in_head, _dot_tn(dsb, q2s), dk2)
                dv2 = jnp.where(in_head, _dot_tn(p.astype(BF16), do2), dv2)
            dqkv_ref[0, pl.ds(koff, QB), cols] = (dq2 * (ATT_HD ** -0.5)).astype(BF16)
            dkpad[pl.ds(koff, KWIN), cols] += dk2
            dvpad[pl.ds(koff, KWIN), cols] += dv2

        @pl.when(i == nq - 1)
        def _():
            dqkv_ref[0, :, ATT_W:2 * ATT_W] = dkpad[pl.ds(KPAD, S), :].astype(BF16)
            dqkv_ref[0, :, 2 * ATT_W:3 * ATT_W] = dvpad[pl.ds(KPAD, S), :].astype(BF16)

        @pl.when((i == nq - 1) & (b == NB - 1))
        def _():
            out = pltpu.make_async_copy(ds_acc, ds_hbm, ds_sem)
            out.start()
            out.wait()

    seq = lambda col: pl.BlockSpec((1, S, ATT_W), lambda b, i: (b, 0, col), pipeline_mode=pl.Buffered(1))
    rows = pl.BlockSpec((1, QB, ATT_W), lambda b, i: (b, i, 0))
    return _pallas(
        body, "att_bwd", (NB, nq),
        [rows, seq(1), seq(2), rows, pl.BlockSpec((1, QB, 128), lambda b, i: (b, i, 0)), rows, _const((ATT_HEADS, QB, KWIN))],
        [pl.BlockSpec((1, S, 3 * ATT_W), lambda b, i: (b, 0, 0)), ANY],
        [jax.ShapeDtypeStruct((NB, S, 3 * ATT_W), BF16), jax.ShapeDtypeStruct((ATT_HEADS, QB, KWIN), F32)],
        (qkv, qkv, qkv, o, lse, do, tab),
        scratch_shapes=[pltpu.VMEM((S + KPAD, ATT_W), BF16), pltpu.VMEM((S + KPAD, ATT_W), BF16),
                        pltpu.VMEM((S + KPAD, ATT_W), F32), pltpu.VMEM((S + KPAD, ATT_W), F32),
                        pltpu.VMEM((ATT_HEADS, QB, KWIN), F32), pltpu.SemaphoreType.DMA],
        sem=("arbitrary", "arbitrary"), ride=ride)


def _rel_bias_grad(ds):
    def body(ds_ref, o_ref):
        sub = lax.broadcasted_iota(jnp.int32, (8, 1), 0)
        ridx = _rel_index_of_column(lax.broadcasted_iota(jnp.int32, (DS_LANES, 1), 0))
        onehot = (ridx == lax.broadcasted_iota(jnp.int32, (1, N_REL), 1)).astype(F32)
        def rows8(q8, accs):
            shift = lax.rem(DS_LANES - q8 * 8, DS_LANES)
            out = []
            for head in range(ATT_HEADS):
                tile = ds_ref[head, pl.ds(pl.multiple_of(q8 * 8, 8), 8), :]
                tile = jnp.concatenate([tile, jnp.zeros((8, DS_LANES - KWIN), F32)], axis=1)
                out.append(accs[head] + pltpu.roll(tile, shift, 1))
            return tuple(out)
        accs = lax.fori_loop(0, QB // 8, rows8, tuple(jnp.zeros((8, DS_LANES), F32) for _ in range(ATT_HEADS)))
        for head in range(ATT_HEADS):
            acc = accs[head]
            diag = jnp.zeros((8, DS_LANES), F32)
            for s in range(8):
                shifted = acc if s == 0 else pltpu.roll(acc, DS_LANES - s, 1)
                diag = jnp.where(sub == s, shifted, diag)
            z = jnp.sum(diag, axis=0, keepdims=True)
            o_ref[head:head + 1, :] = jnp.dot(z, onehot, preferred_element_type=F32, precision=lax.Precision.HIGHEST)

    return pl.pallas_call(body, out_shape=jax.ShapeDtypeStruct((ATT_HEADS, N_REL), F32), name="rel_bias_grad",
                          compiler_params=_params())(ds)


def _memkv_fwd(mem, g, w_kv, tm):
    R = mem.shape[0]
    tm = min(tm, R)

    def body(m_ref, g_ref, w_ref, h_ref, kv_ref):
        xhat, _ = _rms_stats(m_ref[...])
        h = (xhat * g_ref[...]).astype(BF16)
        h_ref[...] = h
        kv_ref[...] = _dot(h, w_ref[...]).astype(BF16)

    row = pl.BlockSpec((tm, D), lambda t: (t, 0))
    return pl.pallas_call(
        body, grid=(R // tm,), name="memkv_fwd", in_specs=[row, _const((1, D)), _const((D, 2 * MEM_W))], out_specs=[row, row],
        out_shape=[jax.ShapeDtypeStruct((R, D), BF16), jax.ShapeDtypeStruct((R, 2 * MEM_W), BF16)],
        compiler_params=_params("parallel"),
    )(mem, g, w_kv)


def _memkv_bwd(mem, dkv, w_kv, tm):
    R = mem.shape[0]
    tm = min(tm, R)

    def body(m_ref, dkv_ref, w_ref, dg_ref):
        xhat, _ = _rms_stats(m_ref[...])
        dh = _dot_nt(dkv_ref[...].astype(BF16), w_ref[...])

        @pl.when(pl.program_id(0) == 0)
        def _():
            dg_ref[...] = jnp.zeros_like(dg_ref)
        dg_ref[...] += jnp.sum(dh * xhat, axis=0, keepdims=True)

    row = pl.BlockSpec((tm, D), lambda t: (t, 0))
    return pl.pallas_call(
        body, grid=(R // tm,), name="memkv_bwd", in_specs=[row, row, _const((D, 2 * MEM_W))],
        out_specs=pl.BlockSpec((1, D), lambda t: (0, 0)), out_shape=jax.ShapeDtypeStruct((1, D), F32),
        compiler_params=_params("arbitrary"),
    )(mem, dkv, w_kv)


def _mem_exp(qh, kh):
    s = _dot_nt(qh, kh) * (MEM_HD ** -0.5)
    e = jnp.exp(s - jnp.max(s, axis=-1, keepdims=True))
    return e, jnp.sum(e, axis=-1, keepdims=True)


def _mematt_fwd(mq, kv, tq):
    NB, S, _ = mq.shape
    M = kv.shape[1]

    def body(q_ref, kv_ref, o_ref):
        for h in range(MEM_HEADS):
            cols = slice(h * MEM_HD, (h + 1) * MEM_HD)
            e, l = _mem_exp(q_ref[0, :, cols], kv_ref[0, :, cols])
            o = _dot(e.astype(BF16), kv_ref[0, :, MEM_W + h * MEM_HD:MEM_W + (h + 1) * MEM_HD]) * (1.0 / l)
            o_ref[0, :, cols] = o.astype(BF16)

    return pl.pallas_call(
        body, grid=(NB, S // tq), name="mematt_fwd",
        in_specs=[pl.BlockSpec((1, tq, MEM_W), lambda b, i: (b, i, 0)), pl.BlockSpec((1, M, 2 * MEM_W), lambda b, i: (b, 0, 0))],
        out_specs=pl.BlockSpec((1, tq, MEM_W), lambda b, i: (b, i, 0)),
        out_shape=jax.ShapeDtypeStruct((NB, S, MEM_W), BF16), compiler_params=_params("parallel", "parallel"),
    )(mq, kv)


def _mematt_bwd(mq, kv, do, tq):
    NB, S, _ = mq.shape
    M = kv.shape[1]

    def body(q_ref, kv_ref, do_ref, dq_ref, dkv_ref):
        @pl.when(pl.program_id(1) == 0)
        def _():
            dkv_ref[...] = jnp.zeros_like(dkv_ref)
        for h in range(MEM_HEADS):
            cols = slice(h * MEM_HD, (h + 1) * MEM_HD)
            vcols = slice(MEM_W + h * MEM_HD, MEM_W + (h + 1) * MEM_HD)
            qh, kh, vh, doh = q_ref[0, :, cols], kv_ref[0, :, cols], kv_ref[0, :, vcols], do_ref[0, :, cols]
            e, l = _mem_exp(qh, kh)
            p = e * (1.0 / l)
            dp = _dot_nt(doh, vh)
            ds = p * (dp - jnp.sum(p * dp, axis=-1, keepdims=True))
            dss = (ds * (MEM_HD ** -0.5)).astype(BF16)
            dq_ref[0, :, cols] = _dot(dss, kh).astype(BF16)
            dkv_ref[0, :, cols] += _dot_tn(dss, qh)
            dkv_ref[0, :, vcols] += _dot_tn(p.astype(BF16), doh)

    qspec = pl.BlockSpec((1, tq, MEM_W), lambda b, i: (b, i, 0))
    kvspec = pl.BlockSpec((1, M, 2 * MEM_W), lambda b, i: (b, 0, 0))
    return pl.pallas_call(
        body, grid=(NB, S // tq), name="mematt_bwd", in_specs=[qspec, kvspec, qspec], out_specs=[qspec, kvspec],
        out_shape=[jax.ShapeDtypeStruct((NB, S, MEM_W), BF16), jax.ShapeDtypeStruct((NB, M, 2 * MEM_W), F32)],
        compiler_params=_params("arbitrary", "arbitrary"),
    )(mq, kv, do)


def _branch(j, in_ref, w_ref, gl_ref, bg_ref):
    y = _dot(in_ref[...], w_ref[...])
    gate = _sigmoid(gl_ref[:, j * D:(j + 1) * D].astype(F32) + bg_ref[:, j * D:(j + 1) * D])
    return y, gate


def _combine_fwd(x, cact, oatt, omem, gl, bg, wpw, wo, wmo, wout, tm):
    T = x.shape[0]

    def body(x_ref, c_ref, a_ref, m_ref, gl_ref, bg_ref, wpw_ref, wo_ref, wmo_ref, wout_ref, xo_ref, y_ref):
        y = None
        for j, (in_ref, w_ref) in enumerate(((c_ref, wpw_ref), (a_ref, wo_ref), (m_ref, wmo_ref))):
            yj, gate = _branch(j, in_ref, w_ref, gl_ref, bg_ref)
            y = gate * yj if y is None else y + gate * yj
        y = y.astype(BF16)
        y_ref[...] = y
        xo_ref[...] = x_ref[...] + _dot(y, wout_ref[...])

    row = lambda w: pl.BlockSpec((tm, w), lambda t: (t, 0))
    wbr = _const((512, D))
    return pl.pallas_call(
        body, grid=(T // tm,), name="combine_fwd",
        in_specs=[row(D), row(512), row(512), row(512), row(3 * D), _const((1, 3 * D)), wbr, wbr, wbr, _const((D, D))],
        out_specs=[row(D), row(D)],
        out_shape=[jax.ShapeDtypeStruct((T, D), F32), jax.ShapeDtypeStruct((T, D), BF16)],
        compiler_params=_params("parallel"),
    )(x, cact, oatt, omem, gl, bg, wpw, wo, wmo, wout)


def _combine_bwd(dx, cact, oatt, omem, gl, bg, wpw, wo, wmo, wout, tm, ride=None):
    T = dx.shape[0]

    def body(dx_ref, c_ref, a_ref, m_ref, gl_ref, bg_ref, wpw_ref, wo_ref, wmo_ref, wout_ref,
             dgl_ref, dc_ref, da_ref, dm_ref, dyc_ref, dya_ref, dym_ref, dbg_ref):
        dy = _dot_nt(dx_ref[...].astype(BF16), wout_ref[...])

        @pl.when(pl.program_id(0) == 0)
        def _():
            dbg_ref[...] = jnp.zeros_like(dbg_ref)
        branches = ((c_ref, wpw_ref, dyc_ref, dc_ref), (a_ref, wo_ref, dya_ref, da_ref), (m_ref, wmo_ref, dym_ref, dm_ref))
        for j, (in_ref, w_ref, dyb_ref, din_ref) in enumerate(branches):
            yj, gate = _branch(j, in_ref, w_ref, gl_ref, bg_ref)
            dyg = dy * gate
            dlogit = dyg * yj * (1.0 - gate)
            dgl_ref[:, j * D:(j + 1) * D] = dlogit.astype(BF16)
            dbg_ref[:, j * D:(j + 1) * D] += jnp.sum(dlogit, axis=0, keepdims=True)
            dyb = dyg.astype(BF16)
            dyb_ref[...] = dyb
            din_ref[...] = _dot_nt(dyb, w_ref[...]).astype(din_ref.dtype)

    row = lambda w: pl.BlockSpec((tm, w), lambda t: (t, 0))
    wbr = _const((512, D))
    sds = jax.ShapeDtypeStruct
    return _pallas(
        body, "combine_bwd", (T // tm,),
        [row(D), row(512), row(512), row(512), row(3 * D), _const((1, 3 * D)), wbr, wbr, wbr, _const((D, D))],
        [row(3 * D), row(512), row(512), row(512), row(D), row(D), row(D), pl.BlockSpec((1, 3 * D), lambda t: (0, 0))],
        [sds((T, 3 * D), BF16), sds((T, 512), F32), sds((T, 512), BF16), sds((T, 512), BF16),
         sds((T, D), BF16), sds((T, D), BF16), sds((T, D), BF16), sds((1, 3 * D), F32)],
        (dx, cact, oatt, omem, gl, bg, wpw, wo, wmo, wout), sem=("arbitrary",), ride=ride)


def _peer(x, y, c, rel):
    rx, ry, rc = (rel >> 2) & 1, (rel >> 1) & 1, rel & 1
    return ((1 - x) if rx else x, (1 - y) if ry else y, (1 - c) if rc else c)


def _all_sum_small(parts):
    n = len(parts)

    def body(*refs):
        p_refs, o_refs, slots = refs[:n], refs[n:2 * n], refs[2 * n:3 * n]
        send_sems, recv_sems = refs[3 * n:]
        x, y, c = _my_coords()
        me = _dev_index(x, y, c)

        def copy(i, rel, arrival):
            peer = _peer(x, y, c, rel)
            return pltpu.make_async_remote_copy(
                src_ref=p_refs[i], dst_ref=slots[i].at[_dev_index(*peer) if arrival else me],
                send_sem=send_sems.at[i, rel - 1], recv_sem=recv_sems.at[i, rel - 1], device_id=peer, device_id_type=MESH)

        for i in range(n):
            slots[i][me] = p_refs[i][...]
        for rel in range(1, NDEV):
            for i in range(n):
                copy(i, rel, False).start()
        for rel in range(1, NDEV):
            for i in range(n):
                copy(i, rel, True).wait_recv()
        for rel in range(1, NDEV):
            for i in range(n):
                copy(i, rel, False).wait_send()
        for i in range(n):
            total = slots[i][0]
            for d in range(1, NDEV):
                total = total + slots[i][d]
            o_refs[i][...] = total

    vmem = pl.BlockSpec(memory_space=pltpu.VMEM)
    return pl.pallas_call(
        body, out_shape=[jax.ShapeDtypeStruct(p.shape, F32) for p in parts], name="all_sum_small",
        in_specs=[vmem] * n, out_specs=[vmem] * n,
        scratch_shapes=[pltpu.VMEM((NDEV,) + p.shape, F32) for p in parts]
        + [pltpu.SemaphoreType.DMA((n, NDEV - 1)), pltpu.SemaphoreType.DMA((n, NDEV - 1))],
        compiler_params=pltpu.CompilerParams(has_side_effects=True),
    )(*parts)


HBM = pl.BlockSpec(memory_space=pltpu.HBM)
SEM = pl.BlockSpec(memory_space=pltpu.SEMAPHORE)


def _own_block(g, kind, m, tag):
    def body(g_ref, land_ref, staged, sem):
        me = _dev_index(*_my_coords())
        for cp in (pltpu.make_async_copy(_window(g_ref, kind, m, me), staged, sem),
                   pltpu.make_async_copy(staged, land_ref.at[me], sem)):
            cp.start()
            cp.wait()

    block = (m, g.shape[1]) if kind == 'row' else (g.shape[0], m)
    return pl.pallas_call(body, in_specs=[ANY], out_specs=ANY, out_shape=jax.ShapeDtypeStruct((NDEV,) + block, g.dtype),
                          scratch_shapes=[pltpu.VMEM(block, g.dtype), pltpu.SemaphoreType.DMA], name="own_block_" + tag)(g)


def _scatter_start(g, land, kind, m, tag):
    def body(g_ref, land_ref, send_sems, recv_sems, g_thru, land_thru, token):
        x, y, c = _my_coords()
        me = _dev_index(x, y, c)
        for rel in range(1, NDEV):
            peer = _peer(x, y, c, rel)
            pltpu.make_async_remote_copy(src_ref=_window(g_ref, kind, m, _dev_index(*peer)), dst_ref=land_ref.at[me],
                                         send_sem=send_sems.at[rel - 1], recv_sem=recv_sems.at[rel - 1],
                                         device_id=peer, device_id_type=MESH).start()
        token[...] = jnp.zeros_like(token)

    return pl.pallas_call(
        body, name="scatter_start_" + tag,
        out_shape=(pltpu.SemaphoreType.DMA((NDEV - 1,)), pltpu.SemaphoreType.DMA((NDEV - 1,)), pltpu.HBM(g.shape, g.dtype),
                   pltpu.HBM(land.shape, land.dtype), jax.ShapeDtypeStruct((8, 128), F32)),
        in_specs=(HBM, HBM), out_specs=(SEM, SEM, HBM, HBM, pl.BlockSpec(memory_space=pltpu.VMEM)),
        input_output_aliases={0: 2, 1: 3},
        compiler_params=pltpu.CompilerParams(has_side_effects=pltpu.SideEffectType.DATAFLOW_SIDE_EFFECTING),
    )(pltpu.with_memory_space_constraint(g, pltpu.HBM), pltpu.with_memory_space_constraint(land, pltpu.HBM))


def _scatter_wait(send_sems, recv_sems, g_thru, land_thru, after, kind, m, tag):
    n_after = len(after)

    def body(*refs):
        g_ref, land_ref, send_sems, recv_sems = refs[:4]
        x, y, c = _my_coords()
        me = _dev_index(x, y, c)
        for rel in range(1, NDEV):
            peer = _peer(x, y, c, rel)
            dev = _dev_index(*peer)
            cp = pltpu.make_async_remote_copy(src_ref=_window(g_ref, kind, m, me), dst_ref=land_ref.at[dev],
                                              send_sem=send_sems.at[rel - 1], recv_sem=recv_sems.at[rel - 1],
                                              device_id=peer, device_id_type=MESH)
            cp.wait_send()
            cp.wait_recv()

    return pl.pallas_call(
        body, name="scatter_wait_" + tag,
        out_shape=(pltpu.HBM(g_thru.shape, g_thru.dtype), pltpu.HBM(land_thru.shape, land_thru.dtype)),
        in_specs=(HBM, HBM, SEM, SEM) + (ANY,) * n_after, out_specs=(HBM, HBM), input_output_aliases={0: 0, 1: 1},
        compiler_params=pltpu.CompilerParams(has_side_effects=pltpu.SideEffectType.DATAFLOW_SIDE_EFFECTING),
    )(g_thru, land_thru, send_sems, recv_sems, *after)[1]


def _adamw_math(w, g, m, v):
    m = ADAM_B1 * m + (1.0 - ADAM_B1) * g
    v = ADAM_B2 * v + (1.0 - ADAM_B2) * (g * g)
    m_hat = m / (1.0 - ADAM_B1 ** ADAM_STEP)
    v_hat = v / (1.0 - ADAM_B2 ** ADAM_STEP)
    delta = -ADAM_LR * (m_hat / (jnp.sqrt(v_hat) + ADAM_EPS) + ADAM_WD * w)
    return delta, m, v


def _sum_adamw(parts, w, m, v, name, after=None):
    R, C = w.shape
    n_parts = len(parts)
    cg = C // n_parts
    tr = max(t for t in range(8, 257, 8) if R % t == 0)
    deps = [] if after is None else [after]

    def body(*refs):
        p_refs = refs[:n_parts]
        w_ref, m_ref, v_ref = refs[n_parts:n_parts + 3]
        g_ref, d_ref, mo_ref, vo_ref = refs[n_parts + 3 + len(deps):]
        for k, p_ref in enumerate(p_refs):
            @pl.when(pl.program_id(0) == k)
            def _():
                g = p_ref[0].astype(F32)
                for d in range(1, NDEV):
                    g = g + p_ref[d].astype(F32)
                g_ref[...] = g
                d_ref[...], mo_ref[...], vo_ref[...] = _adamw_math(w_ref[...], g, m_ref[...], v_ref[...])

    part = pl.BlockSpec((NDEV, tr, cg), lambda k, t: (0, t, 0))
    blk = pl.BlockSpec((tr, cg), lambda k, t: (t, k))
    return pl.pallas_call(
        body, grid=(n_parts, R // tr), name=name, in_specs=[part] * n_parts + [blk, blk, blk] + [ANY] * len(deps),
        out_specs=[blk] * 4, out_shape=[jax.ShapeDtypeStruct((R, C), F32)] * 4, compiler_params=_params("parallel", "parallel"),
    )(*parts, w, m, v, *deps)


def _adamw_small(ws, gs, ms, vs):
    n = len(ws)

    def body(*refs):
        w_refs, g_refs, m_refs, v_refs = (refs[k * n:(k + 1) * n] for k in range(4))
        d_refs, mo_refs, vo_refs = (refs[(4 + k) * n:(5 + k) * n] for k in range(3))
        for i in range(n):
            d_refs[i][...], mo_refs[i][...], vo_refs[i][...] = _adamw_math(w_refs[i][...], g_refs[i][...], m_refs[i][...], v_refs[i][...])

    shapes = [jax.ShapeDtypeStruct(a.shape, F32) for a in ws]
    outs = pl.pallas_call(body, out_shape=shapes * 3, name="adamw_small", compiler_params=_params())(*ws, *gs, *ms, *vs)
    return outs[:n], outs[n:2 * n], outs[2 * n:]


def kernel(x, mem, ffn1_norm, ffn1_w_up, ffn1_w_down, mix_norm, mem_norm, w_in, b_gate, conv_dw_w, conv_dw_b, conv_ln_g, conv_ln_b, conv_w_pw, att_rel_bias, att_w_o, mem_w_kv, mem_w_o, w_out, ffn2_norm, ffn2_w_up, ffn2_w_down, final_norm, loss_target, m_ffn1_norm, m_ffn1_w_up, m_ffn1_w_down, m_mix_norm, m_mem_norm, m_w_in, m_b_gate, m_conv_dw_w, m_conv_dw_b, m_conv_ln_g, m_conv_ln_b, m_conv_w_pw, m_att_rel_bias, m_att_w_o, m_mem_w_kv, m_mem_w_o, m_w_out, m_ffn2_norm, m_ffn2_w_up, m_ffn2_w_down, m_final_norm, v_ffn1_norm, v_ffn1_w_up, v_ffn1_w_down, v_mix_norm, v_mem_norm, v_w_in, v_b_gate, v_conv_dw_w, v_conv_dw_b, v_conv_ln_g, v_conv_ln_b, v_conv_w_pw, v_att_rel_bias, v_att_w_o, v_mem_w_kv, v_mem_w_o, v_w_out, v_ffn2_norm, v_ffn2_w_up, v_ffn2_w_down, v_final_norm):
    given = dict(locals())
    w = {n: given[n] for n in WEIGHTS}
    mom = {n: given["m_" + n] for n in WEIGHTS}
    var = {n: given["v_" + n] for n in WEIGHTS}

    NB, S, _ = x.shape
    T = NB * S
    ML = mem.shape[1]
    x0 = x.reshape(T, D)
    target = loss_target.reshape(T, D)
    mem2 = mem.reshape(NB * ML, D)

    def block(t, n):
        return jnp.transpose(t[0]) if n in TRANSPOSED else t[0]

    sh = dict(zip(BIG_ORDER, _cast_shards([block(w[n], n) for n in BIG_ORDER])))
    dw_t = jnp.transpose(conv_dw_w[0])

    def gather(names, extra=(), extra_kinds=()):
        return _gather_ride([sh[n] for n in names] + list(extra), [BIG[n] for n in names] + list(extra_kinds))

    W = {}
    names0 = ['ffn1_w_up', 'ffn1_w_down']
    tab, got = _bias_table(att_rel_bias[0], ride=gather(names0, [dw_t], [('row', dw_t.shape[0])]))
    W.update(zip(names0, got[:2]))
    dw_full = jnp.transpose(got[2])
    conv_vec = jnp.concatenate([conv_dw_b, conv_ln_g, conv_ln_b, jnp.zeros((5, CONV_W), F32)], axis=0)
    fin_g = final_norm.reshape(1, D)

    names1 = ['w_in', 'conv_w_pw', 'att_w_o', 'mem_w_kv', 'mem_w_o', 'w_out']
    (x1, ab1), got = _ffn_fwd(x0, ffn1_norm, W['ffn1_w_up'], W['ffn1_w_down'], TILE_FFN_FWD, "ffn1_fwd", ride=gather(names1))
    W.update(zip(names1, got))
    (uc, qkv, mq, gl, hmix), _ = _mix_fwd(x1, mix_norm, W['w_in'], TILE_TOKENS)
    uc3 = uc.reshape(NB, S, 2 * CONV_W)
    qkv3 = qkv.reshape(NB, S, 3 * ATT_W)
    mq3 = mq.reshape(NB, S, MEM_W)
    cact, conv_z = _conv_fwd(uc3, dw_full, conv_vec)
    cact = cact.reshape(T, CONV_W)
    names2 = ['ffn2_w_up', 'ffn2_w_down']
    oatt3, att_lse, got = _att_fwd(qkv3, tab, ride=gather(names2))
    W.update(zip(names2, got))
    oatt = oatt3.reshape(T, ATT_W)
    memh, kv = _memkv_fwd(mem2, mem_norm, W['mem_w_kv'], TILE_TOKENS)
    kv3 = kv.reshape(NB, ML, 2 * MEM_W)
    omem = _mematt_fwd(mq3, kv3, TILE_TOKENS).reshape(T, MEM_W)
    branch_w = (W['conv_w_pw'], W['att_w_o'], W['mem_w_o'], W['w_out'])
    x2, ymix = _combine_fwd(x1, cact, oatt, omem, gl, b_gate, *branch_w, TILE_TOKENS)
    dx3, ab2, loss_part, dg_final = _ffn_fwd_loss(x2, ffn2_norm, W['ffn2_w_up'], W['ffn2_w_down'], fin_g, target, TILE_FFN_FWD,
                                                  "ffn2_fwd_loss")

    def scatter(grads, names):
        return _scatter_ride(grads, [BIG[n] for n in names])

    G, P = {}, {}
    dx2, dab2, act2, h2, dg_ffn2 = _ffn_bwd(x2, dx3, ab2, ffn2_norm, W['ffn2_w_up'], W['ffn2_w_down'], TILE_FFN, "ffn2_bwd")
    g_up, _ = _tn_matmul(dab2, h2, 512, "grad_ffn2_w_up_a", tt=TILE_GRAD_TOKENS_WIDE, x_part=(0, 2), out_rows=2 * FF)
    G['ffn2_w_up'], _ = _tn_matmul(dab2, h2, 512, "grad_ffn2_w_up_b", tt=TILE_GRAD_TOKENS_WIDE, x_part=(1, 2), out_rows=2 * FF,
                                   prev=g_up)
    G['ffn2_w_down'], _ = _tn_matmul(act2, dx3, 512, "grad_ffn2_w_down", scale=0.5, tt=TILE_GRAD_TOKENS_WIDE)
    (dgl, dcact, doatt, domem, dyc, dya, dym, dbg), got = _combine_bwd(
        dx2, cact, oatt, omem, gl, b_gate, *branch_w, TILE_COMBINE, ride=scatter([G['ffn2_w_up']], ['ffn2_w_up']))
    P['ffn2_w_up'] = got
    G['w_out'], _ = _tn_matmul(ymix, dx2, D, "grad_w_out", tt=TILE_GRAD_TOKENS_WIDE)
    G['conv_w_pw'], _ = _tn_matmul(cact, dyc, D, "grad_conv_w_pw")
    G['att_w_o'], _ = _tn_matmul(oatt, dya, D, "grad_att_w_o")
    G['mem_w_o'], _ = _tn_matmul(omem, dym, D, "grad_mem_w_o")
    dmq3, dkv3 = _mematt_bwd(mq3, kv3, domem.reshape(NB, S, MEM_W), TILE_TOKENS)
    dkv = dkv3.reshape(NB * ML, 2 * MEM_W)
    dg_mem = _memkv_bwd(mem2, dkv, W['mem_w_kv'], TILE_TOKENS)
    G['mem_w_kv'], _ = _tn_matmul(memh, dkv, 512, "grad_mem_w_kv")
    names = ['ffn2_w_down', 'w_out', 'conv_w_pw', 'att_w_o', 'mem_w_o']
    (dqkv3, dscore), got = _att_bwd(qkv3, oatt3, att_lse, doatt.reshape(NB, S, ATT_W), tab,
                                    ride=scatter([G[n] for n in names], names))
    P.update((n, [p]) for n, p in zip(names, got))
    d_rel = _rel_bias_grad(dscore)
    (duc3, d_dw, d_cvec), got = _conv_bwd(uc3, conv_z, dcact.reshape(NB, S, CONV_W), dw_full, conv_vec,
                                          ride=scatter([G['mem_w_kv']], ['mem_w_kv']))
    P['mem_w_kv'] = got
    duc, dqkv, dmq = duc3.reshape(T, 2 * CONV_W), dqkv3.reshape(T, 3 * ATT_W), dmq3.reshape(T, MEM_W)
    g_in, _ = _tn_matmul(hmix, duc, 1024, "grad_w_in_conv", out_cols=IN_COLS, col_off=0)
    g_in, _ = _tn_matmul(hmix, dqkv, 512, "grad_w_in_qkv", out_cols=IN_COLS, col_off=1024, prev=g_in)
    g_in, _ = _tn_matmul(hmix, dmq, 512, "grad_w_in_mq", out_cols=IN_COLS, col_off=2560, prev=g_in)
    G['w_in'], _ = _tn_matmul(hmix, dgl, 1024, "grad_w_in_gate", out_cols=IN_COLS, col_off=3072, prev=g_in)
    def start_scatter(g, name, tag):
        kind = BIG[name]
        return _scatter_start(g, _own_block(g, *kind, tag), *kind, tag) + (kind, tag)

    def wait_scatter(started, after):
        send_sems, recv_sems, g_thru, land_thru, _, kind, tag = started
        return _scatter_wait(send_sems, recv_sems, g_thru, land_thru, after, *kind, tag)

    ex_in = start_scatter(G['w_in'], 'w_in', "w_in")
    (dx1, dg_mix), _ = _mix_bwd(x1, dx2, duc, dqkv, dmq, dgl, mix_norm, W['w_in'], TILE_TOKENS, after=ex_in[4])
    dx0, dab1, act1, h1, dg_ffn1 = _ffn_bwd(x0, dx1, ab1, ffn1_norm, W['ffn1_w_up'], W['ffn1_w_down'], TILE_FFN, "ffn1_bwd")
    g_wd1, _ = _tn_matmul(act1, dx1, 512, "grad_ffn1_w_down", scale=0.5, tt=TILE_GRAD_TOKENS_WIDE)
    ex_wd = start_scatter(g_wd1, 'ffn1_w_down', "ffn1_w_down")
    g_wu1a, _ = _tn_matmul(dab1, h1, 512, "grad_ffn1_w_up_a", tt=TILE_GRAD_TOKENS_WIDEST, y_part=(0, 2), after=ex_wd[4])
    ex_a = start_scatter(g_wu1a, 'ffn1_w_up', "ffn1_w_up_a")
    g_wu1b, _ = _tn_matmul(dab1, h1, 512, "grad_ffn1_w_up_b", tt=TILE_GRAD_TOKENS_WIDEST, y_part=(1, 2), after=ex_a[4])
    ex_b = start_scatter(g_wu1b, 'ffn1_w_up', "ffn1_w_up_b")
    token = ex_b[4]

    small_names = ['loss', 'ffn1_norm', 'mix_norm', 'mem_norm', 'b_gate', 'conv_dw_w', 'conv_vec', 'att_rel_bias', 'ffn2_norm',
                   'final_norm']
    small = dict(zip(small_names, _all_sum_small(
        [loss_part + token[0:1], dg_ffn1, dg_mix, dg_mem, dbg, d_dw, d_cvec, d_rel, dg_ffn2, dg_final])))
    loss = small['loss'][0, 0]
    me = _dev_index(*_my_coords())
    for i, n in enumerate(['conv_dw_b', 'conv_ln_g', 'conv_ln_b']):
        small[n] = small['conv_vec'][i:i + 1]
    small['conv_dw_w'] = lax.dynamic_slice(small['conv_dw_w'], (0, me * conv_dw_w.shape[2]), (CONV_K, conv_dw_w.shape[2]))
    little = [n for n in WEIGHTS if n not in BIG]
    as2d = lambda t, n: t.reshape(small[n].shape)
    d_s, m_s, v_s = _adamw_small([as2d(w[n], n) for n in little], [small[n] for n in little],
                                 [as2d(mom[n], n) for n in little], [as2d(var[n], n) for n in little])
    grad, delta, new_m, new_v = {}, {}, {}, {}
    for i, n in enumerate(little):
        grad[n], delta[n], new_m[n], new_v[n] = (t.reshape(w[n].shape) for t in (small[n], d_s[i], m_s[i], v_s[i]))
    done = [d_s[0]]
    waited = {'w_in': [ex_in], 'ffn1_w_down': [ex_wd], 'ffn1_w_up': [ex_a, ex_b]}
    order = [n for n in BIG_ORDER if n not in waited] + list(waited)
    for n in order:
        if n in waited:
            P[n] = [wait_scatter(ex, done) for ex in waited[n]]
        outs = _sum_adamw(P[n], block(w[n], n), block(mom[n], n), block(var[n], n), "adamw_" + n,
                          after=None if n in waited else token)
        done.append(outs[0])
        grad[n], delta[n], new_m[n], new_v[n] = ((jnp.transpose(t) if n in TRANSPOSED else t)[None] for t in outs)

    return (loss, dx0.reshape(NB, S, D), *[grad[n] for n in WEIGHTS], *[delta[n] for n in WEIGHTS],
            *[new_m[n] for n in WEIGHTS], *[new_v[n] for n in WEIGHTS])
```

```python
import functools

import jax
import jax.numpy as jnp
from jax import lax
from jax.experimental import pallas as pl
from jax.experimental.pallas import tpu as pltpu

F32 = jnp.float32
BF16 = jnp.bfloat16

EPS = 1e-6
MASK_VALUE = -1e30
D = 1024
NDEV = 8
FF = 2816
FF_SHARD = 704
FF_HALF_ROWS = 352
FF_BLOCK_EDGES = ()
IN_COLS = 6144
CONV_W = 512
CONV_K = 31
CONV_HALO = 32
CONV_CHUNK = 32
CONV_WIN = CONV_CHUNK + 40
GLU_CHUNK = 128
ATT_W = 512
ATT_HEADS = 8
ATT_HD = 64
CHUNK = 64
LEFT_CHUNKS = 8
MAX_REL = 128
N_REL = 192
QB = 256
KWIN = QB + LEFT_CHUNKS * CHUNK
KPAD = LEFT_CHUNKS * CHUNK
DS_LANES = 1024
MEM_W = 512
MEM_HEADS = 4
MEM_HD = 128
ADAM_LR = 0.001
ADAM_B1 = 0.9
ADAM_B2 = 0.999
ADAM_EPS = 1e-08
ADAM_WD = 0.01
ADAM_STEP = 10
VMEM_LIMIT = 60 * 1024 * 1024
TILE_FFN = 256
TILE_FFN_FWD = 512
TILE_COMBINE = 256
TILE_TOKENS = 512
TILE_GRAD_TOKENS = 2048
TILE_GRAD_TOKENS_WIDE = 1024
TILE_GRAD_TOKENS_WIDEST = 512

MESH = pl.DeviceIdType.MESH
ANY = pl.BlockSpec(memory_space=pl.ANY)

WEIGHTS = ['ffn1_norm', 'ffn1_w_up', 'ffn1_w_down', 'mix_norm', 'mem_norm', 'w_in', 'b_gate', 'conv_dw_w', 'conv_dw_b',
           'conv_ln_g', 'conv_ln_b', 'conv_w_pw', 'att_rel_bias', 'att_w_o', 'mem_w_kv', 'mem_w_o', 'w_out', 'ffn2_norm',
           'ffn2_w_up', 'ffn2_w_down', 'final_norm']
BIG = {
    'ffn1_w_up': ('row', FF_SHARD), 'ffn1_w_down': ('row', FF_HALF_ROWS), 'w_in': ('col', 768),
    'conv_w_pw': ('col', 128), 'att_w_o': ('col', 128), 'mem_w_kv': ('row', 128), 'mem_w_o': ('col', 128),
    'w_out': ('row', 128), 'ffn2_w_up': ('row', FF_SHARD), 'ffn2_w_down': ('row', FF_HALF_ROWS),
}
BIG_ORDER = ['ffn1_w_up', 'ffn1_w_down', 'w_in', 'conv_w_pw', 'att_w_o', 'mem_w_kv', 'mem_w_o', 'w_out', 'ffn2_w_up', 'ffn2_w_down']
TRANSPOSED = ('ffn1_w_up', 'ffn2_w_up')


def _dot(a, b):
    return jnp.dot(a, b, preferred_element_type=F32)


def _dot_nt(a, b):
    return lax.dot_general(a, b, (((1,), (1,)), ((), ())), preferred_element_type=F32)


def _dot_tn(a, b):
    return lax.dot_general(a, b, (((0,), (0,)), ((), ())), preferred_element_type=F32)


def _sigmoid(v):
    return jax.nn.sigmoid(v)


def _const(shape):
    return pl.BlockSpec(shape, lambda *_: (0,) * len(shape), pipeline_mode=pl.Buffered(1))


def _params(*sem):
    return pltpu.CompilerParams(dimension_semantics=sem if sem else None, vmem_limit_bytes=VMEM_LIMIT)


def _my_coords():
    return lax.axis_index("x"), lax.axis_index("y"), lax.axis_index("c")


def _dev_index(px, py, pc):
    return 4 * px + 2 * py + pc


def _window(ref, kind, n, p):
    if kind == 'row':
        return ref.at[pl.ds(pl.multiple_of(p * n, n), n), :]
    return ref.at[:, pl.ds(pl.multiple_of(p * n, 128), n)]


def _full_shape(kind, n, shard_shape):
    if kind == 'row':
        return (NDEV * n, shard_shape[1])
    return (shard_shape[0], NDEV * n)


def _cast_shards(shards):
    n = len(shards)

    def body(*refs):
        for i in range(n):
            refs[n + i][...] = refs[i][...].astype(BF16)

    out_shape = [jax.ShapeDtypeStruct(s.shape, BF16) for s in shards]
    return pl.pallas_call(body, out_shape=out_shape, name="cast_shards", compiler_params=_params())(*shards)


class _Ride:
    def __init__(self, inputs, out_shape, scratch, start, finish, mids=()):
        self.inputs, self.out_shape, self.scratch = list(inputs), list(out_shape), list(scratch)
        self.start, self.finish, self.mids = start, finish, tuple(mids)


def _pallas(body, name, grid, in_specs, out_specs, out_shape, args, scratch_shapes=(), sem=None, aliases=None, ride=None,
            after=None):
    if ride is None:
        n_in, n_dep = len(args), 0 if after is None else 1

        def kernel_body(*refs):
            body(*refs[:n_in], *refs[n_in + n_dep:])

        outs = pl.pallas_call(kernel_body if n_dep else body, grid=grid, name=name, in_specs=list(in_specs) + [ANY] * n_dep,
                              out_specs=out_specs, out_shape=out_shape, scratch_shapes=list(scratch_shapes),
                              input_output_aliases=aliases or {}, compiler_params=_params(*sem),
                              )(*args, *([after] if n_dep else []))
        return list(outs), []
    n_in, n_out, n_scr = len(args), len(out_shape), len(scratch_shapes)
    r_in, r_out = len(ride.inputs), len(ride.out_shape)

    def wrapped(*refs):
        k_in, rin = refs[:n_in], refs[n_in:n_in + r_in]
        o0 = n_in + r_in
        k_out, rout = refs[o0:o0 + n_out], refs[o0 + n_out:o0 + n_out + r_out]
        s0 = o0 + n_out + r_out
        k_scr, rscr = refs[s0:s0 + n_scr], refs[s0 + n_scr:]
        ids = [pl.program_id(k) for k in range(len(grid))]
        first = functools.reduce(jnp.logical_and, [i == 0 for i in ids])
        last = functools.reduce(jnp.logical_and, [i == g - 1 for i, g in zip(ids, grid)])
        pl.when(first)(lambda: ride.start(rin, rout, rscr))
        single_step = all(g == 1 for g in grid)
        for quarter, mid in ride.mids:
            if not single_step:
                at_mid = functools.reduce(jnp.logical_and, [ids[0] == (quarter * grid[0]) // 4] + [i == 0 for i in ids[1:]])
                pl.when(at_mid)(functools.partial(mid, rin, rout, rscr))
        body(*k_in, *k_out, *k_scr)
        for _, mid in ride.mids:
            if single_step:
                mid(rin, rout, rscr)
        pl.when(last)(lambda: ride.finish(rin, rout, rscr))

    outs = pl.pallas_call(
        wrapped, grid=grid, name=name, in_specs=list(in_specs) + [ANY] * r_in, out_specs=list(out_specs) + [ANY] * r_out,
        out_shape=list(out_shape) + ride.out_shape, scratch_shapes=list(scratch_shapes) + ride.scratch,
        input_output_aliases=aliases or {}, compiler_params=_params(*(["arbitrary"] * len(grid))),
    )(*args, *ride.inputs)
    return list(outs[:n_out]), list(outs[n_out:])


def _gather_ride(shards, kinds):
    n = len(shards)

    def plan(rin, out, sems):
        send_sems, recv_sems, local_sems = sems[:3]
        x, y, c = _my_coords()
        me, sibling = (x, y, c), (x, y, 1 - c)
        xn, yn, diag = (1 - x, y), (x, 1 - y), (1 - x, 1 - y)

        def win(i, dev):
            return _window(out[i], kinds[i][0], kinds[i][1], _dev_index(*dev))

        def copy(i, k, block, to, from_shard=False):
            return pltpu.make_async_remote_copy(
                src_ref=rin[i] if from_shard else win(i, block), dst_ref=win(i, block),
                send_sem=send_sems.at[i, k], recv_sem=recv_sems.at[i, k], device_id=to, device_id_type=MESH)

        def each(fn):
            return [fn(i) for i in range(n)]

        return dict(
            local=lambda: each(lambda i: pltpu.make_async_copy(rin[i], win(i, me), local_sems.at[i])),
            own=lambda: [cp for i in range(n) for cp in (copy(i, 0, me, sibling, True), copy(i, 1, me, (*xn, c), True),
                                                         copy(i, 2, me, (*yn, c), True))],
            from_x=lambda: each(lambda i: copy(i, 1, (*xn, c), me)),
            from_y=lambda: each(lambda i: copy(i, 2, (*yn, c), me)),
            x_block_on_to_y=lambda: each(lambda i: copy(i, 3, (*xn, c), (*yn, c))),
            y_block_on_to_x=lambda: each(lambda i: copy(i, 3, (*yn, c), (*xn, c))),
            from_diag=lambda: each(lambda i: copy(i, 3, (*diag, c), me)),
            to_sibling=lambda: [copy(i, 4 + j, (*chip, c), sibling) for j, chip in enumerate((xn, yn, diag)) for i in range(n)],
            from_sibling=lambda: [cp for i in range(n) for cp in
                                  [copy(i, 0, sibling, me)] + [copy(i, 4 + j, (*chip, 1 - c), me) for j, chip in enumerate((xn, yn, diag))]],
            north=c == 1)

    def start(rin, out, sems):
        p = plan(rin, out, sems)
        for cp in p['local']() + p['own']():
            cp.start()

    def pass_diagonal(rin, out, sems):
        p = plan(rin, out, sems)

        @pl.when(p['north'])
        def _():
            for got, fwd in zip(p['from_x'](), p['x_block_on_to_y']()):
                got.wait_recv()
                fwd.start()

        @pl.when(jnp.logical_not(p['north']))
        def _():
            for got, fwd in zip(p['from_y'](), p['y_block_on_to_x']()):
                got.wait_recv()
                fwd.start()

    def pass_to_sibling(rin, out, sems):
        p = plan(rin, out, sems)

        @pl.when(p['north'])
        def _():
            for cp in p['from_y']():
                cp.wait_recv()

        @pl.when(jnp.logical_not(p['north']))
        def _():
            for cp in p['from_x']():
                cp.wait_recv()
        for cp in p['from_diag']():
            cp.wait_recv()
        for cp in p['to_sibling']():
            cp.start()

    def finish(rin, out, sems):
        p = plan(rin, out, sems)
        for cp in p['from_sibling']():
            cp.wait_recv()
        for cp in p['own']() + p['to_sibling']():
            cp.wait_send()

        @pl.when(p['north'])
        def _():
            for cp in p['x_block_on_to_y']():
                cp.wait_send()

        @pl.when(jnp.logical_not(p['north']))
        def _():
            for cp in p['y_block_on_to_x']():
                cp.wait_send()
        for cp in p['local']():
            cp.wait()

    out_shape = [jax.ShapeDtypeStruct(_full_shape(k, m, s.shape), s.dtype) for s, (k, m) in zip(shards, kinds)]
    scratch = [pltpu.SemaphoreType.DMA((n, 7)), pltpu.SemaphoreType.DMA((n, 7)), pltpu.SemaphoreType.DMA((n,))]
    return _Ride(shards, out_shape, scratch, start, finish, mids=((2, pass_diagonal), (3, pass_to_sibling)))


def _scatter_ride(grads, kinds):
    n = len(grads)

    def plan(g, out, sems):
        send_sems, recv_sems, local_sems = sems
        x, y, c = _my_coords()
        me = _dev_index(x, y, c)

        def local():
            return [pltpu.make_async_copy(_window(g[i], kinds[i][0], kinds[i][1], me), out[i].at[me], local_sems.at[i])
                    for i in range(n)]

        def remote(arrival):
            cps = []
            for rel in range(1, NDEV):
                peer = _peer(x, y, c, rel)
                dev = _dev_index(*peer)
                for i in range(n):
                    kind, m = kinds[i]
                    cps.append(pltpu.make_async_remote_copy(
                        src_ref=_window(g[i], kind, m, me if arrival else dev), dst_ref=out[i].at[dev if arrival else me],
                        send_sem=send_sems.at[i, rel - 1], recv_sem=recv_sems.at[i, rel - 1], device_id=peer, device_id_type=MESH))
            return cps

        return local, remote

    def start(g, out, sems):
        local, remote = plan(g, out, sems)
        for cp in local() + remote(False):
            cp.start()

    def finish(g, out, sems):
        local, remote = plan(g, out, sems)
        for cp in remote(True):
            cp.wait_recv()
        for cp in remote(False):
            cp.wait_send()
        for cp in local():
            cp.wait()

    def block_shape(gr, kind, m):
        return (m, gr.shape[1]) if kind == 'row' else (gr.shape[0], m)

    out_shape = [jax.ShapeDtypeStruct((NDEV,) + block_shape(gr, k, m), gr.dtype) for gr, (k, m) in zip(grads, kinds)]
    scratch = [pltpu.SemaphoreType.DMA((n, NDEV - 1)), pltpu.SemaphoreType.DMA((n, NDEV - 1)), pltpu.SemaphoreType.DMA((n,))]
    return _Ride(grads, out_shape, scratch, start, finish)


def _rms_stats(xf):
    r = lax.rsqrt(jnp.mean(xf * xf, axis=-1, keepdims=True) + EPS)
    return xf * r, r


def _rms_bwd(dh, g, xhat, r):
    dxhat = dh * g
    return r * (dxhat - xhat * jnp.mean(dxhat * xhat, axis=-1, keepdims=True))


def _ffn_blocks():
    edges = (0,) + FF_BLOCK_EDGES + (FF,)
    return [(slice(lo, hi), slice(FF + lo, FF + hi)) for lo, hi in zip(edges[:-1], edges[1:])]


def _swiglu_tile(x_ref, g_ref, wut_ref, wd_ref, ab_ref):
    xf = x_ref[...]
    xhat, _ = _rms_stats(xf)
    h = (xhat * g_ref[...]).astype(BF16)
    acc = jnp.zeros(xf.shape, F32)
    for ra, rb in _ffn_blocks():
        a = _dot_nt(h, wut_ref[ra, :])
        b = _dot_nt(h, wut_ref[rb, :])
        ab_ref[:, ra] = a.astype(BF16)
        ab_ref[:, rb] = b.astype(BF16)
        act = (a * _sigmoid(a) * b).astype(BF16)
        acc = acc + _dot(act, wd_ref[ra, :])
    return xf + 0.5 * acc


def _ffn_fwd(x, g, wut, wd, tm, name, ride=None):
    T = x.shape[0]

    def body(x_ref, g_ref, wut_ref, wd_ref, xo_ref, ab_ref):
        xo_ref[...] = _swiglu_tile(x_ref, g_ref, wut_ref, wd_ref, ab_ref)

    return _pallas(
        body, name, (T // tm,),
        [pl.BlockSpec((tm, D), lambda t: (t, 0)), _const((1, D)), _const((2 * FF, D)), _const((FF, D))],
        [pl.BlockSpec((tm, D), lambda t: (t, 0)), pl.BlockSpec((tm, 2 * FF), lambda t: (t, 0))],
        [jax.ShapeDtypeStruct((T, D), F32), jax.ShapeDtypeStruct((T, 2 * FF), BF16)],
        (x, g, wut, wd), sem=("arbitrary",), ride=ride)


def _ffn_fwd_loss(x, g, wut, wd, g_final, target, tm, name):
    T = x.shape[0]

    def body(x_ref, g_ref, wut_ref, wd_ref, gf_ref, t_ref, dx_ref, ab_ref, loss_ref, dgf_ref):
        xhat, r = _rms_stats(_swiglu_tile(x_ref, g_ref, wut_ref, wd_ref, ab_ref))
        gain = gf_ref[...]
        diff = xhat * gain - t_ref[...]
        dout = diff * (1.0 / D)

        @pl.when(pl.program_id(0) == 0)
        def _():
            loss_ref[...] = jnp.zeros_like(loss_ref)
            dgf_ref[...] = jnp.zeros_like(dgf_ref)
        sq = jnp.sum(jnp.sum(diff * diff, axis=0, keepdims=True), axis=1, keepdims=True)
        loss_ref[...] += jnp.broadcast_to(sq * (0.5 / D), (1, 128))
        dgf_ref[...] += jnp.sum(dout * xhat, axis=0, keepdims=True)
        dx_ref[...] = _rms_bwd(dout, gain, xhat, r)

    row = pl.BlockSpec((tm, D), lambda t: (t, 0))
    return pl.pallas_call(
        body, grid=(T // tm,), name=name,
        in_specs=[row, _const((1, D)), _const((2 * FF, D)), _const((FF, D)), _const((1, D)), row],
        out_specs=[row, pl.BlockSpec((tm, 2 * FF), lambda t: (t, 0)), pl.BlockSpec((1, 128), lambda t: (0, 0)),
                   pl.BlockSpec((1, D), lambda t: (0, 0))],
        out_shape=[jax.ShapeDtypeStruct((T, D), F32), jax.ShapeDtypeStruct((T, 2 * FF), BF16),
                   jax.ShapeDtypeStruct((1, 128), F32), jax.ShapeDtypeStruct((1, D), F32)],
        compiler_params=_params("arbitrary"),
    )(x, g, wut, wd, g_final, target)


def _ffn_bwd(x, dy, ab, g, wut, wd, tm, name):
    T = x.shape[0]

    def body(x_ref, dy_ref, ab_ref, g_ref, wut_ref, wd_ref, dx_ref, dab_ref, act_ref, h_ref, dg_ref):
        xf = x_ref[...]
        xhat, r = _rms_stats(xf)
        gain = g_ref[...]
        h_ref[...] = (xhat * gain).astype(BF16)
        dy = dy_ref[...]
        dyh = (0.5 * dy).astype(BF16)
        dh = jnp.zeros((tm, D), F32)
        for ra, rb in _ffn_blocks():
            a = ab_ref[:, ra].astype(F32)
            b = ab_ref[:, rb].astype(F32)
            dact = _dot_nt(dyh, wd_ref[ra, :])
            sg = _sigmoid(a)
            sl = a * sg
            act_ref[:, ra] = (sl * b).astype(BF16)
            da = (dact * b * (sg * (1.0 + a * (1.0 - sg)))).astype(BF16)
            db = (dact * sl).astype(BF16)
            dab_ref[:, ra] = da
            dab_ref[:, rb] = db
            dh = dh + _dot(da, wut_ref[ra, :]) + _dot(db, wut_ref[rb, :])
        dx_ref[...] = dy + _rms_bwd(dh, gain, xhat, r)

        @pl.when(pl.program_id(0) == 0)
        def _():
            dg_ref[...] = jnp.zeros_like(dg_ref)
        dg_ref[...] += jnp.sum(dh * xhat, axis=0, keepdims=True)

    return pl.pallas_call(
        body, grid=(T // tm,), name=name,
        in_specs=[pl.BlockSpec((tm, D), lambda t: (t, 0)), pl.BlockSpec((tm, D), lambda t: (t, 0)),
                  pl.BlockSpec((tm, 2 * FF), lambda t: (t, 0)), _const((1, D)), _const((2 * FF, D)), _const((FF, D))],
        out_specs=[pl.BlockSpec((tm, D), lambda t: (t, 0)), pl.BlockSpec((tm, 2 * FF), lambda t: (t, 0)),
                   pl.BlockSpec((tm, FF), lambda t: (t, 0)), pl.BlockSpec((tm, D), lambda t: (t, 0)),
                   pl.BlockSpec((1, D), lambda t: (0, 0))],
        out_shape=[jax.ShapeDtypeStruct((T, D), F32), jax.ShapeDtypeStruct((T, 2 * FF), BF16),
                   jax.ShapeDtypeStruct((T, FF), BF16), jax.ShapeDtypeStruct((T, D), BF16), jax.ShapeDtypeStruct((1, D), F32)],
        compiler_params=_params("arbitrary"),
    )(x, dy, ab, g, wut, wd)


def _tn_matmul(xm, ym, tn, name, scale=None, out_cols=None, col_off=0, prev=None, tt=TILE_GRAD_TOKENS, x_part=(0, 1),
               out_rows=None, y_part=(0, 1), ride=None, after=None):
    T = xm.shape[0]
    xi, xn = x_part
    yi, yn = y_part
    K = xm.shape[1] // xn
    N = ym.shape[1] // yn
    out_cols = N if out_cols is None else out_cols
    row_blk = xi if out_rows is not None else 0
    out_rows = K if out_rows is None else out_rows
    tt = min(tt, T)
    nt = T // tt
    off = col_off // tn

    def body(*refs):
        x_ref, y_ref = refs[0], refs[1]
        o_ref, acc = refs[-2], refs[-1]

        @pl.when(pl.program_id(1) == 0)
        def _():
            acc[...] = jnp.zeros_like(acc)
        acc[...] += _dot_tn(x_ref[...].astype(BF16), y_ref[...].astype(BF16))

        @pl.when(pl.program_id(1) == nt - 1)
        def _():
            res = acc[...]
            o_ref[...] = (res if scale is None else res * scale).astype(BF16)

    ycol = yi * (N // tn)
    in_specs = [pl.BlockSpec((tt, K), lambda n, t: (t, xi)), pl.BlockSpec((tt, tn), lambda n, t: (t, n + ycol))]
    args = [xm, ym]
    aliases = {}
    if prev is not None:
        in_specs.append(ANY)
        args.append(prev)
        aliases = {2: 0}
    outs, rode = _pallas(
        body, name, (N // tn, nt), in_specs, [pl.BlockSpec((K, tn), lambda n, t: (row_blk, n + off))],
        [jax.ShapeDtypeStruct((out_rows, out_cols), BF16)], args, scratch_shapes=[pltpu.VMEM((K, tn), F32)],
        sem=("parallel", "arbitrary"), aliases=aliases, ride=ride, after=after)
    return outs[0], rode


def _mix_fwd(x, g, w_in, tm, ride=None):
    T = x.shape[0]

    def body(x_ref, g_ref, w_ref, uc_ref, qkv_ref, mq_ref, gl_ref, h_ref):
        xhat, _ = _rms_stats(x_ref[...])
        h = (xhat * g_ref[...]).astype(BF16)
        h_ref[...] = h
        uc_ref[...] = _dot(h, w_ref[:, 0:1024])
        qkv_ref[...] = _dot(h, w_ref[:, 1024:2560]).astype(BF16)
        mq_ref[...] = _dot(h, w_ref[:, 2560:3072]).astype(BF16)
        for j in range(3):
            gl_ref[:, j * D:(j + 1) * D] = _dot(h, w_ref[:, 3072 + j * D:3072 + (j + 1) * D]).astype(BF16)

    row = lambda w: pl.BlockSpec((tm, w), lambda t: (t, 0))
    return _pallas(
        body, "mix_fwd", (T // tm,), [row(D), _const((1, D)), _const((D, IN_COLS))],
        [row(1024), row(1536), row(512), row(3072), row(D)],
        [jax.ShapeDtypeStruct((T, 1024), F32), jax.ShapeDtypeStruct((T, 1536), BF16), jax.ShapeDtypeStruct((T, 512), BF16),
         jax.ShapeDtypeStruct((T, 3072), BF16), jax.ShapeDtypeStruct((T, D), BF16)],
        (x, g, w_in), sem=("parallel",), ride=ride)


def _mix_bwd(x, dres, duc, dqkv, dmq, dgl, g, w_in, tm, ride=None, after=None):
    T = x.shape[0]

    def body(x_ref, dres_ref, duc_ref, dqkv_ref, dmq_ref, dgl_ref, g_ref, w_ref, dx_ref, dg_ref):
        xhat, r = _rms_stats(x_ref[...])
        dh = _dot_nt(duc_ref[...], w_ref[:, 0:1024])
        dh = dh + _dot_nt(dqkv_ref[...], w_ref[:, 1024:2560])
        dh = dh + _dot_nt(dmq_ref[...], w_ref[:, 2560:3072])
        dh = dh + _dot_nt(dgl_ref[...], w_ref[:, 3072:6144])
        dx_ref[...] = dres_ref[...] + _rms_bwd(dh, g_ref[...], xhat, r)

        @pl.when(pl.program_id(0) == 0)
        def _():
            dg_ref[...] = jnp.zeros_like(dg_ref)
        dg_ref[...] += jnp.sum(dh * xhat, axis=0, keepdims=True)

    row = lambda w: pl.BlockSpec((tm, w), lambda t: (t, 0))
    return _pallas(
        body, "mix_bwd", (T // tm,),
        [row(D), row(D), row(1024), row(1536), row(512), row(3072), _const((1, D)), _const((D, IN_COLS))],
        [row(D), pl.BlockSpec((1, D), lambda t: (0, 0))],
        [jax.ShapeDtypeStruct((T, D), F32), jax.ShapeDtypeStruct((1, D), F32)],
        (x, dres, duc, dqkv, dmq, dgl, g, w_in), sem=("arbitrary",), ride=ride, after=after)


def _shifted(win, base, copies):
    for k in range(8):
        copies[k] = win[base + k:base + k + CONV_CHUNK + 24]
    return copies


def _tap_slices(copies, tap):
    out = []
    for k in range(8):
        for a in range(4):
            j = tap(a, k)
            if 0 <= j < CONV_K:
                out.append((j, copies[k, pl.ds(8 * a, CONV_CHUNK), :]))
    return out


def _conv_taps(copies, w_ref, tap):
    acc = jnp.zeros((CONV_CHUNK, CONV_W), F32)
    for j, rows in _tap_slices(copies, tap):
        acc = acc + rows * w_ref[j:j + 1, :]
    return acc


def _fold8(v):
    acc = v[0:8]
    for r in range(8, CONV_CHUNK, 8):
        acc = acc + v[r:r + 8]
    return acc


def _glu_into(uc_ref, vpad, S):
    vpad[pl.ds(0, CONV_HALO), :] = jnp.zeros((CONV_HALO, CONV_W), F32)
    vpad[pl.ds(S + CONV_HALO, CONV_HALO), :] = jnp.zeros((CONV_HALO, CONV_W), F32)

    def glu(i, carry):
        r0 = pl.multiple_of(i * GLU_CHUNK, GLU_CHUNK)
        a = uc_ref[0, pl.ds(r0, GLU_CHUNK), 0:CONV_W]
        gt = uc_ref[0, pl.ds(r0, GLU_CHUNK), CONV_W:2 * CONV_W]
        vpad[pl.ds(pl.multiple_of(r0 + CONV_HALO, CONV_HALO), GLU_CHUNK), :] = a * _sigmoid(gt)
        return carry
    lax.fori_loop(0, S // GLU_CHUNK, glu, 0)


def _layer_norm(z, vec_ref):
    xc = z - jnp.mean(z, axis=-1, keepdims=True)
    rstd = lax.rsqrt(jnp.mean(xc * xc, axis=-1, keepdims=True) + EPS)
    xn = xc * rstd
    return xn, rstd, xn * vec_ref[1:2, :] + vec_ref[2:3, :]


def _conv_fwd(uc, dw_w, vec):
    NB, S, _ = uc.shape

    def body(uc_ref, w_ref, vec_ref, o_ref, z_ref, vpad, copies):
        _glu_into(uc_ref, vpad, S)

        def conv(i, carry):
            r0 = pl.multiple_of(i * CONV_CHUNK, CONV_CHUNK)
            win = vpad[pl.ds(r0, CONV_WIN), :]
            z = _conv_taps(_shifted(win, CONV_HALO - (CONV_K - 1), copies), w_ref, lambda a, k: 8 * a + k) + vec_ref[0:1, :]
            z_ref[0, pl.ds(r0, CONV_CHUNK), :] = z
            _, _, yln = _layer_norm(z, vec_ref)
            o_ref[0, pl.ds(r0, CONV_CHUNK), :] = (yln * _sigmoid(yln)).astype(BF16)
            return carry
        lax.fori_loop(0, S // CONV_CHUNK, conv, 0, unroll=4)

    seq = pl.BlockSpec((1, S, CONV_W), lambda b: (b, 0, 0))
    return pl.pallas_call(
        body, grid=(NB,), name="conv_fwd",
        in_specs=[pl.BlockSpec((1, S, 2 * CONV_W), lambda b: (b, 0, 0)), _const((CONV_K, CONV_W)), _const((8, CONV_W))],
        out_specs=[seq, seq],
        out_shape=[jax.ShapeDtypeStruct((NB, S, CONV_W), BF16), jax.ShapeDtypeStruct((NB, S, CONV_W), F32)],
        scratch_shapes=[pltpu.VMEM((S + 2 * CONV_HALO, CONV_W), F32), pltpu.VMEM((8, CONV_CHUNK + 24, CONV_W), F32)],
        compiler_params=_params("parallel"),
    )(uc, dw_w, vec)


def _conv_bwd(uc, z, dcact, dw_w, vec, ride=None):
    NB, S, _ = uc.shape
    n_chunks = S // CONV_CHUNK

    def body(uc_ref, z_ref, dc_ref, w_ref, vec_ref, duc_ref, dw_ref, dvec_ref, vpad, dzpad, dw8, dvec8, copies):
        @pl.when(pl.program_id(0) == 0)
        def _():
            dw8[...] = jnp.zeros_like(dw8)
            dvec8[...] = jnp.zeros_like(dvec8)
        _glu_into(uc_ref, vpad, S)
        dzpad[pl.ds(S, 2 * CONV_HALO), :] = jnp.zeros((2 * CONV_HALO, CONV_W), F32)

        def norm_bwd(i, carry):
            r0 = pl.multiple_of(i * CONV_CHUNK, CONV_CHUNK)
            xn, rstd, yln = _layer_norm(z_ref[0, pl.ds(r0, CONV_CHUNK), :], vec_ref)
            sg = _sigmoid(yln)
            dyln = dc_ref[0, pl.ds(r0, CONV_CHUNK), :] * (sg * (1.0 + yln * (1.0 - sg)))
            dxn = dyln * vec_ref[1:2, :]
            dz = rstd * (dxn - jnp.mean(dxn, axis=-1, keepdims=True) - xn * jnp.mean(dxn * xn, axis=-1, keepdims=True))
            dzpad[pl.ds(r0, CONV_CHUNK), :] = dz
            dvec8[0] += _fold8(dz)
            dvec8[1] += _fold8(dyln * xn)
            dvec8[2] += _fold8(dyln)
            return carry
        lax.fori_loop(0, n_chunks, norm_bwd, 0, unroll=4)

        def taps_bwd(i, carry):
            r0 = pl.multiple_of(i * CONV_CHUNK, CONV_CHUNK)
            dzwin = dzpad[pl.ds(r0, CONV_WIN), :]
            dv = _conv_taps(_shifted(dzwin, 0, copies), w_ref, lambda a, k: CONV_K - 1 - 8 * a - k)
            dz = dzwin[0:CONV_CHUNK]
            vwin = vpad[pl.ds(r0, CONV_WIN), :]
            for j, rows in _tap_slices(_shifted(vwin, CONV_HALO - (CONV_K - 1), copies), lambda a, k: 8 * a + k):
                dw8[j] += _fold8(dz * rows)
            a = uc_ref[0, pl.ds(r0, CONV_CHUNK), 0:CONV_W]
            sg = _sigmoid(uc_ref[0, pl.ds(r0, CONV_CHUNK), CONV_W:2 * CONV_W])
            duc_ref[0, pl.ds(r0, CONV_CHUNK), 0:CONV_W] = (dv * sg).astype(BF16)
            duc_ref[0, pl.ds(r0, CONV_CHUNK), CONV_W:2 * CONV_W] = (dv * a * sg * (1.0 - sg)).astype(BF16)
            return carry
        lax.fori_loop(0, n_chunks, taps_bwd, 0, unroll=2)

        @pl.when(pl.program_id(0) == NB - 1)
        def _():
            dw_ref[...] = jnp.zeros_like(dw_ref)
            dvec_ref[...] = jnp.zeros_like(dvec_ref)
            for j in range(CONV_K):
                dw_ref[j:j + 1, :] = jnp.sum(dw8[j], axis=0, keepdims=True)
            for j in range(3):
                dvec_ref[j:j + 1, :] = jnp.sum(dvec8[j], axis=0, keepdims=True)

    return _pallas(
        body, "conv_bwd", (NB,),
        [pl.BlockSpec((1, S, 2 * CONV_W), lambda b: (b, 0, 0)), pl.BlockSpec((1, S, CONV_W), lambda b: (b, 0, 0)),
         pl.BlockSpec((1, S, CONV_W), lambda b: (b, 0, 0)), _const((CONV_K, CONV_W)), _const((8, CONV_W))],
        [pl.BlockSpec((1, S, 2 * CONV_W), lambda b: (b, 0, 0)), pl.BlockSpec((32, CONV_W), lambda b: (0, 0)),
         pl.BlockSpec((8, CONV_W), lambda b: (0, 0))],
        [jax.ShapeDtypeStruct((NB, S, 2 * CONV_W), BF16), jax.ShapeDtypeStruct((32, CONV_W), F32),
         jax.ShapeDtypeStruct((8, CONV_W), F32)],
        (uc, z, dcact, dw_w, vec),
        scratch_shapes=[pltpu.VMEM((S + 2 * CONV_HALO, CONV_W), F32), pltpu.VMEM((S + 2 * CONV_HALO, CONV_W), F32),
                        pltpu.VMEM((CONV_K, 8, CONV_W), F32), pltpu.VMEM((3, 8, CONV_W), F32),
                        pltpu.VMEM((8, CONV_CHUNK + 24, CONV_W), F32)],
        sem=("arbitrary",), ride=ride)


def _rel_index_of_column(cols):
    offset = jnp.where(cols < KWIN, cols, cols - DS_LANES)
    return jnp.clip(KPAD - offset, -(CHUNK - 1), MAX_REL) + (CHUNK - 1)


def _bias_table(rel_bias, ride=None):
    def body(rb_ref, o_ref, by_offset, first8):
        ridx = _rel_index_of_column(lax.broadcasted_iota(jnp.int32, (1, DS_LANES), 1))
        onehot = (ridx == lax.broadcasted_iota(jnp.int32, (N_REL, 1), 0)).astype(F32)
        by_offset[...] = jnp.dot(rb_ref[...], onehot, preferred_element_type=F32, precision=lax.Precision.HIGHEST)
        sub = lax.broadcasted_iota(jnp.int32, (8, 1), 0)
        kchunk = lax.broadcasted_iota(jnp.int32, (1, KWIN), 1) // CHUNK
        for head in range(ATT_HEADS):
            base = jnp.broadcast_to(by_offset[head:head + 1, :], (8, DS_LANES))
            rows = base
            for s in range(1, 8):
                rows = jnp.where(sub == s, pltpu.roll(base, s, 1), rows)
            first8[head] = rows

        def rows8(q8, carry):
            qchunk = (q8 * 8 + sub) // CHUNK
            band = (kchunk >= qchunk) & (kchunk <= qchunk + LEFT_CHUNKS)
            for head in range(ATT_HEADS):
                tile = pltpu.roll(first8[head], q8 * 8, 1)[:, 0:KWIN]
                o_ref[head, pl.ds(pl.multiple_of(q8 * 8, 8), 8), :] = jnp.where(band, tile, MASK_VALUE)
            return carry
        lax.fori_loop(0, QB // 8, rows8, 0)

    outs, rode = _pallas(
        body, "bias_table", (1,), [pl.BlockSpec((ATT_HEADS, N_REL), lambda i: (0, 0))],
        [pl.BlockSpec((ATT_HEADS, QB, KWIN), lambda i: (0, 0, 0))], [jax.ShapeDtypeStruct((ATT_HEADS, QB, KWIN), F32)], (rel_bias,),
        scratch_shapes=[pltpu.VMEM((ATT_HEADS, DS_LANES), F32), pltpu.VMEM((ATT_HEADS, 8, DS_LANES), F32)],
        sem=("arbitrary",), ride=ride)
    return outs[0], rode


def _load_keys(i, k_ref, v_ref, kpad, vpad, S):
    @pl.when(i == 0)
    def _():
        kpad[pl.ds(0, KPAD), :] = jnp.zeros((KPAD, ATT_W), BF16)
        vpad[pl.ds(0, KPAD), :] = jnp.zeros((KPAD, ATT_W), BF16)
        kpad[pl.ds(KPAD, S), :] = k_ref[0]
        vpad[pl.ds(KPAD, S), :] = v_ref[0]


def _att_scores(q2s, k2, tab_ref, head, in_head, in_seq):
    qm = jnp.where(in_head, q2s, jnp.zeros_like(q2s))
    s = _dot_nt(qm, k2) + tab_ref[head]
    return s if in_seq is None else jnp.where(in_seq, s, MASK_VALUE)


def _by_window(i, step):
    in_seq = (lax.broadcasted_iota(jnp.int32, (1, KWIN), 1) + i * QB) >= KPAD
    pl.when(i < KPAD // QB)(lambda: step(in_seq))
    pl.when(i >= KPAD // QB)(lambda: step(None))


def _scaled(q2):
    return q2 * jnp.asarray(ATT_HD ** -0.5, q2.dtype)


def _att_fwd(qkv, tab, ride=None):
    NB, S, _ = qkv.shape

    def body(q_ref, k_ref, v_ref, tab_ref, o_ref, lse_ref, kpad, vpad):
        i = pl.program_id(1)
        _load_keys(i, k_ref, v_ref, kpad, vpad, S)
        koff = pl.multiple_of(i * QB, QB)
        lane = lax.broadcasted_iota(jnp.int32, (1, 128), 1)

        def step(in_seq):
            lse = jnp.zeros((QB, 128), F32)
            for pair in range(ATT_HEADS // 2):
                cols = slice(pair * 128, (pair + 1) * 128)
                q2s = _scaled(q_ref[0, :, cols])
                k2 = kpad[pl.ds(koff, KWIN), cols]
                v2 = vpad[pl.ds(koff, KWIN), cols]
                o2 = jnp.zeros((QB, 128), F32)
                for hh in range(2):
                    head = 2 * pair + hh
                    in_head = (lane // ATT_HD) == hh
                    s = _att_scores(q2s, k2, tab_ref, head, in_head, in_seq)
                    m = jnp.max(s, axis=-1, keepdims=True)
                    e = jnp.exp(s - m)
                    l = jnp.sum(e, axis=-1, keepdims=True)
                    o2 = jnp.where(in_head, _dot(e.astype(BF16), v2) * (1.0 / l), o2)
                    lse = jnp.where(lane == head, m + jnp.log(l), lse)
                o_ref[0, :, cols] = o2.astype(BF16)
            lse_ref[0] = lse
        _by_window(i, step)

    seq = lambda col: pl.BlockSpec((1, S, ATT_W), lambda b, i: (b, 0, col), pipeline_mode=pl.Buffered(1))
    outs, rode = _pallas(
        body, "att_fwd", (NB, S // QB),
        [pl.BlockSpec((1, QB, ATT_W), lambda b, i: (b, i, 0)), seq(1), seq(2), _const((ATT_HEADS, QB, KWIN))],
        [pl.BlockSpec((1, QB, ATT_W), lambda b, i: (b, i, 0)), pl.BlockSpec((1, QB, 128), lambda b, i: (b, i, 0))],
        [jax.ShapeDtypeStruct((NB, S, ATT_W), BF16), jax.ShapeDtypeStruct((NB, S, 128), F32)],
        (qkv, qkv, qkv, tab),
        scratch_shapes=[pltpu.VMEM((S + KPAD, ATT_W), BF16), pltpu.VMEM((S + KPAD, ATT_W), BF16)],
        sem=("arbitrary", "arbitrary"), ride=ride)
    return outs[0], outs[1], rode


def _att_bwd(qkv, o, lse, do, tab, ride=None):
    NB, S, _ = qkv.shape
    nq = S // QB

    def body(q_ref, k_ref, v_ref, o_ref, lse_ref, do_ref, tab_ref, dqkv_ref, ds_hbm, kpad, vpad, dkpad, dvpad, ds_acc, ds_sem):
        b, i = pl.program_id(0), pl.program_id(1)
        _load_keys(i, k_ref, v_ref, kpad, vpad, S)

        @pl.when(i == 0)
        def _():
            dkpad[...] = jnp.zeros_like(dkpad)
            dvpad[...] = jnp.zeros_like(dvpad)

        @pl.when((i == 0) & (b == 0))
        def _():
            ds_acc[...] = jnp.zeros_like(ds_acc)

        koff = pl.multiple_of(i * QB, QB)
        lane = lax.broadcasted_iota(jnp.int32, (1, 128), 1)

        def step(in_seq):
            for pair in range(ATT_HEADS // 2):
                cols = slice(pair * 128, (pair + 1) * 128)
                q2s = _scaled(q_ref[0, :, cols])
                do2 = do_ref[0, :, cols]
                k2 = kpad[pl.ds(koff, KWIN), cols]
                v2 = vpad[pl.ds(koff, KWIN), cols]
                do_o = do2.astype(F32) * o_ref[0, :, cols].astype(F32)
                dq2 = jnp.zeros((QB, 128), F32)
                dk2 = jnp.zeros((KWIN, 128), F32)
                dv2 = jnp.zeros((KWIN, 128), F32)
                for hh in range(2):
                    head = 2 * pair + hh
                    in_head = (lane // ATT_HD) == hh
                    p = jnp.exp(_att_scores(q2s, k2, tab_ref, head, in_head, in_seq) - lse_ref[0, :, head:head + 1])
                    row_term = jnp.sum(jnp.where(in_head, do_o, 0.0), axis=-1, keepdims=True)
                    dom = jnp.where(in_head, do2, jnp.zeros_like(do2))
                    ds = p * (_dot_nt(dom, v2) - row_term)
                    ds_acc[head] += ds
                    dsb = ds.astype(BF16)
                    dq2 = jnp.where(in_head, _dot(dsb, k2), dq2)
                    dk2 = jnp.where(in_head, _dot_tn(dsb, q2s), dk2)
                    dv2 = jnp.where(in_head, _dot_tn(p.astype(BF16), do2), dv2)
                dqkv_ref[0, pl.ds(koff, QB), cols] = (dq2 * (ATT_HD ** -0.5)).astype(BF16)
                dkpad[pl.ds(koff, KWIN), cols] += dk2
                dvpad[pl.ds(koff, KWIN), cols] += dv2
        _by_window(i, step)

        @pl.when(i == nq - 1)
        def _():
            dqkv_ref[0, :, ATT_W:2 * ATT_W] = dkpad[pl.ds(KPAD, S), :].astype(BF16)
            dqkv_ref[0, :, 2 * ATT_W:3 * ATT_W] = dvpad[pl.ds(KPAD, S), :].astype(BF16)

        @pl.when((i == nq - 1) & (b == NB - 1))
        def _():
            out = pltpu.make_async_copy(ds_acc, ds_hbm, ds_sem)
            out.start()
            out.wait()

    seq = lambda col: pl.BlockSpec((1, S, ATT_W), lambda b, i: (b, 0, col), pipeline_mode=pl.Buffered(1))
    rows = pl.BlockSpec((1, QB, ATT_W), lambda b, i: (b, i, 0))
    return _pallas(
        body, "att_bwd", (NB, nq),
        [rows, seq(1), seq(2), rows, pl.BlockSpec((1, QB, 128), lambda b, i: (b, i, 0)), rows, _const((ATT_HEADS, QB, KWIN))],
        [pl.BlockSpec((1, S, 3 * ATT_W), lambda b, i: (b, 0, 0)), ANY],
        [jax.ShapeDtypeStruct((NB, S, 3 * ATT_W), BF16), jax.ShapeDtypeStruct((ATT_HEADS, QB, KWIN), F32)],
        (qkv, qkv, qkv, o, lse, do, tab),
        scratch_shapes=[pltpu.VMEM((S + KPAD, ATT_W), BF16), pltpu.VMEM((S + KPAD, ATT_W), BF16),
                        pltpu.VMEM((S + KPAD, ATT_W), F32), pltpu.VMEM((S + KPAD, ATT_W), F32),
                        pltpu.VMEM((ATT_HEADS, QB, KWIN), F32), pltpu.SemaphoreType.DMA],
        sem=("arbitrary", "arbitrary"), ride=ride)


def _rel_bias_grad(ds):
    def body(ds_ref, o_ref):
        sub = lax.broadcasted_iota(jnp.int32, (8, 1), 0)
        ridx = _rel_index_of_column(lax.broadcasted_iota(jnp.int32, (DS_LANES, 1), 0))
        onehot = (ridx == lax.broadcasted_iota(jnp.int32, (1, N_REL), 1)).astype(F32)
        def rows8(q8, accs):
            shift = lax.rem(DS_LANES - q8 * 8, DS_LANES)
            out = []
            for head in range(ATT_HEADS):
                tile = ds_ref[head, pl.ds(pl.multiple_of(q8 * 8, 8), 8), :]
                tile = jnp.concatenate([tile, jnp.zeros((8, DS_LANES - KWIN), F32)], axis=1)
                out.append(accs[head] + pltpu.roll(tile, shift, 1))
            return tuple(out)
        accs = lax.fori_loop(0, QB // 8, rows8, tuple(jnp.zeros((8, DS_LANES), F32) for _ in range(ATT_HEADS)))
        for head in range(ATT_HEADS):
            acc = accs[head]
            diag = jnp.zeros((8, DS_LANES), F32)
            for s in range(8):
                shifted = acc if s == 0 else pltpu.roll(acc, DS_LANES - s, 1)
                diag = jnp.where(sub == s, shifted, diag)
            z = jnp.sum(diag, axis=0, keepdims=True)
            o_ref[head:head + 1, :] = jnp.dot(z, onehot, preferred_element_type=F32, precision=lax.Precision.HIGHEST)

    return pl.pallas_call(body, out_shape=jax.ShapeDtypeStruct((ATT_HEADS, N_REL), F32), name="rel_bias_grad",
                          compiler_params=_params())(ds)


def _memkv_fwd(mem, g, w_kv, tm):
    R = mem.shape[0]
    tm = min(tm, R)

    def body(m_ref, g_ref, w_ref, h_ref, kv_ref):
        xhat, _ = _rms_stats(m_ref[...])
        h = (xhat * g_ref[...]).astype(BF16)
        h_ref[...] = h
        kv_ref[...] = _dot(h, w_ref[...]).astype(BF16)

    row = pl.BlockSpec((tm, D), lambda t: (t, 0))
    return pl.pallas_call(
        body, grid=(R // tm,), name="memkv_fwd", in_specs=[row, _const((1, D)), _const((D, 2 * MEM_W))], out_specs=[row, row],
        out_shape=[jax.ShapeDtypeStruct((R, D), BF16), jax.ShapeDtypeStruct((R, 2 * MEM_W), BF16)],
        compiler_params=_params("parallel"),
    )(mem, g, w_kv)


def _memkv_bwd(mem, dkv, w_kv, tm):
    R = mem.shape[0]
    tm = min(tm, R)

    def body(m_ref, dkv_ref, w_ref, dg_ref):
        xhat, _ = _rms_stats(m_ref[...])
        dh = _dot_nt(dkv_ref[...].astype(BF16), w_ref[...])

        @pl.when(pl.program_id(0) == 0)
        def _():
            dg_ref[...] = jnp.zeros_like(dg_ref)
        dg_ref[...] += jnp.sum(dh * xhat, axis=0, keepdims=True)

    row = pl.BlockSpec((tm, D), lambda t: (t, 0))
    return pl.pallas_call(
        body, grid=(R // tm,), name="memkv_bwd", in_specs=[row, row, _const((D, 2 * MEM_W))],
        out_specs=pl.BlockSpec((1, D), lambda t: (0, 0)), out_shape=jax.ShapeDtypeStruct((1, D), F32),
        compiler_params=_params("arbitrary"),
    )(mem, dkv, w_kv)


def _mem_exp(qh, kh):
    s = _dot_nt(qh, kh) * (MEM_HD ** -0.5)
    e = jnp.exp(s - jnp.max(s, axis=-1, keepdims=True))
    return e, jnp.sum(e, axis=-1, keepdims=True)


def _mematt_fwd(mq, kv, tq):
    NB, S, _ = mq.shape
    M = kv.shape[1]

    def body(q_ref, kv_ref, o_ref):
        for h in range(MEM_HEADS):
            cols = slice(h * MEM_HD, (h + 1) * MEM_HD)
            e, l = _mem_exp(q_ref[0, :, cols], kv_ref[0, :, cols])
            o = _dot(e.astype(BF16), kv_ref[0, :, MEM_W + h * MEM_HD:MEM_W + (h + 1) * MEM_HD]) * (1.0 / l)
            o_ref[0, :, cols] = o.astype(BF16)

    return pl.pallas_call(
        body, grid=(NB, S // tq), name="mematt_fwd",
        in_specs=[pl.BlockSpec((1, tq, MEM_W), lambda b, i: (b, i, 0)), pl.BlockSpec((1, M, 2 * MEM_W), lambda b, i: (b, 0, 0))],
        out_specs=pl.BlockSpec((1, tq, MEM_W), lambda b, i: (b, i, 0)),
        out_shape=jax.ShapeDtypeStruct((NB, S, MEM_W), BF16), compiler_params=_params("parallel", "parallel"),
    )(mq, kv)


def _mematt_bwd(mq, kv, do, tq):
    NB, S, _ = mq.shape
    M = kv.shape[1]

    def body(q_ref, kv_ref, do_ref, dq_ref, dkv_ref):
        @pl.when(pl.program_id(1) == 0)
        def _():
            dkv_ref[...] = jnp.zeros_like(dkv_ref)
        for h in range(MEM_HEADS):
            cols = slice(h * MEM_HD, (h + 1) * MEM_HD)
            vcols = slice(MEM_W + h * MEM_HD, MEM_W + (h + 1) * MEM_HD)
            qh, kh, vh, doh = q_ref[0, :, cols], kv_ref[0, :, cols], kv_ref[0, :, vcols], do_ref[0, :, cols]
            e, l = _mem_exp(qh, kh)
            p = e * (1.0 / l)
            dp = _dot_nt(doh, vh)
            ds = p * (dp - jnp.sum(p * dp, axis=-1, keepdims=True))
            dss = (ds * (MEM_HD ** -0.5)).astype(BF16)
            dq_ref[0, :, cols] = _dot(dss, kh).astype(BF16)
            dkv_ref[0, :, cols] += _dot_tn(dss, qh)
            dkv_ref[0, :, vcols] += _dot_tn(p.astype(BF16), doh)

    qspec = pl.BlockSpec((1, tq, MEM_W), lambda b, i: (b, i, 0))
    kvspec = pl.BlockSpec((1, M, 2 * MEM_W), lambda b, i: (b, 0, 0))
    return pl.pallas_call(
        body, grid=(NB, S // tq), name="mematt_bwd", in_specs=[qspec, kvspec, qspec], out_specs=[qspec, kvspec],
        out_shape=[jax.ShapeDtypeStruct((NB, S, MEM_W), BF16), jax.ShapeDtypeStruct((NB, M, 2 * MEM_W), F32)],
        compiler_params=_params("arbitrary", "arbitrary"),
    )(mq, kv, do)


def _branch(j, in_ref, w_ref, gl_ref, bg_ref):
    y = _dot(in_ref[...], w_ref[...])
    gate = _sigmoid(gl_ref[:, j * D:(j + 1) * D].astype(F32) + bg_ref[:, j * D:(j + 1) * D])
    return y, gate


def _combine_fwd(x, cact, oatt, omem, gl, bg, wpw, wo, wmo, wout, tm):
    T = x.shape[0]

    def body(x_ref, c_ref, a_ref, m_ref, gl_ref, bg_ref, wpw_ref, wo_ref, wmo_ref, wout_ref, xo_ref, y_ref):
        y = None
        for j, (in_ref, w_ref) in enumerate(((c_ref, wpw_ref), (a_ref, wo_ref), (m_ref, wmo_ref))):
            yj, gate = _branch(j, in_ref, w_ref, gl_ref, bg_ref)
            y = gate * yj if y is None else y + gate * yj
        y = y.astype(BF16)
        y_ref[...] = y
        xo_ref[...] = x_ref[...] + _dot(y, wout_ref[...])

    row = lambda w: pl.BlockSpec((tm, w), lambda t: (t, 0))
    wbr = _const((512, D))
    return pl.pallas_call(
        body, grid=(T // tm,), name="combine_fwd",
        in_specs=[row(D), row(512), row(512), row(512), row(3 * D), _const((1, 3 * D)), wbr, wbr, wbr, _const((D, D))],
        out_specs=[row(D), row(D)],
        out_shape=[jax.ShapeDtypeStruct((T, D), F32), jax.ShapeDtypeStruct((T, D), BF16)],
        compiler_params=_params("parallel"),
    )(x, cact, oatt, omem, gl, bg, wpw, wo, wmo, wout)


def _combine_bwd(dx, cact, oatt, omem, gl, bg, wpw, wo, wmo, wout, tm, ride=None):
    T = dx.shape[0]

    def body(dx_ref, c_ref, a_ref, m_ref, gl_ref, bg_ref, wpw_ref, wo_ref, wmo_ref, wout_ref,
             dgl_ref, dc_ref, da_ref, dm_ref, dyc_ref, dya_ref, dym_ref, dbg_ref):
        dy = _dot_nt(dx_ref[...].astype(BF16), wout_ref[...])

        @pl.when(pl.program_id(0) == 0)
        def _():
            dbg_ref[...] = jnp.zeros_like(dbg_ref)
        branches = ((c_ref, wpw_ref, dyc_ref, dc_ref), (a_ref, wo_ref, dya_ref, da_ref), (m_ref, wmo_ref, dym_ref, dm_ref))
        for j, (in_ref, w_ref, dyb_ref, din_ref) in enumerate(branches):
            yj, gate = _branch(j, in_ref, w_ref, gl_ref, bg_ref)
            dyg = dy * gate
            dlogit = dyg * yj * (1.0 - gate)
            dgl_ref[:, j * D:(j + 1) * D] = dlogit.astype(BF16)
            dbg_ref[:, j * D:(j + 1) * D] += jnp.sum(dlogit, axis=0, keepdims=True)
            dyb = dyg.astype(BF16)
            dyb_ref[...] = dyb
            din_ref[...] = _dot_nt(dyb, w_ref[...]).astype(din_ref.dtype)

    row = lambda w: pl.BlockSpec((tm, w), lambda t: (t, 0))
    wbr = _const((512, D))
    sds = jax.ShapeDtypeStruct
    return _pallas(
        body, "combine_bwd", (T // tm,),
        [row(D), row(512), row(512), row(512), row(3 * D), _const((1, 3 * D)), wbr, wbr, wbr, _const((D, D))],
        [row(3 * D), row(512), row(512), row(512), row(D), row(D), row(D), pl.BlockSpec((1, 3 * D), lambda t: (0, 0))],
        [sds((T, 3 * D), BF16), sds((T, 512), F32), sds((T, 512), BF16), sds((T, 512), BF16),
         sds((T, D), BF16), sds((T, D), BF16), sds((T, D), BF16), sds((1, 3 * D), F32)],
        (dx, cact, oatt, omem, gl, bg, wpw, wo, wmo, wout), sem=("arbitrary",), ride=ride)


def _peer(x, y, c, rel):
    rx, ry, rc = (rel >> 2) & 1, (rel >> 1) & 1, rel & 1
    return ((1 - x) if rx else x, (1 - y) if ry else y, (1 - c) if rc else c)


def _all_sum_small(parts):
    n = len(parts)

    def body(*refs):
        p_refs, o_refs, slots = refs[:n], refs[n:2 * n], refs[2 * n:3 * n]
        send_sems, recv_sems = refs[3 * n:]
        x, y, c = _my_coords()
        me = _dev_index(x, y, c)

        def copy(i, rel, arrival):
            peer = _peer(x, y, c, rel)
            return pltpu.make_async_remote_copy(
                src_ref=p_refs[i], dst_ref=slots[i].at[_dev_index(*peer) if arrival else me],
                send_sem=send_sems.at[i, rel - 1], recv_sem=recv_sems.at[i, rel - 1], device_id=peer, device_id_type=MESH)

        for i in range(n):
            slots[i][me] = p_refs[i][...]
        for rel in range(1, NDEV):
            for i in range(n):
                copy(i, rel, False).start()
        for rel in range(1, NDEV):
            for i in range(n):
                copy(i, rel, True).wait_recv()
        for rel in range(1, NDEV):
            for i in range(n):
                copy(i, rel, False).wait_send()
        for i in range(n):
            total = slots[i][0]
            for d in range(1, NDEV):
                total = total + slots[i][d]
            o_refs[i][...] = total

    vmem = pl.BlockSpec(memory_space=pltpu.VMEM)
    return pl.pallas_call(
        body, out_shape=[jax.ShapeDtypeStruct(p.shape, F32) for p in parts], name="all_sum_small",
        in_specs=[vmem] * n, out_specs=[vmem] * n,
        scratch_shapes=[pltpu.VMEM((NDEV,) + p.shape, F32) for p in parts]
        + [pltpu.SemaphoreType.DMA((n, NDEV - 1)), pltpu.SemaphoreType.DMA((n, NDEV - 1))],
        compiler_params=pltpu.CompilerParams(has_side_effects=True),
    )(*parts)


HBM = pl.BlockSpec(memory_space=pltpu.HBM)
SEM = pl.BlockSpec(memory_space=pltpu.SEMAPHORE)


def _own_block(g, kind, m, tag):
    def body(g_ref, land_ref, staged, sem):
        me = _dev_index(*_my_coords())
        for cp in (pltpu.make_async_copy(_window(g_ref, kind, m, me), staged, sem),
                   pltpu.make_async_copy(staged, land_ref.at[me], sem)):
            cp.start()
            cp.wait()

    block = (m, g.shape[1]) if kind == 'row' else (g.shape[0], m)
    return pl.pallas_call(body, in_specs=[ANY], out_specs=ANY, out_shape=jax.ShapeDtypeStruct((NDEV,) + block, g.dtype),
                          scratch_shapes=[pltpu.VMEM(block, g.dtype), pltpu.SemaphoreType.DMA], name="own_block_" + tag)(g)


def _scatter_start(g, land, kind, m, tag):
    def body(g_ref, land_ref, send_sems, recv_sems, g_thru, land_thru, token):
        x, y, c = _my_coords()
        me = _dev_index(x, y, c)
        for rel in range(1, NDEV):
            peer = _peer(x, y, c, rel)
            pltpu.make_async_remote_copy(src_ref=_window(g_ref, kind, m, _dev_index(*peer)), dst_ref=land_ref.at[me],
                                         send_sem=send_sems.at[rel - 1], recv_sem=recv_sems.at[rel - 1],
                                         device_id=peer, device_id_type=MESH).start()
        token[...] = jnp.zeros_like(token)

    return pl.pallas_call(
        body, name="scatter_start_" + tag,
        out_shape=(pltpu.SemaphoreType.DMA((NDEV - 1,)), pltpu.SemaphoreType.DMA((NDEV - 1,)), pltpu.HBM(g.shape, g.dtype),
                   pltpu.HBM(land.shape, land.dtype), jax.ShapeDtypeStruct((8, 128), F32)),
        in_specs=(HBM, HBM), out_specs=(SEM, SEM, HBM, HBM, pl.BlockSpec(memory_space=pltpu.VMEM)),
        input_output_aliases={0: 2, 1: 3},
        compiler_params=pltpu.CompilerParams(has_side_effects=pltpu.SideEffectType.DATAFLOW_SIDE_EFFECTING),
    )(pltpu.with_memory_space_constraint(g, pltpu.HBM), pltpu.with_memory_space_constraint(land, pltpu.HBM))


def _scatter_wait(send_sems, recv_sems, g_thru, land_thru, after, kind, m, tag):
    n_after = len(after)

    def body(*refs):
        g_ref, land_ref, send_sems, recv_sems = refs[:4]
        x, y, c = _my_coords()
        me = _dev_index(x, y, c)
        for rel in range(1, NDEV):
            peer = _peer(x, y, c, rel)
            dev = _dev_index(*peer)
            cp = pltpu.make_async_remote_copy(src_ref=_window(g_ref, kind, m, me), dst_ref=land_ref.at[dev],
                                              send_sem=send_sems.at[rel - 1], recv_sem=recv_sems.at[rel - 1],
                                              device_id=peer, device_id_type=MESH)
            cp.wait_send()
            cp.wait_recv()

    return pl.pallas_call(
        body, name="scatter_wait_" + tag,
        out_shape=(pltpu.HBM(g_thru.shape, g_thru.dtype), pltpu.HBM(land_thru.shape, land_thru.dtype)),
        in_specs=(HBM, HBM, SEM, SEM) + (ANY,) * n_after, out_specs=(HBM, HBM), input_output_aliases={0: 0, 1: 1},
        compiler_params=pltpu.CompilerParams(has_side_effects=pltpu.SideEffectType.DATAFLOW_SIDE_EFFECTING),
    )(g_thru, land_thru, send_sems, recv_sems, *after)[1]


def _adamw_math(w, g, m, v):
    m = ADAM_B1 * m + (1.0 - ADAM_B1) * g
    v = ADAM_B2 * v + (1.0 - ADAM_B2) * (g * g)
    m_hat = m / (1.0 - ADAM_B1 ** ADAM_STEP)
    v_hat = v / (1.0 - ADAM_B2 ** ADAM_STEP)
    delta = -ADAM_LR * (m_hat / (jnp.sqrt(v_hat) + ADAM_EPS) + ADAM_WD * w)
    return delta, m, v


def _sum_adamw(parts, w, m, v, name, after=None):
    R, C = w.shape
    n_parts = len(parts)
    cg = C // n_parts
    tr = max(t for t in range(8, 257, 8) if R % t == 0)
    deps = [] if after is None else [after]

    def body(*refs):
        p_refs = refs[:n_parts]
        w_ref, m_ref, v_ref = refs[n_parts:n_parts + 3]
        g_ref, d_ref, mo_ref, vo_ref = refs[n_parts + 3 + len(deps):]
        for k, p_ref in enumerate(p_refs):
            @pl.when(pl.program_id(0) == k)
            def _():
                g = p_ref[0].astype(F32)
                for d in range(1, NDEV):
                    g = g + p_ref[d].astype(F32)
                g_ref[...] = g
                d_ref[...], mo_ref[...], vo_ref[...] = _adamw_math(w_ref[...], g, m_ref[...], v_ref[...])

    part = pl.BlockSpec((NDEV, tr, cg), lambda k, t: (0, t, 0))
    blk = pl.BlockSpec((tr, cg), lambda k, t: (t, k))
    return pl.pallas_call(
        body, grid=(n_parts, R // tr), name=name, in_specs=[part] * n_parts + [blk, blk, blk] + [ANY] * len(deps),
        out_specs=[blk] * 4, out_shape=[jax.ShapeDtypeStruct((R, C), F32)] * 4, compiler_params=_params("parallel", "parallel"),
    )(*parts, w, m, v, *deps)


def _adamw_small(ws, gs, ms, vs):
    n = len(ws)

    def body(*refs):
        w_refs, g_refs, m_refs, v_refs = (refs[k * n:(k + 1) * n] for k in range(4))
        d_refs, mo_refs, vo_refs = (refs[(4 + k) * n:(5 + k) * n] for k in range(3))
        for i in range(n):
            d_refs[i][...], mo_refs[i][...], vo_refs[i][...] = _adamw_math(w_refs[i][...], g_refs[i][...], m_refs[i][...], v_refs[i][...])

    shapes = [jax.ShapeDtypeStruct(a.shape, F32) for a in ws]
    outs = pl.pallas_call(body, out_shape=shapes * 3, name="adamw_small", compiler_params=_params())(*ws, *gs, *ms, *vs)
    return outs[:n], outs[n:2 * n], outs[2 * n:]


def kernel(x, mem, ffn1_norm, ffn1_w_up, ffn1_w_down, mix_norm, mem_norm, w_in, b_gate, conv_dw_w, conv_dw_b, conv_ln_g, conv_ln_b, conv_w_pw, att_rel_bias, att_w_o, mem_w_kv, mem_w_o, w_out, ffn2_norm, ffn2_w_up, ffn2_w_down, final_norm, loss_target, m_ffn1_norm, m_ffn1_w_up, m_ffn1_w_down, m_mix_norm, m_mem_norm, m_w_in, m_b_gate, m_conv_dw_w, m_conv_dw_b, m_conv_ln_g, m_conv_ln_b, m_conv_w_pw, m_att_rel_bias, m_att_w_o, m_mem_w_kv, m_mem_w_o, m_w_out, m_ffn2_norm, m_ffn2_w_up, m_ffn2_w_down, m_final_norm, v_ffn1_norm, v_ffn1_w_up, v_ffn1_w_down, v_mix_norm, v_mem_norm, v_w_in, v_b_gate, v_conv_dw_w, v_conv_dw_b, v_conv_ln_g, v_conv_ln_b, v_conv_w_pw, v_att_rel_bias, v_att_w_o, v_mem_w_kv, v_mem_w_o, v_w_out, v_ffn2_norm, v_ffn2_w_up, v_ffn2_w_down, v_final_norm):
    given = dict(locals())
    w = {n: given[n] for n in WEIGHTS}
    mom = {n: given["m_" + n] for n in WEIGHTS}
    var = {n: given["v_" + n] for n in WEIGHTS}

    NB, S, _ = x.shape
    T = NB * S
    ML = mem.shape[1]
    x0 = x.reshape(T, D)
    target = loss_target.reshape(T, D)
    mem2 = mem.reshape(NB * ML, D)

    def block(t, n):
        return jnp.transpose(t[0]) if n in TRANSPOSED else t[0]

    sh = dict(zip(BIG_ORDER, _cast_shards([block(w[n], n) for n in BIG_ORDER])))
    dw_t = jnp.transpose(conv_dw_w[0])

    def gather(names, extra=(), extra_kinds=()):
        return _gather_ride([sh[n] for n in names] + list(extra), [BIG[n] for n in names] + list(extra_kinds))

    W = {}
    names0 = ['ffn1_w_up', 'ffn1_w_down']
    tab, got = _bias_table(att_rel_bias[0], ride=gather(names0, [dw_t], [('row', dw_t.shape[0])]))
    W.update(zip(names0, got[:2]))
    dw_full = jnp.transpose(got[2])
    conv_vec = jnp.concatenate([conv_dw_b, conv_ln_g, conv_ln_b, jnp.zeros((5, CONV_W), F32)], axis=0)
    fin_g = final_norm.reshape(1, D)

    names1 = ['w_in', 'conv_w_pw', 'att_w_o', 'mem_w_kv', 'mem_w_o', 'w_out']
    (x1, ab1), got = _ffn_fwd(x0, ffn1_norm, W['ffn1_w_up'], W['ffn1_w_down'], TILE_FFN_FWD, "ffn1_fwd", ride=gather(names1))
    W.update(zip(names1, got))
    (uc, qkv, mq, gl, hmix), _ = _mix_fwd(x1, mix_norm, W['w_in'], TILE_TOKENS)
    uc3 = uc.reshape(NB, S, 2 * CONV_W)
    qkv3 = qkv.reshape(NB, S, 3 * ATT_W)
    mq3 = mq.reshape(NB, S, MEM_W)
    cact, conv_z = _conv_fwd(uc3, dw_full, conv_vec)
    cact = cact.reshape(T, CONV_W)
    names2 = ['ffn2_w_up', 'ffn2_w_down']
    oatt3, att_lse, got = _att_fwd(qkv3, tab, ride=gather(names2))
    W.update(zip(names2, got))
    oatt = oatt3.reshape(T, ATT_W)
    memh, kv = _memkv_fwd(mem2, mem_norm, W['mem_w_kv'], TILE_TOKENS)
    kv3 = kv.reshape(NB, ML, 2 * MEM_W)
    omem = _mematt_fwd(mq3, kv3, TILE_TOKENS).reshape(T, MEM_W)
    branch_w = (W['conv_w_pw'], W['att_w_o'], W['mem_w_o'], W['w_out'])
    x2, ymix = _combine_fwd(x1, cact, oatt, omem, gl, b_gate, *branch_w, TILE_TOKENS)
    dx3, ab2, loss_part, dg_final = _ffn_fwd_loss(x2, ffn2_norm, W['ffn2_w_up'], W['ffn2_w_down'], fin_g, target, TILE_FFN_FWD,
                                                  "ffn2_fwd_loss")

    def scatter(grads, names):
        return _scatter_ride(grads, [BIG[n] for n in names])

    G, P = {}, {}
    dx2, dab2, act2, h2, dg_ffn2 = _ffn_bwd(x2, dx3, ab2, ffn2_norm, W['ffn2_w_up'], W['ffn2_w_down'], TILE_FFN, "ffn2_bwd")
    g_up, _ = _tn_matmul(dab2, h2, 512, "grad_ffn2_w_up_a", tt=TILE_GRAD_TOKENS_WIDE, x_part=(0, 2), out_rows=2 * FF)
    G['ffn2_w_up'], _ = _tn_matmul(dab2, h2, 512, "grad_ffn2_w_up_b", tt=TILE_GRAD_TOKENS_WIDE, x_part=(1, 2), out_rows=2 * FF,
                                   prev=g_up)
    G['ffn2_w_down'], _ = _tn_matmul(act2, dx3, 512, "grad_ffn2_w_down", scale=0.5, tt=TILE_GRAD_TOKENS_WIDE)
    (dgl, dcact, doatt, domem, dyc, dya, dym, dbg), got = _combine_bwd(
        dx2, cact, oatt, omem, gl, b_gate, *branch_w, TILE_COMBINE, ride=scatter([G['ffn2_w_up']], ['ffn2_w_up']))
    P['ffn2_w_up'] = got
    G['w_out'], _ = _tn_matmul(ymix, dx2, D, "grad_w_out", tt=TILE_GRAD_TOKENS_WIDE)
    G['conv_w_pw'], _ = _tn_matmul(cact, dyc, D, "grad_conv_w_pw")
    G['att_w_o'], _ = _tn_matmul(oatt, dya, D, "grad_att_w_o")
    G['mem_w_o'], _ = _tn_matmul(omem, dym, D, "grad_mem_w_o")
    dmq3, dkv3 = _mematt_bwd(mq3, kv3, domem.reshape(NB, S, MEM_W), TILE_TOKENS)
    dkv = dkv3.reshape(NB * ML, 2 * MEM_W)
    dg_mem = _memkv_bwd(mem2, dkv, W['mem_w_kv'], TILE_TOKENS)
    G['mem_w_kv'], _ = _tn_matmul(memh, dkv, 512, "grad_mem_w_kv")
    names = ['ffn2_w_down', 'w_out', 'conv_w_pw', 'att_w_o', 'mem_w_o']
    (dqkv3, dscore), got = _att_bwd(qkv3, oatt3, att_lse, doatt.reshape(NB, S, ATT_W), tab,
                                    ride=scatter([G[n] for n in names], names))
    P.update((n, [p]) for n, p in zip(names, got))
    d_rel = _rel_bias_grad(dscore)
    (duc3, d_dw, d_cvec), got = _conv_bwd(uc3, conv_z, dcact.reshape(NB, S, CONV_W), dw_full, conv_vec,
                                          ride=scatter([G['mem_w_kv']], ['mem_w_kv']))
    P['mem_w_kv'] = got
    duc, dqkv, dmq = duc3.reshape(T, 2 * CONV_W), dqkv3.reshape(T, 3 * ATT_W), dmq3.reshape(T, MEM_W)
    g_in, _ = _tn_matmul(hmix, duc, 1024, "grad_w_in_conv", out_cols=IN_COLS, col_off=0)
    g_in, _ = _tn_matmul(hmix, dqkv, 512, "grad_w_in_qkv", out_cols=IN_COLS, col_off=1024, prev=g_in)
    g_in, _ = _tn_matmul(hmix, dmq, 512, "grad_w_in_mq", out_cols=IN_COLS, col_off=2560, prev=g_in)
    G['w_in'], _ = _tn_matmul(hmix, dgl, 1024, "grad_w_in_gate", out_cols=IN_COLS, col_off=3072, prev=g_in)
    def start_scatter(g, name, tag):
        kind = BIG[name]
        return _scatter_start(g, _own_block(g, *kind, tag), *kind, tag) + (kind, tag)

    def wait_scatter(started, after):
        send_sems, recv_sems, g_thru, land_thru, _, kind, tag = started
        return _scatter_wait(send_sems, recv_sems, g_thru, land_thru, after, *kind, tag)

    ex_in = start_scatter(G['w_in'], 'w_in', "w_in")
    (dx1, dg_mix), _ = _mix_bwd(x1, dx2, duc, dqkv, dmq, dgl, mix_norm, W['w_in'], TILE_TOKENS, after=ex_in[4])
    dx0, dab1, act1, h1, dg_ffn1 = _ffn_bwd(x0, dx1, ab1, ffn1_norm, W['ffn1_w_up'], W['ffn1_w_down'], TILE_FFN, "ffn1_bwd")
    g_wd1, _ = _tn_matmul(act1, dx1, 512, "grad_ffn1_w_down", scale=0.5, tt=TILE_GRAD_TOKENS_WIDE)
    ex_wd = start_scatter(g_wd1, 'ffn1_w_down', "ffn1_w_down")
    g_wu1a, _ = _tn_matmul(dab1, h1, 512, "grad_ffn1_w_up_a", tt=TILE_GRAD_TOKENS_WIDEST, y_part=(0, 2), after=ex_wd[4])
    ex_a = start_scatter(g_wu1a, 'ffn1_w_up', "ffn1_w_up_a")
    g_wu1b, _ = _tn_matmul(dab1, h1, 512, "grad_ffn1_w_up_b", tt=TILE_GRAD_TOKENS_WIDEST, y_part=(1, 2), after=ex_a[4])
    ex_b = start_scatter(g_wu1b, 'ffn1_w_up', "ffn1_w_up_b")
    token = ex_b[4]

    small_names = ['loss', 'ffn1_norm', 'mix_norm', 'mem_norm', 'b_gate', 'conv_dw_w', 'conv_vec', 'att_rel_bias', 'ffn2_norm',
                   'final_norm']
    small = dict(zip(small_names, _all_sum_small(
        [loss_part + token[0:1], dg_ffn1, dg_mix, dg_mem, dbg, d_dw, d_cvec, d_rel, dg_ffn2, dg_final])))
    loss = small['loss'][0, 0]
    me = _dev_index(*_my_coords())
    for i, n in enumerate(['conv_dw_b', 'conv_ln_g', 'conv_ln_b']):
        small[n] = small['conv_vec'][i:i + 1]
    small['conv_dw_w'] = lax.dynamic_slice(small['conv_dw_w'], (0, me * conv_dw_w.shape[2]), (CONV_K, conv_dw_w.shape[2]))
    little = [n for n in WEIGHTS if n not in BIG]
    as2d = lambda t, n: t.reshape(small[n].shape)
    d_s, m_s, v_s = _adamw_small([as2d(w[n], n) for n in little], [small[n] for n in little],
                                 [as2d(mom[n], n) for n in little], [as2d(var[n], n) for n in little])
    grad, delta, new_m, new_v = {}, {}, {}, {}
    for i, n in enumerate(little):
        grad[n], delta[n], new_m[n], new_v[n] = (t.reshape(w[n].shape) for t in (small[n], d_s[i], m_s[i], v_s[i]))
    done = [d_s[0]]
    waited = {'w_in': [ex_in], 'ffn1_w_down': [ex_wd], 'ffn1_w_up': [ex_a, ex_b]}
    order = [n for n in BIG_ORDER if n not in waited] + list(waited)
    for n in order:
        if n in waited:
            P[n] = [wait_scatter(ex, done) for ex in waited[n]]
        outs = _sum_adamw(P[n], block(w[n], n), block(mom[n], n), block(var[n], n), "adamw_" + n,
                          after=None if n in waited else token)
        done.append(outs[0])
        grad[n], delta[n], new_m[n], new_v[n] = ((jnp.transpose(t) if n in TRANSPOSED else t)[None] for t in outs)

    return (loss, dx0.reshape(NB, S, D), *[grad[n] for n in WEIGHTS], *[delta[n] for n in WEIGHTS],
            *[new_m[n] for n in WEIGHTS], *[new_v[n] for n in WEIGHTS])
```

```python
import functools

import jax
import jax.numpy as jnp
from jax import lax
from jax.experimental import pallas as pl
from jax.experimental.pallas import tpu as pltpu

F32 = jnp.float32
BF16 = jnp.bfloat16

EPS = 1e-6
MASK_VALUE = -1e30
D = 1024
NDEV = 8
FF = 2816
FF_SHARD = 704
FF_HALF_ROWS = 352
FF_BLOCK_EDGES = ()
IN_COLS = 6144
CONV_W = 512
CONV_K = 31
CONV_HALO = 32
CONV_CHUNK = 32
CONV_WIN = CONV_CHUNK + 40
GLU_CHUNK = 128
ATT_W = 512
ATT_HEADS = 8
ATT_HD = 64
CHUNK = 64
LEFT_CHUNKS = 8
MAX_REL = 128
N_REL = 192
QB = 256
KWIN = QB + LEFT_CHUNKS * CHUNK
KPAD = LEFT_CHUNKS * CHUNK
DS_LANES = 1024
MEM_W = 512
MEM_HEADS = 4
MEM_HD = 128
ADAM_LR = 0.001
ADAM_B1 = 0.9
ADAM_B2 = 0.999
ADAM_EPS = 1e-08
ADAM_WD = 0.01
ADAM_STEP = 10
VMEM_LIMIT = 60 * 1024 * 1024
TILE_FFN = 256
TILE_FFN_FWD = 512
TILE_COMBINE = 256
TILE_TOKENS = 512
TILE_GRAD_TOKENS = 2048
TILE_GRAD_TOKENS_WIDE = 1024
TILE_GRAD_TOKENS_WIDEST = 512

MESH = pl.DeviceIdType.MESH
ANY = pl.BlockSpec(memory_space=pl.ANY)

WEIGHTS = ['ffn1_norm', 'ffn1_w_up', 'ffn1_w_down', 'mix_norm', 'mem_norm', 'w_in', 'b_gate', 'conv_dw_w', 'conv_dw_b',
           'conv_ln_g', 'conv_ln_b', 'conv_w_pw', 'att_rel_bias', 'att_w_o', 'mem_w_kv', 'mem_w_o', 'w_out', 'ffn2_norm',
           'ffn2_w_up', 'ffn2_w_down', 'final_norm']
BIG = {
    'ffn1_w_up': ('row', FF_SHARD), 'ffn1_w_down': ('row', FF_HALF_ROWS), 'w_in': ('col', 768),
    'conv_w_pw': ('col', 128), 'att_w_o': ('col', 128), 'mem_w_kv': ('row', 128), 'mem_w_o': ('col', 128),
    'w_out': ('row', 128), 'ffn2_w_up': ('row', FF_SHARD), 'ffn2_w_down': ('row', FF_HALF_ROWS),
}
BIG_ORDER = ['ffn1_w_up', 'ffn1_w_down', 'w_in', 'conv_w_pw', 'att_w_o', 'mem_w_kv', 'mem_w_o', 'w_out', 'ffn2_w_up', 'ffn2_w_down']
TRANSPOSED = ('ffn1_w_up', 'ffn2_w_up')


def _dot(a, b):
    return jnp.dot(a, b, preferred_element_type=F32)


def _dot_nt(a, b):
    return lax.dot_general(a, b, (((1,), (1,)), ((), ())), preferred_element_type=F32)


def _dot_tn(a, b):
    return lax.dot_general(a, b, (((0,), (0,)), ((), ())), preferred_element_type=F32)


def _sigmoid(v):
    return jax.nn.sigmoid(v)


def _const(shape):
    return pl.BlockSpec(shape, lambda *_: (0,) * len(shape), pipeline_mode=pl.Buffered(1))


def _params(*sem):
    return pltpu.CompilerParams(dimension_semantics=sem if sem else None, vmem_limit_bytes=VMEM_LIMIT)


def _my_coords():
    return lax.axis_index("x"), lax.axis_index("y"), lax.axis_index("c")


def _dev_index(px, py, pc):
    return 4 * px + 2 * py + pc


def _window(ref, kind, n, p):
    if kind == 'row':
        return ref.at[pl.ds(pl.multiple_of(p * n, n), n), :]
    return ref.at[:, pl.ds(pl.multiple_of(p * n, 128), n)]


def _full_shape(kind, n, shard_shape):
    if kind == 'row':
        return (NDEV * n, shard_shape[1])
    return (shard_shape[0], NDEV * n)


def _cast_shards(shards):
    n = len(shards)

    def body(*refs):
        for i in range(n):
            refs[n + i][...] = refs[i][...].astype(BF16)

    out_shape = [jax.ShapeDtypeStruct(s.shape, BF16) for s in shards]
    return pl.pallas_call(body, out_shape=out_shape, name="cast_shards", compiler_params=_params())(*shards)


class _Ride:
    def __init__(self, inputs, out_shape, scratch, start, finish, mids=()):
        self.inputs, self.out_shape, self.scratch = list(inputs), list(out_shape), list(scratch)
        self.start, self.finish, self.mids = start, finish, tuple(mids)


def _pallas(body, name, grid, in_specs, out_specs, out_shape, args, scratch_shapes=(), sem=None, aliases=None, ride=None,
            after=None):
    if ride is None:
        n_in, n_dep = len(args), 0 if after is None else 1

        def kernel_body(*refs):
            body(*refs[:n_in], *refs[n_in + n_dep:])

        outs = pl.pallas_call(kernel_body if n_dep else body, grid=grid, name=name, in_specs=list(in_specs) + [ANY] * n_dep,
                              out_specs=out_specs, out_shape=out_shape, scratch_shapes=list(scratch_shapes),
                              input_output_aliases=aliases or {}, compiler_params=_params(*sem),
                              )(*args, *([after] if n_dep else []))
        return list(outs), []
    n_in, n_out, n_scr = len(args), len(out_shape), len(scratch_shapes)
    r_in, r_out = len(ride.inputs), len(ride.out_shape)

    def wrapped(*refs):
        k_in, rin = refs[:n_in], refs[n_in:n_in + r_in]
        o0 = n_in + r_in
        k_out, rout = refs[o0:o0 + n_out], refs[o0 + n_out:o0 + n_out + r_out]
        s0 = o0 + n_out + r_out
        k_scr, rscr = refs[s0:s0 + n_scr], refs[s0 + n_scr:]
        ids = [pl.program_id(k) for k in range(len(grid))]
        first = functools.reduce(jnp.logical_and, [i == 0 for i in ids])
        last = functools.reduce(jnp.logical_and, [i == g - 1 for i, g in zip(ids, grid)])
        pl.when(first)(lambda: ride.start(rin, rout, rscr))
        single_step = all(g == 1 for g in grid)
        for quarter, mid in ride.mids:
            if not single_step:
                at_mid = functools.reduce(jnp.logical_and, [ids[0] == (quarter * grid[0]) // 4] + [i == 0 for i in ids[1:]])
                pl.when(at_mid)(functools.partial(mid, rin, rout, rscr))
        body(*k_in, *k_out, *k_scr)
        for _, mid in ride.mids:
            if single_step:
                mid(rin, rout, rscr)
        pl.when(last)(lambda: ride.finish(rin, rout, rscr))

    outs = pl.pallas_call(
        wrapped, grid=grid, name=name, in_specs=list(in_specs) + [ANY] * r_in, out_specs=list(out_specs) + [ANY] * r_out,
        out_shape=list(out_shape) + ride.out_shape, scratch_shapes=list(scratch_shapes) + ride.scratch,
        input_output_aliases=aliases or {}, compiler_params=_params(*(["arbitrary"] * len(grid))),
    )(*args, *ride.inputs)
    return list(outs[:n_out]), list(outs[n_out:])


def _gather_ride(shards, kinds):
    n = len(shards)

    def plan(rin, out, sems):
        send_sems, recv_sems, local_sems = sems[:3]
        x, y, c = _my_coords()
        me, sibling = (x, y, c), (x, y, 1 - c)
        xn, yn, diag = (1 - x, y), (x, 1 - y), (1 - x, 1 - y)

        def win(i, dev):
            return _window(out[i], kinds[i][0], kinds[i][1], _dev_index(*dev))

        def copy(i, k, block, to, from_shard=False):
            return pltpu.make_async_remote_copy(
                src_ref=rin[i] if from_shard else win(i, block), dst_ref=win(i, block),
                send_sem=send_sems.at[i, k], recv_sem=recv_sems.at[i, k], device_id=to, device_id_type=MESH)

        def each(fn):
            return [fn(i) for i in range(n)]

        return dict(
            local=lambda: each(lambda i: pltpu.make_async_copy(rin[i], win(i, me), local_sems.at[i])),
            own=lambda: [cp for i in range(n) for cp in (copy(i, 0, me, sibling, True), copy(i, 1, me, (*xn, c), True),
                                                         copy(i, 2, me, (*yn, c), True))],
            from_x=lambda: each(lambda i: copy(i, 1, (*xn, c), me)),
            from_y=lambda: each(lambda i: copy(i, 2, (*yn, c), me)),
            x_block_on_to_y=lambda: each(lambda i: copy(i, 3, (*xn, c), (*yn, c))),
            y_block_on_to_x=lambda: each(lambda i: copy(i, 3, (*yn, c), (*xn, c))),
            from_diag=lambda: each(lambda i: copy(i, 3, (*diag, c), me)),
            to_sibling=lambda j: each(lambda i: copy(i, 4 + j, (*(xn, yn, diag)[j], c), sibling)),
            from_sibling=lambda: [cp for i in range(n) for cp in
                                  [copy(i, 0, sibling, me)] + [copy(i, 4 + j, (*chip, 1 - c), me) for j, chip in enumerate((xn, yn, diag))]],
            north=c == 1)

    def start(rin, out, sems):
        p = plan(rin, out, sems)
        for cp in p['local']() + p['own']():
            cp.start()

    def pass_diagonal(rin, out, sems):
        p = plan(rin, out, sems)

        @pl.when(p['north'])
        def _():
            for got, fwd, sib in zip(p['from_x'](), p['x_block_on_to_y'](), p['to_sibling'](0)):
                got.wait_recv()
                fwd.start()
                sib.start()

        @pl.when(jnp.logical_not(p['north']))
        def _():
            for got, fwd, sib in zip(p['from_y'](), p['y_block_on_to_x'](), p['to_sibling'](1)):
                got.wait_recv()
                fwd.start()
                sib.start()

    def pass_to_sibling(rin, out, sems):
        p = plan(rin, out, sems)

        @pl.when(p['north'])
        def _():
            for got, sib in zip(p['from_y'](), p['to_sibling'](1)):
                got.wait_recv()
                sib.start()

        @pl.when(jnp.logical_not(p['north']))
        def _():
            for got, sib in zip(p['from_x'](), p['to_sibling'](0)):
                got.wait_recv()
                sib.start()
        for got, sib in zip(p['from_diag'](), p['to_sibling'](2)):
            got.wait_recv()
            sib.start()

    def finish(rin, out, sems):
        p = plan(rin, out, sems)
        for cp in p['from_sibling']():
            cp.wait_recv()
        for cp in p['own']() + p['to_sibling'](0) + p['to_sibling'](1) + p['to_sibling'](2):
            cp.wait_send()

        @pl.when(p['north'])
        def _():
            for cp in p['x_block_on_to_y']():
                cp.wait_send()

        @pl.when(jnp.logical_not(p['north']))
        def _():
            for cp in p['y_block_on_to_x']():
                cp.wait_send()
        for cp in p['local']():
            cp.wait()

    out_shape = [jax.ShapeDtypeStruct(_full_shape(k, m, s.shape), s.dtype) for s, (k, m) in zip(shards, kinds)]
    scratch = [pltpu.SemaphoreType.DMA((n, 7)), pltpu.SemaphoreType.DMA((n, 7)), pltpu.SemaphoreType.DMA((n,))]
    return _Ride(shards, out_shape, scratch, start, finish, mids=((2, pass_diagonal), (3, pass_to_sibling)))


def _scatter_ride(grads, kinds):
    n = len(grads)

    def plan(g, out, sems):
        send_sems, recv_sems, local_sems = sems
        x, y, c = _my_coords()
        me = _dev_index(x, y, c)

        def local():
            return [pltpu.make_async_copy(_window(g[i], kinds[i][0], kinds[i][1], me), out[i].at[me], local_sems.at[i])
                    for i in range(n)]

        def remote(arrival):
            cps = []
            for rel in range(1, NDEV):
                peer = _peer(x, y, c, rel)
                dev = _dev_index(*peer)
                for i in range(n):
                    kind, m = kinds[i]
                    cps.append(pltpu.make_async_remote_copy(
                        src_ref=_window(g[i], kind, m, me if arrival else dev), dst_ref=out[i].at[dev if arrival else me],
                        send_sem=send_sems.at[i, rel - 1], recv_sem=recv_sems.at[i, rel - 1], device_id=peer, device_id_type=MESH))
            return cps

        return local, remote

    def start(g, out, sems):
        local, remote = plan(g, out, sems)
        for cp in local() + remote(False):
            cp.start()

    def finish(g, out, sems):
        local, remote = plan(g, out, sems)
        for cp in remote(True):
            cp.wait_recv()
        for cp in remote(False):
            cp.wait_send()
        for cp in local():
            cp.wait()

    def block_shape(gr, kind, m):
        return (m, gr.shape[1]) if kind == 'row' else (gr.shape[0], m)

    out_shape = [jax.ShapeDtypeStruct((NDEV,) + block_shape(gr, k, m), gr.dtype) for gr, (k, m) in zip(grads, kinds)]
    scratch = [pltpu.SemaphoreType.DMA((n, NDEV - 1)), pltpu.SemaphoreType.DMA((n, NDEV - 1)), pltpu.SemaphoreType.DMA((n,))]
    return _Ride(grads, out_shape, scratch, start, finish)


def _rms_stats(xf):
    r = lax.rsqrt(jnp.mean(xf * xf, axis=-1, keepdims=True) + EPS)
    return xf * r, r


def _rms_bwd(dh, g, xhat, r):
    dxhat = dh * g
    return r * (dxhat - xhat * jnp.mean(dxhat * xhat, axis=-1, keepdims=True))


def _ffn_blocks():
    edges = (0,) + FF_BLOCK_EDGES + (FF,)
    return [(slice(lo, hi), slice(FF + lo, FF + hi)) for lo, hi in zip(edges[:-1], edges[1:])]


def _swiglu_tile(x_ref, g_ref, wut_ref, wd_ref, ab_ref):
    xf = x_ref[...]
    xhat, _ = _rms_stats(xf)
    h = (xhat * g_ref[...]).astype(BF16)
    acc = jnp.zeros(xf.shape, F32)
    for ra, rb in _ffn_blocks():
        a = _dot_nt(h, wut_ref[ra, :])
        b = _dot_nt(h, wut_ref[rb, :])
        ab_ref[:, ra] = a.astype(BF16)
        ab_ref[:, rb] = b.astype(BF16)
        act = (a * _sigmoid(a) * b).astype(BF16)
        acc = acc + _dot(act, wd_ref[ra, :])
    return xf + 0.5 * acc


def _ffn_fwd(x, g, wut, wd, tm, name, ride=None):
    T = x.shape[0]

    def body(x_ref, g_ref, wut_ref, wd_ref, xo_ref, ab_ref):
        xo_ref[...] = _swiglu_tile(x_ref, g_ref, wut_ref, wd_ref, ab_ref)

    return _pallas(
        body, name, (T // tm,),
        [pl.BlockSpec((tm, D), lambda t: (t, 0)), _const((1, D)), _const((2 * FF, D)), _const((FF, D))],
        [pl.BlockSpec((tm, D), lambda t: (t, 0)), pl.BlockSpec((tm, 2 * FF), lambda t: (t, 0))],
        [jax.ShapeDtypeStruct((T, D), F32), jax.ShapeDtypeStruct((T, 2 * FF), BF16)],
        (x, g, wut, wd), sem=("arbitrary",), ride=ride)


def _ffn_fwd_loss(x, g, wut, wd, g_final, target, tm, name):
    T = x.shape[0]

    def body(x_ref, g_ref, wut_ref, wd_ref, gf_ref, t_ref, dx_ref, ab_ref, loss_ref, dgf_ref):
        xhat, r = _rms_stats(_swiglu_tile(x_ref, g_ref, wut_ref, wd_ref, ab_ref))
        gain = gf_ref[...]
        diff = xhat * gain - t_ref[...]
        dout = diff * (1.0 / D)

        @pl.when(pl.program_id(0) == 0)
        def _():
            loss_ref[...] = jnp.zeros_like(loss_ref)
            dgf_ref[...] = jnp.zeros_like(dgf_ref)
        sq = jnp.sum(jnp.sum(diff * diff, axis=0, keepdims=True), axis=1, keepdims=True)
        loss_ref[...] += jnp.broadcast_to(sq * (0.5 / D), (1, 128))
        dgf_ref[...] += jnp.sum(dout * xhat, axis=0, keepdims=True)
        dx_ref[...] = _rms_bwd(dout, gain, xhat, r)

    row = pl.BlockSpec((tm, D), lambda t: (t, 0))
    return pl.pallas_call(
        body, grid=(T // tm,), name=name,
        in_specs=[row, _const((1, D)), _const((2 * FF, D)), _const((FF, D)), _const((1, D)), row],
        out_specs=[row, pl.BlockSpec((tm, 2 * FF), lambda t: (t, 0)), pl.BlockSpec((1, 128), lambda t: (0, 0)),
                   pl.BlockSpec((1, D), lambda t: (0, 0))],
        out_shape=[jax.ShapeDtypeStruct((T, D), F32), jax.ShapeDtypeStruct((T, 2 * FF), BF16),
                   jax.ShapeDtypeStruct((1, 128), F32), jax.ShapeDtypeStruct((1, D), F32)],
        compiler_params=_params("arbitrary"),
    )(x, g, wut, wd, g_final, target)


def _ffn_bwd(x, dy, ab, g, wut, wd, tm, name):
    T = x.shape[0]

    def body(x_ref, dy_ref, ab_ref, g_ref, wut_ref, wd_ref, dx_ref, dab_ref, act_ref, h_ref, dg_ref):
        xf = x_ref[...]
        xhat, r = _rms_stats(xf)
        gain = g_ref[...]
        h_ref[...] = (xhat * gain).astype(BF16)
        dy = dy_ref[...]
        dyh = (0.5 * dy).astype(BF16)
        dh = jnp.zeros((tm, D), F32)
        for ra, rb in _ffn_blocks():
            a = ab_ref[:, ra].astype(F32)
            b = ab_ref[:, rb].astype(F32)
            dact = _dot_nt(dyh, wd_ref[ra, :])
            sg = _sigmoid(a)
            sl = a * sg
            act_ref[:, ra] = (sl * b).astype(BF16)
            da = (dact * b * (sg * (1.0 + a * (1.0 - sg)))).astype(BF16)
            db = (dact * sl).astype(BF16)
            dab_ref[:, ra] = da
            dab_ref[:, rb] = db
            dh = dh + _dot(da, wut_ref[ra, :]) + _dot(db, wut_ref[rb, :])
        dx_ref[...] = dy + _rms_bwd(dh, gain, xhat, r)

        @pl.when(pl.program_id(0) == 0)
        def _():
            dg_ref[...] = jnp.zeros_like(dg_ref)
        dg_ref[...] += jnp.sum(dh * xhat, axis=0, keepdims=True)

    return pl.pallas_call(
        body, grid=(T // tm,), name=name,
        in_specs=[pl.BlockSpec((tm, D), lambda t: (t, 0)), pl.BlockSpec((tm, D), lambda t: (t, 0)),
                  pl.BlockSpec((tm, 2 * FF), lambda t: (t, 0)), _const((1, D)), _const((2 * FF, D)), _const((FF, D))],
        out_specs=[pl.BlockSpec((tm, D), lambda t: (t, 0)), pl.BlockSpec((tm, 2 * FF), lambda t: (t, 0)),
                   pl.BlockSpec((tm, FF), lambda t: (t, 0)), pl.BlockSpec((tm, D), lambda t: (t, 0)),
                   pl.BlockSpec((1, D), lambda t: (0, 0))],
        out_shape=[jax.ShapeDtypeStruct((T, D), F32), jax.ShapeDtypeStruct((T, 2 * FF), BF16),
                   jax.ShapeDtypeStruct((T, FF), BF16), jax.ShapeDtypeStruct((T, D), BF16), jax.ShapeDtypeStruct((1, D), F32)],
        compiler_params=_params("arbitrary"),
    )(x, dy, ab, g, wut, wd)


def _tn_matmul(xm, ym, tn, name, scale=None, out_cols=None, col_off=0, prev=None, tt=TILE_GRAD_TOKENS, x_part=(0, 1),
               out_rows=None, y_part=(0, 1), ride=None, after=None):
    T = xm.shape[0]
    xi, xn = x_part
    yi, yn = y_part
    K = xm.shape[1] // xn
    N = ym.shape[1] // yn
    out_cols = N if out_cols is None else out_cols
    row_blk = xi if out_rows is not None else 0
    out_rows = K if out_rows is None else out_rows
    tt = min(tt, T)
    nt = T // tt
    off = col_off // tn

    def body(*refs):
        x_ref, y_ref = refs[0], refs[1]
        o_ref, acc = refs[-2], refs[-1]

        @pl.when(pl.program_id(1) == 0)
        def _():
            acc[...] = jnp.zeros_like(acc)
        acc[...] += _dot_tn(x_ref[...].astype(BF16), y_ref[...].astype(BF16))

        @pl.when(pl.program_id(1) == nt - 1)
        def _():
            res = acc[...]
            o_ref[...] = (res if scale is None else res * scale).astype(BF16)

    ycol = yi * (N // tn)
    in_specs = [pl.BlockSpec((tt, K), lambda n, t: (t, xi)), pl.BlockSpec((tt, tn), lambda n, t: (t, n + ycol))]
    args = [xm, ym]
    aliases = {}
    if prev is not None:
        in_specs.append(ANY)
        args.append(prev)
        aliases = {2: 0}
    outs, rode = _pallas(
        body, name, (N // tn, nt), in_specs, [pl.BlockSpec((K, tn), lambda n, t: (row_blk, n + off))],
        [jax.ShapeDtypeStruct((out_rows, out_cols), BF16)], args, scratch_shapes=[pltpu.VMEM((K, tn), F32)],
        sem=("parallel", "arbitrary"), aliases=aliases, ride=ride, after=after)
    return outs[0], rode


def _mix_fwd(x, g, w_in, tm, ride=None):
    T = x.shape[0]

    def body(x_ref, g_ref, w_ref, uc_ref, qkv_ref, mq_ref, gl_ref, h_ref):
        xhat, _ = _rms_stats(x_ref[...])
        h = (xhat * g_ref[...]).astype(BF16)
        h_ref[...] = h
        uc_ref[...] = _dot(h, w_ref[:, 0:1024])
        qkv_ref[...] = _dot(h, w_ref[:, 1024:2560]).astype(BF16)
        mq_ref[...] = _dot(h, w_ref[:, 2560:3072]).astype(BF16)
        for j in range(3):
            gl_ref[:, j * D:(j + 1) * D] = _dot(h, w_ref[:, 3072 + j * D:3072 + (j + 1) * D]).astype(BF16)

    row = lambda w: pl.BlockSpec((tm, w), lambda t: (t, 0))
    return _pallas(
        body, "mix_fwd", (T // tm,), [row(D), _const((1, D)), _const((D, IN_COLS))],
        [row(1024), row(1536), row(512), row(3072), row(D)],
        [jax.ShapeDtypeStruct((T, 1024), F32), jax.ShapeDtypeStruct((T, 1536), BF16), jax.ShapeDtypeStruct((T, 512), BF16),
         jax.ShapeDtypeStruct((T, 3072), BF16), jax.ShapeDtypeStruct((T, D), BF16)],
        (x, g, w_in), sem=("parallel",), ride=ride)


def _mix_bwd(x, dres, duc, dqkv, dmq, dgl, g, w_in, tm, ride=None, after=None):
    T = x.shape[0]

    def body(x_ref, dres_ref, duc_ref, dqkv_ref, dmq_ref, dgl_ref, g_ref, w_ref, dx_ref, dg_ref):
        xhat, r = _rms_stats(x_ref[...])
        dh = _dot_nt(duc_ref[...], w_ref[:, 0:1024])
        dh = dh + _dot_nt(dqkv_ref[...], w_ref[:, 1024:2560])
        dh = dh + _dot_nt(dmq_ref[...], w_ref[:, 2560:3072])
        dh = dh + _dot_nt(dgl_ref[...], w_ref[:, 3072:6144])
        dx_ref[...] = dres_ref[...] + _rms_bwd(dh, g_ref[...], xhat, r)

        @pl.when(pl.program_id(0) == 0)
        def _():
            dg_ref[...] = jnp.zeros_like(dg_ref)
        dg_ref[...] += jnp.sum(dh * xhat, axis=0, keepdims=True)

    row = lambda w: pl.BlockSpec((tm, w), lambda t: (t, 0))
    return _pallas(
        body, "mix_bwd", (T // tm,),
        [row(D), row(D), row(1024), row(1536), row(512), row(3072), _const((1, D)), _const((D, IN_COLS))],
        [row(D), pl.BlockSpec((1, D), lambda t: (0, 0))],
        [jax.ShapeDtypeStruct((T, D), F32), jax.ShapeDtypeStruct((1, D), F32)],
        (x, dres, duc, dqkv, dmq, dgl, g, w_in), sem=("arbitrary",), ride=ride, after=after)


def _shifted(win, base, copies):
    for k in range(8):
        copies[k] = win[base + k:base + k + CONV_CHUNK + 24]
    return copies


def _tap_slices(copies, tap):
    out = []
    for k in range(8):
        for a in range(4):
            j = tap(a, k)
            if 0 <= j < CONV_K:
                out.append((j, copies[k, pl.ds(8 * a, CONV_CHUNK), :]))
    return out


def _conv_taps(copies, w_ref, tap):
    acc = jnp.zeros((CONV_CHUNK, CONV_W), F32)
    for j, rows in _tap_slices(copies, tap):
        acc = acc + rows * w_ref[j:j + 1, :]
    return acc


def _fold8(v):
    acc = v[0:8]
    for r in range(8, CONV_CHUNK, 8):
        acc = acc + v[r:r + 8]
    return acc


def _glu_into(uc_ref, vpad, S):
    vpad[pl.ds(0, CONV_HALO), :] = jnp.zeros((CONV_HALO, CONV_W), F32)
    vpad[pl.ds(S + CONV_HALO, CONV_HALO), :] = jnp.zeros((CONV_HALO, CONV_W), F32)

    def glu(i, carry):
        r0 = pl.multiple_of(i * GLU_CHUNK, GLU_CHUNK)
        a = uc_ref[0, pl.ds(r0, GLU_CHUNK), 0:CONV_W]
        gt = uc_ref[0, pl.ds(r0, GLU_CHUNK), CONV_W:2 * CONV_W]
        vpad[pl.ds(pl.multiple_of(r0 + CONV_HALO, CONV_HALO), GLU_CHUNK), :] = a * _sigmoid(gt)
        return carry
    lax.fori_loop(0, S // GLU_CHUNK, glu, 0)


def _layer_norm(z, vec_ref):
    xc = z - jnp.mean(z, axis=-1, keepdims=True)
    rstd = lax.rsqrt(jnp.mean(xc * xc, axis=-1, keepdims=True) + EPS)
    xn = xc * rstd
    return xn, rstd, xn * vec_ref[1:2, :] + vec_ref[2:3, :]


def _conv_fwd(uc, dw_w, vec):
    NB, S, _ = uc.shape

    def body(uc_ref, w_ref, vec_ref, o_ref, z_ref, vpad, copies):
        _glu_into(uc_ref, vpad, S)

        def conv(i, carry):
            r0 = pl.multiple_of(i * CONV_CHUNK, CONV_CHUNK)
            win = vpad[pl.ds(r0, CONV_WIN), :]
            z = _conv_taps(_shifted(win, CONV_HALO - (CONV_K - 1), copies), w_ref, lambda a, k: 8 * a + k) + vec_ref[0:1, :]
            z_ref[0, pl.ds(r0, CONV_CHUNK), :] = z
            _, _, yln = _layer_norm(z, vec_ref)
            o_ref[0, pl.ds(r0, CONV_CHUNK), :] = (yln * _sigmoid(yln)).astype(BF16)
            return carry
        lax.fori_loop(0, S // CONV_CHUNK, conv, 0, unroll=4)

    seq = pl.BlockSpec((1, S, CONV_W), lambda b: (b, 0, 0))
    return pl.pallas_call(
        body, grid=(NB,), name="conv_fwd",
        in_specs=[pl.BlockSpec((1, S, 2 * CONV_W), lambda b: (b, 0, 0)), _const((CONV_K, CONV_W)), _const((8, CONV_W))],
        out_specs=[seq, seq],
        out_shape=[jax.ShapeDtypeStruct((NB, S, CONV_W), BF16), jax.ShapeDtypeStruct((NB, S, CONV_W), F32)],
        scratch_shapes=[pltpu.VMEM((S + 2 * CONV_HALO, CONV_W), F32), pltpu.VMEM((8, CONV_CHUNK + 24, CONV_W), F32)],
        compiler_params=_params("parallel"),
    )(uc, dw_w, vec)


def _conv_bwd(uc, z, dcact, dw_w, vec, ride=None):
    NB, S, _ = uc.shape
    n_chunks = S // CONV_CHUNK

    def body(uc_ref, z_ref, dc_ref, w_ref, vec_ref, duc_ref, dw_ref, dvec_ref, vpad, dzpad, dw8, dvec8, copies):
        @pl.when(pl.program_id(0) == 0)
        def _():
            dw8[...] = jnp.zeros_like(dw8)
            dvec8[...] = jnp.zeros_like(dvec8)
        _glu_into(uc_ref, vpad, S)
        dzpad[pl.ds(S, 2 * CONV_HALO), :] = jnp.zeros((2 * CONV_HALO, CONV_W), F32)

        def norm_bwd(i, carry):
            r0 = pl.multiple_of(i * CONV_CHUNK, CONV_CHUNK)
            xn, rstd, yln = _layer_norm(z_ref[0, pl.ds(r0, CONV_CHUNK), :], vec_ref)
            sg = _sigmoid(yln)
            dyln = dc_ref[0, pl.ds(r0, CONV_CHUNK), :] * (sg * (1.0 + yln * (1.0 - sg)))
            dxn = dyln * vec_ref[1:2, :]
            dz = rstd * (dxn - jnp.mean(dxn, axis=-1, keepdims=True) - xn * jnp.mean(dxn * xn, axis=-1, keepdims=True))
            dzpad[pl.ds(r0, CONV_CHUNK), :] = dz
            dvec8[0] += _fold8(dz)
            dvec8[1] += _fold8(dyln * xn)
            dvec8[2] += _fold8(dyln)
            return carry
        lax.fori_loop(0, n_chunks, norm_bwd, 0, unroll=4)

        def taps_bwd(i, carry):
            r0 = pl.multiple_of(i * CONV_CHUNK, CONV_CHUNK)
            dzwin = dzpad[pl.ds(r0, CONV_WIN), :]
            dv = _conv_taps(_shifted(dzwin, 0, copies), w_ref, lambda a, k: CONV_K - 1 - 8 * a - k)
            dz = dzwin[0:CONV_CHUNK]
            vwin = vpad[pl.ds(r0, CONV_WIN), :]
            for j, rows in _tap_slices(_shifted(vwin, CONV_HALO - (CONV_K - 1), copies), lambda a, k: 8 * a + k):
                dw8[j] += _fold8(dz * rows)
            a = uc_ref[0, pl.ds(r0, CONV_CHUNK), 0:CONV_W]
            sg = _sigmoid(uc_ref[0, pl.ds(r0, CONV_CHUNK), CONV_W:2 * CONV_W])
            duc_ref[0, pl.ds(r0, CONV_CHUNK), 0:CONV_W] = (dv * sg).astype(BF16)
            duc_ref[0, pl.ds(r0, CONV_CHUNK), CONV_W:2 * CONV_W] = (dv * a * sg * (1.0 - sg)).astype(BF16)
            return carry
        lax.fori_loop(0, n_chunks, taps_bwd, 0, unroll=2)

        @pl.when(pl.program_id(0) == NB - 1)
        def _():
            dw_ref[...] = jnp.zeros_like(dw_ref)
            dvec_ref[...] = jnp.zeros_like(dvec_ref)
            for j in range(CONV_K):
                dw_ref[j:j + 1, :] = jnp.sum(dw8[j], axis=0, keepdims=True)
            for j in range(3):
                dvec_ref[j:j + 1, :] = jnp.sum(dvec8[j], axis=0, keepdims=True)

    return _pallas(
        body, "conv_bwd", (NB,),
        [pl.BlockSpec((1, S, 2 * CONV_W), lambda b: (b, 0, 0)), pl.BlockSpec((1, S, CONV_W), lambda b: (b, 0, 0)),
         pl.BlockSpec((1, S, CONV_W), lambda b: (b, 0, 0)), _const((CONV_K, CONV_W)), _const((8, CONV_W))],
        [pl.BlockSpec((1, S, 2 * CONV_W), lambda b: (b, 0, 0)), pl.BlockSpec((32, CONV_W), lambda b: (0, 0)),
         pl.BlockSpec((8, CONV_W), lambda b: (0, 0))],
        [jax.ShapeDtypeStruct((NB, S, 2 * CONV_W), BF16), jax.ShapeDtypeStruct((32, CONV_W), F32),
         jax.ShapeDtypeStruct((8, CONV_W), F32)],
        (uc, z, dcact, dw_w, vec),
        scratch_shapes=[pltpu.VMEM((S + 2 * CONV_HALO, CONV_W), F32), pltpu.VMEM((S + 2 * CONV_HALO, CONV_W), F32),
                        pltpu.VMEM((CONV_K, 8, CONV_W), F32), pltpu.VMEM((3, 8, CONV_W), F32),
                        pltpu.VMEM((8, CONV_CHUNK + 24, CONV_W), F32)],
        sem=("arbitrary",), ride=ride)


def _rel_index_of_column(cols):
    offset = jnp.where(cols < KWIN, cols, cols - DS_LANES)
    return jnp.clip(KPAD - offset, -(CHUNK - 1), MAX_REL) + (CHUNK - 1)


def _bias_table(rel_bias, ride=None):
    def body(rb_ref, o_ref, by_offset, first8):
        ridx = _rel_index_of_column(lax.broadcasted_iota(jnp.int32, (1, DS_LANES), 1))
        onehot = (ridx == lax.broadcasted_iota(jnp.int32, (N_REL, 1), 0)).astype(F32)
        by_offset[...] = jnp.dot(rb_ref[...], onehot, preferred_element_type=F32, precision=lax.Precision.HIGHEST)
        sub = lax.broadcasted_iota(jnp.int32, (8, 1), 0)
        kchunk = lax.broadcasted_iota(jnp.int32, (1, KWIN), 1) // CHUNK
        for head in range(ATT_HEADS):
            base = jnp.broadcast_to(by_offset[head:head + 1, :], (8, DS_LANES))
            rows = base
            for s in range(1, 8):
                rows = jnp.where(sub == s, pltpu.roll(base, s, 1), rows)
            first8[head] = rows

        def rows8(q8, carry):
            qchunk = (q8 * 8 + sub) // CHUNK
            band = (kchunk >= qchunk) & (kchunk <= qchunk + LEFT_CHUNKS)
            for head in range(ATT_HEADS):
                tile = pltpu.roll(first8[head], q8 * 8, 1)[:, 0:KWIN]
                o_ref[head, pl.ds(pl.multiple_of(q8 * 8, 8), 8), :] = jnp.where(band, tile, MASK_VALUE)
            return carry
        lax.fori_loop(0, QB // 8, rows8, 0)

    outs, rode = _pallas(
        body, "bias_table", (1,), [pl.BlockSpec((ATT_HEADS, N_REL), lambda i: (0, 0))],
        [pl.BlockSpec((ATT_HEADS, QB, KWIN), lambda i: (0, 0, 0))], [jax.ShapeDtypeStruct((ATT_HEADS, QB, KWIN), F32)], (rel_bias,),
        scratch_shapes=[pltpu.VMEM((ATT_HEADS, DS_LANES), F32), pltpu.VMEM((ATT_HEADS, 8, DS_LANES), F32)],
        sem=("arbitrary",), ride=ride)
    return outs[0], rode


def _load_keys(i, k_ref, v_ref, kpad, vpad, S):
    @pl.when(i == 0)
    def _():
        kpad[pl.ds(0, KPAD), :] = jnp.zeros((KPAD, ATT_W), BF16)
        vpad[pl.ds(0, KPAD), :] = jnp.zeros((KPAD, ATT_W), BF16)
        kpad[pl.ds(KPAD, S), :] = k_ref[0]
        vpad[pl.ds(KPAD, S), :] = v_ref[0]


def _att_scores(q2s, k2, tab_ref, head, in_head, in_seq):
    qm = jnp.where(in_head, q2s, jnp.zeros_like(q2s))
    s = _dot_nt(qm, k2) + tab_ref[head]
    return s if in_seq is None else jnp.where(in_seq, s, MASK_VALUE)


def _by_window(i, step):
    in_seq = (lax.broadcasted_iota(jnp.int32, (1, KWIN), 1) + i * QB) >= KPAD
    pl.when(i < KPAD // QB)(lambda: step(in_seq))
    pl.when(i >= KPAD // QB)(lambda: step(None))


def _scaled(q2):
    return q2 * jnp.asarray(ATT_HD ** -0.5, q2.dtype)


def _att_fwd(qkv, tab, ride=None):
    NB, S, _ = qkv.shape

    def body(q_ref, k_ref, v_ref, tab_ref, o_ref, lse_ref, kpad, vpad):
        i = pl.program_id(1)
        _load_keys(i, k_ref, v_ref, kpad, vpad, S)
        koff = pl.multiple_of(i * QB, QB)
        lane = lax.broadcasted_iota(jnp.int32, (1, 128), 1)

        def step(in_seq):
            lse = jnp.zeros((QB, 128), F32)
            for pair in range(ATT_HEADS // 2):
                cols = slice(pair * 128, (pair + 1) * 128)
                q2s = _scaled(q_ref[0, :, cols])
                k2 = kpad[pl.ds(koff, KWIN), cols]
                v2 = vpad[pl.ds(koff, KWIN), cols]
                o2 = jnp.zeros((QB, 128), F32)
                for hh in range(2):
                    head = 2 * pair + hh
                    in_head = (lane // ATT_HD) == hh
                    s = _att_scores(q2s, k2, tab_ref, head, in_head, in_seq)
                    m = jnp.max(s, axis=-1, keepdims=True)
                    e = jnp.exp(s - m)
                    l = jnp.sum(e, axis=-1, keepdims=True)
                    o2 = jnp.where(in_head, _dot(e.astype(BF16), v2) * (1.0 / l), o2)
                    lse = jnp.where(lane == head, m + jnp.log(l), lse)
                o_ref[0, :, cols] = o2.astype(BF16)
            lse_ref[0] = lse
        _by_window(i, step)

    seq = lambda col: pl.BlockSpec((1, S, ATT_W), lambda b, i: (b, 0, col), pipeline_mode=pl.Buffered(1))
    outs, rode = _pallas(
        body, "att_fwd", (NB, S // QB),
        [pl.BlockSpec((1, QB, ATT_W), lambda b, i: (b, i, 0)), seq(1), seq(2), _const((ATT_HEADS, QB, KWIN))],
        [pl.BlockSpec((1, QB, ATT_W), lambda b, i: (b, i, 0)), pl.BlockSpec((1, QB, 128), lambda b, i: (b, i, 0))],
        [jax.ShapeDtypeStruct((NB, S, ATT_W), BF16), jax.ShapeDtypeStruct((NB, S, 128), F32)],
        (qkv, qkv, qkv, tab),
        scratch_shapes=[pltpu.VMEM((S + KPAD, ATT_W), BF16), pltpu.VMEM((S + KPAD, ATT_W), BF16)],
        sem=("arbitrary", "arbitrary"), ride=ride)
    return outs[0], outs[1], rode


def _att_bwd(qkv, o, lse, do, tab, ride=None):
    NB, S, _ = qkv.shape
    nq = S // QB

    def body(q_ref, k_ref, v_ref, o_ref, lse_ref, do_ref, tab_ref, dqkv_ref, ds_hbm, kpad, vpad, dkpad, dvpad, ds_acc, ds_sem):
        b, i = pl.program_id(0), pl.program_id(1)
        _load_keys(i, k_ref, v_ref, kpad, vpad, S)

        @pl.when(i == 0)
        def _():
            dkpad[...] = jnp.zeros_like(dkpad)
            dvpad[...] = jnp.zeros_like(dvpad)

        @pl.when((i == 0) & (b == 0))
        def _():
            ds_acc[...] = jnp.zeros_like(ds_acc)

        koff = pl.multiple_of(i * QB, QB)
        lane = lax.broadcasted_iota(jnp.int32, (1, 128), 1)

        def step(in_seq):
            for pair in range(ATT_HEADS // 2):
                cols = slice(pair * 128, (pair + 1) * 128)
                q2s = _scaled(q_ref[0, :, cols])
                do2 = do_ref[0, :, cols]
                k2 = kpad[pl.ds(koff, KWIN), cols]
                v2 = vpad[pl.ds(koff, KWIN), cols]
                do_o = do2.astype(F32) * o_ref[0, :, cols].astype(F32)
                dq2 = jnp.zeros((QB, 128), F32)
                dk2 = jnp.zeros((KWIN, 128), F32)
                dv2 = jnp.zeros((KWIN, 128), F32)
                for hh in range(2):
                    head = 2 * pair + hh
                    in_head = (lane // ATT_HD) == hh
                    p = jnp.exp(_att_scores(q2s, k2, tab_ref, head, in_head, in_seq) - lse_ref[0, :, head:head + 1])
                    row_term = jnp.sum(jnp.where(in_head, do_o, 0.0), axis=-1, keepdims=True)
                    dom = jnp.where(in_head, do2, jnp.zeros_like(do2))
                    ds = p * (_dot_nt(dom, v2) - row_term)
                    ds_acc[head] += ds
                    dsb = ds.astype(BF16)
                    dq2 = jnp.where(in_head, _dot(dsb, k2), dq2)
                    dk2 = jnp.where(in_head, _dot_tn(dsb, q2s), dk2)
                    dv2 = jnp.where(in_head, _dot_tn(p.astype(BF16), do2), dv2)
                dqkv_ref[0, pl.ds(koff, QB), cols] = (dq2 * (ATT_HD ** -0.5)).astype(BF16)
                dkpad[pl.ds(koff, KWIN), cols] += dk2
                dvpad[pl.ds(koff, KWIN), cols] += dv2
        _by_window(i, step)

        @pl.when(i == nq - 1)
        def _():
            dqkv_ref[0, :, ATT_W:2 * ATT_W] = dkpad[pl.ds(KPAD, S), :].astype(BF16)
            dqkv_ref[0, :, 2 * ATT_W:3 * ATT_W] = dvpad[pl.ds(KPAD, S), :].astype(BF16)

        @pl.when((i == nq - 1) & (b == NB - 1))
        def _():
            out = pltpu.make_async_copy(ds_acc, ds_hbm, ds_sem)
            out.start()
            out.wait()

    seq = lambda col: pl.BlockSpec((1, S, ATT_W), lambda b, i: (b, 0, col), pipeline_mode=pl.Buffered(1))
    rows = pl.BlockSpec((1, QB, ATT_W), lambda b, i: (b, i, 0))
    return _pallas(
        body, "att_bwd", (NB, nq),
        [rows, seq(1), seq(2), rows, pl.BlockSpec((1, QB, 128), lambda b, i: (b, i, 0)), rows, _const((ATT_HEADS, QB, KWIN))],
        [pl.BlockSpec((1, S, 3 * ATT_W), lambda b, i: (b, 0, 0)), ANY],
        [jax.ShapeDtypeStruct((NB, S, 3 * ATT_W), BF16), jax.ShapeDtypeStruct((ATT_HEADS, QB, KWIN), F32)],
        (qkv, qkv, qkv, o, lse, do, tab),
        scratch_shapes=[pltpu.VMEM((S + KPAD, ATT_W), BF16), pltpu.VMEM((S + KPAD, ATT_W), BF16),
                        pltpu.VMEM((S + KPAD, ATT_W), F32), pltpu.VMEM((S + KPAD, ATT_W), F32),
                        pltpu.VMEM((ATT_HEADS, QB, KWIN), F32), pltpu.SemaphoreType.DMA],
        sem=("arbitrary", "arbitrary"), ride=ride)


def _rel_bias_grad(ds):
    def body(ds_ref, o_ref):
        sub = lax.broadcasted_iota(jnp.int32, (8, 1), 0)
        ridx = _rel_index_of_column(lax.broadcasted_iota(jnp.int32, (DS_LANES, 1), 0))
        onehot = (ridx == lax.broadcasted_iota(jnp.int32, (1, N_REL), 1)).astype(F32)
        def rows8(q8, accs):
            shift = lax.rem(DS_LANES - q8 * 8, DS_LANES)
            out = []
            for head in range(ATT_HEADS):
                tile = ds_ref[head, pl.ds(pl.multiple_of(q8 * 8, 8), 8), :]
                tile = jnp.concatenate([tile, jnp.zeros((8, DS_LANES - KWIN), F32)], axis=1)
                out.append(accs[head] + pltpu.roll(tile, shift, 1))
            return tuple(out)
        accs = lax.fori_loop(0, QB // 8, rows8, tuple(jnp.zeros((8, DS_LANES), F32) for _ in range(ATT_HEADS)))
        for head in range(ATT_HEADS):
            acc = accs[head]
            diag = jnp.zeros((8, DS_LANES), F32)
            for s in range(8):
                shifted = acc if s == 0 else pltpu.roll(acc, DS_LANES - s, 1)
                diag = jnp.where(sub == s, shifted, diag)
            z = jnp.sum(diag, axis=0, keepdims=True)
            o_ref[head:head + 1, :] = jnp.dot(z, onehot, preferred_element_type=F32, precision=lax.Precision.HIGHEST)

    return pl.pallas_call(body, out_shape=jax.ShapeDtypeStruct((ATT_HEADS, N_REL), F32), name="rel_bias_grad",
                          compiler_params=_params())(ds)


def _memkv_fwd(mem, g, w_kv, tm):
    R = mem.shape[0]
    tm = min(tm, R)

    def body(m_ref, g_ref, w_ref, h_ref, kv_ref):
        xhat, _ = _rms_stats(m_ref[...])
        h = (xhat * g_ref[...]).astype(BF16)
        h_ref[...] = h
        kv_ref[...] = _dot(h, w_ref[...]).astype(BF16)

    row = pl.BlockSpec((tm, D), lambda t: (t, 0))
    return pl.pallas_call(
        body, grid=(R // tm,), name="memkv_fwd", in_specs=[row, _const((1, D)), _const((D, 2 * MEM_W))], out_specs=[row, row],
        out_shape=[jax.ShapeDtypeStruct((R, D), BF16), jax.ShapeDtypeStruct((R, 2 * MEM_W), BF16)],
        compiler_params=_params("parallel"),
    )(mem, g, w_kv)


def _memkv_bwd(mem, dkv, w_kv, tm):
    R = mem.shape[0]
    tm = min(tm, R)

    def body(m_ref, dkv_ref, w_ref, dg_ref):
        xhat, _ = _rms_stats(m_ref[...])
        dh = _dot_nt(dkv_ref[...].astype(BF16), w_ref[...])

        @pl.when(pl.program_id(0) == 0)
        def _():
            dg_ref[...] = jnp.zeros_like(dg_ref)
        dg_ref[...] += jnp.sum(dh * xhat, axis=0, keepdims=True)

    row = pl.BlockSpec((tm, D), lambda t: (t, 0))
    return pl.pallas_call(
        body, grid=(R // tm,), name="memkv_bwd", in_specs=[row, row, _const((D, 2 * MEM_W))],
        out_specs=pl.BlockSpec((1, D), lambda t: (0, 0)), out_shape=jax.ShapeDtypeStruct((1, D), F32),
        compiler_params=_params("arbitrary"),
    )(mem, dkv, w_kv)


def _mem_exp(qh, kh):
    s = _dot_nt(qh, kh) * (MEM_HD ** -0.5)
    e = jnp.exp(s - jnp.max(s, axis=-1, keepdims=True))
    return e, jnp.sum(e, axis=-1, keepdims=True)


def _mematt_fwd(mq, kv, tq):
    NB, S, _ = mq.shape
    M = kv.shape[1]

    def body(q_ref, kv_ref, o_ref):
        for h in range(MEM_HEADS):
            cols = slice(h * MEM_HD, (h + 1) * MEM_HD)
            e, l = _mem_exp(q_ref[0, :, cols], kv_ref[0, :, cols])
            o = _dot(e.astype(BF16), kv_ref[0, :, MEM_W + h * MEM_HD:MEM_W + (h + 1) * MEM_HD]) * (1.0 / l)
            o_ref[0, :, cols] = o.astype(BF16)

    return pl.pallas_call(
        body, grid=(NB, S // tq), name="mematt_fwd",
        in_specs=[pl.BlockSpec((1, tq, MEM_W), lambda b, i: (b, i, 0)), pl.BlockSpec((1, M, 2 * MEM_W), lambda b, i: (b, 0, 0))],
        out_specs=pl.BlockSpec((1, tq, MEM_W), lambda b, i: (b, i, 0)),
        out_shape=jax.ShapeDtypeStruct((NB, S, MEM_W), BF16), compiler_params=_params("parallel", "parallel"),
    )(mq, kv)


def _mematt_bwd(mq, kv, do, tq):
    NB, S, _ = mq.shape
    M = kv.shape[1]

    def body(q_ref, kv_ref, do_ref, dq_ref, dkv_ref):
        @pl.when(pl.program_id(1) == 0)
        def _():
            dkv_ref[...] = jnp.zeros_like(dkv_ref)
        for h in range(MEM_HEADS):
            cols = slice(h * MEM_HD, (h + 1) * MEM_HD)
            vcols = slice(MEM_W + h * MEM_HD, MEM_W + (h + 1) * MEM_HD)
            qh, kh, vh, doh = q_ref[0, :, cols], kv_ref[0, :, cols], kv_ref[0, :, vcols], do_ref[0, :, cols]
            e, l = _mem_exp(qh, kh)
            p = e * (1.0 / l)
            dp = _dot_nt(doh, vh)
            ds = p * (dp - jnp.sum(p * dp, axis=-1, keepdims=True))
            dss = (ds * (MEM_HD ** -0.5)).astype(BF16)
            dq_ref[0, :, cols] = _dot(dss, kh).astype(BF16)
            dkv_ref[0, :, cols] += _dot_tn(dss, qh)
            dkv_ref[0, :, vcols] += _dot_tn(p.astype(BF16), doh)

    qspec = pl.BlockSpec((1, tq, MEM_W), lambda b, i: (b, i, 0))
    kvspec = pl.BlockSpec((1, M, 2 * MEM_W), lambda b, i: (b, 0, 0))
    return pl.pallas_call(
        body, grid=(NB, S // tq), name="mematt_bwd", in_specs=[qspec, kvspec, qspec], out_specs=[qspec, kvspec],
        out_shape=[jax.ShapeDtypeStruct((NB, S, MEM_W), BF16), jax.ShapeDtypeStruct((NB, M, 2 * MEM_W), F32)],
        compiler_params=_params("arbitrary", "arbitrary"),
    )(mq, kv, do)


def _branch(j, in_ref, w_ref, gl_ref, bg_ref):
    y = _dot(in_ref[...], w_ref[...])
    gate = _sigmoid(gl_ref[:, j * D:(j + 1) * D].astype(F32) + bg_ref[:, j * D:(j + 1) * D])
    return y, gate


def _combine_fwd(x, cact, oatt, omem, gl, bg, wpw, wo, wmo, wout, tm):
    T = x.shape[0]

    def body(x_ref, c_ref, a_ref, m_ref, gl_ref, bg_ref, wpw_ref, wo_ref, wmo_ref, wout_ref, xo_ref, y_ref):
        y = None
        for j, (in_ref, w_ref) in enumerate(((c_ref, wpw_ref), (a_ref, wo_ref), (m_ref, wmo_ref))):
            yj, gate = _branch(j, in_ref, w_ref, gl_ref, bg_ref)
            y = gate * yj if y is None else y + gate * yj
        y = y.astype(BF16)
        y_ref[...] = y
        xo_ref[...] = x_ref[...] + _dot(y, wout_ref[...])

    row = lambda w: pl.BlockSpec((tm, w), lambda t: (t, 0))
    wbr = _const((512, D))
    return pl.pallas_call(
        body, grid=(T // tm,), name="combine_fwd",
        in_specs=[row(D), row(512), row(512), row(512), row(3 * D), _const((1, 3 * D)), wbr, wbr, wbr, _const((D, D))],
        out_specs=[row(D), row(D)],
        out_shape=[jax.ShapeDtypeStruct((T, D), F32), jax.ShapeDtypeStruct((T, D), BF16)],
        compiler_params=_params("parallel"),
    )(x, cact, oatt, omem, gl, bg, wpw, wo, wmo, wout)


def _combine_bwd(dx, cact, oatt, omem, gl, bg, wpw, wo, wmo, wout, tm, ride=None):
    T = dx.shape[0]

    def body(dx_ref, c_ref, a_ref, m_ref, gl_ref, bg_ref, wpw_ref, wo_ref, wmo_ref, wout_ref,
             dgl_ref, dc_ref, da_ref, dm_ref, dyc_ref, dya_ref, dym_ref, dbg_ref):
        dy = _dot_nt(dx_ref[...].astype(BF16), wout_ref[...])

        @pl.when(pl.program_id(0) == 0)
        def _():
            dbg_ref[...] = jnp.zeros_like(dbg_ref)
        branches = ((c_ref, wpw_ref, dyc_ref, dc_ref), (a_ref, wo_ref, dya_ref, da_ref), (m_ref, wmo_ref, dym_ref, dm_ref))
        for j, (in_ref, w_ref, dyb_ref, din_ref) in enumerate(branches):
            yj, gate = _branch(j, in_ref, w_ref, gl_ref, bg_ref)
            dyg = dy * gate
            dlogit = dyg * yj * (1.0 - gate)
            dgl_ref[:, j * D:(j + 1) * D] = dlogit.astype(BF16)
            dbg_ref[:, j * D:(j + 1) * D] += jnp.sum(dlogit, axis=0, keepdims=True)
            dyb = dyg.astype(BF16)
            dyb_ref[...] = dyb
            din_ref[...] = _dot_nt(dyb, w_ref[...]).astype(din_ref.dtype)

    row = lambda w: pl.BlockSpec((tm, w), lambda t: (t, 0))
    wbr = _const((512, D))
    sds = jax.ShapeDtypeStruct
    return _pallas(
        body, "combine_bwd", (T // tm,),
        [row(D), row(512), row(512), row(512), row(3 * D), _const((1, 3 * D)), wbr, wbr, wbr, _const((D, D))],
        [row(3 * D), row(512), row(512), row(512), row(D), row(D), row(D), pl.BlockSpec((1, 3 * D), lambda t: (0, 0))],
        [sds((T, 3 * D), BF16), sds((T, 512), F32), sds((T, 512), BF16), sds((T, 512), BF16),
         sds((T, D), BF16), sds((T, D), BF16), sds((T, D), BF16), sds((1, 3 * D), F32)],
        (dx, cact, oatt, omem, gl, bg, wpw, wo, wmo, wout), sem=("arbitrary",), ride=ride)


def _peer(x, y, c, rel):
    rx, ry, rc = (rel >> 2) & 1, (rel >> 1) & 1, rel & 1
    return ((1 - x) if rx else x, (1 - y) if ry else y, (1 - c) if rc else c)


def _all_sum_small(parts):
    n = len(parts)

    def body(*refs):
        p_refs, o_refs, slots = refs[:n], refs[n:2 * n], refs[2 * n:3 * n]
        send_sems, recv_sems = refs[3 * n:]
        x, y, c = _my_coords()
        me = _dev_index(x, y, c)

        def copy(i, rel, arrival):
            peer = _peer(x, y, c, rel)
            return pltpu.make_async_remote_copy(
                src_ref=p_refs[i], dst_ref=slots[i].at[_dev_index(*peer) if arrival else me],
                send_sem=send_sems.at[i, rel - 1], recv_sem=recv_sems.at[i, rel - 1], device_id=peer, device_id_type=MESH)

        for i in range(n):
            slots[i][me] = p_refs[i][...]
        for rel in range(1, NDEV):
            for i in range(n):
                copy(i, rel, False).start()
        for rel in range(1, NDEV):
            for i in range(n):
                copy(i, rel, True).wait_recv()
        for rel in range(1, NDEV):
            for i in range(n):
                copy(i, rel, False).wait_send()
        for i in range(n):
            total = slots[i][0]
            for d in range(1, NDEV):
                total = total + slots[i][d]
            o_refs[i][...] = total

    vmem = pl.BlockSpec(memory_space=pltpu.VMEM)
    return pl.pallas_call(
        body, out_shape=[jax.ShapeDtypeStruct(p.shape, F32) for p in parts], name="all_sum_small",
        in_specs=[vmem] * n, out_specs=[vmem] * n,
        scratch_shapes=[pltpu.VMEM((NDEV,) + p.shape, F32) for p in parts]
        + [pltpu.SemaphoreType.DMA((n, NDEV - 1)), pltpu.SemaphoreType.DMA((n, NDEV - 1))],
        compiler_params=pltpu.CompilerParams(has_side_effects=True),
    )(*parts)


HBM = pl.BlockSpec(memory_space=pltpu.HBM)
SEM = pl.BlockSpec(memory_space=pltpu.SEMAPHORE)


def _own_block(g, kind, m, tag):
    def body(g_ref, land_ref, staged, sem):
        me = _dev_index(*_my_coords())
        for cp in (pltpu.make_async_copy(_window(g_ref, kind, m, me), staged, sem),
                   pltpu.make_async_copy(staged, land_ref.at[me], sem)):
            cp.start()
            cp.wait()

    block = (m, g.shape[1]) if kind == 'row' else (g.shape[0], m)
    return pl.pallas_call(body, in_specs=[ANY], out_specs=ANY, out_shape=jax.ShapeDtypeStruct((NDEV,) + block, g.dtype),
                          scratch_shapes=[pltpu.VMEM(block, g.dtype), pltpu.SemaphoreType.DMA], name="own_block_" + tag)(g)


def _scatter_start(g, land, kind, m, tag):
    def body(g_ref, land_ref, send_sems, recv_sems, g_thru, land_thru, token):
        x, y, c = _my_coords()
        me = _dev_index(x, y, c)
        for rel in range(1, NDEV):
            peer = _peer(x, y, c, rel)
            pltpu.make_async_remote_copy(src_ref=_window(g_ref, kind, m, _dev_index(*peer)), dst_ref=land_ref.at[me],
                                         send_sem=send_sems.at[rel - 1], recv_sem=recv_sems.at[rel - 1],
                                         device_id=peer, device_id_type=MESH).start()
        token[...] = jnp.zeros_like(token)

    return pl.pallas_call(
        body, name="scatter_start_" + tag,
        out_shape=(pltpu.SemaphoreType.DMA((NDEV - 1,)), pltpu.SemaphoreType.DMA((NDEV - 1,)), pltpu.HBM(g.shape, g.dtype),
                   pltpu.HBM(land.shape, land.dtype), jax.ShapeDtypeStruct((8, 128), F32)),
        in_specs=(HBM, HBM), out_specs=(SEM, SEM, HBM, HBM, pl.BlockSpec(memory_space=pltpu.VMEM)),
        input_output_aliases={0: 2, 1: 3},
        compiler_params=pltpu.CompilerParams(has_side_effects=pltpu.SideEffectType.DATAFLOW_SIDE_EFFECTING),
    )(pltpu.with_memory_space_constraint(g, pltpu.HBM), pltpu.with_memory_space_constraint(land, pltpu.HBM))


def _scatter_wait(send_sems, recv_sems, g_thru, land_thru, after, kind, m, tag):
    n_after = len(after)

    def body(*refs):
        g_ref, land_ref, send_sems, recv_sems = refs[:4]
        x, y, c = _my_coords()
        me = _dev_index(x, y, c)
        for rel in range(1, NDEV):
            peer = _peer(x, y, c, rel)
            dev = _dev_index(*peer)
            cp = pltpu.make_async_remote_copy(src_ref=_window(g_ref, kind, m, me), dst_ref=land_ref.at[dev],
                                              send_sem=send_sems.at[rel - 1], recv_sem=recv_sems.at[rel - 1],
                                              device_id=peer, device_id_type=MESH)
            cp.wait_send()
            cp.wait_recv()

    return pl.pallas_call(
        body, name="scatter_wait_" + tag,
        out_shape=(pltpu.HBM(g_thru.shape, g_thru.dtype), pltpu.HBM(land_thru.shape, land_thru.dtype)),
        in_specs=(HBM, HBM, SEM, SEM) + (ANY,) * n_after, out_specs=(HBM, HBM), input_output_aliases={0: 0, 1: 1},
        compiler_params=pltpu.CompilerParams(has_side_effects=pltpu.SideEffectType.DATAFLOW_SIDE_EFFECTING),
    )(g_thru, land_thru, send_sems, recv_sems, *after)[1]


def _adamw_math(w, g, m, v):
    m = ADAM_B1 * m + (1.0 - ADAM_B1) * g
    v = ADAM_B2 * v + (1.0 - ADAM_B2) * (g * g)
    m_hat = m / (1.0 - ADAM_B1 ** ADAM_STEP)
    v_hat = v / (1.0 - ADAM_B2 ** ADAM_STEP)
    delta = -ADAM_LR * (m_hat / (jnp.sqrt(v_hat) + ADAM_EPS) + ADAM_WD * w)
    return delta, m, v


def _sum_adamw(parts, w, m, v, name, after=None):
    R, C = w.shape
    n_parts = len(parts)
    cg = C // n_parts
    tr = max(t for t in range(8, 257, 8) if R % t == 0)
    deps = [] if after is None else [after]

    def body(*refs):
        p_refs = refs[:n_parts]
        w_ref, m_ref, v_ref = refs[n_parts:n_parts + 3]
        g_ref, d_ref, mo_ref, vo_ref = refs[n_parts + 3 + len(deps):]
        for k, p_ref in enumerate(p_refs):
            @pl.when(pl.program_id(0) == k)
            def _():
                g = p_ref[0].astype(F32)
                for d in range(1, NDEV):
                    g = g + p_ref[d].astype(F32)
                g_ref[...] = g
                d_ref[...], mo_ref[...], vo_ref[...] = _adamw_math(w_ref[...], g, m_ref[...], v_ref[...])

    part = pl.BlockSpec((NDEV, tr, cg), lambda k, t: (0, t, 0))
    blk = pl.BlockSpec((tr, cg), lambda k, t: (t, k))
    return pl.pallas_call(
        body, grid=(n_parts, R // tr), name=name, in_specs=[part] * n_parts + [blk, blk, blk] + [ANY] * len(deps),
        out_specs=[blk] * 4, out_shape=[jax.ShapeDtypeStruct((R, C), F32)] * 4, compiler_params=_params("parallel", "parallel"),
    )(*parts, w, m, v, *deps)


def _adamw_small(ws, gs, ms, vs):
    n = len(ws)

    def body(*refs):
        w_refs, g_refs, m_refs, v_refs = (refs[k * n:(k + 1) * n] for k in range(4))
        d_refs, mo_refs, vo_refs = (refs[(4 + k) * n:(5 + k) * n] for k in range(3))
        for i in range(n):
            d_refs[i][...], mo_refs[i][...], vo_refs[i][...] = _adamw_math(w_refs[i][...], g_refs[i][...], m_refs[i][...], v_refs[i][...])

    shapes = [jax.ShapeDtypeStruct(a.shape, F32) for a in ws]
    outs = pl.pallas_call(body, out_shape=shapes * 3, name="adamw_small", compiler_params=_params())(*ws, *gs, *ms, *vs)
    return outs[:n], outs[n:2 * n], outs[2 * n:]


def kernel(x, mem, ffn1_norm, ffn1_w_up, ffn1_w_down, mix_norm, mem_norm, w_in, b_gate, conv_dw_w, conv_dw_b, conv_ln_g, conv_ln_b, conv_w_pw, att_rel_bias, att_w_o, mem_w_kv, mem_w_o, w_out, ffn2_norm, ffn2_w_up, ffn2_w_down, final_norm, loss_target, m_ffn1_norm, m_ffn1_w_up, m_ffn1_w_down, m_mix_norm, m_mem_norm, m_w_in, m_b_gate, m_conv_dw_w, m_conv_dw_b, m_conv_ln_g, m_conv_ln_b, m_conv_w_pw, m_att_rel_bias, m_att_w_o, m_mem_w_kv, m_mem_w_o, m_w_out, m_ffn2_norm, m_ffn2_w_up, m_ffn2_w_down, m_final_norm, v_ffn1_norm, v_ffn1_w_up, v_ffn1_w_down, v_mix_norm, v_mem_norm, v_w_in, v_b_gate, v_conv_dw_w, v_conv_dw_b, v_conv_ln_g, v_conv_ln_b, v_conv_w_pw, v_att_rel_bias, v_att_w_o, v_mem_w_kv, v_mem_w_o, v_w_out, v_ffn2_norm, v_ffn2_w_up, v_ffn2_w_down, v_final_norm):
    given = dict(locals())
    w = {n: given[n] for n in WEIGHTS}
    mom = {n: given["m_" + n] for n in WEIGHTS}
    var = {n: given["v_" + n] for n in WEIGHTS}

    NB, S, _ = x.shape
    T = NB * S
    ML = mem.shape[1]
    x0 = x.reshape(T, D)
    target = loss_target.reshape(T, D)
    mem2 = mem.reshape(NB * ML, D)

    def block(t, n):
        return jnp.transpose(t[0]) if n in TRANSPOSED else t[0]

    sh = dict(zip(BIG_ORDER, _cast_shards([block(w[n], n) for n in BIG_ORDER])))
    dw_t = jnp.transpose(conv_dw_w[0])

    def gather(names, extra=(), extra_kinds=()):
        return _gather_ride([sh[n] for n in names] + list(extra), [BIG[n] for n in names] + list(extra_kinds))

    W = {}
    names0 = ['ffn1_w_up', 'ffn1_w_down']
    tab, got = _bias_table(att_rel_bias[0], ride=gather(names0, [dw_t], [('row', dw_t.shape[0])]))
    W.update(zip(names0, got[:2]))
    dw_full = jnp.transpose(got[2])
    conv_vec = jnp.concatenate([conv_dw_b, conv_ln_g, conv_ln_b, jnp.zeros((5, CONV_W), F32)], axis=0)
    fin_g = final_norm.reshape(1, D)

    names1 = ['w_in', 'conv_w_pw', 'att_w_o', 'mem_w_kv', 'mem_w_o', 'w_out']
    (x1, ab1), got = _ffn_fwd(x0, ffn1_norm, W['ffn1_w_up'], W['ffn1_w_down'], TILE_FFN_FWD, "ffn1_fwd", ride=gather(names1))
    W.update(zip(names1, got))
    (uc, qkv, mq, gl, hmix), _ = _mix_fwd(x1, mix_norm, W['w_in'], TILE_TOKENS)
    uc3 = uc.reshape(NB, S, 2 * CONV_W)
    qkv3 = qkv.reshape(NB, S, 3 * ATT_W)
    mq3 = mq.reshape(NB, S, MEM_W)
    cact, conv_z = _conv_fwd(uc3, dw_full, conv_vec)
    cact = cact.reshape(T, CONV_W)
    names2 = ['ffn2_w_up', 'ffn2_w_down']
    oatt3, att_lse, got = _att_fwd(qkv3, tab, ride=gather(names2))
    W.update(zip(names2, got))
    oatt = oatt3.reshape(T, ATT_W)
    memh, kv = _memkv_fwd(mem2, mem_norm, W['mem_w_kv'], TILE_TOKENS)
    kv3 = kv.reshape(NB, ML, 2 * MEM_W)
    omem = _mematt_fwd(mq3, kv3, TILE_TOKENS).reshape(T, MEM_W)
    branch_w = (W['conv_w_pw'], W['att_w_o'], W['mem_w_o'], W['w_out'])
    x2, ymix = _combine_fwd(x1, cact, oatt, omem, gl, b_gate, *branch_w, TILE_TOKENS)
    dx3, ab2, loss_part, dg_final = _ffn_fwd_loss(x2, ffn2_norm, W['ffn2_w_up'], W['ffn2_w_down'], fin_g, target, TILE_FFN_FWD,
                                                  "ffn2_fwd_loss")

    def scatter(grads, names):
        return _scatter_ride(grads, [BIG[n] for n in names])

    G, P = {}, {}
    dx2, dab2, act2, h2, dg_ffn2 = _ffn_bwd(x2, dx3, ab2, ffn2_norm, W['ffn2_w_up'], W['ffn2_w_down'], TILE_FFN, "ffn2_bwd")
    g_up, _ = _tn_matmul(dab2, h2, 512, "grad_ffn2_w_up_a", tt=TILE_GRAD_TOKENS_WIDE, x_part=(0, 2), out_rows=2 * FF)
    G['ffn2_w_up'], _ = _tn_matmul(dab2, h2, 512, "grad_ffn2_w_up_b", tt=TILE_GRAD_TOKENS_WIDE, x_part=(1, 2), out_rows=2 * FF,
                                   prev=g_up)
    G['ffn2_w_down'], _ = _tn_matmul(act2, dx3, 512, "grad_ffn2_w_down", scale=0.5, tt=TILE_GRAD_TOKENS_WIDE)
    (dgl, dcact, doatt, domem, dyc, dya, dym, dbg), got = _combine_bwd(
        dx2, cact, oatt, omem, gl, b_gate, *branch_w, TILE_COMBINE, ride=scatter([G['ffn2_w_up']], ['ffn2_w_up']))
    P['ffn2_w_up'] = got
    G['w_out'], _ = _tn_matmul(ymix, dx2, D, "grad_w_out", tt=TILE_GRAD_TOKENS_WIDE)
    G['conv_w_pw'], _ = _tn_matmul(cact, dyc, D, "grad_conv_w_pw")
    G['att_w_o'], _ = _tn_matmul(oatt, dya, D, "grad_att_w_o")
    G['mem_w_o'], _ = _tn_matmul(omem, dym, D, "grad_mem_w_o")
    dmq3, dkv3 = _mematt_bwd(mq3, kv3, domem.reshape(NB, S, MEM_W), TILE_TOKENS)
    dkv = dkv3.reshape(NB * ML, 2 * MEM_W)
    dg_mem = _memkv_bwd(mem2, dkv, W['mem_w_kv'], TILE_TOKENS)
    G['mem_w_kv'], _ = _tn_matmul(memh, dkv, 512, "grad_mem_w_kv")
    names = ['ffn2_w_down', 'w_out', 'conv_w_pw', 'att_w_o', 'mem_w_o']
    (dqkv3, dscore), got = _att_bwd(qkv3, oatt3, att_lse, doatt.reshape(NB, S, ATT_W), tab,
                                    ride=scatter([G[n] for n in names], names))
    P.update((n, [p]) for n, p in zip(names, got))
    d_rel = _rel_bias_grad(dscore)
    (duc3, d_dw, d_cvec), got = _conv_bwd(uc3, conv_z, dcact.reshape(NB, S, CONV_W), dw_full, conv_vec,
                                          ride=scatter([G['mem_w_kv']], ['mem_w_kv']))
    P['mem_w_kv'] = got
    duc, dqkv, dmq = duc3.reshape(T, 2 * CONV_W), dqkv3.reshape(T, 3 * ATT_W), dmq3.reshape(T, MEM_W)
    g_in, _ = _tn_matmul(hmix, duc, 1024, "grad_w_in_conv", out_cols=IN_COLS, col_off=0)
    g_in, _ = _tn_matmul(hmix, dqkv, 512, "grad_w_in_qkv", out_cols=IN_COLS, col_off=1024, prev=g_in)
    g_in, _ = _tn_matmul(hmix, dmq, 512, "grad_w_in_mq", out_cols=IN_COLS, col_off=2560, prev=g_in)
    G['w_in'], _ = _tn_matmul(hmix, dgl, 1024, "grad_w_in_gate", out_cols=IN_COLS, col_off=3072, prev=g_in)
    def start_scatter(g, name, tag):
        kind = BIG[name]
        return _scatter_start(g, _own_block(g, *kind, tag), *kind, tag) + (kind, tag)

    def wait_scatter(started, after):
        send_sems, recv_sems, g_thru, land_thru, _, kind, tag = started
        return _scatter_wait(send_sems, recv_sems, g_thru, land_thru, after, *kind, tag)

    ex_in = start_scatter(G['w_in'], 'w_in', "w_in")
    (dx1, dg_mix), _ = _mix_bwd(x1, dx2, duc, dqkv, dmq, dgl, mix_norm, W['w_in'], TILE_TOKENS, after=ex_in[4])
    dx0, dab1, act1, h1, dg_ffn1 = _ffn_bwd(x0, dx1, ab1, ffn1_norm, W['ffn1_w_up'], W['ffn1_w_down'], TILE_FFN, "ffn1_bwd")
    g_wd1, _ = _tn_matmul(act1, dx1, 512, "grad_ffn1_w_down", scale=0.5, tt=TILE_GRAD_TOKENS_WIDE)
    ex_wd = start_scatter(g_wd1, 'ffn1_w_down', "ffn1_w_down")
    g_wu1a, _ = _tn_matmul(dab1, h1, 512, "grad_ffn1_w_up_a", tt=TILE_GRAD_TOKENS_WIDEST, y_part=(0, 2), after=ex_wd[4])
    ex_a = start_scatter(g_wu1a, 'ffn1_w_up', "ffn1_w_up_a")
    g_wu1b, _ = _tn_matmul(dab1, h1, 512, "grad_ffn1_w_up_b", tt=TILE_GRAD_TOKENS_WIDEST, y_part=(1, 2), after=ex_a[4])
    ex_b = start_scatter(g_wu1b, 'ffn1_w_up', "ffn1_w_up_b")
    token = ex_b[4]

    small_names = ['loss', 'ffn1_norm', 'mix_norm', 'mem_norm', 'b_gate', 'conv_dw_w', 'conv_vec', 'att_rel_bias', 'ffn2_norm',
                   'final_norm']
    small = dict(zip(small_names, _all_sum_small(
        [loss_part + token[0:1], dg_ffn1, dg_mix, dg_mem, dbg, d_dw, d_cvec, d_rel, dg_ffn2, dg_final])))
    loss = small['loss'][0, 0]
    me = _dev_index(*_my_coords())
    for i, n in enumerate(['conv_dw_b', 'conv_ln_g', 'conv_ln_b']):
        small[n] = small['conv_vec'][i:i + 1]
    small['conv_dw_w'] = lax.dynamic_slice(small['conv_dw_w'], (0, me * conv_dw_w.shape[2]), (CONV_K, conv_dw_w.shape[2]))
    little = [n for n in WEIGHTS if n not in BIG]
    as2d = lambda t, n: t.reshape(small[n].shape)
    d_s, m_s, v_s = _adamw_small([as2d(w[n], n) for n in little], [small[n] for n in little],
                                 [as2d(mom[n], n) for n in little], [as2d(var[n], n) for n in little])
    grad, delta, new_m, new_v = {}, {}, {}, {}
    for i, n in enumerate(little):
        grad[n], delta[n], new_m[n], new_v[n] = (t.reshape(w[n].shape) for t in (small[n], d_s[i], m_s[i], v_s[i]))
    done = [d_s[0]]
    waited = {'w_in': [ex_in], 'ffn1_w_down': [ex_wd], 'ffn1_w_up': [ex_a, ex_b]}
    order = [n for n in BIG_ORDER if n not in waited] + list(waited)
    for n in order:
        if n in waited:
            P[n] = [wait_scatter(ex, done) for ex in waited[n]]
        outs = _sum_adamw(P[n], block(w[n], n), block(mom[n], n), block(var[n], n), "adamw_" + n,
                          after=None if n in waited else token)
        done.append(outs[0])
        grad[n], delta[n], new_m[n], new_v[n] = ((jnp.transpose(t) if n in TRANSPOSED else t)[None] for t in outs)

    return (loss, dx0.reshape(NB, S, D), *[grad[n] for n in WEIGHTS], *[delta[n] for n in WEIGHTS],
            *[new_m[n] for n in WEIGHTS], *[new_v[n] for n in WEIGHTS])
```

```python
import functools

import jax
import jax.numpy as jnp
from jax import lax
from jax.experimental import pallas as pl
from jax.experimental.pallas import tpu as pltpu

F32 = jnp.float32
BF16 = jnp.bfloat16

EPS = 1e-6
MASK_VALUE = -1e30
D = 1024
NDEV = 8
FF = 2816
FF_SHARD = 704
FF_HALF_ROWS = 352
FF_BLOCK_EDGES = ()
IN_COLS = 6144
CONV_W = 512
CONV_K = 31
CONV_HALO = 32
CONV_CHUNK = 32
CONV_WIN = CONV_CHUNK + 40
GLU_CHUNK = 128
ATT_W = 512
ATT_HEADS = 8
ATT_HD = 64
CHUNK = 64
LEFT_CHUNKS = 8
MAX_REL = 128
N_REL = 192
QB = 256
KWIN = QB + LEFT_CHUNKS * CHUNK
KPAD = LEFT_CHUNKS * CHUNK
DS_LANES = 1024
MEM_W = 512
MEM_HEADS = 4
MEM_HD = 128
ADAM_LR = 0.001
ADAM_B1 = 0.9
ADAM_B2 = 0.999
ADAM_EPS = 1e-08
ADAM_WD = 0.01
ADAM_STEP = 10
VMEM_LIMIT = 60 * 1024 * 1024
TILE_FFN = 256
TILE_FFN_FWD = 512
TILE_COMBINE = 512
TILE_TOKENS = 512
TILE_GRAD_TOKENS = 2048
TILE_GRAD_TOKENS_WIDE = 1024
TILE_GRAD_TOKENS_WIDEST = 512

MESH = pl.DeviceIdType.MESH
ANY = pl.BlockSpec(memory_space=pl.ANY)

WEIGHTS = ['ffn1_norm', 'ffn1_w_up', 'ffn1_w_down', 'mix_norm', 'mem_norm', 'w_in', 'b_gate', 'conv_dw_w', 'conv_dw_b',
           'conv_ln_g', 'conv_ln_b', 'conv_w_pw', 'att_rel_bias', 'att_w_o', 'mem_w_kv', 'mem_w_o', 'w_out', 'ffn2_norm',
           'ffn2_w_up', 'ffn2_w_down', 'final_norm']
BIG = {
    'ffn1_w_up': ('row', FF_SHARD), 'ffn1_w_down': ('row', FF_HALF_ROWS), 'w_in': ('col', 768),
    'conv_w_pw': ('col', 128), 'att_w_o': ('col', 128), 'mem_w_kv': ('row', 128), 'mem_w_o': ('col', 128),
    'w_out': ('row', 128), 'ffn2_w_up': ('row', FF_SHARD), 'ffn2_w_down': ('row', FF_HALF_ROWS),
}
BIG_ORDER = ['ffn1_w_up', 'ffn1_w_down', 'w_in', 'conv_w_pw', 'att_w_o', 'mem_w_kv', 'mem_w_o', 'w_out', 'ffn2_w_up', 'ffn2_w_down']
TRANSPOSED = ('ffn1_w_up', 'ffn2_w_up')


def _dot(a, b):
    return jnp.dot(a, b, preferred_element_type=F32)


def _dot_nt(a, b):
    return lax.dot_general(a, b, (((1,), (1,)), ((), ())), preferred_element_type=F32)


def _dot_tn(a, b):
    return lax.dot_general(a, b, (((0,), (0,)), ((), ())), preferred_element_type=F32)


def _sigmoid(v):
    return jax.nn.sigmoid(v)


def _const(shape):
    return pl.BlockSpec(shape, lambda *_: (0,) * len(shape), pipeline_mode=pl.Buffered(1))


def _params(*sem):
    return pltpu.CompilerParams(dimension_semantics=sem if sem else None, vmem_limit_bytes=VMEM_LIMIT)


def _my_coords():
    return lax.axis_index("x"), lax.axis_index("y"), lax.axis_index("c")


def _dev_index(px, py, pc):
    return 4 * px + 2 * py + pc


def _window(ref, kind, n, p):
    if kind == 'row':
        return ref.at[pl.ds(pl.multiple_of(p * n, n), n), :]
    return ref.at[:, pl.ds(pl.multiple_of(p * n, 128), n)]


def _full_shape(kind, n, shard_shape):
    if kind == 'row':
        return (NDEV * n, shard_shape[1])
    return (shard_shape[0], NDEV * n)


def _cast_shards(shards):
    n = len(shards)

    def body(*refs):
        for i in range(n):
            refs[n + i][...] = refs[i][...].astype(BF16)

    out_shape = [jax.ShapeDtypeStruct(s.shape, BF16) for s in shards]
    return pl.pallas_call(body, out_shape=out_shape, name="cast_shards", compiler_params=_params())(*shards)


class _Ride:
    def __init__(self, inputs, out_shape, scratch, start, finish, mids=()):
        self.inputs, self.out_shape, self.scratch = list(inputs), list(out_shape), list(scratch)
        self.start, self.finish, self.mids = start, finish, tuple(mids)


def _pallas(body, name, grid, in_specs, out_specs, out_shape, args, scratch_shapes=(), sem=None, aliases=None, ride=None,
            after=None):
    if ride is None:
        n_in, n_dep = len(args), 0 if after is None else 1

        def kernel_body(*refs):
            body(*refs[:n_in], *refs[n_in + n_dep:])

        outs = pl.pallas_call(kernel_body if n_dep else body, grid=grid, name=name, in_specs=list(in_specs) + [ANY] * n_dep,
                              out_specs=out_specs, out_shape=out_shape, scratch_shapes=list(scratch_shapes),
                              input_output_aliases=aliases or {}, compiler_params=_params(*sem),
                              )(*args, *([after] if n_dep else []))
        return list(outs), []
    n_in, n_out, n_scr = len(args), len(out_shape), len(scratch_shapes)
    r_in, r_out = len(ride.inputs), len(ride.out_shape)

    def wrapped(*refs):
        k_in, rin = refs[:n_in], refs[n_in:n_in + r_in]
        o0 = n_in + r_in
        k_out, rout = refs[o0:o0 + n_out], refs[o0 + n_out:o0 + n_out + r_out]
        s0 = o0 + n_out + r_out
        k_scr, rscr = refs[s0:s0 + n_scr], refs[s0 + n_scr:]
        ids = [pl.program_id(k) for k in range(len(grid))]
        first = functools.reduce(jnp.logical_and, [i == 0 for i in ids])
        last = functools.reduce(jnp.logical_and, [i == g - 1 for i, g in zip(ids, grid)])
        pl.when(first)(lambda: ride.start(rin, rout, rscr))
        single_step = all(g == 1 for g in grid)
        for quarter, mid in ride.mids:
            if not single_step:
                at_mid = functools.reduce(jnp.logical_and, [ids[0] == (quarter * grid[0]) // 4] + [i == 0 for i in ids[1:]])
                pl.when(at_mid)(functools.partial(mid, rin, rout, rscr))
        body(*k_in, *k_out, *k_scr)
        for _, mid in ride.mids:
            if single_step:
                mid(rin, rout, rscr)
        pl.when(last)(lambda: ride.finish(rin, rout, rscr))

    outs = pl.pallas_call(
        wrapped, grid=grid, name=name, in_specs=list(in_specs) + [ANY] * r_in, out_specs=list(out_specs) + [ANY] * r_out,
        out_shape=list(out_shape) + ride.out_shape, scratch_shapes=list(scratch_shapes) + ride.scratch,
        input_output_aliases=aliases or {}, compiler_params=_params(*(["arbitrary"] * len(grid))),
    )(*args, *ride.inputs)
    return list(outs[:n_out]), list(outs[n_out:])


def _gather_ride(shards, kinds):
    n = len(shards)

    def plan(rin, out, sems):
        send_sems, recv_sems, local_sems = sems[:3]
        x, y, c = _my_coords()
        me, sibling = (x, y, c), (x, y, 1 - c)
        xn, yn, diag = (1 - x, y), (x, 1 - y), (1 - x, 1 - y)

        def win(i, dev):
            return _window(out[i], kinds[i][0], kinds[i][1], _dev_index(*dev))

        def copy(i, k, block, to, from_shard=False):
            return pltpu.make_async_remote_copy(
                src_ref=rin[i] if from_shard else win(i, block), dst_ref=win(i, block),
                send_sem=send_sems.at[i, k], recv_sem=recv_sems.at[i, k], device_id=to, device_id_type=MESH)

        def each(fn):
            return [fn(i) for i in range(n)]

        return dict(
            local=lambda: each(lambda i: pltpu.make_async_copy(rin[i], win(i, me), local_sems.at[i])),
            own=lambda: [cp for i in range(n) for cp in (copy(i, 0, me, sibling, True), copy(i, 1, me, (*xn, c), True),
                                                         copy(i, 2, me, (*yn, c), True))],
            from_x=lambda: each(lambda i: copy(i, 1, (*xn, c), me)),
            from_y=lambda: each(lambda i: copy(i, 2, (*yn, c), me)),
            x_block_on_to_y=lambda: each(lambda i: copy(i, 3, (*xn, c), (*yn, c))),
            y_block_on_to_x=lambda: each(lambda i: copy(i, 3, (*yn, c), (*xn, c))),
            from_diag=lambda: each(lambda i: copy(i, 3, (*diag, c), me)),
            to_sibling=lambda j: each(lambda i: copy(i, 4 + j, (*(xn, yn, diag)[j], c), sibling)),
            from_sibling=lambda: [cp for i in range(n) for cp in
                                  [copy(i, 0, sibling, me)] + [copy(i, 4 + j, (*chip, 1 - c), me) for j, chip in enumerate((xn, yn, diag))]],
            north=c == 1)

    def start(rin, out, sems):
        p = plan(rin, out, sems)
        for cp in p['local']() + p['own']():
            cp.start()

    def pass_diagonal(rin, out, sems):
        p = plan(rin, out, sems)

        @pl.when(p['north'])
        def _():
            for got, fwd, sib in zip(p['from_x'](), p['x_block_on_to_y'](), p['to_sibling'](0)):
                got.wait_recv()
                fwd.start()
                sib.start()

        @pl.when(jnp.logical_not(p['north']))
        def _():
            for got, fwd, sib in zip(p['from_y'](), p['y_block_on_to_x'](), p['to_sibling'](1)):
                got.wait_recv()
                fwd.start()
                sib.start()

    def pass_to_sibling(rin, out, sems):
        p = plan(rin, out, sems)

        @pl.when(p['north'])
        def _():
            for got, sib in zip(p['from_y'](), p['to_sibling'](1)):
                got.wait_recv()
                sib.start()

        @pl.when(jnp.logical_not(p['north']))
        def _():
            for got, sib in zip(p['from_x'](), p['to_sibling'](0)):
                got.wait_recv()
                sib.start()
        for got, sib in zip(p['from_diag'](), p['to_sibling'](2)):
            got.wait_recv()
            sib.start()

    def finish(rin, out, sems):
        p = plan(rin, out, sems)
        for cp in p['from_sibling']():
            cp.wait_recv()
        for cp in p['own']() + p['to_sibling'](0) + p['to_sibling'](1) + p['to_sibling'](2):
            cp.wait_send()

        @pl.when(p['north'])
        def _():
            for cp in p['x_block_on_to_y']():
                cp.wait_send()

        @pl.when(jnp.logical_not(p['north']))
        def _():
            for cp in p['y_block_on_to_x']():
                cp.wait_send()
        for cp in p['local']():
            cp.wait()

    out_shape = [jax.ShapeDtypeStruct(_full_shape(k, m, s.shape), s.dtype) for s, (k, m) in zip(shards, kinds)]
    scratch = [pltpu.SemaphoreType.DMA((n, 7)), pltpu.SemaphoreType.DMA((n, 7)), pltpu.SemaphoreType.DMA((n,))]
    return _Ride(shards, out_shape, scratch, start, finish, mids=((2, pass_diagonal), (3, pass_to_sibling)))


def _scatter_ride(grads, kinds):
    n = len(grads)

    def plan(g, out, sems):
        send_sems, recv_sems, local_sems = sems
        x, y, c = _my_coords()
        me = _dev_index(x, y, c)

        def local():
            return [pltpu.make_async_copy(_window(g[i], kinds[i][0], kinds[i][1], me), out[i].at[me], local_sems.at[i])
                    for i in range(n)]

        def remote(arrival):
            cps = []
            for rel in range(1, NDEV):
                peer = _peer(x, y, c, rel)
                dev = _dev_index(*peer)
                for i in range(n):
                    kind, m = kinds[i]
                    cps.append(pltpu.make_async_remote_copy(
                        src_ref=_window(g[i], kind, m, me if arrival else dev), dst_ref=out[i].at[dev if arrival else me],
                        send_sem=send_sems.at[i, rel - 1], recv_sem=recv_sems.at[i, rel - 1], device_id=peer, device_id_type=MESH))
            return cps

        return local, remote

    def start(g, out, sems):
        local, remote = plan(g, out, sems)
        for cp in local() + remote(False):
            cp.start()

    def finish(g, out, sems):
        local, remote = plan(g, out, sems)
        for cp in remote(True):
            cp.wait_recv()
        for cp in remote(False):
            cp.wait_send()
        for cp in local():
            cp.wait()

    def block_shape(gr, kind, m):
        return (m, gr.shape[1]) if kind == 'row' else (gr.shape[0], m)

    out_shape = [jax.ShapeDtypeStruct((NDEV,) + block_shape(gr, k, m), gr.dtype) for gr, (k, m) in zip(grads, kinds)]
    scratch = [pltpu.SemaphoreType.DMA((n, NDEV - 1)), pltpu.SemaphoreType.DMA((n, NDEV - 1)), pltpu.SemaphoreType.DMA((n,))]
    return _Ride(grads, out_shape, scratch, start, finish)


def _rms_stats(xf):
    r = lax.rsqrt(jnp.mean(xf * xf, axis=-1, keepdims=True) + EPS)
    return xf * r, r


def _rms_bwd(dh, g, xhat, r):
    dxhat = dh * g
    return r * (dxhat - xhat * jnp.mean(dxhat * xhat, axis=-1, keepdims=True))


def _ffn_blocks():
    edges = (0,) + FF_BLOCK_EDGES + (FF,)
    return [(slice(lo, hi), slice(FF + lo, FF + hi)) for lo, hi in zip(edges[:-1], edges[1:])]


def _swiglu_tile(x_ref, g_ref, wut_ref, wd_ref, ab_ref):
    xf = x_ref[...]
    xhat, _ = _rms_stats(xf)
    h = (xhat * g_ref[...]).astype(BF16)
    acc = jnp.zeros(xf.shape, F32)
    for ra, rb in _ffn_blocks():
        a = _dot_nt(h, wut_ref[ra, :])
        b = _dot_nt(h, wut_ref[rb, :])
        ab_ref[:, ra] = a.astype(BF16)
        ab_ref[:, rb] = b.astype(BF16)
        act = (a * _sigmoid(a) * b).astype(BF16)
        acc = acc + _dot(act, wd_ref[ra, :])
    return xf + 0.5 * acc


def _ffn_fwd(x, g, wut, wd, tm, name, ride=None):
    T = x.shape[0]

    def body(x_ref, g_ref, wut_ref, wd_ref, xo_ref, ab_ref):
        xo_ref[...] = _swiglu_tile(x_ref, g_ref, wut_ref, wd_ref, ab_ref)

    return _pallas(
        body, name, (T // tm,),
        [pl.BlockSpec((tm, D), lambda t: (t, 0)), _const((1, D)), _const((2 * FF, D)), _const((FF, D))],
        [pl.BlockSpec((tm, D), lambda t: (t, 0)), pl.BlockSpec((tm, 2 * FF), lambda t: (t, 0))],
        [jax.ShapeDtypeStruct((T, D), F32), jax.ShapeDtypeStruct((T, 2 * FF), BF16)],
        (x, g, wut, wd), sem=("arbitrary",), ride=ride)


def _ffn_fwd_loss(x, g, wut, wd, g_final, target, tm, name):
    T = x.shape[0]

    def body(x_ref, g_ref, wut_ref, wd_ref, gf_ref, t_ref, dx_ref, ab_ref, loss_ref, dgf_ref):
        xhat, r = _rms_stats(_swiglu_tile(x_ref, g_ref, wut_ref, wd_ref, ab_ref))
        gain = gf_ref[...]
        diff = xhat * gain - t_ref[...]
        dout = diff * (1.0 / D)

        @pl.when(pl.program_id(0) == 0)
        def _():
            loss_ref[...] = jnp.zeros_like(loss_ref)
            dgf_ref[...] = jnp.zeros_like(dgf_ref)
        sq = jnp.sum(jnp.sum(diff * diff, axis=0, keepdims=True), axis=1, keepdims=True)
        loss_ref[...] += jnp.broadcast_to(sq * (0.5 / D), (1, 128))
        dgf_ref[...] += jnp.sum(dout * xhat, axis=0, keepdims=True)
        dx_ref[...] = _rms_bwd(dout, gain, xhat, r)

    row = pl.BlockSpec((tm, D), lambda t: (t, 0))
    return pl.pallas_call(
        body, grid=(T // tm,), name=name,
        in_specs=[row, _const((1, D)), _const((2 * FF, D)), _const((FF, D)), _const((1, D)), row],
        out_specs=[row, pl.BlockSpec((tm, 2 * FF), lambda t: (t, 0)), pl.BlockSpec((1, 128), lambda t: (0, 0)),
                   pl.BlockSpec((1, D), lambda t: (0, 0))],
        out_shape=[jax.ShapeDtypeStruct((T, D), F32), jax.ShapeDtypeStruct((T, 2 * FF), BF16),
                   jax.ShapeDtypeStruct((1, 128), F32), jax.ShapeDtypeStruct((1, D), F32)],
        compiler_params=_params("arbitrary"),
    )(x, g, wut, wd, g_final, target)


def _ffn_bwd(x, dy, ab, g, wut, wd, tm, name):
    T = x.shape[0]

    def body(x_ref, dy_ref, ab_ref, g_ref, wut_ref, wd_ref, dx_ref, dab_ref, act_ref, h_ref, dg_ref):
        xf = x_ref[...]
        xhat, r = _rms_stats(xf)
        gain = g_ref[...]
        h_ref[...] = (xhat * gain).astype(BF16)
        dy = dy_ref[...]
        dyh = (0.5 * dy).astype(BF16)
        dh = jnp.zeros((tm, D), F32)
        for ra, rb in _ffn_blocks():
            a = ab_ref[:, ra].astype(F32)
            b = ab_ref[:, rb].astype(F32)
            dact = _dot_nt(dyh, wd_ref[ra, :])
            sg = _sigmoid(a)
            sl = a * sg
            act_ref[:, ra] = (sl * b).astype(BF16)
            da = (dact * b * (sg * (1.0 + a * (1.0 - sg)))).astype(BF16)
            db = (dact * sl).astype(BF16)
            dab_ref[:, ra] = da
            dab_ref[:, rb] = db
            dh = dh + _dot(da, wut_ref[ra, :]) + _dot(db, wut_ref[rb, :])
        dx_ref[...] = dy + _rms_bwd(dh, gain, xhat, r)

        @pl.when(pl.program_id(0) == 0)
        def _():
            dg_ref[...] = jnp.zeros_like(dg_ref)
        dg_ref[...] += jnp.sum(dh * xhat, axis=0, keepdims=True)

    return pl.pallas_call(
        body, grid=(T // tm,), name=name,
        in_specs=[pl.BlockSpec((tm, D), lambda t: (t, 0)), pl.BlockSpec((tm, D), lambda t: (t, 0)),
                  pl.BlockSpec((tm, 2 * FF), lambda t: (t, 0)), _const((1, D)), _const((2 * FF, D)), _const((FF, D))],
        out_specs=[pl.BlockSpec((tm, D), lambda t: (t, 0)), pl.BlockSpec((tm, 2 * FF), lambda t: (t, 0)),
                   pl.BlockSpec((tm, FF), lambda t: (t, 0)), pl.BlockSpec((tm, D), lambda t: (t, 0)),
                   pl.BlockSpec((1, D), lambda t: (0, 0))],
        out_shape=[jax.ShapeDtypeStruct((T, D), F32), jax.ShapeDtypeStruct((T, 2 * FF), BF16),
                   jax.ShapeDtypeStruct((T, FF), BF16), jax.ShapeDtypeStruct((T, D), BF16), jax.ShapeDtypeStruct((1, D), F32)],
        compiler_params=_params("arbitrary"),
    )(x, dy, ab, g, wut, wd)


def _tn_matmul(xm, ym, tn, name, scale=None, out_cols=None, col_off=0, prev=None, tt=TILE_GRAD_TOKENS, x_part=(0, 1),
               out_rows=None, y_part=(0, 1), ride=None, after=None):
    T = xm.shape[0]
    xi, xn = x_part
    yi, yn = y_part
    K = xm.shape[1] // xn
    N = ym.shape[1] // yn
    out_cols = N if out_cols is None else out_cols
    row_blk = xi if out_rows is not None else 0
    out_rows = K if out_rows is None else out_rows
    tt = min(tt, T)
    nt = T // tt
    off = col_off // tn

    def body(*refs):
        x_ref, y_ref = refs[0], refs[1]
        o_ref, acc = refs[-2], refs[-1]

        @pl.when(pl.program_id(1) == 0)
        def _():
            acc[...] = jnp.zeros_like(acc)
        acc[...] += _dot_tn(x_ref[...].astype(BF16), y_ref[...].astype(BF16))

        @pl.when(pl.program_id(1) == nt - 1)
        def _():
            res = acc[...]
            o_ref[...] = (res if scale is None else res * scale).astype(BF16)

    ycol = yi * (N // tn)
    in_specs = [pl.BlockSpec((tt, K), lambda n, t: (t, xi)), pl.BlockSpec((tt, tn), lambda n, t: (t, n + ycol))]
    args = [xm, ym]
    aliases = {}
    if prev is not None:
        in_specs.append(ANY)
        args.append(prev)
        aliases = {2: 0}
    outs, rode = _pallas(
        body, name, (N // tn, nt), in_specs, [pl.BlockSpec((K, tn), lambda n, t: (row_blk, n + off))],
        [jax.ShapeDtypeStruct((out_rows, out_cols), BF16)], args, scratch_shapes=[pltpu.VMEM((K, tn), F32)],
        sem=("parallel", "arbitrary"), aliases=aliases, ride=ride, after=after)
    return outs[0], rode


def _mix_fwd(x, g, w_in, tm, ride=None):
    T = x.shape[0]

    def body(x_ref, g_ref, w_ref, uc_ref, qkv_ref, mq_ref, gl_ref, h_ref):
        xhat, _ = _rms_stats(x_ref[...])
        h = (xhat * g_ref[...]).astype(BF16)
        h_ref[...] = h
        uc_ref[...] = _dot(h, w_ref[:, 0:1024])
        qkv_ref[...] = _dot(h, w_ref[:, 1024:2560]).astype(BF16)
        mq_ref[...] = _dot(h, w_ref[:, 2560:3072]).astype(BF16)
        for j in range(3):
            gl_ref[:, j * D:(j + 1) * D] = _dot(h, w_ref[:, 3072 + j * D:3072 + (j + 1) * D]).astype(BF16)

    row = lambda w: pl.BlockSpec((tm, w), lambda t: (t, 0))
    return _pallas(
        body, "mix_fwd", (T // tm,), [row(D), _const((1, D)), _const((D, IN_COLS))],
        [row(1024), row(1536), row(512), row(3072), row(D)],
        [jax.ShapeDtypeStruct((T, 1024), F32), jax.ShapeDtypeStruct((T, 1536), BF16), jax.ShapeDtypeStruct((T, 512), BF16),
         jax.ShapeDtypeStruct((T, 3072), BF16), jax.ShapeDtypeStruct((T, D), BF16)],
        (x, g, w_in), sem=("parallel",), ride=ride)


def _mix_bwd(x, dres, duc, dqkv, dmq, dgl, g, w_in, tm, ride=None, after=None):
    T = x.shape[0]

    def body(x_ref, dres_ref, duc_ref, dqkv_ref, dmq_ref, dgl_ref, g_ref, w_ref, dx_ref, dg_ref):
        xhat, r = _rms_stats(x_ref[...])
        dh = _dot_nt(duc_ref[...], w_ref[:, 0:1024])
        dh = dh + _dot_nt(dqkv_ref[...], w_ref[:, 1024:2560])
        dh = dh + _dot_nt(dmq_ref[...], w_ref[:, 2560:3072])
        dh = dh + _dot_nt(dgl_ref[...], w_ref[:, 3072:6144])
        dx_ref[...] = dres_ref[...] + _rms_bwd(dh, g_ref[...], xhat, r)

        @pl.when(pl.program_id(0) == 0)
        def _():
            dg_ref[...] = jnp.zeros_like(dg_ref)
        dg_ref[...] += jnp.sum(dh * xhat, axis=0, keepdims=True)

    row = lambda w: pl.BlockSpec((tm, w), lambda t: (t, 0))
    return _pallas(
        body, "mix_bwd", (T // tm,),
        [row(D), row(D), row(1024), row(1536), row(512), row(3072), _const((1, D)), _const((D, IN_COLS))],
        [row(D), pl.BlockSpec((1, D), lambda t: (0, 0))],
        [jax.ShapeDtypeStruct((T, D), F32), jax.ShapeDtypeStruct((1, D), F32)],
        (x, dres, duc, dqkv, dmq, dgl, g, w_in), sem=("arbitrary",), ride=ride, after=after)


def _shifted(win, base, copies):
    for k in range(8):
        copies[k] = win[base + k:base + k + CONV_CHUNK + 24]
    return copies


def _tap_slices(copies, tap):
    out = []
    for k in range(8):
        for a in range(4):
            j = tap(a, k)
            if 0 <= j < CONV_K:
                out.append((j, copies[k, pl.ds(8 * a, CONV_CHUNK), :]))
    return out


def _conv_taps(copies, w_ref, tap):
    acc = jnp.zeros((CONV_CHUNK, CONV_W), F32)
    for j, rows in _tap_slices(copies, tap):
        acc = acc + rows * w_ref[j:j + 1, :]
    return acc


def _fold8(v):
    acc = v[0:8]
    for r in range(8, CONV_CHUNK, 8):
        acc = acc + v[r:r + 8]
    return acc


def _glu_into(uc_ref, vpad, S):
    vpad[pl.ds(0, CONV_HALO), :] = jnp.zeros((CONV_HALO, CONV_W), F32)
    vpad[pl.ds(S + CONV_HALO, CONV_HALO), :] = jnp.zeros((CONV_HALO, CONV_W), F32)

    def glu(i, carry):
        r0 = pl.multiple_of(i * GLU_CHUNK, GLU_CHUNK)
        a = uc_ref[0, pl.ds(r0, GLU_CHUNK), 0:CONV_W]
        gt = uc_ref[0, pl.ds(r0, GLU_CHUNK), CONV_W:2 * CONV_W]
        vpad[pl.ds(pl.multiple_of(r0 + CONV_HALO, CONV_HALO), GLU_CHUNK), :] = a * _sigmoid(gt)
        return carry
    lax.fori_loop(0, S // GLU_CHUNK, glu, 0)


def _layer_norm(z, vec_ref):
    xc = z - jnp.mean(z, axis=-1, keepdims=True)
    rstd = lax.rsqrt(jnp.mean(xc * xc, axis=-1, keepdims=True) + EPS)
    xn = xc * rstd
    return xn, rstd, xn * vec_ref[1:2, :] + vec_ref[2:3, :]


def _conv_fwd(uc, dw_w, vec):
    NB, S, _ = uc.shape

    def body(uc_ref, w_ref, vec_ref, o_ref, z_ref, vpad, copies):
        _glu_into(uc_ref, vpad, S)

        def conv(i, carry):
            r0 = pl.multiple_of(i * CONV_CHUNK, CONV_CHUNK)
            win = vpad[pl.ds(r0, CONV_WIN), :]
            z = _conv_taps(_shifted(win, CONV_HALO - (CONV_K - 1), copies), w_ref, lambda a, k: 8 * a + k) + vec_ref[0:1, :]
            z_ref[0, pl.ds(r0, CONV_CHUNK), :] = z
            _, _, yln = _layer_norm(z, vec_ref)
            o_ref[0, pl.ds(r0, CONV_CHUNK), :] = (yln * _sigmoid(yln)).astype(BF16)
            return carry
        lax.fori_loop(0, S // CONV_CHUNK, conv, 0, unroll=4)

    seq = pl.BlockSpec((1, S, CONV_W), lambda b: (b, 0, 0))
    return pl.pallas_call(
        body, grid=(NB,), name="conv_fwd",
        in_specs=[pl.BlockSpec((1, S, 2 * CONV_W), lambda b: (b, 0, 0)), _const((CONV_K, CONV_W)), _const((8, CONV_W))],
        out_specs=[seq, seq],
        out_shape=[jax.ShapeDtypeStruct((NB, S, CONV_W), BF16), jax.ShapeDtypeStruct((NB, S, CONV_W), F32)],
        scratch_shapes=[pltpu.VMEM((S + 2 * CONV_HALO, CONV_W), F32), pltpu.VMEM((8, CONV_CHUNK + 24, CONV_W), F32)],
        compiler_params=_params("parallel"),
    )(uc, dw_w, vec)


def _conv_bwd(uc, z, dcact, dw_w, vec, ride=None):
    NB, S, _ = uc.shape
    n_chunks = S // CONV_CHUNK

    def body(uc_ref, z_ref, dc_ref, w_ref, vec_ref, duc_ref, dw_ref, dvec_ref, vpad, dzpad, dw8, dvec8, copies):
        @pl.when(pl.program_id(0) == 0)
        def _():
            dw8[...] = jnp.zeros_like(dw8)
            dvec8[...] = jnp.zeros_like(dvec8)
        _glu_into(uc_ref, vpad, S)
        dzpad[pl.ds(S, 2 * CONV_HALO), :] = jnp.zeros((2 * CONV_HALO, CONV_W), F32)

        def norm_bwd(i, carry):
            r0 = pl.multiple_of(i * CONV_CHUNK, CONV_CHUNK)
            xn, rstd, yln = _layer_norm(z_ref[0, pl.ds(r0, CONV_CHUNK), :], vec_ref)
            sg = _sigmoid(yln)
            dyln = dc_ref[0, pl.ds(r0, CONV_CHUNK), :] * (sg * (1.0 + yln * (1.0 - sg)))
            dxn = dyln * vec_ref[1:2, :]
            dz = rstd * (dxn - jnp.mean(dxn, axis=-1, keepdims=True) - xn * jnp.mean(dxn * xn, axis=-1, keepdims=True))
            dzpad[pl.ds(r0, CONV_CHUNK), :] = dz
            dvec8[0] += _fold8(dz)
            dvec8[1] += _fold8(dyln * xn)
            dvec8[2] += _fold8(dyln)
            return carry
        lax.fori_loop(0, n_chunks, norm_bwd, 0, unroll=4)

        def taps_bwd(i, carry):
            r0 = pl.multiple_of(i * CONV_CHUNK, CONV_CHUNK)
            dzwin = dzpad[pl.ds(r0, CONV_WIN), :]
            dv = _conv_taps(_shifted(dzwin, 0, copies), w_ref, lambda a, k: CONV_K - 1 - 8 * a - k)
            dz = dzwin[0:CONV_CHUNK]
            vwin = vpad[pl.ds(r0, CONV_WIN), :]
            for j, rows in _tap_slices(_shifted(vwin, CONV_HALO - (CONV_K - 1), copies), lambda a, k: 8 * a + k):
                dw8[j] += _fold8(dz * rows)
            a = uc_ref[0, pl.ds(r0, CONV_CHUNK), 0:CONV_W]
            sg = _sigmoid(uc_ref[0, pl.ds(r0, CONV_CHUNK), CONV_W:2 * CONV_W])
            duc_ref[0, pl.ds(r0, CONV_CHUNK), 0:CONV_W] = (dv * sg).astype(BF16)
            duc_ref[0, pl.ds(r0, CONV_CHUNK), CONV_W:2 * CONV_W] = (dv * a * sg * (1.0 - sg)).astype(BF16)
            return carry
        lax.fori_loop(0, n_chunks, taps_bwd, 0, unroll=2)

        @pl.when(pl.program_id(0) == NB - 1)
        def _():
            dw_ref[...] = jnp.zeros_like(dw_ref)
            dvec_ref[...] = jnp.zeros_like(dvec_ref)
            for j in range(CONV_K):
                dw_ref[j:j + 1, :] = jnp.sum(dw8[j], axis=0, keepdims=True)
            for j in range(3):
                dvec_ref[j:j + 1, :] = jnp.sum(dvec8[j], axis=0, keepdims=True)

    return _pallas(
        body, "conv_bwd", (NB,),
        [pl.BlockSpec((1, S, 2 * CONV_W), lambda b: (b, 0, 0)), pl.BlockSpec((1, S, CONV_W), lambda b: (b, 0, 0)),
         pl.BlockSpec((1, S, CONV_W), lambda b: (b, 0, 0)), _const((CONV_K, CONV_W)), _const((8, CONV_W))],
        [pl.BlockSpec((1, S, 2 * CONV_W), lambda b: (b, 0, 0)), pl.BlockSpec((32, CONV_W), lambda b: (0, 0)),
         pl.BlockSpec((8, CONV_W), lambda b: (0, 0))],
        [jax.ShapeDtypeStruct((NB, S, 2 * CONV_W), BF16), jax.ShapeDtypeStruct((32, CONV_W), F32),
         jax.ShapeDtypeStruct((8, CONV_W), F32)],
        (uc, z, dcact, dw_w, vec),
        scratch_shapes=[pltpu.VMEM((S + 2 * CONV_HALO, CONV_W), F32), pltpu.VMEM((S + 2 * CONV_HALO, CONV_W), F32),
                        pltpu.VMEM((CONV_K, 8, CONV_W), F32), pltpu.VMEM((3, 8, CONV_W), F32),
                        pltpu.VMEM((8, CONV_CHUNK + 24, CONV_W), F32)],
        sem=("arbitrary",), ride=ride)


def _rel_index_of_column(cols):
    offset = jnp.where(cols < KWIN, cols, cols - DS_LANES)
    return jnp.clip(KPAD - offset, -(CHUNK - 1), MAX_REL) + (CHUNK - 1)


def _bias_table(rel_bias, ride=None):
    def body(rb_ref, o_ref, by_offset, first8):
        ridx = _rel_index_of_column(lax.broadcasted_iota(jnp.int32, (1, DS_LANES), 1))
        onehot = (ridx == lax.broadcasted_iota(jnp.int32, (N_REL, 1), 0)).astype(F32)
        by_offset[...] = jnp.dot(rb_ref[...], onehot, preferred_element_type=F32, precision=lax.Precision.HIGHEST)
        sub = lax.broadcasted_iota(jnp.int32, (8, 1), 0)
        kchunk = lax.broadcasted_iota(jnp.int32, (1, KWIN), 1) // CHUNK
        for head in range(ATT_HEADS):
            base = jnp.broadcast_to(by_offset[head:head + 1, :], (8, DS_LANES))
            rows = base
            for s in range(1, 8):
                rows = jnp.where(sub == s, pltpu.roll(base, s, 1), rows)
            first8[head] = rows

        def rows8(q8, carry):
            qchunk = (q8 * 8 + sub) // CHUNK
            band = (kchunk >= qchunk) & (kchunk <= qchunk + LEFT_CHUNKS)
            for head in range(ATT_HEADS):
                tile = pltpu.roll(first8[head], q8 * 8, 1)[:, 0:KWIN]
                o_ref[head, pl.ds(pl.multiple_of(q8 * 8, 8), 8), :] = jnp.where(band, tile, MASK_VALUE)
            return carry
        lax.fori_loop(0, QB // 8, rows8, 0)

    outs, rode = _pallas(
        body, "bias_table", (1,), [pl.BlockSpec((ATT_HEADS, N_REL), lambda i: (0, 0))],
        [pl.BlockSpec((ATT_HEADS, QB, KWIN), lambda i: (0, 0, 0))], [jax.ShapeDtypeStruct((ATT_HEADS, QB, KWIN), F32)], (rel_bias,),
        scratch_shapes=[pltpu.VMEM((ATT_HEADS, DS_LANES), F32), pltpu.VMEM((ATT_HEADS, 8, DS_LANES), F32)],
        sem=("arbitrary",), ride=ride)
    return outs[0], rode


def _load_keys(i, k_ref, v_ref, kpad, vpad, S):
    @pl.when(i == 0)
    def _():
        kpad[pl.ds(0, KPAD), :] = jnp.zeros((KPAD, ATT_W), BF16)
        vpad[pl.ds(0, KPAD), :] = jnp.zeros((KPAD, ATT_W), BF16)
        kpad[pl.ds(KPAD, S), :] = k_ref[0]
        vpad[pl.ds(KPAD, S), :] = v_ref[0]


def _att_scores(q2s, k2, tab_ref, head, in_head, in_seq):
    qm = jnp.where(in_head, q2s, jnp.zeros_like(q2s))
    s = _dot_nt(qm, k2) + tab_ref[head]
    return s if in_seq is None else jnp.where(in_seq, s, MASK_VALUE)


def _by_window(i, step):
    in_seq = (lax.broadcasted_iota(jnp.int32, (1, KWIN), 1) + i * QB) >= KPAD
    pl.when(i < KPAD // QB)(lambda: step(in_seq))
    pl.when(i >= KPAD // QB)(lambda: step(None))


def _scaled(q2):
    return q2 * jnp.asarray(ATT_HD ** -0.5, q2.dtype)


def _att_fwd(qkv, tab, ride=None):
    NB, S, _ = qkv.shape

    def body(q_ref, k_ref, v_ref, tab_ref, o_ref, lse_ref, kpad, vpad):
        i = pl.program_id(1)
        _load_keys(i, k_ref, v_ref, kpad, vpad, S)
        koff = pl.multiple_of(i * QB, QB)
        lane = lax.broadcasted_iota(jnp.int32, (1, 128), 1)

        def step(in_seq):
            lse = jnp.zeros((QB, 128), F32)
            for pair in range(ATT_HEADS // 2):
                cols = slice(pair * 128, (pair + 1) * 128)
                q2s = _scaled(q_ref[0, :, cols])
                k2 = kpad[pl.ds(koff, KWIN), cols]
                v2 = vpad[pl.ds(koff, KWIN), cols]
                o2 = jnp.zeros((QB, 128), F32)
                for hh in range(2):
                    head = 2 * pair + hh
                    in_head = (lane // ATT_HD) == hh
                    s = _att_scores(q2s, k2, tab_ref, head, in_head, in_seq)
                    m = jnp.max(s, axis=-1, keepdims=True)
                    e = jnp.exp(s - m)
                    l = jnp.sum(e, axis=-1, keepdims=True)
                    o2 = jnp.where(in_head, _dot(e.astype(BF16), v2) * (1.0 / l), o2)
                    lse = jnp.where(lane == head, m + jnp.log(l), lse)
                o_ref[0, :, cols] = o2.astype(BF16)
            lse_ref[0] = lse
        _by_window(i, step)

    seq = lambda col: pl.BlockSpec((1, S, ATT_W), lambda b, i: (b, 0, col), pipeline_mode=pl.Buffered(1))
    outs, rode = _pallas(
        body, "att_fwd", (NB, S // QB),
        [pl.BlockSpec((1, QB, ATT_W), lambda b, i: (b, i, 0)), seq(1), seq(2), _const((ATT_HEADS, QB, KWIN))],
        [pl.BlockSpec((1, QB, ATT_W), lambda b, i: (b, i, 0)), pl.BlockSpec((1, QB, 128), lambda b, i: (b, i, 0))],
        [jax.ShapeDtypeStruct((NB, S, ATT_W), BF16), jax.ShapeDtypeStruct((NB, S, 128), F32)],
        (qkv, qkv, qkv, tab),
        scratch_shapes=[pltpu.VMEM((S + KPAD, ATT_W), BF16), pltpu.VMEM((S + KPAD, ATT_W), BF16)],
        sem=("arbitrary", "arbitrary"), ride=ride)
    return outs[0], outs[1], rode


def _att_bwd(qkv, o, lse, do, tab, ride=None):
    NB, S, _ = qkv.shape
    nq = S // QB

    def body(q_ref, k_ref, v_ref, o_ref, lse_ref, do_ref, tab_ref, dqkv_ref, ds_hbm, kpad, vpad, dkpad, dvpad, ds_acc, ds_sem):
        b, i = pl.program_id(0), pl.program_id(1)
        _load_keys(i, k_ref, v_ref, kpad, vpad, S)

        @pl.when(i == 0)
        def _():
            dkpad[...] = jnp.zeros_like(dkpad)
            dvpad[...] = jnp.zeros_like(dvpad)

        @pl.when((i == 0) & (b == 0))
        def _():
            ds_acc[...] = jnp.zeros_like(ds_acc)

        koff = pl.multiple_of(i * QB, QB)
        lane = lax.broadcasted_iota(jnp.int32, (1, 128), 1)

        def step(in_seq):
            for pair in range(ATT_HEADS // 2):
                cols = slice(pair * 128, (pair + 1) * 128)
                q2s = _scaled(q_ref[0, :, cols])
                do2 = do_ref[0, :, cols]
                k2 = kpad[pl.ds(koff, KWIN), cols]
                v2 = vpad[pl.ds(koff, KWIN), cols]
                do_o = do2.astype(F32) * o_ref[0, :, cols].astype(F32)
                dq2 = jnp.zeros((QB, 128), F32)
                dk2 = jnp.zeros((KWIN, 128), F32)
                dv2 = jnp.zeros((KWIN, 128), F32)
                for hh in range(2):
                    head = 2 * pair + hh
                    in_head = (lane // ATT_HD) == hh
                    p = jnp.exp(_att_scores(q2s, k2, tab_ref, head, in_head, in_seq) - lse_ref[0, :, head:head + 1])
                    row_term = jnp.sum(jnp.where(in_head, do_o, 0.0), axis=-1, keepdims=True)
                    dom = jnp.where(in_head, do2, jnp.zeros_like(do2))
                    ds = p * (_dot_nt(dom, v2) - row_term)
                    ds_acc[head] += ds
                    dsb = ds.astype(BF16)
                    dq2 = jnp.where(in_head, _dot(dsb, k2), dq2)
                    dk2 = jnp.where(in_head, _dot_tn(dsb, q2s), dk2)
                    dv2 = jnp.where(in_head, _dot_tn(p.astype(BF16), do2), dv2)
                dqkv_ref[0, pl.ds(koff, QB), cols] = (dq2 * (ATT_HD ** -0.5)).astype(BF16)
                dkpad[pl.ds(koff, KWIN), cols] += dk2
                dvpad[pl.ds(koff, KWIN), cols] += dv2
        _by_window(i, step)

        @pl.when(i == nq - 1)
        def _():
            dqkv_ref[0, :, ATT_W:2 * ATT_W] = dkpad[pl.ds(KPAD, S), :].astype(BF16)
            dqkv_ref[0, :, 2 * ATT_W:3 * ATT_W] = dvpad[pl.ds(KPAD, S), :].astype(BF16)

        @pl.when((i == nq - 1) & (b == NB - 1))
        def _():
            out = pltpu.make_async_copy(ds_acc, ds_hbm, ds_sem)
            out.start()
            out.wait()

    seq = lambda col: pl.BlockSpec((1, S, ATT_W), lambda b, i: (b, 0, col), pipeline_mode=pl.Buffered(1))
    rows = pl.BlockSpec((1, QB, ATT_W), lambda b, i: (b, i, 0))
    return _pallas(
        body, "att_bwd", (NB, nq),
        [rows, seq(1), seq(2), rows, pl.BlockSpec((1, QB, 128), lambda b, i: (b, i, 0)), rows, _const((ATT_HEADS, QB, KWIN))],
        [pl.BlockSpec((1, S, 3 * ATT_W), lambda b, i: (b, 0, 0)), ANY],
        [jax.ShapeDtypeStruct((NB, S, 3 * ATT_W), BF16), jax.ShapeDtypeStruct((ATT_HEADS, QB, KWIN), F32)],
        (qkv, qkv, qkv, o, lse, do, tab),
        scratch_shapes=[pltpu.VMEM((S + KPAD, ATT_W), BF16), pltpu.VMEM((S + KPAD, ATT_W), BF16),
                        pltpu.VMEM((S + KPAD, ATT_W), F32), pltpu.VMEM((S + KPAD, ATT_W), F32),
                        pltpu.VMEM((ATT_HEADS, QB, KWIN), F32), pltpu.SemaphoreType.DMA],
        sem=("arbitrary", "arbitrary"), ride=ride)


def _rel_bias_grad(ds):
    def body(ds_ref, o_ref):
        sub = lax.broadcasted_iota(jnp.int32, (8, 1), 0)
        ridx = _rel_index_of_column(lax.broadcasted_iota(jnp.int32, (DS_LANES, 1), 0))
        onehot = (ridx == lax.broadcasted_iota(jnp.int32, (1, N_REL), 1)).astype(F32)
        def rows8(q8, accs):
            shift = lax.rem(DS_LANES - q8 * 8, DS_LANES)
            out = []
            for head in range(ATT_HEADS):
                tile = ds_ref[head, pl.ds(pl.multiple_of(q8 * 8, 8), 8), :]
                tile = jnp.concatenate([tile, jnp.zeros((8, DS_LANES - KWIN), F32)], axis=1)
                out.append(accs[head] + pltpu.roll(tile, shift, 1))
            return tuple(out)
        accs = lax.fori_loop(0, QB // 8, rows8, tuple(jnp.zeros((8, DS_LANES), F32) for _ in range(ATT_HEADS)))
        for head in range(ATT_HEADS):
            acc = accs[head]
            diag = jnp.zeros((8, DS_LANES), F32)
            for s in range(8):
                shifted = acc if s == 0 else pltpu.roll(acc, DS_LANES - s, 1)
                diag = jnp.where(sub == s, shifted, diag)
            z = jnp.sum(diag, axis=0, keepdims=True)
            o_ref[head:head + 1, :] = jnp.dot(z, onehot, preferred_element_type=F32, precision=lax.Precision.HIGHEST)

    return pl.pallas_call(body, out_shape=jax.ShapeDtypeStruct((ATT_HEADS, N_REL), F32), name="rel_bias_grad",
                          compiler_params=_params())(ds)


def _memkv_fwd(mem, g, w_kv, tm):
    R = mem.shape[0]
    tm = min(tm, R)

    def body(m_ref, g_ref, w_ref, h_ref, kv_ref):
        xhat, _ = _rms_stats(m_ref[...])
        h = (xhat * g_ref[...]).astype(BF16)
        h_ref[...] = h
        kv_ref[...] = _dot(h, w_ref[...]).astype(BF16)

    row = pl.BlockSpec((tm, D), lambda t: (t, 0))
    return pl.pallas_call(
        body, grid=(R // tm,), name="memkv_fwd", in_specs=[row, _const((1, D)), _const((D, 2 * MEM_W))], out_specs=[row, row],
        out_shape=[jax.ShapeDtypeStruct((R, D), BF16), jax.ShapeDtypeStruct((R, 2 * MEM_W), BF16)],
        compiler_params=_params("parallel"),
    )(mem, g, w_kv)


def _memkv_bwd(mem, dkv, w_kv, tm):
    R = mem.shape[0]
    tm = min(tm, R)

    def body(m_ref, dkv_ref, w_ref, dg_ref):
        xhat, _ = _rms_stats(m_ref[...])
        dh = _dot_nt(dkv_ref[...].astype(BF16), w_ref[...])

        @pl.when(pl.program_id(0) == 0)
        def _():
            dg_ref[...] = jnp.zeros_like(dg_ref)
        dg_ref[...] += jnp.sum(dh * xhat, axis=0, keepdims=True)

    row = pl.BlockSpec((tm, D), lambda t: (t, 0))
    return pl.pallas_call(
        body, grid=(R // tm,), name="memkv_bwd", in_specs=[row, row, _const((D, 2 * MEM_W))],
        out_specs=pl.BlockSpec((1, D), lambda t: (0, 0)), out_shape=jax.ShapeDtypeStruct((1, D), F32),
        compiler_params=_params("arbitrary"),
    )(mem, dkv, w_kv)


def _mem_exp(qh, kh):
    s = _dot_nt(qh, kh) * (MEM_HD ** -0.5)
    e = jnp.exp(s - jnp.max(s, axis=-1, keepdims=True))
    return e, jnp.sum(e, axis=-1, keepdims=True)


def _mematt_fwd(mq, kv, tq):
    NB, S, _ = mq.shape
    M = kv.shape[1]

    def body(q_ref, kv_ref, o_ref):
        for h in range(MEM_HEADS):
            cols = slice(h * MEM_HD, (h + 1) * MEM_HD)
            e, l = _mem_exp(q_ref[0, :, cols], kv_ref[0, :, cols])
            o = _dot(e.astype(BF16), kv_ref[0, :, MEM_W + h * MEM_HD:MEM_W + (h + 1) * MEM_HD]) * (1.0 / l)
            o_ref[0, :, cols] = o.astype(BF16)

    return pl.pallas_call(
        body, grid=(NB, S // tq), name="mematt_fwd",
        in_specs=[pl.BlockSpec((1, tq, MEM_W), lambda b, i: (b, i, 0)), pl.BlockSpec((1, M, 2 * MEM_W), lambda b, i: (b, 0, 0))],
        out_specs=pl.BlockSpec((1, tq, MEM_W), lambda b, i: (b, i, 0)),
        out_shape=jax.ShapeDtypeStruct((NB, S, MEM_W), BF16), compiler_params=_params("parallel", "parallel"),
    )(mq, kv)


def _mematt_bwd(mq, kv, do, tq):
    NB, S, _ = mq.shape
    M = kv.shape[1]

    def body(q_ref, kv_ref, do_ref, dq_ref, dkv_ref):
        @pl.when(pl.program_id(1) == 0)
        def _():
            dkv_ref[...] = jnp.zeros_like(dkv_ref)
        for h in range(MEM_HEADS):
            cols = slice(h * MEM_HD, (h + 1) * MEM_HD)
            vcols = slice(MEM_W + h * MEM_HD, MEM_W + (h + 1) * MEM_HD)
            qh, kh, vh, doh = q_ref[0, :, cols], kv_ref[0, :, cols], kv_ref[0, :, vcols], do_ref[0, :, cols]
            e, l = _mem_exp(qh, kh)
            p = e * (1.0 / l)
            dp = _dot_nt(doh, vh)
            ds = p * (dp - jnp.sum(p * dp, axis=-1, keepdims=True))
            dss = (ds * (MEM_HD ** -0.5)).astype(BF16)
            dq_ref[0, :, cols] = _dot(dss, kh).astype(BF16)
            dkv_ref[0, :, cols] += _dot_tn(dss, qh)
            dkv_ref[0, :, vcols] += _dot_tn(p.astype(BF16), doh)

    qspec = pl.BlockSpec((1, tq, MEM_W), lambda b, i: (b, i, 0))
    kvspec = pl.BlockSpec((1, M, 2 * MEM_W), lambda b, i: (b, 0, 0))
    return pl.pallas_call(
        body, grid=(NB, S // tq), name="mematt_bwd", in_specs=[qspec, kvspec, qspec], out_specs=[qspec, kvspec],
        out_shape=[jax.ShapeDtypeStruct((NB, S, MEM_W), BF16), jax.ShapeDtypeStruct((NB, M, 2 * MEM_W), F32)],
        compiler_params=_params("arbitrary", "arbitrary"),
    )(mq, kv, do)


def _branch(j, in_ref, w_ref, gl_ref, bg_ref):
    y = _dot(in_ref[...], w_ref[...])
    gate = _sigmoid(gl_ref[:, j * D:(j + 1) * D].astype(F32) + bg_ref[:, j * D:(j + 1) * D])
    return y, gate


def _combine_fwd(x, cact, oatt, omem, gl, bg, wpw, wo, wmo, wout, tm):
    T = x.shape[0]

    def body(x_ref, c_ref, a_ref, m_ref, gl_ref, bg_ref, wpw_ref, wo_ref, wmo_ref, wout_ref, xo_ref, y_ref):
        y = None
        for j, (in_ref, w_ref) in enumerate(((c_ref, wpw_ref), (a_ref, wo_ref), (m_ref, wmo_ref))):
            yj, gate = _branch(j, in_ref, w_ref, gl_ref, bg_ref)
            y = gate * yj if y is None else y + gate * yj
        y = y.astype(BF16)
        y_ref[...] = y
        xo_ref[...] = x_ref[...] + _dot(y, wout_ref[...])

    row = lambda w: pl.BlockSpec((tm, w), lambda t: (t, 0))
    wbr = _const((512, D))
    return pl.pallas_call(
        body, grid=(T // tm,), name="combine_fwd",
        in_specs=[row(D), row(512), row(512), row(512), row(3 * D), _const((1, 3 * D)), wbr, wbr, wbr, _const((D, D))],
        out_specs=[row(D), row(D)],
        out_shape=[jax.ShapeDtypeStruct((T, D), F32), jax.ShapeDtypeStruct((T, D), BF16)],
        compiler_params=_params("parallel"),
    )(x, cact, oatt, omem, gl, bg, wpw, wo, wmo, wout)


def _combine_bwd(dx, cact, oatt, omem, gl, bg, wpw, wo, wmo, wout, tm, ride=None):
    T = dx.shape[0]

    def body(dx_ref, c_ref, a_ref, m_ref, gl_ref, bg_ref, wpw_ref, wo_ref, wmo_ref, wout_ref,
             dgl_ref, dc_ref, da_ref, dm_ref, dyc_ref, dya_ref, dym_ref, dbg_ref):
        dy = _dot_nt(dx_ref[...].astype(BF16), wout_ref[...])

        @pl.when(pl.program_id(0) == 0)
        def _():
            dbg_ref[...] = jnp.zeros_like(dbg_ref)
        branches = ((c_ref, wpw_ref, dyc_ref, dc_ref), (a_ref, wo_ref, dya_ref, da_ref), (m_ref, wmo_ref, dym_ref, dm_ref))
        for j, (in_ref, w_ref, dyb_ref, din_ref) in enumerate(branches):
            yj, gate = _branch(j, in_ref, w_ref, gl_ref, bg_ref)
            dyg = dy * gate
            dlogit = dyg * yj * (1.0 - gate)
            dgl_ref[:, j * D:(j + 1) * D] = dlogit.astype(BF16)
            dbg_ref[:, j * D:(j + 1) * D] += jnp.sum(dlogit, axis=0, keepdims=True)
            dyb = dyg.astype(BF16)
            dyb_ref[...] = dyb
            din_ref[...] = _dot_nt(dyb, w_ref[...]).astype(din_ref.dtype)

    row = lambda w: pl.BlockSpec((tm, w), lambda t: (t, 0))
    wbr = _const((512, D))
    sds = jax.ShapeDtypeStruct
    return _pallas(
        body, "combine_bwd", (T // tm,),
        [row(D), row(512), row(512), row(512), row(3 * D), _const((1, 3 * D)), wbr, wbr, wbr, _const((D, D))],
        [row(3 * D), row(512), row(512), row(512), row(D), row(D), row(D), pl.BlockSpec((1, 3 * D), lambda t: (0, 0))],
        [sds((T, 3 * D), BF16), sds((T, 512), F32), sds((T, 512), BF16), sds((T, 512), BF16),
         sds((T, D), BF16), sds((T, D), BF16), sds((T, D), BF16), sds((1, 3 * D), F32)],
        (dx, cact, oatt, omem, gl, bg, wpw, wo, wmo, wout), sem=("arbitrary",), ride=ride)


def _peer(x, y, c, rel):
    rx, ry, rc = (rel >> 2) & 1, (rel >> 1) & 1, rel & 1
    return ((1 - x) if rx else x, (1 - y) if ry else y, (1 - c) if rc else c)


def _all_sum_small(parts):
    n = len(parts)

    def body(*refs):
        p_refs, o_refs, slots = refs[:n], refs[n:2 * n], refs[2 * n:3 * n]
        send_sems, recv_sems = refs[3 * n:]
        x, y, c = _my_coords()
        me = _dev_index(x, y, c)

        def copy(i, rel, arrival):
            peer = _peer(x, y, c, rel)
            return pltpu.make_async_remote_copy(
                src_ref=p_refs[i], dst_ref=slots[i].at[_dev_index(*peer) if arrival else me],
                send_sem=send_sems.at[i, rel - 1], recv_sem=recv_sems.at[i, rel - 1], device_id=peer, device_id_type=MESH)

        for i in range(n):
            slots[i][me] = p_refs[i][...]
        for rel in range(1, NDEV):
            for i in range(n):
                copy(i, rel, False).start()
        for rel in range(1, NDEV):
            for i in range(n):
                copy(i, rel, True).wait_recv()
        for rel in range(1, NDEV):
            for i in range(n):
                copy(i, rel, False).wait_send()
        for i in range(n):
            total = slots[i][0]
            for d in range(1, NDEV):
                total = total + slots[i][d]
            o_refs[i][...] = total

    vmem = pl.BlockSpec(memory_space=pltpu.VMEM)
    return pl.pallas_call(
        body, out_shape=[jax.ShapeDtypeStruct(p.shape, F32) for p in parts], name="all_sum_small",
        in_specs=[vmem] * n, out_specs=[vmem] * n,
        scratch_shapes=[pltpu.VMEM((NDEV,) + p.shape, F32) for p in parts]
        + [pltpu.SemaphoreType.DMA((n, NDEV - 1)), pltpu.SemaphoreType.DMA((n, NDEV - 1))],
        compiler_params=pltpu.CompilerParams(has_side_effects=True),
    )(*parts)


HBM = pl.BlockSpec(memory_space=pltpu.HBM)
SEM = pl.BlockSpec(memory_space=pltpu.SEMAPHORE)


def _own_block(g, kind, m, tag):
    def body(g_ref, land_ref, staged, sem):
        me = _dev_index(*_my_coords())
        for cp in (pltpu.make_async_copy(_window(g_ref, kind, m, me), staged, sem),
                   pltpu.make_async_copy(staged, land_ref.at[me], sem)):
            cp.start()
            cp.wait()

    block = (m, g.shape[1]) if kind == 'row' else (g.shape[0], m)
    return pl.pallas_call(body, in_specs=[ANY], out_specs=ANY, out_shape=jax.ShapeDtypeStruct((NDEV,) + block, g.dtype),
                          scratch_shapes=[pltpu.VMEM(block, g.dtype), pltpu.SemaphoreType.DMA], name="own_block_" + tag)(g)


def _scatter_start(g, land, kind, m, tag):
    def body(g_ref, land_ref, send_sems, recv_sems, g_thru, land_thru, token):
        x, y, c = _my_coords()
        me = _dev_index(x, y, c)
        for rel in range(1, NDEV):
            peer = _peer(x, y, c, rel)
            pltpu.make_async_remote_copy(src_ref=_window(g_ref, kind, m, _dev_index(*peer)), dst_ref=land_ref.at[me],
                                         send_sem=send_sems.at[rel - 1], recv_sem=recv_sems.at[rel - 1],
                                         device_id=peer, device_id_type=MESH).start()
        token[...] = jnp.zeros_like(token)

    return pl.pallas_call(
        body, name="scatter_start_" + tag,
        out_shape=(pltpu.SemaphoreType.DMA((NDEV - 1,)), pltpu.SemaphoreType.DMA((NDEV - 1,)), pltpu.HBM(g.shape, g.dtype),
                   pltpu.HBM(land.shape, land.dtype), jax.ShapeDtypeStruct((8, 128), F32)),
        in_specs=(HBM, HBM), out_specs=(SEM, SEM, HBM, HBM, pl.BlockSpec(memory_space=pltpu.VMEM)),
        input_output_aliases={0: 2, 1: 3},
        compiler_params=pltpu.CompilerParams(has_side_effects=pltpu.SideEffectType.DATAFLOW_SIDE_EFFECTING),
    )(pltpu.with_memory_space_constraint(g, pltpu.HBM), pltpu.with_memory_space_constraint(land, pltpu.HBM))


def _scatter_wait(send_sems, recv_sems, g_thru, land_thru, after, kind, m, tag):
    n_after = len(after)

    def body(*refs):
        g_ref, land_ref, send_sems, recv_sems = refs[:4]
        x, y, c = _my_coords()
        me = _dev_index(x, y, c)
        for rel in range(1, NDEV):
            peer = _peer(x, y, c, rel)
            dev = _dev_index(*peer)
            cp = pltpu.make_async_remote_copy(src_ref=_window(g_ref, kind, m, me), dst_ref=land_ref.at[dev],
                                              send_sem=send_sems.at[rel - 1], recv_sem=recv_sems.at[rel - 1],
                                              device_id=peer, device_id_type=MESH)
            cp.wait_send()
            cp.wait_recv()

    return pl.pallas_call(
        body, name="scatter_wait_" + tag,
        out_shape=(pltpu.HBM(g_thru.shape, g_thru.dtype), pltpu.HBM(land_thru.shape, land_thru.dtype)),
        in_specs=(HBM, HBM, SEM, SEM) + (ANY,) * n_after, out_specs=(HBM, HBM), input_output_aliases={0: 0, 1: 1},
        compiler_params=pltpu.CompilerParams(has_side_effects=pltpu.SideEffectType.DATAFLOW_SIDE_EFFECTING),
    )(g_thru, land_thru, send_sems, recv_sems, *after)[1]


def _adamw_math(w, g, m, v):
    m = ADAM_B1 * m + (1.0 - ADAM_B1) * g
    v = ADAM_B2 * v + (1.0 - ADAM_B2) * (g * g)
    m_hat = m / (1.0 - ADAM_B1 ** ADAM_STEP)
    v_hat = v / (1.0 - ADAM_B2 ** ADAM_STEP)
    delta = -ADAM_LR * (m_hat / (jnp.sqrt(v_hat) + ADAM_EPS) + ADAM_WD * w)
    return delta, m, v


def _sum_adamw(parts, w, m, v, name, after=None):
    R, C = w.shape
    n_parts = len(parts)
    cg = C // n_parts
    tr = max(t for t in range(8, 257, 8) if R % t == 0)
    deps = [] if after is None else [after]

    def body(*refs):
        p_refs = refs[:n_parts]
        w_ref, m_ref, v_ref = refs[n_parts:n_parts + 3]
        g_ref, d_ref, mo_ref, vo_ref = refs[n_parts + 3 + len(deps):]
        for k, p_ref in enumerate(p_refs):
            @pl.when(pl.program_id(0) == k)
            def _():
                g = p_ref[0].astype(F32)
                for d in range(1, NDEV):
                    g = g + p_ref[d].astype(F32)
                g_ref[...] = g
                d_ref[...], mo_ref[...], vo_ref[...] = _adamw_math(w_ref[...], g, m_ref[...], v_ref[...])

    part = pl.BlockSpec((NDEV, tr, cg), lambda k, t: (0, t, 0))
    blk = pl.BlockSpec((tr, cg), lambda k, t: (t, k))
    return pl.pallas_call(
        body, grid=(n_parts, R // tr), name=name, in_specs=[part] * n_parts + [blk, blk, blk] + [ANY] * len(deps),
        out_specs=[blk] * 4, out_shape=[jax.ShapeDtypeStruct((R, C), F32)] * 4, compiler_params=_params("parallel", "parallel"),
    )(*parts, w, m, v, *deps)


def _adamw_small(ws, gs, ms, vs):
    n = len(ws)

    def body(*refs):
        w_refs, g_refs, m_refs, v_refs = (refs[k * n:(k + 1) * n] for k in range(4))
        d_refs, mo_refs, vo_refs = (refs[(4 + k) * n:(5 + k) * n] for k in range(3))
        for i in range(n):
            d_refs[i][...], mo_refs[i][...], vo_refs[i][...] = _adamw_math(w_refs[i][...], g_refs[i][...], m_refs[i][...], v_refs[i][...])

    shapes = [jax.ShapeDtypeStruct(a.shape, F32) for a in ws]
    outs = pl.pallas_call(body, out_shape=shapes * 3, name="adamw_small", compiler_params=_params())(*ws, *gs, *ms, *vs)
    return outs[:n], outs[n:2 * n], outs[2 * n:]


def kernel(x, mem, ffn1_norm, ffn1_w_up, ffn1_w_down, mix_norm, mem_norm, w_in, b_gate, conv_dw_w, conv_dw_b, conv_ln_g, conv_ln_b, conv_w_pw, att_rel_bias, att_w_o, mem_w_kv, mem_w_o, w_out, ffn2_norm, ffn2_w_up, ffn2_w_down, final_norm, loss_target, m_ffn1_norm, m_ffn1_w_up, m_ffn1_w_down, m_mix_norm, m_mem_norm, m_w_in, m_b_gate, m_conv_dw_w, m_conv_dw_b, m_conv_ln_g, m_conv_ln_b, m_conv_w_pw, m_att_rel_bias, m_att_w_o, m_mem_w_kv, m_mem_w_o, m_w_out, m_ffn2_norm, m_ffn2_w_up, m_ffn2_w_down, m_final_norm, v_ffn1_norm, v_ffn1_w_up, v_ffn1_w_down, v_mix_norm, v_mem_norm, v_w_in, v_b_gate, v_conv_dw_w, v_conv_dw_b, v_conv_ln_g, v_conv_ln_b, v_conv_w_pw, v_att_rel_bias, v_att_w_o, v_mem_w_kv, v_mem_w_o, v_w_out, v_ffn2_norm, v_ffn2_w_up, v_ffn2_w_down, v_final_norm):
    given = dict(locals())
    w = {n: given[n] for n in WEIGHTS}
    mom = {n: given["m_" + n] for n in WEIGHTS}
    var = {n: given["v_" + n] for n in WEIGHTS}

    NB, S, _ = x.shape
    T = NB * S
    ML = mem.shape[1]
    x0 = x.reshape(T, D)
    target = loss_target.reshape(T, D)
    mem2 = mem.reshape(NB * ML, D)

    def block(t, n):
        return jnp.transpose(t[0]) if n in TRANSPOSED else t[0]

    sh = dict(zip(BIG_ORDER, _cast_shards([block(w[n], n) for n in BIG_ORDER])))
    dw_t = jnp.transpose(conv_dw_w[0])

    def gather(names, extra=(), extra_kinds=()):
        return _gather_ride([sh[n] for n in names] + list(extra), [BIG[n] for n in names] + list(extra_kinds))

    W = {}
    names0 = ['ffn1_w_up', 'ffn1_w_down']
    tab, got = _bias_table(att_rel_bias[0], ride=gather(names0, [dw_t], [('row', dw_t.shape[0])]))
    W.update(zip(names0, got[:2]))
    dw_full = jnp.transpose(got[2])
    conv_vec = jnp.concatenate([conv_dw_b, conv_ln_g, conv_ln_b, jnp.zeros((5, CONV_W), F32)], axis=0)
    fin_g = final_norm.reshape(1, D)

    names1 = ['w_in', 'conv_w_pw', 'att_w_o', 'mem_w_kv', 'mem_w_o', 'w_out']
    (x1, ab1), got = _ffn_fwd(x0, ffn1_norm, W['ffn1_w_up'], W['ffn1_w_down'], TILE_FFN_FWD, "ffn1_fwd", ride=gather(names1))
    W.update(zip(names1, got))
    (uc, qkv, mq, gl, hmix), _ = _mix_fwd(x1, mix_norm, W['w_in'], TILE_TOKENS)
    uc3 = uc.reshape(NB, S, 2 * CONV_W)
    qkv3 = qkv.reshape(NB, S, 3 * ATT_W)
    mq3 = mq.reshape(NB, S, MEM_W)
    cact, conv_z = _conv_fwd(uc3, dw_full, conv_vec)
    cact = cact.reshape(T, CONV_W)
    names2 = ['ffn2_w_up', 'ffn2_w_down']
    oatt3, att_lse, got = _att_fwd(qkv3, tab, ride=gather(names2))
    W.update(zip(names2, got))
    oatt = oatt3.reshape(T, ATT_W)
    memh, kv = _memkv_fwd(mem2, mem_norm, W['mem_w_kv'], TILE_TOKENS)
    kv3 = kv.reshape(NB, ML, 2 * MEM_W)
    omem = _mematt_fwd(mq3, kv3, TILE_TOKENS).reshape(T, MEM_W)
    branch_w = (W['conv_w_pw'], W['att_w_o'], W['mem_w_o'], W['w_out'])
    x2, ymix = _combine_fwd(x1, cact, oatt, omem, gl, b_gate, *branch_w, TILE_TOKENS)
    dx3, ab2, loss_part, dg_final = _ffn_fwd_loss(x2, ffn2_norm, W['ffn2_w_up'], W['ffn2_w_down'], fin_g, target, TILE_FFN_FWD,
                                                  "ffn2_fwd_loss")

    def scatter(grads, names):
        return _scatter_ride(grads, [BIG[n] for n in names])

    G, P = {}, {}
    dx2, dab2, act2, h2, dg_ffn2 = _ffn_bwd(x2, dx3, ab2, ffn2_norm, W['ffn2_w_up'], W['ffn2_w_down'], TILE_FFN, "ffn2_bwd")
    g_up, _ = _tn_matmul(dab2, h2, 512, "grad_ffn2_w_up_a", tt=TILE_GRAD_TOKENS_WIDE, x_part=(0, 2), out_rows=2 * FF)
    G['ffn2_w_up'], _ = _tn_matmul(dab2, h2, 512, "grad_ffn2_w_up_b", tt=TILE_GRAD_TOKENS_WIDE, x_part=(1, 2), out_rows=2 * FF,
                                   prev=g_up)
    G['ffn2_w_down'], _ = _tn_matmul(act2, dx3, 512, "grad_ffn2_w_down", scale=0.5, tt=TILE_GRAD_TOKENS_WIDE)
    (dgl, dcact, doatt, domem, dyc, dya, dym, dbg), got = _combine_bwd(
        dx2, cact, oatt, omem, gl, b_gate, *branch_w, TILE_COMBINE, ride=scatter([G['ffn2_w_up']], ['ffn2_w_up']))
    P['ffn2_w_up'] = got
    G['w_out'], _ = _tn_matmul(ymix, dx2, D, "grad_w_out", tt=TILE_GRAD_TOKENS_WIDE)
    G['conv_w_pw'], _ = _tn_matmul(cact, dyc, D, "grad_conv_w_pw")
    G['att_w_o'], _ = _tn_matmul(oatt, dya, D, "grad_att_w_o")
    G['mem_w_o'], _ = _tn_matmul(omem, dym, D, "grad_mem_w_o")
    dmq3, dkv3 = _mematt_bwd(mq3, kv3, domem.reshape(NB, S, MEM_W), TILE_TOKENS)
    dkv = dkv3.reshape(NB * ML, 2 * MEM_W)
    dg_mem = _memkv_bwd(mem2, dkv, W['mem_w_kv'], TILE_TOKENS)
    G['mem_w_kv'], _ = _tn_matmul(memh, dkv, 512, "grad_mem_w_kv")
    names = ['ffn2_w_down', 'w_out', 'conv_w_pw', 'att_w_o', 'mem_w_o']
    (dqkv3, dscore), got = _att_bwd(qkv3, oatt3, att_lse, doatt.reshape(NB, S, ATT_W), tab,
                                    ride=scatter([G[n] for n in names], names))
    P.update((n, [p]) for n, p in zip(names, got))
    d_rel = _rel_bias_grad(dscore)
    (duc3, d_dw, d_cvec), got = _conv_bwd(uc3, conv_z, dcact.reshape(NB, S, CONV_W), dw_full, conv_vec,
                                          ride=scatter([G['mem_w_kv']], ['mem_w_kv']))
    P['mem_w_kv'] = got
    duc, dqkv, dmq = duc3.reshape(T, 2 * CONV_W), dqkv3.reshape(T, 3 * ATT_W), dmq3.reshape(T, MEM_W)
    g_in, _ = _tn_matmul(hmix, duc, 1024, "grad_w_in_conv", out_cols=IN_COLS, col_off=0)
    g_in, _ = _tn_matmul(hmix, dqkv, 512, "grad_w_in_qkv", out_cols=IN_COLS, col_off=1024, prev=g_in)
    g_in, _ = _tn_matmul(hmix, dmq, 512, "grad_w_in_mq", out_cols=IN_COLS, col_off=2560, prev=g_in)
    G['w_in'], _ = _tn_matmul(hmix, dgl, 1024, "grad_w_in_gate", out_cols=IN_COLS, col_off=3072, prev=g_in)
    def start_scatter(g, name, tag):
        kind = BIG[name]
        return _scatter_start(g, _own_block(g, *kind, tag), *kind, tag) + (kind, tag)

    def wait_scatter(started, after):
        send_sems, recv_sems, g_thru, land_thru, _, kind, tag = started
        return _scatter_wait(send_sems, recv_sems, g_thru, land_thru, after, *kind, tag)

    ex_in = start_scatter(G['w_in'], 'w_in', "w_in")
    (dx1, dg_mix), _ = _mix_bwd(x1, dx2, duc, dqkv, dmq, dgl, mix_norm, W['w_in'], TILE_TOKENS, after=ex_in[4])
    dx0, dab1, act1, h1, dg_ffn1 = _ffn_bwd(x0, dx1, ab1, ffn1_norm, W['ffn1_w_up'], W['ffn1_w_down'], TILE_FFN, "ffn1_bwd")
    g_wd1, _ = _tn_matmul(act1, dx1, 512, "grad_ffn1_w_down", scale=0.5, tt=TILE_GRAD_TOKENS_WIDE)
    ex_wd = start_scatter(g_wd1, 'ffn1_w_down', "ffn1_w_down")
    g_wu1a, _ = _tn_matmul(dab1, h1, 512, "grad_ffn1_w_up_a", tt=TILE_GRAD_TOKENS_WIDEST, y_part=(0, 2), after=ex_wd[4])
    ex_a = start_scatter(g_wu1a, 'ffn1_w_up', "ffn1_w_up_a")
    g_wu1b, _ = _tn_matmul(dab1, h1, 512, "grad_ffn1_w_up_b", tt=TILE_GRAD_TOKENS_WIDEST, y_part=(1, 2), after=ex_a[4])
    ex_b = start_scatter(g_wu1b, 'ffn1_w_up', "ffn1_w_up_b")
    token = ex_b[4]

    small_names = ['loss', 'ffn1_norm', 'mix_norm', 'mem_norm', 'b_gate', 'conv_dw_w', 'conv_vec', 'att_rel_bias', 'ffn2_norm',
                   'final_norm']
    small = dict(zip(small_names, _all_sum_small(
        [loss_part + token[0:1], dg_ffn1, dg_mix, dg_mem, dbg, d_dw, d_cvec, d_rel, dg_ffn2, dg_final])))
    loss = small['loss'][0, 0]
    me = _dev_index(*_my_coords())
    for i, n in enumerate(['conv_dw_b', 'conv_ln_g', 'conv_ln_b']):
        small[n] = small['conv_vec'][i:i + 1]
    small['conv_dw_w'] = lax.dynamic_slice(small['conv_dw_w'], (0, me * conv_dw_w.shape[2]), (CONV_K, conv_dw_w.shape[2]))
    little = [n for n in WEIGHTS if n not in BIG]
    as2d = lambda t, n: t.reshape(small[n].shape)
    d_s, m_s, v_s = _adamw_small([as2d(w[n], n) for n in little], [small[n] for n in little],
                                 [as2d(mom[n], n) for n in little], [as2d(var[n], n) for n in little])
    grad, delta, new_m, new_v = {}, {}, {}, {}
    for i, n in enumerate(little):
        grad[n], delta[n], new_m[n], new_v[n] = (t.reshape(w[n].shape) for t in (small[n], d_s[i], m_s[i], v_s[i]))
    done = [d_s[0]]
    waited = {'w_in': [ex_in], 'ffn1_w_down': [ex_wd], 'ffn1_w_up': [ex_a, ex_b]}
    order = [n for n in BIG_ORDER if n not in waited] + list(waited)
    for n in order:
        if n in waited:
            P[n] = [wait_scatter(ex, done) for ex in waited[n]]
        outs = _sum_adamw(P[n], block(w[n], n), block(mom[n], n), block(var[n], n), "adamw_" + n,
                          after=None if n in waited else token)
        done.append(outs[0])
        grad[n], delta[n], new_m[n], new_v[n] = ((jnp.transpose(t) if n in TRANSPOSED else t)[None] for t in outs)

    return (loss, dx0.reshape(NB, S, D), *[grad[n] for n in WEIGHTS], *[delta[n] for n in WEIGHTS],
            *[new_m[n] for n in WEIGHTS], *[new_v[n] for n in WEIGHTS])
```

```python
import functools

import jax
import jax.numpy as jnp
from jax import lax
from jax.experimental import pallas as pl
from jax.experimental.pallas import tpu as pltpu

F32 = jnp.float32
BF16 = jnp.bfloat16

EPS = 1e-6
MASK_VALUE = -1e30
D = 1024
NDEV = 8
FF = 2816
FF_SHARD = 704
FF_HALF_ROWS = 352
FF_BLOCK_EDGES = ()
IN_COLS = 6144
CONV_W = 512
CONV_K = 31
CONV_HALO = 32
CONV_CHUNK = 32
CONV_WIN = CONV_CHUNK + 40
GLU_CHUNK = 128
ATT_W = 512
ATT_HEADS = 8
ATT_HD = 64
CHUNK = 64
LEFT_CHUNKS = 8
MAX_REL = 128
N_REL = 192
QB = 256
KWIN = QB + LEFT_CHUNKS * CHUNK
KPAD = LEFT_CHUNKS * CHUNK
DS_LANES = 1024
MEM_W = 512
MEM_HEADS = 4
MEM_HD = 128
ADAM_LR = 0.001
ADAM_B1 = 0.9
ADAM_B2 = 0.999
ADAM_EPS = 1e-08
ADAM_WD = 0.01
ADAM_STEP = 10
VMEM_LIMIT = 60 * 1024 * 1024
TILE_FFN = 256
TILE_FFN_FWD = 512
TILE_COMBINE = 256
TILE_TOKENS = 512
TILE_GRAD_TOKENS = 2048
TILE_GRAD_TOKENS_WIDE = 1024
TILE_GRAD_TOKENS_WIDEST = 512

MESH = pl.DeviceIdType.MESH
ANY = pl.BlockSpec(memory_space=pl.ANY)

WEIGHTS = ['ffn1_norm', 'ffn1_w_up', 'ffn1_w_down', 'mix_norm', 'mem_norm', 'w_in', 'b_gate', 'conv_dw_w', 'conv_dw_b',
           'conv_ln_g', 'conv_ln_b', 'conv_w_pw', 'att_rel_bias', 'att_w_o', 'mem_w_kv', 'mem_w_o', 'w_out', 'ffn2_norm',
           'ffn2_w_up', 'ffn2_w_down', 'final_norm']
BIG = {
    'ffn1_w_up': ('row', FF_SHARD), 'ffn1_w_down': ('row', FF_HALF_ROWS), 'w_in': ('col', 768),
    'conv_w_pw': ('col', 128), 'att_w_o': ('col', 128), 'mem_w_kv': ('row', 128), 'mem_w_o': ('col', 128),
    'w_out': ('row', 128), 'ffn2_w_up': ('row', FF_SHARD), 'ffn2_w_down': ('row', FF_HALF_ROWS),
}
BIG_ORDER = ['ffn1_w_up', 'ffn1_w_down', 'w_in', 'conv_w_pw', 'att_w_o', 'mem_w_kv', 'mem_w_o', 'w_out', 'ffn2_w_up', 'ffn2_w_down']
TRANSPOSED = ('ffn1_w_up', 'ffn2_w_up')


def _dot(a, b):
    return jnp.dot(a, b, preferred_element_type=F32)


def _dot_nt(a, b):
    return lax.dot_general(a, b, (((1,), (1,)), ((), ())), preferred_element_type=F32)


def _dot_tn(a, b):
    return lax.dot_general(a, b, (((0,), (0,)), ((), ())), preferred_element_type=F32)


def _sigmoid(v):
    return jax.nn.sigmoid(v)


def _const(shape):
    return pl.BlockSpec(shape, lambda *_: (0,) * len(shape), pipeline_mode=pl.Buffered(1))


def _params(*sem):
    return pltpu.CompilerParams(dimension_semantics=sem if sem else None, vmem_limit_bytes=VMEM_LIMIT)


def _my_coords():
    return lax.axis_index("x"), lax.axis_index("y"), lax.axis_index("c")


def _dev_index(px, py, pc):
    return 4 * px + 2 * py + pc


def _window(ref, kind, n, p):
    if kind == 'row':
        return ref.at[pl.ds(pl.multiple_of(p * n, n), n), :]
    return ref.at[:, pl.ds(pl.multiple_of(p * n, 128), n)]


def _full_shape(kind, n, shard_shape):
    if kind == 'row':
        return (NDEV * n, shard_shape[1])
    return (shard_shape[0], NDEV * n)


def _cast_shards(shards):
    n = len(shards)

    def body(*refs):
        for i in range(n):
            refs[n + i][...] = refs[i][...].astype(BF16)

    out_shape = [jax.ShapeDtypeStruct(s.shape, BF16) for s in shards]
    return pl.pallas_call(body, out_shape=out_shape, name="cast_shards", compiler_params=_params())(*shards)


class _Ride:
    def __init__(self, inputs, out_shape, scratch, start, finish, mids=()):
        self.inputs, self.out_shape, self.scratch = list(inputs), list(out_shape), list(scratch)
        self.start, self.finish, self.mids = start, finish, tuple(mids)


def _pallas(body, name, grid, in_specs, out_specs, out_shape, args, scratch_shapes=(), sem=None, aliases=None, ride=None,
            after=None):
    if ride is None:
        n_in, n_dep = len(args), 0 if after is None else 1

        def kernel_body(*refs):
            body(*refs[:n_in], *refs[n_in + n_dep:])

        outs = pl.pallas_call(kernel_body if n_dep else body, grid=grid, name=name, in_specs=list(in_specs) + [ANY] * n_dep,
                              out_specs=out_specs, out_shape=out_shape, scratch_shapes=list(scratch_shapes),
                              input_output_aliases=aliases or {}, compiler_params=_params(*sem),
                              )(*args, *([after] if n_dep else []))
        return list(outs), []
    n_in, n_out, n_scr = len(args), len(out_shape), len(scratch_shapes)
    r_in, r_out = len(ride.inputs), len(ride.out_shape)

    def wrapped(*refs):
        k_in, rin = refs[:n_in], refs[n_in:n_in + r_in]
        o0 = n_in + r_in
        k_out, rout = refs[o0:o0 + n_out], refs[o0 + n_out:o0 + n_out + r_out]
        s0 = o0 + n_out + r_out
        k_scr, rscr = refs[s0:s0 + n_scr], refs[s0 + n_scr:]
        ids = [pl.program_id(k) for k in range(len(grid))]
        first = functools.reduce(jnp.logical_and, [i == 0 for i in ids])
        last = functools.reduce(jnp.logical_and, [i == g - 1 for i, g in zip(ids, grid)])
        pl.when(first)(lambda: ride.start(rin, rout, rscr))
        single_step = all(g == 1 for g in grid)
        for quarter, mid in ride.mids:
            if not single_step:
                at_mid = functools.reduce(jnp.logical_and, [ids[0] == (quarter * grid[0]) // 4] + [i == 0 for i in ids[1:]])
                pl.when(at_mid)(functools.partial(mid, rin, rout, rscr))
        body(*k_in, *k_out, *k_scr)
        for _, mid in ride.mids:
            if single_step:
                mid(rin, rout, rscr)
        pl.when(last)(lambda: ride.finish(rin, rout, rscr))

    outs = pl.pallas_call(
        wrapped, grid=grid, name=name, in_specs=list(in_specs) + [ANY] * r_in, out_specs=list(out_specs) + [ANY] * r_out,
        out_shape=list(out_shape) + ride.out_shape, scratch_shapes=list(scratch_shapes) + ride.scratch,
        input_output_aliases=aliases or {}, compiler_params=_params(*(["arbitrary"] * len(grid))),
    )(*args, *ride.inputs)
    return list(outs[:n_out]), list(outs[n_out:])


def _gather_ride(shards, kinds):
    n = len(shards)

    def plan(rin, out, sems):
        send_sems, recv_sems, local_sems = sems[:3]
        x, y, c = _my_coords()
        me, sibling = (x, y, c), (x, y, 1 - c)
        xn, yn, diag = (1 - x, y), (x, 1 - y), (1 - x, 1 - y)

        def win(i, dev):
            return _window(out[i], kinds[i][0], kinds[i][1], _dev_index(*dev))

        def copy(i, k, block, to, from_shard=False):
            return pltpu.make_async_remote_copy(
                src_ref=rin[i] if from_shard else win(i, block), dst_ref=win(i, block),
                send_sem=send_sems.at[i, k], recv_sem=recv_sems.at[i, k], device_id=to, device_id_type=MESH)

        def each(fn):
            return [fn(i) for i in range(n)]

        return dict(
            local=lambda: each(lambda i: pltpu.make_async_copy(rin[i], win(i, me), local_sems.at[i])),
            own=lambda: [cp for i in range(n) for cp in (copy(i, 0, me, sibling, True), copy(i, 1, me, (*xn, c), True),
                                                         copy(i, 2, me, (*yn, c), True))],
            from_x=lambda: each(lambda i: copy(i, 1, (*xn, c), me)),
            from_y=lambda: each(lambda i: copy(i, 2, (*yn, c), me)),
            x_block_on_to_y=lambda: each(lambda i: copy(i, 3, (*xn, c), (*yn, c))),
            y_block_on_to_x=lambda: each(lambda i: copy(i, 3, (*yn, c), (*xn, c))),
            from_diag=lambda: each(lambda i: copy(i, 3, (*diag, c), me)),
            to_sibling=lambda j: each(lambda i: copy(i, 4 + j, (*(xn, yn, diag)[j], c), sibling)),
            from_sibling=lambda: [cp for i in range(n) for cp in
                                  [copy(i, 0, sibling, me)] + [copy(i, 4 + j, (*chip, 1 - c), me) for j, chip in enumerate((xn, yn, diag))]],
            north=c == 1)

    def start(rin, out, sems):
        p = plan(rin, out, sems)
        for cp in p['local']() + p['own']():
            cp.start()

    def pass_diagonal(rin, out, sems):
        p = plan(rin, out, sems)

        @pl.when(p['north'])
        def _():
            for got, fwd, sib in zip(p['from_x'](), p['x_block_on_to_y'](), p['to_sibling'](0)):
                got.wait_recv()
                fwd.start()
                sib.start()

        @pl.when(jnp.logical_not(p['north']))
        def _():
            for got, fwd, sib in zip(p['from_y'](), p['y_block_on_to_x'](), p['to_sibling'](1)):
                got.wait_recv()
                fwd.start()
                sib.start()

    def pass_to_sibling(rin, out, sems):
        p = plan(rin, out, sems)

        @pl.when(p['north'])
        def _():
            for got, sib in zip(p['from_y'](), p['to_sibling'](1)):
                got.wait_recv()
                sib.start()

        @pl.when(jnp.logical_not(p['north']))
        def _():
            for got, sib in zip(p['from_x'](), p['to_sibling'](0)):
                got.wait_recv()
                sib.start()
        for got, sib in zip(p['from_diag'](), p['to_sibling'](2)):
            got.wait_recv()
            sib.start()

    def finish(rin, out, sems):
        p = plan(rin, out, sems)
        for cp in p['from_sibling']():
            cp.wait_recv()
        for cp in p['own']() + p['to_sibling'](0) + p['to_sibling'](1) + p['to_sibling'](2):
            cp.wait_send()

        @pl.when(p['north'])
        def _():
            for cp in p['x_block_on_to_y']():
                cp.wait_send()

        @pl.when(jnp.logical_not(p['north']))
        def _():
            for cp in p['y_block_on_to_x']():
                cp.wait_send()
        for cp in p['local']():
            cp.wait()

    out_shape = [jax.ShapeDtypeStruct(_full_shape(k, m, s.shape), s.dtype) for s, (k, m) in zip(shards, kinds)]
    scratch = [pltpu.SemaphoreType.DMA((n, 7)), pltpu.SemaphoreType.DMA((n, 7)), pltpu.SemaphoreType.DMA((n,))]
    return _Ride(shards, out_shape, scratch, start, finish, mids=((2, pass_diagonal), (3, pass_to_sibling)))


def _scatter_ride(grads, kinds):
    n = len(grads)

    def plan(g, out, sems):
        send_sems, recv_sems, local_sems = sems
        x, y, c = _my_coords()
        me = _dev_index(x, y, c)

        def local():
            return [pltpu.make_async_copy(_window(g[i], kinds[i][0], kinds[i][1], me), out[i].at[me], local_sems.at[i])
                    for i in range(n)]

        def remote(arrival):
            cps = []
            for rel in range(1, NDEV):
                peer = _peer(x, y, c, rel)
                dev = _dev_index(*peer)
                for i in range(n):
                    kind, m = kinds[i]
                    cps.append(pltpu.make_async_remote_copy(
                        src_ref=_window(g[i], kind, m, me if arrival else dev), dst_ref=out[i].at[dev if arrival else me],
                        send_sem=send_sems.at[i, rel - 1], recv_sem=recv_sems.at[i, rel - 1], device_id=peer, device_id_type=MESH))
            return cps

        return local, remote

    def start(g, out, sems):
        local, remote = plan(g, out, sems)
        for cp in local() + remote(False):
            cp.start()

    def finish(g, out, sems):
        local, remote = plan(g, out, sems)
        for cp in remote(True):
            cp.wait_recv()
        for cp in remote(False):
            cp.wait_send()
        for cp in local():
            cp.wait()

    def block_shape(gr, kind, m):
        return (m, gr.shape[1]) if kind == 'row' else (gr.shape[0], m)

    out_shape = [jax.ShapeDtypeStruct((NDEV,) + block_shape(gr, k, m), gr.dtype) for gr, (k, m) in zip(grads, kinds)]
    scratch = [pltpu.SemaphoreType.DMA((n, NDEV - 1)), pltpu.SemaphoreType.DMA((n, NDEV - 1)), pltpu.SemaphoreType.DMA((n,))]
    return _Ride(grads, out_shape, scratch, start, finish)


def _rms_stats(xf):
    r = lax.rsqrt(jnp.mean(xf * xf, axis=-1, keepdims=True) + EPS)
    return xf * r, r


def _rms_bwd(dh, g, xhat, r):
    dxhat = dh * g
    return r * (dxhat - xhat * jnp.mean(dxhat * xhat, axis=-1, keepdims=True))


def _ffn_blocks():
    edges = (0,) + FF_BLOCK_EDGES + (FF,)
    return [(slice(lo, hi), slice(FF + lo, FF + hi)) for lo, hi in zip(edges[:-1], edges[1:])]


def _swiglu_tile(x_ref, g_ref, wut_ref, wd_ref, ab_ref):
    xf = x_ref[...]
    xhat, _ = _rms_stats(xf)
    h = (xhat * g_ref[...]).astype(BF16)
    acc = jnp.zeros(xf.shape, F32)
    for ra, rb in _ffn_blocks():
        a = _dot_nt(h, wut_ref[ra, :])
        b = _dot_nt(h, wut_ref[rb, :])
        ab_ref[:, ra] = a.astype(BF16)
        ab_ref[:, rb] = b.astype(BF16)
        act = (a * _sigmoid(a) * b).astype(BF16)
        acc = acc + _dot(act, wd_ref[ra, :])
    return xf + 0.5 * acc


def _ffn_fwd(x, g, wut, wd, tm, name, ride=None):
    T = x.shape[0]

    def body(x_ref, g_ref, wut_ref, wd_ref, xo_ref, ab_ref):
        xo_ref[...] = _swiglu_tile(x_ref, g_ref, wut_ref, wd_ref, ab_ref)

    return _pallas(
        body, name, (T // tm,),
        [pl.BlockSpec((tm, D), lambda t: (t, 0)), _const((1, D)), _const((2 * FF, D)), _const((FF, D))],
        [pl.BlockSpec((tm, D), lambda t: (t, 0)), pl.BlockSpec((tm, 2 * FF), lambda t: (t, 0))],
        [jax.ShapeDtypeStruct((T, D), F32), jax.ShapeDtypeStruct((T, 2 * FF), BF16)],
        (x, g, wut, wd), sem=("arbitrary",), ride=ride)


def _ffn_fwd_loss(x, g, wut, wd, g_final, target, tm, name):
    T = x.shape[0]

    def body(x_ref, g_ref, wut_ref, wd_ref, gf_ref, t_ref, dx_ref, ab_ref, loss_ref, dgf_ref):
        xhat, r = _rms_stats(_swiglu_tile(x_ref, g_ref, wut_ref, wd_ref, ab_ref))
        gain = gf_ref[...]
        diff = xhat * gain - t_ref[...]
        dout = diff * (1.0 / D)

        @pl.when(pl.program_id(0) == 0)
        def _():
            loss_ref[...] = jnp.zeros_like(loss_ref)
            dgf_ref[...] = jnp.zeros_like(dgf_ref)
        sq = jnp.sum(jnp.sum(diff * diff, axis=0, keepdims=True), axis=1, keepdims=True)
        loss_ref[...] += jnp.broadcast_to(sq * (0.5 / D), (1, 128))
        dgf_ref[...] += jnp.sum(dout * xhat, axis=0, keepdims=True)
        dx_ref[...] = _rms_bwd(dout, gain, xhat, r)

    row = pl.BlockSpec((tm, D), lambda t: (t, 0))
    return pl.pallas_call(
        body, grid=(T // tm,), name=name,
        in_specs=[row, _const((1, D)), _const((2 * FF, D)), _const((FF, D)), _const((1, D)), row],
        out_specs=[row, pl.BlockSpec((tm, 2 * FF), lambda t: (t, 0)), pl.BlockSpec((1, 128), lambda t: (0, 0)),
                   pl.BlockSpec((1, D), lambda t: (0, 0))],
        out_shape=[jax.ShapeDtypeStruct((T, D), F32), jax.ShapeDtypeStruct((T, 2 * FF), BF16),
                   jax.ShapeDtypeStruct((1, 128), F32), jax.ShapeDtypeStruct((1, D), F32)],
        compiler_params=_params("arbitrary"),
    )(x, g, wut, wd, g_final, target)


def _ffn_bwd(x, dy, ab, g, wut, wd, tm, name):
    T = x.shape[0]

    def body(x_ref, dy_ref, ab_ref, g_ref, wut_ref, wd_ref, dx_ref, dab_ref, act_ref, h_ref, dg_ref):
        xf = x_ref[...]
        xhat, r = _rms_stats(xf)
        gain = g_ref[...]
        h_ref[...] = (xhat * gain).astype(BF16)
        dy = dy_ref[...]
        dyh = (0.5 * dy).astype(BF16)
        dh = jnp.zeros((tm, D), F32)
        for ra, rb in _ffn_blocks():
            a = ab_ref[:, ra].astype(F32)
            b = ab_ref[:, rb].astype(F32)
            dact = _dot_nt(dyh, wd_ref[ra, :])
            sg = _sigmoid(a)
            sl = a * sg
            act_ref[:, ra] = (sl * b).astype(BF16)
            da = (dact * b * (sg * (1.0 + a * (1.0 - sg)))).astype(BF16)
            db = (dact * sl).astype(BF16)
            dab_ref[:, ra] = da
            dab_ref[:, rb] = db
            dh = dh + _dot(da, wut_ref[ra, :]) + _dot(db, wut_ref[rb, :])
        dx_ref[...] = dy + _rms_bwd(dh, gain, xhat, r)

        @pl.when(pl.program_id(0) == 0)
        def _():
            dg_ref[...] = jnp.zeros_like(dg_ref)
        dg_ref[...] += jnp.sum(dh * xhat, axis=0, keepdims=True)

    return pl.pallas_call(
        body, grid=(T // tm,), name=name,
        in_specs=[pl.BlockSpec((tm, D), lambda t: (t, 0)), pl.BlockSpec((tm, D), lambda t: (t, 0)),
                  pl.BlockSpec((tm, 2 * FF), lambda t: (t, 0)), _const((1, D)), _const((2 * FF, D)), _const((FF, D))],
        out_specs=[pl.BlockSpec((tm, D), lambda t: (t, 0)), pl.BlockSpec((tm, 2 * FF), lambda t: (t, 0)),
                   pl.BlockSpec((tm, FF), lambda t: (t, 0)), pl.BlockSpec((tm, D), lambda t: (t, 0)),
                   pl.BlockSpec((1, D), lambda t: (0, 0))],
        out_shape=[jax.ShapeDtypeStruct((T, D), F32), jax.ShapeDtypeStruct((T, 2 * FF), BF16),
                   jax.ShapeDtypeStruct((T, FF), BF16), jax.ShapeDtypeStruct((T, D), BF16), jax.ShapeDtypeStruct((1, D), F32)],
        compiler_params=_params("arbitrary"),
    )(x, dy, ab, g, wut, wd)


def _tn_matmul(xm, ym, tn, name, scale=None, out_cols=None, col_off=0, prev=None, tt=TILE_GRAD_TOKENS, x_part=(0, 1),
               out_rows=None, y_part=(0, 1), ride=None, after=None):
    T = xm.shape[0]
    xi, xn = x_part
    yi, yn = y_part
    K = xm.shape[1] // xn
    N = ym.shape[1] // yn
    out_cols = N if out_cols is None else out_cols
    row_blk = xi if out_rows is not None else 0
    out_rows = K if out_rows is None else out_rows
    tt = min(tt, T)
    nt = T // tt
    off = col_off // tn

    def body(*refs):
        x_ref, y_ref = refs[0], refs[1]
        o_ref, acc = refs[-2], refs[-1]

        @pl.when(pl.program_id(1) == 0)
        def _():
            acc[...] = jnp.zeros_like(acc)
        acc[...] += _dot_tn(x_ref[...].astype(BF16), y_ref[...].astype(BF16))

        @pl.when(pl.program_id(1) == nt - 1)
        def _():
            res = acc[...]
            o_ref[...] = (res if scale is None else res * scale).astype(BF16)

    ycol = yi * (N // tn)
    in_specs = [pl.BlockSpec((tt, K), lambda n, t: (t, xi)), pl.BlockSpec((tt, tn), lambda n, t: (t, n + ycol))]
    args = [xm, ym]
    aliases = {}
    if prev is not None:
        in_specs.append(ANY)
        args.append(prev)
        aliases = {2: 0}
    outs, rode = _pallas(
        body, name, (N // tn, nt), in_specs, [pl.BlockSpec((K, tn), lambda n, t: (row_blk, n + off))],
        [jax.ShapeDtypeStruct((out_rows, out_cols), BF16)], args, scratch_shapes=[pltpu.VMEM((K, tn), F32)],
        sem=("parallel", "arbitrary"), aliases=aliases, ride=ride, after=after)
    return outs[0], rode


def _mix_fwd(x, g, w_in, tm, ride=None):
    T = x.shape[0]

    def body(x_ref, g_ref, w_ref, uc_ref, qkv_ref, mq_ref, gl_ref, h_ref):
        xhat, _ = _rms_stats(x_ref[...])
        h = (xhat * g_ref[...]).astype(BF16)
        h_ref[...] = h
        uc_ref[...] = _dot(h, w_ref[:, 0:1024])
        qkv_ref[...] = _dot(h, w_ref[:, 1024:2560]).astype(BF16)
        mq_ref[...] = _dot(h, w_ref[:, 2560:3072]).astype(BF16)
        for j in range(3):
            gl_ref[:, j * D:(j + 1) * D] = _dot(h, w_ref[:, 3072 + j * D:3072 + (j + 1) * D]).astype(BF16)

    row = lambda w: pl.BlockSpec((tm, w), lambda t: (t, 0))
    return _pallas(
        body, "mix_fwd", (T // tm,), [row(D), _const((1, D)), _const((D, IN_COLS))],
        [row(1024), row(1536), row(512), row(3072), row(D)],
        [jax.ShapeDtypeStruct((T, 1024), F32), jax.ShapeDtypeStruct((T, 1536), BF16), jax.ShapeDtypeStruct((T, 512), BF16),
         jax.ShapeDtypeStruct((T, 3072), BF16), jax.ShapeDtypeStruct((T, D), BF16)],
        (x, g, w_in), sem=("parallel",), ride=ride)


def _mix_bwd(x, dres, duc, dqkv, dmq, dgl, g, w_in, tm, ride=None, after=None):
    T = x.shape[0]

    def body(x_ref, dres_ref, duc_ref, dqkv_ref, dmq_ref, dgl_ref, g_ref, w_ref, dx_ref, dg_ref):
        xhat, r = _rms_stats(x_ref[...])
        dh = _dot_nt(duc_ref[...], w_ref[:, 0:1024])
        dh = dh + _dot_nt(dqkv_ref[...], w_ref[:, 1024:2560])
        dh = dh + _dot_nt(dmq_ref[...], w_ref[:, 2560:3072])
        dh = dh + _dot_nt(dgl_ref[...], w_ref[:, 3072:6144])
        dx_ref[...] = dres_ref[...] + _rms_bwd(dh, g_ref[...], xhat, r)

        @pl.when(pl.program_id(0) == 0)
        def _():
            dg_ref[...] = jnp.zeros_like(dg_ref)
        dg_ref[...] += jnp.sum(dh * xhat, axis=0, keepdims=True)

    row = lambda w: pl.BlockSpec((tm, w), lambda t: (t, 0))
    return _pallas(
        body, "mix_bwd", (T // tm,),
        [row(D), row(D), row(1024), row(1536), row(512), row(3072), _const((1, D)), _const((D, IN_COLS))],
        [row(D), pl.BlockSpec((1, D), lambda t: (0, 0))],
        [jax.ShapeDtypeStruct((T, D), F32), jax.ShapeDtypeStruct((1, D), F32)],
        (x, dres, duc, dqkv, dmq, dgl, g, w_in), sem=("arbitrary",), ride=ride, after=after)


def _shifted(win, base, copies):
    for k in range(8):
        copies[k] = win[base + k:base + k + CONV_CHUNK + 24]
    return copies


def _tap_slices(copies, tap):
    out = []
    for k in range(8):
        for a in range(4):
            j = tap(a, k)
            if 0 <= j < CONV_K:
                out.append((j, copies[k, pl.ds(8 * a, CONV_CHUNK), :]))
    return out


def _conv_taps(copies, w_ref, tap):
    acc = jnp.zeros((CONV_CHUNK, CONV_W), F32)
    for j, rows in _tap_slices(copies, tap):
        acc = acc + rows * w_ref[j:j + 1, :]
    return acc


def _fold8(v):
    acc = v[0:8]
    for r in range(8, CONV_CHUNK, 8):
        acc = acc + v[r:r + 8]
    return acc


def _glu_into(uc_ref, vpad, S):
    vpad[pl.ds(0, CONV_HALO), :] = jnp.zeros((CONV_HALO, CONV_W), F32)
    vpad[pl.ds(S + CONV_HALO, CONV_HALO), :] = jnp.zeros((CONV_HALO, CONV_W), F32)

    def glu(i, carry):
        r0 = pl.multiple_of(i * GLU_CHUNK, GLU_CHUNK)
        a = uc_ref[0, pl.ds(r0, GLU_CHUNK), 0:CONV_W]
        gt = uc_ref[0, pl.ds(r0, GLU_CHUNK), CONV_W:2 * CONV_W]
        vpad[pl.ds(pl.multiple_of(r0 + CONV_HALO, CONV_HALO), GLU_CHUNK), :] = a * _sigmoid(gt)
        return carry
    lax.fori_loop(0, S // GLU_CHUNK, glu, 0)


def _layer_norm(z, vec_ref):
    xc = z - jnp.mean(z, axis=-1, keepdims=True)
    rstd = lax.rsqrt(jnp.mean(xc * xc, axis=-1, keepdims=True) + EPS)
    xn = xc * rstd
    return xn, rstd, xn * vec_ref[1:2, :] + vec_ref[2:3, :]


def _conv_fwd(uc, dw_w, vec):
    NB, S, _ = uc.shape

    def body(uc_ref, w_ref, vec_ref, o_ref, z_ref, vpad, copies):
        _glu_into(uc_ref, vpad, S)

        def conv(i, carry):
            r0 = pl.multiple_of(i * CONV_CHUNK, CONV_CHUNK)
            win = vpad[pl.ds(r0, CONV_WIN), :]
            z = _conv_taps(_shifted(win, CONV_HALO - (CONV_K - 1), copies), w_ref, lambda a, k: 8 * a + k) + vec_ref[0:1, :]
            z_ref[0, pl.ds(r0, CONV_CHUNK), :] = z
            _, _, yln = _layer_norm(z, vec_ref)
            o_ref[0, pl.ds(r0, CONV_CHUNK), :] = (yln * _sigmoid(yln)).astype(BF16)
            return carry
        lax.fori_loop(0, S // CONV_CHUNK, conv, 0, unroll=8)

    seq = pl.BlockSpec((1, S, CONV_W), lambda b: (b, 0, 0))
    return pl.pallas_call(
        body, grid=(NB,), name="conv_fwd",
        in_specs=[pl.BlockSpec((1, S, 2 * CONV_W), lambda b: (b, 0, 0)), _const((CONV_K, CONV_W)), _const((8, CONV_W))],
        out_specs=[seq, seq],
        out_shape=[jax.ShapeDtypeStruct((NB, S, CONV_W), BF16), jax.ShapeDtypeStruct((NB, S, CONV_W), F32)],
        scratch_shapes=[pltpu.VMEM((S + 2 * CONV_HALO, CONV_W), F32), pltpu.VMEM((8, CONV_CHUNK + 24, CONV_W), F32)],
        compiler_params=_params("parallel"),
    )(uc, dw_w, vec)


def _conv_bwd(uc, z, dcact, dw_w, vec, ride=None):
    NB, S, _ = uc.shape
    n_chunks = S // CONV_CHUNK

    def body(uc_ref, z_ref, dc_ref, w_ref, vec_ref, duc_ref, dw_ref, dvec_ref, vpad, dzpad, dw8, dvec8, copies):
        @pl.when(pl.program_id(0) == 0)
        def _():
            dw8[...] = jnp.zeros_like(dw8)
            dvec8[...] = jnp.zeros_like(dvec8)
        _glu_into(uc_ref, vpad, S)
        dzpad[pl.ds(S, 2 * CONV_HALO), :] = jnp.zeros((2 * CONV_HALO, CONV_W), F32)

        def norm_bwd(i, carry):
            r0 = pl.multiple_of(i * CONV_CHUNK, CONV_CHUNK)
            xn, rstd, yln = _layer_norm(z_ref[0, pl.ds(r0, CONV_CHUNK), :], vec_ref)
            sg = _sigmoid(yln)
            dyln = dc_ref[0, pl.ds(r0, CONV_CHUNK), :] * (sg * (1.0 + yln * (1.0 - sg)))
            dxn = dyln * vec_ref[1:2, :]
            dz = rstd * (dxn - jnp.mean(dxn, axis=-1, keepdims=True) - xn * jnp.mean(dxn * xn, axis=-1, keepdims=True))
            dzpad[pl.ds(r0, CONV_CHUNK), :] = dz
            dvec8[0] += _fold8(dz)
            dvec8[1] += _fold8(dyln * xn)
            dvec8[2] += _fold8(dyln)
            return carry
        lax.fori_loop(0, n_chunks, norm_bwd, 0, unroll=8)

        def taps_bwd(i, carry):
            r0 = pl.multiple_of(i * CONV_CHUNK, CONV_CHUNK)
            dzwin = dzpad[pl.ds(r0, CONV_WIN), :]
            dv = _conv_taps(_shifted(dzwin, 0, copies), w_ref, lambda a, k: CONV_K - 1 - 8 * a - k)
            dz = dzwin[0:CONV_CHUNK]
            vwin = vpad[pl.ds(r0, CONV_WIN), :]
            for j, rows in _tap_slices(_shifted(vwin, CONV_HALO - (CONV_K - 1), copies), lambda a, k: 8 * a + k):
                dw8[j] += _fold8(dz * rows)
            a = uc_ref[0, pl.ds(r0, CONV_CHUNK), 0:CONV_W]
            sg = _sigmoid(uc_ref[0, pl.ds(r0, CONV_CHUNK), CONV_W:2 * CONV_W])
            duc_ref[0, pl.ds(r0, CONV_CHUNK), 0:CONV_W] = (dv * sg).astype(BF16)
            duc_ref[0, pl.ds(r0, CONV_CHUNK), CONV_W:2 * CONV_W] = (dv * a * sg * (1.0 - sg)).astype(BF16)
            return carry
        lax.fori_loop(0, n_chunks, taps_bwd, 0, unroll=2)

        @pl.when(pl.program_id(0) == NB - 1)
        def _():
            dw_ref[...] = jnp.zeros_like(dw_ref)
            dvec_ref[...] = jnp.zeros_like(dvec_ref)
            for j in range(CONV_K):
                dw_ref[j:j + 1, :] = jnp.sum(dw8[j], axis=0, keepdims=True)
            for j in range(3):
                dvec_ref[j:j + 1, :] = jnp.sum(dvec8[j], axis=0, keepdims=True)

    return _pallas(
        body, "conv_bwd", (NB,),
        [pl.BlockSpec((1, S, 2 * CONV_W), lambda b: (b, 0, 0)), pl.BlockSpec((1, S, CONV_W), lambda b: (b, 0, 0)),
         pl.BlockSpec((1, S, CONV_W), lambda b: (b, 0, 0)), _const((CONV_K, CONV_W)), _const((8, CONV_W))],
        [pl.BlockSpec((1, S, 2 * CONV_W), lambda b: (b, 0, 0)), pl.BlockSpec((32, CONV_W), lambda b: (0, 0)),
         pl.BlockSpec((8, CONV_W), lambda b: (0, 0))],
        [jax.ShapeDtypeStruct((NB, S, 2 * CONV_W), BF16), jax.ShapeDtypeStruct((32, CONV_W), F32),
         jax.ShapeDtypeStruct((8, CONV_W), F32)],
        (uc, z, dcact, dw_w, vec),
        scratch_shapes=[pltpu.VMEM((S + 2 * CONV_HALO, CONV_W), F32), pltpu.VMEM((S + 2 * CONV_HALO, CONV_W), F32),
                        pltpu.VMEM((CONV_K, 8, CONV_W), F32), pltpu.VMEM((3, 8, CONV_W), F32),
                        pltpu.VMEM((8, CONV_CHUNK + 24, CONV_W), F32)],
        sem=("arbitrary",), ride=ride)


def _rel_index_of_column(cols):
    offset = jnp.where(cols < KWIN, cols, cols - DS_LANES)
    return jnp.clip(KPAD - offset, -(CHUNK - 1), MAX_REL) + (CHUNK - 1)


def _bias_table(rel_bias, ride=None):
    def body(rb_ref, o_ref, by_offset, first8):
        ridx = _rel_index_of_column(lax.broadcasted_iota(jnp.int32, (1, DS_LANES), 1))
        onehot = (ridx == lax.broadcasted_iota(jnp.int32, (N_REL, 1), 0)).astype(F32)
        by_offset[...] = jnp.dot(rb_ref[...], onehot, preferred_element_type=F32, precision=lax.Precision.HIGHEST)
        sub = lax.broadcasted_iota(jnp.int32, (8, 1), 0)
        kchunk = lax.broadcasted_iota(jnp.int32, (1, KWIN), 1) // CHUNK
        for head in range(ATT_HEADS):
            base = jnp.broadcast_to(by_offset[head:head + 1, :], (8, DS_LANES))
            rows = base
            for s in range(1, 8):
                rows = jnp.where(sub == s, pltpu.roll(base, s, 1), rows)
            first8[head] = rows

        def rows8(q8, carry):
            qchunk = (q8 * 8 + sub) // CHUNK
            band = (kchunk >= qchunk) & (kchunk <= qchunk + LEFT_CHUNKS)
            for head in range(ATT_HEADS):
                tile = pltpu.roll(first8[head], q8 * 8, 1)[:, 0:KWIN]
                o_ref[head, pl.ds(pl.multiple_of(q8 * 8, 8), 8), :] = jnp.where(band, tile, MASK_VALUE)
            return carry
        lax.fori_loop(0, QB // 8, rows8, 0)

    outs, rode = _pallas(
        body, "bias_table", (1,), [pl.BlockSpec((ATT_HEADS, N_REL), lambda i: (0, 0))],
        [pl.BlockSpec((ATT_HEADS, QB, KWIN), lambda i: (0, 0, 0))], [jax.ShapeDtypeStruct((ATT_HEADS, QB, KWIN), F32)], (rel_bias,),
        scratch_shapes=[pltpu.VMEM((ATT_HEADS, DS_LANES), F32), pltpu.VMEM((ATT_HEADS, 8, DS_LANES), F32)],
        sem=("arbitrary",), ride=ride)
    return outs[0], rode


def _load_keys(i, k_ref, v_ref, kpad, vpad, S):
    @pl.when(i == 0)
    def _():
        kpad[pl.ds(0, KPAD), :] = jnp.zeros((KPAD, ATT_W), BF16)
        vpad[pl.ds(0, KPAD), :] = jnp.zeros((KPAD, ATT_W), BF16)
        kpad[pl.ds(KPAD, S), :] = k_ref[0]
        vpad[pl.ds(KPAD, S), :] = v_ref[0]


def _att_scores(q2s, k2, tab_ref, head, in_head, in_seq):
    qm = jnp.where(in_head, q2s, jnp.zeros_like(q2s))
    s = _dot_nt(qm, k2) + tab_ref[head]
    return s if in_seq is None else jnp.where(in_seq, s, MASK_VALUE)


def _by_window(i, step):
    in_seq = (lax.broadcasted_iota(jnp.int32, (1, KWIN), 1) + i * QB) >= KPAD
    pl.when(i < KPAD // QB)(lambda: step(in_seq))
    pl.when(i >= KPAD // QB)(lambda: step(None))


def _scaled(q2):
    return q2 * jnp.asarray(ATT_HD ** -0.5, q2.dtype)


def _att_fwd(qkv, tab, ride=None):
    NB, S, _ = qkv.shape

    def body(q_ref, k_ref, v_ref, tab_ref, o_ref, lse_ref, kpad, vpad):
        i = pl.program_id(1)
        _load_keys(i, k_ref, v_ref, kpad, vpad, S)
        koff = pl.multiple_of(i * QB, QB)
        lane = lax.broadcasted_iota(jnp.int32, (1, 128), 1)

        def step(in_seq):
            lse = jnp.zeros((QB, 128), F32)
            for pair in range(ATT_HEADS // 2):
                cols = slice(pair * 128, (pair + 1) * 128)
                q2s = _scaled(q_ref[0, :, cols])
                k2 = kpad[pl.ds(koff, KWIN), cols]
                v2 = vpad[pl.ds(koff, KWIN), cols]
                o2 = jnp.zeros((QB, 128), F32)
                for hh in range(2):
                    head = 2 * pair + hh
                    in_head = (lane // ATT_HD) == hh
                    s = _att_scores(q2s, k2, tab_ref, head, in_head, in_seq)
                    m = jnp.max(s, axis=-1, keepdims=True)
                    e = jnp.exp(s - m)
                    l = jnp.sum(e, axis=-1, keepdims=True)
                    o2 = jnp.where(in_head, _dot(e.astype(BF16), v2) * (1.0 / l), o2)
                    lse = jnp.where(lane == head, m + jnp.log(l), lse)
                o_ref[0, :, cols] = o2.astype(BF16)
            lse_ref[0] = lse
        _by_window(i, step)

    seq = lambda col: pl.BlockSpec((1, S, ATT_W), lambda b, i: (b, 0, col), pipeline_mode=pl.Buffered(1))
    outs, rode = _pallas(
        body, "att_fwd", (NB, S // QB),
        [pl.BlockSpec((1, QB, ATT_W), lambda b, i: (b, i, 0)), seq(1), seq(2), _const((ATT_HEADS, QB, KWIN))],
        [pl.BlockSpec((1, QB, ATT_W), lambda b, i: (b, i, 0)), pl.BlockSpec((1, QB, 128), lambda b, i: (b, i, 0))],
        [jax.ShapeDtypeStruct((NB, S, ATT_W), BF16), jax.ShapeDtypeStruct((NB, S, 128), F32)],
        (qkv, qkv, qkv, tab),
        scratch_shapes=[pltpu.VMEM((S + KPAD, ATT_W), BF16), pltpu.VMEM((S + KPAD, ATT_W), BF16)],
        sem=("arbitrary", "arbitrary"), ride=ride)
    return outs[0], outs[1], rode


def _att_bwd(qkv, o, lse, do, tab, ride=None):
    NB, S, _ = qkv.shape
    nq = S // QB

    def body(q_ref, k_ref, v_ref, o_ref, lse_ref, do_ref, tab_ref, dqkv_ref, ds_hbm, kpad, vpad, dkpad, dvpad, ds_acc, ds_sem):
        b, i = pl.program_id(0), pl.program_id(1)
        _load_keys(i, k_ref, v_ref, kpad, vpad, S)

        @pl.when(i == 0)
        def _():
            dkpad[...] = jnp.zeros_like(dkpad)
            dvpad[...] = jnp.zeros_like(dvpad)

        @pl.when((i == 0) & (b == 0))
        def _():
            ds_acc[...] = jnp.zeros_like(ds_acc)

        koff = pl.multiple_of(i * QB, QB)
        lane = lax.broadcasted_iota(jnp.int32, (1, 128), 1)

        def step(in_seq):
            for pair in range(ATT_HEADS // 2):
                cols = slice(pair * 128, (pair + 1) * 128)
                q2s = _scaled(q_ref[0, :, cols])
                do2 = do_ref[0, :, cols]
                k2 = kpad[pl.ds(koff, KWIN), cols]
                v2 = vpad[pl.ds(koff, KWIN), cols]
                do_o = do2.astype(F32) * o_ref[0, :, cols].astype(F32)
                dq2 = jnp.zeros((QB, 128), F32)
                dk2 = jnp.zeros((KWIN, 128), F32)
                dv2 = jnp.zeros((KWIN, 128), F32)
                for hh in range(2):
                    head = 2 * pair + hh
                    in_head = (lane // ATT_HD) == hh
                    p = jnp.exp(_att_scores(q2s, k2, tab_ref, head, in_head, in_seq) - lse_ref[0, :, head:head + 1])
                    row_term = jnp.sum(jnp.where(in_head, do_o, 0.0), axis=-1, keepdims=True)
                    dom = jnp.where(in_head, do2, jnp.zeros_like(do2))
                    ds = p * (_dot_nt(dom, v2) - row_term)
                    ds_acc[head] += ds
                    dsb = ds.astype(BF16)
                    dq2 = jnp.where(in_head, _dot(dsb, k2), dq2)
                    dk2 = jnp.where(in_head, _dot_tn(dsb, q2s), dk2)
                    dv2 = jnp.where(in_head, _dot_tn(p.astype(BF16), do2), dv2)
                dqkv_ref[0, pl.ds(koff, QB), cols] = (dq2 * (ATT_HD ** -0.5)).astype(BF16)
                dkpad[pl.ds(koff, KWIN), cols] += dk2
                dvpad[pl.ds(koff, KWIN), cols] += dv2
        _by_window(i, step)

        @pl.when(i == nq - 1)
        def _():
            dqkv_ref[0, :, ATT_W:2 * ATT_W] = dkpad[pl.ds(KPAD, S), :].astype(BF16)
            dqkv_ref[0, :, 2 * ATT_W:3 * ATT_W] = dvpad[pl.ds(KPAD, S), :].astype(BF16)

        @pl.when((i == nq - 1) & (b == NB - 1))
        def _():
            out = pltpu.make_async_copy(ds_acc, ds_hbm, ds_sem)
            out.start()
            out.wait()

    seq = lambda col: pl.BlockSpec((1, S, ATT_W), lambda b, i: (b, 0, col), pipeline_mode=pl.Buffered(1))
    rows = pl.BlockSpec((1, QB, ATT_W), lambda b, i: (b, i, 0))
    return _pallas(
        body, "att_bwd", (NB, nq),
        [rows, seq(1), seq(2), rows, pl.BlockSpec((1, QB, 128), lambda b, i: (b, i, 0)), rows, _const((ATT_HEADS, QB, KWIN))],
        [pl.BlockSpec((1, S, 3 * ATT_W), lambda b, i: (b, 0, 0)), ANY],
        [jax.ShapeDtypeStruct((NB, S, 3 * ATT_W), BF16), jax.ShapeDtypeStruct((ATT_HEADS, QB, KWIN), F32)],
        (qkv, qkv, qkv, o, lse, do, tab),
        scratch_shapes=[pltpu.VMEM((S + KPAD, ATT_W), BF16), pltpu.VMEM((S + KPAD, ATT_W), BF16),
                        pltpu.VMEM((S + KPAD, ATT_W), F32), pltpu.VMEM((S + KPAD, ATT_W), F32),
                        pltpu.VMEM((ATT_HEADS, QB, KWIN), F32), pltpu.SemaphoreType.DMA],
        sem=("arbitrary", "arbitrary"), ride=ride)


def _rel_bias_grad(ds):
    def body(ds_ref, o_ref):
        sub = lax.broadcasted_iota(jnp.int32, (8, 1), 0)
        ridx = _rel_index_of_column(lax.broadcasted_iota(jnp.int32, (DS_LANES, 1), 0))
        onehot = (ridx == lax.broadcasted_iota(jnp.int32, (1, N_REL), 1)).astype(F32)
        def rows8(q8, accs):
            shift = lax.rem(DS_LANES - q8 * 8, DS_LANES)
            out = []
            for head in range(ATT_HEADS):
                tile = ds_ref[head, pl.ds(pl.multiple_of(q8 * 8, 8), 8), :]
                tile = jnp.concatenate([tile, jnp.zeros((8, DS_LANES - KWIN), F32)], axis=1)
                out.append(accs[head] + pltpu.roll(tile, shift, 1))
            return tuple(out)
        accs = lax.fori_loop(0, QB // 8, rows8, tuple(jnp.zeros((8, DS_LANES), F32) for _ in range(ATT_HEADS)))
        for head in range(ATT_HEADS):
            acc = accs[head]
            diag = jnp.zeros((8, DS_LANES), F32)
            for s in range(8):
                shifted = acc if s == 0 else pltpu.roll(acc, DS_LANES - s, 1)
                diag = jnp.where(sub == s, shifted, diag)
            z = jnp.sum(diag, axis=0, keepdims=True)
            o_ref[head:head + 1, :] = jnp.dot(z, onehot, preferred_element_type=F32, precision=lax.Precision.HIGHEST)

    return pl.pallas_call(body, out_shape=jax.ShapeDtypeStruct((ATT_HEADS, N_REL), F32), name="rel_bias_grad",
                          compiler_params=_params())(ds)


def _memkv_fwd(mem, g, w_kv, tm):
    R = mem.shape[0]
    tm = min(tm, R)

    def body(m_ref, g_ref, w_ref, h_ref, kv_ref):
        xhat, _ = _rms_stats(m_ref[...])
        h = (xhat * g_ref[...]).astype(BF16)
        h_ref[...] = h
        kv_ref[...] = _dot(h, w_ref[...]).astype(BF16)

    row = pl.BlockSpec((tm, D), lambda t: (t, 0))
    return pl.pallas_call(
        body, grid=(R // tm,), name="memkv_fwd", in_specs=[row, _const((1, D)), _const((D, 2 * MEM_W))], out_specs=[row, row],
        out_shape=[jax.ShapeDtypeStruct((R, D), BF16), jax.ShapeDtypeStruct((R, 2 * MEM_W), BF16)],
        compiler_params=_params("parallel"),
    )(mem, g, w_kv)


def _memkv_bwd(mem, dkv, w_kv, tm):
    R = mem.shape[0]
    tm = min(tm, R)

    def body(m_ref, dkv_ref, w_ref, dg_ref):
        xhat, _ = _rms_stats(m_ref[...])
        dh = _dot_nt(dkv_ref[...].astype(BF16), w_ref[...])

        @pl.when(pl.program_id(0) == 0)
        def _():
            dg_ref[...] = jnp.zeros_like(dg_ref)
        dg_ref[...] += jnp.sum(dh * xhat, axis=0, keepdims=True)

    row = pl.BlockSpec((tm, D), lambda t: (t, 0))
    return pl.pallas_call(
        body, grid=(R // tm,), name="memkv_bwd", in_specs=[row, row, _const((D, 2 * MEM_W))],
        out_specs=pl.BlockSpec((1, D), lambda t: (0, 0)), out_shape=jax.ShapeDtypeStruct((1, D), F32),
        compiler_params=_params("arbitrary"),
    )(mem, dkv, w_kv)


def _mem_exp(qh, kh):
    s = _dot_nt(qh, kh) * (MEM_HD ** -0.5)
    e = jnp.exp(s - jnp.max(s, axis=-1, keepdims=True))
    return e, jnp.sum(e, axis=-1, keepdims=True)


def _mematt_fwd(mq, kv, tq):
    NB, S, _ = mq.shape
    M = kv.shape[1]

    def body(q_ref, kv_ref, o_ref):
        for h in range(MEM_HEADS):
            cols = slice(h * MEM_HD, (h + 1) * MEM_HD)
            e, l = _mem_exp(q_ref[0, :, cols], kv_ref[0, :, cols])
            o = _dot(e.astype(BF16), kv_ref[0, :, MEM_W + h * MEM_HD:MEM_W + (h + 1) * MEM_HD]) * (1.0 / l)
            o_ref[0, :, cols] = o.astype(BF16)

    return pl.pallas_call(
        body, grid=(NB, S // tq), name="mematt_fwd",
        in_specs=[pl.BlockSpec((1, tq, MEM_W), lambda b, i: (b, i, 0)), pl.BlockSpec((1, M, 2 * MEM_W), lambda b, i: (b, 0, 0))],
        out_specs=pl.BlockSpec((1, tq, MEM_W), lambda b, i: (b, i, 0)),
        out_shape=jax.ShapeDtypeStruct((NB, S, MEM_W), BF16), compiler_params=_params("parallel", "parallel"),
    )(mq, kv)


def _mematt_bwd(mq, kv, do, tq):
    NB, S, _ = mq.shape
    M = kv.shape[1]

    def body(q_ref, kv_ref, do_ref, dq_ref, dkv_ref):
        @pl.when(pl.program_id(1) == 0)
        def _():
            dkv_ref[...] = jnp.zeros_like(dkv_ref)
        for h in range(MEM_HEADS):
            cols = slice(h * MEM_HD, (h + 1) * MEM_HD)
            vcols = slice(MEM_W + h * MEM_HD, MEM_W + (h + 1) * MEM_HD)
            qh, kh, vh, doh = q_ref[0, :, cols], kv_ref[0, :, cols], kv_ref[0, :, vcols], do_ref[0, :, cols]
            e, l = _mem_exp(qh, kh)
            p = e * (1.0 / l)
            dp = _dot_nt(doh, vh)
            ds = p * (dp - jnp.sum(p * dp, axis=-1, keepdims=True))
            dss = (ds * (MEM_HD ** -0.5)).astype(BF16)
            dq_ref[0, :, cols] = _dot(dss, kh).astype(BF16)
            dkv_ref[0, :, cols] += _dot_tn(dss, qh)
            dkv_ref[0, :, vcols] += _dot_tn(p.astype(BF16), doh)

    qspec = pl.BlockSpec((1, tq, MEM_W), lambda b, i: (b, i, 0))
    kvspec = pl.BlockSpec((1, M, 2 * MEM_W), lambda b, i: (b, 0, 0))
    return pl.pallas_call(
        body, grid=(NB, S // tq), name="mematt_bwd", in_specs=[qspec, kvspec, qspec], out_specs=[qspec, kvspec],
        out_shape=[jax.ShapeDtypeStruct((NB, S, MEM_W), BF16), jax.ShapeDtypeStruct((NB, M, 2 * MEM_W), F32)],
        compiler_params=_params("arbitrary", "arbitrary"),
    )(mq, kv, do)


def _branch(j, in_ref, w_ref, gl_ref, bg_ref):
    y = _dot(in_ref[...], w_ref[...])
    gate = _sigmoid(gl_ref[:, j * D:(j + 1) * D].astype(F32) + bg_ref[:, j * D:(j + 1) * D])
    return y, gate


def _combine_fwd(x, cact, oatt, omem, gl, bg, wpw, wo, wmo, wout, tm):
    T = x.shape[0]

    def body(x_ref, c_ref, a_ref, m_ref, gl_ref, bg_ref, wpw_ref, wo_ref, wmo_ref, wout_ref, xo_ref, y_ref):
        y = None
        for j, (in_ref, w_ref) in enumerate(((c_ref, wpw_ref), (a_ref, wo_ref), (m_ref, wmo_ref))):
            yj, gate = _branch(j, in_ref, w_ref, gl_ref, bg_ref)
            y = gate * yj if y is None else y + gate * yj
        y = y.astype(BF16)
        y_ref[...] = y
        xo_ref[...] = x_ref[...] + _dot(y, wout_ref[...])

    row = lambda w: pl.BlockSpec((tm, w), lambda t: (t, 0))
    wbr = _const((512, D))
    return pl.pallas_call(
        body, grid=(T // tm,), name="combine_fwd",
        in_specs=[row(D), row(512), row(512), row(512), row(3 * D), _const((1, 3 * D)), wbr, wbr, wbr, _const((D, D))],
        out_specs=[row(D), row(D)],
        out_shape=[jax.ShapeDtypeStruct((T, D), F32), jax.ShapeDtypeStruct((T, D), BF16)],
        compiler_params=_params("parallel"),
    )(x, cact, oatt, omem, gl, bg, wpw, wo, wmo, wout)


def _combine_bwd(dx, cact, oatt, omem, gl, bg, wpw, wo, wmo, wout, tm, ride=None):
    T = dx.shape[0]

    def body(dx_ref, c_ref, a_ref, m_ref, gl_ref, bg_ref, wpw_ref, wo_ref, wmo_ref, wout_ref,
             dgl_ref, dc_ref, da_ref, dm_ref, dyc_ref, dya_ref, dym_ref, dbg_ref):
        dy = _dot_nt(dx_ref[...].astype(BF16), wout_ref[...])

        @pl.when(pl.program_id(0) == 0)
        def _():
            dbg_ref[...] = jnp.zeros_like(dbg_ref)
        branches = ((c_ref, wpw_ref, dyc_ref, dc_ref), (a_ref, wo_ref, dya_ref, da_ref), (m_ref, wmo_ref, dym_ref, dm_ref))
        for j, (in_ref, w_ref, dyb_ref, din_ref) in enumerate(branches):
            yj, gate = _branch(j, in_ref, w_ref, gl_ref, bg_ref)
            dyg = dy * gate
            dlogit = dyg * yj * (1.0 - gate)
            dgl_ref[:, j * D:(j + 1) * D] = dlogit.astype(BF16)
            dbg_ref[:, j * D:(j + 1) * D] += jnp.sum(dlogit, axis=0, keepdims=True)
            dyb = dyg.astype(BF16)
            dyb_ref[...] = dyb
            din_ref[...] = _dot_nt(dyb, w_ref[...]).astype(din_ref.dtype)

    row = lambda w: pl.BlockSpec((tm, w), lambda t: (t, 0))
    wbr = _const((512, D))
    sds = jax.ShapeDtypeStruct
    return _pallas(
        body, "combine_bwd", (T // tm,),
        [row(D), row(512), row(512), row(512), row(3 * D), _const((1, 3 * D)), wbr, wbr, wbr, _const((D, D))],
        [row(3 * D), row(512), row(512), row(512), row(D), row(D), row(D), pl.BlockSpec((1, 3 * D), lambda t: (0, 0))],
        [sds((T, 3 * D), BF16), sds((T, 512), F32), sds((T, 512), BF16), sds((T, 512), BF16),
         sds((T, D), BF16), sds((T, D), BF16), sds((T, D), BF16), sds((1, 3 * D), F32)],
        (dx, cact, oatt, omem, gl, bg, wpw, wo, wmo, wout), sem=("arbitrary",), ride=ride)


def _peer(x, y, c, rel):
    rx, ry, rc = (rel >> 2) & 1, (rel >> 1) & 1, rel & 1
    return ((1 - x) if rx else x, (1 - y) if ry else y, (1 - c) if rc else c)


def _all_sum_small(parts):
    n = len(parts)

    def body(*refs):
        p_refs, o_refs, slots = refs[:n], refs[n:2 * n], refs[2 * n:3 * n]
        send_sems, recv_sems = refs[3 * n:]
        x, y, c = _my_coords()
        me = _dev_index(x, y, c)

        def copy(i, rel, arrival):
            peer = _peer(x, y, c, rel)
            return pltpu.make_async_remote_copy(
                src_ref=p_refs[i], dst_ref=slots[i].at[_dev_index(*peer) if arrival else me],
                send_sem=send_sems.at[i, rel - 1], recv_sem=recv_sems.at[i, rel - 1], device_id=peer, device_id_type=MESH)

        for i in range(n):
            slots[i][me] = p_refs[i][...]
        for rel in range(1, NDEV):
            for i in range(n):
                copy(i, rel, False).start()
        for rel in range(1, NDEV):
            for i in range(n):
                copy(i, rel, True).wait_recv()
        for rel in range(1, NDEV):
            for i in range(n):
                copy(i, rel, False).wait_send()
        for i in range(n):
            total = slots[i][0]
            for d in range(1, NDEV):
                total = total + slots[i][d]
            o_refs[i][...] = total

    vmem = pl.BlockSpec(memory_space=pltpu.VMEM)
    return pl.pallas_call(
        body, out_shape=[jax.ShapeDtypeStruct(p.shape, F32) for p in parts], name="all_sum_small",
        in_specs=[vmem] * n, out_specs=[vmem] * n,
        scratch_shapes=[pltpu.VMEM((NDEV,) + p.shape, F32) for p in parts]
        + [pltpu.SemaphoreType.DMA((n, NDEV - 1)), pltpu.SemaphoreType.DMA((n, NDEV - 1))],
        compiler_params=pltpu.CompilerParams(has_side_effects=True),
    )(*parts)


HBM = pl.BlockSpec(memory_space=pltpu.HBM)
SEM = pl.BlockSpec(memory_space=pltpu.SEMAPHORE)


def _own_block(g, kind, m, tag):
    def body(g_ref, land_ref, staged, sem):
        me = _dev_index(*_my_coords())
        for cp in (pltpu.make_async_copy(_window(g_ref, kind, m, me), staged, sem),
                   pltpu.make_async_copy(staged, land_ref.at[me], sem)):
            cp.start()
            cp.wait()

    block = (m, g.shape[1]) if kind == 'row' else (g.shape[0], m)
    return pl.pallas_call(body, in_specs=[ANY], out_specs=ANY, out_shape=jax.ShapeDtypeStruct((NDEV,) + block, g.dtype),
                          scratch_shapes=[pltpu.VMEM(block, g.dtype), pltpu.SemaphoreType.DMA], name="own_block_" + tag)(g)


def _scatter_start(g, land, kind, m, tag):
    def body(g_ref, land_ref, send_sems, recv_sems, g_thru, land_thru, token):
        x, y, c = _my_coords()
        me = _dev_index(x, y, c)
        for rel in range(1, NDEV):
            peer = _peer(x, y, c, rel)
            pltpu.make_async_remote_copy(src_ref=_window(g_ref, kind, m, _dev_index(*peer)), dst_ref=land_ref.at[me],
                                         send_sem=send_sems.at[rel - 1], recv_sem=recv_sems.at[rel - 1],
                                         device_id=peer, device_id_type=MESH).start()
        token[...] = jnp.zeros_like(token)

    return pl.pallas_call(
        body, name="scatter_start_" + tag,
        out_shape=(pltpu.SemaphoreType.DMA((NDEV - 1,)), pltpu.SemaphoreType.DMA((NDEV - 1,)), pltpu.HBM(g.shape, g.dtype),
                   pltpu.HBM(land.shape, land.dtype), jax.ShapeDtypeStruct((8, 128), F32)),
        in_specs=(HBM, HBM), out_specs=(SEM, SEM, HBM, HBM, pl.BlockSpec(memory_space=pltpu.VMEM)),
        input_output_aliases={0: 2, 1: 3},
        compiler_params=pltpu.CompilerParams(has_side_effects=pltpu.SideEffectType.DATAFLOW_SIDE_EFFECTING),
    )(pltpu.with_memory_space_constraint(g, pltpu.HBM), pltpu.with_memory_space_constraint(land, pltpu.HBM))


def _scatter_wait(send_sems, recv_sems, g_thru, land_thru, after, kind, m, tag):
    n_after = len(after)

    def body(*refs):
        g_ref, land_ref, send_sems, recv_sems = refs[:4]
        x, y, c = _my_coords()
        me = _dev_index(x, y, c)
        for rel in range(1, NDEV):
            peer = _peer(x, y, c, rel)
            dev = _dev_index(*peer)
            cp = pltpu.make_async_remote_copy(src_ref=_window(g_ref, kind, m, me), dst_ref=land_ref.at[dev],
                                              send_sem=send_sems.at[rel - 1], recv_sem=recv_sems.at[rel - 1],
                                              device_id=peer, device_id_type=MESH)
            cp.wait_send()
            cp.wait_recv()

    return pl.pallas_call(
        body, name="scatter_wait_" + tag,
        out_shape=(pltpu.HBM(g_thru.shape, g_thru.dtype), pltpu.HBM(land_thru.shape, land_thru.dtype)),
        in_specs=(HBM, HBM, SEM, SEM) + (ANY,) * n_after, out_specs=(HBM, HBM), input_output_aliases={0: 0, 1: 1},
        compiler_params=pltpu.CompilerParams(has_side_effects=pltpu.SideEffectType.DATAFLOW_SIDE_EFFECTING),
    )(g_thru, land_thru, send_sems, recv_sems, *after)[1]


def _adamw_math(w, g, m, v):
    m = ADAM_B1 * m + (1.0 - ADAM_B1) * g
    v = ADAM_B2 * v + (1.0 - ADAM_B2) * (g * g)
    m_hat = m / (1.0 - ADAM_B1 ** ADAM_STEP)
    v_hat = v / (1.0 - ADAM_B2 ** ADAM_STEP)
    delta = -ADAM_LR * (m_hat / (jnp.sqrt(v_hat) + ADAM_EPS) + ADAM_WD * w)
    return delta, m, v


def _sum_adamw(parts, w, m, v, name, after=None):
    R, C = w.shape
    n_parts = len(parts)
    cg = C // n_parts
    tr = max(t for t in range(8, 257, 8) if R % t == 0)
    deps = [] if after is None else [after]

    def body(*refs):
        p_refs = refs[:n_parts]
        w_ref, m_ref, v_ref = refs[n_parts:n_parts + 3]
        g_ref, d_ref, mo_ref, vo_ref = refs[n_parts + 3 + len(deps):]
        for k, p_ref in enumerate(p_refs):
            @pl.when(pl.program_id(0) == k)
            def _():
                g = p_ref[0].astype(F32)
                for d in range(1, NDEV):
                    g = g + p_ref[d].astype(F32)
                g_ref[...] = g
                d_ref[...], mo_ref[...], vo_ref[...] = _adamw_math(w_ref[...], g, m_ref[...], v_ref[...])

    part = pl.BlockSpec((NDEV, tr, cg), lambda k, t: (0, t, 0))
    blk = pl.BlockSpec((tr, cg), lambda k, t: (t, k))
    return pl.pallas_call(
        body, grid=(n_parts, R // tr), name=name, in_specs=[part] * n_parts + [blk, blk, blk] + [ANY] * len(deps),
        out_specs=[blk] * 4, out_shape=[jax.ShapeDtypeStruct((R, C), F32)] * 4, compiler_params=_params("parallel", "parallel"),
    )(*parts, w, m, v, *deps)


def _adamw_small(ws, gs, ms, vs):
    n = len(ws)

    def body(*refs):
        w_refs, g_refs, m_refs, v_refs = (refs[k * n:(k + 1) * n] for k in range(4))
        d_refs, mo_refs, vo_refs = (refs[(4 + k) * n:(5 + k) * n] for k in range(3))
        for i in range(n):
            d_refs[i][...], mo_refs[i][...], vo_refs[i][...] = _adamw_math(w_refs[i][...], g_refs[i][...], m_refs[i][...], v_refs[i][...])

    shapes = [jax.ShapeDtypeStruct(a.shape, F32) for a in ws]
    outs = pl.pallas_call(body, out_shape=shapes * 3, name="adamw_small", compiler_params=_params())(*ws, *gs, *ms, *vs)
    return outs[:n], outs[n:2 * n], outs[2 * n:]


def kernel(x, mem, ffn1_norm, ffn1_w_up, ffn1_w_down, mix_norm, mem_norm, w_in, b_gate, conv_dw_w, conv_dw_b, conv_ln_g, conv_ln_b, conv_w_pw, att_rel_bias, att_w_o, mem_w_kv, mem_w_o, w_out, ffn2_norm, ffn2_w_up, ffn2_w_down, final_norm, loss_target, m_ffn1_norm, m_ffn1_w_up, m_ffn1_w_down, m_mix_norm, m_mem_norm, m_w_in, m_b_gate, m_conv_dw_w, m_conv_dw_b, m_conv_ln_g, m_conv_ln_b, m_conv_w_pw, m_att_rel_bias, m_att_w_o, m_mem_w_kv, m_mem_w_o, m_w_out, m_ffn2_norm, m_ffn2_w_up, m_ffn2_w_down, m_final_norm, v_ffn1_norm, v_ffn1_w_up, v_ffn1_w_down, v_mix_norm, v_mem_norm, v_w_in, v_b_gate, v_conv_dw_w, v_conv_dw_b, v_conv_ln_g, v_conv_ln_b, v_conv_w_pw, v_att_rel_bias, v_att_w_o, v_mem_w_kv, v_mem_w_o, v_w_out, v_ffn2_norm, v_ffn2_w_up, v_ffn2_w_down, v_final_norm):
    given = dict(locals())
    w = {n: given[n] for n in WEIGHTS}
    mom = {n: given["m_" + n] for n in WEIGHTS}
    var = {n: given["v_" + n] for n in WEIGHTS}

    NB, S, _ = x.shape
    T = NB * S
    ML = mem.shape[1]
    x0 = x.reshape(T, D)
    target = loss_target.reshape(T, D)
    mem2 = mem.reshape(NB * ML, D)

    def block(t, n):
        return jnp.transpose(t[0]) if n in TRANSPOSED else t[0]

    sh = dict(zip(BIG_ORDER, _cast_shards([block(w[n], n) for n in BIG_ORDER])))
    dw_t = jnp.transpose(conv_dw_w[0])

    def gather(names, extra=(), extra_kinds=()):
        return _gather_ride([sh[n] for n in names] + list(extra), [BIG[n] for n in names] + list(extra_kinds))

    W = {}
    names0 = ['ffn1_w_up', 'ffn1_w_down']
    tab, got = _bias_table(att_rel_bias[0], ride=gather(names0, [dw_t], [('row', dw_t.shape[0])]))
    W.update(zip(names0, got[:2]))
    dw_full = jnp.transpose(got[2])
    conv_vec = jnp.concatenate([conv_dw_b, conv_ln_g, conv_ln_b, jnp.zeros((5, CONV_W), F32)], axis=0)
    fin_g = final_norm.reshape(1, D)

    names1 = ['w_in', 'conv_w_pw', 'att_w_o', 'mem_w_kv', 'mem_w_o', 'w_out']
    (x1, ab1), got = _ffn_fwd(x0, ffn1_norm, W['ffn1_w_up'], W['ffn1_w_down'], TILE_FFN_FWD, "ffn1_fwd", ride=gather(names1))
    W.update(zip(names1, got))
    (uc, qkv, mq, gl, hmix), _ = _mix_fwd(x1, mix_norm, W['w_in'], TILE_TOKENS)
    uc3 = uc.reshape(NB, S, 2 * CONV_W)
    qkv3 = qkv.reshape(NB, S, 3 * ATT_W)
    mq3 = mq.reshape(NB, S, MEM_W)
    cact, conv_z = _conv_fwd(uc3, dw_full, conv_vec)
    cact = cact.reshape(T, CONV_W)
    names2 = ['ffn2_w_up', 'ffn2_w_down']
    oatt3, att_lse, got = _att_fwd(qkv3, tab, ride=gather(names2))
    W.update(zip(names2, got))
    oatt = oatt3.reshape(T, ATT_W)
    memh, kv = _memkv_fwd(mem2, mem_norm, W['mem_w_kv'], TILE_TOKENS)
    kv3 = kv.reshape(NB, ML, 2 * MEM_W)
    omem = _mematt_fwd(mq3, kv3, TILE_TOKENS).reshape(T, MEM_W)
    branch_w = (W['conv_w_pw'], W['att_w_o'], W['mem_w_o'], W['w_out'])
    x2, ymix = _combine_fwd(x1, cact, oatt, omem, gl, b_gate, *branch_w, TILE_TOKENS)
    dx3, ab2, loss_part, dg_final = _ffn_fwd_loss(x2, ffn2_norm, W['ffn2_w_up'], W['ffn2_w_down'], fin_g, target, TILE_FFN_FWD,
                                                  "ffn2_fwd_loss")

    def scatter(grads, names):
        return _scatter_ride(grads, [BIG[n] for n in names])

    G, P = {}, {}
    dx2, dab2, act2, h2, dg_ffn2 = _ffn_bwd(x2, dx3, ab2, ffn2_norm, W['ffn2_w_up'], W['ffn2_w_down'], TILE_FFN, "ffn2_bwd")
    g_up, _ = _tn_matmul(dab2, h2, 512, "grad_ffn2_w_up_a", tt=TILE_GRAD_TOKENS_WIDE, x_part=(0, 2), out_rows=2 * FF)
    G['ffn2_w_up'], _ = _tn_matmul(dab2, h2, 512, "grad_ffn2_w_up_b", tt=TILE_GRAD_TOKENS_WIDE, x_part=(1, 2), out_rows=2 * FF,
                                   prev=g_up)
    G['ffn2_w_down'], _ = _tn_matmul(act2, dx3, 512, "grad_ffn2_w_down", scale=0.5, tt=TILE_GRAD_TOKENS_WIDE)
    (dgl, dcact, doatt, domem, dyc, dya, dym, dbg), got = _combine_bwd(
        dx2, cact, oatt, omem, gl, b_gate, *branch_w, TILE_COMBINE, ride=scatter([G['ffn2_w_up']], ['ffn2_w_up']))
    P['ffn2_w_up'] = got
    G['w_out'], _ = _tn_matmul(ymix, dx2, D, "grad_w_out", tt=TILE_GRAD_TOKENS_WIDE)
    G['conv_w_pw'], _ = _tn_matmul(cact, dyc, D, "grad_conv_w_pw")
    G['att_w_o'], _ = _tn_matmul(oatt, dya, D, "grad_att_w_o")
    G['mem_w_o'], _ = _tn_matmul(omem, dym, D, "grad_mem_w_o")
    dmq3, dkv3 = _mematt_bwd(mq3, kv3, domem.reshape(NB, S, MEM_W), TILE_TOKENS)
    dkv = dkv3.reshape(NB * ML, 2 * MEM_W)
    dg_mem = _memkv_bwd(mem2, dkv, W['mem_w_kv'], TILE_TOKENS)
    G['mem_w_kv'], _ = _tn_matmul(memh, dkv, 512, "grad_mem_w_kv")
    names = ['ffn2_w_down', 'w_out', 'conv_w_pw', 'att_w_o', 'mem_w_o']
    (dqkv3, dscore), got = _att_bwd(qkv3, oatt3, att_lse, doatt.reshape(NB, S, ATT_W), tab,
                                    ride=scatter([G[n] for n in names], names))
    P.update((n, [p]) for n, p in zip(names, got))
    d_rel = _rel_bias_grad(dscore)
    (duc3, d_dw, d_cvec), got = _conv_bwd(uc3, conv_z, dcact.reshape(NB, S, CONV_W), dw_full, conv_vec,
                                          ride=scatter([G['mem_w_kv']], ['mem_w_kv']))
    P['mem_w_kv'] = got
    duc, dqkv, dmq = duc3.reshape(T, 2 * CONV_W), dqkv3.reshape(T, 3 * ATT_W), dmq3.reshape(T, MEM_W)
    g_in, _ = _tn_matmul(hmix, duc, 1024, "grad_w_in_conv", out_cols=IN_COLS, col_off=0)
    g_in, _ = _tn_matmul(hmix, dqkv, 512, "grad_w_in_qkv", out_cols=IN_COLS, col_off=1024, prev=g_in)
    g_in, _ = _tn_matmul(hmix, dmq, 512, "grad_w_in_mq", out_cols=IN_COLS, col_off=2560, prev=g_in)
    G['w_in'], _ = _tn_matmul(hmix, dgl, 1024, "grad_w_in_gate", out_cols=IN_COLS, col_off=3072, prev=g_in)
    def start_scatter(g, name, tag):
        kind = BIG[name]
        return _scatter_start(g, _own_block(g, *kind, tag), *kind, tag) + (kind, tag)

    def wait_scatter(started, after):
        send_sems, recv_sems, g_thru, land_thru, _, kind, tag = started
        return _scatter_wait(send_sems, recv_sems, g_thru, land_thru, after, *kind, tag)

    ex_in = start_scatter(G['w_in'], 'w_in', "w_in")
    (dx1, dg_mix), _ = _mix_bwd(x1, dx2, duc, dqkv, dmq, dgl, mix_norm, W['w_in'], TILE_TOKENS, after=ex_in[4])
    dx0, dab1, act1, h1, dg_ffn1 = _ffn_bwd(x0, dx1, ab1, ffn1_norm, W['ffn1_w_up'], W['ffn1_w_down'], TILE_FFN, "ffn1_bwd")
    g_wd1, _ = _tn_matmul(act1, dx1, 512, "grad_ffn1_w_down", scale=0.5, tt=TILE_GRAD_TOKENS_WIDE)
    ex_wd = start_scatter(g_wd1, 'ffn1_w_down', "ffn1_w_down")
    g_wu1a, _ = _tn_matmul(dab1, h1, 512, "grad_ffn1_w_up_a", tt=TILE_GRAD_TOKENS_WIDEST, y_part=(0, 2), after=ex_wd[4])
    ex_a = start_scatter(g_wu1a, 'ffn1_w_up', "ffn1_w_up_a")
    g_wu1b, _ = _tn_matmul(dab1, h1, 512, "grad_ffn1_w_up_b", tt=TILE_GRAD_TOKENS_WIDEST, y_part=(1, 2), after=ex_a[4])
    ex_b = start_scatter(g_wu1b, 'ffn1_w_up', "ffn1_w_up_b")
    token = ex_b[4]

    small_names = ['loss', 'ffn1_norm', 'mix_norm', 'mem_norm', 'b_gate', 'conv_dw_w', 'conv_vec', 'att_rel_bias', 'ffn2_norm',
                   'final_norm']
    small = dict(zip(small_names, _all_sum_small(
        [loss_part + token[0:1], dg_ffn1, dg_mix, dg_mem, dbg, d_dw, d_cvec, d_rel, dg_ffn2, dg_final])))
    loss = small['loss'][0, 0]
    me = _dev_index(*_my_coords())
    for i, n in enumerate(['conv_dw_b', 'conv_ln_g', 'conv_ln_b']):
        small[n] = small['conv_vec'][i:i + 1]
    small['conv_dw_w'] = lax.dynamic_slice(small['conv_dw_w'], (0, me * conv_dw_w.shape[2]), (CONV_K, conv_dw_w.shape[2]))
    little = [n for n in WEIGHTS if n not in BIG]
    as2d = lambda t, n: t.reshape(small[n].shape)
    d_s, m_s, v_s = _adamw_small([as2d(w[n], n) for n in little], [small[n] for n in little],
                                 [as2d(mom[n], n) for n in little], [as2d(var[n], n) for n in little])
    grad, delta, new_m, new_v = {}, {}, {}, {}
    for i, n in enumerate(little):
        grad[n], delta[n], new_m[n], new_v[n] = (t.reshape(w[n].shape) for t in (small[n], d_s[i], m_s[i], v_s[i]))
    done = [d_s[0]]
    waited = {'w_in': [ex_in], 'ffn1_w_down': [ex_wd], 'ffn1_w_up': [ex_a, ex_b]}
    order = [n for n in BIG_ORDER if n not in waited] + list(waited)
    for n in order:
        if n in waited:
            P[n] = [wait_scatter(ex, done) for ex in waited[n]]
        outs = _sum_adamw(P[n], block(w[n], n), block(mom[n], n), block(var[n], n), "adamw_" + n,
                          after=None if n in waited else token)
        done.append(outs[0])
        grad[n], delta[n], new_m[n], new_v[n] = ((jnp.transpose(t) if n in TRANSPOSED else t)[None] for t in outs)

    return (loss, dx0.reshape(NB, S, D), *[grad[n] for n in WEIGHTS], *[delta[n] for n in WEIGHTS],
            *[new_m[n] for n in WEIGHTS], *[new_v[n] for n in WEIGHTS])
```

```python
import functools

import jax
import jax.numpy as jnp
from jax import lax
from jax.experimental import pallas as pl
from jax.experimental.pallas import tpu as pltpu

F32 = jnp.float32
BF16 = jnp.bfloat16

EPS = 1e-6
MASK_VALUE = -1e30
D = 1024
NDEV = 8
FF = 2816
FF_SHARD = 704
FF_HALF_ROWS = 352
FF_BLOCK_EDGES = ()
IN_COLS = 6144
CONV_W = 512
CONV_K = 31
CONV_HALO = 32
CONV_CHUNK = 32
CONV_WIN = CONV_CHUNK + 40
GLU_CHUNK = 128
ATT_W = 512
ATT_HEADS = 8
ATT_HD = 64
CHUNK = 64
LEFT_CHUNKS = 8
MAX_REL = 128
N_REL = 192
QB = 256
KWIN = QB + LEFT_CHUNKS * CHUNK
KPAD = LEFT_CHUNKS * CHUNK
DS_LANES = 1024
MEM_W = 512
MEM_HEADS = 4
MEM_HD = 128
ADAM_LR = 0.001
ADAM_B1 = 0.9
ADAM_B2 = 0.999
ADAM_EPS = 1e-08
ADAM_WD = 0.01
ADAM_STEP = 10
VMEM_LIMIT = 60 * 1024 * 1024
TILE_FFN = 256
TILE_FFN_FWD = 512
TILE_COMBINE = 256
RING_SLOTS = 3
TILE_TOKENS = 512
TILE_GRAD_TOKENS = 2048
TILE_GRAD_TOKENS_WIDE = 1024
TILE_GRAD_TOKENS_WIDEST = 512

MESH = pl.DeviceIdType.MESH
ANY = pl.BlockSpec(memory_space=pl.ANY)

WEIGHTS = ['ffn1_norm', 'ffn1_w_up', 'ffn1_w_down', 'mix_norm', 'mem_norm', 'w_in', 'b_gate', 'conv_dw_w', 'conv_dw_b',
           'conv_ln_g', 'conv_ln_b', 'conv_w_pw', 'att_rel_bias', 'att_w_o', 'mem_w_kv', 'mem_w_o', 'w_out', 'ffn2_norm',
           'ffn2_w_up', 'ffn2_w_down', 'final_norm']
BIG = {
    'ffn1_w_up': ('row', FF_SHARD), 'ffn1_w_down': ('row', FF_HALF_ROWS), 'w_in': ('col', 768),
    'conv_w_pw': ('col', 128), 'att_w_o': ('col', 128), 'mem_w_kv': ('row', 128), 'mem_w_o': ('col', 128),
    'w_out': ('row', 128), 'ffn2_w_up': ('row', FF_SHARD), 'ffn2_w_down': ('row', FF_HALF_ROWS),
}
BIG_ORDER = ['ffn1_w_up', 'ffn1_w_down', 'w_in', 'conv_w_pw', 'att_w_o', 'mem_w_kv', 'mem_w_o', 'w_out', 'ffn2_w_up', 'ffn2_w_down']
TRANSPOSED = ('ffn1_w_up', 'ffn2_w_up')


def _dot(a, b):
    return jnp.dot(a, b, preferred_element_type=F32)


def _dot_nt(a, b):
    return lax.dot_general(a, b, (((1,), (1,)), ((), ())), preferred_element_type=F32)


def _dot_tn(a, b):
    return lax.dot_general(a, b, (((0,), (0,)), ((), ())), preferred_element_type=F32)


def _sigmoid(v):
    return jax.nn.sigmoid(v)


def _const(shape):
    return pl.BlockSpec(shape, lambda *_: (0,) * len(shape), pipeline_mode=pl.Buffered(1))


def _params(*sem):
    return pltpu.CompilerParams(dimension_semantics=sem if sem else None, vmem_limit_bytes=VMEM_LIMIT)


def _my_coords():
    return lax.axis_index("x"), lax.axis_index("y"), lax.axis_index("c")


def _dev_index(px, py, pc):
    return 4 * px + 2 * py + pc


def _window(ref, kind, n, p):
    if kind == 'row':
        return ref.at[pl.ds(pl.multiple_of(p * n, n), n), :]
    return ref.at[:, pl.ds(pl.multiple_of(p * n, 128), n)]


def _full_shape(kind, n, shard_shape):
    if kind == 'row':
        return (NDEV * n, shard_shape[1])
    return (shard_shape[0], NDEV * n)


def _cast_shards(shards):
    n = len(shards)

    def body(*refs):
        for i in range(n):
            refs[n + i][...] = refs[i][...].astype(BF16)

    out_shape = [jax.ShapeDtypeStruct(s.shape, BF16) for s in shards]
    return pl.pallas_call(body, out_shape=out_shape, name="cast_shards", compiler_params=_params())(*shards)


class _Ride:
    def __init__(self, inputs, out_shape, scratch, start, finish, mids=()):
        self.inputs, self.out_shape, self.scratch = list(inputs), list(out_shape), list(scratch)
        self.start, self.finish, self.mids = start, finish, tuple(mids)


def _pallas(body, name, grid, in_specs, out_specs, out_shape, args, scratch_shapes=(), sem=None, aliases=None, ride=None,
            after=None):
    if ride is None:
        n_in, n_dep = len(args), 0 if after is None else 1

        def kernel_body(*refs):
            body(*refs[:n_in], *refs[n_in + n_dep:])

        outs = pl.pallas_call(kernel_body if n_dep else body, grid=grid, name=name, in_specs=list(in_specs) + [ANY] * n_dep,
                              out_specs=out_specs, out_shape=out_shape, scratch_shapes=list(scratch_shapes),
                              input_output_aliases=aliases or {}, compiler_params=_params(*sem),
                              )(*args, *([after] if n_dep else []))
        return list(outs), []
    n_in, n_out, n_scr = len(args), len(out_shape), len(scratch_shapes)
    r_in, r_out = len(ride.inputs), len(ride.out_shape)

    def wrapped(*refs):
        k_in, rin = refs[:n_in], refs[n_in:n_in + r_in]
        o0 = n_in + r_in
        k_out, rout = refs[o0:o0 + n_out], refs[o0 + n_out:o0 + n_out + r_out]
        s0 = o0 + n_out + r_out
        k_scr, rscr = refs[s0:s0 + n_scr], refs[s0 + n_scr:]
        ids = [pl.program_id(k) for k in range(len(grid))]
        first = functools.reduce(jnp.logical_and, [i == 0 for i in ids])
        last = functools.reduce(jnp.logical_and, [i == g - 1 for i, g in zip(ids, grid)])
        pl.when(first)(lambda: ride.start(rin, rout, rscr))
        single_step = all(g == 1 for g in grid)
        for quarter, mid in ride.mids:
            if not single_step:
                at_mid = functools.reduce(jnp.logical_and, [ids[0] == (quarter * grid[0]) // 4] + [i == 0 for i in ids[1:]])
                pl.when(at_mid)(functools.partial(mid, rin, rout, rscr))
        body(*k_in, *k_out, *k_scr)
        for _, mid in ride.mids:
            if single_step:
                mid(rin, rout, rscr)
        pl.when(last)(lambda: ride.finish(rin, rout, rscr))

    outs = pl.pallas_call(
        wrapped, grid=grid, name=name, in_specs=list(in_specs) + [ANY] * r_in, out_specs=list(out_specs) + [ANY] * r_out,
        out_shape=list(out_shape) + ride.out_shape, scratch_shapes=list(scratch_shapes) + ride.scratch,
        input_output_aliases=aliases or {}, compiler_params=_params(*(["arbitrary"] * len(grid))),
    )(*args, *ride.inputs)
    return list(outs[:n_out]), list(outs[n_out:])


def _gather_ride(shards, kinds):
    n = len(shards)

    def plan(rin, out, sems):
        send_sems, recv_sems, local_sems = sems[:3]
        x, y, c = _my_coords()
        me, sibling = (x, y, c), (x, y, 1 - c)
        xn, yn, diag = (1 - x, y), (x, 1 - y), (1 - x, 1 - y)

        def win(i, dev):
            return _window(out[i], kinds[i][0], kinds[i][1], _dev_index(*dev))

        def copy(i, k, block, to, from_shard=False):
            return pltpu.make_async_remote_copy(
                src_ref=rin[i] if from_shard else win(i, block), dst_ref=win(i, block),
                send_sem=send_sems.at[i, k], recv_sem=recv_sems.at[i, k], device_id=to, device_id_type=MESH)

        def each(fn):
            return [fn(i) for i in range(n)]

        return dict(
            local=lambda: each(lambda i: pltpu.make_async_copy(rin[i], win(i, me), local_sems.at[i])),
            own=lambda: [cp for i in range(n) for cp in (copy(i, 0, me, sibling, True), copy(i, 1, me, (*xn, c), True),
                                                         copy(i, 2, me, (*yn, c), True))],
            from_x=lambda: each(lambda i: copy(i, 1, (*xn, c), me)),
            from_y=lambda: each(lambda i: copy(i, 2, (*yn, c), me)),
            x_block_on_to_y=lambda: each(lambda i: copy(i, 3, (*xn, c), (*yn, c))),
            y_block_on_to_x=lambda: each(lambda i: copy(i, 3, (*yn, c), (*xn, c))),
            from_diag=lambda: each(lambda i: copy(i, 3, (*diag, c), me)),
            to_sibling=lambda j: each(lambda i: copy(i, 4 + j, (*(xn, yn, diag)[j], c), sibling)),
            from_sibling=lambda: [cp for i in range(n) for cp in
                                  [copy(i, 0, sibling, me)] + [copy(i, 4 + j, (*chip, 1 - c), me) for j, chip in enumerate((xn, yn, diag))]],
            north=c == 1)

    def start(rin, out, sems):
        p = plan(rin, out, sems)
        for cp in p['local']() + p['own']():
            cp.start()

    def pass_diagonal(rin, out, sems):
        p = plan(rin, out, sems)

        @pl.when(p['north'])
        def _():
            for got, fwd, sib in zip(p['from_x'](), p['x_block_on_to_y'](), p['to_sibling'](0)):
                got.wait_recv()
                fwd.start()
                sib.start()

        @pl.when(jnp.logical_not(p['north']))
        def _():
            for got, fwd, sib in zip(p['from_y'](), p['y_block_on_to_x'](), p['to_sibling'](1)):
                got.wait_recv()
                fwd.start()
                sib.start()

    def pass_to_sibling(rin, out, sems):
        p = plan(rin, out, sems)

        @pl.when(p['north'])
        def _():
            for got, sib in zip(p['from_y'](), p['to_sibling'](1)):
                got.wait_recv()
                sib.start()

        @pl.when(jnp.logical_not(p['north']))
        def _():
            for got, sib in zip(p['from_x'](), p['to_sibling'](0)):
                got.wait_recv()
                sib.start()
        for got, sib in zip(p['from_diag'](), p['to_sibling'](2)):
            got.wait_recv()
            sib.start()

    def finish(rin, out, sems):
        p = plan(rin, out, sems)
        for cp in p['from_sibling']():
            cp.wait_recv()
        for cp in p['own']() + p['to_sibling'](0) + p['to_sibling'](1) + p['to_sibling'](2):
            cp.wait_send()

        @pl.when(p['north'])
        def _():
            for cp in p['x_block_on_to_y']():
                cp.wait_send()

        @pl.when(jnp.logical_not(p['north']))
        def _():
            for cp in p['y_block_on_to_x']():
                cp.wait_send()
        for cp in p['local']():
            cp.wait()

    out_shape = [jax.ShapeDtypeStruct(_full_shape(k, m, s.shape), s.dtype) for s, (k, m) in zip(shards, kinds)]
    scratch = [pltpu.SemaphoreType.DMA((n, 7)), pltpu.SemaphoreType.DMA((n, 7)), pltpu.SemaphoreType.DMA((n,))]
    return _Ride(shards, out_shape, scratch, start, finish, mids=((2, pass_diagonal), (3, pass_to_sibling)))


def _scatter_ride(grads, kinds):
    n = len(grads)

    def plan(g, out, sems):
        send_sems, recv_sems, local_sems = sems
        x, y, c = _my_coords()
        me = _dev_index(x, y, c)

        def local():
            return [pltpu.make_async_copy(_window(g[i], kinds[i][0], kinds[i][1], me), out[i].at[me], local_sems.at[i])
                    for i in range(n)]

        def remote(arrival):
            cps = []
            for rel in range(1, NDEV):
                peer = _peer(x, y, c, rel)
                dev = _dev_index(*peer)
                for i in range(n):
                    kind, m = kinds[i]
                    cps.append(pltpu.make_async_remote_copy(
                        src_ref=_window(g[i], kind, m, me if arrival else dev), dst_ref=out[i].at[dev if arrival else me],
                        send_sem=send_sems.at[i, rel - 1], recv_sem=recv_sems.at[i, rel - 1], device_id=peer, device_id_type=MESH))
            return cps

        return local, remote

    def start(g, out, sems):
        local, remote = plan(g, out, sems)
        for cp in local() + remote(False):
            cp.start()

    def finish(g, out, sems):
        local, remote = plan(g, out, sems)
        for cp in remote(True):
            cp.wait_recv()
        for cp in remote(False):
            cp.wait_send()
        for cp in local():
            cp.wait()

    def block_shape(gr, kind, m):
        return (m, gr.shape[1]) if kind == 'row' else (gr.shape[0], m)

    out_shape = [jax.ShapeDtypeStruct((NDEV,) + block_shape(gr, k, m), gr.dtype) for gr, (k, m) in zip(grads, kinds)]
    scratch = [pltpu.SemaphoreType.DMA((n, NDEV - 1)), pltpu.SemaphoreType.DMA((n, NDEV - 1)), pltpu.SemaphoreType.DMA((n,))]
    return _Ride(grads, out_shape, scratch, start, finish)


def _rms_stats(xf):
    r = lax.rsqrt(jnp.mean(xf * xf, axis=-1, keepdims=True) + EPS)
    return xf * r, r


def _rms_bwd(dh, g, xhat, r):
    dxhat = dh * g
    return r * (dxhat - xhat * jnp.mean(dxhat * xhat, axis=-1, keepdims=True))


def _ffn_blocks():
    edges = (0,) + FF_BLOCK_EDGES + (FF,)
    return [(slice(lo, hi), slice(FF + lo, FF + hi)) for lo, hi in zip(edges[:-1], edges[1:])]


def _swiglu_tile(x_ref, g_ref, wut_ref, wd_ref, ab_ref):
    xf = x_ref[...]
    xhat, _ = _rms_stats(xf)
    h = (xhat * g_ref[...]).astype(BF16)
    acc = jnp.zeros(xf.shape, F32)
    for ra, rb in _ffn_blocks():
        a = _dot_nt(h, wut_ref[ra, :])
        b = _dot_nt(h, wut_ref[rb, :])
        ab_ref[:, ra] = a.astype(BF16)
        ab_ref[:, rb] = b.astype(BF16)
        act = (a * _sigmoid(a) * b).astype(BF16)
        acc = acc + _dot(act, wd_ref[ra, :])
    return xf + 0.5 * acc


def _ffn_fwd(x, g, wut, wd, tm, name, ride=None):
    T = x.shape[0]

    def body(x_ref, g_ref, wut_ref, wd_ref, xo_ref, ab_ref):
        xo_ref[...] = _swiglu_tile(x_ref, g_ref, wut_ref, wd_ref, ab_ref)

    return _pallas(
        body, name, (T // tm,),
        [pl.BlockSpec((tm, D), lambda t: (t, 0)), _const((1, D)), _const((2 * FF, D)), _const((FF, D))],
        [pl.BlockSpec((tm, D), lambda t: (t, 0)), pl.BlockSpec((tm, 2 * FF), lambda t: (t, 0))],
        [jax.ShapeDtypeStruct((T, D), F32), jax.ShapeDtypeStruct((T, 2 * FF), BF16)],
        (x, g, wut, wd), sem=("arbitrary",), ride=ride)


def _ffn_fwd_loss(x, g, wut, wd, g_final, target, tm, name):
    T = x.shape[0]

    def body(x_ref, g_ref, wut_ref, wd_ref, gf_ref, t_ref, dx_ref, ab_ref, loss_ref, dgf_ref):
        xhat, r = _rms_stats(_swiglu_tile(x_ref, g_ref, wut_ref, wd_ref, ab_ref))
        gain = gf_ref[...]
        diff = xhat * gain - t_ref[...]
        dout = diff * (1.0 / D)

        @pl.when(pl.program_id(0) == 0)
        def _():
            loss_ref[...] = jnp.zeros_like(loss_ref)
            dgf_ref[...] = jnp.zeros_like(dgf_ref)
        sq = jnp.sum(jnp.sum(diff * diff, axis=0, keepdims=True), axis=1, keepdims=True)
        loss_ref[...] += jnp.broadcast_to(sq * (0.5 / D), (1, 128))
        dgf_ref[...] += jnp.sum(dout * xhat, axis=0, keepdims=True)
        dx_ref[...] = _rms_bwd(dout, gain, xhat, r)

    row = pl.BlockSpec((tm, D), lambda t: (t, 0))
    return pl.pallas_call(
        body, grid=(T // tm,), name=name,
        in_specs=[row, _const((1, D)), _const((2 * FF, D)), _const((FF, D)), _const((1, D)), row],
        out_specs=[row, pl.BlockSpec((tm, 2 * FF), lambda t: (t, 0)), pl.BlockSpec((1, 128), lambda t: (0, 0)),
                   pl.BlockSpec((1, D), lambda t: (0, 0))],
        out_shape=[jax.ShapeDtypeStruct((T, D), F32), jax.ShapeDtypeStruct((T, 2 * FF), BF16),
                   jax.ShapeDtypeStruct((1, 128), F32), jax.ShapeDtypeStruct((1, D), F32)],
        compiler_params=_params("arbitrary"),
    )(x, g, wut, wd, g_final, target)


def _ffn_bwd(x, dy, ab, g, wut, wd, tm, name):
    T = x.shape[0]

    def body(x_ref, dy_ref, ab_ref, g_ref, wut_ref, wd_ref, dx_ref, dab_ref, act_ref, h_ref, dg_ref):
        xf = x_ref[...]
        xhat, r = _rms_stats(xf)
        gain = g_ref[...]
        h_ref[...] = (xhat * gain).astype(BF16)
        dy = dy_ref[...]
        dyh = (0.5 * dy).astype(BF16)
        dh = jnp.zeros((tm, D), F32)
        for ra, rb in _ffn_blocks():
            a = ab_ref[:, ra].astype(F32)
            b = ab_ref[:, rb].astype(F32)
            dact = _dot_nt(dyh, wd_ref[ra, :])
            sg = _sigmoid(a)
            sl = a * sg
            act_ref[:, ra] = (sl * b).astype(BF16)
            da = (dact * b * (sg * (1.0 + a * (1.0 - sg)))).astype(BF16)
            db = (dact * sl).astype(BF16)
            dab_ref[:, ra] = da
            dab_ref[:, rb] = db
            dh = dh + _dot(da, wut_ref[ra, :]) + _dot(db, wut_ref[rb, :])
        dx_ref[...] = dy + _rms_bwd(dh, gain, xhat, r)

        @pl.when(pl.program_id(0) == 0)
        def _():
            dg_ref[...] = jnp.zeros_like(dg_ref)
        dg_ref[...] += jnp.sum(dh * xhat, axis=0, keepdims=True)

    return pl.pallas_call(
        body, grid=(T // tm,), name=name,
        in_specs=[pl.BlockSpec((tm, D), lambda t: (t, 0)), pl.BlockSpec((tm, D), lambda t: (t, 0)),
                  pl.BlockSpec((tm, 2 * FF), lambda t: (t, 0)), _const((1, D)), _const((2 * FF, D)), _const((FF, D))],
        out_specs=[pl.BlockSpec((tm, D), lambda t: (t, 0)), pl.BlockSpec((tm, 2 * FF), lambda t: (t, 0)),
                   pl.BlockSpec((tm, FF), lambda t: (t, 0)), pl.BlockSpec((tm, D), lambda t: (t, 0)),
                   pl.BlockSpec((1, D), lambda t: (0, 0))],
        out_shape=[jax.ShapeDtypeStruct((T, D), F32), jax.ShapeDtypeStruct((T, 2 * FF), BF16),
                   jax.ShapeDtypeStruct((T, FF), BF16), jax.ShapeDtypeStruct((T, D), BF16), jax.ShapeDtypeStruct((1, D), F32)],
        compiler_params=_params("arbitrary"),
    )(x, dy, ab, g, wut, wd)


def _tn_matmul(xm, ym, tn, name, scale=None, out_cols=None, col_off=0, prev=None, tt=TILE_GRAD_TOKENS, x_part=(0, 1),
               out_rows=None, y_part=(0, 1), ride=None, after=None):
    T = xm.shape[0]
    xi, xn = x_part
    yi, yn = y_part
    K = xm.shape[1] // xn
    N = ym.shape[1] // yn
    out_cols = N if out_cols is None else out_cols
    row_blk = xi if out_rows is not None else 0
    out_rows = K if out_rows is None else out_rows
    tt = min(tt, T)
    nt = T // tt
    off = col_off // tn

    def body(*refs):
        x_ref, y_ref = refs[0], refs[1]
        o_ref, acc = refs[-2], refs[-1]

        @pl.when(pl.program_id(1) == 0)
        def _():
            acc[...] = jnp.zeros_like(acc)
        acc[...] += _dot_tn(x_ref[...].astype(BF16), y_ref[...].astype(BF16))

        @pl.when(pl.program_id(1) == nt - 1)
        def _():
            res = acc[...]
            o_ref[...] = (res if scale is None else res * scale).astype(BF16)

    ycol = yi * (N // tn)
    in_specs = [pl.BlockSpec((tt, K), lambda n, t: (t, xi)), pl.BlockSpec((tt, tn), lambda n, t: (t, n + ycol))]
    args = [xm, ym]
    aliases = {}
    if prev is not None:
        in_specs.append(ANY)
        args.append(prev)
        aliases = {2: 0}
    outs, rode = _pallas(
        body, name, (N // tn, nt), in_specs, [pl.BlockSpec((K, tn), lambda n, t: (row_blk, n + off))],
        [jax.ShapeDtypeStruct((out_rows, out_cols), BF16)], args, scratch_shapes=[pltpu.VMEM((K, tn), F32)],
        sem=("parallel", "arbitrary"), aliases=aliases, ride=ride, after=after)
    return outs[0], rode


def _mix_fwd(x, g, w_in, tm, ride=None):
    T = x.shape[0]

    def body(x_ref, g_ref, w_ref, uc_ref, qkv_ref, mq_ref, gl_ref, h_ref):
        xhat, _ = _rms_stats(x_ref[...])
        h = (xhat * g_ref[...]).astype(BF16)
        h_ref[...] = h
        uc_ref[...] = _dot(h, w_ref[:, 0:1024])
        qkv_ref[...] = _dot(h, w_ref[:, 1024:2560]).astype(BF16)
        mq_ref[...] = _dot(h, w_ref[:, 2560:3072]).astype(BF16)
        for j in range(3):
            gl_ref[:, j * D:(j + 1) * D] = _dot(h, w_ref[:, 3072 + j * D:3072 + (j + 1) * D]).astype(BF16)

    row = lambda w: pl.BlockSpec((tm, w), lambda t: (t, 0))
    return _pallas(
        body, "mix_fwd", (T // tm,), [row(D), _const((1, D)), _const((D, IN_COLS))],
        [row(1024), row(1536), row(512), row(3072), row(D)],
        [jax.ShapeDtypeStruct((T, 1024), F32), jax.ShapeDtypeStruct((T, 1536), BF16), jax.ShapeDtypeStruct((T, 512), BF16),
         jax.ShapeDtypeStruct((T, 3072), BF16), jax.ShapeDtypeStruct((T, D), BF16)],
        (x, g, w_in), sem=("parallel",), ride=ride)


def _mix_bwd(x, dres, duc, dqkv, dmq, dgl, g, w_in, tm, ride=None, after=None):
    T = x.shape[0]

    def body(x_ref, dres_ref, duc_ref, dqkv_ref, dmq_ref, dgl_ref, g_ref, w_ref, dx_ref, dg_ref):
        xhat, r = _rms_stats(x_ref[...])
        dh = _dot_nt(duc_ref[...], w_ref[:, 0:1024])
        dh = dh + _dot_nt(dqkv_ref[...], w_ref[:, 1024:2560])
        dh = dh + _dot_nt(dmq_ref[...], w_ref[:, 2560:3072])
        dh = dh + _dot_nt(dgl_ref[...], w_ref[:, 3072:6144])
        dx_ref[...] = dres_ref[...] + _rms_bwd(dh, g_ref[...], xhat, r)

        @pl.when(pl.program_id(0) == 0)
        def _():
            dg_ref[...] = jnp.zeros_like(dg_ref)
        dg_ref[...] += jnp.sum(dh * xhat, axis=0, keepdims=True)

    row = lambda w: pl.BlockSpec((tm, w), lambda t: (t, 0))
    return _pallas(
        body, "mix_bwd", (T // tm,),
        [row(D), row(D), row(1024), row(1536), row(512), row(3072), _const((1, D)), _const((D, IN_COLS))],
        [row(D), pl.BlockSpec((1, D), lambda t: (0, 0))],
        [jax.ShapeDtypeStruct((T, D), F32), jax.ShapeDtypeStruct((1, D), F32)],
        (x, dres, duc, dqkv, dmq, dgl, g, w_in), sem=("arbitrary",), ride=ride, after=after)


def _shifted(win, base, copies):
    for k in range(8):
        copies[k] = win[base + k:base + k + CONV_CHUNK + 24]
    return copies


def _tap_slices(copies, tap):
    out = []
    for k in range(8):
        for a in range(4):
            j = tap(a, k)
            if 0 <= j < CONV_K:
                out.append((j, copies[k, pl.ds(8 * a, CONV_CHUNK), :]))
    return out


def _conv_taps(copies, w_ref, tap):
    acc = jnp.zeros((CONV_CHUNK, CONV_W), F32)
    for j, rows in _tap_slices(copies, tap):
        acc = acc + rows * w_ref[j:j + 1, :]
    return acc


def _fold8(v):
    acc = v[0:8]
    for r in range(8, CONV_CHUNK, 8):
        acc = acc + v[r:r + 8]
    return acc


def _glu_into(uc_ref, vpad, S):
    vpad[pl.ds(0, CONV_HALO), :] = jnp.zeros((CONV_HALO, CONV_W), F32)
    vpad[pl.ds(S + CONV_HALO, CONV_HALO), :] = jnp.zeros((CONV_HALO, CONV_W), F32)

    def glu(i, carry):
        r0 = pl.multiple_of(i * GLU_CHUNK, GLU_CHUNK)
        a = uc_ref[0, pl.ds(r0, GLU_CHUNK), 0:CONV_W]
        gt = uc_ref[0, pl.ds(r0, GLU_CHUNK), CONV_W:2 * CONV_W]
        vpad[pl.ds(pl.multiple_of(r0 + CONV_HALO, CONV_HALO), GLU_CHUNK), :] = a * _sigmoid(gt)
        return carry
    lax.fori_loop(0, S // GLU_CHUNK, glu, 0)


def _layer_norm(z, vec_ref):
    xc = z - jnp.mean(z, axis=-1, keepdims=True)
    rstd = lax.rsqrt(jnp.mean(xc * xc, axis=-1, keepdims=True) + EPS)
    xn = xc * rstd
    return xn, rstd, xn * vec_ref[1:2, :] + vec_ref[2:3, :]


def _conv_fwd(uc, dw_w, vec):
    NB, S, _ = uc.shape

    def body(uc_ref, w_ref, vec_ref, o_ref, z_ref, vpad, copies):
        _glu_into(uc_ref, vpad, S)

        def conv(i, carry):
            r0 = pl.multiple_of(i * CONV_CHUNK, CONV_CHUNK)
            win = vpad[pl.ds(r0, CONV_WIN), :]
            z = _conv_taps(_shifted(win, CONV_HALO - (CONV_K - 1), copies), w_ref, lambda a, k: 8 * a + k) + vec_ref[0:1, :]
            z_ref[0, pl.ds(r0, CONV_CHUNK), :] = z
            _, _, yln = _layer_norm(z, vec_ref)
            o_ref[0, pl.ds(r0, CONV_CHUNK), :] = (yln * _sigmoid(yln)).astype(BF16)
            return carry
        lax.fori_loop(0, S // CONV_CHUNK, conv, 0, unroll=8)

    seq = pl.BlockSpec((1, S, CONV_W), lambda b: (b, 0, 0))
    return pl.pallas_call(
        body, grid=(NB,), name="conv_fwd",
        in_specs=[pl.BlockSpec((1, S, 2 * CONV_W), lambda b: (b, 0, 0)), _const((CONV_K, CONV_W)), _const((8, CONV_W))],
        out_specs=[seq, seq],
        out_shape=[jax.ShapeDtypeStruct((NB, S, CONV_W), BF16), jax.ShapeDtypeStruct((NB, S, CONV_W), F32)],
        scratch_shapes=[pltpu.VMEM((S + 2 * CONV_HALO, CONV_W), F32), pltpu.VMEM((8, CONV_CHUNK + 24, CONV_W), F32)],
        compiler_params=_params("parallel"),
    )(uc, dw_w, vec)


def _conv_bwd(uc, z, dcact, dw_w, vec, ride=None):
    NB, S, _ = uc.shape
    n_chunks = S // CONV_CHUNK

    def body(uc_ref, z_ref, dc_ref, w_ref, vec_ref, duc_ref, dw_ref, dvec_ref, vpad, dzpad, dw8, dvec8, copies):
        @pl.when(pl.program_id(0) == 0)
        def _():
            dw8[...] = jnp.zeros_like(dw8)
            dvec8[...] = jnp.zeros_like(dvec8)
        _glu_into(uc_ref, vpad, S)
        dzpad[pl.ds(S, 2 * CONV_HALO), :] = jnp.zeros((2 * CONV_HALO, CONV_W), F32)

        def norm_bwd(i, carry):
            r0 = pl.multiple_of(i * CONV_CHUNK, CONV_CHUNK)
            xn, rstd, yln = _layer_norm(z_ref[0, pl.ds(r0, CONV_CHUNK), :], vec_ref)
            sg = _sigmoid(yln)
            dyln = dc_ref[0, pl.ds(r0, CONV_CHUNK), :] * (sg * (1.0 + yln * (1.0 - sg)))
            dxn = dyln * vec_ref[1:2, :]
            dz = rstd * (dxn - jnp.mean(dxn, axis=-1, keepdims=True) - xn * jnp.mean(dxn * xn, axis=-1, keepdims=True))
            dzpad[pl.ds(r0, CONV_CHUNK), :] = dz
            dvec8[0] += _fold8(dz)
            dvec8[1] += _fold8(dyln * xn)
            dvec8[2] += _fold8(dyln)
            return carry
        lax.fori_loop(0, n_chunks, norm_bwd, 0, unroll=8)

        def taps_bwd(i, carry):
            r0 = pl.multiple_of(i * CONV_CHUNK, CONV_CHUNK)
            dzwin = dzpad[pl.ds(r0, CONV_WIN), :]
            dv = _conv_taps(_shifted(dzwin, 0, copies), w_ref, lambda a, k: CONV_K - 1 - 8 * a - k)
            dz = dzwin[0:CONV_CHUNK]
            vwin = vpad[pl.ds(r0, CONV_WIN), :]
            for j, rows in _tap_slices(_shifted(vwin, CONV_HALO - (CONV_K - 1), copies), lambda a, k: 8 * a + k):
                dw8[j] += _fold8(dz * rows)
            a = uc_ref[0, pl.ds(r0, CONV_CHUNK), 0:CONV_W]
            sg = _sigmoid(uc_ref[0, pl.ds(r0, CONV_CHUNK), CONV_W:2 * CONV_W])
            duc_ref[0, pl.ds(r0, CONV_CHUNK), 0:CONV_W] = (dv * sg).astype(BF16)
            duc_ref[0, pl.ds(r0, CONV_CHUNK), CONV_W:2 * CONV_W] = (dv * a * sg * (1.0 - sg)).astype(BF16)
            return carry
        lax.fori_loop(0, n_chunks, taps_bwd, 0, unroll=2)

        @pl.when(pl.program_id(0) == NB - 1)
        def _():
            dw_ref[...] = jnp.zeros_like(dw_ref)
            dvec_ref[...] = jnp.zeros_like(dvec_ref)
            for j in range(CONV_K):
                dw_ref[j:j + 1, :] = jnp.sum(dw8[j], axis=0, keepdims=True)
            for j in range(3):
                dvec_ref[j:j + 1, :] = jnp.sum(dvec8[j], axis=0, keepdims=True)

    return _pallas(
        body, "conv_bwd", (NB,),
        [pl.BlockSpec((1, S, 2 * CONV_W), lambda b: (b, 0, 0)), pl.BlockSpec((1, S, CONV_W), lambda b: (b, 0, 0)),
         pl.BlockSpec((1, S, CONV_W), lambda b: (b, 0, 0)), _const((CONV_K, CONV_W)), _const((8, CONV_W))],
        [pl.BlockSpec((1, S, 2 * CONV_W), lambda b: (b, 0, 0)), pl.BlockSpec((32, CONV_W), lambda b: (0, 0)),
         pl.BlockSpec((8, CONV_W), lambda b: (0, 0))],
        [jax.ShapeDtypeStruct((NB, S, 2 * CONV_W), BF16), jax.ShapeDtypeStruct((32, CONV_W), F32),
         jax.ShapeDtypeStruct((8, CONV_W), F32)],
        (uc, z, dcact, dw_w, vec),
        scratch_shapes=[pltpu.VMEM((S + 2 * CONV_HALO, CONV_W), F32), pltpu.VMEM((S + 2 * CONV_HALO, CONV_W), F32),
                        pltpu.VMEM((CONV_K, 8, CONV_W), F32), pltpu.VMEM((3, 8, CONV_W), F32),
                        pltpu.VMEM((8, CONV_CHUNK + 24, CONV_W), F32)],
        sem=("arbitrary",), ride=ride)


def _rel_index_of_column(cols):
    offset = jnp.where(cols < KWIN, cols, cols - DS_LANES)
    return jnp.clip(KPAD - offset, -(CHUNK - 1), MAX_REL) + (CHUNK - 1)


def _bias_table(rel_bias, ride=None):
    def body(rb_ref, o_ref, by_offset, first8):
        ridx = _rel_index_of_column(lax.broadcasted_iota(jnp.int32, (1, DS_LANES), 1))
        onehot = (ridx == lax.broadcasted_iota(jnp.int32, (N_REL, 1), 0)).astype(F32)
        by_offset[...] = jnp.dot(rb_ref[...], onehot, preferred_element_type=F32, precision=lax.Precision.HIGHEST)
        sub = lax.broadcasted_iota(jnp.int32, (8, 1), 0)
        kchunk = lax.broadcasted_iota(jnp.int32, (1, KWIN), 1) // CHUNK
        for head in range(ATT_HEADS):
            base = jnp.broadcast_to(by_offset[head:head + 1, :], (8, DS_LANES))
            rows = base
            for s in range(1, 8):
                rows = jnp.where(sub == s, pltpu.roll(base, s, 1), rows)
            first8[head] = rows

        def rows8(q8, carry):
            qchunk = (q8 * 8 + sub) // CHUNK
            band = (kchunk >= qchunk) & (kchunk <= qchunk + LEFT_CHUNKS)
            for head in range(ATT_HEADS):
                tile = pltpu.roll(first8[head], q8 * 8, 1)[:, 0:KWIN]
                o_ref[head, pl.ds(pl.multiple_of(q8 * 8, 8), 8), :] = jnp.where(band, tile, MASK_VALUE)
            return carry
        lax.fori_loop(0, QB // 8, rows8, 0)

    outs, rode = _pallas(
        body, "bias_table", (1,), [pl.BlockSpec((ATT_HEADS, N_REL), lambda i: (0, 0))],
        [pl.BlockSpec((ATT_HEADS, QB, KWIN), lambda i: (0, 0, 0))], [jax.ShapeDtypeStruct((ATT_HEADS, QB, KWIN), F32)], (rel_bias,),
        scratch_shapes=[pltpu.VMEM((ATT_HEADS, DS_LANES), F32), pltpu.VMEM((ATT_HEADS, 8, DS_LANES), F32)],
        sem=("arbitrary",), ride=ride)
    return outs[0], rode


def _load_keys(i, k_ref, v_ref, kpad, vpad, S):
    @pl.when(i == 0)
    def _():
        kpad[pl.ds(0, KPAD), :] = jnp.zeros((KPAD, ATT_W), BF16)
        vpad[pl.ds(0, KPAD), :] = jnp.zeros((KPAD, ATT_W), BF16)
        kpad[pl.ds(KPAD, S), :] = k_ref[0]
        vpad[pl.ds(KPAD, S), :] = v_ref[0]


def _att_scores(q2s, k2, tab_ref, head, in_head, in_seq):
    qm = jnp.where(in_head, q2s, jnp.zeros_like(q2s))
    s = _dot_nt(qm, k2) + tab_ref[head]
    return s if in_seq is None else jnp.where(in_seq, s, MASK_VALUE)


def _by_window(i, step):
    in_seq = (lax.broadcasted_iota(jnp.int32, (1, KWIN), 1) + i * QB) >= KPAD
    pl.when(i < KPAD // QB)(lambda: step(in_seq))
    pl.when(i >= KPAD // QB)(lambda: step(None))


def _scaled(q2):
    return q2 * jnp.asarray(ATT_HD ** -0.5, q2.dtype)


def _att_fwd(qkv, tab, ride=None):
    NB, S, _ = qkv.shape

    def body(q_ref, k_ref, v_ref, tab_ref, o_ref, lse_ref, kpad, vpad):
        i = pl.program_id(1)
        _load_keys(i, k_ref, v_ref, kpad, vpad, S)
        koff = pl.multiple_of(i * QB, QB)
        lane = lax.broadcasted_iota(jnp.int32, (1, 128), 1)

        def step(in_seq):
            lse = jnp.zeros((QB, 128), F32)
            for pair in range(ATT_HEADS // 2):
                cols = slice(pair * 128, (pair + 1) * 128)
                q2s = _scaled(q_ref[0, :, cols])
                k2 = kpad[pl.ds(koff, KWIN), cols]
                v2 = vpad[pl.ds(koff, KWIN), cols]
                o2 = jnp.zeros((QB, 128), F32)
                for hh in range(2):
                    head = 2 * pair + hh
                    in_head = (lane // ATT_HD) == hh
                    s = _att_scores(q2s, k2, tab_ref, head, in_head, in_seq)
                    m = jnp.max(s, axis=-1, keepdims=True)
                    e = jnp.exp(s - m)
                    l = jnp.sum(e, axis=-1, keepdims=True)
                    o2 = jnp.where(in_head, _dot(e.astype(BF16), v2) * (1.0 / l), o2)
                    lse = jnp.where(lane == head, m + jnp.log(l), lse)
                o_ref[0, :, cols] = o2.astype(BF16)
            lse_ref[0] = lse
        _by_window(i, step)

    seq = lambda col: pl.BlockSpec((1, S, ATT_W), lambda b, i: (b, 0, col), pipeline_mode=pl.Buffered(1))
    outs, rode = _pallas(
        body, "att_fwd", (NB, S // QB),
        [pl.BlockSpec((1, QB, ATT_W), lambda b, i: (b, i, 0)), seq(1), seq(2), _const((ATT_HEADS, QB, KWIN))],
        [pl.BlockSpec((1, QB, ATT_W), lambda b, i: (b, i, 0)), pl.BlockSpec((1, QB, 128), lambda b, i: (b, i, 0))],
        [jax.ShapeDtypeStruct((NB, S, ATT_W), BF16), jax.ShapeDtypeStruct((NB, S, 128), F32)],
        (qkv, qkv, qkv, tab),
        scratch_shapes=[pltpu.VMEM((S + KPAD, ATT_W), BF16), pltpu.VMEM((S + KPAD, ATT_W), BF16)],
        sem=("arbitrary", "arbitrary"), ride=ride)
    return outs[0], outs[1], rode


def _att_bwd(qkv, o, lse, do, tab, ride=None):
    NB, S, _ = qkv.shape
    nq = S // QB

    def body(q_ref, k_ref, v_ref, o_ref, lse_ref, do_ref, tab_ref, dqkv_ref, ds_hbm, kpad, vpad, dkpad, dvpad, ds_acc, ds_sem):
        b, i = pl.program_id(0), pl.program_id(1)
        _load_keys(i, k_ref, v_ref, kpad, vpad, S)

        @pl.when(i == 0)
        def _():
            dkpad[...] = jnp.zeros_like(dkpad)
            dvpad[...] = jnp.zeros_like(dvpad)

        @pl.when((i == 0) & (b == 0))
        def _():
            ds_acc[...] = jnp.zeros_like(ds_acc)

        koff = pl.multiple_of(i * QB, QB)
        lane = lax.broadcasted_iota(jnp.int32, (1, 128), 1)

        def step(in_seq):
            for pair in range(ATT_HEADS // 2):
                cols = slice(pair * 128, (pair + 1) * 128)
                q2s = _scaled(q_ref[0, :, cols])
                do2 = do_ref[0, :, cols]
                k2 = kpad[pl.ds(koff, KWIN), cols]
                v2 = vpad[pl.ds(koff, KWIN), cols]
                do_o = do2.astype(F32) * o_ref[0, :, cols].astype(F32)
                dq2 = jnp.zeros((QB, 128), F32)
                dk2 = jnp.zeros((KWIN, 128), F32)
                dv2 = jnp.zeros((KWIN, 128), F32)
                for hh in range(2):
                    head = 2 * pair + hh
                    in_head = (lane // ATT_HD) == hh
                    p = jnp.exp(_att_scores(q2s, k2, tab_ref, head, in_head, in_seq) - lse_ref[0, :, head:head + 1])
                    row_term = jnp.sum(jnp.where(in_head, do_o, 0.0), axis=-1, keepdims=True)
                    dom = jnp.where(in_head, do2, jnp.zeros_like(do2))
                    ds = p * (_dot_nt(dom, v2) - row_term)
                    ds_acc[head] += ds
                    dsb = ds.astype(BF16)
                    dq2 = jnp.where(in_head, _dot(dsb, k2), dq2)
                    dk2 = jnp.where(in_head, _dot_tn(dsb, q2s), dk2)
                    dv2 = jnp.where(in_head, _dot_tn(p.astype(BF16), do2), dv2)
                dqkv_ref[0, pl.ds(koff, QB), cols] = (dq2 * (ATT_HD ** -0.5)).astype(BF16)
                dkpad[pl.ds(koff, KWIN), cols] += dk2
                dvpad[pl.ds(koff, KWIN), cols] += dv2
        _by_window(i, step)

        @pl.when(i == nq - 1)
        def _():
            dqkv_ref[0, :, ATT_W:2 * ATT_W] = dkpad[pl.ds(KPAD, S), :].astype(BF16)
            dqkv_ref[0, :, 2 * ATT_W:3 * ATT_W] = dvpad[pl.ds(KPAD, S), :].astype(BF16)

        @pl.when((i == nq - 1) & (b == NB - 1))
        def _():
            out = pltpu.make_async_copy(ds_acc, ds_hbm, ds_sem)
            out.start()
            out.wait()

    seq = lambda col: pl.BlockSpec((1, S, ATT_W), lambda b, i: (b, 0, col), pipeline_mode=pl.Buffered(1))
    rows = pl.BlockSpec((1, QB, ATT_W), lambda b, i: (b, i, 0))
    return _pallas(
        body, "att_bwd", (NB, nq),
        [rows, seq(1), seq(2), rows, pl.BlockSpec((1, QB, 128), lambda b, i: (b, i, 0)), rows, _const((ATT_HEADS, QB, KWIN))],
        [pl.BlockSpec((1, S, 3 * ATT_W), lambda b, i: (b, 0, 0)), ANY],
        [jax.ShapeDtypeStruct((NB, S, 3 * ATT_W), BF16), jax.ShapeDtypeStruct((ATT_HEADS, QB, KWIN), F32)],
        (qkv, qkv, qkv, o, lse, do, tab),
        scratch_shapes=[pltpu.VMEM((S + KPAD, ATT_W), BF16), pltpu.VMEM((S + KPAD, ATT_W), BF16),
                        pltpu.VMEM((S + KPAD, ATT_W), F32), pltpu.VMEM((S + KPAD, ATT_W), F32),
                        pltpu.VMEM((ATT_HEADS, QB, KWIN), F32), pltpu.SemaphoreType.DMA],
        sem=("arbitrary", "arbitrary"), ride=ride)


def _rel_bias_grad(ds):
    def body(ds_ref, o_ref):
        sub = lax.broadcasted_iota(jnp.int32, (8, 1), 0)
        ridx = _rel_index_of_column(lax.broadcasted_iota(jnp.int32, (DS_LANES, 1), 0))
        onehot = (ridx == lax.broadcasted_iota(jnp.int32, (1, N_REL), 1)).astype(F32)
        def rows8(q8, accs):
            shift = lax.rem(DS_LANES - q8 * 8, DS_LANES)
            out = []
            for head in range(ATT_HEADS):
                tile = ds_ref[head, pl.ds(pl.multiple_of(q8 * 8, 8), 8), :]
                tile = jnp.concatenate([tile, jnp.zeros((8, DS_LANES - KWIN), F32)], axis=1)
                out.append(accs[head] + pltpu.roll(tile, shift, 1))
            return tuple(out)
        accs = lax.fori_loop(0, QB // 8, rows8, tuple(jnp.zeros((8, DS_LANES), F32) for _ in range(ATT_HEADS)))
        for head in range(ATT_HEADS):
            acc = accs[head]
            diag = jnp.zeros((8, DS_LANES), F32)
            for s in range(8):
                shifted = acc if s == 0 else pltpu.roll(acc, DS_LANES - s, 1)
                diag = jnp.where(sub == s, shifted, diag)
            z = jnp.sum(diag, axis=0, keepdims=True)
            o_ref[head:head + 1, :] = jnp.dot(z, onehot, preferred_element_type=F32, precision=lax.Precision.HIGHEST)

    return pl.pallas_call(body, out_shape=jax.ShapeDtypeStruct((ATT_HEADS, N_REL), F32), name="rel_bias_grad",
                          compiler_params=_params())(ds)


def _memkv_fwd(mem, g, w_kv, tm):
    R = mem.shape[0]
    tm = min(tm, R)

    def body(m_ref, g_ref, w_ref, h_ref, kv_ref):
        xhat, _ = _rms_stats(m_ref[...])
        h = (xhat * g_ref[...]).astype(BF16)
        h_ref[...] = h
        kv_ref[...] = _dot(h, w_ref[...]).astype(BF16)

    row = pl.BlockSpec((tm, D), lambda t: (t, 0))
    return pl.pallas_call(
        body, grid=(R // tm,), name="memkv_fwd", in_specs=[row, _const((1, D)), _const((D, 2 * MEM_W))], out_specs=[row, row],
        out_shape=[jax.ShapeDtypeStruct((R, D), BF16), jax.ShapeDtypeStruct((R, 2 * MEM_W), BF16)],
        compiler_params=_params("parallel"),
    )(mem, g, w_kv)


def _memkv_bwd(mem, dkv, w_kv, tm):
    R = mem.shape[0]
    tm = min(tm, R)

    def body(m_ref, dkv_ref, w_ref, dg_ref):
        xhat, _ = _rms_stats(m_ref[...])
        dh = _dot_nt(dkv_ref[...].astype(BF16), w_ref[...])

        @pl.when(pl.program_id(0) == 0)
        def _():
            dg_ref[...] = jnp.zeros_like(dg_ref)
        dg_ref[...] += jnp.sum(dh * xhat, axis=0, keepdims=True)

    row = pl.BlockSpec((tm, D), lambda t: (t, 0))
    return pl.pallas_call(
        body, grid=(R // tm,), name="memkv_bwd", in_specs=[row, row, _const((D, 2 * MEM_W))],
        out_specs=pl.BlockSpec((1, D), lambda t: (0, 0)), out_shape=jax.ShapeDtypeStruct((1, D), F32),
        compiler_params=_params("arbitrary"),
    )(mem, dkv, w_kv)


def _mem_exp(qh, kh):
    s = _dot_nt(qh, kh) * (MEM_HD ** -0.5)
    e = jnp.exp(s - jnp.max(s, axis=-1, keepdims=True))
    return e, jnp.sum(e, axis=-1, keepdims=True)


def _mematt_fwd(mq, kv, tq):
    NB, S, _ = mq.shape
    M = kv.shape[1]

    def body(q_ref, kv_ref, o_ref):
        for h in range(MEM_HEADS):
            cols = slice(h * MEM_HD, (h + 1) * MEM_HD)
            e, l = _mem_exp(q_ref[0, :, cols], kv_ref[0, :, cols])
            o = _dot(e.astype(BF16), kv_ref[0, :, MEM_W + h * MEM_HD:MEM_W + (h + 1) * MEM_HD]) * (1.0 / l)
            o_ref[0, :, cols] = o.astype(BF16)

    return pl.pallas_call(
        body, grid=(NB, S // tq), name="mematt_fwd",
        in_specs=[pl.BlockSpec((1, tq, MEM_W), lambda b, i: (b, i, 0)), pl.BlockSpec((1, M, 2 * MEM_W), lambda b, i: (b, 0, 0))],
        out_specs=pl.BlockSpec((1, tq, MEM_W), lambda b, i: (b, i, 0)),
        out_shape=jax.ShapeDtypeStruct((NB, S, MEM_W), BF16), compiler_params=_params("parallel", "parallel"),
    )(mq, kv)


def _mematt_bwd(mq, kv, do, tq):
    NB, S, _ = mq.shape
    M = kv.shape[1]

    def body(q_ref, kv_ref, do_ref, dq_ref, dkv_ref):
        @pl.when(pl.program_id(1) == 0)
        def _():
            dkv_ref[...] = jnp.zeros_like(dkv_ref)
        for h in range(MEM_HEADS):
            cols = slice(h * MEM_HD, (h + 1) * MEM_HD)
            vcols = slice(MEM_W + h * MEM_HD, MEM_W + (h + 1) * MEM_HD)
            qh, kh, vh, doh = q_ref[0, :, cols], kv_ref[0, :, cols], kv_ref[0, :, vcols], do_ref[0, :, cols]
            e, l = _mem_exp(qh, kh)
            p = e * (1.0 / l)
            dp = _dot_nt(doh, vh)
            ds = p * (dp - jnp.sum(p * dp, axis=-1, keepdims=True))
            dss = (ds * (MEM_HD ** -0.5)).astype(BF16)
            dq_ref[0, :, cols] = _dot(dss, kh).astype(BF16)
            dkv_ref[0, :, cols] += _dot_tn(dss, qh)
            dkv_ref[0, :, vcols] += _dot_tn(p.astype(BF16), doh)

    qspec = pl.BlockSpec((1, tq, MEM_W), lambda b, i: (b, i, 0))
    kvspec = pl.BlockSpec((1, M, 2 * MEM_W), lambda b, i: (b, 0, 0))
    return pl.pallas_call(
        body, grid=(NB, S // tq), name="mematt_bwd", in_specs=[qspec, kvspec, qspec], out_specs=[qspec, kvspec],
        out_shape=[jax.ShapeDtypeStruct((NB, S, MEM_W), BF16), jax.ShapeDtypeStruct((NB, M, 2 * MEM_W), F32)],
        compiler_params=_params("arbitrary", "arbitrary"),
    )(mq, kv, do)


def _branch(j, in_ref, w_ref, gl_ref, bg_ref):
    y = _dot(in_ref[...], w_ref[...])
    gate = _sigmoid(gl_ref[:, j * D:(j + 1) * D].astype(F32) + bg_ref[:, j * D:(j + 1) * D])
    return y, gate


def _combine_fwd(x, cact, oatt, omem, gl, bg, wpw, wo, wmo, wout, tm):
    T = x.shape[0]

    def body(x_ref, c_ref, a_ref, m_ref, gl_ref, bg_ref, wpw_ref, wo_ref, wmo_ref, wout_ref, xo_ref, y_ref):
        y = None
        for j, (in_ref, w_ref) in enumerate(((c_ref, wpw_ref), (a_ref, wo_ref), (m_ref, wmo_ref))):
            yj, gate = _branch(j, in_ref, w_ref, gl_ref, bg_ref)
            y = gate * yj if y is None else y + gate * yj
        y = y.astype(BF16)
        y_ref[...] = y
        xo_ref[...] = x_ref[...] + _dot(y, wout_ref[...])

    row = lambda w: pl.BlockSpec((tm, w), lambda t: (t, 0))
    wbr = _const((512, D))
    return pl.pallas_call(
        body, grid=(T // tm,), name="combine_fwd",
        in_specs=[row(D), row(512), row(512), row(512), row(3 * D), _const((1, 3 * D)), wbr, wbr, wbr, _const((D, D))],
        out_specs=[row(D), row(D)],
        out_shape=[jax.ShapeDtypeStruct((T, D), F32), jax.ShapeDtypeStruct((T, D), BF16)],
        compiler_params=_params("parallel"),
    )(x, cact, oatt, omem, gl, bg, wpw, wo, wmo, wout)


def _combine_bwd(dx, cact, oatt, omem, gl, bg, wpw, wo, wmo, wout, tm, ride=None):
    T = dx.shape[0]
    n_steps = T // tm

    def body(dx_hbm, c_ref, a_ref, m_ref, gl_hbm, bg_ref, wpw_ref, wo_ref, wmo_ref, wout_ref,
             dgl_ref, dc_ref, da_ref, dm_ref, dyc_ref, dya_ref, dym_ref, dbg_ref, dx_ring, gl_ring, ring_sems):
        t = pl.program_id(0)

        def fetch(s):
            rows = pl.ds(pl.multiple_of(s * tm, tm), tm)
            slot = s % RING_SLOTS
            return (pltpu.make_async_copy(dx_hbm.at[rows, :], dx_ring.at[slot], ring_sems.at[0, slot]),
                    pltpu.make_async_copy(gl_hbm.at[rows, :], gl_ring.at[slot], ring_sems.at[1, slot]))

        @pl.when(t == 0)
        def _():
            dbg_ref[...] = jnp.zeros_like(dbg_ref)
            for s in range(min(RING_SLOTS - 1, n_steps)):
                for cp in fetch(s):
                    cp.start()

        @pl.when(t + RING_SLOTS - 1 < n_steps)
        def _():
            for cp in fetch(t + RING_SLOTS - 1):
                cp.start()
        for cp in fetch(t):
            cp.wait()
        dx_ref, gl_ref = dx_ring.at[t % RING_SLOTS], gl_ring.at[t % RING_SLOTS]
        dy = _dot_nt(dx_ref[...].astype(BF16), wout_ref[...])
        branches = ((c_ref, wpw_ref, dyc_ref, dc_ref), (a_ref, wo_ref, dya_ref, da_ref), (m_ref, wmo_ref, dym_ref, dm_ref))
        for j, (in_ref, w_ref, dyb_ref, din_ref) in enumerate(branches):
            yj, gate = _branch(j, in_ref, w_ref, gl_ref, bg_ref)
            dyg = dy * gate
            dlogit = dyg * yj * (1.0 - gate)
            dgl_ref[:, j * D:(j + 1) * D] = dlogit.astype(BF16)
            dbg_ref[:, j * D:(j + 1) * D] += jnp.sum(dlogit, axis=0, keepdims=True)
            dyb = dyg.astype(BF16)
            dyb_ref[...] = dyb
            din_ref[...] = _dot_nt(dyb, w_ref[...]).astype(din_ref.dtype)

    row = lambda w: pl.BlockSpec((tm, w), lambda t: (t, 0))
    wbr = _const((512, D))
    sds = jax.ShapeDtypeStruct
    return _pallas(
        body, "combine_bwd", (n_steps,),
        [ANY, row(512), row(512), row(512), ANY, _const((1, 3 * D)), wbr, wbr, wbr, _const((D, D))],
        [row(3 * D), row(512), row(512), row(512), row(D), row(D), row(D), pl.BlockSpec((1, 3 * D), lambda t: (0, 0))],
        [sds((T, 3 * D), BF16), sds((T, 512), F32), sds((T, 512), BF16), sds((T, 512), BF16),
         sds((T, D), BF16), sds((T, D), BF16), sds((T, D), BF16), sds((1, 3 * D), F32)],
        (dx, cact, oatt, omem, gl, bg, wpw, wo, wmo, wout),
        scratch_shapes=[pltpu.VMEM((RING_SLOTS, tm, D), F32), pltpu.VMEM((RING_SLOTS, tm, 3 * D), BF16),
                        pltpu.SemaphoreType.DMA((2, RING_SLOTS))],
        sem=("arbitrary",), ride=ride)


def _peer(x, y, c, rel):
    rx, ry, rc = (rel >> 2) & 1, (rel >> 1) & 1, rel & 1
    return ((1 - x) if rx else x, (1 - y) if ry else y, (1 - c) if rc else c)


def _all_sum_small(parts):
    n = len(parts)

    def body(*refs):
        p_refs, o_refs, slots = refs[:n], refs[n:2 * n], refs[2 * n:3 * n]
        send_sems, recv_sems = refs[3 * n:]
        x, y, c = _my_coords()
        me = _dev_index(x, y, c)

        def copy(i, rel, arrival):
            peer = _peer(x, y, c, rel)
            return pltpu.make_async_remote_copy(
                src_ref=p_refs[i], dst_ref=slots[i].at[_dev_index(*peer) if arrival else me],
                send_sem=send_sems.at[i, rel - 1], recv_sem=recv_sems.at[i, rel - 1], device_id=peer, device_id_type=MESH)

        for i in range(n):
            slots[i][me] = p_refs[i][...]
        for rel in range(1, NDEV):
            for i in range(n):
                copy(i, rel, False).start()
        for rel in range(1, NDEV):
            for i in range(n):
                copy(i, rel, True).wait_recv()
        for rel in range(1, NDEV):
            for i in range(n):
                copy(i, rel, False).wait_send()
        for i in range(n):
            total = slots[i][0]
            for d in range(1, NDEV):
                total = total + slots[i][d]
            o_refs[i][...] = total

    vmem = pl.BlockSpec(memory_space=pltpu.VMEM)
    return pl.pallas_call(
        body, out_shape=[jax.ShapeDtypeStruct(p.shape, F32) for p in parts], name="all_sum_small",
        in_specs=[vmem] * n, out_specs=[vmem] * n,
        scratch_shapes=[pltpu.VMEM((NDEV,) + p.shape, F32) for p in parts]
        + [pltpu.SemaphoreType.DMA((n, NDEV - 1)), pltpu.SemaphoreType.DMA((n, NDEV - 1))],
        compiler_params=pltpu.CompilerParams(has_side_effects=True),
    )(*parts)


HBM = pl.BlockSpec(memory_space=pltpu.HBM)
SEM = pl.BlockSpec(memory_space=pltpu.SEMAPHORE)


def _own_block(g, kind, m, tag):
    def body(g_ref, land_ref, staged, sem):
        me = _dev_index(*_my_coords())
        for cp in (pltpu.make_async_copy(_window(g_ref, kind, m, me), staged, sem),
                   pltpu.make_async_copy(staged, land_ref.at[me], sem)):
            cp.start()
            cp.wait()

    block = (m, g.shape[1]) if kind == 'row' else (g.shape[0], m)
    return pl.pallas_call(body, in_specs=[ANY], out_specs=ANY, out_shape=jax.ShapeDtypeStruct((NDEV,) + block, g.dtype),
                          scratch_shapes=[pltpu.VMEM(block, g.dtype), pltpu.SemaphoreType.DMA], name="own_block_" + tag)(g)


def _scatter_start(g, land, kind, m, tag):
    def body(g_ref, land_ref, send_sems, recv_sems, g_thru, land_thru, token):
        x, y, c = _my_coords()
        me = _dev_index(x, y, c)
        for rel in range(1, NDEV):
            peer = _peer(x, y, c, rel)
            pltpu.make_async_remote_copy(src_ref=_window(g_ref, kind, m, _dev_index(*peer)), dst_ref=land_ref.at[me],
                                         send_sem=send_sems.at[rel - 1], recv_sem=recv_sems.at[rel - 1],
                                         device_id=peer, device_id_type=MESH).start()
        token[...] = jnp.zeros_like(token)

    return pl.pallas_call(
        body, name="scatter_start_" + tag,
        out_shape=(pltpu.SemaphoreType.DMA((NDEV - 1,)), pltpu.SemaphoreType.DMA((NDEV - 1,)), pltpu.HBM(g.shape, g.dtype),
                   pltpu.HBM(land.shape, land.dtype), jax.ShapeDtypeStruct((8, 128), F32)),
        in_specs=(HBM, HBM), out_specs=(SEM, SEM, HBM, HBM, pl.BlockSpec(memory_space=pltpu.VMEM)),
        input_output_aliases={0: 2, 1: 3},
        compiler_params=pltpu.CompilerParams(has_side_effects=pltpu.SideEffectType.DATAFLOW_SIDE_EFFECTING),
    )(pltpu.with_memory_space_constraint(g, pltpu.HBM), pltpu.with_memory_space_constraint(land, pltpu.HBM))


def _scatter_wait(send_sems, recv_sems, g_thru, land_thru, after, kind, m, tag):
    n_after = len(after)

    def body(*refs):
        g_ref, land_ref, send_sems, recv_sems = refs[:4]
        x, y, c = _my_coords()
        me = _dev_index(x, y, c)
        for rel in range(1, NDEV):
            peer = _peer(x, y, c, rel)
            dev = _dev_index(*peer)
            cp = pltpu.make_async_remote_copy(src_ref=_window(g_ref, kind, m, me), dst_ref=land_ref.at[dev],
                                              send_sem=send_sems.at[rel - 1], recv_sem=recv_sems.at[rel - 1],
                                              device_id=peer, device_id_type=MESH)
            cp.wait_send()
            cp.wait_recv()

    return pl.pallas_call(
        body, name="scatter_wait_" + tag,
        out_shape=(pltpu.HBM(g_thru.shape, g_thru.dtype), pltpu.HBM(land_thru.shape, land_thru.dtype)),
        in_specs=(HBM, HBM, SEM, SEM) + (ANY,) * n_after, out_specs=(HBM, HBM), input_output_aliases={0: 0, 1: 1},
        compiler_params=pltpu.CompilerParams(has_side_effects=pltpu.SideEffectType.DATAFLOW_SIDE_EFFECTING),
    )(g_thru, land_thru, send_sems, recv_sems, *after)[1]


def _adamw_math(w, g, m, v):
    m = ADAM_B1 * m + (1.0 - ADAM_B1) * g
    v = ADAM_B2 * v + (1.0 - ADAM_B2) * (g * g)
    m_hat = m / (1.0 - ADAM_B1 ** ADAM_STEP)
    v_hat = v / (1.0 - ADAM_B2 ** ADAM_STEP)
    delta = -ADAM_LR * (m_hat / (jnp.sqrt(v_hat) + ADAM_EPS) + ADAM_WD * w)
    return delta, m, v


def _sum_adamw(parts, w, m, v, name, after=None):
    R, C = w.shape
    n_parts = len(parts)
    cg = C // n_parts
    tr = max(t for t in range(8, 257, 8) if R % t == 0)
    deps = [] if after is None else [after]

    def body(*refs):
        p_refs = refs[:n_parts]
        w_ref, m_ref, v_ref = refs[n_parts:n_parts + 3]
        g_ref, d_ref, mo_ref, vo_ref = refs[n_parts + 3 + len(deps):]
        for k, p_ref in enumerate(p_refs):
            @pl.when(pl.program_id(0) == k)
            def _():
                g = p_ref[0].astype(F32)
                for d in range(1, NDEV):
                    g = g + p_ref[d].astype(F32)
                g_ref[...] = g
                d_ref[...], mo_ref[...], vo_ref[...] = _adamw_math(w_ref[...], g, m_ref[...], v_ref[...])

    part = pl.BlockSpec((NDEV, tr, cg), lambda k, t: (0, t, 0))
    blk = pl.BlockSpec((tr, cg), lambda k, t: (t, k))
    return pl.pallas_call(
        body, grid=(n_parts, R // tr), name=name, in_specs=[part] * n_parts + [blk, blk, blk] + [ANY] * len(deps),
        out_specs=[blk] * 4, out_shape=[jax.ShapeDtypeStruct((R, C), F32)] * 4, compiler_params=_params("parallel", "parallel"),
    )(*parts, w, m, v, *deps)


def _adamw_small(ws, gs, ms, vs):
    n = len(ws)

    def body(*refs):
        w_refs, g_refs, m_refs, v_refs = (refs[k * n:(k + 1) * n] for k in range(4))
        d_refs, mo_refs, vo_refs = (refs[(4 + k) * n:(5 + k) * n] for k in range(3))
        for i in range(n):
            d_refs[i][...], mo_refs[i][...], vo_refs[i][...] = _adamw_math(w_refs[i][...], g_refs[i][...], m_refs[i][...], v_refs[i][...])

    shapes = [jax.ShapeDtypeStruct(a.shape, F32) for a in ws]
    outs = pl.pallas_call(body, out_shape=shapes * 3, name="adamw_small", compiler_params=_params())(*ws, *gs, *ms, *vs)
    return outs[:n], outs[n:2 * n], outs[2 * n:]


def kernel(x, mem, ffn1_norm, ffn1_w_up, ffn1_w_down, mix_norm, mem_norm, w_in, b_gate, conv_dw_w, conv_dw_b, conv_ln_g, conv_ln_b, conv_w_pw, att_rel_bias, att_w_o, mem_w_kv, mem_w_o, w_out, ffn2_norm, ffn2_w_up, ffn2_w_down, final_norm, loss_target, m_ffn1_norm, m_ffn1_w_up, m_ffn1_w_down, m_mix_norm, m_mem_norm, m_w_in, m_b_gate, m_conv_dw_w, m_conv_dw_b, m_conv_ln_g, m_conv_ln_b, m_conv_w_pw, m_att_rel_bias, m_att_w_o, m_mem_w_kv, m_mem_w_o, m_w_out, m_ffn2_norm, m_ffn2_w_up, m_ffn2_w_down, m_final_norm, v_ffn1_norm, v_ffn1_w_up, v_ffn1_w_down, v_mix_norm, v_mem_norm, v_w_in, v_b_gate, v_conv_dw_w, v_conv_dw_b, v_conv_ln_g, v_conv_ln_b, v_conv_w_pw, v_att_rel_bias, v_att_w_o, v_mem_w_kv, v_mem_w_o, v_w_out, v_ffn2_norm, v_ffn2_w_up, v_ffn2_w_down, v_final_norm):
    given = dict(locals())
    w = {n: given[n] for n in WEIGHTS}
    mom = {n: given["m_" + n] for n in WEIGHTS}
    var = {n: given["v_" + n] for n in WEIGHTS}

    NB, S, _ = x.shape
    T = NB * S
    ML = mem.shape[1]
    x0 = x.reshape(T, D)
    target = loss_target.reshape(T, D)
    mem2 = mem.reshape(NB * ML, D)

    def block(t, n):
        return jnp.transpose(t[0]) if n in TRANSPOSED else t[0]

    sh = dict(zip(BIG_ORDER, _cast_shards([block(w[n], n) for n in BIG_ORDER])))
    dw_t = jnp.transpose(conv_dw_w[0])

    def gather(names, extra=(), extra_kinds=()):
        return _gather_ride([sh[n] for n in names] + list(extra), [BIG[n] for n in names] + list(extra_kinds))

    W = {}
    names0 = ['ffn1_w_up', 'ffn1_w_down']
    tab, got = _bias_table(att_rel_bias[0], ride=gather(names0, [dw_t], [('row', dw_t.shape[0])]))
    W.update(zip(names0, got[:2]))
    dw_full = jnp.transpose(got[2])
    conv_vec = jnp.concatenate([conv_dw_b, conv_ln_g, conv_ln_b, jnp.zeros((5, CONV_W), F32)], axis=0)
    fin_g = final_norm.reshape(1, D)

    names1 = ['w_in', 'conv_w_pw', 'att_w_o', 'mem_w_kv', 'mem_w_o', 'w_out']
    (x1, ab1), got = _ffn_fwd(x0, ffn1_norm, W['ffn1_w_up'], W['ffn1_w_down'], TILE_FFN_FWD, "ffn1_fwd", ride=gather(names1))
    W.update(zip(names1, got))
    (uc, qkv, mq, gl, hmix), _ = _mix_fwd(x1, mix_norm, W['w_in'], TILE_TOKENS)
    uc3 = uc.reshape(NB, S, 2 * CONV_W)
    qkv3 = qkv.reshape(NB, S, 3 * ATT_W)
    mq3 = mq.reshape(NB, S, MEM_W)
    cact, conv_z = _conv_fwd(uc3, dw_full, conv_vec)
    cact = cact.reshape(T, CONV_W)
    names2 = ['ffn2_w_up', 'ffn2_w_down']
    oatt3, att_lse, got = _att_fwd(qkv3, tab, ride=gather(names2))
    W.update(zip(names2, got))
    oatt = oatt3.reshape(T, ATT_W)
    memh, kv = _memkv_fwd(mem2, mem_norm, W['mem_w_kv'], TILE_TOKENS)
    kv3 = kv.reshape(NB, ML, 2 * MEM_W)
    omem = _mematt_fwd(mq3, kv3, TILE_TOKENS).reshape(T, MEM_W)
    branch_w = (W['conv_w_pw'], W['att_w_o'], W['mem_w_o'], W['w_out'])
    x2, ymix = _combine_fwd(x1, cact, oatt, omem, gl, b_gate, *branch_w, TILE_TOKENS)
    dx3, ab2, loss_part, dg_final = _ffn_fwd_loss(x2, ffn2_norm, W['ffn2_w_up'], W['ffn2_w_down'], fin_g, target, TILE_FFN_FWD,
                                                  "ffn2_fwd_loss")

    def scatter(grads, names):
        return _scatter_ride(grads, [BIG[n] for n in names])

    G, P = {}, {}
    dx2, dab2, act2, h2, dg_ffn2 = _ffn_bwd(x2, dx3, ab2, ffn2_norm, W['ffn2_w_up'], W['ffn2_w_down'], TILE_FFN, "ffn2_bwd")
    g_up, _ = _tn_matmul(dab2, h2, 512, "grad_ffn2_w_up_a", tt=TILE_GRAD_TOKENS_WIDE, x_part=(0, 2), out_rows=2 * FF)
    G['ffn2_w_up'], _ = _tn_matmul(dab2, h2, 512, "grad_ffn2_w_up_b", tt=TILE_GRAD_TOKENS_WIDE, x_part=(1, 2), out_rows=2 * FF,
                                   prev=g_up)
    G['ffn2_w_down'], _ = _tn_matmul(act2, dx3, 512, "grad_ffn2_w_down", scale=0.5, tt=TILE_GRAD_TOKENS_WIDE)
    (dgl, dcact, doatt, domem, dyc, dya, dym, dbg), got = _combine_bwd(
        dx2, cact, oatt, omem, gl, b_gate, *branch_w, TILE_COMBINE, ride=scatter([G['ffn2_w_up']], ['ffn2_w_up']))
    P['ffn2_w_up'] = got
    G['w_out'], _ = _tn_matmul(ymix, dx2, D, "grad_w_out", tt=TILE_GRAD_TOKENS_WIDE)
    G['conv_w_pw'], _ = _tn_matmul(cact, dyc, D, "grad_conv_w_pw")
    G['att_w_o'], _ = _tn_matmul(oatt, dya, D, "grad_att_w_o")
    G['mem_w_o'], _ = _tn_matmul(omem, dym, D, "grad_mem_w_o")
    dmq3, dkv3 = _mematt_bwd(mq3, kv3, domem.reshape(NB, S, MEM_W), TILE_TOKENS)
    dkv = dkv3.reshape(NB * ML, 2 * MEM_W)
    dg_mem = _memkv_bwd(mem2, dkv, W['mem_w_kv'], TILE_TOKENS)
    G['mem_w_kv'], _ = _tn_matmul(memh, dkv, 512, "grad_mem_w_kv")
    names = ['ffn2_w_down', 'w_out', 'conv_w_pw', 'att_w_o', 'mem_w_o']
    (dqkv3, dscore), got = _att_bwd(qkv3, oatt3, att_lse, doatt.reshape(NB, S, ATT_W), tab,
                                    ride=scatter([G[n] for n in names], names))
    P.update((n, [p]) for n, p in zip(names, got))
    d_rel = _rel_bias_grad(dscore)
    (duc3, d_dw, d_cvec), got = _conv_bwd(uc3, conv_z, dcact.reshape(NB, S, CONV_W), dw_full, conv_vec,
                                          ride=scatter([G['mem_w_kv']], ['mem_w_kv']))
    P['mem_w_kv'] = got
    duc, dqkv, dmq = duc3.reshape(T, 2 * CONV_W), dqkv3.reshape(T, 3 * ATT_W), dmq3.reshape(T, MEM_W)
    g_in, _ = _tn_matmul(hmix, duc, 1024, "grad_w_in_conv", out_cols=IN_COLS, col_off=0)
    g_in, _ = _tn_matmul(hmix, dqkv, 512, "grad_w_in_qkv", out_cols=IN_COLS, col_off=1024, prev=g_in)
    g_in, _ = _tn_matmul(hmix, dmq, 512, "grad_w_in_mq", out_cols=IN_COLS, col_off=2560, prev=g_in)
    G['w_in'], _ = _tn_matmul(hmix, dgl, 1024, "grad_w_in_gate", out_cols=IN_COLS, col_off=3072, prev=g_in)
    def start_scatter(g, name, tag):
        kind = BIG[name]
        return _scatter_start(g, _own_block(g, *kind, tag), *kind, tag) + (kind, tag)

    def wait_scatter(started, after):
        send_sems, recv_sems, g_thru, land_thru, _, kind, tag = started
        return _scatter_wait(send_sems, recv_sems, g_thru, land_thru, after, *kind, tag)

    ex_in = start_scatter(G['w_in'], 'w_in', "w_in")
    (dx1, dg_mix), _ = _mix_bwd(x1, dx2, duc, dqkv, dmq, dgl, mix_norm, W['w_in'], TILE_TOKENS, after=ex_in[4])
    dx0, dab1, act1, h1, dg_ffn1 = _ffn_bwd(x0, dx1, ab1, ffn1_norm, W['ffn1_w_up'], W['ffn1_w_down'], TILE_FFN, "ffn1_bwd")
    g_wd1, _ = _tn_matmul(act1, dx1, 512, "grad_ffn1_w_down", scale=0.5, tt=TILE_GRAD_TOKENS_WIDE)
    ex_wd = start_scatter(g_wd1, 'ffn1_w_down', "ffn1_w_down")
    g_wu1a, _ = _tn_matmul(dab1, h1, 512, "grad_ffn1_w_up_a", tt=TILE_GRAD_TOKENS_WIDEST, y_part=(0, 2), after=ex_wd[4])
    ex_a = start_scatter(g_wu1a, 'ffn1_w_up', "ffn1_w_up_a")
    g_wu1b, _ = _tn_matmul(dab1, h1, 512, "grad_ffn1_w_up_b", tt=TILE_GRAD_TOKENS_WIDEST, y_part=(1, 2), after=ex_a[4])
    ex_b = start_scatter(g_wu1b, 'ffn1_w_up', "ffn1_w_up_b")
    token = ex_b[4]

    small_names = ['loss', 'ffn1_norm', 'mix_norm', 'mem_norm', 'b_gate', 'conv_dw_w', 'conv_vec', 'att_rel_bias', 'ffn2_norm',
                   'final_norm']
    small = dict(zip(small_names, _all_sum_small(
        [loss_part + token[0:1], dg_ffn1, dg_mix, dg_mem, dbg, d_dw, d_cvec, d_rel, dg_ffn2, dg_final])))
    loss = small['loss'][0, 0]
    me = _dev_index(*_my_coords())
    for i, n in enumerate(['conv_dw_b', 'conv_ln_g', 'conv_ln_b']):
        small[n] = small['conv_vec'][i:i + 1]
    small['conv_dw_w'] = lax.dynamic_slice(small['conv_dw_w'], (0, me * conv_dw_w.shape[2]), (CONV_K, conv_dw_w.shape[2]))
    little = [n for n in WEIGHTS if n not in BIG]
    as2d = lambda t, n: t.reshape(small[n].shape)
    d_s, m_s, v_s = _adamw_small([as2d(w[n], n) for n in little], [small[n] for n in little],
                                 [as2d(mom[n], n) for n in little], [as2d(var[n], n) for n in little])
    grad, delta, new_m, new_v = {}, {}, {}, {}
    for i, n in enumerate(little):
        grad[n], delta[n], new_m[n], new_v[n] = (t.reshape(w[n].shape) for t in (small[n], d_s[i], m_s[i], v_s[i]))
    done = [d_s[0]]
    waited = {'w_in': [ex_in], 'ffn1_w_down': [ex_wd], 'ffn1_w_up': [ex_a, ex_b]}
    order = [n for n in BIG_ORDER if n not in waited] + list(waited)
    for n in order:
        if n in waited:
            P[n] = [wait_scatter(ex, done) for ex in waited[n]]
        outs = _sum_adamw(P[n], block(w[n], n), block(mom[n], n), block(var[n], n), "adamw_" + n,
                          after=None if n in waited else token)
        done.append(outs[0])
        grad[n], delta[n], new_m[n], new_v[n] = ((jnp.transpose(t) if n in TRANSPOSED else t)[None] for t in outs)

    return (loss, dx0.reshape(NB, S, D), *[grad[n] for n in WEIGHTS], *[delta[n] for n in WEIGHTS],
            *[new_m[n] for n in WEIGHTS], *[new_v[n] for n in WEIGHTS])
```
